```python
import math
import jax
import jax.numpy as jnp
from jax import lax
import numpy as np

D_MODEL = 1024
BATCH = 1
SEQ = 16384
DEPTH = 4

GRID_W = 64
CTX_LEN = 256
HEAD_DIM = 64
BRANCH_W = D_MODEL // 2
MIX_W = 2 * BRANCH_W
NA_HEADS = BRANCH_W // HEAD_DIM
NA_WIN_ROWS = 8
NA_WIN_COLS = 16
GQA_HEADS = BRANCH_W // HEAD_DIM
GQA_KV_HEADS = 2
GQA_GROUP = GQA_HEADS // GQA_KV_HEADS
GQA_KV_W = GQA_KV_HEADS * HEAD_DIM
ROPE_THETA = 10000.0
ROPE_FREQS = HEAD_DIM // 4
Q_BLOCK = 128
RWKV_HEADS = BRANCH_W // HEAD_DIM
DECAY_LORA = 32
ICLR_LORA = 32
HY_WIDTH = BRANCH_W
HY_ORDER = 2
HY_POS_BANDS = 16
HY_POS_DIM = 1 + 2 * HY_POS_BANDS
HY_FILTER_HIDDEN = 64
HY_DECAY_TARGET = 1e-2
HY_FAST_DECAY_PCT = 0.3
HY_SLOW_DECAY_PCT = 1.5
NORM_EPS = 1e-6
RWKV_GN_EPS = 64e-5
N_ATTN_LAYERS = (DEPTH + 1) // 2
N_REC_LAYERS = DEPTH // 2
ATTN_SPLITS = (BRANCH_W, BRANCH_W, BRANCH_W, BRANCH_W, BRANCH_W, GQA_KV_W, GQA_KV_W, BRANCH_W)
ATTN_IN = 6 * BRANCH_W + 2 * GQA_KV_W
RWKV_SPLITS = (BRANCH_W, BRANCH_W, BRANCH_W, DECAY_LORA, DECAY_LORA, ICLR_LORA, ICLR_LORA)
RWKV_SHIFT_W = 3 * BRANCH_W + 2 * DECAY_LORA + 2 * ICLR_LORA
HY_IN_W = (HY_ORDER + 1) * HY_WIDTH
REC_SPLITS = (RWKV_SHIFT_W, BRANCH_W, HY_IN_W, HY_WIDTH)
REC_IN = RWKV_SHIFT_W + BRANCH_W + HY_IN_W + HY_WIDTH

kernel_name = "hybrid_natten_gqa_rwkv7_hyena_dit"


def split_cols(u, sizes):
    parts, start = [], 0
    for s in sizes:
        parts.append(u[..., start:start + s])
        start += s
    return parts


def to_heads(t, n_heads):
    return t.reshape(t.shape[:-1] + (n_heads, HEAD_DIM))


def rms_norm(x, gain):
    xf = x.astype(jnp.float32)
    y = xf * lax.rsqrt(jnp.mean(xf * xf, axis=-1, keepdims=True) + NORM_EPS)
    return (y * gain.astype(jnp.float32)).astype(x.dtype)


def adaln(cond, ada_w, ada_b):
    m = jax.nn.silu(cond) @ ada_w + ada_b
    return jnp.split(m, 3, axis=-1)


def modulate(x, gain, shift, scale):
    return rms_norm(x, gain) * (1 + scale) + shift


def centred_pad(u):
    return jnp.pad(u, ((0, 0), (1, 1), (0, 0)))


def centred_conv3(u, taps):
    up = centred_pad(u)
    return up[:, :-2] * taps[0] + up[:, 1:-1] * taps[1] + up[:, 2:] * taps[2]


def centred_token_shift(u, mu):
    up = centred_pad(u)
    return u + (0.5 * (up[:, :-2] + up[:, 2:]) - u) * mu


def axial_rope_tables(n):
    t = jnp.arange(n, dtype=jnp.int32)
    pos = jnp.stack([t // GRID_W, t % GRID_W], axis=-1).astype(jnp.float32)
    inv_freq = ROPE_THETA ** (-jnp.arange(ROPE_FREQS, dtype=jnp.float32) / ROPE_FREQS)
    ang = pos[:, :, None] * inv_freq
    return jnp.cos(ang), jnp.sin(ang)


def apply_axial_rope(x, cos, sin):
    xs = x.astype(jnp.float32).reshape(x.shape[:-1] + (2, 2, ROPE_FREQS))
    x1, x2 = xs[..., 0, :], xs[..., 1, :]
    c, s = cos[None, :, None], sin[None, :, None]
    y = jnp.stack([x1 * c - x2 * s, x2 * c + x1 * s], axis=-2)
    return y.reshape(x.shape).astype(x.dtype)


def softmax_f32(s):
    return jax.nn.softmax(s.astype(jnp.float32), axis=-1)


def dense_gqa(q, k, v):
    s = jnp.einsum('bqhgd,bkhd->bhgqk', q, k) * (q.shape[-1] ** -0.5)
    p = softmax_f32(s).astype(v.dtype)
    return jnp.einsum('bhgqk,bkhd->bqhgd', p, v)


def blockwise_gqa(q, k, v, k_ctx, v_ctx):
    b, n, hk, g, dh = q.shape
    scale = dh ** -0.5

    def block(q_blk):
        s = jnp.concatenate([jnp.einsum('bqhgd,bkhd->bhgqk', q_blk, k),
                             jnp.einsum('bqhgd,bkhd->bhgqk', q_blk, k_ctx)], axis=-1) * scale
        p = softmax_f32(s).astype(v.dtype)
        return (jnp.einsum('bhgqk,bkhd->bqhgd', p[..., :n], v)
                + jnp.einsum('bhgqk,bkhd->bqhgd', p[..., n:], v_ctx))

    q_blocks = jnp.moveaxis(q.reshape(b, n // Q_BLOCK, Q_BLOCK, hk, g, dh), 1, 0)
    out = lax.map(block, q_blocks)
    return jnp.moveaxis(out, 0, 1).reshape(b, n, hk, g, dh)


def neighbourhood_layout(n, rows):
    kh, kw = min(NA_WIN_ROWS, rows), NA_WIN_COLS
    t = jnp.arange(n, dtype=jnp.int32)
    q_row, q_col = t // GRID_W, t % GRID_W
    row0 = jnp.clip(q_row - kh // 2, 0, rows - kh)
    col0 = jnp.clip(q_col - kw // 2, 0, GRID_W - kw)
    k_row = row0[:, None, None] + jnp.arange(kh, dtype=jnp.int32)[None, :, None]
    k_col = col0[:, None, None] + jnp.arange(kw, dtype=jnp.int32)[None, None, :]
    shape = (n, kh, kw)
    idx = jnp.broadcast_to(k_row * GRID_W + k_col, shape).reshape(n, kh * kw)
    rel_r = jnp.broadcast_to(k_row - q_row[:, None, None] + NA_WIN_ROWS - 1, shape).reshape(n, kh * kw)
    rel_c = jnp.broadcast_to(k_col - q_col[:, None, None] + NA_WIN_COLS - 1, shape).reshape(n, kh * kw)
    return idx, rel_r, rel_c


def neighbourhood_attention(q, k, v, k_ctx, v_ctx, rpb, rows):
    b, n, h, dh = q.shape
    idx, rel_r, rel_c = neighbourhood_layout(n, rows)
    n_win = idx.shape[-1]
    nblk = n // Q_BLOCK
    scale = dh ** -0.5

    def block(args):
        q_blk, idx_blk, rr_blk, rc_blk = args
        k_win = jnp.take(k, idx_blk, axis=1)
        v_win = jnp.take(v, idx_blk, axis=1)
        s_win = jnp.einsum('bqhd,bqkhd->bhqk', q_blk, k_win) * scale + rpb[:, rr_blk, rc_blk][None]
        s_ctx = jnp.einsum('bqhd,bkhd->bhqk', q_blk, k_ctx) * scale
        p = softmax_f32(jnp.concatenate([s_win, s_ctx], axis=-1)).astype(v.dtype)
        return (jnp.einsum('bhqk,bqkhd->bqhd', p[..., :n_win], v_win)
                + jnp.einsum('bhqk,bkhd->bqhd', p[..., n_win:], v_ctx))

    xs = (jnp.moveaxis(q.reshape(b, nblk, Q_BLOCK, h, dh), 1, 0),
          idx.reshape(nblk, Q_BLOCK, n_win),
          rel_r.reshape(nblk, Q_BLOCK, n_win),
          rel_c.reshape(nblk, Q_BLOCK, n_win))
    out = lax.map(block, xs)
    return jnp.moveaxis(out, 0, 1).reshape(b, n, h, dh)


def rwkv_features(u, w0, w_up, a0, a_up, k_k, k_a):
    r, k, v, dw_f, dw_b, da_f, da_b = split_cols(u, RWKV_SPLITS)
    kk = to_heads(k * k_k, RWKV_HEADS).astype(jnp.float32)
    kk = kk * lax.rsqrt(jnp.sum(kk * kk, axis=-1, keepdims=True) + 1e-12)
    dirs = []
    for d, (dw, da) in enumerate(((dw_f, da_f), (dw_b, da_b))):
        w_log = -jax.nn.softplus(-(w0[d] + jnp.tanh(dw) @ w_up[d])) - 0.5
        decay = jnp.exp(-jnp.exp(w_log.astype(jnp.float32)))
        a = jax.nn.sigmoid(a0[d] + da @ a_up[d])
        k_d = k * (1 + (a - 1) * k_a)
        dirs.append((to_heads(decay, RWKV_HEADS), to_heads(k_d, RWKV_HEADS), kk * to_heads(a, RWKV_HEADS)))
    return to_heads(r, RWKV_HEADS), to_heads(v, RWKV_HEADS), kk, dirs


def wkv_scan(s0, r, decay, k, v, kk, b, reverse, emit):
    xs = tuple(jnp.moveaxis(t, 1, 0) for t in (r, decay, k, v, kk, b))

    def step(s, inp):
        r_t, w_t, k_t, v_t, kk_t, b_t = inp
        sa = jnp.einsum('bhij,bhj->bhi', s, kk_t)
        s = s * w_t[:, :, None, :] - sa[..., None] * b_t[:, :, None, :] + v_t[..., None] * k_t[:, :, None, :]
        y = jnp.einsum('bhij,bhj->bhi', s, r_t) if emit else None
        return s, y

    s_final, ys = lax.scan(step, s0, xs, reverse=reverse)
    return s_final, (jnp.moveaxis(ys, 0, 1) if emit else None)


def rwkv_readout(y, r, k_bonus, v, r_k, gn_w, gn_b):
    yf = y.astype(jnp.float32)
    mean = jnp.mean(yf, axis=-1, keepdims=True)
    var = jnp.mean(jnp.square(yf - mean), axis=-1, keepdims=True)
    yn = ((yf - mean) * lax.rsqrt(var + RWKV_GN_EPS)).reshape(y.shape[:2] + (BRANCH_W,))
    yn = (yn * gn_w + gn_b).astype(v.dtype)
    bonus = jnp.sum(r * k_bonus * r_k, axis=-1, keepdims=True) * v
    return yn + bonus.reshape(yn.shape)


def rwkv_branch(u_lat, u_ctx, mu, w0, w_up, a0, a_up, k_k, k_a, r_k, gn_w, gn_b, ctx_out):
    r, v, kk, dirs = rwkv_features(centred_token_shift(u_lat, mu), w0, w_up, a0, a_up, k_k, k_a)
    rc, vc, kkc, dirs_c = rwkv_features(centred_token_shift(u_ctx, mu), w0, w_up, a0, a_up, k_k, k_a)
    s0 = jnp.zeros((u_lat.shape[0], RWKV_HEADS, HEAD_DIM, HEAD_DIM), jnp.float32)
    y_lat, y_ctx = [], []
    for d in range(2):
        reverse = d == 1
        decay_c, k_c, b_c = dirs_c[d]
        s_ctx, yc = wkv_scan(s0, rc, decay_c, k_c, vc, kkc, b_c, reverse, ctx_out)
        decay_l, k_l, b_l = dirs[d]
        _, yl = wkv_scan(s_ctx, r, decay_l, k_l, v, kk, b_l, reverse, True)
        y_lat.append(yl)
        y_ctx.append(yc)
    out_lat = rwkv_readout(y_lat[0] + y_lat[1], r, 0.5 * (dirs[0][1] + dirs[1][1]), v, r_k, gn_w, gn_b)
    out_ctx = None
    if ctx_out:
        out_ctx = rwkv_readout(y_ctx[0] + y_ctx[1], rc, 0.5 * (dirs_c[0][1] + dirs_c[1][1]), vc, r_k, gn_w, gn_b)
    return out_lat, out_ctx


def hyena_filters(L, w1, b1, w2, b2, w3, b3):
    t_idx = jnp.arange(L, dtype=jnp.float32)
    t = t_idx / max(L - 1, 1)
    bands = jnp.linspace(1e-4, HY_POS_BANDS - 1, HY_POS_BANDS, dtype=jnp.float32)
    ang = 2.0 * math.pi * bands[None, :] * t_idx[:, None] / L
    feats = jnp.concatenate([t[:, None], jnp.cos(ang), -jnp.sin(ang)], axis=-1)
    hid = jnp.sin(feats @ w1 + b1)
    hid = jnp.sin(hid @ w2 + b2)
    taps = (hid @ w3 + b3).reshape(L, 2, HY_ORDER, HY_WIDTH)
    deltas = jnp.linspace(math.log(HY_DECAY_TARGET) / HY_FAST_DECAY_PCT,
                          math.log(HY_DECAY_TARGET) / HY_SLOW_DECAY_PCT, HY_WIDTH, dtype=jnp.float32)
    window = jnp.exp(-t[:, None] * jnp.abs(deltas))
    taps = taps * window[:, None, None, :]
    fwd, bwd = taps[:, 0], taps[:, 1]
    kern = jnp.concatenate([fwd[:1] + bwd[:1], fwd[1:], jnp.zeros_like(fwd[:1]), bwd[:0:-1]], axis=0)
    kern = kern.astype(jnp.float32)
    return kern * lax.rsqrt(jnp.sum(kern * kern, axis=0, keepdims=True))


def fft_long_conv(z, kern, skip):
    L = z.shape[1]
    zf = jnp.fft.rfft(z.astype(jnp.float32), n=2 * L, axis=1)
    kf = jnp.fft.rfft(kern, n=2 * L, axis=0)
    y = jnp.fft.irfft(zf * kf[None], n=2 * L, axis=1)[:, :L]
    return (y + z * skip).astype(z.dtype)


def hyena_operator(u, short_taps, kern, skip):
    v, x1, x2 = split_cols(centred_conv3(u, short_taps), (HY_WIDTH, HY_WIDTH, HY_WIDTH))
    z = v
    for o, gate in enumerate((x1, x2)):
        z = gate * fft_long_conv(z, kern[:, o], skip[o])
    return z


def attn_layer(x, ctx, c, c_ctx, norm_g, ada_w, ada_b, w_in, rpb, q_gain, k_gain, w_out, cos, sin, rows, ctx_out):
    b, n, _ = x.shape
    lc = ctx.shape[1]
    shift, scale, gate = adaln(c, ada_w, ada_b)
    shift_c, scale_c, gate_c = adaln(c_ctx, ada_w, ada_b)
    u = modulate(x, norm_g, shift[:, None], scale[:, None]) @ w_in
    uc = modulate(ctx, norm_g, shift_c, scale_c) @ w_in
    qa, ka, va, ga, qb, kb, vb, gb = split_cols(u, ATTN_SPLITS)
    qa_c, ka_c, va_c, ga_c, qb_c, kb_c, vb_c, gb_c = split_cols(uc, ATTN_SPLITS)
    ka_c, va_c = to_heads(ka_c, NA_HEADS), to_heads(va_c, NA_HEADS)
    o_a = neighbourhood_attention(to_heads(qa, NA_HEADS), to_heads(ka, NA_HEADS), to_heads(va, NA_HEADS),
                                  ka_c, va_c, rpb, rows)
    kb_c = rms_norm(to_heads(kb_c, GQA_KV_HEADS), k_gain)
    vb_c = to_heads(vb_c, GQA_KV_HEADS)
    q_lat = apply_axial_rope(rms_norm(to_heads(qb, GQA_HEADS), q_gain), cos, sin)
    q_lat = q_lat.reshape(b, n, GQA_KV_HEADS, GQA_GROUP, HEAD_DIM)
    k_lat = apply_axial_rope(rms_norm(to_heads(kb, GQA_KV_HEADS), k_gain), cos, sin)
    o_b = blockwise_gqa(q_lat, k_lat, to_heads(vb, GQA_KV_HEADS), kb_c, vb_c)
    y = jnp.concatenate([o_a.reshape(b, n, BRANCH_W) * jax.nn.silu(ga),
                         o_b.reshape(b, n, BRANCH_W) * jax.nn.silu(gb)], axis=-1) @ w_out
    x = x + gate[:, None] * y
    if ctx_out:
        oa_c = dense_gqa(to_heads(qa_c, NA_HEADS)[:, :, :, None], ka_c, va_c)
        qb_c = rms_norm(to_heads(qb_c, GQA_HEADS), q_gain).reshape(b, lc, GQA_KV_HEADS, GQA_GROUP, HEAD_DIM)
        ob_c = dense_gqa(qb_c, kb_c, vb_c)
        yc = jnp.concatenate([oa_c.reshape(b, lc, BRANCH_W) * jax.nn.silu(ga_c),
                              ob_c.reshape(b, lc, BRANCH_W) * jax.nn.silu(gb_c)], axis=-1) @ w_out
        ctx = ctx + gate_c * yc
    return x, ctx


def rec_layer(x, ctx, c, c_ctx, norm_g, ada_w, ada_b, w_in, mu, w0, w_up, a0, a_up, k_k, k_a, r_k, gn_w, gn_b,
              hy_short, hy_w1, hy_b1, hy_w2, hy_b2, hy_w3, hy_b3, hy_skip, w_out, ctx_out):
    shift, scale, gate = adaln(c, ada_w, ada_b)
    shift_c, scale_c, gate_c = adaln(c_ctx, ada_w, ada_b)
    u = modulate(x, norm_g, shift[:, None], scale[:, None]) @ w_in
    uc = modulate(ctx, norm_g, shift_c, scale_c) @ w_in
    u_rw, g_rw, u_hy, g_hy = split_cols(u, REC_SPLITS)
    uc_rw, gc_rw, uc_hy, gc_hy = split_cols(uc, REC_SPLITS)
    y_rw, yc_rw = rwkv_branch(u_rw, uc_rw, mu, w0, w_up, a0, a_up, k_k, k_a, r_k, gn_w, gn_b, ctx_out)
    y_hy = hyena_operator(u_hy, hy_short, hyena_filters(x.shape[1], hy_w1, hy_b1, hy_w2, hy_b2, hy_w3, hy_b3), hy_skip)
    y = jnp.concatenate([y_rw * jax.nn.silu(g_rw), y_hy * jax.nn.silu(g_hy)], axis=-1) @ w_out
    x = x + gate[:, None] * y
    if ctx_out:
        kern_c = hyena_filters(ctx.shape[1], hy_w1, hy_b1, hy_w2, hy_b2, hy_w3, hy_b3)
        yc_hy = hyena_operator(uc_hy, hy_short, kern_c, hy_skip)
        yc = jnp.concatenate([yc_rw * jax.nn.silu(gc_rw), yc_hy * jax.nn.silu(gc_hy)], axis=-1) @ w_out
        ctx = ctx + gate_c * yc
    return x, ctx


def setup_inputs(seed: int = 0) -> dict:
    key = jax.random.key(seed)
    keys = iter(jax.random.split(key, 48))

    def normal(shape, scale=1.0):
        return scale * jax.random.normal(next(keys), shape, jnp.float32)

    na, nr, d = N_ATTN_LAYERS, N_REC_LAYERS, D_MODEL
    return {
        "x": normal((BATCH, SEQ, d)),
        "c": normal((BATCH, d)),
        "ctx": normal((BATCH, CTX_LEN, d)),
        "c_ctx": normal((d,)),
        "attn_norm": 1.0 + normal((na, d), 0.1),
        "attn_ada_w": normal((na, d, 3 * d), d ** -0.5),
        "attn_ada_b": normal((na, 3 * d), 0.1),
        "attn_w_in": normal((na, d, ATTN_IN), d ** -0.5),
        "na_rpb": normal((na, NA_HEADS, 2 * NA_WIN_ROWS - 1, 2 * NA_WIN_COLS - 1), 0.5),
        "gqa_q_gain": 1.0 + normal((na, HEAD_DIM), 0.1),
        "gqa_k_gain": 1.0 + normal((na, HEAD_DIM), 0.1),
        "attn_w_out": normal((na, MIX_W, d), MIX_W ** -0.5),
        "rec_norm": 1.0 + normal((nr, d), 0.1),
        "rec_ada_w": normal((nr, d, 3 * d), d ** -0.5),
        "rec_ada_b": normal((nr, 3 * d), 0.1),
        "rec_w_in": normal((nr, d, REC_IN), d ** -0.5),
        "rwkv_mu": jax.random.uniform(next(keys), (nr, RWKV_SHIFT_W), jnp.float32),
        "rwkv_w0": jnp.linspace(-6.0, -0.5, BRANCH_W, dtype=jnp.float32) + normal((nr, 2, BRANCH_W), 0.1),
        "rwkv_w_up": normal((nr, 2, DECAY_LORA, BRANCH_W), 0.1),
        "rwkv_a0": normal((nr, 2, BRANCH_W), 0.5),
        "rwkv_a_up": normal((nr, 2, ICLR_LORA, BRANCH_W), ICLR_LORA ** -0.5),
        "rwkv_k_k": 0.85 + normal((nr, BRANCH_W), 0.05),
        "rwkv_k_a": 1.0 + normal((nr, BRANCH_W), 0.05),
        "rwkv_r_k": normal((nr, RWKV_HEADS, HEAD_DIM), 0.1),
        "rwkv_gn_w": 1.0 + normal((nr, BRANCH_W), 0.1),
        "rwkv_gn_b": normal((nr, BRANCH_W), 0.1),
        "hy_short": normal((nr, 3, HY_IN_W), 3 ** -0.5),
        "hy_w1": normal((nr, HY_POS_DIM, HY_FILTER_HIDDEN), HY_POS_DIM ** -0.5),
        "hy_b1": normal((nr, HY_FILTER_HIDDEN), 0.1),
        "hy_w2": normal((nr, HY_FILTER_HIDDEN, HY_FILTER_HIDDEN), HY_FILTER_HIDDEN ** -0.5),
        "hy_b2": normal((nr, HY_FILTER_HIDDEN), 0.1),
        "hy_w3": normal((nr, HY_FILTER_HIDDEN, 2 * HY_ORDER * HY_WIDTH), HY_FILTER_HIDDEN ** -0.5),
        "hy_b3": normal((nr, 2 * HY_ORDER * HY_WIDTH), 0.1),
        "hy_skip": normal((nr, HY_ORDER, HY_WIDTH), 0.5),
        "rec_w_out": normal((nr, MIX_W, d), MIX_W ** -0.5),
        "final_norm": 1.0 + normal((d,), 0.1),
    }


def reference(x, c, ctx, c_ctx, attn_norm, attn_ada_w, attn_ada_b, attn_w_in, na_rpb, gqa_q_gain, gqa_k_gain,
              attn_w_out, rec_norm, rec_ada_w, rec_ada_b, rec_w_in, rwkv_mu, rwkv_w0, rwkv_w_up, rwkv_a0, rwkv_a_up,
              rwkv_k_k, rwkv_k_a, rwkv_r_k, rwkv_gn_w, rwkv_gn_b, hy_short, hy_w1, hy_b1, hy_w2, hy_b2, hy_w3, hy_b3,
              hy_skip, rec_w_out, final_norm):
    n = x.shape[1]
    rows = n // GRID_W
    cos, sin = axial_rope_tables(n)
    for layer in range(DEPTH):
        i = layer // 2
        ctx_out = layer < DEPTH - 1
        if layer % 2 == 0:
            x, ctx = attn_layer(x, ctx, c, c_ctx, attn_norm[i], attn_ada_w[i], attn_ada_b[i], attn_w_in[i],
                                na_rpb[i], gqa_q_gain[i], gqa_k_gain[i], attn_w_out[i], cos, sin, rows, ctx_out)
        else:
            x, ctx = rec_layer(x, ctx, c, c_ctx, rec_norm[i], rec_ada_w[i], rec_ada_b[i], rec_w_in[i],
                               rwkv_mu[i], rwkv_w0[i], rwkv_w_up[i], rwkv_a0[i], rwkv_a_up[i], rwkv_k_k[i],
                               rwkv_k_a[i], rwkv_r_k[i], rwkv_gn_w[i], rwkv_gn_b[i], hy_short[i], hy_w1[i],
                               hy_b1[i], hy_w2[i], hy_b2[i], hy_w3[i], hy_b3[i], hy_skip[i], rec_w_out[i], ctx_out)
    return rms_norm(x, final_norm)
```

```python
import functools
import math

import jax
import jax.numpy as jnp
import numpy as np
from jax import lax
from jax.experimental import pallas as pl
from jax.experimental.pallas import tpu as pltpu

F32 = jnp.float32
BF16 = jnp.bfloat16
HIGHEST = lax.Precision.HIGHEST

D_MODEL = 1024
GRID_W = 64
CTX_LEN = 256
HEAD_DIM = 64
BRANCH_W = 512
N_HEADS = 8
GQA_KV_W = 128
NA_WIN_ROWS = 8
NA_WIN_COLS = 16
ROPE_THETA = 10000.0
ROPE_FREQS = 16
NORM_EPS = 1e-6
ROW_TILE = 256
NA_GROUP_ROWS = 4
NEG_BIG = -1e30
VMEM_LIMIT = 56 * 1024 * 1024

ATTN_SPLITS = (512, 512, 512, 512, 512, 128, 128, 512)
GQA_HEAD_ORDER = (0, 4, 1, 5, 2, 6, 3, 7)


def _cparams(sem):
    return pltpu.CompilerParams(dimension_semantics=sem, vmem_limit_bytes=VMEM_LIMIT)


def _silu(v):
    return v * (1.0 / (1.0 + jnp.exp(-v)))


def _lane_half(shape):
    return (lax.broadcasted_iota(jnp.int32, shape, len(shape) - 1) // HEAD_DIM) % 2


def _dot_nt(a, b):
    return lax.dot_general(a, b, (((1,), (1,)), ((), ())), preferred_element_type=F32)


def _adaln_body(cond_ref, w_ref, b_ref, o_ref):
    s = _silu(cond_ref[...])
    o_ref[0] = jnp.dot(s, w_ref[0], precision=HIGHEST, preferred_element_type=F32) + b_ref[0]


def adaln_all(cond8, ada_w, ada_b):
    nl = ada_w.shape[0]
    d = D_MODEL
    return pl.pallas_call(
        _adaln_body,
        out_shape=jax.ShapeDtypeStruct((nl, 8, 3 * d), F32),
        grid=(nl, 3),
        in_specs=[
            pl.BlockSpec((8, d), lambda l, j: (0, 0)),
            pl.BlockSpec((1, d, d), lambda l, j: (l, 0, j)),
            pl.BlockSpec((1, 1, d), lambda l, j: (l, 0, j)),
        ],
        out_specs=pl.BlockSpec((1, 8, d), lambda l, j: (l, 0, j)),
        compiler_params=_cparams(("parallel", "parallel")),
        name="adaln",
    )(cond8, ada_w, ada_b.reshape(nl, 1, 3 * d))


def _modulated(x_ref, g_ref, scale_ref, shift_ref):
    xf = x_ref[...]
    y = xf * lax.rsqrt(jnp.mean(xf * xf, axis=-1, keepdims=True) + NORM_EPS)
    return (y * g_ref[...]) * (1.0 + scale_ref[0]) + shift_ref[0]


def _mod_specs():
    d = D_MODEL
    return [
        pl.BlockSpec((ROW_TILE, d), lambda i: (i, 0)),
        pl.BlockSpec((1, d), lambda i: (0, 0)),
        pl.BlockSpec((1, 1, d), lambda i: (jnp.minimum(i, 1), 0, 0)),
        pl.BlockSpec((1, 1, d), lambda i: (jnp.minimum(i, 1), 0, 0)),
    ]


def _attn_in_body(x_ref, g_ref, scale_ref, shift_ref, w_ref, cos_ref, sin_ref, gq_ref, gqs_ref, gk_ref, gks_ref,
                  bdq_ref, bdk_ref,
                  qa_ref, ka_ref, va_ref, sga_ref, qb_ref, kb_ref, vb_ref, sgb_ref):
    xm = _modulated(x_ref, g_ref, scale_ref, shift_ref).astype(BF16)
    u = jnp.dot(xm, w_ref[...], preferred_element_type=F32)
    qa, ka, va, ga = u[:, 0:512], u[:, 512:1024], u[:, 1024:1536], u[:, 1536:2048]
    qb, kb, vb, gb = u[:, 2048:2560], u[:, 2560:2688], u[:, 2688:2816], u[:, 2816:3328]
    qbs, kbs = u[:, 3328:3840], u[:, 3840:3968]
    scale = HEAD_DIM ** -0.5
    qa_ref[...] = (qa * scale).astype(BF16)
    ka_ref[...] = ka.astype(BF16)
    va_ref[...] = va.astype(BF16)
    sga_ref[...] = _silu(ga)
    sgb_ref[...] = _silu(gb)
    vb_ref[...] = vb.astype(BF16)
    cos_k, sin_k = cos_ref[...], sin_ref[...]
    cos_q = jnp.concatenate([cos_k] * 4, axis=1)
    sin_q = jnp.concatenate([sin_k] * 4, axis=1)
    rs_q = lax.rsqrt(jnp.dot(qb * qb, bdq_ref[...], precision=HIGHEST, preferred_element_type=F32) + NORM_EPS)
    rs_k = lax.rsqrt(jnp.dot(kb * kb, bdk_ref[...], precision=HIGHEST, preferred_element_type=F32) + NORM_EPS)
    qr = rs_q * (qb * gq_ref[...] * cos_q + qbs * gqs_ref[...] * sin_q)
    kr = rs_k * (kb * gk_ref[...] * cos_k + kbs * gks_ref[...] * sin_k)
    qb_ref[...] = (qr * scale).astype(BF16)
    kb_ref[...] = kr.astype(BF16)


def _rope_partner():
    d = np.arange(HEAD_DIM)
    return np.where((d // ROPE_FREQS) % 2 == 0, d + ROPE_FREQS, d - ROPE_FREQS)


def _rope_tables(n):
    t = jnp.arange(n, dtype=jnp.int32)
    pos = jnp.stack([t // GRID_W, t % GRID_W], axis=-1).astype(F32)
    inv_freq = ROPE_THETA ** (-jnp.arange(ROPE_FREQS, dtype=F32) / ROPE_FREQS)
    ang = pos[:, :, None] * inv_freq
    c, s = jnp.cos(ang), jnp.sin(ang)
    cos64 = jnp.concatenate([c[:, 0], c[:, 0], c[:, 1], c[:, 1]], axis=-1)
    sin64 = jnp.concatenate([-s[:, 0], s[:, 0], -s[:, 1], s[:, 1]], axis=-1)
    cos64 = jnp.concatenate([jnp.ones((CTX_LEN, HEAD_DIM), F32), cos64], axis=0)
    sin64 = jnp.concatenate([jnp.zeros((CTX_LEN, HEAD_DIM), F32), sin64], axis=0)
    return jnp.tile(cos64, (1, 2)), jnp.tile(sin64, (1, 2))


def _head_cols(order):
    return np.concatenate([np.arange(h * HEAD_DIM, (h + 1) * HEAD_DIM) for h in order])


def attn_in_proj(xs, norm_g, scale2, shift2, w_in, q_gain, k_gain, cos_t, sin_t):
    r = xs.shape[0]
    d = D_MODEL
    partner = _rope_partner()
    parts, start = [], 0
    for s in ATTN_SPLITS:
        parts.append(w_in[:, start:start + s])
        start += s
    wqa, wka, wva, wga, wqb, wkb, wvb, wgb = parts
    perm = _head_cols(GQA_HEAD_ORDER)
    wqb_p = wqb[:, perm]
    wgb_p = wgb[:, perm]
    wqb_sw = wqb.reshape(d, N_HEADS, HEAD_DIM)[:, :, partner].reshape(d, BRANCH_W)[:, perm]
    wkb_sw = wkb.reshape(d, 2, HEAD_DIM)[:, :, partner].reshape(d, GQA_KV_W)
    w_ext = jnp.concatenate([wqa, wka, wva, wga, wqb_p, wkb, wvb, wgb_p, wqb_sw, wkb_sw], axis=1).astype(BF16)
    gq = jnp.tile(q_gain, N_HEADS)[None]
    gqs = jnp.tile(q_gain[partner], N_HEADS)[None]
    gk = jnp.tile(k_gain, 2)[None]
    gks = jnp.tile(k_gain[partner], 2)[None]
    bdq = jnp.asarray(np.kron(np.eye(N_HEADS), np.full((HEAD_DIM, HEAD_DIM), 1.0 / HEAD_DIM)), F32)
    bdk = jnp.asarray(np.kron(np.eye(2), np.full((HEAD_DIM, HEAD_DIM), 1.0 / HEAD_DIM)), F32)
    wcols = w_ext.shape[1]
    const = lambda shp: pl.BlockSpec(shp, lambda i: (0,) * len(shp))
    rows = lambda w: pl.BlockSpec((ROW_TILE, w), lambda i: (i, 0))
    out_shapes = [
        jax.ShapeDtypeStruct((r, 512), BF16), jax.ShapeDtypeStruct((r, 512), BF16), jax.ShapeDtypeStruct((r, 512), BF16),
        jax.ShapeDtypeStruct((r, 512), F32),
        jax.ShapeDtypeStruct((r, 512), BF16), jax.ShapeDtypeStruct((r, 128), BF16), jax.ShapeDtypeStruct((r, 128), BF16),
        jax.ShapeDtypeStruct((r, 512), F32),
    ]
    return pl.pallas_call(
        _attn_in_body,
        out_shape=out_shapes,
        grid=(r // ROW_TILE,),
        in_specs=_mod_specs() + [const((d, wcols)), rows(128), rows(128), const((1, 512)), const((1, 512)),
                                 const((1, 128)), const((1, 128)), const((512, 512)), const((128, 128))],
        out_specs=[rows(512), rows(512), rows(512), rows(512), rows(512), rows(128), rows(128), rows(512)],
        compiler_params=_cparams(("parallel",)),
        name="attn_in_proj",
    )(xs, norm_g[None], scale2, shift2, w_ext, cos_t, sin_t, gq, gqs, gk, gks, bdq, bdk)


def _na_bias_tables(rpb, rows):
    kh = min(NA_WIN_ROWS, rows)
    g = rows // NA_GROUP_ROWS
    cases = [(0, 0), (NA_GROUP_ROWS, 0), (rows - NA_GROUP_ROWS, NA_GROUP_ROWS * (g - 3))]
    j = np.arange(NA_GROUP_ROWS)[:, None, None, None]
    qc = np.arange(GRID_W)[None, :, None, None]
    i = np.arange(3 * NA_GROUP_ROWS)[None, None, :, None]
    kc = np.arange(GRID_W)[None, None, None, :]
    col0 = np.clip(qc - NA_WIN_COLS // 2, 0, GRID_W - NA_WIN_COLS)
    tabs = []
    for qr_first, start in cases:
        qr = qr_first + j
        kr = start + i
        row0 = np.clip(qr - kh // 2, 0, rows - kh)
        valid = (kr >= row0) & (kr < row0 + kh) & (kc >= col0) & (kc < col0 + NA_WIN_COLS)
        rr = np.clip(kr - qr + NA_WIN_ROWS - 1, 0, 2 * NA_WIN_ROWS - 2)
        rc = np.clip(kc - qc + NA_WIN_COLS - 1, 0, 2 * NA_WIN_COLS - 2)
        shape = (NA_GROUP_ROWS, GRID_W, 3 * NA_GROUP_ROWS, GRID_W)
        rr, rc, valid = (np.broadcast_to(a, shape).reshape(ROW_TILE, 3 * ROW_TILE) for a in (rr, rc, valid))
        tabs.append(jnp.where(jnp.asarray(valid)[None], rpb[:, rr, rc], NEG_BIG))
    return jnp.stack(tabs)


def _na_body(q_ref, kc_ref, k0_ref, k1_ref, k2_ref, vc_ref, v0_ref, v1_ref, v2_ref, bias_ref, sg_ref, o_ref):
    half = _lane_half((ROW_TILE, 128))
    for hp in range(N_HEADS // 2):
        ls = slice(hp * 128, (hp + 1) * 128)
        qp = q_ref[:, ls]
        ks = [r[:, ls] for r in (k0_ref, k1_ref, k2_ref, kc_ref)]
        vs = [r[:, ls] for r in (v0_ref, v1_ref, v2_ref, vc_ref)]
        outs = []
        for j in range(2):
            qm = jnp.where(half == j, qp, jnp.zeros_like(qp))
            s = [_dot_nt(qm, k) for k in ks]
            s_win = jnp.concatenate(s[:3], axis=1) + bias_ref[0, 2 * hp + j]
            s_ctx = s[3]
            m = jnp.maximum(jnp.max(s_win, axis=1, keepdims=True), jnp.max(s_ctx, axis=1, keepdims=True))
            p_win = jnp.exp(s_win - m)
            p_ctx = jnp.exp(s_ctx - m)
            l = jnp.sum(p_win, axis=1, keepdims=True) + jnp.sum(p_ctx, axis=1, keepdims=True)
            o = jnp.dot(p_ctx.astype(BF16), vs[3], preferred_element_type=F32)
            for b in range(3):
                o += jnp.dot(p_win[:, b * ROW_TILE:(b + 1) * ROW_TILE].astype(BF16), vs[b], preferred_element_type=F32)
            outs.append(o / l)
        o_pair = jnp.where(half == 0, outs[0], outs[1])
        o_ref[:, ls] = (o_pair * sg_ref[:, ls]).astype(BF16)


def na_attention(qa, ka, va, sga, bias_tabs, n):
    g = n // ROW_TILE
    w = BRANCH_W

    def kv_spec(off):
        return pl.BlockSpec((ROW_TILE, w), lambda i: (jnp.clip(i - 1, 0, g - 3) + off + 1, 0))

    ctx_spec = pl.BlockSpec((ROW_TILE, w), lambda i: (0, 0))
    q_spec = pl.BlockSpec((ROW_TILE, w), lambda i: (i + 1, 0))
    case = lambda i: jnp.where(i == 0, 0, jnp.where(i == g - 1, 2, 1))
    bias_spec = pl.BlockSpec((1, N_HEADS, ROW_TILE, 3 * ROW_TILE), lambda i: (case(i), 0, 0, 0))
    return pl.pallas_call(
        _na_body,
        out_shape=jax.ShapeDtypeStruct((n, w), BF16),
        grid=(g,),
        in_specs=[q_spec, ctx_spec, kv_spec(0), kv_spec(1), kv_spec(2), ctx_spec, kv_spec(0), kv_spec(1), kv_spec(2),
                  bias_spec, q_spec],
        out_specs=pl.BlockSpec((ROW_TILE, w), lambda i: (i, 0)),
        compiler_params=_cparams(("parallel",)),
        name="na_attention",
    )(qa, ka, ka, ka, ka, va, va, va, va, bias_tabs, sga)


def _flash_body(q_ref, k_ref, v_ref, sg_ref, o_ref, m_ref, l_ref, acc_ref, *, mha):
    kv = pl.program_id(1)
    tq = q_ref.shape[0]

    @pl.when(kv == 0)
    def _():
        m_ref[...] = jnp.full(m_ref.shape, NEG_BIG, F32)
        l_ref[...] = jnp.zeros(l_ref.shape, F32)
        acc_ref[...] = jnp.zeros(acc_ref.shape, F32)

    khalf = _lane_half((k_ref.shape[0], 128))
    for p in range(N_HEADS // 2):
        ls = slice(p * 128, (p + 1) * 128)
        kls = ls if mha else slice(0, 128)
        qp = q_ref[:, ls]
        kp = k_ref[:, kls]
        vp = v_ref[:, kls]
        for j in range(2):
            hh = 2 * p + j
            km = jnp.where(khalf == j, kp, jnp.zeros_like(kp))
            s = _dot_nt(qp, km)
            m_prev = m_ref[hh]
            m_new = jnp.maximum(m_prev, jnp.max(s, axis=1, keepdims=True))
            alpha = jnp.exp(m_prev - m_new)
            pr = jnp.exp(s - m_new[:, :1])
            l_ref[hh] = alpha * l_ref[hh] + jnp.sum(pr, axis=1, keepdims=True)
            acc_ref[hh] = alpha * acc_ref[hh] + jnp.dot(pr.astype(BF16), vp, preferred_element_type=F32)
            m_ref[hh] = m_new

    @pl.when(kv == pl.num_programs(1) - 1)
    def _():
        half = _lane_half((tq, 128))
        for p in range(N_HEADS // 2):
            ls = slice(p * 128, (p + 1) * 128)
            o0 = acc_ref[2 * p] / l_ref[2 * p]
            o1 = acc_ref[2 * p + 1] / l_ref[2 * p + 1]
            o_ref[:, ls] = (jnp.where(half == 0, o0, o1) * sg_ref[:, ls]).astype(BF16)


def flash_attention(q, k, v, sg, *, q_block0, nq, tk, nk, mha):
    kw = k.shape[1]
    tq = ROW_TILE
    return pl.pallas_call(
        functools.partial(_flash_body, mha=mha),
        out_shape=jax.ShapeDtypeStruct((nq * tq, BRANCH_W), BF16),
        grid=(nq, nk),
        in_specs=[
            pl.BlockSpec((tq, BRANCH_W), lambda i, j: (i + q_block0, 0)),
            pl.BlockSpec((tk, kw), lambda i, j: (j, 0)),
            pl.BlockSpec((tk, kw), lambda i, j: (j, 0)),
            pl.BlockSpec((tq, BRANCH_W), lambda i, j: (i + q_block0, 0)),
        ],
        out_specs=pl.BlockSpec((tq, BRANCH_W), lambda i, j: (i, 0)),
        scratch_shapes=[pltpu.VMEM((N_HEADS, tq, 128), F32)] * 3,
        compiler_params=_cparams(("parallel", "arbitrary")),
        name="flash_mha" if mha else "flash_gqa",
    )(q, k, v, sg)


def _out_body(x_ref, ya_ref, yb_ref, wa_ref, wb_ref, gate_ref, fin_ref, o_ref, *, final):
    y = jnp.dot(ya_ref[...], wa_ref[...], preferred_element_type=F32)
    y += jnp.dot(yb_ref[...], wb_ref[...], preferred_element_type=F32)
    xn = x_ref[...] + gate_ref[0] * y
    if final:
        xn = xn * lax.rsqrt(jnp.mean(xn * xn, axis=-1, keepdims=True) + NORM_EPS) * fin_ref[...]
    o_ref[...] = xn


def out_proj(xs, ya, yb, wa, wb, gate2, final_g, *, final):
    r = xs.shape[0]
    d = D_MODEL
    rows = lambda w: pl.BlockSpec((ROW_TILE, w), lambda i: (i, 0))
    const = lambda shp: pl.BlockSpec(shp, lambda i: (0,) * len(shp))
    return pl.pallas_call(
        functools.partial(_out_body, final=final),
        out_shape=jax.ShapeDtypeStruct((r, d), F32),
        grid=(r // ROW_TILE,),
        in_specs=[rows(d), rows(BRANCH_W), rows(BRANCH_W), const((BRANCH_W, d)), const((BRANCH_W, d)),
                  pl.BlockSpec((1, 1, d), lambda i: (jnp.minimum(i, 1), 0, 0)), const((1, d))],
        out_specs=rows(d),
        compiler_params=_cparams(("parallel",)),
        name="out_proj",
    )(xs, ya, yb, wa.astype(BF16), wb.astype(BF16), gate2, final_g[None])


def _kv_tile(r):
    nb = r // ROW_TILE
    best = max(k for k in range(1, 9) if nb % k == 0)
    return best * ROW_TILE, nb // best


def attn_layer(xs, mods, norm_g, w_in, rpb, q_gain, k_gain, w_out, cos_t, sin_t, final_g, final):
    r = xs.shape[0]
    n = r - CTX_LEN
    shift2, scale2, gate2 = mods
    qa, ka, va, sga, qb, kb, vb, sgb = attn_in_proj(xs, norm_g, scale2, shift2, w_in, q_gain, k_gain, cos_t, sin_t)
    bias_tabs = _na_bias_tables(rpb, n // GRID_W)
    ya_lat = na_attention(qa, ka, va, sga, bias_tabs, n)
    ya_ctx = flash_attention(qa, ka, va, sga, q_block0=0, nq=1, tk=CTX_LEN, nk=1, mha=True)
    tk, nk = _kv_tile(r)
    yb_lat = flash_attention(qb, kb, vb, sgb, q_block0=1, nq=n // ROW_TILE, tk=tk, nk=nk, mha=False)
    yb_ctx = flash_attention(qb, kb, vb, sgb, q_block0=0, nq=1, tk=CTX_LEN, nk=1, mha=False)
    ya = jnp.concatenate([ya_ctx, ya_lat], axis=0)
    yb = jnp.concatenate([yb_ctx, yb_lat], axis=0)
    wb = w_out[BRANCH_W:][_head_cols(GQA_HEAD_ORDER)]
    return out_proj(xs, ya, yb, w_out[:BRANCH_W], wb, gate2, final_g, final=final)


def _split(a):
    hi = a.astype(BF16)
    return hi, (a - hi.astype(F32)).astype(BF16)


def _dot3(a, b, dims=(((1,), (0,)), ((), ()))):
    ah, al = _split(a)
    bh, bl = _split(b)
    dg = functools.partial(lax.dot_general, dimension_numbers=dims, preferred_element_type=F32)
    return dg(ah, bh) + (dg(al, bh) + dg(ah, bl))


_NT = (((1,), (1,)), ((), ()))
_TN = (((0,), (0,)), ((), ()))


RWKV_SHIFT_W = 1664
HY_IN_W = 1536
HALO = 8
REC_HALO_W = RWKV_SHIFT_W + HY_IN_W


def _rec_in_body(x_ref, xp_ref, xn_ref, g_ref, scale_ref, shift_ref, w_ref, mu_ref, taps_ref,
                 rw_ref, hv_ref, hx1_ref, hx2_ref, sgr_ref, sgh_ref, u_scr):
    i = pl.program_id(0)
    nt = pl.num_programs(0)
    xe = jnp.concatenate([xp_ref[...], x_ref[...], xn_ref[...]], axis=0)
    y = xe * lax.rsqrt(jnp.mean(xe * xe, axis=-1, keepdims=True) + NORM_EPS)
    xm = ((y * g_ref[...]) * (1.0 + scale_ref[0]) + shift_ref[0]).astype(BF16)
    u = jnp.dot(xm, w_ref[...], preferred_element_type=F32)
    row = lax.broadcasted_iota(jnp.int32, (ROW_TILE + 2 * HALO, 1), 0)
    keep = jnp.logical_and(jnp.logical_or(row >= HALO, i >= 2),
                           jnp.logical_or(row < ROW_TILE + HALO, jnp.logical_and(i >= 1, i < nt - 1)))
    u_scr[...] = jnp.where(keep, u[:, :REC_HALO_W], 0.0)
    up = u_scr[pl.ds(HALO - 1, ROW_TILE), :]
    uc = u_scr[pl.ds(HALO, ROW_TILE), :]
    un = u_scr[pl.ds(HALO + 1, ROW_TILE), :]
    w = RWKV_SHIFT_W
    rw_c = uc[:, :w]
    rw_ref[...] = rw_c + (0.5 * (up[:, :w] + un[:, :w]) - rw_c) * mu_ref[...]
    hy = up[:, w:] * taps_ref[0:1] + uc[:, w:] * taps_ref[1:2] + un[:, w:] * taps_ref[2:3]
    hv_ref[...] = hy[:, 0:512]
    hx1_ref[...] = hy[:, 512:1024]
    hx2_ref[...] = hy[:, 1024:1536]
    uc_all = u[HALO:HALO + ROW_TILE]
    sgr_ref[...] = _silu(uc_all[:, REC_HALO_W:REC_HALO_W + 512])
    sgh_ref[...] = _silu(uc_all[:, REC_HALO_W + 512:REC_HALO_W + 1024])


def rec_in_proj(xs, norm_g, scale2, shift2, w_in, mu, hy_short):
    r = xs.shape[0]
    d = D_MODEL
    w = RWKV_SHIFT_W
    w_ext = jnp.concatenate([w_in[:, :w], w_in[:, w + 512:w + 512 + HY_IN_W], w_in[:, w:w + 512],
                             w_in[:, w + 512 + HY_IN_W:]], axis=1).astype(BF16)
    nh = r // HALO
    per = ROW_TILE // HALO
    const = lambda shp: pl.BlockSpec(shp, lambda i: (0,) * len(shp))
    rows = lambda wd: pl.BlockSpec((ROW_TILE, wd), lambda i: (i, 0))
    f = lambda wd: jax.ShapeDtypeStruct((r, wd), F32)
    mod = _mod_specs()
    return pl.pallas_call(
        _rec_in_body,
        out_shape=[f(w), f(512), f(512), f(512), f(512), f(512)],
        grid=(r // ROW_TILE,),
        in_specs=[mod[0],
                  pl.BlockSpec((HALO, d), lambda i: (jnp.maximum(i * per - 1, 0), 0)),
                  pl.BlockSpec((HALO, d), lambda i: (jnp.minimum((i + 1) * per, nh - 1), 0)),
                  mod[1], mod[2], mod[3], const((d, w_ext.shape[1])), const((1, w)), const((3, HY_IN_W))],
        out_specs=[rows(w), rows(512), rows(512), rows(512), rows(512), rows(512)],
        scratch_shapes=[pltpu.VMEM((ROW_TILE + 2 * HALO, REC_HALO_W), F32)],
        compiler_params=_cparams(("parallel",)),
        name="rec_in_proj",
    )(xs, xs, xs, norm_g[None], scale2, shift2, w_ext, mu[None], hy_short)


CHUNK = 64
CPT = ROW_TILE // CHUNK


def _block_sum_mat(width, value):
    return jnp.asarray(np.kron(np.eye(width // HEAD_DIM), np.full((HEAD_DIM, HEAD_DIM), value)), F32)


def _rwkv_prep_body(r_ref, k_ref, v_ref, lora_ref, w0_ref, wup_ref, a0_ref, aup_ref, kk_ref, ka_ref, rk_ref,
                    tri_ref, bs_ref, g_ref, add_ref, bonus_ref):
    d = pl.program_id(2)
    t = ROW_TILE
    r, k, v, lora = r_ref[...], k_ref[...], v_ref[...], lora_ref[...]
    bs = bs_ref[...]
    tri = tri_ref[0]
    kk = k * kk_ref[...]
    kk = kk * lax.rsqrt(_dot3(kk * kk, bs) + 1e-12)
    wl = w0_ref[0] + _dot3(jnp.tanh(lora), wup_ref[0])
    z = -wl
    w_log = -(jnp.maximum(z, 0.0) + jnp.log(1.0 + jnp.exp(-jnp.abs(z)))) - 0.5
    lw = -jnp.exp(w_log)
    a = 1.0 / (1.0 + jnp.exp(-(a0_ref[0] + _dot3(lora, aup_ref[0]))))
    kd = k * (1.0 + (a - 1.0) * ka_ref[...])
    b = kk * a

    bon = 0.5 * _dot3(r * kd * rk_ref[...], bs) * v

    @pl.when(d == 0)
    def _():
        bonus_ref[...] = bon

    @pl.when(d == 1)
    def _():
        bonus_ref[...] += bon

    row = lax.broadcasted_iota(jnp.int32, (t, t), 0)
    col = lax.broadcasted_iota(jnp.int32, (t, t), 1)
    incl = tri > 0.5
    strict = jnp.logical_and(incl, row != col)
    same = (row // CHUNK) == (col // CHUNK)
    cs = _dot3(tri, lw)
    tot = _dot3(same.astype(F32), lw)
    w_incl = jnp.exp(cs)
    w_inv = jnp.exp(-cs)
    kkt = kk * jnp.exp(cs - lw)
    kh = kd * w_inv
    bh = b * w_inv
    rt = r * w_incl
    w_rest = jnp.exp(tot - cs)
    kdd = kd * w_rest
    bdd = b * w_rest
    half = _lane_half((t, 128))
    eye = (row == col).astype(F32)

    per_half = []
    for j in range(2):
        sel = half == j
        bm = jnp.where(sel, bh, 0.0)
        km = jnp.where(sel, kh, 0.0)
        l_b = jnp.where(strict, _dot3(kkt, bm, _NT), 0.0)
        l_k = jnp.where(strict, _dot3(kkt, km, _NT), 0.0)
        a_rk = jnp.where(incl, _dot3(rt, km, _NT), 0.0)
        a_rb = jnp.where(incl, _dot3(rt, bm, _NT), 0.0)
        tinv = eye - jnp.where((row // 2) == (col // 2), l_b, 0.0)
        bsz = 2
        while bsz < CHUNK:
            off = jnp.where(jnp.logical_and((row // (2 * bsz)) == (col // (2 * bsz)), (row // bsz) != (col // bsz)),
                            l_b, 0.0)
            tinv = tinv - _dot3(tinv, _dot3(off, tinv))
            bsz *= 2
        p = _dot3(tinv, kkt)
        u0 = _dot3(tinv, _dot3(l_k, v))
        q = rt - _dot3(a_rb, p)
        y0 = _dot3(a_rk, v) - _dot3(a_rb, u0)
        per_half.append((p, u0, q, y0))
    sel0 = half == 0
    p, u0, q, y0 = (jnp.where(sel0, x0, x1) for x0, x1 in zip(*per_half))

    half_c = _lane_half((HEAD_DIM, 128))
    rowc = lax.broadcasted_iota(jnp.int32, (HEAD_DIM, 128), 0)
    lanec = lax.broadcasted_iota(jnp.int32, (HEAD_DIM, 128), 1)
    for c in range(CPT):
        rs = slice(c * CHUNK, (c + 1) * CHUNK)
        x1 = _dot3(bdd[rs], p[rs], _TN)
        x2 = _dot3(kdd[rs], v[rs], _TN) - _dot3(bdd[rs], u0[rs], _TN)
        m_pair = jnp.where(half_c == 0, x1[:HEAD_DIM], x1[HEAD_DIM:])
        n_pair = jnp.where(half_c == 0, x2[:HEAD_DIM], x2[HEAD_DIM:])
        wc = jnp.exp(tot[c * CHUNK:c * CHUNK + 1])
        dg = jnp.where((lanec % HEAD_DIM) == rowc, wc, 0.0)
        g_ref[c, 0, 0:HEAD_DIM, :] = dg - m_pair
        g_ref[c, 0, HEAD_DIM:, :] = q[rs]
        add_ref[c, 0, 0:HEAD_DIM, :] = n_pair
        add_ref[c, 0, HEAD_DIM:, :] = y0[rs]


def _lora_ext(up, first_row):
    out = jnp.zeros((2, 128, BRANCH_W), F32)
    for d in range(2):
        out = out.at[d, first_row + 32 * d:first_row + 32 * (d + 1)].set(up[d])
    return out


def rwkv_prep(rw, w0, w_up, a0, a_up, k_k, k_a, r_k):
    r = rw.shape[0]
    nt = r // ROW_TILE
    nch = r // CHUNK
    t = ROW_TILE
    ii = np.arange(t)
    same = (ii[:, None] // CHUNK) == (ii[None, :] // CHUNK)
    tri = jnp.asarray(np.stack([same & (ii[None, :] <= ii[:, None]), same & (ii[None, :] >= ii[:, None])]), F32)
    lane = lambda blk: pl.BlockSpec((t, 128), lambda i, p, d, blk=blk: (i, blk + p))
    pvec = pl.BlockSpec((1, 128), lambda i, p, d: (0, p))
    dvec = pl.BlockSpec((1, 1, 128), lambda i, p, d: (d, 0, p))
    dmat = pl.BlockSpec((1, 128, 128), lambda i, p, d: (d, 0, p))
    gspec = pl.BlockSpec((CPT, 1, HEAD_DIM + CHUNK, 128), lambda i, p, d: (i, d, 0, p))
    gshape = jax.ShapeDtypeStruct((nch, 2, HEAD_DIM + CHUNK, BRANCH_W), F32)
    return pl.pallas_call(
        _rwkv_prep_body,
        out_shape=[gshape, gshape, jax.ShapeDtypeStruct((r, BRANCH_W), F32)],
        grid=(nt, N_HEADS // 2, 2),
        in_specs=[lane(0), lane(4), lane(8), pl.BlockSpec((t, 128), lambda i, p, d: (i, 12)),
                  dvec, dmat, dvec, dmat, pvec, pvec, pvec,
                  pl.BlockSpec((1, t, t), lambda i, p, d: (d, 0, 0)),
                  pl.BlockSpec((128, 128), lambda i, p, d: (0, 0))],
        out_specs=[gspec, gspec, pl.BlockSpec((t, 128), lambda i, p, d: (i, p))],
        compiler_params=_cparams(("parallel", "parallel", "arbitrary")),
        name="rwkv_prep",
    )(rw, rw, rw, rw, w0.reshape(2, 1, BRANCH_W), _lora_ext(w_up, 0), a0.reshape(2, 1, BRANCH_W),
      _lora_ext(a_up, 64), k_k[None], k_a[None], r_k.reshape(1, BRANCH_W), tri, _block_sum_mat(128, 1.0))


def _rwkv_scan_body(gf_ref, af_ref, gb_ref, ab_ref, yf_ref, yb_ref, st_ref):
    @pl.when(pl.program_id(0) == 0)
    def _():
        st_ref[...] = jnp.zeros(st_ref.shape, F32)

    rowh = lax.broadcasted_iota(jnp.int32, (128, 128), 0) // HEAD_DIM
    diag = rowh == _lane_half((128, 128))
    for d, (g_ref, a_ref, y_ref) in enumerate(((gf_ref, af_ref, yf_ref), (gb_ref, ab_ref, yb_ref))):
        for p in range(N_HEADS // 2):
            ls = slice(p * 128, (p + 1) * 128)
            out = _dot3(g_ref[0, 0, :, ls], st_ref[d, p]) + a_ref[0, 0, :, ls]
            hn = out[:HEAD_DIM]
            st_ref[d, p] = jnp.where(diag, jnp.concatenate([hn, hn], axis=0), 0.0)
            y_ref[:, ls] = out[HEAD_DIM:]


def rwkv_scan(g, add):
    nch = g.shape[0]
    r = nch * CHUNK
    nctx = CTX_LEN // CHUNK
    rev = lambda c: jnp.where(c < nctx, nctx - 1 - c, nch + nctx - 1 - c)
    blk = (1, 1, HEAD_DIM + CHUNK, BRANCH_W)
    fwd = pl.BlockSpec(blk, lambda c: (c, 0, 0, 0))
    bwd = pl.BlockSpec(blk, lambda c: (rev(c), 1, 0, 0))
    yshape = jax.ShapeDtypeStruct((r, BRANCH_W), F32)
    return pl.pallas_call(
        _rwkv_scan_body,
        out_shape=[yshape, yshape],
        grid=(nch,),
        in_specs=[fwd, fwd, bwd, bwd],
        out_specs=[pl.BlockSpec((CHUNK, BRANCH_W), lambda c: (c, 0)),
                   pl.BlockSpec((CHUNK, BRANCH_W), lambda c: (rev(c), 0))],
        scratch_shapes=[pltpu.VMEM((2, N_HEADS // 2, 128, 128), F32)],
        compiler_params=_cparams(("arbitrary",)),
        name="rwkv_scan",
    )(g, add, g, add)


HY_WIDTH = 512
HY_ORDER = 2
HY_POS_BANDS = 16
HY_HIDDEN = 64
HY_TAPS_W = 2 * HY_ORDER * HY_WIDTH
FFT_N2 = ROW_TILE


def _dot3c(ah, al, b):
    bh, bl = _split(b)
    dg = functools.partial(jnp.dot, preferred_element_type=F32)
    return dg(ah, bh) + (dg(al, bh) + dg(ah, bl))


def _split_const(m):
    m = np.asarray(m, np.float32)
    hi = m.astype(BF16)
    lo = (m - hi.astype(np.float32)).astype(BF16)
    return jnp.asarray(hi), jnp.asarray(lo)


def _hy_taps_body(c2pb_ref, w1t_ref, w1c_ref, w1s_ref, b1_ref, w2_ref, b2_ref, w3_ref, b3_ref, absd_ref,
                  taps_ref, ssq_ref, *, length):
    i = pl.program_id(0)
    t_idx = (i * ROW_TILE + lax.broadcasted_iota(jnp.int32, (ROW_TILE, 1), 0)).astype(F32)
    t = t_idx / float(max(length - 1, 1))
    ang = c2pb_ref[...] * t_idx / float(length)
    pre = t * w1t_ref[...] + _dot3(jnp.cos(ang), w1c_ref[...]) - _dot3(jnp.sin(ang), w1s_ref[...]) + b1_ref[...]
    hid = jnp.sin(pre)
    hid = jnp.sin(_dot3(hid, w2_ref[...]) + b2_ref[...])
    taps = (_dot3(hid, w3_ref[...]) + b3_ref[...]) * jnp.exp(-t * absd_ref[...])
    taps_ref[...] = taps

    @pl.when(i == 0)
    def _():
        ssq_ref[...] = jnp.zeros(ssq_ref.shape, F32)

    ssq_ref[...] += jnp.sum(taps * taps, axis=0, keepdims=True)


def hyena_taps(length, w1, b1, w2, b2, w3, b3):
    bands = jnp.linspace(1e-4, HY_POS_BANDS - 1, HY_POS_BANDS, dtype=F32)
    c2pb = jnp.zeros((1, 128), F32).at[0, :HY_POS_BANDS].set(2.0 * math.pi * bands)
    pad = lambda m: jnp.zeros((128, HY_HIDDEN), F32).at[:HY_POS_BANDS].set(m)
    deltas = jnp.linspace(math.log(1e-2) / 0.3, math.log(1e-2) / 1.5, HY_WIDTH, dtype=F32)
    absd = jnp.tile(jnp.abs(deltas), 2 * HY_ORDER)[None]
    const = lambda shp: pl.BlockSpec(shp, lambda i: (0,) * len(shp))
    h = HY_HIDDEN
    taps, ssq = pl.pallas_call(
        functools.partial(_hy_taps_body, length=length),
        out_shape=[jax.ShapeDtypeStruct((length, HY_TAPS_W), F32), jax.ShapeDtypeStruct((1, HY_TAPS_W), F32)],
        grid=(length // ROW_TILE,),
        in_specs=[const((1, 128)), const((1, h)), const((128, h)), const((128, h)), const((1, h)), const((h, h)),
                  const((1, h)), const((h, HY_TAPS_W)), const((1, HY_TAPS_W)), const((1, HY_TAPS_W))],
        out_specs=[pl.BlockSpec((ROW_TILE, HY_TAPS_W), lambda i: (i, 0)), const((1, HY_TAPS_W))],
        compiler_params=_cparams(("arbitrary",)),
        name="hyena_taps",
    )(c2pb, w1[0:1], pad(w1[1:1 + HY_POS_BANDS]), pad(w1[1 + HY_POS_BANDS:]), b1[None], w2, b2[None], w3, b3[None], absd)
    hw = HY_TAPS_W // 2
    norm2 = ssq[:, :hw] + ssq[:, hw:] + 2.0 * taps[0:1, :hw] * taps[0:1, hw:]
    return taps, lax.rsqrt(norm2)


class _FftPlan:
    def __init__(self, length):
        self.length = length
        self.n = 2 * length
        self.n2 = FFT_N2
        self.n1 = self.n // self.n2
        self.n1h = self.n1 // 2
        k1 = self.n1h + 1
        self.k1p = -(-k1 // 8) * 8
        kk = np.arange(self.k1p)[:, None].astype(np.float64)
        live = (kk < k1)
        nn = np.arange(self.n1h)[None, :].astype(np.float64)
        th = 2.0 * np.pi * kk * nn / self.n1
        self.f1 = _split_const(np.concatenate([np.cos(th) * live, -np.sin(th) * live], axis=0))
        ck = np.where((kk == 0) | (kk == self.n1h), 1.0, 2.0) * live / self.n
        self.g1 = _split_const(np.concatenate([np.cos(th) * ck, -np.sin(th) * ck], axis=0).T)
        m = np.arange(self.n2).astype(np.float64)
        ph = 2.0 * np.pi * np.outer(m, m) / self.n2
        c, s = np.cos(ph), np.sin(ph)
        self.fb = _split_const(np.block([[c, s], [-s, c]]))
        self.fbi = _split_const(np.block([[c, -s], [s, c]]))
        tw = 2.0 * np.pi * kk[:, :, None] * m[None, :, None] / self.n
        self.twc = jnp.asarray(np.cos(tw), F32)
        self.tws = jnp.asarray(np.sin(tw), F32)


def _fft_a_body(fh_ref, fl_ref, x_ref, o_ref):
    o_ref[...] = _dot3c(fh_ref[...], fl_ref[...], x_ref[...])


def fft_stage_a(plan, xv, tn):
    m = xv.shape[1]
    rows = 2 * plan.k1p
    return pl.pallas_call(
        _fft_a_body,
        out_shape=jax.ShapeDtypeStruct((rows, m), F32),
        grid=(m // tn,),
        in_specs=[pl.BlockSpec((rows, plan.n1h), lambda j: (0, 0)), pl.BlockSpec((rows, plan.n1h), lambda j: (0, 0)),
                  pl.BlockSpec((plan.n1h, tn), lambda j: (0, j))],
        out_specs=pl.BlockSpec((rows, tn), lambda j: (0, j)),
        compiler_params=_cparams(("parallel",)),
        name="fft_stage_a",
    )(plan.f1[0], plan.f1[1], xv)


def _twiddled(a_ref, twc_ref, tws_ref):
    are, aim = a_ref[0, 0], a_ref[1, 0]
    c, s = twc_ref[0], tws_ref[0]
    return jnp.concatenate([are * c + aim * s, aim * c - are * s], axis=0)


def _fft_filter_b_body(a_ref, twc_ref, tws_ref, fbh_ref, fbl_ref, scale_ref, o_ref):
    n2 = FFT_N2
    x = _dot3c(fbh_ref[...], fbl_ref[...], _twiddled(a_ref, twc_ref, tws_ref))
    hw = HY_TAPS_W // 2
    xre, xim = x[:n2], x[n2:]
    o_ref[0, 0] = (xre[:, :hw] + xre[:, hw:]) * scale_ref[...]
    o_ref[1, 0] = (xim[:, :hw] - xim[:, hw:]) * scale_ref[...]


def fft_filter_stage_b(plan, a, scale):
    n2, k1p = plan.n2, plan.k1p
    a4 = a.reshape(2, k1p, n2, HY_TAPS_W)
    hw = HY_TAPS_W // 2
    const = lambda shp: pl.BlockSpec(shp, lambda k: (0,) * len(shp))
    return pl.pallas_call(
        _fft_filter_b_body,
        out_shape=jax.ShapeDtypeStruct((2, k1p, n2, hw), F32),
        grid=(k1p,),
        in_specs=[pl.BlockSpec((2, 1, n2, HY_TAPS_W), lambda k: (0, k, 0, 0)),
                  pl.BlockSpec((1, n2, 1), lambda k: (k, 0, 0)), pl.BlockSpec((1, n2, 1), lambda k: (k, 0, 0)),
                  const((2 * n2, 2 * n2)), const((2 * n2, 2 * n2)), const((1, hw))],
        out_specs=pl.BlockSpec((2, 1, n2, hw), lambda k: (0, k, 0, 0)),
        compiler_params=_cparams(("parallel",)),
        name="fft_filter_stage_b",
    )(a4, plan.twc, plan.tws, plan.fb[0], plan.fb[1], scale)


def _fft_conv_b_body(a_ref, kf_ref, twc_ref, tws_ref, fbh_ref, fbl_ref, fih_ref, fil_ref, o_ref):
    n2 = FFT_N2
    z = _dot3c(fbh_ref[...], fbl_ref[...], _twiddled(a_ref, twc_ref, tws_ref))
    zre, zim = z[:n2], z[n2:]
    kre, kim = kf_ref[0, 0], kf_ref[1, 0]
    y = jnp.concatenate([zre * kre - zim * kim, zre * kim + zim * kre], axis=0)
    q = _dot3c(fih_ref[...], fil_ref[...], y)
    qre, qim = q[:n2], q[n2:]
    c, s = twc_ref[0], tws_ref[0]
    o_ref[0, 0] = qre * c - qim * s
    o_ref[1, 0] = qim * c + qre * s


def fft_conv_stage_b(plan, a, kf, order):
    n2, k1p = plan.n2, plan.k1p
    a4 = a.reshape(2, k1p, n2, HY_WIDTH)
    const = lambda shp: pl.BlockSpec(shp, lambda k: (0,) * len(shp))
    return pl.pallas_call(
        _fft_conv_b_body,
        out_shape=jax.ShapeDtypeStruct((2, k1p, n2, HY_WIDTH), F32),
        grid=(k1p,),
        in_specs=[pl.BlockSpec((2, 1, n2, HY_WIDTH), lambda k: (0, k, 0, 0)),
                  pl.BlockSpec((2, 1, n2, HY_WIDTH), lambda k: (0, k, 0, order)),
                  pl.BlockSpec((1, n2, 1), lambda k: (k, 0, 0)), pl.BlockSpec((1, n2, 1), lambda k: (k, 0, 0)),
                  const((2 * n2, 2 * n2)), const((2 * n2, 2 * n2)), const((2 * n2, 2 * n2)), const((2 * n2, 2 * n2))],
        out_specs=pl.BlockSpec((2, 1, n2, HY_WIDTH), lambda k: (0, k, 0, 0)),
        compiler_params=_cparams(("parallel",)),
        name="fft_conv_stage_b",
    )(a4, kf, plan.twc, plan.tws, plan.fb[0], plan.fb[1], plan.fbi[0], plan.fbi[1])


def _fft_inv_a_body(gh_ref, gl_ref, q_ref, z_ref, gate_ref, skip_ref, o_ref):
    y = _dot3c(gh_ref[...], gl_ref[...], q_ref[...])
    o_ref[...] = gate_ref[...] * (y + z_ref[...] * skip_ref[...])


def fft_inv_stage_a(plan, q, zv, gatev, skip_t, tn):
    m = zv.shape[1]
    rows = 2 * plan.k1p
    qv = q.reshape(rows, m)
    col = lambda r_: pl.BlockSpec((r_, tn), lambda j: (0, j))
    return pl.pallas_call(
        _fft_inv_a_body,
        out_shape=jax.ShapeDtypeStruct((plan.n1h, m), F32),
        grid=(m // tn,),
        in_specs=[pl.BlockSpec((plan.n1h, rows), lambda j: (0, 0)), pl.BlockSpec((plan.n1h, rows), lambda j: (0, 0)),
                  col(rows), col(plan.n1h), col(plan.n1h), pl.BlockSpec((1, tn), lambda j: (0, 0))],
        out_specs=col(plan.n1h),
        compiler_params=_cparams(("parallel",)),
        name="fft_inv_stage_a",
    )(plan.g1[0], plan.g1[1], qv, zv, gatev, skip_t)


FFT_TN = 4096


def hyena_long(hv, hx1, hx2, taps, scale, skip):
    length = hv.shape[0]
    plan = _FftPlan(length)
    m = plan.n2 * HY_WIDTH
    tn = min(FFT_TN, m)
    view = lambda a: a.reshape(plan.n1h, m)
    ta = fft_stage_a(plan, taps.reshape(plan.n1h, plan.n2 * HY_TAPS_W), tn)
    kf = fft_filter_stage_b(plan, ta, scale)
    z = view(hv)
    for o, gate in enumerate((hx1, hx2)):
        a = fft_stage_a(plan, z, tn)
        q = fft_conv_stage_b(plan, a, kf, o)
        z = fft_inv_stage_a(plan, q, z, view(gate), jnp.tile(skip[o], tn // HY_WIDTH)[None], tn)
    return z.reshape(length, HY_WIDTH)


def _rec_out_body(x_ref, yf_ref, yb_ref, bonus_ref, sgr_ref, zh_ref, sgh_ref, gnw_ref, gnb_ref, bm_ref, wa_ref, wb_ref,
                  gate_ref, fin_ref, o_ref, *, final):
    y = yf_ref[...] + yb_ref[...]
    bm = bm_ref[...]
    mean = jnp.dot(y, bm, precision=HIGHEST, preferred_element_type=F32)
    yc = y - mean
    var = jnp.dot(yc * yc, bm, precision=HIGHEST, preferred_element_type=F32)
    yn = yc * lax.rsqrt(var + RWKV_GN_EPS) * gnw_ref[...] + gnb_ref[...]
    ya = ((yn + bonus_ref[...]) * sgr_ref[...]).astype(BF16)
    yh = (zh_ref[...] * sgh_ref[...]).astype(BF16)
    out = jnp.dot(ya, wa_ref[...], preferred_element_type=F32) + jnp.dot(yh, wb_ref[...], preferred_element_type=F32)
    xn = x_ref[...] + gate_ref[0] * out
    if final:
        xn = xn * lax.rsqrt(jnp.mean(xn * xn, axis=-1, keepdims=True) + NORM_EPS) * fin_ref[...]
    o_ref[...] = xn


RWKV_GN_EPS = 64e-5


def rec_out_proj(xs, yf, yb, bonus, sgr, zh, sgh, gn_w, gn_b, w_out, gate2, final_g, *, final):
    r = xs.shape[0]
    d = D_MODEL
    off = 1 if final else 0
    nt = r // ROW_TILE - off
    rows = lambda w: pl.BlockSpec((ROW_TILE, w), lambda i: (i + off, 0))
    const = lambda shp: pl.BlockSpec(shp, lambda i: (0,) * len(shp))
    bw = BRANCH_W
    return pl.pallas_call(
        functools.partial(_rec_out_body, final=final),
        out_shape=jax.ShapeDtypeStruct((nt * ROW_TILE, d), F32),
        grid=(nt,),
        in_specs=[rows(d), rows(bw), rows(bw), rows(bw), rows(bw), rows(bw), rows(bw), const((1, bw)), const((1, bw)),
                  const((bw, bw)), const((bw, d)), const((bw, d)),
                  pl.BlockSpec((1, 1, d), lambda i: (jnp.minimum(i + off, 1), 0, 0)), const((1, d))],
        out_specs=pl.BlockSpec((ROW_TILE, d), lambda i: (i, 0)),
        compiler_params=_cparams(("parallel",)),
        name="rec_out_proj",
    )(xs, yf, yb, bonus, sgr, zh, sgh, gn_w[None], gn_b[None], _block_sum_mat(bw, 1.0 / HEAD_DIM),
      w_out[:bw].astype(BF16), w_out[bw:].astype(BF16), gate2, final_g[None])


def rec_layer(xs, mods, norm_g, w_in, mu, w0, w_up, a0, a_up, k_k, k_a, r_k, gn_w, gn_b, hy_short, hy_w1, hy_b1, hy_w2,
              hy_b2, hy_w3, hy_b3, hy_skip, w_out, final_g, final):
    shift2, scale2, gate2 = mods
    rw, hv, hx1, hx2, sgr, sgh = rec_in_proj(xs, norm_g, scale2, shift2, w_in, mu, hy_short)
    g, add, bonus = rwkv_prep(rw, w0, w_up, a0, a_up, k_k, k_a, r_k)
    yf, yb = rwkv_scan(g, add)
    fargs = (hy_w1, hy_b1, hy_w2, hy_b2, hy_w3, hy_b3)
    n = xs.shape[0] - CTX_LEN
    taps, scale = hyena_taps(n, *fargs)
    z_lat = hyena_long(hv[CTX_LEN:], hx1[CTX_LEN:], hx2[CTX_LEN:], taps, scale, hy_skip)
    if final:
        z_ctx = jnp.zeros((CTX_LEN, HY_WIDTH), F32)
    else:
        taps_c, scale_c = hyena_taps(CTX_LEN, *fargs)
        z_ctx = hyena_short(hv[:CTX_LEN], hx1[:CTX_LEN], hx2[:CTX_LEN], taps_c, scale_c, hy_skip)
    zh = jnp.concatenate([z_ctx, z_lat], axis=0)
    return rec_out_proj(xs, yf, yb, bonus, sgr, zh, sgh, gn_w, gn_b, w_out, gate2, final_g, final=final)


def kernel(x, c, ctx, c_ctx, attn_norm, attn_ada_w, attn_ada_b, attn_w_in, na_rpb, gqa_q_gain, gqa_k_gain, attn_w_out,
           rec_norm, rec_ada_w, rec_ada_b, rec_w_in, rwkv_mu, rwkv_w0, rwkv_w_up, rwkv_a0, rwkv_a_up, rwkv_k_k, rwkv_k_a,
           rwkv_r_k, rwkv_gn_w, rwkv_gn_b, hy_short, hy_w1, hy_b1, hy_w2, hy_b2, hy_w3, hy_b3, hy_skip, rec_w_out,
           final_norm):
    assert x.shape[0] == 1 and ctx.shape[1] == CTX_LEN and x.shape[2] == D_MODEL
    n = x.shape[1]
    assert n % ROW_TILE == 0 and n // ROW_TILE >= 3
    assert attn_w_in.shape[0] == rec_w_in.shape[0]
    d = D_MODEL
    cond8 = jnp.zeros((8, d), F32).at[0].set(c_ctx).at[1].set(c[0])
    m_attn = adaln_all(cond8, attn_ada_w, attn_ada_b)
    m_rec = adaln_all(cond8, rec_ada_w, rec_ada_b)
    mods = lambda m, i: tuple(m[i, :2, j * d:(j + 1) * d].reshape(2, 1, d) for j in range(3))
    cos_t, sin_t = _rope_tables(n)
    xs = jnp.concatenate([ctx[0], x[0]], axis=0)
    depth = attn_w_in.shape[0] + rec_w_in.shape[0]
    for layer in range(depth):
        i = layer // 2
        final = layer == depth - 1
        if layer % 2 == 0:
            xs = attn_layer(xs, mods(m_attn, i), attn_norm[i], attn_w_in[i], na_rpb[i], gqa_q_gain[i], gqa_k_gain[i],
                            attn_w_out[i], cos_t, sin_t, final_norm, final)
        else:
            xs = rec_layer(xs, mods(m_rec, i), rec_norm[i], rec_w_in[i], rwkv_mu[i], rwkv_w0[i], rwkv_w_up[i],
                           rwkv_a0[i], rwkv_a_up[i], rwkv_k_k[i], rwkv_k_a[i], rwkv_r_k[i], rwkv_gn_w[i], rwkv_gn_b[i],
                           hy_short[i], hy_w1[i], hy_b1[i], hy_w2[i], hy_b2[i], hy_w3[i], hy_b3[i], hy_skip[i],
                           rec_w_out[i], final_norm, final)
    return xs[None]


def _hy_short_body(fh_ref, fl_ref, gh_ref, gl_ref, v_ref, x1_ref, x2_ref, taps_ref, scale_ref, skip_ref, o_ref):
    fh, fl, gh, gl = fh_ref[...], fl_ref[...], gh_ref[...], gl_ref[...]
    kp = fh.shape[0] // 2
    hw = HY_TAPS_W // 2
    tf = _dot3c(fh, fl, taps_ref[...])
    kre = (tf[:kp, :hw] + tf[:kp, hw:]) * scale_ref[...]
    kim = (tf[kp:, :hw] - tf[kp:, hw:]) * scale_ref[...]
    z = v_ref[...]
    for o, gate_ref in enumerate((x1_ref, x2_ref)):
        ls = slice(o * HY_WIDTH, (o + 1) * HY_WIDTH)
        zf = _dot3c(fh, fl, z)
        zre, zim = zf[:kp], zf[kp:]
        y = jnp.concatenate([zre * kre[:, ls] - zim * kim[:, ls], zre * kim[:, ls] + zim * kre[:, ls]], axis=0)
        z = gate_ref[...] * (_dot3c(gh, gl, y) + z * skip_ref[o:o + 1])
    o_ref[...] = z


def hyena_short(hv, hx1, hx2, taps, scale, skip):
    length = hv.shape[0]
    n = 2 * length
    k1 = length + 1
    kp = -(-k1 // 8) * 8
    kk = np.arange(kp)[:, None].astype(np.float64)
    live = kk < k1
    th = 2.0 * np.pi * kk * np.arange(length)[None, :] / n
    f = _split_const(np.concatenate([np.cos(th) * live, -np.sin(th) * live], axis=0))
    ck = np.where((kk == 0) | (kk == length), 1.0, 2.0) * live / n
    g = _split_const(np.concatenate([np.cos(th) * ck, -np.sin(th) * ck], axis=0).T)
    return pl.pallas_call(
        _hy_short_body,
        out_shape=jax.ShapeDtypeStruct((length, HY_WIDTH), F32),
        compiler_params=pltpu.CompilerParams(vmem_limit_bytes=VMEM_LIMIT),
        name="hyena_short",
    )(f[0], f[1], g[0], g[1], hv, hx1, hx2, taps, scale, skip)
```

```python
import functools
import math

import jax
import jax.numpy as jnp
import numpy as np
from jax import lax
from jax.experimental import pallas as pl
from jax.experimental.pallas import tpu as pltpu

F32 = jnp.float32
BF16 = jnp.bfloat16
HIGHEST = lax.Precision.HIGHEST

D_MODEL = 1024
GRID_W = 64
CTX_LEN = 256
HEAD_DIM = 64
BRANCH_W = 512
N_HEADS = 8
GQA_KV_W = 128
NA_WIN_ROWS = 8
NA_WIN_COLS = 16
ROPE_THETA = 10000.0
ROPE_FREQS = 16
NORM_EPS = 1e-6
ROW_TILE = 256
NA_GROUP_ROWS = 4
NEG_BIG = -1e30
LOG2E = math.log2(math.e)
QK_SCALE = HEAD_DIM ** -0.5 * LOG2E
VMEM_LIMIT = 56 * 1024 * 1024

ATTN_SPLITS = (512, 512, 512, 512, 512, 128, 128, 512)
GQA_HEAD_ORDER = (0, 4, 1, 5, 2, 6, 3, 7)


def _cparams(sem):
    return pltpu.CompilerParams(dimension_semantics=sem, vmem_limit_bytes=VMEM_LIMIT)


def _silu(v):
    return v * (1.0 / (1.0 + jnp.exp(-v)))


def _lane_half(shape):
    return (lax.broadcasted_iota(jnp.int32, shape, len(shape) - 1) // HEAD_DIM) % 2


def _dot_nt(a, b):
    return lax.dot_general(a, b, (((1,), (1,)), ((), ())), preferred_element_type=F32)


def _adaln_body(cond_ref, w_ref, b_ref, o_ref):
    s = _silu(cond_ref[...])
    o_ref[0] = jnp.dot(s, w_ref[0], precision=HIGHEST, preferred_element_type=F32) + b_ref[0]


def adaln_all(cond8, ada_w, ada_b):
    nl = ada_w.shape[0]
    d = D_MODEL
    return pl.pallas_call(
        _adaln_body,
        out_shape=jax.ShapeDtypeStruct((nl, 8, 3 * d), F32),
        grid=(nl, 3),
        in_specs=[
            pl.BlockSpec((8, d), lambda l, j: (0, 0)),
            pl.BlockSpec((1, d, d), lambda l, j: (l, 0, j)),
            pl.BlockSpec((1, 1, d), lambda l, j: (l, 0, j)),
        ],
        out_specs=pl.BlockSpec((1, 8, d), lambda l, j: (l, 0, j)),
        compiler_params=_cparams(("parallel", "parallel")),
        name="adaln",
    )(cond8, ada_w, ada_b.reshape(nl, 1, 3 * d))


def _modulated(x_ref, g_ref, scale_ref, shift_ref):
    xf = x_ref[...]
    y = xf * lax.rsqrt(jnp.mean(xf * xf, axis=-1, keepdims=True) + NORM_EPS)
    return (y * g_ref[...]) * (1.0 + scale_ref[0]) + shift_ref[0]


def _mod_specs():
    d = D_MODEL
    return [
        pl.BlockSpec((ROW_TILE, d), lambda i: (i, 0)),
        pl.BlockSpec((1, d), lambda i: (0, 0)),
        pl.BlockSpec((1, 1, d), lambda i: (jnp.minimum(i, 1), 0, 0)),
        pl.BlockSpec((1, 1, d), lambda i: (jnp.minimum(i, 1), 0, 0)),
    ]


def _attn_in_body(x_ref, g_ref, scale_ref, shift_ref, w_ref, cos_ref, sin_ref, gq_ref, gqs_ref, gk_ref, gks_ref,
                  bdq_ref, bdk_ref,
                  qa_ref, ka_ref, va_ref, sga_ref, qb_ref, kb_ref, vb_ref, sgb_ref):
    xm = _modulated(x_ref, g_ref, scale_ref, shift_ref).astype(BF16)
    u = jnp.dot(xm, w_ref[...], preferred_element_type=F32)
    qa, ka, va, ga = u[:, 0:512], u[:, 512:1024], u[:, 1024:1536], u[:, 1536:2048]
    qb, kb, vb, gb = u[:, 2048:2560], u[:, 2560:2688], u[:, 2688:2816], u[:, 2816:3328]
    qbs, kbs = u[:, 3328:3840], u[:, 3840:3968]
    scale = QK_SCALE
    qa_ref[...] = (qa * scale).astype(BF16)
    ka_ref[...] = ka.astype(BF16)
    va_ref[...] = va.astype(BF16)
    sga_ref[...] = _silu(ga)
    sgb_ref[...] = _silu(gb)
    vb_ref[...] = vb.astype(BF16)
    cos_k, sin_k = cos_ref[...], sin_ref[...]
    cos_q = jnp.concatenate([cos_k] * 4, axis=1)
    sin_q = jnp.concatenate([sin_k] * 4, axis=1)
    rs_q = lax.rsqrt(jnp.dot(qb * qb, bdq_ref[...], precision=HIGHEST, preferred_element_type=F32) + NORM_EPS)
    rs_k = lax.rsqrt(jnp.dot(kb * kb, bdk_ref[...], precision=HIGHEST, preferred_element_type=F32) + NORM_EPS)
    qr = rs_q * (qb * gq_ref[...] * cos_q + qbs * gqs_ref[...] * sin_q)
    kr = rs_k * (kb * gk_ref[...] * cos_k + kbs * gks_ref[...] * sin_k)
    qb_ref[...] = (qr * scale).astype(BF16)
    kb_ref[...] = kr.astype(BF16)


def _rope_tables(n):
    t = jnp.arange(n, dtype=jnp.int32)
    pos = jnp.stack([t // GRID_W, t % GRID_W], axis=-1).astype(F32)
    inv_freq = ROPE_THETA ** (-jnp.arange(ROPE_FREQS, dtype=F32) / ROPE_FREQS)
    ang = pos[:, :, None] * inv_freq
    c, s = jnp.cos(ang), jnp.sin(ang)
    cos64 = jnp.concatenate([c[:, 0], c[:, 0], c[:, 1], c[:, 1]], axis=-1)
    sin64 = jnp.concatenate([-s[:, 0], s[:, 0], -s[:, 1], s[:, 1]], axis=-1)
    cos64 = jnp.concatenate([jnp.ones((CTX_LEN, HEAD_DIM), F32), cos64], axis=0)
    sin64 = jnp.concatenate([jnp.zeros((CTX_LEN, HEAD_DIM), F32), sin64], axis=0)
    return jnp.tile(cos64, (1, 2)), jnp.tile(sin64, (1, 2))


def _reorder_heads(w, order, axis):
    take = lambda h: lax.slice_in_dim(w, h * HEAD_DIM, (h + 1) * HEAD_DIM, axis=axis)
    return jnp.concatenate([take(h) for h in order], axis=axis)


def _swap_rope_halves(w):
    shp = w.shape
    return jnp.flip(w.reshape(shp[:-1] + (shp[-1] // (2 * ROPE_FREQS), 2, ROPE_FREQS)), axis=-2).reshape(shp)


def attn_in_proj(xs, norm_g, scale2, shift2, w_in, q_gain, k_gain, cos_t, sin_t):
    r = xs.shape[0]
    d = D_MODEL
    parts, start = [], 0
    for s in ATTN_SPLITS:
        parts.append(w_in[:, start:start + s])
        start += s
    wqa, wka, wva, wga, wqb, wkb, wvb, wgb = parts
    wqb_p = _reorder_heads(wqb, GQA_HEAD_ORDER, 1)
    wgb_p = _reorder_heads(wgb, GQA_HEAD_ORDER, 1)
    wqb_sw = _swap_rope_halves(wqb_p)
    wkb_sw = _swap_rope_halves(wkb)
    w_ext = jnp.concatenate([wqa, wka, wva, wga, wqb_p, wkb, wvb, wgb_p, wqb_sw, wkb_sw], axis=1).astype(BF16)
    gq = jnp.tile(q_gain, N_HEADS)[None]
    gqs = jnp.tile(_swap_rope_halves(q_gain), N_HEADS)[None]
    gk = jnp.tile(k_gain, 2)[None]
    gks = jnp.tile(_swap_rope_halves(k_gain), 2)[None]
    bdq = jnp.asarray(np.kron(np.eye(N_HEADS), np.full((HEAD_DIM, HEAD_DIM), 1.0 / HEAD_DIM)), F32)
    bdk = jnp.asarray(np.kron(np.eye(2), np.full((HEAD_DIM, HEAD_DIM), 1.0 / HEAD_DIM)), F32)
    wcols = w_ext.shape[1]
    const = lambda shp: pl.BlockSpec(shp, lambda i: (0,) * len(shp))
    rows = lambda w: pl.BlockSpec((ROW_TILE, w), lambda i: (i, 0))
    out_shapes = [
        jax.ShapeDtypeStruct((r, 512), BF16), jax.ShapeDtypeStruct((r, 512), BF16), jax.ShapeDtypeStruct((r, 512), BF16),
        jax.ShapeDtypeStruct((r, 512), F32),
        jax.ShapeDtypeStruct((r, 512), BF16), jax.ShapeDtypeStruct((r, 128), BF16), jax.ShapeDtypeStruct((r, 128), BF16),
        jax.ShapeDtypeStruct((r, 512), F32),
    ]
    return pl.pallas_call(
        _attn_in_body,
        out_shape=out_shapes,
        grid=(r // ROW_TILE,),
        in_specs=_mod_specs() + [const((d, wcols)), rows(128), rows(128), const((1, 512)), const((1, 512)),
                                 const((1, 128)), const((1, 128)), const((512, 512)), const((128, 128))],
        out_specs=[rows(512), rows(512), rows(512), rows(512), rows(512), rows(128), rows(128), rows(512)],
        compiler_params=_cparams(("parallel",)),
        name="attn_in_proj",
    )(xs, norm_g[None], scale2, shift2, w_ext, cos_t, sin_t, gq, gqs, gk, gks, bdq, bdk)


def _na_cols_body(rpb_ref, sel_ref, neg_ref, o_ref):
    o_ref[...] = jnp.dot(rpb_ref[...], sel_ref[...], precision=HIGHEST, preferred_element_type=F32) + neg_ref[...]


def _na_bias_tables(rpb, rows):
    nrel_r, nrel_c = 2 * NA_WIN_ROWS - 1, 2 * NA_WIN_COLS - 1
    qc = np.arange(GRID_W)[:, None]
    kc = np.arange(GRID_W)[None, :]
    col0 = np.clip(qc - NA_WIN_COLS // 2, 0, GRID_W - NA_WIN_COLS)
    col_ok = (kc >= col0) & (kc < col0 + NA_WIN_COLS)
    rc = kc - qc + NA_WIN_COLS - 1
    sel = np.zeros((128, GRID_W * GRID_W), np.float32)
    sel[np.where(col_ok, rc, 127).reshape(-1), np.arange(GRID_W * GRID_W)] = col_ok.reshape(-1)
    neg = np.where(col_ok, 0.0, NEG_BIG).astype(np.float32).reshape(1, -1)
    rpb2 = jnp.zeros((128, 128), F32).at[:N_HEADS * nrel_r, :nrel_c].set(rpb.reshape(N_HEADS * nrel_r, nrel_c))
    cols = pl.pallas_call(
        _na_cols_body,
        out_shape=jax.ShapeDtypeStruct((128, GRID_W * GRID_W), F32),
        name="na_bias_cols",
    )(rpb2, jnp.asarray(sel), jnp.asarray(neg))
    cols = cols[:N_HEADS * nrel_r].reshape(N_HEADS, nrel_r, GRID_W, GRID_W)
    kh = min(NA_WIN_ROWS, rows)
    g = rows // NA_GROUP_ROWS
    cases = [(0, 0), (NA_GROUP_ROWS, 0), (rows - NA_GROUP_ROWS, NA_GROUP_ROWS * (g - 3))]
    masked = jnp.full((N_HEADS, GRID_W, GRID_W), NEG_BIG, F32)
    tabs = []
    for qr_first, start in cases:
        blocks = []
        for j in range(NA_GROUP_ROWS):
            qr = qr_first + j
            row0 = min(max(qr - kh // 2, 0), rows - kh)
            for i in range(3 * NA_GROUP_ROWS):
                kr = start + i
                blocks.append(cols[:, kr - qr + NA_WIN_ROWS - 1] if row0 <= kr < row0 + kh else masked)
        tab = jnp.stack(blocks, axis=1).reshape(N_HEADS, NA_GROUP_ROWS, 3 * NA_GROUP_ROWS, GRID_W, GRID_W)
        tabs.append(tab.transpose(0, 1, 3, 2, 4).reshape(N_HEADS, ROW_TILE, 3 * ROW_TILE))
    return jnp.stack(tabs) * LOG2E


def _na_body(q_ref, kc_ref, k0_ref, k1_ref, k2_ref, vc_ref, v0_ref, v1_ref, v2_ref, bias_ref, sg_ref, o_ref):
    half = _lane_half((ROW_TILE, 128))
    for hp in range(N_HEADS // 2):
        ls = slice(hp * 128, (hp + 1) * 128)
        qp = q_ref[:, ls]
        ks = [r[:, ls] for r in (k0_ref, k1_ref, k2_ref, kc_ref)]
        vs = [r[:, ls] for r in (v0_ref, v1_ref, v2_ref, vc_ref)]
        outs = []
        for j in range(2):
            qm = jnp.where(half == j, qp, jnp.zeros_like(qp))
            s = [_dot_nt(qm, k) for k in ks]
            s_win = jnp.concatenate(s[:3], axis=1) + bias_ref[0, 2 * hp + j]
            s_ctx = s[3]
            m = jnp.maximum(jnp.max(s_win, axis=1, keepdims=True), jnp.max(s_ctx, axis=1, keepdims=True))
            p_win = jnp.exp2(s_win - m)
            p_ctx = jnp.exp2(s_ctx - m)
            l = jnp.sum(p_win, axis=1, keepdims=True) + jnp.sum(p_ctx, axis=1, keepdims=True)
            o = jnp.dot(p_ctx.astype(BF16), vs[3], preferred_element_type=F32)
            for b in range(3):
                o += jnp.dot(p_win[:, b * ROW_TILE:(b + 1) * ROW_TILE].astype(BF16), vs[b], preferred_element_type=F32)
            outs.append(o / l)
        o_pair = jnp.where(half == 0, outs[0], outs[1])
        o_ref[:, ls] = (o_pair * sg_ref[:, ls]).astype(BF16)


def na_attention(qa, ka, va, sga, bias_tabs, n):
    g = n // ROW_TILE
    w = BRANCH_W

    def kv_spec(off):
        return pl.BlockSpec((ROW_TILE, w), lambda i: (jnp.clip(i - 1, 0, g - 3) + off + 1, 0))

    ctx_spec = pl.BlockSpec((ROW_TILE, w), lambda i: (0, 0))
    q_spec = pl.BlockSpec((ROW_TILE, w), lambda i: (i + 1, 0))
    case = lambda i: jnp.where(i == 0, 0, jnp.where(i == g - 1, 2, 1))
    bias_spec = pl.BlockSpec((1, N_HEADS, ROW_TILE, 3 * ROW_TILE), lambda i: (case(i), 0, 0, 0))
    return pl.pallas_call(
        _na_body,
        out_shape=jax.ShapeDtypeStruct((n, w), BF16),
        grid=(g,),
        in_specs=[q_spec, ctx_spec, kv_spec(0), kv_spec(1), kv_spec(2), ctx_spec, kv_spec(0), kv_spec(1), kv_spec(2),
                  bias_spec, q_spec],
        out_specs=pl.BlockSpec((ROW_TILE, w), lambda i: (i, 0)),
        compiler_params=_cparams(("parallel",)),
        name="na_attention",
    )(qa, ka, ka, ka, ka, va, va, va, va, bias_tabs, sga)


def _flash_body(q_ref, k_ref, v_ref, sg_ref, o_ref, m_ref, acc_ref, *, mha):
    kv = pl.program_id(1)
    tq = q_ref.shape[0]

    @pl.when(kv == 0)
    def _():
        m_ref[...] = jnp.full(m_ref.shape, NEG_BIG, F32)
        acc_ref[...] = jnp.zeros(acc_ref.shape, F32)

    khalf = _lane_half((k_ref.shape[0], 128))
    for p in range(N_HEADS // 2):
        ls = slice(p * 128, (p + 1) * 128)
        kls = ls if mha else slice(0, 128)
        qp = q_ref[:, ls]
        kp = k_ref[:, kls]
        vp = v_ref[:, kls]
        for j in range(2):
            hh = 2 * p + j
            km = jnp.where(khalf == j, kp, jnp.zeros_like(kp))
            vm = jnp.where(khalf == j, vp, jnp.ones_like(vp))
            s = _dot_nt(qp, km)
            m_prev = m_ref[hh]
            m_new = jnp.maximum(m_prev, jnp.max(s, axis=1, keepdims=True))
            alpha = jnp.exp2(m_prev - m_new)
            pr = jnp.exp2(s - m_new[:, :1]).astype(BF16)
            acc_ref[hh] = alpha * acc_ref[hh] + jnp.dot(pr, vm, preferred_element_type=F32)
            m_ref[hh] = m_new

    @pl.when(kv == pl.num_programs(1) - 1)
    def _():
        half = _lane_half((tq, 128))
        for p in range(N_HEADS // 2):
            ls = slice(p * 128, (p + 1) * 128)
            a0, a1 = acc_ref[2 * p], acc_ref[2 * p + 1]
            o0 = a0 / pltpu.roll(a0, HEAD_DIM, 1)
            o1 = a1 / pltpu.roll(a1, HEAD_DIM, 1)
            o_ref[:, ls] = (jnp.where(half == 0, o0, o1) * sg_ref[:, ls]).astype(BF16)


def flash_attention(q, k, v, sg, *, q_block0, nq, tk, nk, mha):
    kw = k.shape[1]
    tq = ROW_TILE
    return pl.pallas_call(
        functools.partial(_flash_body, mha=mha),
        out_shape=jax.ShapeDtypeStruct((nq * tq, BRANCH_W), BF16),
        grid=(nq, nk),
        in_specs=[
            pl.BlockSpec((tq, BRANCH_W), lambda i, j: (i + q_block0, 0)),
            pl.BlockSpec((tk, kw), lambda i, j: (j, 0)),
            pl.BlockSpec((tk, kw), lambda i, j: (j, 0)),
            pl.BlockSpec((tq, BRANCH_W), lambda i, j: (i + q_block0, 0)),
        ],
        out_specs=pl.BlockSpec((tq, BRANCH_W), lambda i, j: (i, 0)),
        scratch_shapes=[pltpu.VMEM((N_HEADS, tq, 128), F32)] * 2,
        compiler_params=_cparams(("parallel", "arbitrary")),
        name="flash_mha" if mha else "flash_gqa",
    )(q, k, v, sg)


def _out_body(x_ref, ya_ref, yb_ref, wa_ref, wb_ref, gate_ref, fin_ref, o_ref, *, final):
    y = jnp.dot(ya_ref[...], wa_ref[...], preferred_element_type=F32)
    y += jnp.dot(yb_ref[...], wb_ref[...], preferred_element_type=F32)
    xn = x_ref[...] + gate_ref[0] * y
    if final:
        xn = xn * lax.rsqrt(jnp.mean(xn * xn, axis=-1, keepdims=True) + NORM_EPS) * fin_ref[...]
    o_ref[...] = xn


def out_proj(xs, ya, yb, wa, wb, gate2, final_g, *, final):
    r = xs.shape[0]
    d = D_MODEL
    rows = lambda w: pl.BlockSpec((ROW_TILE, w), lambda i: (i, 0))
    const = lambda shp: pl.BlockSpec(shp, lambda i: (0,) * len(shp))
    return pl.pallas_call(
        functools.partial(_out_body, final=final),
        out_shape=jax.ShapeDtypeStruct((r, d), F32),
        grid=(r // ROW_TILE,),
        in_specs=[rows(d), rows(BRANCH_W), rows(BRANCH_W), const((BRANCH_W, d)), const((BRANCH_W, d)),
                  pl.BlockSpec((1, 1, d), lambda i: (jnp.minimum(i, 1), 0, 0)), const((1, d))],
        out_specs=rows(d),
        compiler_params=_cparams(("parallel",)),
        name="out_proj",
    )(xs, ya, yb, wa.astype(BF16), wb.astype(BF16), gate2, final_g[None])


def _kv_tile(r):
    nb = r // ROW_TILE
    best = max(k for k in range(1, 9) if nb % k == 0)
    return best * ROW_TILE, nb // best


def attn_layer(xs, mods, norm_g, w_in, rpb, q_gain, k_gain, w_out, cos_t, sin_t, final_g, final):
    r = xs.shape[0]
    n = r - CTX_LEN
    shift2, scale2, gate2 = mods
    qa, ka, va, sga, qb, kb, vb, sgb = attn_in_proj(xs, norm_g, scale2, shift2, w_in, q_gain, k_gain, cos_t, sin_t)
    bias_tabs = _na_bias_tables(rpb, n // GRID_W)
    ya_lat = na_attention(qa, ka, va, sga, bias_tabs, n)
    ya_ctx = flash_attention(qa, ka, va, sga, q_block0=0, nq=1, tk=CTX_LEN, nk=1, mha=True)
    tk, nk = _kv_tile(r)
    yb_lat = flash_attention(qb, kb, vb, sgb, q_block0=1, nq=n // ROW_TILE, tk=tk, nk=nk, mha=False)
    yb_ctx = flash_attention(qb, kb, vb, sgb, q_block0=0, nq=1, tk=CTX_LEN, nk=1, mha=False)
    ya = jnp.concatenate([ya_ctx, ya_lat], axis=0)
    yb = jnp.concatenate([yb_ctx, yb_lat], axis=0)
    wb = _reorder_heads(w_out[BRANCH_W:], GQA_HEAD_ORDER, 0)
    return out_proj(xs, ya, yb, w_out[:BRANCH_W], wb, gate2, final_g, final=final)


def _split(a):
    hi = a.astype(BF16)
    return hi, (a - hi.astype(F32)).astype(BF16)


def _dot3(a, b, dims=(((1,), (0,)), ((), ()))):
    ah, al = _split(a)
    bh, bl = _split(b)
    dg = functools.partial(lax.dot_general, dimension_numbers=dims, preferred_element_type=F32)
    return dg(ah, bh) + (dg(al, bh) + dg(ah, bl))


def _dot1(a, b, dims=(((1,), (0,)), ((), ()))):
    return lax.dot_general(a.astype(BF16), b.astype(BF16), dims, preferred_element_type=F32)


_NT = (((1,), (1,)), ((), ()))
_TN = (((0,), (0,)), ((), ()))


RWKV_SHIFT_W = 1664
HY_IN_W = 1536
HALO = 8
REC_HALO_W = RWKV_SHIFT_W + HY_IN_W


def _rec_in_body(x_ref, xp_ref, xn_ref, g_ref, scale_ref, shift_ref, w_ref, mu_ref, taps_ref,
                 rw_ref, hv_ref, hx1_ref, hx2_ref, sgr_ref, sgh_ref, u_scr):
    i = pl.program_id(0)
    nt = pl.num_programs(0)
    xe = jnp.concatenate([xp_ref[...], x_ref[...], xn_ref[...]], axis=0)
    y = xe * lax.rsqrt(jnp.mean(xe * xe, axis=-1, keepdims=True) + NORM_EPS)
    xm = ((y * g_ref[...]) * (1.0 + scale_ref[0]) + shift_ref[0]).astype(BF16)
    u = jnp.dot(xm, w_ref[...], preferred_element_type=F32)
    row = lax.broadcasted_iota(jnp.int32, (ROW_TILE + 2 * HALO, 1), 0)
    keep = jnp.logical_and(jnp.logical_or(row >= HALO, i >= 2),
                           jnp.logical_or(row < ROW_TILE + HALO, jnp.logical_and(i >= 1, i < nt - 1)))
    u_scr[...] = jnp.where(keep, u[:, :REC_HALO_W], 0.0)
    up = u_scr[pl.ds(HALO - 1, ROW_TILE), :]
    uc = u_scr[pl.ds(HALO, ROW_TILE), :]
    un = u_scr[pl.ds(HALO + 1, ROW_TILE), :]
    w = RWKV_SHIFT_W
    rw_c = uc[:, :w]
    rw_ref[...] = rw_c + (0.5 * (up[:, :w] + un[:, :w]) - rw_c) * mu_ref[...]
    hy = up[:, w:] * taps_ref[0:1] + uc[:, w:] * taps_ref[1:2] + un[:, w:] * taps_ref[2:3]
    hv_ref[...] = hy[:, 0:512]
    hx1_ref[...] = hy[:, 512:1024]
    hx2_ref[...] = hy[:, 1024:1536]
    uc_all = u[HALO:HALO + ROW_TILE]
    sgr_ref[...] = _silu(uc_all[:, REC_HALO_W:REC_HALO_W + 512])
    sgh_ref[...] = _silu(uc_all[:, REC_HALO_W + 512:REC_HALO_W + 1024])


def rec_in_proj(xs, norm_g, scale2, shift2, w_in, mu, hy_short):
    r = xs.shape[0]
    d = D_MODEL
    w = RWKV_SHIFT_W
    w_ext = jnp.concatenate([w_in[:, :w], w_in[:, w + 512:w + 512 + HY_IN_W], w_in[:, w:w + 512],
                             w_in[:, w + 512 + HY_IN_W:]], axis=1).astype(BF16)
    nh = r // HALO
    per = ROW_TILE // HALO
    const = lambda shp: pl.BlockSpec(shp, lambda i: (0,) * len(shp))
    rows = lambda wd: pl.BlockSpec((ROW_TILE, wd), lambda i: (i, 0))
    f = lambda wd: jax.ShapeDtypeStruct((r, wd), F32)
    mod = _mod_specs()
    return pl.pallas_call(
        _rec_in_body,
        out_shape=[f(w), f(512), f(512), f(512), f(512), f(512)],
        grid=(r // ROW_TILE,),
        in_specs=[mod[0],
                  pl.BlockSpec((HALO, d), lambda i: (jnp.maximum(i * per - 1, 0), 0)),
                  pl.BlockSpec((HALO, d), lambda i: (jnp.minimum((i + 1) * per, nh - 1), 0)),
                  mod[1], mod[2], mod[3], const((d, w_ext.shape[1])), const((1, w)), const((3, HY_IN_W))],
        out_specs=[rows(w), rows(512), rows(512), rows(512), rows(512), rows(512)],
        scratch_shapes=[pltpu.VMEM((ROW_TILE + 2 * HALO, REC_HALO_W), F32)],
        compiler_params=_cparams(("parallel",)),
        name="rec_in_proj",
    )(xs, xs, xs, norm_g[None], scale2, shift2, w_ext, mu[None], hy_short)


CHUNK = 64
CPT = ROW_TILE // CHUNK


def _block_sum_mat(width, value):
    return jnp.asarray(np.kron(np.eye(width // HEAD_DIM), np.full((HEAD_DIM, HEAD_DIM), value)), F32)


def _rwkv_prep_body(r_ref, k_ref, v_ref, lora_ref, w0_ref, wup_ref, a0_ref, aup_ref, kk_ref, ka_ref, rk_ref,
                    tri_ref, bs_ref, g_ref, add_ref, bonus_ref):
    d = pl.program_id(2)
    t = ROW_TILE
    r, k, v, lora = r_ref[...], k_ref[...], v_ref[...], lora_ref[...]
    bs = bs_ref[...]
    tri = tri_ref[0]
    kk = k * kk_ref[...]
    kk = kk * lax.rsqrt(_dot3(kk * kk, bs) + 1e-12)
    wl = w0_ref[0] + _dot3(jnp.tanh(lora), wup_ref[0])
    z = -wl
    w_log = -(jnp.maximum(z, 0.0) + jnp.log(1.0 + jnp.exp(-jnp.abs(z)))) - 0.5
    lw = -jnp.exp(w_log)
    a = 1.0 / (1.0 + jnp.exp(-(a0_ref[0] + _dot3(lora, aup_ref[0]))))
    kd = k * (1.0 + (a - 1.0) * ka_ref[...])
    b = kk * a

    bon = 0.5 * _dot3(r * kd * rk_ref[...], bs) * v

    @pl.when(d == 0)
    def _():
        bonus_ref[...] = bon

    @pl.when(d == 1)
    def _():
        bonus_ref[...] += bon

    row = lax.broadcasted_iota(jnp.int32, (t, t), 0)
    col = lax.broadcasted_iota(jnp.int32, (t, t), 1)
    incl = tri > 0.5
    strict = jnp.logical_and(incl, row != col)
    same = (row // CHUNK) == (col // CHUNK)
    cs = _dot3(tri, lw)
    tot = _dot3(same.astype(F32), lw)
    w_incl = jnp.exp(cs)
    w_inv = jnp.exp(-cs)
    kkt = kk * jnp.exp(cs - lw)
    kh = kd * w_inv
    bh = b * w_inv
    rt = r * w_incl
    w_rest = jnp.exp(tot - cs)
    kdd = kd * w_rest
    bdd = b * w_rest
    half = _lane_half((t, 128))
    eye = (row == col).astype(F32)

    per_half = []
    for j in range(2):
        sel = half == j
        bm = jnp.where(sel, bh, 0.0)
        km = jnp.where(sel, kh, 0.0)
        l_b = jnp.where(strict, _dot3(kkt, bm, _NT), 0.0)
        l_k = jnp.where(strict, _dot1(kkt, km, _NT), 0.0)
        a_rk = jnp.where(incl, _dot1(rt, km, _NT), 0.0)
        a_rb = jnp.where(incl, _dot1(rt, bm, _NT), 0.0)
        tinv = eye - jnp.where((row // 2) == (col // 2), l_b, 0.0)
        bsz = 2
        while bsz < CHUNK:
            off = jnp.where(jnp.logical_and((row // (2 * bsz)) == (col // (2 * bsz)), (row // bsz) != (col // bsz)),
                            l_b, 0.0)
            tinv = tinv - _dot1(tinv, _dot1(off, tinv))
            bsz *= 2
        resid = eye - (tinv + _dot3(l_b, tinv))
        tinv = tinv + _dot1(tinv, resid)
        p = _dot1(tinv, kkt)
        u0 = _dot1(tinv, _dot1(l_k, v))
        q = rt - _dot1(a_rb, p)
        y0 = _dot1(a_rk, v) - _dot1(a_rb, u0)
        per_half.append((p, u0, q, y0))
    sel0 = half == 0
    p, u0, q, y0 = (jnp.where(sel0, x0, x1) for x0, x1 in zip(*per_half))

    half_c = _lane_half((HEAD_DIM, 128))
    rowc = lax.broadcasted_iota(jnp.int32, (HEAD_DIM, 128), 0)
    lanec = lax.broadcasted_iota(jnp.int32, (HEAD_DIM, 128), 1)
    for c in range(CPT):
        rs = slice(c * CHUNK, (c + 1) * CHUNK)
        x1 = _dot1(bdd[rs], p[rs], _TN)
        x2 = _dot1(kdd[rs], v[rs], _TN) - _dot1(bdd[rs], u0[rs], _TN)
        m_pair = jnp.where(half_c == 0, x1[:HEAD_DIM], x1[HEAD_DIM:])
        n_pair = jnp.where(half_c == 0, x2[:HEAD_DIM], x2[HEAD_DIM:])
        wc = jnp.exp(tot[c * CHUNK:c * CHUNK + 1])
        dg = jnp.where((lanec % HEAD_DIM) == rowc, wc, 0.0)
        g_ref[c, 0, 0:HEAD_DIM, :] = dg - m_pair
        g_ref[c, 0, HEAD_DIM:, :] = q[rs]
        add_ref[c, 0, 0:HEAD_DIM, :] = n_pair
        add_ref[c, 0, HEAD_DIM:, :] = y0[rs]


def _lora_ext(up, first_row):
    out = jnp.zeros((2, 128, BRANCH_W), F32)
    for d in range(2):
        out = out.at[d, first_row + 32 * d:first_row + 32 * (d + 1)].set(up[d])
    return out


def rwkv_prep(rw, w0, w_up, a0, a_up, k_k, k_a, r_k):
    r = rw.shape[0]
    nt = r // ROW_TILE
    nch = r // CHUNK
    t = ROW_TILE
    ii = np.arange(t)
    same = (ii[:, None] // CHUNK) == (ii[None, :] // CHUNK)
    tri = jnp.asarray(np.stack([same & (ii[None, :] <= ii[:, None]), same & (ii[None, :] >= ii[:, None])]), F32)
    lane = lambda blk: pl.BlockSpec((t, 128), lambda i, p, d, blk=blk: (i, blk + p))
    pvec = pl.BlockSpec((1, 128), lambda i, p, d: (0, p))
    dvec = pl.BlockSpec((1, 1, 128), lambda i, p, d: (d, 0, p))
    dmat = pl.BlockSpec((1, 128, 128), lambda i, p, d: (d, 0, p))
    gspec = pl.BlockSpec((CPT, 1, HEAD_DIM + CHUNK, 128), lambda i, p, d: (i, d, 0, p))
    gshape = jax.ShapeDtypeStruct((nch, 2, HEAD_DIM + CHUNK, BRANCH_W), F32)
    return pl.pallas_call(
        _rwkv_prep_body,
        out_shape=[gshape, gshape, jax.ShapeDtypeStruct((r, BRANCH_W), F32)],
        grid=(nt, N_HEADS // 2, 2),
        in_specs=[lane(0), lane(4), lane(8), pl.BlockSpec((t, 128), lambda i, p, d: (i, 12)),
                  dvec, dmat, dvec, dmat, pvec, pvec, pvec,
                  pl.BlockSpec((1, t, t), lambda i, p, d: (d, 0, 0)),
                  pl.BlockSpec((128, 128), lambda i, p, d: (0, 0))],
        out_specs=[gspec, gspec, pl.BlockSpec((t, 128), lambda i, p, d: (i, p))],
        compiler_params=_cparams(("parallel", "parallel", "arbitrary")),
        name="rwkv_prep",
    )(rw, rw, rw, rw, w0.reshape(2, 1, BRANCH_W), _lora_ext(w_up, 0), a0.reshape(2, 1, BRANCH_W),
      _lora_ext(a_up, 64), k_k[None], k_a[None], r_k.reshape(1, BRANCH_W), tri, _block_sum_mat(128, 1.0))


def _rwkv_scan_body(gf_ref, af_ref, gb_ref, ab_ref, yf_ref, yb_ref, st_ref):
    @pl.when(pl.program_id(0) == 0)
    def _():
        st_ref[...] = jnp.zeros(st_ref.shape, F32)

    rowh = lax.broadcasted_iota(jnp.int32, (128, 128), 0) // HEAD_DIM
    diag = rowh == _lane_half((128, 128))
    for d, (g_ref, a_ref, y_ref) in enumerate(((gf_ref, af_ref, yf_ref), (gb_ref, ab_ref, yb_ref))):
        for p in range(N_HEADS // 2):
            ls = slice(p * 128, (p + 1) * 128)
            out = _dot3(g_ref[0, 0, :, ls], st_ref[d, p]) + a_ref[0, 0, :, ls]
            hn = out[:HEAD_DIM]
            st_ref[d, p] = jnp.where(diag, jnp.concatenate([hn, hn], axis=0), 0.0)
            y_ref[:, ls] = out[HEAD_DIM:]


def rwkv_scan(g, add):
    nch = g.shape[0]
    r = nch * CHUNK
    nctx = CTX_LEN // CHUNK
    rev = lambda c: jnp.where(c < nctx, nctx - 1 - c, nch + nctx - 1 - c)
    blk = (1, 1, HEAD_DIM + CHUNK, BRANCH_W)
    fwd = pl.BlockSpec(blk, lambda c: (c, 0, 0, 0))
    bwd = pl.BlockSpec(blk, lambda c: (rev(c), 1, 0, 0))
    yshape = jax.ShapeDtypeStruct((r, BRANCH_W), F32)
    return pl.pallas_call(
        _rwkv_scan_body,
        out_shape=[yshape, yshape],
        grid=(nch,),
        in_specs=[fwd, fwd, bwd, bwd],
        out_specs=[pl.BlockSpec((CHUNK, BRANCH_W), lambda c: (c, 0)),
                   pl.BlockSpec((CHUNK, BRANCH_W), lambda c: (rev(c), 0))],
        scratch_shapes=[pltpu.VMEM((2, N_HEADS // 2, 128, 128), F32)],
        compiler_params=_cparams(("arbitrary",)),
        name="rwkv_scan",
    )(g, add, g, add)


HY_WIDTH = 512
HY_ORDER = 2
HY_POS_BANDS = 16
HY_HIDDEN = 64
HY_TAPS_W = 2 * HY_ORDER * HY_WIDTH
FFT_N2 = ROW_TILE


def _dot3c(ah, al, b):
    bh, bl = _split(b)
    dg = functools.partial(jnp.dot, preferred_element_type=F32)
    return dg(ah, bh) + (dg(al, bh) + dg(ah, bl))


def _split_const(m):
    m = np.asarray(m, np.float32)
    hi = m.astype(BF16)
    lo = (m - hi.astype(np.float32)).astype(BF16)
    return jnp.asarray(hi), jnp.asarray(lo)


def _hy_taps_body(c2pb_ref, w1t_ref, w1c_ref, w1s_ref, b1_ref, w2_ref, b2_ref, w3_ref, b3_ref, absd_ref,
                  taps_ref, ssq_ref, *, length):
    i = pl.program_id(0)
    t_idx = (i * ROW_TILE + lax.broadcasted_iota(jnp.int32, (ROW_TILE, 1), 0)).astype(F32)
    t = t_idx / float(max(length - 1, 1))
    ang = c2pb_ref[...] * t_idx / float(length)
    pre = t * w1t_ref[...] + _dot3(jnp.cos(ang), w1c_ref[...]) - _dot3(jnp.sin(ang), w1s_ref[...]) + b1_ref[...]
    hid = jnp.sin(pre)
    hid = jnp.sin(_dot3(hid, w2_ref[...]) + b2_ref[...])
    taps = (_dot3(hid, w3_ref[...]) + b3_ref[...]) * jnp.exp(-t * absd_ref[...])
    taps_ref[...] = taps

    @pl.when(i == 0)
    def _():
        ssq_ref[...] = jnp.zeros(ssq_ref.shape, F32)

    ssq_ref[...] += jnp.sum(taps * taps, axis=0, keepdims=True)


def hyena_taps(length, w1, b1, w2, b2, w3, b3):
    bands = jnp.linspace(1e-4, HY_POS_BANDS - 1, HY_POS_BANDS, dtype=F32)
    c2pb = jnp.zeros((1, 128), F32).at[0, :HY_POS_BANDS].set(2.0 * math.pi * bands)
    pad = lambda m: jnp.zeros((128, HY_HIDDEN), F32).at[:HY_POS_BANDS].set(m)
    deltas = jnp.linspace(math.log(1e-2) / 0.3, math.log(1e-2) / 1.5, HY_WIDTH, dtype=F32)
    absd = jnp.tile(jnp.abs(deltas), 2 * HY_ORDER)[None]
    const = lambda shp: pl.BlockSpec(shp, lambda i: (0,) * len(shp))
    h = HY_HIDDEN
    taps, ssq = pl.pallas_call(
        functools.partial(_hy_taps_body, length=length),
        out_shape=[jax.ShapeDtypeStruct((length, HY_TAPS_W), F32), jax.ShapeDtypeStruct((1, HY_TAPS_W), F32)],
        grid=(length // ROW_TILE,),
        in_specs=[const((1, 128)), const((1, h)), const((128, h)), const((128, h)), const((1, h)), const((h, h)),
                  const((1, h)), const((h, HY_TAPS_W)), const((1, HY_TAPS_W)), const((1, HY_TAPS_W))],
        out_specs=[pl.BlockSpec((ROW_TILE, HY_TAPS_W), lambda i: (i, 0)), const((1, HY_TAPS_W))],
        compiler_params=_cparams(("arbitrary",)),
        name="hyena_taps",
    )(c2pb, w1[0:1], pad(w1[1:1 + HY_POS_BANDS]), pad(w1[1 + HY_POS_BANDS:]), b1[None], w2, b2[None], w3, b3[None], absd)
    hw = HY_TAPS_W // 2
    norm2 = ssq[:, :hw] + ssq[:, hw:] + 2.0 * taps[0:1, :hw] * taps[0:1, hw:]
    return taps, lax.rsqrt(norm2)


class _FftPlan:
    def __init__(self, length):
        self.length = length
        self.n = 2 * length
        self.n2 = FFT_N2
        self.n1 = self.n // self.n2
        self.n1h = self.n1 // 2
        k1 = self.n1h + 1
        self.k1p = -(-k1 // 8) * 8
        kk = np.arange(self.k1p)[:, None].astype(np.float64)
        live = (kk < k1)
        nn = np.arange(self.n1h)[None, :].astype(np.float64)
        th = 2.0 * np.pi * kk * nn / self.n1
        self.f1 = _split_const(np.concatenate([np.cos(th) * live, -np.sin(th) * live], axis=0))
        ck = np.where((kk == 0) | (kk == self.n1h), 1.0, 2.0) * live / self.n
        self.g1 = _split_const(np.concatenate([np.cos(th) * ck, -np.sin(th) * ck], axis=0).T)
        m = np.arange(self.n2).astype(np.float64)
        ph = 2.0 * np.pi * np.outer(m, m) / self.n2
        c, s = np.cos(ph), np.sin(ph)
        self.fb = _split_const(np.block([[c, s], [-s, c]]))
        self.fbi = _split_const(np.block([[c, -s], [s, c]]))
        tw = 2.0 * np.pi * kk[:, :, None] * m[None, :, None] / self.n
        self.twc = jnp.asarray(np.cos(tw), F32)
        self.tws = jnp.asarray(np.sin(tw), F32)


FFT_TS = 8
FFT_TC = 512


def _fft_a_body(fh_ref, fl_ref, x_ref, o_ref):
    fh, fl = fh_ref[...], fl_ref[...]
    for s in range(FFT_TS):
        o_ref[:, s, :] = _dot3c(fh, fl, x_ref[:, s, :])


def fft_stage_a(plan, x3, lead):
    rows_in, n2, c = x3.shape
    rows = 2 * plan.k1p
    fh, fl = (jnp.pad(f, ((0, 0), (lead, 0))) for f in plan.f1)
    blk = lambda r_: pl.BlockSpec((r_, FFT_TS, FFT_TC), lambda j, cc: (0, j, cc))
    fspec = pl.BlockSpec((rows, rows_in), lambda j, cc: (0, 0))
    return pl.pallas_call(
        _fft_a_body,
        out_shape=jax.ShapeDtypeStruct((rows, n2, c), F32),
        grid=(n2 // FFT_TS, c // FFT_TC),
        in_specs=[fspec, fspec, blk(rows_in)],
        out_specs=blk(rows),
        compiler_params=_cparams(("parallel", "parallel")),
        name="fft_stage_a",
    )(fh, fl, x3)


def _twiddled(a_ref, twc_ref, tws_ref):
    are, aim = a_ref[0, 0], a_ref[1, 0]
    c, s = twc_ref[0], tws_ref[0]
    return jnp.concatenate([are * c + aim * s, aim * c - are * s], axis=0)


def _fft_filter_b_body(a_ref, twc_ref, tws_ref, fbh_ref, fbl_ref, scale_ref, o_ref):
    n2 = FFT_N2
    x = _dot3c(fbh_ref[...], fbl_ref[...], _twiddled(a_ref, twc_ref, tws_ref))
    hw = HY_TAPS_W // 2
    xre, xim = x[:n2], x[n2:]
    o_ref[0, 0] = (xre[:, :hw] + xre[:, hw:]) * scale_ref[...]
    o_ref[1, 0] = (xim[:, :hw] - xim[:, hw:]) * scale_ref[...]


def fft_filter_stage_b(plan, a, scale):
    n2, k1p = plan.n2, plan.k1p
    hw = HY_TAPS_W // 2
    const = lambda shp: pl.BlockSpec(shp, lambda k: (0,) * len(shp))
    return pl.pallas_call(
        _fft_filter_b_body,
        out_shape=jax.ShapeDtypeStruct((2, k1p, n2, hw), F32),
        grid=(k1p,),
        in_specs=[pl.BlockSpec((2, 1, n2, HY_TAPS_W), lambda k: (0, k, 0, 0)),
                  pl.BlockSpec((1, n2, 1), lambda k: (k, 0, 0)), pl.BlockSpec((1, n2, 1), lambda k: (k, 0, 0)),
                  const((2 * n2, 2 * n2)), const((2 * n2, 2 * n2)), const((1, hw))],
        out_specs=pl.BlockSpec((2, 1, n2, hw), lambda k: (0, k, 0, 0)),
        compiler_params=_cparams(("parallel",)),
        name="fft_filter_stage_b",
    )(a, plan.twc, plan.tws, plan.fb[0], plan.fb[1], scale)


def _fft_conv_b_body(a_ref, kf_ref, twc_ref, tws_ref, fbh_ref, fbl_ref, fih_ref, fil_ref, o_ref):
    n2 = FFT_N2
    z = _dot3c(fbh_ref[...], fbl_ref[...], _twiddled(a_ref, twc_ref, tws_ref))
    zre, zim = z[:n2], z[n2:]
    kre, kim = kf_ref[0, 0], kf_ref[1, 0]
    y = jnp.concatenate([zre * kre - zim * kim, zre * kim + zim * kre], axis=0)
    q = _dot3c(fih_ref[...], fil_ref[...], y)
    qre, qim = q[:n2], q[n2:]
    c, s = twc_ref[0], tws_ref[0]
    o_ref[0, 0] = qre * c - qim * s
    o_ref[1, 0] = qim * c + qre * s


def fft_conv_stage_b(plan, a, kf, order):
    n2, k1p = plan.n2, plan.k1p
    const = lambda shp: pl.BlockSpec(shp, lambda k: (0,) * len(shp))
    return pl.pallas_call(
        _fft_conv_b_body,
        out_shape=jax.ShapeDtypeStruct((2, k1p, n2, HY_WIDTH), F32),
        grid=(k1p,),
        in_specs=[pl.BlockSpec((2, 1, n2, HY_WIDTH), lambda k: (0, k, 0, 0)),
                  pl.BlockSpec((2, 1, n2, HY_WIDTH), lambda k: (0, k, 0, order)),
                  pl.BlockSpec((1, n2, 1), lambda k: (k, 0, 0)), pl.BlockSpec((1, n2, 1), lambda k: (k, 0, 0)),
                  const((2 * n2, 2 * n2)), const((2 * n2, 2 * n2)), const((2 * n2, 2 * n2)), const((2 * n2, 2 * n2))],
        out_specs=pl.BlockSpec((2, 1, n2, HY_WIDTH), lambda k: (0, k, 0, 0)),
        compiler_params=_cparams(("parallel",)),
        name="fft_conv_stage_b",
    )(a, kf, plan.twc, plan.tws, plan.fb[0], plan.fb[1], plan.fbi[0], plan.fbi[1])


def _fft_inv_a_body(gh_ref, gl_ref, q_ref, z_ref, gate_ref, skip_ref, o_ref):
    gh, gl = gh_ref[...], gl_ref[...]
    for s in range(FFT_TS):
        y = _dot3c(gh, gl, q_ref[:, s, :])
        o_ref[:, s, :] = gate_ref[:, s, :] * (y + z_ref[:, s, :] * skip_ref[...])


def fft_inv_stage_a(plan, q3, z3, gate3, skip_row, lead):
    rows_out, n2, c = z3.shape
    rows = 2 * plan.k1p
    gh, gl = (jnp.pad(g, ((lead, 0), (0, 0))) for g in plan.g1)
    blk = lambda r_: pl.BlockSpec((r_, FFT_TS, FFT_TC), lambda j, cc: (0, j, cc))
    gspec = pl.BlockSpec((rows_out, rows), lambda j, cc: (0, 0))
    return pl.pallas_call(
        _fft_inv_a_body,
        out_shape=jax.ShapeDtypeStruct((rows_out, n2, c), F32),
        grid=(n2 // FFT_TS, c // FFT_TC),
        in_specs=[gspec, gspec, blk(rows), blk(rows_out), blk(rows_out),
                  pl.BlockSpec((1, FFT_TC), lambda j, cc: (0, cc))],
        out_specs=blk(rows_out),
        compiler_params=_cparams(("parallel", "parallel")),
        name="fft_inv_stage_a",
    )(gh, gl, q3, z3, gate3, skip_row)


def hyena_long(hv, hx1, hx2, taps, scale, skip, lead):
    length = taps.shape[0]
    plan = _FftPlan(length)
    n2, k1p = plan.n2, plan.k1p
    v3 = lambda a: a.reshape(lead + plan.n1h, n2, HY_WIDTH)
    ta = fft_stage_a(plan, taps.reshape(plan.n1h, n2, HY_TAPS_W), 0)
    kf = fft_filter_stage_b(plan, ta.reshape(2, k1p, n2, HY_TAPS_W), scale)
    z = v3(hv)
    for o, gate in enumerate((hx1, hx2)):
        a = fft_stage_a(plan, z, lead)
        q = fft_conv_stage_b(plan, a.reshape(2, k1p, n2, HY_WIDTH), kf, o)
        z = fft_inv_stage_a(plan, q.reshape(2 * k1p, n2, HY_WIDTH), z, v3(gate), skip[o:o + 1], lead)
    return z.reshape(-1, HY_WIDTH)


def _rec_out_body(x_ref, yf_ref, yb_ref, bonus_ref, sgr_ref, zh_ref, zc_ref, sgh_ref, gnw_ref, gnb_ref, bm_ref, wa_ref, wb_ref,
                  gate_ref, fin_ref, o_ref, *, final):
    y = yf_ref[...] + yb_ref[...]
    bm = bm_ref[...]
    mean = jnp.dot(y, bm, precision=HIGHEST, preferred_element_type=F32)
    yc = y - mean
    var = jnp.dot(yc * yc, bm, precision=HIGHEST, preferred_element_type=F32)
    yn = yc * lax.rsqrt(var + RWKV_GN_EPS) * gnw_ref[...] + gnb_ref[...]
    ya = ((yn + bonus_ref[...]) * sgr_ref[...]).astype(BF16)
    zh = zh_ref[...]
    if not final:
        zh = jnp.where(pl.program_id(0) == 0, zc_ref[...], zh)
    yh = (zh * sgh_ref[...]).astype(BF16)
    out = jnp.dot(ya, wa_ref[...], preferred_element_type=F32) + jnp.dot(yh, wb_ref[...], preferred_element_type=F32)
    xn = x_ref[...] + gate_ref[0] * out
    if final:
        xn = xn * lax.rsqrt(jnp.mean(xn * xn, axis=-1, keepdims=True) + NORM_EPS) * fin_ref[...]
    o_ref[...] = xn


RWKV_GN_EPS = 64e-5


def rec_out_proj(xs, yf, yb, bonus, sgr, zh, zc, sgh, gn_w, gn_b, w_out, gate2, final_g, *, final):
    r = xs.shape[0]
    d = D_MODEL
    off = 1 if final else 0
    nt = r // ROW_TILE - off
    rows = lambda w: pl.BlockSpec((ROW_TILE, w), lambda i: (i + off, 0))
    const = lambda shp: pl.BlockSpec(shp, lambda i: (0,) * len(shp))
    bw = BRANCH_W
    return pl.pallas_call(
        functools.partial(_rec_out_body, final=final),
        out_shape=jax.ShapeDtypeStruct((nt * ROW_TILE, d), F32),
        grid=(nt,),
        in_specs=[rows(d), rows(bw), rows(bw), rows(bw), rows(bw), rows(bw), const((CTX_LEN, bw)), rows(bw),
                  const((1, bw)), const((1, bw)),
                  const((bw, bw)), const((bw, d)), const((bw, d)),
                  pl.BlockSpec((1, 1, d), lambda i: (jnp.minimum(i + off, 1), 0, 0)), const((1, d))],
        out_specs=pl.BlockSpec((ROW_TILE, d), lambda i: (i, 0)),
        compiler_params=_cparams(("parallel",)),
        name="rec_out_proj",
    )(xs, yf, yb, bonus, sgr, zh, zc, sgh, gn_w[None], gn_b[None], _block_sum_mat(bw, 1.0 / HEAD_DIM),
      w_out[:bw].astype(BF16), w_out[bw:].astype(BF16), gate2, final_g[None])


def rec_layer(xs, mods, norm_g, w_in, mu, w0, w_up, a0, a_up, k_k, k_a, r_k, gn_w, gn_b, hy_short, hy_w1, hy_b1, hy_w2,
              hy_b2, hy_w3, hy_b3, hy_skip, w_out, final_g, final):
    shift2, scale2, gate2 = mods
    rw, hv, hx1, hx2, sgr, sgh = rec_in_proj(xs, norm_g, scale2, shift2, w_in, mu, hy_short)
    g, add, bonus = rwkv_prep(rw, w0, w_up, a0, a_up, k_k, k_a, r_k)
    yf, yb = rwkv_scan(g, add)
    fargs = (hy_w1, hy_b1, hy_w2, hy_b2, hy_w3, hy_b3)
    n = xs.shape[0] - CTX_LEN
    taps, scale = hyena_taps(n, *fargs)
    zh = hyena_long(hv, hx1, hx2, taps, scale, hy_skip, CTX_LEN // FFT_N2)
    if final:
        z_ctx = hv[:CTX_LEN]
    else:
        taps_c, scale_c = hyena_taps(CTX_LEN, *fargs)
        z_ctx = hyena_short(hv[:CTX_LEN], hx1[:CTX_LEN], hx2[:CTX_LEN], taps_c, scale_c, hy_skip)
    return rec_out_proj(xs, yf, yb, bonus, sgr, zh, z_ctx, sgh, gn_w, gn_b, w_out, gate2, final_g, final=final)


def kernel(x, c, ctx, c_ctx, attn_norm, attn_ada_w, attn_ada_b, attn_w_in, na_rpb, gqa_q_gain, gqa_k_gain, attn_w_out,
           rec_norm, rec_ada_w, rec_ada_b, rec_w_in, rwkv_mu, rwkv_w0, rwkv_w_up, rwkv_a0, rwkv_a_up, rwkv_k_k, rwkv_k_a,
           rwkv_r_k, rwkv_gn_w, rwkv_gn_b, hy_short, hy_w1, hy_b1, hy_w2, hy_b2, hy_w3, hy_b3, hy_skip, rec_w_out,
           final_norm):
    assert x.shape[0] == 1 and ctx.shape[1] == CTX_LEN and x.shape[2] == D_MODEL
    n = x.shape[1]
    assert n % ROW_TILE == 0 and n // ROW_TILE >= 3
    assert attn_w_in.shape[0] == rec_w_in.shape[0]
    d = D_MODEL
    cond8 = jnp.zeros((8, d), F32).at[0].set(c_ctx).at[1].set(c[0])
    m_attn = adaln_all(cond8, attn_ada_w, attn_ada_b)
    m_rec = adaln_all(cond8, rec_ada_w, rec_ada_b)
    mods = lambda m, i: tuple(m[i, :2, j * d:(j + 1) * d].reshape(2, 1, d) for j in range(3))
    cos_t, sin_t = _rope_tables(n)
    xs = jnp.concatenate([ctx[0], x[0]], axis=0)
    depth = attn_w_in.shape[0] + rec_w_in.shape[0]
    for layer in range(depth):
        i = layer // 2
        final = layer == depth - 1
        if layer % 2 == 0:
            xs = attn_layer(xs, mods(m_attn, i), attn_norm[i], attn_w_in[i], na_rpb[i], gqa_q_gain[i], gqa_k_gain[i],
                            attn_w_out[i], cos_t, sin_t, final_norm, final)
        else:
            xs = rec_layer(xs, mods(m_rec, i), rec_norm[i], rec_w_in[i], rwkv_mu[i], rwkv_w0[i], rwkv_w_up[i],
                           rwkv_a0[i], rwkv_a_up[i], rwkv_k_k[i], rwkv_k_a[i], rwkv_r_k[i], rwkv_gn_w[i], rwkv_gn_b[i],
                           hy_short[i], hy_w1[i], hy_b1[i], hy_w2[i], hy_b2[i], hy_w3[i], hy_b3[i], hy_skip[i],
                           rec_w_out[i], final_norm, final)
    return xs[None]


def _hy_short_body(fh_ref, fl_ref, gh_ref, gl_ref, v_ref, x1_ref, x2_ref, taps_ref, scale_ref, skip_ref, o_ref):
    fh, fl, gh, gl = fh_ref[...], fl_ref[...], gh_ref[...], gl_ref[...]
    kp = fh.shape[0] // 2
    hw = HY_TAPS_W // 2
    tf = _dot3c(fh, fl, taps_ref[...])
    kre = (tf[:kp, :hw] + tf[:kp, hw:]) * scale_ref[...]
    kim = (tf[kp:, :hw] - tf[kp:, hw:]) * scale_ref[...]
    z = v_ref[...]
    for o, gate_ref in enumerate((x1_ref, x2_ref)):
        ls = slice(o * HY_WIDTH, (o + 1) * HY_WIDTH)
        zf = _dot3c(fh, fl, z)
        zre, zim = zf[:kp], zf[kp:]
        y = jnp.concatenate([zre * kre[:, ls] - zim * kim[:, ls], zre * kim[:, ls] + zim * kre[:, ls]], axis=0)
        z = gate_ref[...] * (_dot3c(gh, gl, y) + z * skip_ref[o:o + 1])
    o_ref[...] = z


def hyena_short(hv, hx1, hx2, taps, scale, skip):
    length = hv.shape[0]
    n = 2 * length
    k1 = length + 1
    kp = -(-k1 // 8) * 8
    kk = np.arange(kp)[:, None].astype(np.float64)
    live = kk < k1
    th = 2.0 * np.pi * kk * np.arange(length)[None, :] / n
    f = _split_const(np.concatenate([np.cos(th) * live, -np.sin(th) * live], axis=0))
    ck = np.where((kk == 0) | (kk == length), 1.0, 2.0) * live / n
    g = _split_const(np.concatenate([np.cos(th) * ck, -np.sin(th) * ck], axis=0).T)
    return pl.pallas_call(
        _hy_short_body,
        out_shape=jax.ShapeDtypeStruct((length, HY_WIDTH), F32),
        compiler_params=pltpu.CompilerParams(vmem_limit_bytes=VMEM_LIMIT),
        name="hyena_short",
    )(f[0], f[1], g[0], g[1], hv, hx1, hx2, taps, scale, skip)
```

```python
import functools
import math

import jax
import jax.numpy as jnp
import numpy as np
from jax import lax
from jax.experimental import pallas as pl
from jax.experimental.pallas import tpu as pltpu

F32 = jnp.float32
BF16 = jnp.bfloat16
HIGHEST = lax.Precision.HIGHEST

D_MODEL = 1024
GRID_W = 64
CTX_LEN = 256
HEAD_DIM = 64
BRANCH_W = 512
N_HEADS = 8
GQA_KV_W = 128
NA_WIN_ROWS = 8
NA_WIN_COLS = 16
ROPE_THETA = 10000.0
ROPE_FREQS = 16
NORM_EPS = 1e-6
ROW_TILE = 256
NA_GROUP_ROWS = 4
NEG_BIG = -1e30
LOG2E = math.log2(math.e)
QK_SCALE = HEAD_DIM ** -0.5 * LOG2E
VMEM_LIMIT = 56 * 1024 * 1024

ATTN_SPLITS = (512, 512, 512, 512, 512, 128, 128, 512)
GQA_HEAD_ORDER = (0, 4, 1, 5, 2, 6, 3, 7)


def _cparams(sem):
    return pltpu.CompilerParams(dimension_semantics=sem, vmem_limit_bytes=VMEM_LIMIT)


def _silu(v):
    return v * (1.0 / (1.0 + jnp.exp(-v)))


def _lane_half(shape):
    return (lax.broadcasted_iota(jnp.int32, shape, len(shape) - 1) // HEAD_DIM) % 2


def _dot_nt(a, b):
    return lax.dot_general(a, b, (((1,), (1,)), ((), ())), preferred_element_type=F32)


def _adaln_body(cond_ref, w_ref, b_ref, o_ref):
    s = _silu(cond_ref[...])
    o_ref[0] = jnp.dot(s, w_ref[0], precision=HIGHEST, preferred_element_type=F32) + b_ref[0]


def adaln_all(cond8, ada_w, ada_b):
    nl = ada_w.shape[0]
    d = D_MODEL
    return pl.pallas_call(
        _adaln_body,
        out_shape=jax.ShapeDtypeStruct((nl, 8, 3 * d), F32),
        grid=(nl, 3),
        in_specs=[
            pl.BlockSpec((8, d), lambda l, j: (0, 0)),
            pl.BlockSpec((1, d, d), lambda l, j: (l, 0, j)),
            pl.BlockSpec((1, 1, d), lambda l, j: (l, 0, j)),
        ],
        out_specs=pl.BlockSpec((1, 8, d), lambda l, j: (l, 0, j)),
        compiler_params=_cparams(("parallel", "parallel")),
        name="adaln",
    )(cond8, ada_w, ada_b.reshape(nl, 1, 3 * d))


def _modulated(x_ref, g_ref, scale_ref, shift_ref):
    xf = x_ref[...]
    y = xf * lax.rsqrt(jnp.mean(xf * xf, axis=-1, keepdims=True) + NORM_EPS)
    return (y * g_ref[...]) * (1.0 + scale_ref[0]) + shift_ref[0]


def _mod_specs():
    d = D_MODEL
    return [
        pl.BlockSpec((ROW_TILE, d), lambda i: (i, 0)),
        pl.BlockSpec((1, d), lambda i: (0, 0)),
        pl.BlockSpec((1, 1, d), lambda i: (jnp.minimum(i, 1), 0, 0)),
        pl.BlockSpec((1, 1, d), lambda i: (jnp.minimum(i, 1), 0, 0)),
    ]


def _attn_in_body(x_ref, g_ref, scale_ref, shift_ref, w_ref, cos_ref, sin_ref, gq_ref, gqs_ref, gk_ref, gks_ref,
                  bdq_ref, bdk_ref,
                  qa_ref, ka_ref, va_ref, sga_ref, qb_ref, kb_ref, vb_ref, sgb_ref):
    xm = _modulated(x_ref, g_ref, scale_ref, shift_ref).astype(BF16)
    u = jnp.dot(xm, w_ref[...], preferred_element_type=F32)
    qa, ka, va, ga = u[:, 0:512], u[:, 512:1024], u[:, 1024:1536], u[:, 1536:2048]
    qb, kb, vb, gb = u[:, 2048:2560], u[:, 2560:2688], u[:, 2688:2816], u[:, 2816:3328]
    qbs, kbs = u[:, 3328:3840], u[:, 3840:3968]
    scale = QK_SCALE
    qa_ref[...] = (qa * scale).astype(BF16)
    ka_ref[...] = ka.astype(BF16)
    va_ref[...] = va.astype(BF16)
    sga_ref[...] = _silu(ga)
    sgb_ref[...] = _silu(gb)
    vb_ref[...] = vb.astype(BF16)
    cos_k, sin_k = cos_ref[...], sin_ref[...]
    cos_q = jnp.concatenate([cos_k] * 4, axis=1)
    sin_q = jnp.concatenate([sin_k] * 4, axis=1)
    rs_q = lax.rsqrt(jnp.dot(qb * qb, bdq_ref[...], precision=HIGHEST, preferred_element_type=F32) + NORM_EPS)
    rs_k = lax.rsqrt(jnp.dot(kb * kb, bdk_ref[...], precision=HIGHEST, preferred_element_type=F32) + NORM_EPS)
    qr = rs_q * (qb * gq_ref[...] * cos_q + qbs * gqs_ref[...] * sin_q)
    kr = rs_k * (kb * gk_ref[...] * cos_k + kbs * gks_ref[...] * sin_k)
    qb_ref[...] = (qr * scale).astype(BF16)
    kb_ref[...] = kr.astype(BF16)


def _rope_tables(n):
    t = jnp.arange(n, dtype=jnp.int32)
    pos = jnp.stack([t // GRID_W, t % GRID_W], axis=-1).astype(F32)
    inv_freq = ROPE_THETA ** (-jnp.arange(ROPE_FREQS, dtype=F32) / ROPE_FREQS)
    ang = pos[:, :, None] * inv_freq
    c, s = jnp.cos(ang), jnp.sin(ang)
    cos64 = jnp.concatenate([c[:, 0], c[:, 0], c[:, 1], c[:, 1]], axis=-1)
    sin64 = jnp.concatenate([-s[:, 0], s[:, 0], -s[:, 1], s[:, 1]], axis=-1)
    cos64 = jnp.concatenate([jnp.ones((CTX_LEN, HEAD_DIM), F32), cos64], axis=0)
    sin64 = jnp.concatenate([jnp.zeros((CTX_LEN, HEAD_DIM), F32), sin64], axis=0)
    return jnp.tile(cos64, (1, 2)), jnp.tile(sin64, (1, 2))


def _reorder_heads(w, order, axis):
    take = lambda h: lax.slice_in_dim(w, h * HEAD_DIM, (h + 1) * HEAD_DIM, axis=axis)
    return jnp.concatenate([take(h) for h in order], axis=axis)


def _swap_rope_halves(w):
    shp = w.shape
    return jnp.flip(w.reshape(shp[:-1] + (shp[-1] // (2 * ROPE_FREQS), 2, ROPE_FREQS)), axis=-2).reshape(shp)


def attn_in_proj(xs, norm_g, scale2, shift2, w_in, q_gain, k_gain, cos_t, sin_t):
    r = xs.shape[0]
    d = D_MODEL
    parts, start = [], 0
    for s in ATTN_SPLITS:
        parts.append(w_in[:, start:start + s])
        start += s
    wqa, wka, wva, wga, wqb, wkb, wvb, wgb = parts
    wqb_p = _reorder_heads(wqb, GQA_HEAD_ORDER, 1)
    wgb_p = _reorder_heads(wgb, GQA_HEAD_ORDER, 1)
    wqb_sw = _swap_rope_halves(wqb_p)
    wkb_sw = _swap_rope_halves(wkb)
    w_ext = jnp.concatenate([wqa, wka, wva, wga, wqb_p, wkb, wvb, wgb_p, wqb_sw, wkb_sw], axis=1).astype(BF16)
    gq = jnp.tile(q_gain, N_HEADS)[None]
    gqs = jnp.tile(_swap_rope_halves(q_gain), N_HEADS)[None]
    gk = jnp.tile(k_gain, 2)[None]
    gks = jnp.tile(_swap_rope_halves(k_gain), 2)[None]
    bdq = jnp.asarray(np.kron(np.eye(N_HEADS), np.full((HEAD_DIM, HEAD_DIM), 1.0 / HEAD_DIM)), F32)
    bdk = jnp.asarray(np.kron(np.eye(2), np.full((HEAD_DIM, HEAD_DIM), 1.0 / HEAD_DIM)), F32)
    wcols = w_ext.shape[1]
    const = lambda shp: pl.BlockSpec(shp, lambda i: (0,) * len(shp))
    rows = lambda w: pl.BlockSpec((ROW_TILE, w), lambda i: (i, 0))
    out_shapes = [
        jax.ShapeDtypeStruct((r, 512), BF16), jax.ShapeDtypeStruct((r, 512), BF16), jax.ShapeDtypeStruct((r, 512), BF16),
        jax.ShapeDtypeStruct((r, 512), F32),
        jax.ShapeDtypeStruct((r, 512), BF16), jax.ShapeDtypeStruct((r, 128), BF16), jax.ShapeDtypeStruct((r, 128), BF16),
        jax.ShapeDtypeStruct((r, 512), F32),
    ]
    return pl.pallas_call(
        _attn_in_body,
        out_shape=out_shapes,
        grid=(r // ROW_TILE,),
        in_specs=_mod_specs() + [const((d, wcols)), rows(128), rows(128), const((1, 512)), const((1, 512)),
                                 const((1, 128)), const((1, 128)), const((512, 512)), const((128, 128))],
        out_specs=[rows(512), rows(512), rows(512), rows(512), rows(512), rows(128), rows(128), rows(512)],
        compiler_params=_cparams(("parallel",)),
        name="attn_in_proj",
    )(xs, norm_g[None], scale2, shift2, w_ext, cos_t, sin_t, gq, gqs, gk, gks, bdq, bdk)


def _na_cols_body(rpb_ref, sel_ref, neg_ref, o_ref):
    o_ref[...] = jnp.dot(rpb_ref[...], sel_ref[...], precision=HIGHEST, preferred_element_type=F32) + neg_ref[...]


def _na_bias_tables(rpb, rows):
    nrel_r, nrel_c = 2 * NA_WIN_ROWS - 1, 2 * NA_WIN_COLS - 1
    qc = np.arange(GRID_W)[:, None]
    kc = np.arange(GRID_W)[None, :]
    col0 = np.clip(qc - NA_WIN_COLS // 2, 0, GRID_W - NA_WIN_COLS)
    col_ok = (kc >= col0) & (kc < col0 + NA_WIN_COLS)
    rc = kc - qc + NA_WIN_COLS - 1
    sel = np.zeros((128, GRID_W * GRID_W), np.float32)
    sel[np.where(col_ok, rc, 127).reshape(-1), np.arange(GRID_W * GRID_W)] = col_ok.reshape(-1)
    neg = np.where(col_ok, 0.0, NEG_BIG).astype(np.float32).reshape(1, -1)
    rpb2 = jnp.zeros((128, 128), F32).at[:N_HEADS * nrel_r, :nrel_c].set(rpb.reshape(N_HEADS * nrel_r, nrel_c))
    cols = pl.pallas_call(
        _na_cols_body,
        out_shape=jax.ShapeDtypeStruct((128, GRID_W * GRID_W), F32),
        name="na_bias_cols",
    )(rpb2, jnp.asarray(sel), jnp.asarray(neg))
    cols = cols[:N_HEADS * nrel_r].reshape(N_HEADS, nrel_r, GRID_W, GRID_W)
    kh = min(NA_WIN_ROWS, rows)
    g = rows // NA_GROUP_ROWS
    cases = [(0, 0), (NA_GROUP_ROWS, 0), (rows - NA_GROUP_ROWS, NA_GROUP_ROWS * (g - 3))]
    masked = jnp.full((N_HEADS, GRID_W, GRID_W), NEG_BIG, F32)
    tabs = []
    for qr_first, start in cases:
        blocks = []
        for j in range(NA_GROUP_ROWS):
            qr = qr_first + j
            row0 = min(max(qr - kh // 2, 0), rows - kh)
            for i in range(3 * NA_GROUP_ROWS):
                kr = start + i
                blocks.append(cols[:, kr - qr + NA_WIN_ROWS - 1] if row0 <= kr < row0 + kh else masked)
        tab = jnp.stack(blocks, axis=1).reshape(N_HEADS, NA_GROUP_ROWS, 3 * NA_GROUP_ROWS, GRID_W, GRID_W)
        tabs.append(tab.transpose(0, 1, 3, 2, 4).reshape(N_HEADS, ROW_TILE, 3 * ROW_TILE))
    return jnp.stack(tabs) * LOG2E


def _na_body(q_ref, kc_ref, k0_ref, k1_ref, k2_ref, vc_ref, v0_ref, v1_ref, v2_ref, bias_ref, sg_ref, o_ref):
    half = _lane_half((ROW_TILE, 128))
    for hp in range(N_HEADS // 2):
        ls = slice(hp * 128, (hp + 1) * 128)
        qp = q_ref[:, ls]
        ks = [r[:, ls] for r in (k0_ref, k1_ref, k2_ref, kc_ref)]
        vs = [r[:, ls] for r in (v0_ref, v1_ref, v2_ref, vc_ref)]
        outs = []
        for j in range(2):
            qm = jnp.where(half == j, qp, jnp.zeros_like(qp))
            s = [_dot_nt(qm, k) for k in ks]
            s_win = jnp.concatenate(s[:3], axis=1) + bias_ref[0, 2 * hp + j]
            s_ctx = s[3]
            m = jnp.maximum(jnp.max(s_win, axis=1, keepdims=True), jnp.max(s_ctx, axis=1, keepdims=True))
            p_win = jnp.exp2(s_win - m)
            p_ctx = jnp.exp2(s_ctx - m)
            l = jnp.sum(p_win, axis=1, keepdims=True) + jnp.sum(p_ctx, axis=1, keepdims=True)
            o = jnp.dot(p_ctx.astype(BF16), vs[3], preferred_element_type=F32)
            for b in range(3):
                o += jnp.dot(p_win[:, b * ROW_TILE:(b + 1) * ROW_TILE].astype(BF16), vs[b], preferred_element_type=F32)
            outs.append(o / l)
        o_pair = jnp.where(half == 0, outs[0], outs[1])
        o_ref[:, ls] = (o_pair * sg_ref[:, ls]).astype(BF16)


def na_attention(qa, ka, va, sga, bias_tabs, n):
    g = n // ROW_TILE
    w = BRANCH_W

    def kv_spec(off):
        return pl.BlockSpec((ROW_TILE, w), lambda i: (jnp.clip(i - 1, 0, g - 3) + off + 1, 0))

    ctx_spec = pl.BlockSpec((ROW_TILE, w), lambda i: (0, 0))
    q_spec = pl.BlockSpec((ROW_TILE, w), lambda i: (i + 1, 0))
    case = lambda i: jnp.where(i == 0, 0, jnp.where(i == g - 1, 2, 1))
    bias_spec = pl.BlockSpec((1, N_HEADS, ROW_TILE, 3 * ROW_TILE), lambda i: (case(i), 0, 0, 0))
    return pl.pallas_call(
        _na_body,
        out_shape=jax.ShapeDtypeStruct((n, w), BF16),
        grid=(g,),
        in_specs=[q_spec, ctx_spec, kv_spec(0), kv_spec(1), kv_spec(2), ctx_spec, kv_spec(0), kv_spec(1), kv_spec(2),
                  bias_spec, q_spec],
        out_specs=pl.BlockSpec((ROW_TILE, w), lambda i: (i, 0)),
        compiler_params=_cparams(("parallel",)),
        name="na_attention",
    )(qa, ka, ka, ka, ka, va, va, va, va, bias_tabs, sga)


def _flash_body(q_ref, k_ref, v_ref, sg_ref, o_ref, m_ref, acc_ref, *, mha):
    kv = pl.program_id(1)
    tq = q_ref.shape[0]

    @pl.when(kv == 0)
    def _():
        m_ref[...] = jnp.full(m_ref.shape, NEG_BIG, F32)
        acc_ref[...] = jnp.zeros(acc_ref.shape, F32)

    khalf = _lane_half((k_ref.shape[0], 128))
    for p in range(N_HEADS // 2):
        ls = slice(p * 128, (p + 1) * 128)
        kls = ls if mha else slice(0, 128)
        qp = q_ref[:, ls]
        kp = k_ref[:, kls]
        vp = v_ref[:, kls]
        for j in range(2):
            hh = 2 * p + j
            km = jnp.where(khalf == j, kp, jnp.zeros_like(kp))
            vm = jnp.where(khalf == j, vp, jnp.ones_like(vp))
            s = _dot_nt(qp, km)
            m_prev = m_ref[hh]
            m_new = jnp.maximum(m_prev, jnp.max(s, axis=1, keepdims=True))
            alpha = jnp.exp2(m_prev - m_new)
            pr = jnp.exp2(s - m_new[:, :1]).astype(BF16)
            acc_ref[hh] = alpha * acc_ref[hh] + jnp.dot(pr, vm, preferred_element_type=F32)
            m_ref[hh] = m_new

    @pl.when(kv == pl.num_programs(1) - 1)
    def _():
        half = _lane_half((tq, 128))
        for p in range(N_HEADS // 2):
            ls = slice(p * 128, (p + 1) * 128)
            a0, a1 = acc_ref[2 * p], acc_ref[2 * p + 1]
            o0 = a0 / pltpu.roll(a0, HEAD_DIM, 1)
            o1 = a1 / pltpu.roll(a1, HEAD_DIM, 1)
            o_ref[:, ls] = (jnp.where(half == 0, o0, o1) * sg_ref[:, ls]).astype(BF16)


def flash_attention(q, k, v, sg, *, q_block0, nq, tk, nk, mha):
    kw = k.shape[1]
    tq = ROW_TILE
    return pl.pallas_call(
        functools.partial(_flash_body, mha=mha),
        out_shape=jax.ShapeDtypeStruct((nq * tq, BRANCH_W), BF16),
        grid=(nq, nk),
        in_specs=[
            pl.BlockSpec((tq, BRANCH_W), lambda i, j: (i + q_block0, 0)),
            pl.BlockSpec((tk, kw), lambda i, j: (j, 0)),
            pl.BlockSpec((tk, kw), lambda i, j: (j, 0)),
            pl.BlockSpec((tq, BRANCH_W), lambda i, j: (i + q_block0, 0)),
        ],
        out_specs=pl.BlockSpec((tq, BRANCH_W), lambda i, j: (i, 0)),
        scratch_shapes=[pltpu.VMEM((N_HEADS, tq, 128), F32)] * 2,
        compiler_params=_cparams(("parallel", "arbitrary")),
        name="flash_mha" if mha else "flash_gqa",
    )(q, k, v, sg)


def _out_body(x_ref, ya_ref, yb_ref, wa_ref, wb_ref, gate_ref, fin_ref, o_ref, *, final):
    y = jnp.dot(ya_ref[...], wa_ref[...], preferred_element_type=F32)
    y += jnp.dot(yb_ref[...], wb_ref[...], preferred_element_type=F32)
    xn = x_ref[...] + gate_ref[0] * y
    if final:
        xn = xn * lax.rsqrt(jnp.mean(xn * xn, axis=-1, keepdims=True) + NORM_EPS) * fin_ref[...]
    o_ref[...] = xn


def out_proj(xs, ya, yb, wa, wb, gate2, final_g, *, final):
    r = xs.shape[0]
    d = D_MODEL
    rows = lambda w: pl.BlockSpec((ROW_TILE, w), lambda i: (i, 0))
    const = lambda shp: pl.BlockSpec(shp, lambda i: (0,) * len(shp))
    return pl.pallas_call(
        functools.partial(_out_body, final=final),
        out_shape=jax.ShapeDtypeStruct((r, d), F32),
        grid=(r // ROW_TILE,),
        in_specs=[rows(d), rows(BRANCH_W), rows(BRANCH_W), const((BRANCH_W, d)), const((BRANCH_W, d)),
                  pl.BlockSpec((1, 1, d), lambda i: (jnp.minimum(i, 1), 0, 0)), const((1, d))],
        out_specs=rows(d),
        compiler_params=_cparams(("parallel",)),
        name="out_proj",
    )(xs, ya, yb, wa.astype(BF16), wb.astype(BF16), gate2, final_g[None])


def _kv_tile(r):
    nb = r // ROW_TILE
    best = max(k for k in range(1, 9) if nb % k == 0)
    return best * ROW_TILE, nb // best


def attn_layer(xs, mods, norm_g, w_in, rpb, q_gain, k_gain, w_out, cos_t, sin_t, final_g, final):
    r = xs.shape[0]
    n = r - CTX_LEN
    shift2, scale2, gate2 = mods
    qa, ka, va, sga, qb, kb, vb, sgb = attn_in_proj(xs, norm_g, scale2, shift2, w_in, q_gain, k_gain, cos_t, sin_t)
    bias_tabs = _na_bias_tables(rpb, n // GRID_W)
    ya_lat = na_attention(qa, ka, va, sga, bias_tabs, n)
    ya_ctx = flash_attention(qa, ka, va, sga, q_block0=0, nq=1, tk=CTX_LEN, nk=1, mha=True)
    tk, nk = _kv_tile(r)
    yb_lat = flash_attention(qb, kb, vb, sgb, q_block0=1, nq=n // ROW_TILE, tk=tk, nk=nk, mha=False)
    yb_ctx = flash_attention(qb, kb, vb, sgb, q_block0=0, nq=1, tk=CTX_LEN, nk=1, mha=False)
    ya = jnp.concatenate([ya_ctx, ya_lat], axis=0)
    yb = jnp.concatenate([yb_ctx, yb_lat], axis=0)
    wb = _reorder_heads(w_out[BRANCH_W:], GQA_HEAD_ORDER, 0)
    return out_proj(xs, ya, yb, w_out[:BRANCH_W], wb, gate2, final_g, final=final)


def _split(a):
    hi = a.astype(BF16)
    return hi, (a - hi.astype(F32)).astype(BF16)


def _dot3(a, b, dims=(((1,), (0,)), ((), ()))):
    ah, al = _split(a)
    bh, bl = _split(b)
    dg = functools.partial(lax.dot_general, dimension_numbers=dims, preferred_element_type=F32)
    return dg(ah, bh) + (dg(al, bh) + dg(ah, bl))


def _dot1(a, b, dims=(((1,), (0,)), ((), ()))):
    return lax.dot_general(a.astype(BF16), b.astype(BF16), dims, preferred_element_type=F32)


_NT = (((1,), (1,)), ((), ()))
_TN = (((0,), (0,)), ((), ()))


RWKV_SHIFT_W = 1664
HY_IN_W = 1536
HALO = 8
REC_HALO_W = RWKV_SHIFT_W + HY_IN_W


def _rec_in_body(x_ref, xp_ref, xn_ref, g_ref, scale_ref, shift_ref, w_ref, mu_ref, taps_ref,
                 rw_ref, hv_ref, hx1_ref, hx2_ref, sgr_ref, sgh_ref, u_scr):
    i = pl.program_id(0)
    nt = pl.num_programs(0)
    xe = jnp.concatenate([xp_ref[...], x_ref[...], xn_ref[...]], axis=0)
    y = xe * lax.rsqrt(jnp.mean(xe * xe, axis=-1, keepdims=True) + NORM_EPS)
    xm = ((y * g_ref[...]) * (1.0 + scale_ref[0]) + shift_ref[0]).astype(BF16)
    u = jnp.dot(xm, w_ref[...], preferred_element_type=F32)
    row = lax.broadcasted_iota(jnp.int32, (ROW_TILE + 2 * HALO, 1), 0)
    keep = jnp.logical_and(jnp.logical_or(row >= HALO, i >= 2),
                           jnp.logical_or(row < ROW_TILE + HALO, jnp.logical_and(i >= 1, i < nt - 1)))
    u_scr[...] = jnp.where(keep, u[:, :REC_HALO_W], 0.0)
    up = u_scr[pl.ds(HALO - 1, ROW_TILE), :]
    uc = u_scr[pl.ds(HALO, ROW_TILE), :]
    un = u_scr[pl.ds(HALO + 1, ROW_TILE), :]
    w = RWKV_SHIFT_W
    rw_c = uc[:, :w]
    rw_ref[...] = rw_c + (0.5 * (up[:, :w] + un[:, :w]) - rw_c) * mu_ref[...]
    hy = up[:, w:] * taps_ref[0:1] + uc[:, w:] * taps_ref[1:2] + un[:, w:] * taps_ref[2:3]
    hv_ref[...] = hy[:, 0:512]
    hx1_ref[...] = hy[:, 512:1024]
    hx2_ref[...] = hy[:, 1024:1536]
    uc_all = u[HALO:HALO + ROW_TILE]
    sgr_ref[...] = _silu(uc_all[:, REC_HALO_W:REC_HALO_W + 512])
    sgh_ref[...] = _silu(uc_all[:, REC_HALO_W + 512:REC_HALO_W + 1024])


def rec_in_proj(xs, norm_g, scale2, shift2, w_in, mu, hy_short):
    r = xs.shape[0]
    d = D_MODEL
    w = RWKV_SHIFT_W
    w_ext = jnp.concatenate([w_in[:, :w], w_in[:, w + 512:w + 512 + HY_IN_W], w_in[:, w:w + 512],
                             w_in[:, w + 512 + HY_IN_W:]], axis=1).astype(BF16)
    nh = r // HALO
    per = ROW_TILE // HALO
    const = lambda shp: pl.BlockSpec(shp, lambda i: (0,) * len(shp))
    rows = lambda wd: pl.BlockSpec((ROW_TILE, wd), lambda i: (i, 0))
    f = lambda wd: jax.ShapeDtypeStruct((r, wd), F32)
    mod = _mod_specs()
    return pl.pallas_call(
        _rec_in_body,
        out_shape=[f(w), f(512), f(512), f(512), f(512), f(512)],
        grid=(r // ROW_TILE,),
        in_specs=[mod[0],
                  pl.BlockSpec((HALO, d), lambda i: (jnp.maximum(i * per - 1, 0), 0)),
                  pl.BlockSpec((HALO, d), lambda i: (jnp.minimum((i + 1) * per, nh - 1), 0)),
                  mod[1], mod[2], mod[3], const((d, w_ext.shape[1])), const((1, w)), const((3, HY_IN_W))],
        out_specs=[rows(w), rows(512), rows(512), rows(512), rows(512), rows(512)],
        scratch_shapes=[pltpu.VMEM((ROW_TILE + 2 * HALO, REC_HALO_W), F32)],
        compiler_params=_cparams(("parallel",)),
        name="rec_in_proj",
    )(xs, xs, xs, norm_g[None], scale2, shift2, w_ext, mu[None], hy_short)


CHUNK = 64
CPT = ROW_TILE // CHUNK


def _block_sum_mat(width, value):
    return jnp.asarray(np.kron(np.eye(width // HEAD_DIM), np.full((HEAD_DIM, HEAD_DIM), value)), F32)


def _rwkv_prep_body(r_ref, k_ref, v_ref, lora_ref, w0_ref, wup_ref, a0_ref, aup_ref, kk_ref, ka_ref, rk_ref,
                    tri_ref, bs_ref, g_ref, add_ref, bonus_ref):
    t = ROW_TILE
    r, k, v, lora = r_ref[...], k_ref[...], v_ref[...], lora_ref[...]
    bs = bs_ref[...]
    kk = k * kk_ref[...]
    kk = kk * lax.rsqrt(_dot3(kk * kk, bs) + 1e-12)
    tanh_lora = jnp.tanh(lora)
    row = lax.broadcasted_iota(jnp.int32, (t, t), 0)
    col = lax.broadcasted_iota(jnp.int32, (t, t), 1)
    same = ((row // CHUNK) == (col // CHUNK)).astype(F32)
    eye = (row == col).astype(F32)
    half = _lane_half((t, 128))
    half_c = _lane_half((HEAD_DIM, 128))
    rowc = lax.broadcasted_iota(jnp.int32, (HEAD_DIM, 128), 0)
    lanec = lax.broadcasted_iota(jnp.int32, (HEAD_DIM, 128), 1)
    level_masks = []
    bsz = 2
    while bsz < CHUNK:
        level_masks.append(jnp.logical_and((row // (2 * bsz)) == (col // (2 * bsz)), (row // bsz) != (col // bsz)))
        bsz *= 2
    first_mask = (row // 2) == (col // 2)

    dirs = []
    for d in range(2):
        tri = tri_ref[d]
        wl = w0_ref[d] + _dot3(tanh_lora, wup_ref[d])
        z = -wl
        w_log = -(jnp.maximum(z, 0.0) + jnp.log(1.0 + jnp.exp(-jnp.abs(z)))) - 0.5
        lw = -jnp.exp(w_log)
        a = 1.0 / (1.0 + jnp.exp(-(a0_ref[d] + _dot3(lora, aup_ref[d]))))
        kd = k * (1.0 + (a - 1.0) * ka_ref[...])
        b = kk * a
        incl = tri > 0.5
        cs = _dot3(tri, lw)
        tot = _dot3(same, lw)
        w_inv = jnp.exp(-cs)
        w_rest = jnp.exp(tot - cs)
        dirs.append(dict(kd=kd, incl=incl, strict=jnp.logical_and(incl, row != col), tot=tot,
                         kkt=kk * jnp.exp(cs - lw), kh=kd * w_inv, bh=b * w_inv, rt=r * jnp.exp(cs),
                         kdd=kd * w_rest, bdd=b * w_rest))
    chains = [(dd, j) for dd in dirs for j in range(2)]
    sels = [half == j for _, j in chains]
    bms = [jnp.where(sel, dd["bh"], 0.0) for (dd, _), sel in zip(chains, sels)]
    kms = [jnp.where(sel, dd["kh"], 0.0) for (dd, _), sel in zip(chains, sels)]
    l_bs = [jnp.where(dd["strict"], _dot3(dd["kkt"], bm, _NT), 0.0) for (dd, _), bm in zip(chains, bms)]
    tinvs = [eye - jnp.where(first_mask, l_b, 0.0) for l_b in l_bs]
    for mask in level_masks:
        xs = [_dot1(jnp.where(mask, l_b, 0.0), tinv) for l_b, tinv in zip(l_bs, tinvs)]
        tinvs = [tinv - _dot1(tinv, x) for tinv, x in zip(tinvs, xs)]
    resids = [eye - (tinv + _dot3(l_b, tinv)) for l_b, tinv in zip(l_bs, tinvs)]
    tinvs = [tinv + _dot1(tinv, resid) for tinv, resid in zip(tinvs, resids)]
    l_ks = [jnp.where(dd["strict"], _dot1(dd["kkt"], km, _NT), 0.0) for (dd, _), km in zip(chains, kms)]
    a_rks = [jnp.where(dd["incl"], _dot1(dd["rt"], km, _NT), 0.0) for (dd, _), km in zip(chains, kms)]
    a_rbs = [jnp.where(dd["incl"], _dot1(dd["rt"], bm, _NT), 0.0) for (dd, _), bm in zip(chains, bms)]
    ps = [_dot1(tinv, dd["kkt"]) for (dd, _), tinv in zip(chains, tinvs)]
    lvs = [_dot1(l_k, v) for l_k in l_ks]
    u0s = [_dot1(tinv, lv) for tinv, lv in zip(tinvs, lvs)]
    qs = [dd["rt"] - _dot1(a_rb, p) for (dd, _), a_rb, p in zip(chains, a_rbs, ps)]
    y0s = [_dot1(a_rk, v) - _dot1(a_rb, u0) for a_rk, a_rb, u0 in zip(a_rks, a_rbs, u0s)]

    sel0 = half == 0
    for d, dd in enumerate(dirs):
        p, u0, q, y0 = (jnp.where(sel0, x[2 * d], x[2 * d + 1]) for x in (ps, u0s, qs, y0s))
        for c in range(CPT):
            rs = slice(c * CHUNK, (c + 1) * CHUNK)
            x1 = _dot1(dd["bdd"][rs], p[rs], _TN)
            x2 = _dot1(dd["kdd"][rs], v[rs], _TN) - _dot1(dd["bdd"][rs], u0[rs], _TN)
            m_pair = jnp.where(half_c == 0, x1[:HEAD_DIM], x1[HEAD_DIM:])
            n_pair = jnp.where(half_c == 0, x2[:HEAD_DIM], x2[HEAD_DIM:])
            wc = jnp.exp(dd["tot"][c * CHUNK:c * CHUNK + 1])
            dg = jnp.where((lanec % HEAD_DIM) == rowc, wc, 0.0)
            g_ref[c, d, 0:HEAD_DIM, :] = dg - m_pair
            g_ref[c, d, HEAD_DIM:, :] = q[rs]
            add_ref[c, d, 0:HEAD_DIM, :] = n_pair
            add_ref[c, d, HEAD_DIM:, :] = y0[rs]

    kd_sum = dirs[0]["kd"] + dirs[1]["kd"]
    bonus_ref[...] = 0.5 * _dot3(r * kd_sum * rk_ref[...], bs) * v


def _lora_ext(up, first_row):
    out = jnp.zeros((2, 128, BRANCH_W), F32)
    for d in range(2):
        out = out.at[d, first_row + 32 * d:first_row + 32 * (d + 1)].set(up[d])
    return out


def rwkv_prep(rw, w0, w_up, a0, a_up, k_k, k_a, r_k):
    r = rw.shape[0]
    nt = r // ROW_TILE
    nch = r // CHUNK
    t = ROW_TILE
    ii = np.arange(t)
    same = (ii[:, None] // CHUNK) == (ii[None, :] // CHUNK)
    tri = jnp.asarray(np.stack([same & (ii[None, :] <= ii[:, None]), same & (ii[None, :] >= ii[:, None])]), F32)
    lane = lambda blk: pl.BlockSpec((t, 128), lambda i, p, blk=blk: (i, blk + p))
    pvec = pl.BlockSpec((1, 128), lambda i, p: (0, p))
    dvec = pl.BlockSpec((2, 1, 128), lambda i, p: (0, 0, p))
    dmat = pl.BlockSpec((2, 128, 128), lambda i, p: (0, 0, p))
    gspec = pl.BlockSpec((CPT, 2, HEAD_DIM + CHUNK, 128), lambda i, p: (i, 0, 0, p))
    gshape = jax.ShapeDtypeStruct((nch, 2, HEAD_DIM + CHUNK, BRANCH_W), F32)
    return pl.pallas_call(
        _rwkv_prep_body,
        out_shape=[gshape, gshape, jax.ShapeDtypeStruct((r, BRANCH_W), F32)],
        grid=(nt, N_HEADS // 2),
        in_specs=[lane(0), lane(4), lane(8), pl.BlockSpec((t, 128), lambda i, p: (i, 12)),
                  dvec, dmat, dvec, dmat, pvec, pvec, pvec,
                  pl.BlockSpec((2, t, t), lambda i, p: (0, 0, 0)),
                  pl.BlockSpec((128, 128), lambda i, p: (0, 0))],
        out_specs=[gspec, gspec, pl.BlockSpec((t, 128), lambda i, p: (i, p))],
        compiler_params=_cparams(("parallel", "parallel")),
        name="rwkv_prep",
    )(rw, rw, rw, rw, w0.reshape(2, 1, BRANCH_W), _lora_ext(w_up, 0), a0.reshape(2, 1, BRANCH_W),
      _lora_ext(a_up, 64), k_k[None], k_a[None], r_k.reshape(1, BRANCH_W), tri, _block_sum_mat(128, 1.0))


def _rwkv_scan_body(gf_ref, af_ref, gb_ref, ab_ref, yf_ref, yb_ref, st_ref):
    @pl.when(pl.program_id(0) == 0)
    def _():
        st_ref[...] = jnp.zeros(st_ref.shape, F32)

    rowh = lax.broadcasted_iota(jnp.int32, (128, 128), 0) // HEAD_DIM
    diag = rowh == _lane_half((128, 128))
    for d, (g_ref, a_ref, y_ref) in enumerate(((gf_ref, af_ref, yf_ref), (gb_ref, ab_ref, yb_ref))):
        for p in range(N_HEADS // 2):
            ls = slice(p * 128, (p + 1) * 128)
            out = _dot3(g_ref[0, 0, :, ls], st_ref[d, p]) + a_ref[0, 0, :, ls]
            hn = out[:HEAD_DIM]
            st_ref[d, p] = jnp.where(diag, jnp.concatenate([hn, hn], axis=0), 0.0)
            y_ref[:, ls] = out[HEAD_DIM:]


def rwkv_scan(g, add):
    nch = g.shape[0]
    r = nch * CHUNK
    nctx = CTX_LEN // CHUNK
    rev = lambda c: jnp.where(c < nctx, nctx - 1 - c, nch + nctx - 1 - c)
    blk = (1, 1, HEAD_DIM + CHUNK, BRANCH_W)
    fwd = pl.BlockSpec(blk, lambda c: (c, 0, 0, 0))
    bwd = pl.BlockSpec(blk, lambda c: (rev(c), 1, 0, 0))
    yshape = jax.ShapeDtypeStruct((r, BRANCH_W), F32)
    return pl.pallas_call(
        _rwkv_scan_body,
        out_shape=[yshape, yshape],
        grid=(nch,),
        in_specs=[fwd, fwd, bwd, bwd],
        out_specs=[pl.BlockSpec((CHUNK, BRANCH_W), lambda c: (c, 0)),
                   pl.BlockSpec((CHUNK, BRANCH_W), lambda c: (rev(c), 0))],
        scratch_shapes=[pltpu.VMEM((2, N_HEADS // 2, 128, 128), F32)],
        compiler_params=_cparams(("arbitrary",)),
        name="rwkv_scan",
    )(g, add, g, add)


HY_WIDTH = 512
HY_ORDER = 2
HY_POS_BANDS = 16
HY_HIDDEN = 64
HY_TAPS_W = 2 * HY_ORDER * HY_WIDTH
FFT_N2 = ROW_TILE


def _dot3c(ah, al, b):
    bh, bl = _split(b)
    dg = functools.partial(jnp.dot, preferred_element_type=F32)
    return dg(ah, bh) + (dg(al, bh) + dg(ah, bl))


def _split_const(m):
    m = np.asarray(m, np.float32)
    hi = m.astype(BF16)
    lo = (m - hi.astype(np.float32)).astype(BF16)
    return jnp.asarray(hi), jnp.asarray(lo)


def _hy_taps_body(c2pb_ref, w1t_ref, w1c_ref, w1s_ref, b1_ref, w2_ref, b2_ref, w3_ref, b3_ref, absd_ref,
                  taps_ref, ssq_ref, *, length):
    i = pl.program_id(0)
    t_idx = (i * ROW_TILE + lax.broadcasted_iota(jnp.int32, (ROW_TILE, 1), 0)).astype(F32)
    t = t_idx / float(max(length - 1, 1))
    ang = c2pb_ref[...] * t_idx / float(length)
    pre = t * w1t_ref[...] + _dot3(jnp.cos(ang), w1c_ref[...]) - _dot3(jnp.sin(ang), w1s_ref[...]) + b1_ref[...]
    hid = jnp.sin(pre)
    hid = jnp.sin(_dot3(hid, w2_ref[...]) + b2_ref[...])
    taps = (_dot3(hid, w3_ref[...]) + b3_ref[...]) * jnp.exp(-t * absd_ref[...])
    taps_ref[...] = taps

    @pl.when(i == 0)
    def _():
        ssq_ref[...] = jnp.zeros(ssq_ref.shape, F32)

    ssq_ref[...] += jnp.sum(taps * taps, axis=0, keepdims=True)


def hyena_taps(length, w1, b1, w2, b2, w3, b3):
    bands = jnp.linspace(1e-4, HY_POS_BANDS - 1, HY_POS_BANDS, dtype=F32)
    c2pb = jnp.zeros((1, 128), F32).at[0, :HY_POS_BANDS].set(2.0 * math.pi * bands)
    pad = lambda m: jnp.zeros((128, HY_HIDDEN), F32).at[:HY_POS_BANDS].set(m)
    deltas = jnp.linspace(math.log(1e-2) / 0.3, math.log(1e-2) / 1.5, HY_WIDTH, dtype=F32)
    absd = jnp.tile(jnp.abs(deltas), 2 * HY_ORDER)[None]
    const = lambda shp: pl.BlockSpec(shp, lambda i: (0,) * len(shp))
    h = HY_HIDDEN
    taps, ssq = pl.pallas_call(
        functools.partial(_hy_taps_body, length=length),
        out_shape=[jax.ShapeDtypeStruct((length, HY_TAPS_W), F32), jax.ShapeDtypeStruct((1, HY_TAPS_W), F32)],
        grid=(length // ROW_TILE,),
        in_specs=[const((1, 128)), const((1, h)), const((128, h)), const((128, h)), const((1, h)), const((h, h)),
                  const((1, h)), const((h, HY_TAPS_W)), const((1, HY_TAPS_W)), const((1, HY_TAPS_W))],
        out_specs=[pl.BlockSpec((ROW_TILE, HY_TAPS_W), lambda i: (i, 0)), const((1, HY_TAPS_W))],
        compiler_params=_cparams(("arbitrary",)),
        name="hyena_taps",
    )(c2pb, w1[0:1], pad(w1[1:1 + HY_POS_BANDS]), pad(w1[1 + HY_POS_BANDS:]), b1[None], w2, b2[None], w3, b3[None], absd)
    hw = HY_TAPS_W // 2
    norm2 = ssq[:, :hw] + ssq[:, hw:] + 2.0 * taps[0:1, :hw] * taps[0:1, hw:]
    return taps, lax.rsqrt(norm2)


class _FftPlan:
    def __init__(self, length):
        self.length = length
        self.n = 2 * length
        self.n2 = FFT_N2
        self.n1 = self.n // self.n2
        self.n1h = self.n1 // 2
        k1 = self.n1h + 1
        self.k1p = -(-k1 // 8) * 8
        kk = np.arange(self.k1p)[:, None].astype(np.float64)
        live = (kk < k1)
        nn = np.arange(self.n1h)[None, :].astype(np.float64)
        th = 2.0 * np.pi * kk * nn / self.n1
        self.f1 = _split_const(np.concatenate([np.cos(th) * live, -np.sin(th) * live], axis=0))
        ck = np.where((kk == 0) | (kk == self.n1h), 1.0, 2.0) * live / self.n
        self.g1 = _split_const(np.concatenate([np.cos(th) * ck, -np.sin(th) * ck], axis=0).T)
        m = np.arange(self.n2).astype(np.float64)
        ph = 2.0 * np.pi * np.outer(m, m) / self.n2
        c, s = np.cos(ph), np.sin(ph)
        self.fb = _split_const(np.block([[c, s], [-s, c]]))
        self.fbi = _split_const(np.block([[c, -s], [s, c]]))
        tw = 2.0 * np.pi * kk[:, :, None] * m[None, :, None] / self.n
        self.twc = jnp.asarray(np.cos(tw), F32)
        self.tws = jnp.asarray(np.sin(tw), F32)


FFT_TN = 4096


def _fft_a_body(fh_ref, fl_ref, x_ref, o_ref):
    o_ref[...] = _dot3c(fh_ref[...], fl_ref[...], x_ref[...])


def fft_stage_a(plan, xf, lead):
    rows_in, m = xf.shape
    rows = 2 * plan.k1p
    tn = min(FFT_TN, m)
    fh, fl = (jnp.pad(f, ((0, 0), (lead, 0))) for f in plan.f1)
    fspec = pl.BlockSpec((rows, rows_in), lambda j: (0, 0))
    return pl.pallas_call(
        _fft_a_body,
        out_shape=jax.ShapeDtypeStruct((rows, m), F32),
        grid=(m // tn,),
        in_specs=[fspec, fspec, pl.BlockSpec((rows_in, tn), lambda j: (0, j))],
        out_specs=pl.BlockSpec((rows, tn), lambda j: (0, j)),
        compiler_params=_cparams(("parallel",)),
        name="fft_stage_a",
    )(fh, fl, xf)


def _twiddled(a_ref, twc_ref, tws_ref):
    are, aim = a_ref[0, 0], a_ref[1, 0]
    c, s = twc_ref[0], tws_ref[0]
    return jnp.concatenate([are * c + aim * s, aim * c - are * s], axis=0)


def _fft_filter_b_body(a_ref, twc_ref, tws_ref, fbh_ref, fbl_ref, scale_ref, o_ref):
    n2 = FFT_N2
    x = _dot3c(fbh_ref[...], fbl_ref[...], _twiddled(a_ref, twc_ref, tws_ref))
    hw = HY_TAPS_W // 2
    xre, xim = x[:n2], x[n2:]
    o_ref[0, 0] = (xre[:, :hw] + xre[:, hw:]) * scale_ref[...]
    o_ref[1, 0] = (xim[:, :hw] - xim[:, hw:]) * scale_ref[...]


def fft_filter_stage_b(plan, a, scale):
    n2, k1p = plan.n2, plan.k1p
    hw = HY_TAPS_W // 2
    const = lambda shp: pl.BlockSpec(shp, lambda k: (0,) * len(shp))
    return pl.pallas_call(
        _fft_filter_b_body,
        out_shape=jax.ShapeDtypeStruct((2, k1p, n2, hw), F32),
        grid=(k1p,),
        in_specs=[pl.BlockSpec((2, 1, n2, HY_TAPS_W), lambda k: (0, k, 0, 0)),
                  pl.BlockSpec((1, n2, 1), lambda k: (k, 0, 0)), pl.BlockSpec((1, n2, 1), lambda k: (k, 0, 0)),
                  const((2 * n2, 2 * n2)), const((2 * n2, 2 * n2)), const((1, hw))],
        out_specs=pl.BlockSpec((2, 1, n2, hw), lambda k: (0, k, 0, 0)),
        compiler_params=_cparams(("parallel",)),
        name="fft_filter_stage_b",
    )(a, plan.twc, plan.tws, plan.fb[0], plan.fb[1], scale)


def _fft_conv_b_body(a_ref, kf_ref, twc_ref, tws_ref, fbh_ref, fbl_ref, fih_ref, fil_ref, o_ref):
    n2 = FFT_N2
    z = _dot3c(fbh_ref[...], fbl_ref[...], _twiddled(a_ref, twc_ref, tws_ref))
    zre, zim = z[:n2], z[n2:]
    kre, kim = kf_ref[0, 0], kf_ref[1, 0]
    y = jnp.concatenate([zre * kre - zim * kim, zre * kim + zim * kre], axis=0)
    q = _dot3c(fih_ref[...], fil_ref[...], y)
    qre, qim = q[:n2], q[n2:]
    c, s = twc_ref[0], tws_ref[0]
    o_ref[0, 0] = qre * c - qim * s
    o_ref[1, 0] = qim * c + qre * s


def fft_conv_stage_b(plan, a, kf, order):
    n2, k1p = plan.n2, plan.k1p
    const = lambda shp: pl.BlockSpec(shp, lambda k: (0,) * len(shp))
    return pl.pallas_call(
        _fft_conv_b_body,
        out_shape=jax.ShapeDtypeStruct((2, k1p, n2, HY_WIDTH), F32),
        grid=(k1p,),
        in_specs=[pl.BlockSpec((2, 1, n2, HY_WIDTH), lambda k: (0, k, 0, 0)),
                  pl.BlockSpec((2, 1, n2, HY_WIDTH), lambda k: (0, k, 0, order)),
                  pl.BlockSpec((1, n2, 1), lambda k: (k, 0, 0)), pl.BlockSpec((1, n2, 1), lambda k: (k, 0, 0)),
                  const((2 * n2, 2 * n2)), const((2 * n2, 2 * n2)), const((2 * n2, 2 * n2)), const((2 * n2, 2 * n2))],
        out_specs=pl.BlockSpec((2, 1, n2, HY_WIDTH), lambda k: (0, k, 0, 0)),
        compiler_params=_cparams(("parallel",)),
        name="fft_conv_stage_b",
    )(a, kf, plan.twc, plan.tws, plan.fb[0], plan.fb[1], plan.fbi[0], plan.fbi[1])


def _fft_inv_a_body(gh_ref, gl_ref, q_ref, z_ref, gate_ref, skip_ref, o_ref):
    y = _dot3c(gh_ref[...], gl_ref[...], q_ref[...])
    o_ref[...] = gate_ref[...] * (y + z_ref[...] * skip_ref[...])


def fft_inv_stage_a(plan, qf, zf, gatef, skip_t, lead):
    rows_out, m = zf.shape
    rows = 2 * plan.k1p
    tn = skip_t.shape[1]
    gh, gl = (jnp.pad(g, ((lead, 0), (0, 0))) for g in plan.g1)
    col = lambda r_: pl.BlockSpec((r_, tn), lambda j: (0, j))
    gspec = pl.BlockSpec((rows_out, rows), lambda j: (0, 0))
    return pl.pallas_call(
        _fft_inv_a_body,
        out_shape=jax.ShapeDtypeStruct((rows_out, m), F32),
        grid=(m // tn,),
        in_specs=[gspec, gspec, col(rows), col(rows_out), col(rows_out), pl.BlockSpec((1, tn), lambda j: (0, 0))],
        out_specs=col(rows_out),
        compiler_params=_cparams(("parallel",)),
        name="fft_inv_stage_a",
    )(gh, gl, qf, zf, gatef, skip_t)


def hyena_long(hv, hx1, hx2, taps, scale, skip, lead):
    length = taps.shape[0]
    plan = _FftPlan(length)
    n2, k1p = plan.n2, plan.k1p
    m = n2 * HY_WIDTH
    tn = min(FFT_TN, m)
    flat = lambda a: a.reshape(lead + plan.n1h, m)
    ta = fft_stage_a(plan, taps.reshape(plan.n1h, n2 * HY_TAPS_W), 0)
    kf = fft_filter_stage_b(plan, ta.reshape(2, k1p, n2, HY_TAPS_W), scale)
    z = flat(hv)
    for o, gate in enumerate((hx1, hx2)):
        a = fft_stage_a(plan, z, lead)
        q = fft_conv_stage_b(plan, a.reshape(2, k1p, n2, HY_WIDTH), kf, o)
        z = fft_inv_stage_a(plan, q.reshape(2 * k1p, m), z, flat(gate), jnp.tile(skip[o], tn // HY_WIDTH)[None], lead)
    return z.reshape(-1, HY_WIDTH)


def _rec_out_body(x_ref, yf_ref, yb_ref, bonus_ref, sgr_ref, zh_ref, zc_ref, sgh_ref, gnw_ref, gnb_ref, bm_ref, wa_ref, wb_ref,
                  gate_ref, fin_ref, o_ref, *, final):
    y = yf_ref[...] + yb_ref[...]
    bm = bm_ref[...]
    mean = jnp.dot(y, bm, precision=HIGHEST, preferred_element_type=F32)
    yc = y - mean
    var = jnp.dot(yc * yc, bm, precision=HIGHEST, preferred_element_type=F32)
    yn = yc * lax.rsqrt(var + RWKV_GN_EPS) * gnw_ref[...] + gnb_ref[...]
    ya = ((yn + bonus_ref[...]) * sgr_ref[...]).astype(BF16)
    zh = zh_ref[...]
    if not final:
        zh = jnp.where(pl.program_id(0) == 0, zc_ref[...], zh)
    yh = (zh * sgh_ref[...]).astype(BF16)
    out = jnp.dot(ya, wa_ref[...], preferred_element_type=F32) + jnp.dot(yh, wb_ref[...], preferred_element_type=F32)
    xn = x_ref[...] + gate_ref[0] * out
    if final:
        xn = xn * lax.rsqrt(jnp.mean(xn * xn, axis=-1, keepdims=True) + NORM_EPS) * fin_ref[...]
    o_ref[...] = xn


RWKV_GN_EPS = 64e-5


def rec_out_proj(xs, yf, yb, bonus, sgr, zh, zc, sgh, gn_w, gn_b, w_out, gate2, final_g, *, final):
    r = xs.shape[0]
    d = D_MODEL
    off = 1 if final else 0
    nt = r // ROW_TILE - off
    rows = lambda w: pl.BlockSpec((ROW_TILE, w), lambda i: (i + off, 0))
    const = lambda shp: pl.BlockSpec(shp, lambda i: (0,) * len(shp))
    bw = BRANCH_W
    return pl.pallas_call(
        functools.partial(_rec_out_body, final=final),
        out_shape=jax.ShapeDtypeStruct((nt * ROW_TILE, d), F32),
        grid=(nt,),
        in_specs=[rows(d), rows(bw), rows(bw), rows(bw), rows(bw), rows(bw), const((CTX_LEN, bw)), rows(bw),
                  const((1, bw)), const((1, bw)),
                  const((bw, bw)), const((bw, d)), const((bw, d)),
                  pl.BlockSpec((1, 1, d), lambda i: (jnp.minimum(i + off, 1), 0, 0)), const((1, d))],
        out_specs=pl.BlockSpec((ROW_TILE, d), lambda i: (i, 0)),
        compiler_params=_cparams(("parallel",)),
        name="rec_out_proj",
    )(xs, yf, yb, bonus, sgr, zh, zc, sgh, gn_w[None], gn_b[None], _block_sum_mat(bw, 1.0 / HEAD_DIM),
      w_out[:bw].astype(BF16), w_out[bw:].astype(BF16), gate2, final_g[None])


def rec_layer(xs, mods, norm_g, w_in, mu, w0, w_up, a0, a_up, k_k, k_a, r_k, gn_w, gn_b, hy_short, hy_w1, hy_b1, hy_w2,
              hy_b2, hy_w3, hy_b3, hy_skip, w_out, final_g, final):
    shift2, scale2, gate2 = mods
    rw, hv, hx1, hx2, sgr, sgh = rec_in_proj(xs, norm_g, scale2, shift2, w_in, mu, hy_short)
    g, add, bonus = rwkv_prep(rw, w0, w_up, a0, a_up, k_k, k_a, r_k)
    yf, yb = rwkv_scan(g, add)
    fargs = (hy_w1, hy_b1, hy_w2, hy_b2, hy_w3, hy_b3)
    n = xs.shape[0] - CTX_LEN
    taps, scale = hyena_taps(n, *fargs)
    zh = hyena_long(hv, hx1, hx2, taps, scale, hy_skip, CTX_LEN // FFT_N2)
    if final:
        z_ctx = hv[:CTX_LEN]
    else:
        taps_c, scale_c = hyena_taps(CTX_LEN, *fargs)
        z_ctx = hyena_short(hv[:CTX_LEN], hx1[:CTX_LEN], hx2[:CTX_LEN], taps_c, scale_c, hy_skip)
    return rec_out_proj(xs, yf, yb, bonus, sgr, zh, z_ctx, sgh, gn_w, gn_b, w_out, gate2, final_g, final=final)


def kernel(x, c, ctx, c_ctx, attn_norm, attn_ada_w, attn_ada_b, attn_w_in, na_rpb, gqa_q_gain, gqa_k_gain, attn_w_out,
           rec_norm, rec_ada_w, rec_ada_b, rec_w_in, rwkv_mu, rwkv_w0, rwkv_w_up, rwkv_a0, rwkv_a_up, rwkv_k_k, rwkv_k_a,
           rwkv_r_k, rwkv_gn_w, rwkv_gn_b, hy_short, hy_w1, hy_b1, hy_w2, hy_b2, hy_w3, hy_b3, hy_skip, rec_w_out,
           final_norm):
    assert x.shape[0] == 1 and ctx.shape[1] == CTX_LEN and x.shape[2] == D_MODEL
    n = x.shape[1]
    assert n % ROW_TILE == 0 and n // ROW_TILE >= 3
    assert attn_w_in.shape[0] == rec_w_in.shape[0]
    d = D_MODEL
    cond8 = jnp.zeros((8, d), F32).at[0].set(c_ctx).at[1].set(c[0])
    m_attn = adaln_all(cond8, attn_ada_w, attn_ada_b)
    m_rec = adaln_all(cond8, rec_ada_w, rec_ada_b)
    mods = lambda m, i: tuple(m[i, :2, j * d:(j + 1) * d].reshape(2, 1, d) for j in range(3))
    cos_t, sin_t = _rope_tables(n)
    xs = jnp.concatenate([ctx[0], x[0]], axis=0)
    depth = attn_w_in.shape[0] + rec_w_in.shape[0]
    for layer in range(depth):
        i = layer // 2
        final = layer == depth - 1
        if layer % 2 == 0:
            xs = attn_layer(xs, mods(m_attn, i), attn_norm[i], attn_w_in[i], na_rpb[i], gqa_q_gain[i], gqa_k_gain[i],
                            attn_w_out[i], cos_t, sin_t, final_norm, final)
        else:
            xs = rec_layer(xs, mods(m_rec, i), rec_norm[i], rec_w_in[i], rwkv_mu[i], rwkv_w0[i], rwkv_w_up[i],
                           rwkv_a0[i], rwkv_a_up[i], rwkv_k_k[i], rwkv_k_a[i], rwkv_r_k[i], rwkv_gn_w[i], rwkv_gn_b[i],
                           hy_short[i], hy_w1[i], hy_b1[i], hy_w2[i], hy_b2[i], hy_w3[i], hy_b3[i], hy_skip[i],
                           rec_w_out[i], final_norm, final)
    return xs[None]


def _hy_short_body(fh_ref, fl_ref, gh_ref, gl_ref, v_ref, x1_ref, x2_ref, taps_ref, scale_ref, skip_ref, o_ref):
    fh, fl, gh, gl = fh_ref[...], fl_ref[...], gh_ref[...], gl_ref[...]
    kp = fh.shape[0] // 2
    hw = HY_TAPS_W // 2
    tf = _dot3c(fh, fl, taps_ref[...])
    kre = (tf[:kp, :hw] + tf[:kp, hw:]) * scale_ref[...]
    kim = (tf[kp:, :hw] - tf[kp:, hw:]) * scale_ref[...]
    z = v_ref[...]
    for o, gate_ref in enumerate((x1_ref, x2_ref)):
        ls = slice(o * HY_WIDTH, (o + 1) * HY_WIDTH)
        zf = _dot3c(fh, fl, z)
        zre, zim = zf[:kp], zf[kp:]
        y = jnp.concatenate([zre * kre[:, ls] - zim * kim[:, ls], zre * kim[:, ls] + zim * kre[:, ls]], axis=0)
        z = gate_ref[...] * (_dot3c(gh, gl, y) + z * skip_ref[o:o + 1])
    o_ref[...] = z


def hyena_short(hv, hx1, hx2, taps, scale, skip):
    length = hv.shape[0]
    n = 2 * length
    k1 = length + 1
    kp = -(-k1 // 8) * 8
    kk = np.arange(kp)[:, None].astype(np.float64)
    live = kk < k1
    th = 2.0 * np.pi * kk * np.arange(length)[None, :] / n
    f = _split_const(np.concatenate([np.cos(th) * live, -np.sin(th) * live], axis=0))
    ck = np.where((kk == 0) | (kk == length), 1.0, 2.0) * live / n
    g = _split_const(np.concatenate([np.cos(th) * ck, -np.sin(th) * ck], axis=0).T)
    return pl.pallas_call(
        _hy_short_body,
        out_shape=jax.ShapeDtypeStruct((length, HY_WIDTH), F32),
        compiler_params=pltpu.CompilerParams(vmem_limit_bytes=VMEM_LIMIT),
        name="hyena_short",
    )(f[0], f[1], g[0], g[1], hv, hx1, hx2, taps, scale, skip)
```

```python
import functools
import math

import jax
import jax.numpy as jnp
import numpy as np
from jax import lax
from jax.experimental import pallas as pl
from jax.experimental.pallas import tpu as pltpu

F32 = jnp.float32
BF16 = jnp.bfloat16
HIGHEST = lax.Precision.HIGHEST

D_MODEL = 1024
GRID_W = 64
CTX_LEN = 256
HEAD_DIM = 64
BRANCH_W = 512
N_HEADS = 8
GQA_KV_W = 128
NA_WIN_ROWS = 8
NA_WIN_COLS = 16
ROPE_THETA = 10000.0
ROPE_FREQS = 16
NORM_EPS = 1e-6
ROW_TILE = 256
NA_GROUP_ROWS = 4
NEG_BIG = -1e30
LOG2E = math.log2(math.e)
QK_SCALE = HEAD_DIM ** -0.5 * LOG2E
VMEM_LIMIT = 56 * 1024 * 1024

ATTN_SPLITS = (512, 512, 512, 512, 512, 128, 128, 512)
GQA_HEAD_ORDER = (0, 4, 1, 5, 2, 6, 3, 7)


def _cparams(sem):
    return pltpu.CompilerParams(dimension_semantics=sem, vmem_limit_bytes=VMEM_LIMIT)


def _silu(v):
    return v * (1.0 / (1.0 + jnp.exp(-v)))


def _lane_half(shape):
    return (lax.broadcasted_iota(jnp.int32, shape, len(shape) - 1) // HEAD_DIM) % 2


def _dot_nt(a, b):
    return lax.dot_general(a, b, (((1,), (1,)), ((), ())), preferred_element_type=F32)


def _adaln_body(cond_ref, w_ref, b_ref, o_ref):
    s = _silu(cond_ref[...])
    o_ref[0] = jnp.dot(s, w_ref[0], precision=HIGHEST, preferred_element_type=F32) + b_ref[0]


def adaln_all(cond8, ada_w, ada_b):
    nl = ada_w.shape[0]
    d = D_MODEL
    return pl.pallas_call(
        _adaln_body,
        out_shape=jax.ShapeDtypeStruct((nl, 8, 3 * d), F32),
        grid=(nl, 3),
        in_specs=[
            pl.BlockSpec((8, d), lambda l, j: (0, 0)),
            pl.BlockSpec((1, d, d), lambda l, j: (l, 0, j)),
            pl.BlockSpec((1, 1, d), lambda l, j: (l, 0, j)),
        ],
        out_specs=pl.BlockSpec((1, 8, d), lambda l, j: (l, 0, j)),
        compiler_params=_cparams(("parallel", "parallel")),
        name="adaln",
    )(cond8, ada_w, ada_b.reshape(nl, 1, 3 * d))


def _modulated(x_ref, g_ref, scale_ref, shift_ref):
    xf = x_ref[...]
    y = xf * lax.rsqrt(jnp.mean(xf * xf, axis=-1, keepdims=True) + NORM_EPS)
    return (y * g_ref[...]) * (1.0 + scale_ref[0]) + shift_ref[0]


def _mod_specs():
    d = D_MODEL
    return [
        pl.BlockSpec((ROW_TILE, d), lambda i: (i, 0)),
        pl.BlockSpec((1, d), lambda i: (0, 0)),
        pl.BlockSpec((1, 1, d), lambda i: (jnp.minimum(i, 1), 0, 0)),
        pl.BlockSpec((1, 1, d), lambda i: (jnp.minimum(i, 1), 0, 0)),
    ]


def _attn_in_body(x_ref, g_ref, scale_ref, shift_ref, w_ref, cos_ref, sin_ref, gq_ref, gqs_ref, gk_ref, gks_ref,
                  bdq_ref, bdk_ref,
                  qa_ref, ka_ref, va_ref, sga_ref, qb_ref, kb_ref, vb_ref, sgb_ref):
    xm = _modulated(x_ref, g_ref, scale_ref, shift_ref).astype(BF16)
    u = jnp.dot(xm, w_ref[...], preferred_element_type=F32)
    qa, ka, va, ga = u[:, 0:512], u[:, 512:1024], u[:, 1024:1536], u[:, 1536:2048]
    qb, kb, vb, gb = u[:, 2048:2560], u[:, 2560:2688], u[:, 2688:2816], u[:, 2816:3328]
    qbs, kbs = u[:, 3328:3840], u[:, 3840:3968]
    scale = QK_SCALE
    qa_ref[...] = (qa * scale).astype(BF16)
    ka_ref[...] = ka.astype(BF16)
    va_ref[...] = va.astype(BF16)
    sga_ref[...] = _silu(ga)
    sgb_ref[...] = _silu(gb)
    vb_ref[...] = jnp.transpose(vb).astype(BF16)
    cos_k, sin_k = cos_ref[...], sin_ref[...]
    cos_q = jnp.concatenate([cos_k] * 4, axis=1)
    sin_q = jnp.concatenate([sin_k] * 4, axis=1)
    rs_q = lax.rsqrt(jnp.dot(qb * qb, bdq_ref[...], precision=HIGHEST, preferred_element_type=F32) + NORM_EPS)
    rs_k = lax.rsqrt(jnp.dot(kb * kb, bdk_ref[...], precision=HIGHEST, preferred_element_type=F32) + NORM_EPS)
    qr = rs_q * (qb * gq_ref[...] * cos_q + qbs * gqs_ref[...] * sin_q)
    kr = rs_k * (kb * gk_ref[...] * cos_k + kbs * gks_ref[...] * sin_k)
    qb_ref[...] = jnp.transpose(qr * scale).astype(BF16)
    kb_ref[...] = kr.astype(BF16)


def _rope_tables(n):
    t = jnp.arange(n, dtype=jnp.int32)
    pos = jnp.stack([t // GRID_W, t % GRID_W], axis=-1).astype(F32)
    inv_freq = ROPE_THETA ** (-jnp.arange(ROPE_FREQS, dtype=F32) / ROPE_FREQS)
    ang = pos[:, :, None] * inv_freq
    c, s = jnp.cos(ang), jnp.sin(ang)
    cos64 = jnp.concatenate([c[:, 0], c[:, 0], c[:, 1], c[:, 1]], axis=-1)
    sin64 = jnp.concatenate([-s[:, 0], s[:, 0], -s[:, 1], s[:, 1]], axis=-1)
    cos64 = jnp.concatenate([jnp.ones((CTX_LEN, HEAD_DIM), F32), cos64], axis=0)
    sin64 = jnp.concatenate([jnp.zeros((CTX_LEN, HEAD_DIM), F32), sin64], axis=0)
    return jnp.tile(cos64, (1, 2)), jnp.tile(sin64, (1, 2))


def _reorder_heads(w, order, axis):
    take = lambda h: lax.slice_in_dim(w, h * HEAD_DIM, (h + 1) * HEAD_DIM, axis=axis)
    return jnp.concatenate([take(h) for h in order], axis=axis)


def _swap_rope_halves(w):
    shp = w.shape
    return jnp.flip(w.reshape(shp[:-1] + (shp[-1] // (2 * ROPE_FREQS), 2, ROPE_FREQS)), axis=-2).reshape(shp)


def attn_in_proj(xs, norm_g, scale2, shift2, w_in, q_gain, k_gain, cos_t, sin_t):
    r = xs.shape[0]
    d = D_MODEL
    parts, start = [], 0
    for s in ATTN_SPLITS:
        parts.append(w_in[:, start:start + s])
        start += s
    wqa, wka, wva, wga, wqb, wkb, wvb, wgb = parts
    wqb_p = _reorder_heads(wqb, GQA_HEAD_ORDER, 1)
    wgb_p = _reorder_heads(wgb, GQA_HEAD_ORDER, 1)
    wqb_sw = _swap_rope_halves(wqb_p)
    wkb_sw = _swap_rope_halves(wkb)
    w_ext = jnp.concatenate([wqa, wka, wva, wga, wqb_p, wkb, wvb, wgb_p, wqb_sw, wkb_sw], axis=1).astype(BF16)
    gq = jnp.tile(q_gain, N_HEADS)[None]
    gqs = jnp.tile(_swap_rope_halves(q_gain), N_HEADS)[None]
    gk = jnp.tile(k_gain, 2)[None]
    gks = jnp.tile(_swap_rope_halves(k_gain), 2)[None]
    bdq = jnp.asarray(np.kron(np.eye(N_HEADS), np.full((HEAD_DIM, HEAD_DIM), 1.0 / HEAD_DIM)), F32)
    bdk = jnp.asarray(np.kron(np.eye(2), np.full((HEAD_DIM, HEAD_DIM), 1.0 / HEAD_DIM)), F32)
    wcols = w_ext.shape[1]
    const = lambda shp: pl.BlockSpec(shp, lambda i: (0,) * len(shp))
    rows = lambda w: pl.BlockSpec((ROW_TILE, w), lambda i: (i, 0))
    out_shapes = [
        jax.ShapeDtypeStruct((r, 512), BF16), jax.ShapeDtypeStruct((r, 512), BF16), jax.ShapeDtypeStruct((r, 512), BF16),
        jax.ShapeDtypeStruct((r, 512), F32),
        jax.ShapeDtypeStruct((512, r), BF16), jax.ShapeDtypeStruct((r, 128), BF16), jax.ShapeDtypeStruct((128, r), BF16),
        jax.ShapeDtypeStruct((r, 512), F32),
    ]
    cols = lambda w: pl.BlockSpec((w, ROW_TILE), lambda i: (0, i))
    return pl.pallas_call(
        _attn_in_body,
        out_shape=out_shapes,
        grid=(r // ROW_TILE,),
        in_specs=_mod_specs() + [const((d, wcols)), rows(128), rows(128), const((1, 512)), const((1, 512)),
                                 const((1, 128)), const((1, 128)), const((512, 512)), const((128, 128))],
        out_specs=[rows(512), rows(512), rows(512), rows(512), cols(512), rows(128), cols(128), rows(512)],
        compiler_params=_cparams(("parallel",)),
        name="attn_in_proj",
    )(xs, norm_g[None], scale2, shift2, w_ext, cos_t, sin_t, gq, gqs, gk, gks, bdq, bdk)


def _na_cols_body(rpb_ref, sel_ref, neg_ref, o_ref):
    o_ref[...] = jnp.dot(rpb_ref[...], sel_ref[...], precision=HIGHEST, preferred_element_type=F32) + neg_ref[...]


def _na_bias_tables(rpb, rows):
    nrel_r, nrel_c = 2 * NA_WIN_ROWS - 1, 2 * NA_WIN_COLS - 1
    qc = np.arange(GRID_W)[:, None]
    kc = np.arange(GRID_W)[None, :]
    col0 = np.clip(qc - NA_WIN_COLS // 2, 0, GRID_W - NA_WIN_COLS)
    col_ok = (kc >= col0) & (kc < col0 + NA_WIN_COLS)
    rc = kc - qc + NA_WIN_COLS - 1
    sel = np.zeros((128, GRID_W * GRID_W), np.float32)
    sel[np.where(col_ok, rc, 127).reshape(-1), np.arange(GRID_W * GRID_W)] = col_ok.reshape(-1)
    neg = np.where(col_ok, 0.0, NEG_BIG).astype(np.float32).reshape(1, -1)
    rpb2 = jnp.zeros((128, 128), F32).at[:N_HEADS * nrel_r, :nrel_c].set(rpb.reshape(N_HEADS * nrel_r, nrel_c))
    cols = pl.pallas_call(
        _na_cols_body,
        out_shape=jax.ShapeDtypeStruct((128, GRID_W * GRID_W), F32),
        name="na_bias_cols",
    )(rpb2, jnp.asarray(sel), jnp.asarray(neg))
    cols = cols[:N_HEADS * nrel_r].reshape(N_HEADS, nrel_r, GRID_W, GRID_W)
    kh = min(NA_WIN_ROWS, rows)
    g = rows // NA_GROUP_ROWS
    cases = [(0, 0), (NA_GROUP_ROWS, 0), (rows - NA_GROUP_ROWS, NA_GROUP_ROWS * (g - 3))]
    masked = jnp.full((N_HEADS, GRID_W, GRID_W), NEG_BIG, F32)
    tabs = []
    for qr_first, start in cases:
        blocks = []
        for j in range(NA_GROUP_ROWS):
            qr = qr_first + j
            row0 = min(max(qr - kh // 2, 0), rows - kh)
            for i in range(3 * NA_GROUP_ROWS):
                kr = start + i
                blocks.append(cols[:, kr - qr + NA_WIN_ROWS - 1] if row0 <= kr < row0 + kh else masked)
        tab = jnp.stack(blocks, axis=1).reshape(N_HEADS, NA_GROUP_ROWS, 3 * NA_GROUP_ROWS, GRID_W, GRID_W)
        tabs.append(tab.transpose(0, 1, 3, 2, 4).reshape(N_HEADS, ROW_TILE, 3 * ROW_TILE))
    return jnp.stack(tabs) * LOG2E


def _na_body(q_ref, kc_ref, k0_ref, k1_ref, k2_ref, vc_ref, v0_ref, v1_ref, v2_ref, bias_ref, sg_ref, o_ref):
    half = _lane_half((ROW_TILE, 128))
    for hp in range(N_HEADS // 2):
        ls = slice(hp * 128, (hp + 1) * 128)
        qp = q_ref[:, ls]
        ks = [r[:, ls] for r in (k0_ref, k1_ref, k2_ref, kc_ref)]
        vs = [r[:, ls] for r in (v0_ref, v1_ref, v2_ref, vc_ref)]
        outs = []
        for j in range(2):
            qm = jnp.where(half == j, qp, jnp.zeros_like(qp))
            s = [_dot_nt(qm, k) for k in ks]
            s_win = jnp.concatenate(s[:3], axis=1) + bias_ref[0, 2 * hp + j]
            s_ctx = s[3]
            m = jnp.maximum(jnp.max(s_win, axis=1, keepdims=True), jnp.max(s_ctx, axis=1, keepdims=True))
            p_win = jnp.exp2(s_win - m)
            p_ctx = jnp.exp2(s_ctx - m)
            l = jnp.sum(p_win, axis=1, keepdims=True) + jnp.sum(p_ctx, axis=1, keepdims=True)
            o = jnp.dot(p_ctx.astype(BF16), vs[3], preferred_element_type=F32)
            for b in range(3):
                o += jnp.dot(p_win[:, b * ROW_TILE:(b + 1) * ROW_TILE].astype(BF16), vs[b], preferred_element_type=F32)
            outs.append(o / l)
        o_pair = jnp.where(half == 0, outs[0], outs[1])
        o_ref[:, ls] = (o_pair * sg_ref[:, ls]).astype(BF16)


def na_attention(qa, ka, va, sga, bias_tabs, n):
    g = n // ROW_TILE
    w = BRANCH_W

    def kv_spec(off):
        return pl.BlockSpec((ROW_TILE, w), lambda i: (jnp.clip(i - 1, 0, g - 3) + off + 1, 0))

    ctx_spec = pl.BlockSpec((ROW_TILE, w), lambda i: (0, 0))
    q_spec = pl.BlockSpec((ROW_TILE, w), lambda i: (i + 1, 0))
    case = lambda i: jnp.where(i == 0, 0, jnp.where(i == g - 1, 2, 1))
    bias_spec = pl.BlockSpec((1, N_HEADS, ROW_TILE, 3 * ROW_TILE), lambda i: (case(i), 0, 0, 0))
    return pl.pallas_call(
        _na_body,
        out_shape=jax.ShapeDtypeStruct((n, w), BF16),
        grid=(g,),
        in_specs=[q_spec, ctx_spec, kv_spec(0), kv_spec(1), kv_spec(2), ctx_spec, kv_spec(0), kv_spec(1), kv_spec(2),
                  bias_spec, q_spec],
        out_specs=pl.BlockSpec((ROW_TILE, w), lambda i: (i, 0)),
        compiler_params=_cparams(("parallel",)),
        name="na_attention",
    )(qa, ka, ka, ka, ka, va, va, va, va, bias_tabs, sga)


def _flash_mha_body(q_ref, k_ref, v_ref, sg_ref, o_ref, m_ref, acc_ref):
    kv = pl.program_id(1)
    tq = q_ref.shape[0]

    @pl.when(kv == 0)
    def _():
        m_ref[...] = jnp.full(m_ref.shape, NEG_BIG, F32)
        acc_ref[...] = jnp.zeros(acc_ref.shape, F32)

    khalf = _lane_half((k_ref.shape[0], 128))
    for p in range(N_HEADS // 2):
        ls = slice(p * 128, (p + 1) * 128)
        qp = q_ref[:, ls]
        kp = k_ref[:, ls]
        vp = v_ref[:, ls]
        for j in range(2):
            hh = 2 * p + j
            km = jnp.where(khalf == j, kp, jnp.zeros_like(kp))
            vm = jnp.where(khalf == j, vp, jnp.ones_like(vp))
            s = _dot_nt(qp, km)
            m_prev = m_ref[hh]
            m_new = jnp.maximum(m_prev, jnp.max(s, axis=1, keepdims=True))
            alpha = jnp.exp2(m_prev - m_new)
            pr = jnp.exp2(s - m_new[:, :1]).astype(BF16)
            acc_ref[hh] = alpha * acc_ref[hh] + jnp.dot(pr, vm, preferred_element_type=F32)
            m_ref[hh] = m_new

    @pl.when(kv == pl.num_programs(1) - 1)
    def _():
        half = _lane_half((tq, 128))
        for p in range(N_HEADS // 2):
            ls = slice(p * 128, (p + 1) * 128)
            a0, a1 = acc_ref[2 * p], acc_ref[2 * p + 1]
            o0 = a0 / pltpu.roll(a0, HEAD_DIM, 1)
            o1 = a1 / pltpu.roll(a1, HEAD_DIM, 1)
            o_ref[:, ls] = (jnp.where(half == 0, o0, o1) * sg_ref[:, ls]).astype(BF16)


def flash_mha(q, k, v, sg, *, q_block0, nq, tk, nk):
    tq = ROW_TILE
    return pl.pallas_call(
        _flash_mha_body,
        out_shape=jax.ShapeDtypeStruct((nq * tq, BRANCH_W), BF16),
        grid=(nq, nk),
        in_specs=[
            pl.BlockSpec((tq, BRANCH_W), lambda i, j: (i + q_block0, 0)),
            pl.BlockSpec((tk, BRANCH_W), lambda i, j: (j, 0)),
            pl.BlockSpec((tk, BRANCH_W), lambda i, j: (j, 0)),
            pl.BlockSpec((tq, BRANCH_W), lambda i, j: (i + q_block0, 0)),
        ],
        out_specs=pl.BlockSpec((tq, BRANCH_W), lambda i, j: (i, 0)),
        scratch_shapes=[pltpu.VMEM((N_HEADS, tq, 128), F32)] * 2,
        compiler_params=_cparams(("parallel", "arbitrary")),
        name="flash_mha",
    )(q, k, v, sg)


def _flash_gqa_body(qt_ref, k_ref, vt_ref, sg_ref, o_ref, m_ref, acc_ref):
    kv = pl.program_id(1)
    tq = qt_ref.shape[1]
    tk = k_ref.shape[0]

    @pl.when(kv == 0)
    def _():
        m_ref[...] = jnp.full(m_ref.shape, NEG_BIG, F32)
        acc_ref[...] = jnp.zeros(acc_ref.shape, F32)

    khalf = _lane_half((tk, 128))
    vhalf = lax.broadcasted_iota(jnp.int32, (128, tk), 0) // HEAD_DIM
    kb = k_ref[...]
    vt = vt_ref[...]
    kms = [jnp.where(khalf == j, kb, jnp.zeros_like(kb)) for j in range(2)]
    vms = [jnp.where(vhalf == j, vt, jnp.ones_like(vt)) for j in range(2)]

    def scores(hh):
        p, j = divmod(hh, 2)
        return jnp.dot(kms[j], qt_ref[p * 128:(p + 1) * 128, :], preferred_element_type=F32)

    st_next = scores(0)
    for hh in range(N_HEADS):
        st = st_next
        if hh + 1 < N_HEADS:
            st_next = scores(hh + 1)
        m_prev = m_ref[hh]
        m_new = jnp.maximum(m_prev, jnp.max(st, axis=0, keepdims=True))
        alpha = jnp.exp2(m_prev - m_new)
        pt = jnp.exp2(st - m_new[0:1]).astype(BF16)
        acc_ref[hh] = alpha[0:1] * acc_ref[hh] + jnp.dot(vms[hh % 2], pt, preferred_element_type=F32)
        m_ref[hh] = m_new

    @pl.when(kv == pl.num_programs(1) - 1)
    def _():
        for p in range(N_HEADS // 2):
            ls = slice(p * 128, (p + 1) * 128)
            a0, a1 = acc_ref[2 * p], acc_ref[2 * p + 1]
            ot = jnp.concatenate([a0[:HEAD_DIM] / a0[HEAD_DIM:], a1[HEAD_DIM:] / a1[:HEAD_DIM]], axis=0)
            o_ref[:, ls] = (jnp.transpose(ot) * sg_ref[:, ls]).astype(BF16)


def flash_gqa(qt, k, vt, sg, *, q_block0, nq, tk, nk):
    tq = ROW_TILE
    return pl.pallas_call(
        _flash_gqa_body,
        out_shape=jax.ShapeDtypeStruct((nq * tq, BRANCH_W), BF16),
        grid=(nq, nk),
        in_specs=[
            pl.BlockSpec((BRANCH_W, tq), lambda i, j: (0, i + q_block0)),
            pl.BlockSpec((tk, GQA_KV_W), lambda i, j: (j, 0)),
            pl.BlockSpec((GQA_KV_W, tk), lambda i, j: (0, j)),
            pl.BlockSpec((tq, BRANCH_W), lambda i, j: (i + q_block0, 0)),
        ],
        out_specs=pl.BlockSpec((tq, BRANCH_W), lambda i, j: (i, 0)),
        scratch_shapes=[pltpu.VMEM((N_HEADS, 8, tq), F32), pltpu.VMEM((N_HEADS, 128, tq), F32)],
        compiler_params=_cparams(("parallel", "arbitrary")),
        name="flash_gqa",
    )(qt, k, vt, sg)


def _out_body(x_ref, ya_ref, yb_ref, wa_ref, wb_ref, gate_ref, fin_ref, o_ref, *, final):
    y = jnp.dot(ya_ref[...], wa_ref[...], preferred_element_type=F32)
    y += jnp.dot(yb_ref[...], wb_ref[...], preferred_element_type=F32)
    xn = x_ref[...] + gate_ref[0] * y
    if final:
        xn = xn * lax.rsqrt(jnp.mean(xn * xn, axis=-1, keepdims=True) + NORM_EPS) * fin_ref[...]
    o_ref[...] = xn


def out_proj(xs, ya, yb, wa, wb, gate2, final_g, *, final):
    r = xs.shape[0]
    d = D_MODEL
    rows = lambda w: pl.BlockSpec((ROW_TILE, w), lambda i: (i, 0))
    const = lambda shp: pl.BlockSpec(shp, lambda i: (0,) * len(shp))
    return pl.pallas_call(
        functools.partial(_out_body, final=final),
        out_shape=jax.ShapeDtypeStruct((r, d), F32),
        grid=(r // ROW_TILE,),
        in_specs=[rows(d), rows(BRANCH_W), rows(BRANCH_W), const((BRANCH_W, d)), const((BRANCH_W, d)),
                  pl.BlockSpec((1, 1, d), lambda i: (jnp.minimum(i, 1), 0, 0)), const((1, d))],
        out_specs=rows(d),
        compiler_params=_cparams(("parallel",)),
        name="out_proj",
    )(xs, ya, yb, wa.astype(BF16), wb.astype(BF16), gate2, final_g[None])


def _kv_tile(r):
    nb = r // ROW_TILE
    best = max(k for k in range(1, 9) if nb % k == 0)
    return best * ROW_TILE, nb // best


def attn_layer(xs, mods, norm_g, w_in, rpb, q_gain, k_gain, w_out, cos_t, sin_t, final_g, final):
    r = xs.shape[0]
    n = r - CTX_LEN
    shift2, scale2, gate2 = mods
    qa, ka, va, sga, qbt, kb, vbt, sgb = attn_in_proj(xs, norm_g, scale2, shift2, w_in, q_gain, k_gain, cos_t, sin_t)
    bias_tabs = _na_bias_tables(rpb, n // GRID_W)
    ya_lat = na_attention(qa, ka, va, sga, bias_tabs, n)
    ya_ctx = flash_mha(qa, ka, va, sga, q_block0=0, nq=1, tk=CTX_LEN, nk=1)
    tk, nk = _kv_tile(r)
    yb_lat = flash_gqa(qbt, kb, vbt, sgb, q_block0=1, nq=n // ROW_TILE, tk=tk, nk=nk)
    yb_ctx = flash_gqa(qbt, kb, vbt, sgb, q_block0=0, nq=1, tk=CTX_LEN, nk=1)
    ya = jnp.concatenate([ya_ctx, ya_lat], axis=0)
    yb = jnp.concatenate([yb_ctx, yb_lat], axis=0)
    wb = _reorder_heads(w_out[BRANCH_W:], GQA_HEAD_ORDER, 0)
    return out_proj(xs, ya, yb, w_out[:BRANCH_W], wb, gate2, final_g, final=final)


def _split(a):
    hi = a.astype(BF16)
    return hi, (a - hi.astype(F32)).astype(BF16)


def _dot3(a, b, dims=(((1,), (0,)), ((), ()))):
    ah, al = _split(a)
    bh, bl = _split(b)
    dg = functools.partial(lax.dot_general, dimension_numbers=dims, preferred_element_type=F32)
    return dg(ah, bh) + (dg(al, bh) + dg(ah, bl))


def _dot1(a, b, dims=(((1,), (0,)), ((), ()))):
    return lax.dot_general(a.astype(BF16), b.astype(BF16), dims, preferred_element_type=F32)


_NT = (((1,), (1,)), ((), ()))
_TN = (((0,), (0,)), ((), ()))


RWKV_SHIFT_W = 1664
HY_IN_W = 1536
HALO = 8
REC_HALO_W = RWKV_SHIFT_W + HY_IN_W


def _rec_in_body(x_ref, xp_ref, xn_ref, g_ref, scale_ref, shift_ref, w_ref, mu_ref, taps_ref,
                 rw_ref, hv_ref, hx1_ref, hx2_ref, sgr_ref, sgh_ref, u_scr):
    i = pl.program_id(0)
    nt = pl.num_programs(0)
    xe = jnp.concatenate([xp_ref[...], x_ref[...], xn_ref[...]], axis=0)
    y = xe * lax.rsqrt(jnp.mean(xe * xe, axis=-1, keepdims=True) + NORM_EPS)
    xm = ((y * g_ref[...]) * (1.0 + scale_ref[0]) + shift_ref[0]).astype(BF16)
    u = jnp.dot(xm, w_ref[...], preferred_element_type=F32)
    row = lax.broadcasted_iota(jnp.int32, (ROW_TILE + 2 * HALO, 1), 0)
    keep = jnp.logical_and(jnp.logical_or(row >= HALO, i >= 2),
                           jnp.logical_or(row < ROW_TILE + HALO, jnp.logical_and(i >= 1, i < nt - 1)))
    u_scr[...] = jnp.where(keep, u[:, :REC_HALO_W], 0.0)
    up = u_scr[pl.ds(HALO - 1, ROW_TILE), :]
    uc = u_scr[pl.ds(HALO, ROW_TILE), :]
    un = u_scr[pl.ds(HALO + 1, ROW_TILE), :]
    w = RWKV_SHIFT_W
    rw_c = uc[:, :w]
    rw_ref[...] = rw_c + (0.5 * (up[:, :w] + un[:, :w]) - rw_c) * mu_ref[...]
    hy = up[:, w:] * taps_ref[0:1] + uc[:, w:] * taps_ref[1:2] + un[:, w:] * taps_ref[2:3]
    hv_ref[...] = hy[:, 0:512]
    hx1_ref[...] = hy[:, 512:1024]
    hx2_ref[...] = hy[:, 1024:1536]
    uc_all = u[HALO:HALO + ROW_TILE]
    sgr_ref[...] = _silu(uc_all[:, REC_HALO_W:REC_HALO_W + 512])
    sgh_ref[...] = _silu(uc_all[:, REC_HALO_W + 512:REC_HALO_W + 1024])


def rec_in_proj(xs, norm_g, scale2, shift2, w_in, mu, hy_short):
    r = xs.shape[0]
    d = D_MODEL
    w = RWKV_SHIFT_W
    w_ext = jnp.concatenate([w_in[:, :w], w_in[:, w + 512:w + 512 + HY_IN_W], w_in[:, w:w + 512],
                             w_in[:, w + 512 + HY_IN_W:]], axis=1).astype(BF16)
    nh = r // HALO
    per = ROW_TILE // HALO
    const = lambda shp: pl.BlockSpec(shp, lambda i: (0,) * len(shp))
    rows = lambda wd: pl.BlockSpec((ROW_TILE, wd), lambda i: (i, 0))
    f = lambda wd: jax.ShapeDtypeStruct((r, wd), F32)
    mod = _mod_specs()
    return pl.pallas_call(
        _rec_in_body,
        out_shape=[f(w), f(512), f(512), f(512), f(512), f(512)],
        grid=(r // ROW_TILE,),
        in_specs=[mod[0],
                  pl.BlockSpec((HALO, d), lambda i: (jnp.maximum(i * per - 1, 0), 0)),
                  pl.BlockSpec((HALO, d), lambda i: (jnp.minimum((i + 1) * per, nh - 1), 0)),
                  mod[1], mod[2], mod[3], const((d, w_ext.shape[1])), const((1, w)), const((3, HY_IN_W))],
        out_specs=[rows(w), rows(512), rows(512), rows(512), rows(512), rows(512)],
        scratch_shapes=[pltpu.VMEM((ROW_TILE + 2 * HALO, REC_HALO_W), F32)],
        compiler_params=_cparams(("parallel",)),
        name="rec_in_proj",
    )(xs, xs, xs, norm_g[None], scale2, shift2, w_ext, mu[None], hy_short)


CHUNK = 64
CPT = ROW_TILE // CHUNK


def _block_sum_mat(width, value):
    return jnp.asarray(np.kron(np.eye(width // HEAD_DIM), np.full((HEAD_DIM, HEAD_DIM), value)), F32)


def _rwkv_prep_body(r_ref, k_ref, v_ref, lora_ref, w0_ref, wup_ref, a0_ref, aup_ref, kk_ref, ka_ref, rk_ref,
                    tri_ref, bs_ref, g_ref, add_ref, bonus_ref):
    t = ROW_TILE
    r, k, v, lora = r_ref[...], k_ref[...], v_ref[...], lora_ref[...]
    bs = bs_ref[...]
    kk = k * kk_ref[...]
    kk = kk * lax.rsqrt(_dot3(kk * kk, bs) + 1e-12)
    tanh_lora = jnp.tanh(lora)
    row = lax.broadcasted_iota(jnp.int32, (t, t), 0)
    col = lax.broadcasted_iota(jnp.int32, (t, t), 1)
    same = ((row // CHUNK) == (col // CHUNK)).astype(F32)
    eye = (row == col).astype(F32)
    half = _lane_half((t, 128))
    half_c = _lane_half((HEAD_DIM, 128))
    rowc = lax.broadcasted_iota(jnp.int32, (HEAD_DIM, 128), 0)
    lanec = lax.broadcasted_iota(jnp.int32, (HEAD_DIM, 128), 1)
    level_masks = []
    bsz = 2
    while bsz < CHUNK:
        level_masks.append(jnp.logical_and((row // (2 * bsz)) == (col // (2 * bsz)), (row // bsz) != (col // bsz)))
        bsz *= 2
    first_mask = (row // 2) == (col // 2)

    dirs = []
    for d in range(2):
        tri = tri_ref[d]
        wl = w0_ref[d] + _dot3(tanh_lora, wup_ref[d])
        z = -wl
        w_log = -(jnp.maximum(z, 0.0) + jnp.log(1.0 + jnp.exp(-jnp.abs(z)))) - 0.5
        lw = -jnp.exp(w_log)
        a = 1.0 / (1.0 + jnp.exp(-(a0_ref[d] + _dot3(lora, aup_ref[d]))))
        kd = k * (1.0 + (a - 1.0) * ka_ref[...])
        b = kk * a
        incl = tri > 0.5
        cs = _dot3(tri, lw)
        tot = _dot3(same, lw)
        w_inv = jnp.exp(-cs)
        w_rest = jnp.exp(tot - cs)
        dirs.append(dict(kd=kd, incl=incl, strict=jnp.logical_and(incl, row != col), tot=tot,
                         kkt=kk * jnp.exp(cs - lw), kh=kd * w_inv, bh=b * w_inv, rt=r * jnp.exp(cs),
                         kdd=kd * w_rest, bdd=b * w_rest))
    chains = [(dd, j) for dd in dirs for j in range(2)]
    sels = [half == j for _, j in chains]
    bms = [jnp.where(sel, dd["bh"], 0.0) for (dd, _), sel in zip(chains, sels)]
    kms = [jnp.where(sel, dd["kh"], 0.0) for (dd, _), sel in zip(chains, sels)]
    l_bs = [jnp.where(dd["strict"], _dot3(dd["kkt"], bm, _NT), 0.0) for (dd, _), bm in zip(chains, bms)]
    tinvs = [eye - jnp.where(first_mask, l_b, 0.0) for l_b in l_bs]
    for mask in level_masks:
        xs = [_dot1(jnp.where(mask, l_b, 0.0), tinv) for l_b, tinv in zip(l_bs, tinvs)]
        tinvs = [tinv - _dot1(tinv, x) for tinv, x in zip(tinvs, xs)]
    resids = [eye - (tinv + _dot3(l_b, tinv)) for l_b, tinv in zip(l_bs, tinvs)]
    tinvs = [tinv + _dot1(tinv, resid) for tinv, resid in zip(tinvs, resids)]
    l_ks = [jnp.where(dd["strict"], _dot1(dd["kkt"], km, _NT), 0.0) for (dd, _), km in zip(chains, kms)]
    a_rks = [jnp.where(dd["incl"], _dot1(dd["rt"], km, _NT), 0.0) for (dd, _), km in zip(chains, kms)]
    a_rbs = [jnp.where(dd["incl"], _dot1(dd["rt"], bm, _NT), 0.0) for (dd, _), bm in zip(chains, bms)]
    ps = [_dot1(tinv, dd["kkt"]) for (dd, _), tinv in zip(chains, tinvs)]
    lvs = [_dot1(l_k, v) for l_k in l_ks]
    u0s = [_dot1(tinv, lv) for tinv, lv in zip(tinvs, lvs)]
    qs = [dd["rt"] - _dot1(a_rb, p) for (dd, _), a_rb, p in zip(chains, a_rbs, ps)]
    y0s = [_dot1(a_rk, v) - _dot1(a_rb, u0) for a_rk, a_rb, u0 in zip(a_rks, a_rbs, u0s)]

    sel0 = half == 0
    for d, dd in enumerate(dirs):
        p, u0, q, y0 = (jnp.where(sel0, x[2 * d], x[2 * d + 1]) for x in (ps, u0s, qs, y0s))
        for c in range(CPT):
            rs = slice(c * CHUNK, (c + 1) * CHUNK)
            x1 = _dot1(dd["bdd"][rs], p[rs], _TN)
            x2 = _dot1(dd["kdd"][rs], v[rs], _TN) - _dot1(dd["bdd"][rs], u0[rs], _TN)
            m_pair = jnp.where(half_c == 0, x1[:HEAD_DIM], x1[HEAD_DIM:])
            n_pair = jnp.where(half_c == 0, x2[:HEAD_DIM], x2[HEAD_DIM:])
            wc = jnp.exp(dd["tot"][c * CHUNK:c * CHUNK + 1])
            dg = jnp.where((lanec % HEAD_DIM) == rowc, wc, 0.0)
            g_ref[c, d, 0:HEAD_DIM, :] = dg - m_pair
            g_ref[c, d, HEAD_DIM:, :] = q[rs]
            add_ref[c, d, 0:HEAD_DIM, :] = n_pair
            add_ref[c, d, HEAD_DIM:, :] = y0[rs]

    kd_sum = dirs[0]["kd"] + dirs[1]["kd"]
    bonus_ref[...] = 0.5 * _dot3(r * kd_sum * rk_ref[...], bs) * v


def _lora_ext(up, first_row):
    out = jnp.zeros((2, 128, BRANCH_W), F32)
    for d in range(2):
        out = out.at[d, first_row + 32 * d:first_row + 32 * (d + 1)].set(up[d])
    return out


def rwkv_prep(rw, w0, w_up, a0, a_up, k_k, k_a, r_k):
    r = rw.shape[0]
    nt = r // ROW_TILE
    nch = r // CHUNK
    t = ROW_TILE
    ii = np.arange(t)
    same = (ii[:, None] // CHUNK) == (ii[None, :] // CHUNK)
    tri = jnp.asarray(np.stack([same & (ii[None, :] <= ii[:, None]), same & (ii[None, :] >= ii[:, None])]), F32)
    lane = lambda blk: pl.BlockSpec((t, 128), lambda i, p, blk=blk: (i, blk + p))
    pvec = pl.BlockSpec((1, 128), lambda i, p: (0, p))
    dvec = pl.BlockSpec((2, 1, 128), lambda i, p: (0, 0, p))
    dmat = pl.BlockSpec((2, 128, 128), lambda i, p: (0, 0, p))
    gspec = pl.BlockSpec((CPT, 2, HEAD_DIM + CHUNK, 128), lambda i, p: (i, 0, 0, p))
    gshape = jax.ShapeDtypeStruct((nch, 2, HEAD_DIM + CHUNK, BRANCH_W), F32)
    return pl.pallas_call(
        _rwkv_prep_body,
        out_shape=[gshape, gshape, jax.ShapeDtypeStruct((r, BRANCH_W), F32)],
        grid=(nt, N_HEADS // 2),
        in_specs=[lane(0), lane(4), lane(8), pl.BlockSpec((t, 128), lambda i, p: (i, 12)),
                  dvec, dmat, dvec, dmat, pvec, pvec, pvec,
                  pl.BlockSpec((2, t, t), lambda i, p: (0, 0, 0)),
                  pl.BlockSpec((128, 128), lambda i, p: (0, 0))],
        out_specs=[gspec, gspec, pl.BlockSpec((t, 128), lambda i, p: (i, p))],
        compiler_params=_cparams(("parallel", "parallel")),
        name="rwkv_prep",
    )(rw, rw, rw, rw, w0.reshape(2, 1, BRANCH_W), _lora_ext(w_up, 0), a0.reshape(2, 1, BRANCH_W),
      _lora_ext(a_up, 64), k_k[None], k_a[None], r_k.reshape(1, BRANCH_W), tri, _block_sum_mat(128, 1.0))


def _rwkv_scan_body(gf_ref, af_ref, gb_ref, ab_ref, yf_ref, yb_ref, st_ref):
    @pl.when(pl.program_id(0) == 0)
    def _():
        st_ref[...] = jnp.zeros(st_ref.shape, F32)

    rowh = lax.broadcasted_iota(jnp.int32, (128, 128), 0) // HEAD_DIM
    diag = rowh == _lane_half((128, 128))
    for d, (g_ref, a_ref, y_ref) in enumerate(((gf_ref, af_ref, yf_ref), (gb_ref, ab_ref, yb_ref))):
        for p in range(N_HEADS // 2):
            ls = slice(p * 128, (p + 1) * 128)
            out = _dot3(g_ref[0, 0, :, ls], st_ref[d, p]) + a_ref[0, 0, :, ls]
            hn = out[:HEAD_DIM]
            st_ref[d, p] = jnp.where(diag, jnp.concatenate([hn, hn], axis=0), 0.0)
            y_ref[:, ls] = out[HEAD_DIM:]


def rwkv_scan(g, add):
    nch = g.shape[0]
    r = nch * CHUNK
    nctx = CTX_LEN // CHUNK
    rev = lambda c: jnp.where(c < nctx, nctx - 1 - c, nch + nctx - 1 - c)
    blk = (1, 1, HEAD_DIM + CHUNK, BRANCH_W)
    fwd = pl.BlockSpec(blk, lambda c: (c, 0, 0, 0))
    bwd = pl.BlockSpec(blk, lambda c: (rev(c), 1, 0, 0))
    yshape = jax.ShapeDtypeStruct((r, BRANCH_W), F32)
    return pl.pallas_call(
        _rwkv_scan_body,
        out_shape=[yshape, yshape],
        grid=(nch,),
        in_specs=[fwd, fwd, bwd, bwd],
        out_specs=[pl.BlockSpec((CHUNK, BRANCH_W), lambda c: (c, 0)),
                   pl.BlockSpec((CHUNK, BRANCH_W), lambda c: (rev(c), 0))],
        scratch_shapes=[pltpu.VMEM((2, N_HEADS // 2, 128, 128), F32)],
        compiler_params=_cparams(("arbitrary",)),
        name="rwkv_scan",
    )(g, add, g, add)


HY_WIDTH = 512
HY_ORDER = 2
HY_POS_BANDS = 16
HY_HIDDEN = 64
HY_TAPS_W = 2 * HY_ORDER * HY_WIDTH
FFT_N2 = ROW_TILE


def _dot3c(ah, al, b):
    bh, bl = _split(b)
    dg = functools.partial(jnp.dot, preferred_element_type=F32)
    return dg(ah, bh) + (dg(al, bh) + dg(ah, bl))


def _split_const(m):
    m = np.asarray(m, np.float32)
    hi = m.astype(BF16)
    lo = (m - hi.astype(np.float32)).astype(BF16)
    return jnp.asarray(hi), jnp.asarray(lo)


def _hy_taps_body(c2pb_ref, w1t_ref, w1c_ref, w1s_ref, b1_ref, w2_ref, b2_ref, w3_ref, b3_ref, absd_ref,
                  taps_ref, ssq_ref, *, length):
    i = pl.program_id(0)
    t_idx = (i * ROW_TILE + lax.broadcasted_iota(jnp.int32, (ROW_TILE, 1), 0)).astype(F32)
    t = t_idx / float(max(length - 1, 1))
    ang = c2pb_ref[...] * t_idx / float(length)
    pre = t * w1t_ref[...] + _dot3(jnp.cos(ang), w1c_ref[...]) - _dot3(jnp.sin(ang), w1s_ref[...]) + b1_ref[...]
    hid = jnp.sin(pre)
    hid = jnp.sin(_dot3(hid, w2_ref[...]) + b2_ref[...])
    taps = (_dot3(hid, w3_ref[...]) + b3_ref[...]) * jnp.exp(-t * absd_ref[...])
    taps_ref[...] = taps

    @pl.when(i == 0)
    def _():
        ssq_ref[...] = jnp.zeros(ssq_ref.shape, F32)

    ssq_ref[...] += jnp.sum(taps * taps, axis=0, keepdims=True)


def hyena_taps(length, w1, b1, w2, b2, w3, b3):
    bands = jnp.linspace(1e-4, HY_POS_BANDS - 1, HY_POS_BANDS, dtype=F32)
    c2pb = jnp.zeros((1, 128), F32).at[0, :HY_POS_BANDS].set(2.0 * math.pi * bands)
    pad = lambda m: jnp.zeros((128, HY_HIDDEN), F32).at[:HY_POS_BANDS].set(m)
    deltas = jnp.linspace(math.log(1e-2) / 0.3, math.log(1e-2) / 1.5, HY_WIDTH, dtype=F32)
    absd = jnp.tile(jnp.abs(deltas), 2 * HY_ORDER)[None]
    const = lambda shp: pl.BlockSpec(shp, lambda i: (0,) * len(shp))
    h = HY_HIDDEN
    taps, ssq = pl.pallas_call(
        functools.partial(_hy_taps_body, length=length),
        out_shape=[jax.ShapeDtypeStruct((length, HY_TAPS_W), F32), jax.ShapeDtypeStruct((1, HY_TAPS_W), F32)],
        grid=(length // ROW_TILE,),
        in_specs=[const((1, 128)), const((1, h)), const((128, h)), const((128, h)), const((1, h)), const((h, h)),
                  const((1, h)), const((h, HY_TAPS_W)), const((1, HY_TAPS_W)), const((1, HY_TAPS_W))],
        out_specs=[pl.BlockSpec((ROW_TILE, HY_TAPS_W), lambda i: (i, 0)), const((1, HY_TAPS_W))],
        compiler_params=_cparams(("arbitrary",)),
        name="hyena_taps",
    )(c2pb, w1[0:1], pad(w1[1:1 + HY_POS_BANDS]), pad(w1[1 + HY_POS_BANDS:]), b1[None], w2, b2[None], w3, b3[None], absd)
    hw = HY_TAPS_W // 2
    norm2 = ssq[:, :hw] + ssq[:, hw:] + 2.0 * taps[0:1, :hw] * taps[0:1, hw:]
    return taps, lax.rsqrt(norm2)


class _FftPlan:
    def __init__(self, length):
        self.length = length
        self.n = 2 * length
        self.n2 = FFT_N2
        self.n1 = self.n // self.n2
        self.n1h = self.n1 // 2
        k1 = self.n1h + 1
        self.k1p = -(-k1 // 8) * 8
        kk = np.arange(self.k1p)[:, None].astype(np.float64)
        live = (kk < k1)
        nn = np.arange(self.n1h)[None, :].astype(np.float64)
        th = 2.0 * np.pi * kk * nn / self.n1
        self.f1 = _split_const(np.concatenate([np.cos(th) * live, -np.sin(th) * live], axis=0))
        ck = np.where((kk == 0) | (kk == self.n1h), 1.0, 2.0) * live / self.n
        self.g1 = _split_const(np.concatenate([np.cos(th) * ck, -np.sin(th) * ck], axis=0).T)
        m = np.arange(self.n2).astype(np.float64)
        ph = 2.0 * np.pi * np.outer(m, m) / self.n2
        c, s = np.cos(ph), np.sin(ph)
        self.fb = _split_const(np.block([[c, s], [-s, c]]))
        self.fbi = _split_const(np.block([[c, -s], [s, c]]))
        tw = 2.0 * np.pi * kk[:, :, None] * m[None, :, None] / self.n
        self.twc = jnp.asarray(np.cos(tw), F32)
        self.tws = jnp.asarray(np.sin(tw), F32)


FFT_TN = 4096


def _fft_a_body(fh_ref, fl_ref, x_ref, o_ref):
    o_ref[...] = _dot3c(fh_ref[...], fl_ref[...], x_ref[...])


def fft_stage_a(plan, xf, lead):
    rows_in, m = xf.shape
    rows = 2 * plan.k1p
    tn = min(FFT_TN, m)
    fh, fl = (jnp.pad(f, ((0, 0), (lead, 0))) for f in plan.f1)
    fspec = pl.BlockSpec((rows, rows_in), lambda j: (0, 0))
    return pl.pallas_call(
        _fft_a_body,
        out_shape=jax.ShapeDtypeStruct((rows, m), F32),
        grid=(m // tn,),
        in_specs=[fspec, fspec, pl.BlockSpec((rows_in, tn), lambda j: (0, j))],
        out_specs=pl.BlockSpec((rows, tn), lambda j: (0, j)),
        compiler_params=_cparams(("parallel",)),
        name="fft_stage_a",
    )(fh, fl, xf)


def _twiddled(a_ref, twc_ref, tws_ref):
    are, aim = a_ref[0, 0], a_ref[1, 0]
    c, s = twc_ref[0], tws_ref[0]
    return jnp.concatenate([are * c + aim * s, aim * c - are * s], axis=0)


def _fft_filter_b_body(a_ref, twc_ref, tws_ref, fbh_ref, fbl_ref, scale_ref, o_ref):
    n2 = FFT_N2
    x = _dot3c(fbh_ref[...], fbl_ref[...], _twiddled(a_ref, twc_ref, tws_ref))
    hw = HY_TAPS_W // 2
    xre, xim = x[:n2], x[n2:]
    o_ref[0, 0] = (xre[:, :hw] + xre[:, hw:]) * scale_ref[...]
    o_ref[1, 0] = (xim[:, :hw] - xim[:, hw:]) * scale_ref[...]


def fft_filter_stage_b(plan, a, scale):
    n2, k1p = plan.n2, plan.k1p
    hw = HY_TAPS_W // 2
    const = lambda shp: pl.BlockSpec(shp, lambda k: (0,) * len(shp))
    return pl.pallas_call(
        _fft_filter_b_body,
        out_shape=jax.ShapeDtypeStruct((2, k1p, n2, hw), F32),
        grid=(k1p,),
        in_specs=[pl.BlockSpec((2, 1, n2, HY_TAPS_W), lambda k: (0, k, 0, 0)),
                  pl.BlockSpec((1, n2, 1), lambda k: (k, 0, 0)), pl.BlockSpec((1, n2, 1), lambda k: (k, 0, 0)),
                  const((2 * n2, 2 * n2)), const((2 * n2, 2 * n2)), const((1, hw))],
        out_specs=pl.BlockSpec((2, 1, n2, hw), lambda k: (0, k, 0, 0)),
        compiler_params=_cparams(("parallel",)),
        name="fft_filter_stage_b",
    )(a, plan.twc, plan.tws, plan.fb[0], plan.fb[1], scale)


def _fft_conv_b_body(a_ref, kf_ref, twc_ref, tws_ref, fbh_ref, fbl_ref, fih_ref, fil_ref, o_ref):
    n2 = FFT_N2
    z = _dot3c(fbh_ref[...], fbl_ref[...], _twiddled(a_ref, twc_ref, tws_ref))
    zre, zim = z[:n2], z[n2:]
    kre, kim = kf_ref[0, 0], kf_ref[1, 0]
    y = jnp.concatenate([zre * kre - zim * kim, zre * kim + zim * kre], axis=0)
    q = _dot3c(fih_ref[...], fil_ref[...], y)
    qre, qim = q[:n2], q[n2:]
    c, s = twc_ref[0], tws_ref[0]
    o_ref[0, 0] = qre * c - qim * s
    o_ref[1, 0] = qim * c + qre * s


def fft_conv_stage_b(plan, a, kf, order):
    n2, k1p = plan.n2, plan.k1p
    const = lambda shp: pl.BlockSpec(shp, lambda k: (0,) * len(shp))
    return pl.pallas_call(
        _fft_conv_b_body,
        out_shape=jax.ShapeDtypeStruct((2, k1p, n2, HY_WIDTH), F32),
        grid=(k1p,),
        in_specs=[pl.BlockSpec((2, 1, n2, HY_WIDTH), lambda k: (0, k, 0, 0)),
                  pl.BlockSpec((2, 1, n2, HY_WIDTH), lambda k: (0, k, 0, order)),
                  pl.BlockSpec((1, n2, 1), lambda k: (k, 0, 0)), pl.BlockSpec((1, n2, 1), lambda k: (k, 0, 0)),
                  const((2 * n2, 2 * n2)), const((2 * n2, 2 * n2)), const((2 * n2, 2 * n2)), const((2 * n2, 2 * n2))],
        out_specs=pl.BlockSpec((2, 1, n2, HY_WIDTH), lambda k: (0, k, 0, 0)),
        compiler_params=_cparams(("parallel",)),
        name="fft_conv_stage_b",
    )(a, kf, plan.twc, plan.tws, plan.fb[0], plan.fb[1], plan.fbi[0], plan.fbi[1])


def _fft_inv_a_body(gh_ref, gl_ref, q_ref, z_ref, gate_ref, skip_ref, o_ref):
    y = _dot3c(gh_ref[...], gl_ref[...], q_ref[...])
    o_ref[...] = gate_ref[...] * (y + z_ref[...] * skip_ref[...])


def fft_inv_stage_a(plan, qf, zf, gatef, skip_t, lead):
    rows_out, m = zf.shape
    rows = 2 * plan.k1p
    tn = skip_t.shape[1]
    gh, gl = (jnp.pad(g, ((lead, 0), (0, 0))) for g in plan.g1)
    col = lambda r_: pl.BlockSpec((r_, tn), lambda j: (0, j))
    gspec = pl.BlockSpec((rows_out, rows), lambda j: (0, 0))
    return pl.pallas_call(
        _fft_inv_a_body,
        out_shape=jax.ShapeDtypeStruct((rows_out, m), F32),
        grid=(m // tn,),
        in_specs=[gspec, gspec, col(rows), col(rows_out), col(rows_out), pl.BlockSpec((1, tn), lambda j: (0, 0))],
        out_specs=col(rows_out),
        compiler_params=_cparams(("parallel",)),
        name="fft_inv_stage_a",
    )(gh, gl, qf, zf, gatef, skip_t)


def hyena_long(hv, hx1, hx2, taps, scale, skip, lead):
    length = taps.shape[0]
    plan = _FftPlan(length)
    n2, k1p = plan.n2, plan.k1p
    m = n2 * HY_WIDTH
    tn = min(FFT_TN, m)
    flat = lambda a: a.reshape(lead + plan.n1h, m)
    ta = fft_stage_a(plan, taps.reshape(plan.n1h, n2 * HY_TAPS_W), 0)
    kf = fft_filter_stage_b(plan, ta.reshape(2, k1p, n2, HY_TAPS_W), scale)
    z = flat(hv)
    for o, gate in enumerate((hx1, hx2)):
        a = fft_stage_a(plan, z, lead)
        q = fft_conv_stage_b(plan, a.reshape(2, k1p, n2, HY_WIDTH), kf, o)
        z = fft_inv_stage_a(plan, q.reshape(2 * k1p, m), z, flat(gate), jnp.tile(skip[o], tn // HY_WIDTH)[None], lead)
    return z.reshape(-1, HY_WIDTH)


def _rec_out_body(x_ref, yf_ref, yb_ref, bonus_ref, sgr_ref, zh_ref, zc_ref, sgh_ref, gnw_ref, gnb_ref, bm_ref, wa_ref, wb_ref,
                  gate_ref, fin_ref, o_ref, *, final):
    y = yf_ref[...] + yb_ref[...]
    bm = bm_ref[...]
    mean = jnp.dot(y, bm, precision=HIGHEST, preferred_element_type=F32)
    yc = y - mean
    var = jnp.dot(yc * yc, bm, precision=HIGHEST, preferred_element_type=F32)
    yn = yc * lax.rsqrt(var + RWKV_GN_EPS) * gnw_ref[...] + gnb_ref[...]
    ya = ((yn + bonus_ref[...]) * sgr_ref[...]).astype(BF16)
    zh = zh_ref[...]
    if not final:
        zh = jnp.where(pl.program_id(0) == 0, zc_ref[...], zh)
    yh = (zh * sgh_ref[...]).astype(BF16)
    out = jnp.dot(ya, wa_ref[...], preferred_element_type=F32) + jnp.dot(yh, wb_ref[...], preferred_element_type=F32)
    xn = x_ref[...] + gate_ref[0] * out
    if final:
        xn = xn * lax.rsqrt(jnp.mean(xn * xn, axis=-1, keepdims=True) + NORM_EPS) * fin_ref[...]
    o_ref[...] = xn


RWKV_GN_EPS = 64e-5


def rec_out_proj(xs, yf, yb, bonus, sgr, zh, zc, sgh, gn_w, gn_b, w_out, gate2, final_g, *, final):
    r = xs.shape[0]
    d = D_MODEL
    off = 1 if final else 0
    nt = r // ROW_TILE - off
    rows = lambda w: pl.BlockSpec((ROW_TILE, w), lambda i: (i + off, 0))
    const = lambda shp: pl.BlockSpec(shp, lambda i: (0,) * len(shp))
    bw = BRANCH_W
    return pl.pallas_call(
        functools.partial(_rec_out_body, final=final),
        out_shape=jax.ShapeDtypeStruct((nt * ROW_TILE, d), F32),
        grid=(nt,),
        in_specs=[rows(d), rows(bw), rows(bw), rows(bw), rows(bw), rows(bw), const((CTX_LEN, bw)), rows(bw),
                  const((1, bw)), const((1, bw)),
                  const((bw, bw)), const((bw, d)), const((bw, d)),
                  pl.BlockSpec((1, 1, d), lambda i: (jnp.minimum(i + off, 1), 0, 0)), const((1, d))],
        out_specs=pl.BlockSpec((ROW_TILE, d), lambda i: (i, 0)),
        compiler_params=_cparams(("parallel",)),
        name="rec_out_proj",
    )(xs, yf, yb, bonus, sgr, zh, zc, sgh, gn_w[None], gn_b[None], _block_sum_mat(bw, 1.0 / HEAD_DIM),
      w_out[:bw].astype(BF16), w_out[bw:].astype(BF16), gate2, final_g[None])


def rec_layer(xs, mods, norm_g, w_in, mu, w0, w_up, a0, a_up, k_k, k_a, r_k, gn_w, gn_b, hy_short, hy_w1, hy_b1, hy_w2,
              hy_b2, hy_w3, hy_b3, hy_skip, w_out, final_g, final):
    shift2, scale2, gate2 = mods
    rw, hv, hx1, hx2, sgr, sgh = rec_in_proj(xs, norm_g, scale2, shift2, w_in, mu, hy_short)
    g, add, bonus = rwkv_prep(rw, w0, w_up, a0, a_up, k_k, k_a, r_k)
    yf, yb = rwkv_scan(g, add)
    fargs = (hy_w1, hy_b1, hy_w2, hy_b2, hy_w3, hy_b3)
    n = xs.shape[0] - CTX_LEN
    taps, scale = hyena_taps(n, *fargs)
    zh = hyena_long(hv, hx1, hx2, taps, scale, hy_skip, CTX_LEN // FFT_N2)
    if final:
        z_ctx = hv[:CTX_LEN]
    else:
        taps_c, scale_c = hyena_taps(CTX_LEN, *fargs)
        z_ctx = hyena_short(hv[:CTX_LEN], hx1[:CTX_LEN], hx2[:CTX_LEN], taps_c, scale_c, hy_skip)
    return rec_out_proj(xs, yf, yb, bonus, sgr, zh, z_ctx, sgh, gn_w, gn_b, w_out, gate2, final_g, final=final)


def kernel(x, c, ctx, c_ctx, attn_norm, attn_ada_w, attn_ada_b, attn_w_in, na_rpb, gqa_q_gain, gqa_k_gain, attn_w_out,
           rec_norm, rec_ada_w, rec_ada_b, rec_w_in, rwkv_mu, rwkv_w0, rwkv_w_up, rwkv_a0, rwkv_a_up, rwkv_k_k, rwkv_k_a,
           rwkv_r_k, rwkv_gn_w, rwkv_gn_b, hy_short, hy_w1, hy_b1, hy_w2, hy_b2, hy_w3, hy_b3, hy_skip, rec_w_out,
           final_norm):
    assert x.shape[0] == 1 and ctx.shape[1] == CTX_LEN and x.shape[2] == D_MODEL
    n = x.shape[1]
    assert n % ROW_TILE == 0 and n // ROW_TILE >= 3
    assert attn_w_in.shape[0] == rec_w_in.shape[0]
    d = D_MODEL
    cond8 = jnp.zeros((8, d), F32).at[0].set(c_ctx).at[1].set(c[0])
    m_attn = adaln_all(cond8, attn_ada_w, attn_ada_b)
    m_rec = adaln_all(cond8, rec_ada_w, rec_ada_b)
    mods = lambda m, i: tuple(m[i, :2, j * d:(j + 1) * d].reshape(2, 1, d) for j in range(3))
    cos_t, sin_t = _rope_tables(n)
    xs = jnp.concatenate([ctx[0], x[0]], axis=0)
    depth = attn_w_in.shape[0] + rec_w_in.shape[0]
    for layer in range(depth):
        i = layer // 2
        final = layer == depth - 1
        if layer % 2 == 0:
            xs = attn_layer(xs, mods(m_attn, i), attn_norm[i], attn_w_in[i], na_rpb[i], gqa_q_gain[i], gqa_k_gain[i],
                            attn_w_out[i], cos_t, sin_t, final_norm, final)
        else:
            xs = rec_layer(xs, mods(m_rec, i), rec_norm[i], rec_w_in[i], rwkv_mu[i], rwkv_w0[i], rwkv_w_up[i],
                           rwkv_a0[i], rwkv_a_up[i], rwkv_k_k[i], rwkv_k_a[i], rwkv_r_k[i], rwkv_gn_w[i], rwkv_gn_b[i],
                           hy_short[i], hy_w1[i], hy_b1[i], hy_w2[i], hy_b2[i], hy_w3[i], hy_b3[i], hy_skip[i],
                           rec_w_out[i], final_norm, final)
    return xs[None]


def _hy_short_body(fh_ref, fl_ref, gh_ref, gl_ref, v_ref, x1_ref, x2_ref, taps_ref, scale_ref, skip_ref, o_ref):
    fh, fl, gh, gl = fh_ref[...], fl_ref[...], gh_ref[...], gl_ref[...]
    kp = fh.shape[0] // 2
    hw = HY_TAPS_W // 2
    tf = _dot3c(fh, fl, taps_ref[...])
    kre = (tf[:kp, :hw] + tf[:kp, hw:]) * scale_ref[...]
    kim = (tf[kp:, :hw] - tf[kp:, hw:]) * scale_ref[...]
    z = v_ref[...]
    for o, gate_ref in enumerate((x1_ref, x2_ref)):
        ls = slice(o * HY_WIDTH, (o + 1) * HY_WIDTH)
        zf = _dot3c(fh, fl, z)
        zre, zim = zf[:kp], zf[kp:]
        y = jnp.concatenate([zre * kre[:, ls] - zim * kim[:, ls], zre * kim[:, ls] + zim * kre[:, ls]], axis=0)
        z = gate_ref[...] * (_dot3c(gh, gl, y) + z * skip_ref[o:o + 1])
    o_ref[...] = z


def hyena_short(hv, hx1, hx2, taps, scale, skip):
    length = hv.shape[0]
    n = 2 * length
    k1 = length + 1
    kp = -(-k1 // 8) * 8
    kk = np.arange(kp)[:, None].astype(np.float64)
    live = kk < k1
    th = 2.0 * np.pi * kk * np.arange(length)[None, :] / n
    f = _split_const(np.concatenate([np.cos(th) * live, -np.sin(th) * live], axis=0))
    ck = np.where((kk == 0) | (kk == length), 1.0, 2.0) * live / n
    g = _split_const(np.concatenate([np.cos(th) * ck, -np.sin(th) * ck], axis=0).T)
    return pl.pallas_call(
        _hy_short_body,
        out_shape=jax.ShapeDtypeStruct((length, HY_WIDTH), F32),
        compiler_params=pltpu.CompilerParams(vmem_limit_bytes=VMEM_LIMIT),
        name="hyena_short",
    )(f[0], f[1], g[0], g[1], hv, hx1, hx2, taps, scale, skip)
```

```python
import functools
import math

import jax
import jax.numpy as jnp
import numpy as np
from jax import lax
from jax.experimental import pallas as pl
from jax.experimental.pallas import tpu as pltpu

F32 = jnp.float32
BF16 = jnp.bfloat16
HIGHEST = lax.Precision.HIGHEST

D_MODEL = 1024
GRID_W = 64
CTX_LEN = 256
HEAD_DIM = 64
BRANCH_W = 512
N_HEADS = 8
GQA_KV_W = 128
NA_WIN_ROWS = 8
NA_WIN_COLS = 16
ROPE_THETA = 10000.0
ROPE_FREQS = 16
NORM_EPS = 1e-6
ROW_TILE = 256
NA_GROUP_ROWS = 4
NEG_BIG = -1e30
LOG2E = math.log2(math.e)
QK_SCALE = HEAD_DIM ** -0.5 * LOG2E
VMEM_LIMIT = 56 * 1024 * 1024

ATTN_SPLITS = (512, 512, 512, 512, 512, 128, 128, 512)
GQA_HEAD_ORDER = (0, 4, 1, 5, 2, 6, 3, 7)


def _cparams(sem):
    return pltpu.CompilerParams(dimension_semantics=sem, vmem_limit_bytes=VMEM_LIMIT)


def _silu(v):
    return v * (1.0 / (1.0 + jnp.exp(-v)))


def _lane_half(shape):
    return (lax.broadcasted_iota(jnp.int32, shape, len(shape) - 1) // HEAD_DIM) % 2


def _dot_nt(a, b):
    return lax.dot_general(a, b, (((1,), (1,)), ((), ())), preferred_element_type=F32)


def _adaln_body(cond_ref, w_ref, b_ref, o_ref):
    s = _silu(cond_ref[...])
    o_ref[0] = jnp.dot(s, w_ref[0], precision=HIGHEST, preferred_element_type=F32) + b_ref[0]


def adaln_all(cond8, ada_w, ada_b):
    nl = ada_w.shape[0]
    d = D_MODEL
    return pl.pallas_call(
        _adaln_body,
        out_shape=jax.ShapeDtypeStruct((nl, 8, 3 * d), F32),
        grid=(nl, 3),
        in_specs=[
            pl.BlockSpec((8, d), lambda l, j: (0, 0)),
            pl.BlockSpec((1, d, d), lambda l, j: (l, 0, j)),
            pl.BlockSpec((1, 1, d), lambda l, j: (l, 0, j)),
        ],
        out_specs=pl.BlockSpec((1, 8, d), lambda l, j: (l, 0, j)),
        compiler_params=_cparams(("parallel", "parallel")),
        name="adaln",
    )(cond8, ada_w, ada_b.reshape(nl, 1, 3 * d))


def _modulated(x_ref, g_ref, scale_ref, shift_ref):
    xf = x_ref[...]
    y = xf * lax.rsqrt(jnp.mean(xf * xf, axis=-1, keepdims=True) + NORM_EPS)
    return (y * g_ref[...]) * (1.0 + scale_ref[0]) + shift_ref[0]


def _mod_specs():
    d = D_MODEL
    return [
        pl.BlockSpec((ROW_TILE, d), lambda i: (i, 0)),
        pl.BlockSpec((1, d), lambda i: (0, 0)),
        pl.BlockSpec((1, 1, d), lambda i: (jnp.minimum(i, 1), 0, 0)),
        pl.BlockSpec((1, 1, d), lambda i: (jnp.minimum(i, 1), 0, 0)),
    ]


def _attn_in_body(x_ref, g_ref, scale_ref, shift_ref, w_ref, cos_ref, sin_ref, gq_ref, gqs_ref, gk_ref, gks_ref,
                  bdq_ref, bdk_ref,
                  qa_ref, ka_ref, va_ref, sga_ref, qb_ref, kb_ref, vb_ref, sgb_ref):
    xm = _modulated(x_ref, g_ref, scale_ref, shift_ref).astype(BF16)
    u = jnp.dot(xm, w_ref[...], preferred_element_type=F32)
    qa, ka, va, ga = u[:, 0:512], u[:, 512:1024], u[:, 1024:1536], u[:, 1536:2048]
    qb, kb, vb, gb = u[:, 2048:2560], u[:, 2560:2688], u[:, 2688:2816], u[:, 2816:3328]
    qbs, kbs = u[:, 3328:3840], u[:, 3840:3968]
    scale = QK_SCALE
    qa_ref[...] = (qa * scale).astype(BF16)
    ka_ref[...] = ka.astype(BF16)
    va_ref[...] = va.astype(BF16)
    sga_ref[...] = _silu(ga)
    sgb_ref[...] = _silu(gb)
    vb_ref[...] = jnp.transpose(vb).astype(BF16)
    cos_k, sin_k = cos_ref[...], sin_ref[...]
    cos_q = jnp.concatenate([cos_k] * 4, axis=1)
    sin_q = jnp.concatenate([sin_k] * 4, axis=1)
    rs_q = lax.rsqrt(jnp.dot(qb * qb, bdq_ref[...], precision=HIGHEST, preferred_element_type=F32) + NORM_EPS)
    rs_k = lax.rsqrt(jnp.dot(kb * kb, bdk_ref[...], precision=HIGHEST, preferred_element_type=F32) + NORM_EPS)
    qr = rs_q * (qb * gq_ref[...] * cos_q + qbs * gqs_ref[...] * sin_q)
    kr = rs_k * (kb * gk_ref[...] * cos_k + kbs * gks_ref[...] * sin_k)
    qb_ref[...] = jnp.transpose(qr * scale).astype(BF16)
    kb_ref[...] = kr.astype(BF16)


def _rope_tables(n):
    t = jnp.arange(n, dtype=jnp.int32)
    pos = jnp.stack([t // GRID_W, t % GRID_W], axis=-1).astype(F32)
    inv_freq = ROPE_THETA ** (-jnp.arange(ROPE_FREQS, dtype=F32) / ROPE_FREQS)
    ang = pos[:, :, None] * inv_freq
    c, s = jnp.cos(ang), jnp.sin(ang)
    cos64 = jnp.concatenate([c[:, 0], c[:, 0], c[:, 1], c[:, 1]], axis=-1)
    sin64 = jnp.concatenate([-s[:, 0], s[:, 0], -s[:, 1], s[:, 1]], axis=-1)
    cos64 = jnp.concatenate([jnp.ones((CTX_LEN, HEAD_DIM), F32), cos64], axis=0)
    sin64 = jnp.concatenate([jnp.zeros((CTX_LEN, HEAD_DIM), F32), sin64], axis=0)
    return jnp.tile(cos64, (1, 2)), jnp.tile(sin64, (1, 2))


def _reorder_heads(w, order, axis):
    take = lambda h: lax.slice_in_dim(w, h * HEAD_DIM, (h + 1) * HEAD_DIM, axis=axis)
    return jnp.concatenate([take(h) for h in order], axis=axis)


def _swap_rope_halves(w):
    shp = w.shape
    return jnp.flip(w.reshape(shp[:-1] + (shp[-1] // (2 * ROPE_FREQS), 2, ROPE_FREQS)), axis=-2).reshape(shp)


def attn_in_proj(xs, norm_g, scale2, shift2, w_in, q_gain, k_gain, cos_t, sin_t):
    r = xs.shape[0]
    d = D_MODEL
    parts, start = [], 0
    for s in ATTN_SPLITS:
        parts.append(w_in[:, start:start + s])
        start += s
    wqa, wka, wva, wga, wqb, wkb, wvb, wgb = parts
    wqb_p = _reorder_heads(wqb, GQA_HEAD_ORDER, 1)
    wgb_p = _reorder_heads(wgb, GQA_HEAD_ORDER, 1)
    wqb_sw = _swap_rope_halves(wqb_p)
    wkb_sw = _swap_rope_halves(wkb)
    w_ext = jnp.concatenate([wqa, wka, wva, wga, wqb_p, wkb, wvb, wgb_p, wqb_sw, wkb_sw], axis=1).astype(BF16)
    gq = jnp.tile(q_gain, N_HEADS)[None]
    gqs = jnp.tile(_swap_rope_halves(q_gain), N_HEADS)[None]
    gk = jnp.tile(k_gain, 2)[None]
    gks = jnp.tile(_swap_rope_halves(k_gain), 2)[None]
    bdq = jnp.asarray(np.kron(np.eye(N_HEADS), np.full((HEAD_DIM, HEAD_DIM), 1.0 / HEAD_DIM)), F32)
    bdk = jnp.asarray(np.kron(np.eye(2), np.full((HEAD_DIM, HEAD_DIM), 1.0 / HEAD_DIM)), F32)
    wcols = w_ext.shape[1]
    const = lambda shp: pl.BlockSpec(shp, lambda i: (0,) * len(shp))
    rows = lambda w: pl.BlockSpec((ROW_TILE, w), lambda i: (i, 0))
    out_shapes = [
        jax.ShapeDtypeStruct((r, 512), BF16), jax.ShapeDtypeStruct((r, 512), BF16), jax.ShapeDtypeStruct((r, 512), BF16),
        jax.ShapeDtypeStruct((r, 512), F32),
        jax.ShapeDtypeStruct((512, r), BF16), jax.ShapeDtypeStruct((r, 128), BF16), jax.ShapeDtypeStruct((128, r), BF16),
        jax.ShapeDtypeStruct((r, 512), F32),
    ]
    cols = lambda w: pl.BlockSpec((w, ROW_TILE), lambda i: (0, i))
    return pl.pallas_call(
        _attn_in_body,
        out_shape=out_shapes,
        grid=(r // ROW_TILE,),
        in_specs=_mod_specs() + [const((d, wcols)), rows(128), rows(128), const((1, 512)), const((1, 512)),
                                 const((1, 128)), const((1, 128)), const((512, 512)), const((128, 128))],
        out_specs=[rows(512), rows(512), rows(512), rows(512), cols(512), rows(128), cols(128), rows(512)],
        compiler_params=_cparams(("parallel",)),
        name="attn_in_proj",
    )(xs, norm_g[None], scale2, shift2, w_ext, cos_t, sin_t, gq, gqs, gk, gks, bdq, bdk)


def _na_cols_body(rpb_ref, sel_ref, neg_ref, o_ref):
    o_ref[...] = jnp.dot(rpb_ref[...], sel_ref[...], precision=HIGHEST, preferred_element_type=F32) + neg_ref[...]


def _na_bias_tables(rpb, rows):
    nrel_r, nrel_c = 2 * NA_WIN_ROWS - 1, 2 * NA_WIN_COLS - 1
    qc = np.arange(GRID_W)[:, None]
    kc = np.arange(GRID_W)[None, :]
    col0 = np.clip(qc - NA_WIN_COLS // 2, 0, GRID_W - NA_WIN_COLS)
    col_ok = (kc >= col0) & (kc < col0 + NA_WIN_COLS)
    rc = kc - qc + NA_WIN_COLS - 1
    sel = np.zeros((128, GRID_W * GRID_W), np.float32)
    sel[np.where(col_ok, rc, 127).reshape(-1), np.arange(GRID_W * GRID_W)] = col_ok.reshape(-1)
    neg = np.where(col_ok, 0.0, NEG_BIG).astype(np.float32).reshape(1, -1)
    rpb2 = jnp.zeros((128, 128), F32).at[:N_HEADS * nrel_r, :nrel_c].set(rpb.reshape(N_HEADS * nrel_r, nrel_c))
    cols = pl.pallas_call(
        _na_cols_body,
        out_shape=jax.ShapeDtypeStruct((128, GRID_W * GRID_W), F32),
        name="na_bias_cols",
    )(rpb2, jnp.asarray(sel), jnp.asarray(neg))
    cols = cols[:N_HEADS * nrel_r].reshape(N_HEADS, nrel_r, GRID_W, GRID_W)
    kh = min(NA_WIN_ROWS, rows)
    g = rows // NA_GROUP_ROWS
    cases = [(0, 0), (NA_GROUP_ROWS, 0), (rows - NA_GROUP_ROWS, NA_GROUP_ROWS * (g - 3))]
    masked = jnp.full((N_HEADS, GRID_W, GRID_W), NEG_BIG, F32)
    tabs = []
    for qr_first, start in cases:
        blocks = []
        for j in range(NA_GROUP_ROWS):
            qr = qr_first + j
            row0 = min(max(qr - kh // 2, 0), rows - kh)
            for i in range(3 * NA_GROUP_ROWS):
                kr = start + i
                blocks.append(cols[:, kr - qr + NA_WIN_ROWS - 1] if row0 <= kr < row0 + kh else masked)
        tab = jnp.stack(blocks, axis=1).reshape(N_HEADS, NA_GROUP_ROWS, 3 * NA_GROUP_ROWS, GRID_W, GRID_W)
        tabs.append(tab.transpose(0, 1, 3, 2, 4).reshape(N_HEADS, ROW_TILE, 3 * ROW_TILE))
    return jnp.stack(tabs) * LOG2E


def _na_body(q_ref, kc_ref, k0_ref, k1_ref, k2_ref, vc_ref, v0_ref, v1_ref, v2_ref, bias_ref, sg_ref, o_ref):
    half = _lane_half((ROW_TILE, 128))
    for hp in range(N_HEADS // 2):
        ls = slice(hp * 128, (hp + 1) * 128)
        qp = q_ref[:, ls]
        ks = [r[:, ls] for r in (k0_ref, k1_ref, k2_ref, kc_ref)]
        vs = [r[:, ls] for r in (v0_ref, v1_ref, v2_ref, vc_ref)]
        outs = []
        for j in range(2):
            qm = jnp.where(half == j, qp, jnp.zeros_like(qp))
            s = [_dot_nt(qm, k) for k in ks]
            s_win = jnp.concatenate(s[:3], axis=1) + bias_ref[0, 2 * hp + j]
            s_ctx = s[3]
            m = jnp.maximum(jnp.max(s_win, axis=1, keepdims=True), jnp.max(s_ctx, axis=1, keepdims=True))
            p_win = jnp.exp2(s_win - m)
            p_ctx = jnp.exp2(s_ctx - m)
            l = jnp.sum(p_win, axis=1, keepdims=True) + jnp.sum(p_ctx, axis=1, keepdims=True)
            o = jnp.dot(p_ctx.astype(BF16), vs[3], preferred_element_type=F32)
            for b in range(3):
                o += jnp.dot(p_win[:, b * ROW_TILE:(b + 1) * ROW_TILE].astype(BF16), vs[b], preferred_element_type=F32)
            outs.append(o / l)
        o_pair = jnp.where(half == 0, outs[0], outs[1])
        o_ref[:, ls] = (o_pair * sg_ref[:, ls]).astype(BF16)


def na_attention(qa, ka, va, sga, bias_tabs, n):
    g = n // ROW_TILE
    w = BRANCH_W

    def kv_spec(off):
        return pl.BlockSpec((ROW_TILE, w), lambda i: (jnp.clip(i - 1, 0, g - 3) + off + 1, 0))

    ctx_spec = pl.BlockSpec((ROW_TILE, w), lambda i: (0, 0))
    q_spec = pl.BlockSpec((ROW_TILE, w), lambda i: (i + 1, 0))
    case = lambda i: jnp.where(i == 0, 0, jnp.where(i == g - 1, 2, 1))
    bias_spec = pl.BlockSpec((1, N_HEADS, ROW_TILE, 3 * ROW_TILE), lambda i: (case(i), 0, 0, 0))
    return pl.pallas_call(
        _na_body,
        out_shape=jax.ShapeDtypeStruct((n, w), BF16),
        grid=(g,),
        in_specs=[q_spec, ctx_spec, kv_spec(0), kv_spec(1), kv_spec(2), ctx_spec, kv_spec(0), kv_spec(1), kv_spec(2),
                  bias_spec, q_spec],
        out_specs=pl.BlockSpec((ROW_TILE, w), lambda i: (i, 0)),
        compiler_params=_cparams(("parallel",)),
        name="na_attention",
    )(qa, ka, ka, ka, ka, va, va, va, va, bias_tabs, sga)


def _flash_mha_body(q_ref, k_ref, v_ref, sg_ref, o_ref, m_ref, acc_ref):
    kv = pl.program_id(1)
    tq = q_ref.shape[0]

    @pl.when(kv == 0)
    def _():
        m_ref[...] = jnp.full(m_ref.shape, NEG_BIG, F32)
        acc_ref[...] = jnp.zeros(acc_ref.shape, F32)

    khalf = _lane_half((k_ref.shape[0], 128))
    for p in range(N_HEADS // 2):
        ls = slice(p * 128, (p + 1) * 128)
        qp = q_ref[:, ls]
        kp = k_ref[:, ls]
        vp = v_ref[:, ls]
        for j in range(2):
            hh = 2 * p + j
            km = jnp.where(khalf == j, kp, jnp.zeros_like(kp))
            vm = jnp.where(khalf == j, vp, jnp.ones_like(vp))
            s = _dot_nt(qp, km)
            m_prev = m_ref[hh]
            m_new = jnp.maximum(m_prev, jnp.max(s, axis=1, keepdims=True))
            alpha = jnp.exp2(m_prev - m_new)
            pr = jnp.exp2(s - m_new[:, :1]).astype(BF16)
            acc_ref[hh] = alpha * acc_ref[hh] + jnp.dot(pr, vm, preferred_element_type=F32)
            m_ref[hh] = m_new

    @pl.when(kv == pl.num_programs(1) - 1)
    def _():
        half = _lane_half((tq, 128))
        for p in range(N_HEADS // 2):
            ls = slice(p * 128, (p + 1) * 128)
            a0, a1 = acc_ref[2 * p], acc_ref[2 * p + 1]
            o0 = a0 / pltpu.roll(a0, HEAD_DIM, 1)
            o1 = a1 / pltpu.roll(a1, HEAD_DIM, 1)
            o_ref[:, ls] = (jnp.where(half == 0, o0, o1) * sg_ref[:, ls]).astype(BF16)


def flash_mha(q, k, v, sg, *, q_block0, nq, tk, nk):
    tq = ROW_TILE
    return pl.pallas_call(
        _flash_mha_body,
        out_shape=jax.ShapeDtypeStruct((nq * tq, BRANCH_W), BF16),
        grid=(nq, nk),
        in_specs=[
            pl.BlockSpec((tq, BRANCH_W), lambda i, j: (i + q_block0, 0)),
            pl.BlockSpec((tk, BRANCH_W), lambda i, j: (j, 0)),
            pl.BlockSpec((tk, BRANCH_W), lambda i, j: (j, 0)),
            pl.BlockSpec((tq, BRANCH_W), lambda i, j: (i + q_block0, 0)),
        ],
        out_specs=pl.BlockSpec((tq, BRANCH_W), lambda i, j: (i, 0)),
        scratch_shapes=[pltpu.VMEM((N_HEADS, tq, 128), F32)] * 2,
        compiler_params=_cparams(("parallel", "arbitrary")),
        name="flash_mha",
    )(q, k, v, sg)


def _flash_gqa_body(qt_ref, k_ref, vt_ref, sg_ref, o_ref, m_ref, acc_ref):
    kv = pl.program_id(1)
    tq = qt_ref.shape[1]
    tk = k_ref.shape[0]

    @pl.when(kv == 0)
    def _():
        m_ref[...] = jnp.full(m_ref.shape, NEG_BIG, F32)
        acc_ref[...] = jnp.zeros(acc_ref.shape, F32)

    khalf = _lane_half((tk, 128))
    vhalf = lax.broadcasted_iota(jnp.int32, (128, tk), 0) // HEAD_DIM
    kb = k_ref[...]
    vt = vt_ref[...]
    kms = [jnp.where(khalf == j, kb, jnp.zeros_like(kb)) for j in range(2)]
    vms = [jnp.where(vhalf == j, vt, jnp.ones_like(vt)) for j in range(2)]

    def scores(hh):
        p, j = divmod(hh, 2)
        return jnp.dot(kms[j], qt_ref[p * 128:(p + 1) * 128, :], preferred_element_type=F32)

    st_next = scores(0)
    for hh in range(N_HEADS):
        st = st_next
        if hh + 1 < N_HEADS:
            st_next = scores(hh + 1)
        m_prev = m_ref[hh]
        m_new = jnp.maximum(m_prev, jnp.max(st, axis=0, keepdims=True))
        alpha = jnp.exp2(m_prev - m_new)
        pt = jnp.exp2(st - m_new[0:1]).astype(BF16)
        acc_ref[hh] = alpha[0:1] * acc_ref[hh] + jnp.dot(vms[hh % 2], pt, preferred_element_type=F32)
        m_ref[hh] = m_new

    @pl.when(kv == pl.num_programs(1) - 1)
    def _():
        for p in range(N_HEADS // 2):
            ls = slice(p * 128, (p + 1) * 128)
            a0, a1 = acc_ref[2 * p], acc_ref[2 * p + 1]
            ot = jnp.concatenate([a0[:HEAD_DIM] / a0[HEAD_DIM:], a1[HEAD_DIM:] / a1[:HEAD_DIM]], axis=0)
            o_ref[:, ls] = (jnp.transpose(ot) * sg_ref[:, ls]).astype(BF16)


def flash_gqa(qt, k, vt, sg, *, q_block0, nq, tk, nk):
    tq = ROW_TILE
    return pl.pallas_call(
        _flash_gqa_body,
        out_shape=jax.ShapeDtypeStruct((nq * tq, BRANCH_W), BF16),
        grid=(nq, nk),
        in_specs=[
            pl.BlockSpec((BRANCH_W, tq), lambda i, j: (0, i + q_block0)),
            pl.BlockSpec((tk, GQA_KV_W), lambda i, j: (j, 0)),
            pl.BlockSpec((GQA_KV_W, tk), lambda i, j: (0, j)),
            pl.BlockSpec((tq, BRANCH_W), lambda i, j: (i + q_block0, 0)),
        ],
        out_specs=pl.BlockSpec((tq, BRANCH_W), lambda i, j: (i, 0)),
        scratch_shapes=[pltpu.VMEM((N_HEADS, 8, tq), F32), pltpu.VMEM((N_HEADS, 128, tq), F32)],
        compiler_params=_cparams(("parallel", "arbitrary")),
        name="flash_gqa",
    )(qt, k, vt, sg)


def _out_body(x_ref, ya_ref, yac_ref, yb_ref, ybc_ref, wa_ref, wb_ref, gate_ref, o_ref):
    is_ctx = pl.program_id(0) == 0
    ya = jnp.where(is_ctx, yac_ref[...], ya_ref[...])
    yb = jnp.where(is_ctx, ybc_ref[...], yb_ref[...])
    y = jnp.dot(ya, wa_ref[...], preferred_element_type=F32)
    y += jnp.dot(yb, wb_ref[...], preferred_element_type=F32)
    o_ref[...] = x_ref[...] + gate_ref[0] * y


def out_proj(xs, ya_lat, ya_ctx, yb_lat, yb_ctx, wa, wb, gate2):
    r = xs.shape[0]
    d = D_MODEL
    rows = lambda w: pl.BlockSpec((ROW_TILE, w), lambda i: (i, 0))
    lat = pl.BlockSpec((ROW_TILE, BRANCH_W), lambda i: (jnp.maximum(i - 1, 0), 0))
    const = lambda shp: pl.BlockSpec(shp, lambda i: (0,) * len(shp))
    ctx = const((CTX_LEN, BRANCH_W))
    return pl.pallas_call(
        _out_body,
        out_shape=jax.ShapeDtypeStruct((r, d), F32),
        grid=(r // ROW_TILE,),
        in_specs=[rows(d), lat, ctx, lat, ctx, const((BRANCH_W, d)), const((BRANCH_W, d)),
                  pl.BlockSpec((1, 1, d), lambda i: (jnp.minimum(i, 1), 0, 0))],
        out_specs=rows(d),
        compiler_params=_cparams(("parallel",)),
        name="out_proj",
    )(xs, ya_lat, ya_ctx, yb_lat, yb_ctx, wa.astype(BF16), wb.astype(BF16), gate2)


KV_TILE_MAX_BLOCKS = 13


def _kv_tile(r):
    nb = r // ROW_TILE
    best = max(k for k in range(1, KV_TILE_MAX_BLOCKS + 1) if nb % k == 0)
    return best * ROW_TILE, nb // best


def attn_layer(xs, mods, norm_g, w_in, rpb, q_gain, k_gain, w_out, cos_t, sin_t):
    r = xs.shape[0]
    n = r - CTX_LEN
    shift2, scale2, gate2 = mods
    qa, ka, va, sga, qbt, kb, vbt, sgb = attn_in_proj(xs, norm_g, scale2, shift2, w_in, q_gain, k_gain, cos_t, sin_t)
    bias_tabs = _na_bias_tables(rpb, n // GRID_W)
    ya_lat = na_attention(qa, ka, va, sga, bias_tabs, n)
    ya_ctx = flash_mha(qa, ka, va, sga, q_block0=0, nq=1, tk=CTX_LEN, nk=1)
    tk, nk = _kv_tile(r)
    yb_lat = flash_gqa(qbt, kb, vbt, sgb, q_block0=1, nq=n // ROW_TILE, tk=tk, nk=nk)
    yb_ctx = flash_gqa(qbt, kb, vbt, sgb, q_block0=0, nq=1, tk=CTX_LEN, nk=1)
    wb = _reorder_heads(w_out[BRANCH_W:], GQA_HEAD_ORDER, 0)
    return out_proj(xs, ya_lat, ya_ctx, yb_lat, yb_ctx, w_out[:BRANCH_W], wb, gate2)


def _split(a):
    hi = a.astype(BF16)
    return hi, (a - hi.astype(F32)).astype(BF16)


def _dot3(a, b, dims=(((1,), (0,)), ((), ()))):
    ah, al = _split(a)
    bh, bl = _split(b)
    dg = functools.partial(lax.dot_general, dimension_numbers=dims, preferred_element_type=F32)
    return dg(ah, bh) + (dg(al, bh) + dg(ah, bl))


def _dot1(a, b, dims=(((1,), (0,)), ((), ()))):
    return lax.dot_general(a.astype(BF16), b.astype(BF16), dims, preferred_element_type=F32)


_NT = (((1,), (1,)), ((), ()))
_TN = (((0,), (0,)), ((), ()))


RWKV_SHIFT_W = 1664
HY_IN_W = 1536
HALO = 8
REC_HALO_W = RWKV_SHIFT_W + HY_IN_W


def _rec_in_body(x_ref, xp_ref, xn_ref, g_ref, scale_ref, shift_ref, w_ref, mu_ref, taps_ref,
                 rw_ref, hv_ref, hx1_ref, hx2_ref, sgr_ref, sgh_ref, u_scr):
    i = pl.program_id(0)
    nt = pl.num_programs(0)
    xe = jnp.concatenate([xp_ref[...], x_ref[...], xn_ref[...]], axis=0)
    y = xe * lax.rsqrt(jnp.mean(xe * xe, axis=-1, keepdims=True) + NORM_EPS)
    xm = ((y * g_ref[...]) * (1.0 + scale_ref[0]) + shift_ref[0]).astype(BF16)
    u = jnp.dot(xm, w_ref[...], preferred_element_type=F32)
    row = lax.broadcasted_iota(jnp.int32, (ROW_TILE + 2 * HALO, 1), 0)
    keep = jnp.logical_and(jnp.logical_or(row >= HALO, i >= 2),
                           jnp.logical_or(row < ROW_TILE + HALO, jnp.logical_and(i >= 1, i < nt - 1)))
    u_scr[...] = jnp.where(keep, u[:, :REC_HALO_W], 0.0)
    up = u_scr[pl.ds(HALO - 1, ROW_TILE), :]
    uc = u_scr[pl.ds(HALO, ROW_TILE), :]
    un = u_scr[pl.ds(HALO + 1, ROW_TILE), :]
    w = RWKV_SHIFT_W
    rw_c = uc[:, :w]
    rw_ref[...] = rw_c + (0.5 * (up[:, :w] + un[:, :w]) - rw_c) * mu_ref[...]
    hy = up[:, w:] * taps_ref[0:1] + uc[:, w:] * taps_ref[1:2] + un[:, w:] * taps_ref[2:3]
    hv_ref[...] = hy[:, 0:512]
    hx1_ref[...] = hy[:, 512:1024]
    hx2_ref[...] = hy[:, 1024:1536]
    uc_all = u[HALO:HALO + ROW_TILE]
    sgr_ref[...] = _silu(uc_all[:, REC_HALO_W:REC_HALO_W + 512])
    sgh_ref[...] = _silu(uc_all[:, REC_HALO_W + 512:REC_HALO_W + 1024])


def rec_in_proj(xs, norm_g, scale2, shift2, w_in, mu, hy_short):
    r = xs.shape[0]
    d = D_MODEL
    w = RWKV_SHIFT_W
    w_ext = jnp.concatenate([w_in[:, :w], w_in[:, w + 512:w + 512 + HY_IN_W], w_in[:, w:w + 512],
                             w_in[:, w + 512 + HY_IN_W:]], axis=1).astype(BF16)
    nh = r // HALO
    per = ROW_TILE // HALO
    const = lambda shp: pl.BlockSpec(shp, lambda i: (0,) * len(shp))
    rows = lambda wd: pl.BlockSpec((ROW_TILE, wd), lambda i: (i, 0))
    f = lambda wd: jax.ShapeDtypeStruct((r, wd), F32)
    mod = _mod_specs()
    return pl.pallas_call(
        _rec_in_body,
        out_shape=[f(w), f(512), f(512), f(512), f(512), f(512)],
        grid=(r // ROW_TILE,),
        in_specs=[mod[0],
                  pl.BlockSpec((HALO, d), lambda i: (jnp.maximum(i * per - 1, 0), 0)),
                  pl.BlockSpec((HALO, d), lambda i: (jnp.minimum((i + 1) * per, nh - 1), 0)),
                  mod[1], mod[2], mod[3], const((d, w_ext.shape[1])), const((1, w)), const((3, HY_IN_W))],
        out_specs=[rows(w), rows(512), rows(512), rows(512), rows(512), rows(512)],
        scratch_shapes=[pltpu.VMEM((ROW_TILE + 2 * HALO, REC_HALO_W), F32)],
        compiler_params=_cparams(("parallel",)),
        name="rec_in_proj",
    )(xs, xs, xs, norm_g[None], scale2, shift2, w_ext, mu[None], hy_short)


CHUNK = 64
CPT = ROW_TILE // CHUNK


def _block_sum_mat(width, value):
    return jnp.asarray(np.kron(np.eye(width // HEAD_DIM), np.full((HEAD_DIM, HEAD_DIM), value)), F32)


def _rwkv_prep_body(r_ref, k_ref, v_ref, lora_ref, w0_ref, wup_ref, a0_ref, aup_ref, kk_ref, ka_ref, rk_ref,
                    tri_ref, bs_ref, g_ref, add_ref, bonus_ref):
    t = ROW_TILE
    r, k, v, lora = r_ref[...], k_ref[...], v_ref[...], lora_ref[...]
    bs = bs_ref[...]
    kk = k * kk_ref[...]
    kk = kk * lax.rsqrt(_dot3(kk * kk, bs) + 1e-12)
    tanh_lora = jnp.tanh(lora)
    row = lax.broadcasted_iota(jnp.int32, (t, t), 0)
    col = lax.broadcasted_iota(jnp.int32, (t, t), 1)
    same = ((row // CHUNK) == (col // CHUNK)).astype(F32)
    eye = (row == col).astype(F32)
    half = _lane_half((t, 128))
    half_c = _lane_half((HEAD_DIM, 128))
    rowc = lax.broadcasted_iota(jnp.int32, (HEAD_DIM, 128), 0)
    lanec = lax.broadcasted_iota(jnp.int32, (HEAD_DIM, 128), 1)
    level_masks = []
    bsz = 2
    while bsz < CHUNK:
        level_masks.append(jnp.logical_and((row // (2 * bsz)) == (col // (2 * bsz)), (row // bsz) != (col // bsz)))
        bsz *= 2
    first_mask = (row // 2) == (col // 2)

    dirs = []
    for d in range(2):
        tri = tri_ref[d]
        wl = w0_ref[d] + _dot3(tanh_lora, wup_ref[d])
        z = -wl
        w_log = -(jnp.maximum(z, 0.0) + jnp.log(1.0 + jnp.exp(-jnp.abs(z)))) - 0.5
        lw = -jnp.exp(w_log)
        a = 1.0 / (1.0 + jnp.exp(-(a0_ref[d] + _dot3(lora, aup_ref[d]))))
        kd = k * (1.0 + (a - 1.0) * ka_ref[...])
        b = kk * a
        incl = tri > 0.5
        cs = _dot3(tri, lw)
        tot = _dot3(same, lw)
        w_inv = jnp.exp(-cs)
        w_rest = jnp.exp(tot - cs)
        dirs.append(dict(kd=kd, incl=incl, strict=jnp.logical_and(incl, row != col), tot=tot,
                         kkt=kk * jnp.exp(cs - lw), kh=kd * w_inv, bh=b * w_inv, rt=r * jnp.exp(cs),
                         kdd=kd * w_rest, bdd=b * w_rest))
    chains = [(dd, j) for dd in dirs for j in range(2)]
    sels = [half == j for _, j in chains]
    bms = [jnp.where(sel, dd["bh"], 0.0) for (dd, _), sel in zip(chains, sels)]
    kms = [jnp.where(sel, dd["kh"], 0.0) for (dd, _), sel in zip(chains, sels)]
    l_bs = [jnp.where(dd["strict"], _dot1(dd["kkt"], bm, _NT), 0.0) for (dd, _), bm in zip(chains, bms)]
    tinvs = [eye - jnp.where(first_mask, l_b, 0.0) for l_b in l_bs]
    for mask in level_masks:
        xs = [_dot1(jnp.where(mask, l_b, 0.0), tinv) for l_b, tinv in zip(l_bs, tinvs)]
        tinvs = [tinv - _dot1(tinv, x) for tinv, x in zip(tinvs, xs)]
    l_ks = [jnp.where(dd["strict"], _dot1(dd["kkt"], km, _NT), 0.0) for (dd, _), km in zip(chains, kms)]
    a_rks = [jnp.where(dd["incl"], _dot1(dd["rt"], km, _NT), 0.0) for (dd, _), km in zip(chains, kms)]
    a_rbs = [jnp.where(dd["incl"], _dot1(dd["rt"], bm, _NT), 0.0) for (dd, _), bm in zip(chains, bms)]
    lvs = [_dot1(l_k, v) for l_k in l_ks]
    pus = [_dot1(tinv, jnp.concatenate([dd["kkt"], lv], axis=1)) for (dd, _), tinv, lv in zip(chains, tinvs, lvs)]
    cors = [_dot1(a_rb, pu) for a_rb, pu in zip(a_rbs, pus)]
    ps = [pu[:, :128] for pu in pus]
    u0s = [pu[:, 128:] for pu in pus]
    qs = [dd["rt"] - cor[:, :128] for (dd, _), cor in zip(chains, cors)]
    y0s = [_dot1(a_rk, v) - cor[:, 128:] for a_rk, cor in zip(a_rks, cors)]

    sel0 = half == 0
    for d, dd in enumerate(dirs):
        p, u0, q, y0 = (jnp.where(sel0, x[2 * d], x[2 * d + 1]) for x in (ps, u0s, qs, y0s))
        for c in range(CPT):
            rs = slice(c * CHUNK, (c + 1) * CHUNK)
            x1 = _dot1(dd["bdd"][rs], p[rs], _TN)
            x2 = _dot1(dd["kdd"][rs], v[rs], _TN) - _dot1(dd["bdd"][rs], u0[rs], _TN)
            m_pair = jnp.where(half_c == 0, x1[:HEAD_DIM], x1[HEAD_DIM:])
            n_pair = jnp.where(half_c == 0, x2[:HEAD_DIM], x2[HEAD_DIM:])
            wc = jnp.exp(dd["tot"][c * CHUNK:c * CHUNK + 1])
            dg = jnp.where((lanec % HEAD_DIM) == rowc, wc, 0.0)
            g_ref[c, d, 0:HEAD_DIM, :] = dg - m_pair
            g_ref[c, d, HEAD_DIM:, :] = q[rs]
            add_ref[c, d, 0:HEAD_DIM, :] = n_pair
            add_ref[c, d, HEAD_DIM:, :] = y0[rs]

    kd_sum = dirs[0]["kd"] + dirs[1]["kd"]
    bonus_ref[...] = 0.5 * _dot3(r * kd_sum * rk_ref[...], bs) * v


def _lora_ext(up, first_row):
    out = jnp.zeros((2, 128, BRANCH_W), F32)
    for d in range(2):
        out = out.at[d, first_row + 32 * d:first_row + 32 * (d + 1)].set(up[d])
    return out


def rwkv_prep(rw, w0, w_up, a0, a_up, k_k, k_a, r_k):
    r = rw.shape[0]
    nt = r // ROW_TILE
    nch = r // CHUNK
    t = ROW_TILE
    ii = np.arange(t)
    same = (ii[:, None] // CHUNK) == (ii[None, :] // CHUNK)
    tri = jnp.asarray(np.stack([same & (ii[None, :] <= ii[:, None]), same & (ii[None, :] >= ii[:, None])]), F32)
    lane = lambda blk: pl.BlockSpec((t, 128), lambda i, p, blk=blk: (i, blk + p))
    pvec = pl.BlockSpec((1, 128), lambda i, p: (0, p))
    dvec = pl.BlockSpec((2, 1, 128), lambda i, p: (0, 0, p))
    dmat = pl.BlockSpec((2, 128, 128), lambda i, p: (0, 0, p))
    gspec = pl.BlockSpec((CPT, 2, HEAD_DIM + CHUNK, 128), lambda i, p: (i, 0, 0, p))
    gshape = jax.ShapeDtypeStruct((nch, 2, HEAD_DIM + CHUNK, BRANCH_W), F32)
    return pl.pallas_call(
        _rwkv_prep_body,
        out_shape=[gshape, gshape, jax.ShapeDtypeStruct((r, BRANCH_W), F32)],
        grid=(nt, N_HEADS // 2),
        in_specs=[lane(0), lane(4), lane(8), pl.BlockSpec((t, 128), lambda i, p: (i, 12)),
                  dvec, dmat, dvec, dmat, pvec, pvec, pvec,
                  pl.BlockSpec((2, t, t), lambda i, p: (0, 0, 0)),
                  pl.BlockSpec((128, 128), lambda i, p: (0, 0))],
        out_specs=[gspec, gspec, pl.BlockSpec((t, 128), lambda i, p: (i, p))],
        compiler_params=_cparams(("parallel", "parallel")),
        name="rwkv_prep",
    )(rw, rw, rw, rw, w0.reshape(2, 1, BRANCH_W), _lora_ext(w_up, 0), a0.reshape(2, 1, BRANCH_W),
      _lora_ext(a_up, 64), k_k[None], k_a[None], r_k.reshape(1, BRANCH_W), tri, _block_sum_mat(128, 1.0))


def _rwkv_scan_body(gf_ref, af_ref, gb_ref, ab_ref, yf_ref, yb_ref, st_ref):
    @pl.when(pl.program_id(0) == 0)
    def _():
        st_ref[...] = jnp.zeros(st_ref.shape, F32)

    rowh = lax.broadcasted_iota(jnp.int32, (128, 128), 0) // HEAD_DIM
    diag = rowh == _lane_half((128, 128))
    for d, (g_ref, a_ref, y_ref) in enumerate(((gf_ref, af_ref, yf_ref), (gb_ref, ab_ref, yb_ref))):
        for p in range(N_HEADS // 2):
            ls = slice(p * 128, (p + 1) * 128)
            out = _dot3(g_ref[0, 0, :, ls], st_ref[d, p]) + a_ref[0, 0, :, ls]
            hn = out[:HEAD_DIM]
            st_ref[d, p] = jnp.where(diag, jnp.concatenate([hn, hn], axis=0), 0.0)
            y_ref[:, ls] = out[HEAD_DIM:]


def rwkv_scan(g, add):
    nch = g.shape[0]
    r = nch * CHUNK
    nctx = CTX_LEN // CHUNK
    rev = lambda c: jnp.where(c < nctx, nctx - 1 - c, nch + nctx - 1 - c)
    blk = (1, 1, HEAD_DIM + CHUNK, BRANCH_W)
    fwd = pl.BlockSpec(blk, lambda c: (c, 0, 0, 0))
    bwd = pl.BlockSpec(blk, lambda c: (rev(c), 1, 0, 0))
    yshape = jax.ShapeDtypeStruct((r, BRANCH_W), F32)
    return pl.pallas_call(
        _rwkv_scan_body,
        out_shape=[yshape, yshape],
        grid=(nch,),
        in_specs=[fwd, fwd, bwd, bwd],
        out_specs=[pl.BlockSpec((CHUNK, BRANCH_W), lambda c: (c, 0)),
                   pl.BlockSpec((CHUNK, BRANCH_W), lambda c: (rev(c), 0))],
        scratch_shapes=[pltpu.VMEM((2, N_HEADS // 2, 128, 128), F32)],
        compiler_params=_cparams(("arbitrary",)),
        name="rwkv_scan",
    )(g, add, g, add)


HY_WIDTH = 512
HY_ORDER = 2
HY_POS_BANDS = 16
HY_HIDDEN = 64
HY_TAPS_W = 2 * HY_ORDER * HY_WIDTH
FFT_N2 = ROW_TILE


def _dot3c(ah, al, b):
    bh, bl = _split(b)
    dg = functools.partial(jnp.dot, preferred_element_type=F32)
    return dg(ah, bh) + (dg(al, bh) + dg(ah, bl))


def _dotc(ah, b):
    return jnp.dot(ah, b.astype(BF16), preferred_element_type=F32)


def _split_const(m):
    m = np.asarray(m, np.float32)
    hi = m.astype(BF16)
    lo = (m - hi.astype(np.float32)).astype(BF16)
    return jnp.asarray(hi), jnp.asarray(lo)


TAPS_FLAT_COLS = 8


def _filter_taps(t_idx, length, c2pb_ref, w1t_ref, w1c_ref, w1s_ref, b1_ref, w2_ref, b2_ref, w3_ref, b3_ref, absd_ref):
    t = t_idx / float(max(length - 1, 1))
    ang = c2pb_ref[...] * t_idx / float(length)
    pre = t * w1t_ref[...] + _dot3(jnp.cos(ang), w1c_ref[...]) - _dot3(jnp.sin(ang), w1s_ref[...]) + b1_ref[...]
    hid = jnp.sin(pre)
    hid = jnp.sin(_dot3(hid, w2_ref[...]) + b2_ref[...])
    return (_dot3(hid, w3_ref[...]) + b3_ref[...]) * jnp.exp(-t * absd_ref[...])


def _hy_taps_body(*refs, length):
    taps_ref, ssq_ref = refs[-2:]
    i = pl.program_id(0)
    t_idx = (i * ROW_TILE + lax.broadcasted_iota(jnp.int32, (ROW_TILE, 1), 0)).astype(F32)
    taps = _filter_taps(t_idx, length, *refs[:-2])
    taps_ref[...] = taps

    @pl.when(i == 0)
    def _():
        ssq_ref[...] = jnp.zeros(ssq_ref.shape, F32)

    ssq_ref[...] += jnp.sum(taps * taps, axis=0, keepdims=True)


def _hy_taps_flat_body(*refs, length):
    taps_ref, ssq_ref = refs[-2:]
    j = pl.program_id(0)
    rows = length // FFT_N2
    base = lax.broadcasted_iota(jnp.int32, (rows, 1), 0) * FFT_N2 + j * TAPS_FLAT_COLS

    @pl.when(j == 0)
    def _():
        ssq_ref[...] = jnp.zeros(ssq_ref.shape, F32)

    for b in range(TAPS_FLAT_COLS):
        taps = _filter_taps((base + b).astype(F32), length, *refs[:-2])
        taps_ref[:, b * HY_TAPS_W:(b + 1) * HY_TAPS_W] = taps
        ssq_ref[...] += jnp.sum(taps * taps, axis=0, keepdims=True)


def hyena_taps(length, w1, b1, w2, b2, w3, b3, *, flat):
    bands = jnp.linspace(1e-4, HY_POS_BANDS - 1, HY_POS_BANDS, dtype=F32)
    c2pb = jnp.zeros((1, 128), F32).at[0, :HY_POS_BANDS].set(2.0 * math.pi * bands)
    pad = lambda m: jnp.zeros((128, HY_HIDDEN), F32).at[:HY_POS_BANDS].set(m)
    deltas = jnp.linspace(math.log(1e-2) / 0.3, math.log(1e-2) / 1.5, HY_WIDTH, dtype=F32)
    absd = jnp.tile(jnp.abs(deltas), 2 * HY_ORDER)[None]
    const = lambda shp: pl.BlockSpec(shp, lambda i: (0,) * len(shp))
    h = HY_HIDDEN
    if flat:
        rows = length // FFT_N2
        body, grid = _hy_taps_flat_body, (FFT_N2 // TAPS_FLAT_COLS,)
        taps_shape = (rows, FFT_N2 * HY_TAPS_W)
        taps_spec = pl.BlockSpec((rows, TAPS_FLAT_COLS * HY_TAPS_W), lambda i: (0, i))
    else:
        body, grid = _hy_taps_body, (length // ROW_TILE,)
        taps_shape = (length, HY_TAPS_W)
        taps_spec = pl.BlockSpec((ROW_TILE, HY_TAPS_W), lambda i: (i, 0))
    taps, ssq = pl.pallas_call(
        functools.partial(body, length=length),
        out_shape=[jax.ShapeDtypeStruct(taps_shape, F32), jax.ShapeDtypeStruct((1, HY_TAPS_W), F32)],
        grid=grid,
        in_specs=[const((1, 128)), const((1, h)), const((128, h)), const((128, h)), const((1, h)), const((h, h)),
                  const((1, h)), const((h, HY_TAPS_W)), const((1, HY_TAPS_W)), const((1, HY_TAPS_W))],
        out_specs=[taps_spec, const((1, HY_TAPS_W))],
        compiler_params=_cparams(("arbitrary",)),
        name="hyena_taps_flat" if flat else "hyena_taps",
    )(c2pb, w1[0:1], pad(w1[1:1 + HY_POS_BANDS]), pad(w1[1 + HY_POS_BANDS:]), b1[None], w2, b2[None], w3, b3[None], absd)
    hw = HY_TAPS_W // 2
    tap0 = taps[0:1, :HY_TAPS_W]
    norm2 = ssq[:, :hw] + ssq[:, hw:] + 2.0 * tap0[:, :hw] * tap0[:, hw:]
    return taps, lax.rsqrt(norm2)


class _FftPlan:
    def __init__(self, length):
        self.length = length
        self.n = 2 * length
        self.n2 = FFT_N2
        self.n1 = self.n // self.n2
        self.n1h = self.n1 // 2
        k1 = self.n1h + 1
        self.k1p = -(-k1 // 8) * 8
        kk = np.arange(self.k1p)[:, None].astype(np.float64)
        live = (kk < k1)
        nn = np.arange(self.n1h)[None, :].astype(np.float64)
        th = 2.0 * np.pi * kk * nn / self.n1
        self.f1 = _split_const(np.concatenate([np.cos(th) * live, -np.sin(th) * live], axis=0))
        ck = np.where((kk == 0) | (kk == self.n1h), 1.0, 2.0) * live / self.n
        self.g1 = _split_const(np.concatenate([np.cos(th) * ck, -np.sin(th) * ck], axis=0).T)
        m = np.arange(self.n2).astype(np.float64)
        ph = 2.0 * np.pi * np.outer(m, m) / self.n2
        c, s = np.cos(ph), np.sin(ph)
        self.fb = _split_const(np.block([[c, s], [-s, c]]))
        self.fbi = _split_const(np.block([[c, -s], [s, c]]))
        tw = 2.0 * np.pi * kk[:, :, None] * m[None, :, None] / self.n
        self.twc = jnp.asarray(np.cos(tw), F32)
        self.tws = jnp.asarray(np.sin(tw), F32)


FFT_TN = 4096


def _fft_a_body(fh_ref, x_ref, o_ref):
    o_ref[...] = _dotc(fh_ref[...], x_ref[...])


def fft_stage_a(plan, xf, lead):
    rows_in, m = xf.shape
    rows = 2 * plan.k1p
    tn = min(FFT_TN, m)
    fh = jnp.pad(plan.f1[0], ((0, 0), (lead, 0)))
    fspec = pl.BlockSpec((rows, rows_in), lambda j: (0, 0))
    return pl.pallas_call(
        _fft_a_body,
        out_shape=jax.ShapeDtypeStruct((rows, m), F32),
        grid=(m // tn,),
        in_specs=[fspec, pl.BlockSpec((rows_in, tn), lambda j: (0, j))],
        out_specs=pl.BlockSpec((rows, tn), lambda j: (0, j)),
        compiler_params=_cparams(("parallel",)),
        name="fft_stage_a",
    )(fh, xf)


def _twiddled(a_ref, twc_ref, tws_ref):
    are, aim = a_ref[0, 0], a_ref[1, 0]
    c, s = twc_ref[0], tws_ref[0]
    return jnp.concatenate([are * c + aim * s, aim * c - are * s], axis=0)


def _fft_filter_b_body(a_ref, twc_ref, tws_ref, fbh_ref, scale_ref, o_ref):
    n2 = FFT_N2
    x = _dotc(fbh_ref[...], _twiddled(a_ref, twc_ref, tws_ref))
    hw = HY_TAPS_W // 2
    xre, xim = x[:n2], x[n2:]
    o_ref[0, 0] = (xre[:, :hw] + xre[:, hw:]) * scale_ref[...]
    o_ref[1, 0] = (xim[:, :hw] - xim[:, hw:]) * scale_ref[...]


def fft_filter_stage_b(plan, a, scale):
    n2, k1p = plan.n2, plan.k1p
    hw = HY_TAPS_W // 2
    const = lambda shp: pl.BlockSpec(shp, lambda k: (0,) * len(shp))
    return pl.pallas_call(
        _fft_filter_b_body,
        out_shape=jax.ShapeDtypeStruct((2, k1p, n2, hw), F32),
        grid=(k1p,),
        in_specs=[pl.BlockSpec((2, 1, n2, HY_TAPS_W), lambda k: (0, k, 0, 0)),
                  pl.BlockSpec((1, n2, 1), lambda k: (k, 0, 0)), pl.BlockSpec((1, n2, 1), lambda k: (k, 0, 0)),
                  const((2 * n2, 2 * n2)), const((1, hw))],
        out_specs=pl.BlockSpec((2, 1, n2, hw), lambda k: (0, k, 0, 0)),
        compiler_params=_cparams(("parallel",)),
        name="fft_filter_stage_b",
    )(a, plan.twc, plan.tws, plan.fb[0], scale)


def _fft_conv_b_body(a_ref, kf_ref, twc_ref, tws_ref, fbh_ref, fih_ref, o_ref):
    n2 = FFT_N2
    z = _dotc(fbh_ref[...], _twiddled(a_ref, twc_ref, tws_ref))
    zre, zim = z[:n2], z[n2:]
    kre, kim = kf_ref[0, 0], kf_ref[1, 0]
    y = jnp.concatenate([zre * kre - zim * kim, zre * kim + zim * kre], axis=0)
    q = _dotc(fih_ref[...], y)
    qre, qim = q[:n2], q[n2:]
    c, s = twc_ref[0], tws_ref[0]
    o_ref[0, 0] = qre * c - qim * s
    o_ref[1, 0] = qim * c + qre * s


def fft_conv_stage_b(plan, a, kf, order):
    n2, k1p = plan.n2, plan.k1p
    const = lambda shp: pl.BlockSpec(shp, lambda k: (0,) * len(shp))
    return pl.pallas_call(
        _fft_conv_b_body,
        out_shape=jax.ShapeDtypeStruct((2, k1p, n2, HY_WIDTH), F32),
        grid=(k1p,),
        in_specs=[pl.BlockSpec((2, 1, n2, HY_WIDTH), lambda k: (0, k, 0, 0)),
                  pl.BlockSpec((2, 1, n2, HY_WIDTH), lambda k: (0, k, 0, order)),
                  pl.BlockSpec((1, n2, 1), lambda k: (k, 0, 0)), pl.BlockSpec((1, n2, 1), lambda k: (k, 0, 0)),
                  const((2 * n2, 2 * n2)), const((2 * n2, 2 * n2))],
        out_specs=pl.BlockSpec((2, 1, n2, HY_WIDTH), lambda k: (0, k, 0, 0)),
        compiler_params=_cparams(("parallel",)),
        name="fft_conv_stage_b",
    )(a, kf, plan.twc, plan.tws, plan.fb[0], plan.fbi[0])


def _fft_inv_a_body(gh_ref, q_ref, z_ref, gate_ref, skip_ref, o_ref):
    y = _dotc(gh_ref[...], q_ref[...])
    o_ref[...] = gate_ref[...] * (y + z_ref[...] * skip_ref[...])


def fft_inv_stage_a(plan, qf, zf, gatef, skip_t, lead):
    rows_out, m = zf.shape
    rows = 2 * plan.k1p
    tn = skip_t.shape[1]
    gh = jnp.pad(plan.g1[0], ((lead, 0), (0, 0)))
    col = lambda r_: pl.BlockSpec((r_, tn), lambda j: (0, j))
    gspec = pl.BlockSpec((rows_out, rows), lambda j: (0, 0))
    return pl.pallas_call(
        _fft_inv_a_body,
        out_shape=jax.ShapeDtypeStruct((rows_out, m), F32),
        grid=(m // tn,),
        in_specs=[gspec, col(rows), col(rows_out), col(rows_out), pl.BlockSpec((1, tn), lambda j: (0, 0))],
        out_specs=col(rows_out),
        compiler_params=_cparams(("parallel",)),
        name="fft_inv_stage_a",
    )(gh, qf, zf, gatef, skip_t)


def hyena_long(hv, hx1, hx2, taps, scale, skip, lead):
    length = taps.shape[0] * FFT_N2
    plan = _FftPlan(length)
    n2, k1p = plan.n2, plan.k1p
    m = n2 * HY_WIDTH
    tn = min(FFT_TN, m)
    flat = lambda a: a.reshape(lead + plan.n1h, m)
    ta = fft_stage_a(plan, taps, 0)
    kf = fft_filter_stage_b(plan, ta.reshape(2, k1p, n2, HY_TAPS_W), scale)
    z = flat(hv)
    for o, gate in enumerate((hx1, hx2)):
        a = fft_stage_a(plan, z, lead)
        q = fft_conv_stage_b(plan, a.reshape(2, k1p, n2, HY_WIDTH), kf, o)
        z = fft_inv_stage_a(plan, q.reshape(2 * k1p, m), z, flat(gate), jnp.tile(skip[o], tn // HY_WIDTH)[None], lead)
    return z.reshape(-1, HY_WIDTH)


def _rec_out_body(x_ref, yf_ref, yb_ref, bonus_ref, sgr_ref, zh_ref, zc_ref, sgh_ref, gnw_ref, gnb_ref, bm_ref, wa_ref, wb_ref,
                  gate_ref, fin_ref, o_ref, *, final):
    y = yf_ref[...] + yb_ref[...]
    bm = bm_ref[...]
    mean = jnp.dot(y, bm, precision=HIGHEST, preferred_element_type=F32)
    yc = y - mean
    var = jnp.dot(yc * yc, bm, precision=HIGHEST, preferred_element_type=F32)
    yn = yc * lax.rsqrt(var + RWKV_GN_EPS) * gnw_ref[...] + gnb_ref[...]
    ya = ((yn + bonus_ref[...]) * sgr_ref[...]).astype(BF16)
    zh = zh_ref[...]
    if not final:
        zh = jnp.where(pl.program_id(0) == 0, zc_ref[...], zh)
    yh = (zh * sgh_ref[...]).astype(BF16)
    out = jnp.dot(ya, wa_ref[...], preferred_element_type=F32) + jnp.dot(yh, wb_ref[...], preferred_element_type=F32)
    xn = x_ref[...] + gate_ref[0] * out
    if final:
        xn = xn * lax.rsqrt(jnp.mean(xn * xn, axis=-1, keepdims=True) + NORM_EPS) * fin_ref[...]
    o_ref[...] = xn


RWKV_GN_EPS = 64e-5


def rec_out_proj(xs, yf, yb, bonus, sgr, zh, zc, sgh, gn_w, gn_b, w_out, gate2, final_g, *, final):
    r = xs.shape[0]
    d = D_MODEL
    off = 1 if final else 0
    nt = r // ROW_TILE - off
    rows = lambda w: pl.BlockSpec((ROW_TILE, w), lambda i: (i + off, 0))
    const = lambda shp: pl.BlockSpec(shp, lambda i: (0,) * len(shp))
    bw = BRANCH_W
    return pl.pallas_call(
        functools.partial(_rec_out_body, final=final),
        out_shape=jax.ShapeDtypeStruct((nt * ROW_TILE, d), F32),
        grid=(nt,),
        in_specs=[rows(d), rows(bw), rows(bw), rows(bw), rows(bw), rows(bw), const((CTX_LEN, bw)), rows(bw),
                  const((1, bw)), const((1, bw)),
                  const((bw, bw)), const((bw, d)), const((bw, d)),
                  pl.BlockSpec((1, 1, d), lambda i: (jnp.minimum(i + off, 1), 0, 0)), const((1, d))],
        out_specs=pl.BlockSpec((ROW_TILE, d), lambda i: (i, 0)),
        compiler_params=_cparams(("parallel",)),
        name="rec_out_proj",
    )(xs, yf, yb, bonus, sgr, zh, zc, sgh, gn_w[None], gn_b[None], _block_sum_mat(bw, 1.0 / HEAD_DIM),
      w_out[:bw].astype(BF16), w_out[bw:].astype(BF16), gate2, final_g[None])


def rec_layer(xs, mods, norm_g, w_in, mu, w0, w_up, a0, a_up, k_k, k_a, r_k, gn_w, gn_b, hy_short, hy_w1, hy_b1, hy_w2,
              hy_b2, hy_w3, hy_b3, hy_skip, w_out, final_g, final):
    shift2, scale2, gate2 = mods
    rw, hv, hx1, hx2, sgr, sgh = rec_in_proj(xs, norm_g, scale2, shift2, w_in, mu, hy_short)
    g, add, bonus = rwkv_prep(rw, w0, w_up, a0, a_up, k_k, k_a, r_k)
    yf, yb = rwkv_scan(g, add)
    fargs = (hy_w1, hy_b1, hy_w2, hy_b2, hy_w3, hy_b3)
    n = xs.shape[0] - CTX_LEN
    taps, scale = hyena_taps(n, *fargs, flat=True)
    zh = hyena_long(hv, hx1, hx2, taps, scale, hy_skip, CTX_LEN // FFT_N2)
    if final:
        z_ctx = hv[:CTX_LEN]
    else:
        taps_c, scale_c = hyena_taps(CTX_LEN, *fargs, flat=False)
        z_ctx = hyena_short(hv[:CTX_LEN], hx1[:CTX_LEN], hx2[:CTX_LEN], taps_c, scale_c, hy_skip)
    return rec_out_proj(xs, yf, yb, bonus, sgr, zh, z_ctx, sgh, gn_w, gn_b, w_out, gate2, final_g, final=final)


def kernel(x, c, ctx, c_ctx, attn_norm, attn_ada_w, attn_ada_b, attn_w_in, na_rpb, gqa_q_gain, gqa_k_gain, attn_w_out,
           rec_norm, rec_ada_w, rec_ada_b, rec_w_in, rwkv_mu, rwkv_w0, rwkv_w_up, rwkv_a0, rwkv_a_up, rwkv_k_k, rwkv_k_a,
           rwkv_r_k, rwkv_gn_w, rwkv_gn_b, hy_short, hy_w1, hy_b1, hy_w2, hy_b2, hy_w3, hy_b3, hy_skip, rec_w_out,
           final_norm):
    assert x.shape[0] == 1 and ctx.shape[1] == CTX_LEN and x.shape[2] == D_MODEL
    n = x.shape[1]
    assert n % ROW_TILE == 0 and n // ROW_TILE >= 3
    assert attn_w_in.shape[0] == rec_w_in.shape[0]
    d = D_MODEL
    cond8 = jnp.zeros((8, d), F32).at[0].set(c_ctx).at[1].set(c[0])
    m_attn = adaln_all(cond8, attn_ada_w, attn_ada_b)
    m_rec = adaln_all(cond8, rec_ada_w, rec_ada_b)
    mods = lambda m, i: tuple(m[i, :2, j * d:(j + 1) * d].reshape(2, 1, d) for j in range(3))
    cos_t, sin_t = _rope_tables(n)
    xs = jnp.concatenate([ctx[0], x[0]], axis=0)
    depth = attn_w_in.shape[0] + rec_w_in.shape[0]
    for layer in range(depth):
        i = layer // 2
        final = layer == depth - 1
        if layer % 2 == 0:
            xs = attn_layer(xs, mods(m_attn, i), attn_norm[i], attn_w_in[i], na_rpb[i], gqa_q_gain[i], gqa_k_gain[i],
                            attn_w_out[i], cos_t, sin_t)
        else:
            xs = rec_layer(xs, mods(m_rec, i), rec_norm[i], rec_w_in[i], rwkv_mu[i], rwkv_w0[i], rwkv_w_up[i],
                           rwkv_a0[i], rwkv_a_up[i], rwkv_k_k[i], rwkv_k_a[i], rwkv_r_k[i], rwkv_gn_w[i], rwkv_gn_b[i],
                           hy_short[i], hy_w1[i], hy_b1[i], hy_w2[i], hy_b2[i], hy_w3[i], hy_b3[i], hy_skip[i],
                           rec_w_out[i], final_norm, final)
    return xs[None]


def _hy_short_body(fh_ref, fl_ref, gh_ref, gl_ref, v_ref, x1_ref, x2_ref, taps_ref, scale_ref, skip_ref, o_ref):
    fh, fl, gh, gl = fh_ref[...], fl_ref[...], gh_ref[...], gl_ref[...]
    kp = fh.shape[0] // 2
    hw = HY_TAPS_W // 2
    tf = _dot3c(fh, fl, taps_ref[...])
    kre = (tf[:kp, :hw] + tf[:kp, hw:]) * scale_ref[...]
    kim = (tf[kp:, :hw] - tf[kp:, hw:]) * scale_ref[...]
    z = v_ref[...]
    for o, gate_ref in enumerate((x1_ref, x2_ref)):
        ls = slice(o * HY_WIDTH, (o + 1) * HY_WIDTH)
        zf = _dot3c(fh, fl, z)
        zre, zim = zf[:kp], zf[kp:]
        y = jnp.concatenate([zre * kre[:, ls] - zim * kim[:, ls], zre * kim[:, ls] + zim * kre[:, ls]], axis=0)
        z = gate_ref[...] * (_dot3c(gh, gl, y) + z * skip_ref[o:o + 1])
    o_ref[...] = z


def hyena_short(hv, hx1, hx2, taps, scale, skip):
    length = hv.shape[0]
    n = 2 * length
    k1 = length + 1
    kp = -(-k1 // 8) * 8
    kk = np.arange(kp)[:, None].astype(np.float64)
    live = kk < k1
    th = 2.0 * np.pi * kk * np.arange(length)[None, :] / n
    f = _split_const(np.concatenate([np.cos(th) * live, -np.sin(th) * live], axis=0))
    ck = np.where((kk == 0) | (kk == length), 1.0, 2.0) * live / n
    g = _split_const(np.concatenate([np.cos(th) * ck, -np.sin(th) * ck], axis=0).T)
    return pl.pallas_call(
        _hy_short_body,
        out_shape=jax.ShapeDtypeStruct((length, HY_WIDTH), F32),
        compiler_params=pltpu.CompilerParams(vmem_limit_bytes=VMEM_LIMIT),
        name="hyena_short",
    )(f[0], f[1], g[0], g[1], hv, hx1, hx2, taps, scale, skip)
```

```python
import functools
import math

import jax
import jax.numpy as jnp
import numpy as np
from jax import lax
from jax.experimental import pallas as pl
from jax.experimental.pallas import tpu as pltpu

F32 = jnp.float32
BF16 = jnp.bfloat16
HIGHEST = lax.Precision.HIGHEST

D_MODEL = 1024
GRID_W = 64
CTX_LEN = 256
HEAD_DIM = 64
BRANCH_W = 512
N_HEADS = 8
GQA_KV_W = 128
NA_WIN_ROWS = 8
NA_WIN_COLS = 16
ROPE_THETA = 10000.0
ROPE_FREQS = 16
NORM_EPS = 1e-6
ROW_TILE = 256
NA_GROUP_ROWS = 4
NEG_BIG = -1e30
LOG2E = math.log2(math.e)
QK_SCALE = HEAD_DIM ** -0.5 * LOG2E
VMEM_LIMIT = 56 * 1024 * 1024

ATTN_SPLITS = (512, 512, 512, 512, 512, 128, 128, 512)
GQA_HEAD_ORDER = (0, 4, 1, 5, 2, 6, 3, 7)


def _cparams(sem):
    return pltpu.CompilerParams(dimension_semantics=sem, vmem_limit_bytes=VMEM_LIMIT)


def _silu(v):
    return v * (1.0 / (1.0 + jnp.exp(-v)))


def _lane_half(shape):
    return (lax.broadcasted_iota(jnp.int32, shape, len(shape) - 1) // HEAD_DIM) % 2


def _dot_stat(a, block_mat):
    hi = a.astype(BF16)
    lo = (a - hi.astype(F32)).astype(BF16)
    return jnp.dot(hi, block_mat, preferred_element_type=F32) + jnp.dot(lo, block_mat, preferred_element_type=F32)


def _dot_nt(a, b):
    return lax.dot_general(a, b, (((1,), (1,)), ((), ())), preferred_element_type=F32)


def _adaln_body(cond_ref, w_ref, b_ref, o_ref):
    s = _silu(cond_ref[...])
    o_ref[0] = jnp.dot(s, w_ref[0], precision=HIGHEST, preferred_element_type=F32) + b_ref[0]


def adaln_all(cond8, ada_w, ada_b):
    nl = ada_w.shape[0]
    d = D_MODEL
    return pl.pallas_call(
        _adaln_body,
        out_shape=jax.ShapeDtypeStruct((nl, 8, 3 * d), F32),
        grid=(nl, 3),
        in_specs=[
            pl.BlockSpec((8, d), lambda l, j: (0, 0)),
            pl.BlockSpec((1, d, d), lambda l, j: (l, 0, j)),
            pl.BlockSpec((1, 1, d), lambda l, j: (l, 0, j)),
        ],
        out_specs=pl.BlockSpec((1, 8, d), lambda l, j: (l, 0, j)),
        compiler_params=_cparams(("parallel", "parallel")),
        name="adaln",
    )(cond8, ada_w, ada_b.reshape(nl, 1, 3 * d))


def _modulated(x_ref, g_ref, scale_ref, shift_ref):
    xf = x_ref[...]
    y = xf * lax.rsqrt(jnp.mean(xf * xf, axis=-1, keepdims=True) + NORM_EPS)
    return (y * g_ref[...]) * (1.0 + scale_ref[0]) + shift_ref[0]


def _mod_specs():
    d = D_MODEL
    return [
        pl.BlockSpec((ROW_TILE, d), lambda i: (i, 0)),
        pl.BlockSpec((1, d), lambda i: (0, 0)),
        pl.BlockSpec((1, 1, d), lambda i: (jnp.minimum(i, 1), 0, 0)),
        pl.BlockSpec((1, 1, d), lambda i: (jnp.minimum(i, 1), 0, 0)),
    ]


def _attn_in_body(x_ref, g_ref, scale_ref, shift_ref, w_ref, cos_ref, sin_ref, gq_ref, gqs_ref, gk_ref, gks_ref,
                  bdq_ref, bdk_ref,
                  qa_ref, ka_ref, va_ref, sga_ref, qb_ref, kb_ref, vb_ref, sgb_ref):
    xm = _modulated(x_ref, g_ref, scale_ref, shift_ref).astype(BF16)
    u = jnp.dot(xm, w_ref[...], preferred_element_type=F32)
    qa, ka, va, ga = u[:, 0:512], u[:, 512:1024], u[:, 1024:1536], u[:, 1536:2048]
    qb, kb, vb, gb = u[:, 2048:2560], u[:, 2560:2688], u[:, 2688:2816], u[:, 2816:3328]
    qbs, kbs = u[:, 3328:3840], u[:, 3840:3968]
    scale = QK_SCALE
    qa_ref[...] = (qa * scale).astype(BF16)
    ka_ref[...] = ka.astype(BF16)
    va_ref[...] = va.astype(BF16)
    sga_ref[...] = _silu(ga)
    sgb_ref[...] = _silu(gb)
    vb_ref[...] = jnp.transpose(vb).astype(BF16)
    cos_k, sin_k = cos_ref[...], sin_ref[...]
    cos_q = jnp.concatenate([cos_k] * 4, axis=1)
    sin_q = jnp.concatenate([sin_k] * 4, axis=1)
    rs_q = lax.rsqrt(_dot_stat(qb * qb, bdq_ref[...]) + NORM_EPS)
    rs_k = lax.rsqrt(_dot_stat(kb * kb, bdk_ref[...]) + NORM_EPS)
    qr = rs_q * (qb * gq_ref[...] * cos_q + qbs * gqs_ref[...] * sin_q)
    kr = rs_k * (kb * gk_ref[...] * cos_k + kbs * gks_ref[...] * sin_k)
    qb_ref[...] = jnp.transpose(qr * scale).astype(BF16)
    kb_ref[...] = kr.astype(BF16)


def _rope_tables(n):
    t = jnp.arange(n, dtype=jnp.int32)
    pos = jnp.stack([t // GRID_W, t % GRID_W], axis=-1).astype(F32)
    inv_freq = ROPE_THETA ** (-jnp.arange(ROPE_FREQS, dtype=F32) / ROPE_FREQS)
    ang = pos[:, :, None] * inv_freq
    c, s = jnp.cos(ang), jnp.sin(ang)
    cos64 = jnp.concatenate([c[:, 0], c[:, 0], c[:, 1], c[:, 1]], axis=-1)
    sin64 = jnp.concatenate([-s[:, 0], s[:, 0], -s[:, 1], s[:, 1]], axis=-1)
    cos64 = jnp.concatenate([jnp.ones((CTX_LEN, HEAD_DIM), F32), cos64], axis=0)
    sin64 = jnp.concatenate([jnp.zeros((CTX_LEN, HEAD_DIM), F32), sin64], axis=0)
    return jnp.tile(cos64, (1, 2)), jnp.tile(sin64, (1, 2))


def _reorder_heads(w, order, axis):
    take = lambda h: lax.slice_in_dim(w, h * HEAD_DIM, (h + 1) * HEAD_DIM, axis=axis)
    return jnp.concatenate([take(h) for h in order], axis=axis)


def _swap_rope_halves(w):
    shp = w.shape
    return jnp.flip(w.reshape(shp[:-1] + (shp[-1] // (2 * ROPE_FREQS), 2, ROPE_FREQS)), axis=-2).reshape(shp)


def attn_in_proj(xs, norm_g, scale2, shift2, w_in, q_gain, k_gain, cos_t, sin_t):
    r = xs.shape[0]
    d = D_MODEL
    parts, start = [], 0
    for s in ATTN_SPLITS:
        parts.append(w_in[:, start:start + s])
        start += s
    wqa, wka, wva, wga, wqb, wkb, wvb, wgb = parts
    wqb_p = _reorder_heads(wqb, GQA_HEAD_ORDER, 1)
    wgb_p = _reorder_heads(wgb, GQA_HEAD_ORDER, 1)
    wqb_sw = _swap_rope_halves(wqb_p)
    wkb_sw = _swap_rope_halves(wkb)
    w_ext = jnp.concatenate([wqa, wka, wva, wga, wqb_p, wkb, wvb, wgb_p, wqb_sw, wkb_sw], axis=1).astype(BF16)
    gq = jnp.tile(q_gain, N_HEADS)[None]
    gqs = jnp.tile(_swap_rope_halves(q_gain), N_HEADS)[None]
    gk = jnp.tile(k_gain, 2)[None]
    gks = jnp.tile(_swap_rope_halves(k_gain), 2)[None]
    bdq = jnp.asarray(np.kron(np.eye(N_HEADS), np.full((HEAD_DIM, HEAD_DIM), 1.0 / HEAD_DIM)), BF16)
    bdk = jnp.asarray(np.kron(np.eye(2), np.full((HEAD_DIM, HEAD_DIM), 1.0 / HEAD_DIM)), BF16)
    wcols = w_ext.shape[1]
    const = lambda shp: pl.BlockSpec(shp, lambda i: (0,) * len(shp))
    rows = lambda w: pl.BlockSpec((ROW_TILE, w), lambda i: (i, 0))
    out_shapes = [
        jax.ShapeDtypeStruct((r, 512), BF16), jax.ShapeDtypeStruct((r, 512), BF16), jax.ShapeDtypeStruct((r, 512), BF16),
        jax.ShapeDtypeStruct((r, 512), F32),
        jax.ShapeDtypeStruct((512, r), BF16), jax.ShapeDtypeStruct((r, 128), BF16), jax.ShapeDtypeStruct((128, r), BF16),
        jax.ShapeDtypeStruct((r, 512), F32),
    ]
    cols = lambda w: pl.BlockSpec((w, ROW_TILE), lambda i: (0, i))
    return pl.pallas_call(
        _attn_in_body,
        out_shape=out_shapes,
        grid=(r // ROW_TILE,),
        in_specs=_mod_specs() + [const((d, wcols)), rows(128), rows(128), const((1, 512)), const((1, 512)),
                                 const((1, 128)), const((1, 128)), const((512, 512)), const((128, 128))],
        out_specs=[rows(512), rows(512), rows(512), rows(512), cols(512), rows(128), cols(128), rows(512)],
        compiler_params=_cparams(("parallel",)),
        name="attn_in_proj",
    )(xs, norm_g[None], scale2, shift2, w_ext, cos_t, sin_t, gq, gqs, gk, gks, bdq, bdk)


def _na_cols_body(rpb_ref, sel_ref, neg_ref, o_ref):
    o_ref[...] = jnp.dot(rpb_ref[...], sel_ref[...], precision=HIGHEST, preferred_element_type=F32) + neg_ref[...]


def _na_bias_tables(rpb, rows):
    nrel_r, nrel_c = 2 * NA_WIN_ROWS - 1, 2 * NA_WIN_COLS - 1
    qc = np.arange(GRID_W)[:, None]
    kc = np.arange(GRID_W)[None, :]
    col0 = np.clip(qc - NA_WIN_COLS // 2, 0, GRID_W - NA_WIN_COLS)
    col_ok = (kc >= col0) & (kc < col0 + NA_WIN_COLS)
    rc = kc - qc + NA_WIN_COLS - 1
    sel = np.zeros((128, GRID_W * GRID_W), np.float32)
    sel[np.where(col_ok, rc, 127).reshape(-1), np.arange(GRID_W * GRID_W)] = col_ok.reshape(-1)
    neg = np.where(col_ok, 0.0, NEG_BIG).astype(np.float32).reshape(1, -1)
    rpb2 = jnp.zeros((128, 128), F32).at[:N_HEADS * nrel_r, :nrel_c].set(rpb.reshape(N_HEADS * nrel_r, nrel_c))
    cols = pl.pallas_call(
        _na_cols_body,
        out_shape=jax.ShapeDtypeStruct((128, GRID_W * GRID_W), F32),
        name="na_bias_cols",
    )(rpb2, jnp.asarray(sel), jnp.asarray(neg))
    cols = cols[:N_HEADS * nrel_r].reshape(N_HEADS, nrel_r, GRID_W, GRID_W)
    kh = min(NA_WIN_ROWS, rows)
    g = rows // NA_GROUP_ROWS
    cases = [(0, 0), (NA_GROUP_ROWS, 0), (rows - NA_GROUP_ROWS, NA_GROUP_ROWS * (g - 3))]
    masked = jnp.full((N_HEADS, GRID_W, GRID_W), NEG_BIG, F32)
    tabs = []
    for qr_first, start in cases:
        blocks = []
        for j in range(NA_GROUP_ROWS):
            qr = qr_first + j
            row0 = min(max(qr - kh // 2, 0), rows - kh)
            for i in range(3 * NA_GROUP_ROWS):
                kr = start + i
                blocks.append(cols[:, kr - qr + NA_WIN_ROWS - 1] if row0 <= kr < row0 + kh else masked)
        tab = jnp.stack(blocks, axis=1).reshape(N_HEADS, NA_GROUP_ROWS, 3 * NA_GROUP_ROWS, GRID_W, GRID_W)
        tabs.append(tab.transpose(0, 1, 3, 2, 4).reshape(N_HEADS, ROW_TILE, 3 * ROW_TILE))
    return jnp.stack(tabs) * LOG2E


def _na_body(q_ref, kc_ref, k0_ref, k1_ref, k2_ref, vc_ref, v0_ref, v1_ref, v2_ref, bias_ref, sg_ref, o_ref):
    half = _lane_half((ROW_TILE, 128))
    for hp in range(N_HEADS // 2):
        ls = slice(hp * 128, (hp + 1) * 128)
        qp = q_ref[:, ls]
        ks = [r[:, ls] for r in (k0_ref, k1_ref, k2_ref, kc_ref)]
        vs = [r[:, ls] for r in (v0_ref, v1_ref, v2_ref, vc_ref)]
        outs = []
        for j in range(2):
            qm = jnp.where(half == j, qp, jnp.zeros_like(qp))
            s = [_dot_nt(qm, k) for k in ks]
            s_win = jnp.concatenate(s[:3], axis=1) + bias_ref[0, 2 * hp + j]
            s_ctx = s[3]
            m = jnp.maximum(jnp.max(s_win, axis=1, keepdims=True), jnp.max(s_ctx, axis=1, keepdims=True))
            p_win = jnp.exp2(s_win - m)
            p_ctx = jnp.exp2(s_ctx - m)
            l = jnp.sum(p_win, axis=1, keepdims=True) + jnp.sum(p_ctx, axis=1, keepdims=True)
            o = jnp.dot(p_ctx.astype(BF16), vs[3], preferred_element_type=F32)
            for b in range(3):
                o += jnp.dot(p_win[:, b * ROW_TILE:(b + 1) * ROW_TILE].astype(BF16), vs[b], preferred_element_type=F32)
            outs.append(o / l)
        o_pair = jnp.where(half == 0, outs[0], outs[1])
        o_ref[:, ls] = (o_pair * sg_ref[:, ls]).astype(BF16)


def na_attention(qa, ka, va, sga, bias_tabs, n):
    g = n // ROW_TILE
    w = BRANCH_W

    def kv_spec(off):
        return pl.BlockSpec((ROW_TILE, w), lambda i: (jnp.clip(i - 1, 0, g - 3) + off + 1, 0))

    ctx_spec = pl.BlockSpec((ROW_TILE, w), lambda i: (0, 0))
    q_spec = pl.BlockSpec((ROW_TILE, w), lambda i: (i + 1, 0))
    case = lambda i: jnp.where(i == 0, 0, jnp.where(i == g - 1, 2, 1))
    bias_spec = pl.BlockSpec((1, N_HEADS, ROW_TILE, 3 * ROW_TILE), lambda i: (case(i), 0, 0, 0))
    return pl.pallas_call(
        _na_body,
        out_shape=jax.ShapeDtypeStruct((n, w), BF16),
        grid=(g,),
        in_specs=[q_spec, ctx_spec, kv_spec(0), kv_spec(1), kv_spec(2), ctx_spec, kv_spec(0), kv_spec(1), kv_spec(2),
                  bias_spec, q_spec],
        out_specs=pl.BlockSpec((ROW_TILE, w), lambda i: (i, 0)),
        compiler_params=_cparams(("parallel",)),
        name="na_attention",
    )(qa, ka, ka, ka, ka, va, va, va, va, bias_tabs, sga)


def _flash_mha_body(q_ref, k_ref, v_ref, sg_ref, o_ref, m_ref, acc_ref):
    kv = pl.program_id(1)
    tq = q_ref.shape[0]

    @pl.when(kv == 0)
    def _():
        m_ref[...] = jnp.full(m_ref.shape, NEG_BIG, F32)
        acc_ref[...] = jnp.zeros(acc_ref.shape, F32)

    khalf = _lane_half((k_ref.shape[0], 128))
    for p in range(N_HEADS // 2):
        ls = slice(p * 128, (p + 1) * 128)
        qp = q_ref[:, ls]
        kp = k_ref[:, ls]
        vp = v_ref[:, ls]
        for j in range(2):
            hh = 2 * p + j
            km = jnp.where(khalf == j, kp, jnp.zeros_like(kp))
            vm = jnp.where(khalf == j, vp, jnp.ones_like(vp))
            s = _dot_nt(qp, km)
            m_prev = m_ref[hh]
            m_new = jnp.maximum(m_prev, jnp.max(s, axis=1, keepdims=True))
            alpha = jnp.exp2(m_prev - m_new)
            pr = jnp.exp2(s - m_new[:, :1]).astype(BF16)
            acc_ref[hh] = alpha * acc_ref[hh] + jnp.dot(pr, vm, preferred_element_type=F32)
            m_ref[hh] = m_new

    @pl.when(kv == pl.num_programs(1) - 1)
    def _():
        half = _lane_half((tq, 128))
        for p in range(N_HEADS // 2):
            ls = slice(p * 128, (p + 1) * 128)
            a0, a1 = acc_ref[2 * p], acc_ref[2 * p + 1]
            o0 = a0 / pltpu.roll(a0, HEAD_DIM, 1)
            o1 = a1 / pltpu.roll(a1, HEAD_DIM, 1)
            o_ref[:, ls] = (jnp.where(half == 0, o0, o1) * sg_ref[:, ls]).astype(BF16)


def flash_mha(q, k, v, sg, *, q_block0, nq, tk, nk):
    tq = ROW_TILE
    return pl.pallas_call(
        _flash_mha_body,
        out_shape=jax.ShapeDtypeStruct((nq * tq, BRANCH_W), BF16),
        grid=(nq, nk),
        in_specs=[
            pl.BlockSpec((tq, BRANCH_W), lambda i, j: (i + q_block0, 0)),
            pl.BlockSpec((tk, BRANCH_W), lambda i, j: (j, 0)),
            pl.BlockSpec((tk, BRANCH_W), lambda i, j: (j, 0)),
            pl.BlockSpec((tq, BRANCH_W), lambda i, j: (i + q_block0, 0)),
        ],
        out_specs=pl.BlockSpec((tq, BRANCH_W), lambda i, j: (i, 0)),
        scratch_shapes=[pltpu.VMEM((N_HEADS, tq, 128), F32)] * 2,
        compiler_params=_cparams(("parallel", "arbitrary")),
        name="flash_mha",
    )(q, k, v, sg)


def _flash_gqa_body(qt_ref, k_ref, vt_ref, sg_ref, o_ref, m_ref, acc_ref):
    kv = pl.program_id(1)
    tq = qt_ref.shape[1]
    tk = k_ref.shape[0]

    @pl.when(kv == 0)
    def _():
        m_ref[...] = jnp.full(m_ref.shape, NEG_BIG, F32)
        acc_ref[...] = jnp.zeros(acc_ref.shape, F32)

    khalf = _lane_half((tk, 128))
    vhalf = lax.broadcasted_iota(jnp.int32, (128, tk), 0) // HEAD_DIM
    kb = k_ref[...]
    vt = vt_ref[...]
    kms = [jnp.where(khalf == j, kb, jnp.zeros_like(kb)) for j in range(2)]
    vms = [jnp.where(vhalf == j, vt, jnp.ones_like(vt)) for j in range(2)]

    def scores(hh):
        p, j = divmod(hh, 2)
        return jnp.dot(kms[j], qt_ref[p * 128:(p + 1) * 128, :], preferred_element_type=F32).astype(BF16)

    st_next = scores(0)
    for hh in range(N_HEADS):
        st = st_next
        if hh + 1 < N_HEADS:
            st_next = scores(hh + 1)
        m_prev = m_ref[hh]
        m_new = jnp.maximum(m_prev, jnp.max(st, axis=0, keepdims=True).astype(F32))
        alpha = jnp.exp2(m_prev - m_new)
        pt = jnp.exp2(st - m_new[0:1].astype(BF16))
        acc_ref[hh] = alpha[0:1] * acc_ref[hh] + jnp.dot(vms[hh % 2], pt, preferred_element_type=F32)
        m_ref[hh] = m_new

    @pl.when(kv == pl.num_programs(1) - 1)
    def _():
        for p in range(N_HEADS // 2):
            ls = slice(p * 128, (p + 1) * 128)
            a0, a1 = acc_ref[2 * p], acc_ref[2 * p + 1]
            ot = jnp.concatenate([a0[:HEAD_DIM] / a0[HEAD_DIM:], a1[HEAD_DIM:] / a1[:HEAD_DIM]], axis=0)
            o_ref[:, ls] = (jnp.transpose(ot) * sg_ref[:, ls]).astype(BF16)


def flash_gqa(qt, k, vt, sg, *, q_block0, nq, tk, nk):
    tq = ROW_TILE
    return pl.pallas_call(
        _flash_gqa_body,
        out_shape=jax.ShapeDtypeStruct((nq * tq, BRANCH_W), BF16),
        grid=(nq, nk),
        in_specs=[
            pl.BlockSpec((BRANCH_W, tq), lambda i, j: (0, i + q_block0)),
            pl.BlockSpec((tk, GQA_KV_W), lambda i, j: (j, 0)),
            pl.BlockSpec((GQA_KV_W, tk), lambda i, j: (0, j)),
            pl.BlockSpec((tq, BRANCH_W), lambda i, j: (i + q_block0, 0)),
        ],
        out_specs=pl.BlockSpec((tq, BRANCH_W), lambda i, j: (i, 0)),
        scratch_shapes=[pltpu.VMEM((N_HEADS, 8, tq), F32), pltpu.VMEM((N_HEADS, 128, tq), F32)],
        compiler_params=_cparams(("parallel", "arbitrary")),
        name="flash_gqa",
    )(qt, k, vt, sg)


def _out_body(x_ref, ya_ref, yac_ref, yb_ref, ybc_ref, wa_ref, wb_ref, gate_ref, o_ref):
    is_ctx = pl.program_id(0) == 0
    ya = jnp.where(is_ctx, yac_ref[...], ya_ref[...])
    yb = jnp.where(is_ctx, ybc_ref[...], yb_ref[...])
    y = jnp.dot(ya, wa_ref[...], preferred_element_type=F32)
    y += jnp.dot(yb, wb_ref[...], preferred_element_type=F32)
    o_ref[...] = x_ref[...] + gate_ref[0] * y


def out_proj(xs, ya_lat, ya_ctx, yb_lat, yb_ctx, wa, wb, gate2):
    r = xs.shape[0]
    d = D_MODEL
    rows = lambda w: pl.BlockSpec((ROW_TILE, w), lambda i: (i, 0))
    lat = pl.BlockSpec((ROW_TILE, BRANCH_W), lambda i: (jnp.maximum(i - 1, 0), 0))
    const = lambda shp: pl.BlockSpec(shp, lambda i: (0,) * len(shp))
    ctx = const((CTX_LEN, BRANCH_W))
    return pl.pallas_call(
        _out_body,
        out_shape=jax.ShapeDtypeStruct((r, d), F32),
        grid=(r // ROW_TILE,),
        in_specs=[rows(d), lat, ctx, lat, ctx, const((BRANCH_W, d)), const((BRANCH_W, d)),
                  pl.BlockSpec((1, 1, d), lambda i: (jnp.minimum(i, 1), 0, 0))],
        out_specs=rows(d),
        compiler_params=_cparams(("parallel",)),
        name="out_proj",
    )(xs, ya_lat, ya_ctx, yb_lat, yb_ctx, wa.astype(BF16), wb.astype(BF16), gate2)


KV_TILE_MAX_BLOCKS = 13


def _kv_tile(r):
    nb = r // ROW_TILE
    best = max(k for k in range(1, KV_TILE_MAX_BLOCKS + 1) if nb % k == 0)
    return best * ROW_TILE, nb // best


def attn_layer(xs, mods, norm_g, w_in, rpb, q_gain, k_gain, w_out, cos_t, sin_t):
    r = xs.shape[0]
    n = r - CTX_LEN
    shift2, scale2, gate2 = mods
    qa, ka, va, sga, qbt, kb, vbt, sgb = attn_in_proj(xs, norm_g, scale2, shift2, w_in, q_gain, k_gain, cos_t, sin_t)
    bias_tabs = _na_bias_tables(rpb, n // GRID_W)
    ya_lat = na_attention(qa, ka, va, sga, bias_tabs, n)
    ya_ctx = flash_mha(qa, ka, va, sga, q_block0=0, nq=1, tk=CTX_LEN, nk=1)
    tk, nk = _kv_tile(r)
    yb_lat = flash_gqa(qbt, kb, vbt, sgb, q_block0=1, nq=n // ROW_TILE, tk=tk, nk=nk)
    yb_ctx = flash_gqa(qbt, kb, vbt, sgb, q_block0=0, nq=1, tk=CTX_LEN, nk=1)
    wb = _reorder_heads(w_out[BRANCH_W:], GQA_HEAD_ORDER, 0)
    return out_proj(xs, ya_lat, ya_ctx, yb_lat, yb_ctx, w_out[:BRANCH_W], wb, gate2)


def _split(a):
    hi = a.astype(BF16)
    return hi, (a - hi.astype(F32)).astype(BF16)


def _dot3(a, b, dims=(((1,), (0,)), ((), ()))):
    ah, al = _split(a)
    bh, bl = _split(b)
    dg = functools.partial(lax.dot_general, dimension_numbers=dims, preferred_element_type=F32)
    return dg(ah, bh) + (dg(al, bh) + dg(ah, bl))


def _dot1(a, b, dims=(((1,), (0,)), ((), ()))):
    return lax.dot_general(a.astype(BF16), b.astype(BF16), dims, preferred_element_type=F32)


_NT = (((1,), (1,)), ((), ()))
_TN = (((0,), (0,)), ((), ()))


RWKV_SHIFT_W = 1664
HY_IN_W = 1536
HALO = 8
REC_HALO_W = RWKV_SHIFT_W + HY_IN_W


def _rec_in_body(x_ref, xp_ref, xn_ref, g_ref, scale_ref, shift_ref, w_ref, mu_ref, taps_ref,
                 rw_ref, hv_ref, hx1_ref, hx2_ref, sgr_ref, sgh_ref, u_scr):
    i = pl.program_id(0)
    nt = pl.num_programs(0)
    xe = jnp.concatenate([xp_ref[...], x_ref[...], xn_ref[...]], axis=0)
    y = xe * lax.rsqrt(jnp.mean(xe * xe, axis=-1, keepdims=True) + NORM_EPS)
    xm = ((y * g_ref[...]) * (1.0 + scale_ref[0]) + shift_ref[0]).astype(BF16)
    u = jnp.dot(xm, w_ref[...], preferred_element_type=F32)
    row = lax.broadcasted_iota(jnp.int32, (ROW_TILE + 2 * HALO, 1), 0)
    keep = jnp.logical_and(jnp.logical_or(row >= HALO, i >= 2),
                           jnp.logical_or(row < ROW_TILE + HALO, jnp.logical_and(i >= 1, i < nt - 1)))
    u_scr[...] = jnp.where(keep, u[:, :REC_HALO_W], 0.0)
    up = u_scr[pl.ds(HALO - 1, ROW_TILE), :]
    uc = u_scr[pl.ds(HALO, ROW_TILE), :]
    un = u_scr[pl.ds(HALO + 1, ROW_TILE), :]
    w = RWKV_SHIFT_W
    rw_c = uc[:, :w]
    rw_ref[...] = rw_c + (0.5 * (up[:, :w] + un[:, :w]) - rw_c) * mu_ref[...]
    hy = up[:, w:] * taps_ref[0:1] + uc[:, w:] * taps_ref[1:2] + un[:, w:] * taps_ref[2:3]
    hv_ref[...] = hy[:, 0:512]
    hx1_ref[...] = hy[:, 512:1024]
    hx2_ref[...] = hy[:, 1024:1536]
    uc_all = u[HALO:HALO + ROW_TILE]
    sgr_ref[...] = _silu(uc_all[:, REC_HALO_W:REC_HALO_W + 512])
    sgh_ref[...] = _silu(uc_all[:, REC_HALO_W + 512:REC_HALO_W + 1024])


def rec_in_proj(xs, norm_g, scale2, shift2, w_in, mu, hy_short):
    r = xs.shape[0]
    d = D_MODEL
    w = RWKV_SHIFT_W
    w_ext = jnp.concatenate([w_in[:, :w], w_in[:, w + 512:w + 512 + HY_IN_W], w_in[:, w:w + 512],
                             w_in[:, w + 512 + HY_IN_W:]], axis=1).astype(BF16)
    nh = r // HALO
    per = ROW_TILE // HALO
    const = lambda shp: pl.BlockSpec(shp, lambda i: (0,) * len(shp))
    rows = lambda wd: pl.BlockSpec((ROW_TILE, wd), lambda i: (i, 0))
    f = lambda wd: jax.ShapeDtypeStruct((r, wd), F32)
    mod = _mod_specs()
    return pl.pallas_call(
        _rec_in_body,
        out_shape=[f(w), f(512), f(512), f(512), f(512), f(512)],
        grid=(r // ROW_TILE,),
        in_specs=[mod[0],
                  pl.BlockSpec((HALO, d), lambda i: (jnp.maximum(i * per - 1, 0), 0)),
                  pl.BlockSpec((HALO, d), lambda i: (jnp.minimum((i + 1) * per, nh - 1), 0)),
                  mod[1], mod[2], mod[3], const((d, w_ext.shape[1])), const((1, w)), const((3, HY_IN_W))],
        out_specs=[rows(w), rows(512), rows(512), rows(512), rows(512), rows(512)],
        scratch_shapes=[pltpu.VMEM((ROW_TILE + 2 * HALO, REC_HALO_W), F32)],
        compiler_params=_cparams(("parallel",)),
        name="rec_in_proj",
    )(xs, xs, xs, norm_g[None], scale2, shift2, w_ext, mu[None], hy_short)


CHUNK = 64
CPT = ROW_TILE // CHUNK


def _block_sum_mat(width, value):
    return jnp.asarray(np.kron(np.eye(width // HEAD_DIM), np.full((HEAD_DIM, HEAD_DIM), value)), F32)


def _rwkv_prep_body(r_ref, k_ref, v_ref, lora_ref, w0_ref, wup_ref, a0_ref, aup_ref, kk_ref, ka_ref, rk_ref,
                    tri_ref, bs_ref, g_ref, add_ref, bonus_ref):
    t = ROW_TILE
    r, k, v, lora = r_ref[...], k_ref[...], v_ref[...], lora_ref[...]
    bs = bs_ref[...]
    kk = k * kk_ref[...]
    kk = kk * lax.rsqrt(_dot3(kk * kk, bs) + 1e-12)
    tanh_lora = jnp.tanh(lora)
    row = lax.broadcasted_iota(jnp.int32, (t, t), 0)
    col = lax.broadcasted_iota(jnp.int32, (t, t), 1)
    same = ((row // CHUNK) == (col // CHUNK)).astype(F32)
    eye = (row == col).astype(F32)
    half = _lane_half((t, 128))
    half_c = _lane_half((HEAD_DIM, 128))
    rowc = lax.broadcasted_iota(jnp.int32, (HEAD_DIM, 128), 0)
    lanec = lax.broadcasted_iota(jnp.int32, (HEAD_DIM, 128), 1)
    level_masks = []
    bsz = 2
    while bsz < CHUNK:
        level_masks.append(jnp.logical_and((row // (2 * bsz)) == (col // (2 * bsz)), (row // bsz) != (col // bsz)))
        bsz *= 2
    first_mask = (row // 2) == (col // 2)

    dirs = []
    for d in range(2):
        tri = tri_ref[d]
        wl = w0_ref[d] + _dot3(tanh_lora, wup_ref[d])
        z = -wl
        w_log = -(jnp.maximum(z, 0.0) + jnp.log(1.0 + jnp.exp(-jnp.abs(z)))) - 0.5
        lw = -jnp.exp(w_log)
        a = 1.0 / (1.0 + jnp.exp(-(a0_ref[d] + _dot3(lora, aup_ref[d]))))
        kd = k * (1.0 + (a - 1.0) * ka_ref[...])
        b = kk * a
        incl = tri > 0.5
        cs = _dot3(tri, lw)
        tot = _dot3(same, lw)
        w_inv = jnp.exp(-cs)
        w_rest = jnp.exp(tot - cs)
        dirs.append(dict(kd=kd, incl=incl, strict=jnp.logical_and(incl, row != col), tot=tot,
                         kkt=kk * jnp.exp(cs - lw), kh=kd * w_inv, bh=b * w_inv, rt=r * jnp.exp(cs),
                         kdd=kd * w_rest, bdd=b * w_rest))
    chains = [(dd, j) for dd in dirs for j in range(2)]
    sels = [half == j for _, j in chains]
    bms = [jnp.where(sel, dd["bh"], 0.0) for (dd, _), sel in zip(chains, sels)]
    kms = [jnp.where(sel, dd["kh"], 0.0) for (dd, _), sel in zip(chains, sels)]
    l_bs = [jnp.where(dd["strict"], _dot1(dd["kkt"], bm, _NT), 0.0) for (dd, _), bm in zip(chains, bms)]
    tinvs = [eye - jnp.where(first_mask, l_b, 0.0) for l_b in l_bs]
    for mask in level_masks:
        xs = [_dot1(jnp.where(mask, l_b, 0.0), tinv) for l_b, tinv in zip(l_bs, tinvs)]
        tinvs = [tinv - _dot1(tinv, x) for tinv, x in zip(tinvs, xs)]
    l_ks = [jnp.where(dd["strict"], _dot1(dd["kkt"], km, _NT), 0.0) for (dd, _), km in zip(chains, kms)]
    a_rks = [jnp.where(dd["incl"], _dot1(dd["rt"], km, _NT), 0.0) for (dd, _), km in zip(chains, kms)]
    a_rbs = [jnp.where(dd["incl"], _dot1(dd["rt"], bm, _NT), 0.0) for (dd, _), bm in zip(chains, bms)]
    lvs = [_dot1(l_k, v) for l_k in l_ks]
    pus = [_dot1(tinv, jnp.concatenate([dd["kkt"], lv], axis=1)) for (dd, _), tinv, lv in zip(chains, tinvs, lvs)]
    cors = [_dot1(a_rb, pu) for a_rb, pu in zip(a_rbs, pus)]
    ps = [pu[:, :128] for pu in pus]
    u0s = [pu[:, 128:] for pu in pus]
    qs = [dd["rt"] - cor[:, :128] for (dd, _), cor in zip(chains, cors)]
    y0s = [_dot1(a_rk, v) - cor[:, 128:] for a_rk, cor in zip(a_rks, cors)]

    sel0 = half == 0
    for d, dd in enumerate(dirs):
        p, u0, q, y0 = (jnp.where(sel0, x[2 * d], x[2 * d + 1]) for x in (ps, u0s, qs, y0s))
        for c in range(CPT):
            rs = slice(c * CHUNK, (c + 1) * CHUNK)
            x1 = _dot1(dd["bdd"][rs], p[rs], _TN)
            x2 = _dot1(dd["kdd"][rs], v[rs], _TN) - _dot1(dd["bdd"][rs], u0[rs], _TN)
            m_pair = jnp.where(half_c == 0, x1[:HEAD_DIM], x1[HEAD_DIM:])
            n_pair = jnp.where(half_c == 0, x2[:HEAD_DIM], x2[HEAD_DIM:])
            wc = jnp.exp(dd["tot"][c * CHUNK:c * CHUNK + 1])
            dg = jnp.where((lanec % HEAD_DIM) == rowc, wc, 0.0)
            g_ref[c, d, 0:HEAD_DIM, :] = dg - m_pair
            g_ref[c, d, HEAD_DIM:, :] = q[rs]
            add_ref[c, d, 0:HEAD_DIM, :] = n_pair
            add_ref[c, d, HEAD_DIM:, :] = y0[rs]

    kd_sum = dirs[0]["kd"] + dirs[1]["kd"]
    bonus_ref[...] = 0.5 * _dot3(r * kd_sum * rk_ref[...], bs) * v


def _lora_ext(up, first_row):
    out = jnp.zeros((2, 128, BRANCH_W), F32)
    for d in range(2):
        out = out.at[d, first_row + 32 * d:first_row + 32 * (d + 1)].set(up[d])
    return out


def rwkv_prep(rw, w0, w_up, a0, a_up, k_k, k_a, r_k):
    r = rw.shape[0]
    nt = r // ROW_TILE
    nch = r // CHUNK
    t = ROW_TILE
    ii = np.arange(t)
    same = (ii[:, None] // CHUNK) == (ii[None, :] // CHUNK)
    tri = jnp.asarray(np.stack([same & (ii[None, :] <= ii[:, None]), same & (ii[None, :] >= ii[:, None])]), F32)
    lane = lambda blk: pl.BlockSpec((t, 128), lambda i, p, blk=blk: (i, blk + p))
    pvec = pl.BlockSpec((1, 128), lambda i, p: (0, p))
    dvec = pl.BlockSpec((2, 1, 128), lambda i, p: (0, 0, p))
    dmat = pl.BlockSpec((2, 128, 128), lambda i, p: (0, 0, p))
    gspec = pl.BlockSpec((CPT, 2, HEAD_DIM + CHUNK, 128), lambda i, p: (i, 0, 0, p))
    gshape = jax.ShapeDtypeStruct((nch, 2, HEAD_DIM + CHUNK, BRANCH_W), F32)
    return pl.pallas_call(
        _rwkv_prep_body,
        out_shape=[gshape, gshape, jax.ShapeDtypeStruct((r, BRANCH_W), F32)],
        grid=(nt, N_HEADS // 2),
        in_specs=[lane(0), lane(4), lane(8), pl.BlockSpec((t, 128), lambda i, p: (i, 12)),
                  dvec, dmat, dvec, dmat, pvec, pvec, pvec,
                  pl.BlockSpec((2, t, t), lambda i, p: (0, 0, 0)),
                  pl.BlockSpec((128, 128), lambda i, p: (0, 0))],
        out_specs=[gspec, gspec, pl.BlockSpec((t, 128), lambda i, p: (i, p))],
        compiler_params=_cparams(("parallel", "parallel")),
        name="rwkv_prep",
    )(rw, rw, rw, rw, w0.reshape(2, 1, BRANCH_W), _lora_ext(w_up, 0), a0.reshape(2, 1, BRANCH_W),
      _lora_ext(a_up, 64), k_k[None], k_a[None], r_k.reshape(1, BRANCH_W), tri, _block_sum_mat(128, 1.0))


def _rwkv_scan_body(gf_ref, af_ref, gb_ref, ab_ref, yf_ref, yb_ref, st_ref):
    @pl.when(pl.program_id(0) == 0)
    def _():
        st_ref[...] = jnp.zeros(st_ref.shape, F32)

    rowh = lax.broadcasted_iota(jnp.int32, (128, 128), 0) // HEAD_DIM
    diag = rowh == _lane_half((128, 128))
    for d, (g_ref, a_ref, y_ref) in enumerate(((gf_ref, af_ref, yf_ref), (gb_ref, ab_ref, yb_ref))):
        for p in range(N_HEADS // 2):
            ls = slice(p * 128, (p + 1) * 128)
            out = _dot3(g_ref[0, 0, :, ls], st_ref[d, p]) + a_ref[0, 0, :, ls]
            hn = out[:HEAD_DIM]
            st_ref[d, p] = jnp.where(diag, jnp.concatenate([hn, hn], axis=0), 0.0)
            y_ref[:, ls] = out[HEAD_DIM:]


def rwkv_scan(g, add):
    nch = g.shape[0]
    r = nch * CHUNK
    nctx = CTX_LEN // CHUNK
    rev = lambda c: jnp.where(c < nctx, nctx - 1 - c, nch + nctx - 1 - c)
    blk = (1, 1, HEAD_DIM + CHUNK, BRANCH_W)
    fwd = pl.BlockSpec(blk, lambda c: (c, 0, 0, 0))
    bwd = pl.BlockSpec(blk, lambda c: (rev(c), 1, 0, 0))
    yshape = jax.ShapeDtypeStruct((r, BRANCH_W), F32)
    return pl.pallas_call(
        _rwkv_scan_body,
        out_shape=[yshape, yshape],
        grid=(nch,),
        in_specs=[fwd, fwd, bwd, bwd],
        out_specs=[pl.BlockSpec((CHUNK, BRANCH_W), lambda c: (c, 0)),
                   pl.BlockSpec((CHUNK, BRANCH_W), lambda c: (rev(c), 0))],
        scratch_shapes=[pltpu.VMEM((2, N_HEADS // 2, 128, 128), F32)],
        compiler_params=_cparams(("arbitrary",)),
        name="rwkv_scan",
    )(g, add, g, add)


HY_WIDTH = 512
HY_ORDER = 2
HY_POS_BANDS = 16
HY_HIDDEN = 64
HY_TAPS_W = 2 * HY_ORDER * HY_WIDTH
FFT_N2 = ROW_TILE


def _dot3c(ah, al, b):
    bh, bl = _split(b)
    dg = functools.partial(jnp.dot, preferred_element_type=F32)
    return dg(ah, bh) + (dg(al, bh) + dg(ah, bl))


def _dotc(ah, b):
    return jnp.dot(ah, b.astype(BF16), preferred_element_type=F32)


def _split_const(m):
    m = np.asarray(m, np.float32)
    hi = m.astype(BF16)
    lo = (m - hi.astype(np.float32)).astype(BF16)
    return jnp.asarray(hi), jnp.asarray(lo)


TAPS_FLAT_COLS = 8


def _filter_taps(t_idx, length, c2pb_ref, w1t_ref, w1c_ref, w1s_ref, b1_ref, w2_ref, b2_ref, w3_ref, b3_ref, absd_ref):
    t = t_idx / float(max(length - 1, 1))
    ang = c2pb_ref[...] * t_idx / float(length)
    pre = t * w1t_ref[...] + _dot3(jnp.cos(ang), w1c_ref[...]) - _dot3(jnp.sin(ang), w1s_ref[...]) + b1_ref[...]
    hid = jnp.sin(pre)
    hid = jnp.sin(_dot3(hid, w2_ref[...]) + b2_ref[...])
    return (_dot3(hid, w3_ref[...]) + b3_ref[...]) * jnp.exp(-t * absd_ref[...])


def _hy_taps_body(*refs, length):
    taps_ref, ssq_ref, tap0_ref = refs[-3:]
    i = pl.program_id(0)
    t_idx = (i * ROW_TILE + lax.broadcasted_iota(jnp.int32, (ROW_TILE, 1), 0)).astype(F32)
    taps = _filter_taps(t_idx, length, *refs[:-3])
    taps_ref[...] = taps

    @pl.when(i == 0)
    def _():
        ssq_ref[...] = jnp.zeros(ssq_ref.shape, F32)
        tap0_ref[...] = taps[0:1]

    ssq_ref[...] += jnp.sum(taps * taps, axis=0, keepdims=True)


def _hy_taps_flat_body(*refs, length):
    taps_ref, ssq_ref, tap0_ref = refs[-3:]
    j = pl.program_id(0)
    rows = length // FFT_N2
    base = lax.broadcasted_iota(jnp.int32, (rows, 1), 0) * FFT_N2 + j * TAPS_FLAT_COLS

    @pl.when(j == 0)
    def _():
        ssq_ref[...] = jnp.zeros(ssq_ref.shape, F32)

    for b in range(TAPS_FLAT_COLS):
        taps = _filter_taps((base + b).astype(F32), length, *refs[:-3])
        taps_ref[:, b * HY_TAPS_W:(b + 1) * HY_TAPS_W] = taps.astype(taps_ref.dtype)
        if b == 0:
            @pl.when(j == 0)
            def _():
                tap0_ref[...] = taps[0:1]
        ssq_ref[...] += jnp.sum(taps * taps, axis=0, keepdims=True)


def hyena_taps(length, w1, b1, w2, b2, w3, b3, *, flat):
    bands = jnp.linspace(1e-4, HY_POS_BANDS - 1, HY_POS_BANDS, dtype=F32)
    c2pb = jnp.zeros((1, 128), F32).at[0, :HY_POS_BANDS].set(2.0 * math.pi * bands)
    pad = lambda m: jnp.zeros((128, HY_HIDDEN), F32).at[:HY_POS_BANDS].set(m)
    deltas = jnp.linspace(math.log(1e-2) / 0.3, math.log(1e-2) / 1.5, HY_WIDTH, dtype=F32)
    absd = jnp.tile(jnp.abs(deltas), 2 * HY_ORDER)[None]
    const = lambda shp: pl.BlockSpec(shp, lambda i: (0,) * len(shp))
    h = HY_HIDDEN
    if flat:
        rows = length // FFT_N2
        body, grid = _hy_taps_flat_body, (FFT_N2 // TAPS_FLAT_COLS,)
        taps_shape, taps_dtype = (rows, FFT_N2 * HY_TAPS_W), BF16
        taps_spec = pl.BlockSpec((rows, TAPS_FLAT_COLS * HY_TAPS_W), lambda i: (0, i))
    else:
        body, grid = _hy_taps_body, (length // ROW_TILE,)
        taps_shape, taps_dtype = (length, HY_TAPS_W), F32
        taps_spec = pl.BlockSpec((ROW_TILE, HY_TAPS_W), lambda i: (i, 0))
    taps, ssq, tap0 = pl.pallas_call(
        functools.partial(body, length=length),
        out_shape=[jax.ShapeDtypeStruct(taps_shape, taps_dtype), jax.ShapeDtypeStruct((1, HY_TAPS_W), F32),
                   jax.ShapeDtypeStruct((1, HY_TAPS_W), F32)],
        grid=grid,
        in_specs=[const((1, 128)), const((1, h)), const((128, h)), const((128, h)), const((1, h)), const((h, h)),
                  const((1, h)), const((h, HY_TAPS_W)), const((1, HY_TAPS_W)), const((1, HY_TAPS_W))],
        out_specs=[taps_spec, const((1, HY_TAPS_W)), const((1, HY_TAPS_W))],
        compiler_params=_cparams(("arbitrary",)),
        name="hyena_taps_flat" if flat else "hyena_taps",
    )(c2pb, w1[0:1], pad(w1[1:1 + HY_POS_BANDS]), pad(w1[1 + HY_POS_BANDS:]), b1[None], w2, b2[None], w3, b3[None], absd)
    hw = HY_TAPS_W // 2
    norm2 = ssq[:, :hw] + ssq[:, hw:] + 2.0 * tap0[:, :hw] * tap0[:, hw:]
    return taps, lax.rsqrt(norm2)


class _FftPlan:
    def __init__(self, length):
        self.length = length
        self.n = 2 * length
        self.n2 = FFT_N2
        self.n1 = self.n // self.n2
        self.n1h = self.n1 // 2
        k1 = self.n1h + 1
        self.k1p = -(-k1 // 8) * 8
        kk = np.arange(self.k1p)[:, None].astype(np.float64)
        live = (kk < k1)
        nn = np.arange(self.n1h)[None, :].astype(np.float64)
        th = 2.0 * np.pi * kk * nn / self.n1
        self.f1 = _split_const(np.concatenate([np.cos(th) * live, -np.sin(th) * live], axis=0))
        ck = np.where((kk == 0) | (kk == self.n1h), 1.0, 2.0) * live / self.n
        self.g1 = _split_const(np.concatenate([np.cos(th) * ck, -np.sin(th) * ck], axis=0).T)
        m = np.arange(self.n2).astype(np.float64)
        ph = 2.0 * np.pi * np.outer(m, m) / self.n2
        c, s = np.cos(ph), np.sin(ph)
        self.fb = _split_const(np.block([[c, s], [-s, c]]))
        self.fbi = _split_const(np.block([[c, -s], [s, c]]))
        tw = 2.0 * np.pi * kk[:, :, None] * m[None, :, None] / self.n
        self.twc = jnp.asarray(np.cos(tw), F32)
        self.tws = jnp.asarray(np.sin(tw), F32)


FFT_TN = 4096


def _fft_a_body(fh_ref, x_ref, o_ref):
    o_ref[...] = _dotc(fh_ref[...], x_ref[...]).astype(o_ref.dtype)


def fft_stage_a(plan, xf, lead):
    rows_in, m = xf.shape
    rows = 2 * plan.k1p
    tn = min(FFT_TN, m)
    fh = jnp.pad(plan.f1[0], ((0, 0), (lead, 0)))
    fspec = pl.BlockSpec((rows, rows_in), lambda j: (0, 0))
    return pl.pallas_call(
        _fft_a_body,
        out_shape=jax.ShapeDtypeStruct((rows, m), BF16),
        grid=(m // tn,),
        in_specs=[fspec, pl.BlockSpec((rows_in, tn), lambda j: (0, j))],
        out_specs=pl.BlockSpec((rows, tn), lambda j: (0, j)),
        compiler_params=_cparams(("parallel",)),
        name="fft_stage_a",
    )(fh, xf)


def _twiddled(a_ref, twc_ref, tws_ref):
    are, aim = a_ref[0, 0].astype(F32), a_ref[1, 0].astype(F32)
    c, s = twc_ref[0], tws_ref[0]
    return jnp.concatenate([are * c + aim * s, aim * c - are * s], axis=0)


def _fft_filter_b_body(a_ref, twc_ref, tws_ref, fbh_ref, scale_ref, o_ref):
    n2 = FFT_N2
    x = _dotc(fbh_ref[...], _twiddled(a_ref, twc_ref, tws_ref))
    hw = HY_TAPS_W // 2
    xre, xim = x[:n2], x[n2:]
    o_ref[0, 0] = ((xre[:, :hw] + xre[:, hw:]) * scale_ref[...]).astype(o_ref.dtype)
    o_ref[1, 0] = ((xim[:, :hw] - xim[:, hw:]) * scale_ref[...]).astype(o_ref.dtype)


def fft_filter_stage_b(plan, a, scale):
    n2, k1p = plan.n2, plan.k1p
    hw = HY_TAPS_W // 2
    const = lambda shp: pl.BlockSpec(shp, lambda k: (0,) * len(shp))
    return pl.pallas_call(
        _fft_filter_b_body,
        out_shape=jax.ShapeDtypeStruct((2, k1p, n2, hw), BF16),
        grid=(k1p,),
        in_specs=[pl.BlockSpec((2, 1, n2, HY_TAPS_W), lambda k: (0, k, 0, 0)),
                  pl.BlockSpec((1, n2, 1), lambda k: (k, 0, 0)), pl.BlockSpec((1, n2, 1), lambda k: (k, 0, 0)),
                  const((2 * n2, 2 * n2)), const((1, hw))],
        out_specs=pl.BlockSpec((2, 1, n2, hw), lambda k: (0, k, 0, 0)),
        compiler_params=_cparams(("parallel",)),
        name="fft_filter_stage_b",
    )(a, plan.twc, plan.tws, plan.fb[0], scale)


def _fft_conv_b_body(a_ref, kf_ref, twc_ref, tws_ref, fbh_ref, fih_ref, o_ref):
    n2 = FFT_N2
    z = _dotc(fbh_ref[...], _twiddled(a_ref, twc_ref, tws_ref))
    zre, zim = z[:n2], z[n2:]
    kre, kim = kf_ref[0, 0].astype(F32), kf_ref[1, 0].astype(F32)
    y = jnp.concatenate([zre * kre - zim * kim, zre * kim + zim * kre], axis=0)
    q = _dotc(fih_ref[...], y)
    qre, qim = q[:n2], q[n2:]
    c, s = twc_ref[0], tws_ref[0]
    o_ref[0, 0] = (qre * c - qim * s).astype(o_ref.dtype)
    o_ref[1, 0] = (qim * c + qre * s).astype(o_ref.dtype)


def fft_conv_stage_b(plan, a, kf, order):
    n2, k1p = plan.n2, plan.k1p
    const = lambda shp: pl.BlockSpec(shp, lambda k: (0,) * len(shp))
    return pl.pallas_call(
        _fft_conv_b_body,
        out_shape=jax.ShapeDtypeStruct((2, k1p, n2, HY_WIDTH), BF16),
        grid=(k1p,),
        in_specs=[pl.BlockSpec((2, 1, n2, HY_WIDTH), lambda k: (0, k, 0, 0)),
                  pl.BlockSpec((2, 1, n2, HY_WIDTH), lambda k: (0, k, 0, order)),
                  pl.BlockSpec((1, n2, 1), lambda k: (k, 0, 0)), pl.BlockSpec((1, n2, 1), lambda k: (k, 0, 0)),
                  const((2 * n2, 2 * n2)), const((2 * n2, 2 * n2))],
        out_specs=pl.BlockSpec((2, 1, n2, HY_WIDTH), lambda k: (0, k, 0, 0)),
        compiler_params=_cparams(("parallel",)),
        name="fft_conv_stage_b",
    )(a, kf, plan.twc, plan.tws, plan.fb[0], plan.fbi[0])


def _fft_inv_a_body(gh_ref, q_ref, z_ref, gate_ref, skip_ref, o_ref):
    y = _dotc(gh_ref[...], q_ref[...])
    o_ref[...] = gate_ref[...] * (y + z_ref[...] * skip_ref[...])


def fft_inv_stage_a(plan, qf, zf, gatef, skip_t, lead):
    rows_out, m = zf.shape
    rows = 2 * plan.k1p
    tn = skip_t.shape[1]
    gh = jnp.pad(plan.g1[0], ((lead, 0), (0, 0)))
    col = lambda r_: pl.BlockSpec((r_, tn), lambda j: (0, j))
    gspec = pl.BlockSpec((rows_out, rows), lambda j: (0, 0))
    return pl.pallas_call(
        _fft_inv_a_body,
        out_shape=jax.ShapeDtypeStruct((rows_out, m), F32),
        grid=(m // tn,),
        in_specs=[gspec, col(rows), col(rows_out), col(rows_out), pl.BlockSpec((1, tn), lambda j: (0, 0))],
        out_specs=col(rows_out),
        compiler_params=_cparams(("parallel",)),
        name="fft_inv_stage_a",
    )(gh, qf, zf, gatef, skip_t)


def hyena_long(hv, hx1, hx2, taps, scale, skip, lead):
    length = taps.shape[0] * FFT_N2
    plan = _FftPlan(length)
    n2, k1p = plan.n2, plan.k1p
    m = n2 * HY_WIDTH
    tn = min(FFT_TN, m)
    flat = lambda a: a.reshape(lead + plan.n1h, m)
    ta = fft_stage_a(plan, taps, 0)
    kf = fft_filter_stage_b(plan, ta.reshape(2, k1p, n2, HY_TAPS_W), scale)
    z = flat(hv)
    for o, gate in enumerate((hx1, hx2)):
        a = fft_stage_a(plan, z, lead)
        q = fft_conv_stage_b(plan, a.reshape(2, k1p, n2, HY_WIDTH), kf, o)
        z = fft_inv_stage_a(plan, q.reshape(2 * k1p, m), z, flat(gate), jnp.tile(skip[o], tn // HY_WIDTH)[None], lead)
    return z.reshape(-1, HY_WIDTH)


def _rec_out_body(x_ref, yf_ref, yb_ref, bonus_ref, sgr_ref, zh_ref, zc_ref, sgh_ref, gnw_ref, gnb_ref, bm_ref, wa_ref, wb_ref,
                  gate_ref, fin_ref, o_ref, *, final):
    y = yf_ref[...] + yb_ref[...]
    bm = bm_ref[...]
    mean = _dot_stat(y, bm)
    yc = y - mean
    var = _dot_stat(yc * yc, bm)
    yn = yc * lax.rsqrt(var + RWKV_GN_EPS) * gnw_ref[...] + gnb_ref[...]
    ya = ((yn + bonus_ref[...]) * sgr_ref[...]).astype(BF16)
    zh = zh_ref[...]
    if not final:
        zh = jnp.where(pl.program_id(0) == 0, zc_ref[...], zh)
    yh = (zh * sgh_ref[...]).astype(BF16)
    out = jnp.dot(ya, wa_ref[...], preferred_element_type=F32) + jnp.dot(yh, wb_ref[...], preferred_element_type=F32)
    xn = x_ref[...] + gate_ref[0] * out
    if final:
        xn = xn * lax.rsqrt(jnp.mean(xn * xn, axis=-1, keepdims=True) + NORM_EPS) * fin_ref[...]
    o_ref[...] = xn


RWKV_GN_EPS = 64e-5


def rec_out_proj(xs, yf, yb, bonus, sgr, zh, zc, sgh, gn_w, gn_b, w_out, gate2, final_g, *, final):
    r = xs.shape[0]
    d = D_MODEL
    off = 1 if final else 0
    nt = r // ROW_TILE - off
    rows = lambda w: pl.BlockSpec((ROW_TILE, w), lambda i: (i + off, 0))
    const = lambda shp: pl.BlockSpec(shp, lambda i: (0,) * len(shp))
    bw = BRANCH_W
    return pl.pallas_call(
        functools.partial(_rec_out_body, final=final),
        out_shape=jax.ShapeDtypeStruct((nt * ROW_TILE, d), F32),
        grid=(nt,),
        in_specs=[rows(d), rows(bw), rows(bw), rows(bw), rows(bw), rows(bw), const((CTX_LEN, bw)), rows(bw),
                  const((1, bw)), const((1, bw)),
                  const((bw, bw)), const((bw, d)), const((bw, d)),
                  pl.BlockSpec((1, 1, d), lambda i: (jnp.minimum(i + off, 1), 0, 0)), const((1, d))],
        out_specs=pl.BlockSpec((ROW_TILE, d), lambda i: (i, 0)),
        compiler_params=_cparams(("parallel",)),
        name="rec_out_proj",
    )(xs, yf, yb, bonus, sgr, zh, zc, sgh, gn_w[None], gn_b[None], _block_sum_mat(bw, 1.0 / HEAD_DIM).astype(BF16),
      w_out[:bw].astype(BF16), w_out[bw:].astype(BF16), gate2, final_g[None])


def rec_layer(xs, mods, norm_g, w_in, mu, w0, w_up, a0, a_up, k_k, k_a, r_k, gn_w, gn_b, hy_short, hy_w1, hy_b1, hy_w2,
              hy_b2, hy_w3, hy_b3, hy_skip, w_out, final_g, final):
    shift2, scale2, gate2 = mods
    rw, hv, hx1, hx2, sgr, sgh = rec_in_proj(xs, norm_g, scale2, shift2, w_in, mu, hy_short)
    g, add, bonus = rwkv_prep(rw, w0, w_up, a0, a_up, k_k, k_a, r_k)
    yf, yb = rwkv_scan(g, add)
    fargs = (hy_w1, hy_b1, hy_w2, hy_b2, hy_w3, hy_b3)
    n = xs.shape[0] - CTX_LEN
    taps, scale = hyena_taps(n, *fargs, flat=True)
    zh = hyena_long(hv, hx1, hx2, taps, scale, hy_skip, CTX_LEN // FFT_N2)
    if final:
        z_ctx = hv[:CTX_LEN]
    else:
        taps_c, scale_c = hyena_taps(CTX_LEN, *fargs, flat=False)
        z_ctx = hyena_short(hv[:CTX_LEN], hx1[:CTX_LEN], hx2[:CTX_LEN], taps_c, scale_c, hy_skip)
    return rec_out_proj(xs, yf, yb, bonus, sgr, zh, z_ctx, sgh, gn_w, gn_b, w_out, gate2, final_g, final=final)


def kernel(x, c, ctx, c_ctx, attn_norm, attn_ada_w, attn_ada_b, attn_w_in, na_rpb, gqa_q_gain, gqa_k_gain, attn_w_out,
           rec_norm, rec_ada_w, rec_ada_b, rec_w_in, rwkv_mu, rwkv_w0, rwkv_w_up, rwkv_a0, rwkv_a_up, rwkv_k_k, rwkv_k_a,
           rwkv_r_k, rwkv_gn_w, rwkv_gn_b, hy_short, hy_w1, hy_b1, hy_w2, hy_b2, hy_w3, hy_b3, hy_skip, rec_w_out,
           final_norm):
    assert x.shape[0] == 1 and ctx.shape[1] == CTX_LEN and x.shape[2] == D_MODEL
    n = x.shape[1]
    assert n % ROW_TILE == 0 and n // ROW_TILE >= 3
    assert attn_w_in.shape[0] == rec_w_in.shape[0]
    d = D_MODEL
    cond8 = jnp.zeros((8, d), F32).at[0].set(c_ctx).at[1].set(c[0])
    m_attn = adaln_all(cond8, attn_ada_w, attn_ada_b)
    m_rec = adaln_all(cond8, rec_ada_w, rec_ada_b)
    mods = lambda m, i: tuple(m[i, :2, j * d:(j + 1) * d].reshape(2, 1, d) for j in range(3))
    cos_t, sin_t = _rope_tables(n)
    xs = jnp.concatenate([ctx[0], x[0]], axis=0)
    depth = attn_w_in.shape[0] + rec_w_in.shape[0]
    for layer in range(depth):
        i = layer // 2
        final = layer == depth - 1
        if layer % 2 == 0:
            xs = attn_layer(xs, mods(m_attn, i), attn_norm[i], attn_w_in[i], na_rpb[i], gqa_q_gain[i], gqa_k_gain[i],
                            attn_w_out[i], cos_t, sin_t)
        else:
            xs = rec_layer(xs, mods(m_rec, i), rec_norm[i], rec_w_in[i], rwkv_mu[i], rwkv_w0[i], rwkv_w_up[i],
                           rwkv_a0[i], rwkv_a_up[i], rwkv_k_k[i], rwkv_k_a[i], rwkv_r_k[i], rwkv_gn_w[i], rwkv_gn_b[i],
                           hy_short[i], hy_w1[i], hy_b1[i], hy_w2[i], hy_b2[i], hy_w3[i], hy_b3[i], hy_skip[i],
                           rec_w_out[i], final_norm, final)
    return xs[None]


def _hy_short_body(fh_ref, fl_ref, gh_ref, gl_ref, v_ref, x1_ref, x2_ref, taps_ref, scale_ref, skip_ref, o_ref):
    fh, fl, gh, gl = fh_ref[...], fl_ref[...], gh_ref[...], gl_ref[...]
    kp = fh.shape[0] // 2
    hw = HY_TAPS_W // 2
    tf = _dot3c(fh, fl, taps_ref[...])
    kre = (tf[:kp, :hw] + tf[:kp, hw:]) * scale_ref[...]
    kim = (tf[kp:, :hw] - tf[kp:, hw:]) * scale_ref[...]
    z = v_ref[...]
    for o, gate_ref in enumerate((x1_ref, x2_ref)):
        ls = slice(o * HY_WIDTH, (o + 1) * HY_WIDTH)
        zf = _dot3c(fh, fl, z)
        zre, zim = zf[:kp], zf[kp:]
        y = jnp.concatenate([zre * kre[:, ls] - zim * kim[:, ls], zre * kim[:, ls] + zim * kre[:, ls]], axis=0)
        z = gate_ref[...] * (_dot3c(gh, gl, y) + z * skip_ref[o:o + 1])
    o_ref[...] = z


def hyena_short(hv, hx1, hx2, taps, scale, skip):
    length = hv.shape[0]
    n = 2 * length
    k1 = length + 1
    kp = -(-k1 // 8) * 8
    kk = np.arange(kp)[:, None].astype(np.float64)
    live = kk < k1
    th = 2.0 * np.pi * kk * np.arange(length)[None, :] / n
    f = _split_const(np.concatenate([np.cos(th) * live, -np.sin(th) * live], axis=0))
    ck = np.where((kk == 0) | (kk == length), 1.0, 2.0) * live / n
    g = _split_const(np.concatenate([np.cos(th) * ck, -np.sin(th) * ck], axis=0).T)
    return pl.pallas_call(
        _hy_short_body,
        out_shape=jax.ShapeDtypeStruct((length, HY_WIDTH), F32),
        compiler_params=pltpu.CompilerParams(vmem_limit_bytes=VMEM_LIMIT),
        name="hyena_short",
    )(f[0], f[1], g[0], g[1], hv, hx1, hx2, taps, scale, skip)
```

```python
import functools
import math

import jax
import jax.numpy as jnp
import numpy as np
from jax import lax
from jax.experimental import pallas as pl
from jax.experimental.pallas import tpu as pltpu

F32 = jnp.float32
BF16 = jnp.bfloat16
HIGHEST = lax.Precision.HIGHEST

D_MODEL = 1024
GRID_W = 64
CTX_LEN = 256
HEAD_DIM = 64
BRANCH_W = 512
N_HEADS = 8
GQA_KV_W = 128
NA_WIN_ROWS = 8
NA_WIN_COLS = 16
ROPE_THETA = 10000.0
ROPE_FREQS = 16
NORM_EPS = 1e-6
ROW_TILE = 256
NA_GROUP_ROWS = 4
NEG_BIG = -1e30
LOG2E = math.log2(math.e)
QK_SCALE = HEAD_DIM ** -0.5 * LOG2E
VMEM_LIMIT = 56 * 1024 * 1024

ATTN_SPLITS = (512, 512, 512, 512, 512, 128, 128, 512)
GQA_HEAD_ORDER = (0, 4, 1, 5, 2, 6, 3, 7)


def _cparams(sem):
    return pltpu.CompilerParams(dimension_semantics=sem, vmem_limit_bytes=VMEM_LIMIT)


def _silu(v):
    return v * (1.0 / (1.0 + jnp.exp(-v)))


def _lane_half(shape):
    return (lax.broadcasted_iota(jnp.int32, shape, len(shape) - 1) // HEAD_DIM) % 2


def _dot_stat(a, block_mat):
    hi = a.astype(BF16)
    lo = (a - hi.astype(F32)).astype(BF16)
    return jnp.dot(hi, block_mat, preferred_element_type=F32) + jnp.dot(lo, block_mat, preferred_element_type=F32)


def _dot_nt(a, b):
    return lax.dot_general(a, b, (((1,), (1,)), ((), ())), preferred_element_type=F32)


def _adaln_body(cond_ref, w_ref, b_ref, o_ref):
    s = _silu(cond_ref[...])
    o_ref[0] = jnp.dot(s, w_ref[0], precision=HIGHEST, preferred_element_type=F32) + b_ref[0]


def adaln_all(cond8, ada_w, ada_b):
    nl = ada_w.shape[0]
    d = D_MODEL
    return pl.pallas_call(
        _adaln_body,
        out_shape=jax.ShapeDtypeStruct((nl, 8, 3 * d), F32),
        grid=(nl, 3),
        in_specs=[
            pl.BlockSpec((8, d), lambda l, j: (0, 0)),
            pl.BlockSpec((1, d, d), lambda l, j: (l, 0, j)),
            pl.BlockSpec((1, 1, d), lambda l, j: (l, 0, j)),
        ],
        out_specs=pl.BlockSpec((1, 8, d), lambda l, j: (l, 0, j)),
        compiler_params=_cparams(("parallel", "parallel")),
        name="adaln",
    )(cond8, ada_w, ada_b.reshape(nl, 1, 3 * d))


def _modulated(x_ref, g_ref, scale_ref, shift_ref):
    xf = x_ref[...]
    y = xf * lax.rsqrt(jnp.mean(xf * xf, axis=-1, keepdims=True) + NORM_EPS)
    return (y * g_ref[...]) * (1.0 + scale_ref[0]) + shift_ref[0]


def _mod_specs():
    d = D_MODEL
    return [
        pl.BlockSpec((ROW_TILE, d), lambda i: (i, 0)),
        pl.BlockSpec((1, d), lambda i: (0, 0)),
        pl.BlockSpec((1, 1, d), lambda i: (jnp.minimum(i, 1), 0, 0)),
        pl.BlockSpec((1, 1, d), lambda i: (jnp.minimum(i, 1), 0, 0)),
    ]


def _attn_in_body(x_ref, g_ref, scale_ref, shift_ref, w_ref, cos_ref, sin_ref, gq_ref, gqs_ref, gk_ref, gks_ref,
                  bdq_ref, bdk_ref,
                  qa_ref, ka_ref, va_ref, sga_ref, qb_ref, kb_ref, vb_ref, sgb_ref):
    xm = _modulated(x_ref, g_ref, scale_ref, shift_ref).astype(BF16)
    u = jnp.dot(xm, w_ref[...], preferred_element_type=F32)
    qa, ka, va, ga = u[:, 0:512], u[:, 512:1024], u[:, 1024:1536], u[:, 1536:2048]
    qb, kb, vb, gb = u[:, 2048:2560], u[:, 2560:2688], u[:, 2688:2816], u[:, 2816:3328]
    qbs, kbs = u[:, 3328:3840], u[:, 3840:3968]
    scale = QK_SCALE
    qa_ref[...] = (qa * scale).astype(BF16)
    ka_ref[...] = ka.astype(BF16)
    va_ref[...] = va.astype(BF16)
    sga_ref[...] = _silu(ga)
    sgb_ref[...] = _silu(gb)
    vb_ref[...] = jnp.transpose(vb).astype(BF16)
    cos_k, sin_k = cos_ref[...], sin_ref[...]
    cos_q = jnp.concatenate([cos_k] * 4, axis=1)
    sin_q = jnp.concatenate([sin_k] * 4, axis=1)
    rs_q = lax.rsqrt(_dot_stat(qb * qb, bdq_ref[...]) + NORM_EPS)
    rs_k = lax.rsqrt(_dot_stat(kb * kb, bdk_ref[...]) + NORM_EPS)
    qr = rs_q * (qb * gq_ref[...] * cos_q + qbs * gqs_ref[...] * sin_q)
    kr = rs_k * (kb * gk_ref[...] * cos_k + kbs * gks_ref[...] * sin_k)
    qb_ref[...] = jnp.transpose(qr * scale).astype(BF16)
    kb_ref[...] = kr.astype(BF16)


def _rope_tables(n):
    t = jnp.arange(n, dtype=jnp.int32)
    pos = jnp.stack([t // GRID_W, t % GRID_W], axis=-1).astype(F32)
    inv_freq = ROPE_THETA ** (-jnp.arange(ROPE_FREQS, dtype=F32) / ROPE_FREQS)
    ang = pos[:, :, None] * inv_freq
    c, s = jnp.cos(ang), jnp.sin(ang)
    cos64 = jnp.concatenate([c[:, 0], c[:, 0], c[:, 1], c[:, 1]], axis=-1)
    sin64 = jnp.concatenate([-s[:, 0], s[:, 0], -s[:, 1], s[:, 1]], axis=-1)
    cos64 = jnp.concatenate([jnp.ones((CTX_LEN, HEAD_DIM), F32), cos64], axis=0)
    sin64 = jnp.concatenate([jnp.zeros((CTX_LEN, HEAD_DIM), F32), sin64], axis=0)
    return jnp.tile(cos64, (1, 2)), jnp.tile(sin64, (1, 2))


def _reorder_heads(w, order, axis):
    take = lambda h: lax.slice_in_dim(w, h * HEAD_DIM, (h + 1) * HEAD_DIM, axis=axis)
    return jnp.concatenate([take(h) for h in order], axis=axis)


def _swap_rope_halves(w):
    shp = w.shape
    return jnp.flip(w.reshape(shp[:-1] + (shp[-1] // (2 * ROPE_FREQS), 2, ROPE_FREQS)), axis=-2).reshape(shp)


def attn_in_proj(xs, norm_g, scale2, shift2, w_in, q_gain, k_gain, cos_t, sin_t):
    r = xs.shape[0]
    d = D_MODEL
    parts, start = [], 0
    for s in ATTN_SPLITS:
        parts.append(w_in[:, start:start + s])
        start += s
    wqa, wka, wva, wga, wqb, wkb, wvb, wgb = parts
    wqb_p = _reorder_heads(wqb, GQA_HEAD_ORDER, 1)
    wgb_p = _reorder_heads(wgb, GQA_HEAD_ORDER, 1)
    wqb_sw = _swap_rope_halves(wqb_p)
    wkb_sw = _swap_rope_halves(wkb)
    w_ext = jnp.concatenate([wqa, wka, wva, wga, wqb_p, wkb, wvb, wgb_p, wqb_sw, wkb_sw], axis=1).astype(BF16)
    gq = jnp.tile(q_gain, N_HEADS)[None]
    gqs = jnp.tile(_swap_rope_halves(q_gain), N_HEADS)[None]
    gk = jnp.tile(k_gain, 2)[None]
    gks = jnp.tile(_swap_rope_halves(k_gain), 2)[None]
    bdq = jnp.asarray(np.kron(np.eye(N_HEADS), np.full((HEAD_DIM, HEAD_DIM), 1.0 / HEAD_DIM)), BF16)
    bdk = jnp.asarray(np.kron(np.eye(2), np.full((HEAD_DIM, HEAD_DIM), 1.0 / HEAD_DIM)), BF16)
    wcols = w_ext.shape[1]
    const = lambda shp: pl.BlockSpec(shp, lambda i: (0,) * len(shp))
    rows = lambda w: pl.BlockSpec((ROW_TILE, w), lambda i: (i, 0))
    out_shapes = [
        jax.ShapeDtypeStruct((r, 512), BF16), jax.ShapeDtypeStruct((r, 512), BF16), jax.ShapeDtypeStruct((r, 512), BF16),
        jax.ShapeDtypeStruct((r, 512), F32),
        jax.ShapeDtypeStruct((512, r), BF16), jax.ShapeDtypeStruct((r, 128), BF16), jax.ShapeDtypeStruct((128, r), BF16),
        jax.ShapeDtypeStruct((r, 512), F32),
    ]
    cols = lambda w: pl.BlockSpec((w, ROW_TILE), lambda i: (0, i))
    return pl.pallas_call(
        _attn_in_body,
        out_shape=out_shapes,
        grid=(r // ROW_TILE,),
        in_specs=_mod_specs() + [const((d, wcols)), rows(128), rows(128), const((1, 512)), const((1, 512)),
                                 const((1, 128)), const((1, 128)), const((512, 512)), const((128, 128))],
        out_specs=[rows(512), rows(512), rows(512), rows(512), cols(512), rows(128), cols(128), rows(512)],
        compiler_params=_cparams(("parallel",)),
        name="attn_in_proj",
    )(xs, norm_g[None], scale2, shift2, w_ext, cos_t, sin_t, gq, gqs, gk, gks, bdq, bdk)


def _na_cols_body(rpb_ref, sel_ref, neg_ref, o_ref):
    o_ref[...] = jnp.dot(rpb_ref[...], sel_ref[...], precision=HIGHEST, preferred_element_type=F32) + neg_ref[...]


def _na_bias_tables(rpb, rows):
    nrel_r, nrel_c = 2 * NA_WIN_ROWS - 1, 2 * NA_WIN_COLS - 1
    qc = np.arange(GRID_W)[:, None]
    kc = np.arange(GRID_W)[None, :]
    col0 = np.clip(qc - NA_WIN_COLS // 2, 0, GRID_W - NA_WIN_COLS)
    col_ok = (kc >= col0) & (kc < col0 + NA_WIN_COLS)
    rc = kc - qc + NA_WIN_COLS - 1
    sel = np.zeros((128, GRID_W * GRID_W), np.float32)
    sel[np.where(col_ok, rc, 127).reshape(-1), np.arange(GRID_W * GRID_W)] = col_ok.reshape(-1)
    neg = np.where(col_ok, 0.0, NEG_BIG).astype(np.float32).reshape(1, -1)
    rpb2 = jnp.zeros((128, 128), F32).at[:N_HEADS * nrel_r, :nrel_c].set(rpb.reshape(N_HEADS * nrel_r, nrel_c))
    cols = pl.pallas_call(
        _na_cols_body,
        out_shape=jax.ShapeDtypeStruct((128, GRID_W * GRID_W), F32),
        name="na_bias_cols",
    )(rpb2, jnp.asarray(sel), jnp.asarray(neg))
    cols = cols[:N_HEADS * nrel_r].reshape(N_HEADS, nrel_r, GRID_W, GRID_W)
    kh = min(NA_WIN_ROWS, rows)
    g = rows // NA_GROUP_ROWS
    cases = [(0, 0), (NA_GROUP_ROWS, 0), (rows - NA_GROUP_ROWS, NA_GROUP_ROWS * (g - 3))]
    masked = jnp.full((N_HEADS, GRID_W, GRID_W), NEG_BIG, F32)
    tabs = []
    for qr_first, start in cases:
        blocks = []
        for j in range(NA_GROUP_ROWS):
            qr = qr_first + j
            row0 = min(max(qr - kh // 2, 0), rows - kh)
            for i in range(3 * NA_GROUP_ROWS):
                kr = start + i
                blocks.append(cols[:, kr - qr + NA_WIN_ROWS - 1] if row0 <= kr < row0 + kh else masked)
        tab = jnp.stack(blocks, axis=1).reshape(N_HEADS, NA_GROUP_ROWS, 3 * NA_GROUP_ROWS, GRID_W, GRID_W)
        tabs.append(tab.transpose(0, 1, 3, 2, 4).reshape(N_HEADS, ROW_TILE, 3 * ROW_TILE))
    return jnp.stack(tabs) * LOG2E


def _na_body(q_ref, kc_ref, k0_ref, k1_ref, k2_ref, vc_ref, v0_ref, v1_ref, v2_ref, bias_ref, sg_ref, o_ref):
    half = _lane_half((ROW_TILE, 128))
    k_refs = (k0_ref, k1_ref, k2_ref, kc_ref)
    v_refs = (v0_ref, v1_ref, v2_ref, vc_ref)

    def scores(h):
        hp, j = divmod(h, 2)
        ls = slice(hp * 128, (hp + 1) * 128)
        qp = q_ref[:, ls]
        qm = jnp.where(half == j, qp, jnp.zeros_like(qp))
        return [_dot_nt(qm, r[:, ls]) for r in k_refs]

    s_next = scores(0)
    outs = []
    for h in range(N_HEADS):
        hp, j = divmod(h, 2)
        ls = slice(hp * 128, (hp + 1) * 128)
        s = s_next
        if h + 1 < N_HEADS:
            s_next = scores(h + 1)
        s_win = jnp.concatenate(s[:3], axis=1) + bias_ref[0, h]
        s_ctx = s[3]
        m = jnp.maximum(jnp.max(s_win, axis=1, keepdims=True), jnp.max(s_ctx, axis=1, keepdims=True))
        p_win = jnp.exp2(s_win - m).astype(BF16)
        p_ctx = jnp.exp2(s_ctx - m).astype(BF16)
        vms = [jnp.where(half == j, r[:, ls], jnp.ones((ROW_TILE, 128), BF16)) for r in v_refs]
        o = jnp.dot(p_ctx, vms[3], preferred_element_type=F32)
        for b in range(3):
            o += jnp.dot(p_win[:, b * ROW_TILE:(b + 1) * ROW_TILE], vms[b], preferred_element_type=F32)
        outs.append(o / pltpu.roll(o, HEAD_DIM, 1))
        if j == 1:
            o_pair = jnp.where(half == 0, outs[h - 1], outs[h])
            o_ref[:, ls] = (o_pair * sg_ref[:, ls]).astype(BF16)


def na_attention(qa, ka, va, sga, bias_tabs, n):
    g = n // ROW_TILE
    w = BRANCH_W

    def kv_spec(off):
        return pl.BlockSpec((ROW_TILE, w), lambda i: (jnp.clip(i - 1, 0, g - 3) + off + 1, 0))

    ctx_spec = pl.BlockSpec((ROW_TILE, w), lambda i: (0, 0))
    q_spec = pl.BlockSpec((ROW_TILE, w), lambda i: (i + 1, 0))
    case = lambda i: jnp.where(i == 0, 0, jnp.where(i == g - 1, 2, 1))
    bias_spec = pl.BlockSpec((1, N_HEADS, ROW_TILE, 3 * ROW_TILE), lambda i: (case(i), 0, 0, 0))
    return pl.pallas_call(
        _na_body,
        out_shape=jax.ShapeDtypeStruct((n, w), BF16),
        grid=(g,),
        in_specs=[q_spec, ctx_spec, kv_spec(0), kv_spec(1), kv_spec(2), ctx_spec, kv_spec(0), kv_spec(1), kv_spec(2),
                  bias_spec, q_spec],
        out_specs=pl.BlockSpec((ROW_TILE, w), lambda i: (i, 0)),
        compiler_params=_cparams(("parallel",)),
        name="na_attention",
    )(qa, ka, ka, ka, ka, va, va, va, va, bias_tabs, sga)


def _flash_mha_body(q_ref, k_ref, v_ref, sg_ref, o_ref, m_ref, acc_ref):
    kv = pl.program_id(1)
    tq = q_ref.shape[0]

    @pl.when(kv == 0)
    def _():
        m_ref[...] = jnp.full(m_ref.shape, NEG_BIG, F32)
        acc_ref[...] = jnp.zeros(acc_ref.shape, F32)

    khalf = _lane_half((k_ref.shape[0], 128))
    for p in range(N_HEADS // 2):
        ls = slice(p * 128, (p + 1) * 128)
        qp = q_ref[:, ls]
        kp = k_ref[:, ls]
        vp = v_ref[:, ls]
        for j in range(2):
            hh = 2 * p + j
            km = jnp.where(khalf == j, kp, jnp.zeros_like(kp))
            vm = jnp.where(khalf == j, vp, jnp.ones_like(vp))
            s = _dot_nt(qp, km)
            m_prev = m_ref[hh]
            m_new = jnp.maximum(m_prev, jnp.max(s, axis=1, keepdims=True))
            alpha = jnp.exp2(m_prev - m_new)
            pr = jnp.exp2(s - m_new[:, :1]).astype(BF16)
            acc_ref[hh] = alpha * acc_ref[hh] + jnp.dot(pr, vm, preferred_element_type=F32)
            m_ref[hh] = m_new

    @pl.when(kv == pl.num_programs(1) - 1)
    def _():
        half = _lane_half((tq, 128))
        for p in range(N_HEADS // 2):
            ls = slice(p * 128, (p + 1) * 128)
            a0, a1 = acc_ref[2 * p], acc_ref[2 * p + 1]
            o0 = a0 / pltpu.roll(a0, HEAD_DIM, 1)
            o1 = a1 / pltpu.roll(a1, HEAD_DIM, 1)
            o_ref[:, ls] = (jnp.where(half == 0, o0, o1) * sg_ref[:, ls]).astype(BF16)


def flash_mha(q, k, v, sg, *, q_block0, nq, tk, nk):
    tq = ROW_TILE
    return pl.pallas_call(
        _flash_mha_body,
        out_shape=jax.ShapeDtypeStruct((nq * tq, BRANCH_W), BF16),
        grid=(nq, nk),
        in_specs=[
            pl.BlockSpec((tq, BRANCH_W), lambda i, j: (i + q_block0, 0)),
            pl.BlockSpec((tk, BRANCH_W), lambda i, j: (j, 0)),
            pl.BlockSpec((tk, BRANCH_W), lambda i, j: (j, 0)),
            pl.BlockSpec((tq, BRANCH_W), lambda i, j: (i + q_block0, 0)),
        ],
        out_specs=pl.BlockSpec((tq, BRANCH_W), lambda i, j: (i, 0)),
        scratch_shapes=[pltpu.VMEM((N_HEADS, tq, 128), F32)] * 2,
        compiler_params=_cparams(("parallel", "arbitrary")),
        name="flash_mha",
    )(q, k, v, sg)


def _flash_gqa_body(qt_ref, k_ref, vt_ref, sg_ref, o_ref, m_ref, acc_ref):
    kv = pl.program_id(1)
    tq = qt_ref.shape[1]
    tk = k_ref.shape[0]

    @pl.when(kv == 0)
    def _():
        m_ref[...] = jnp.full(m_ref.shape, NEG_BIG, F32)
        acc_ref[...] = jnp.zeros(acc_ref.shape, F32)

    khalf = _lane_half((tk, 128))
    vhalf = lax.broadcasted_iota(jnp.int32, (128, tk), 0) // HEAD_DIM
    kb = k_ref[...]
    vt = vt_ref[...]
    kms = [jnp.where(khalf == j, kb, jnp.zeros_like(kb)) for j in range(2)]
    vms = [jnp.where(vhalf == j, vt, jnp.ones_like(vt)) for j in range(2)]

    def scores(hh):
        p, j = divmod(hh, 2)
        return jnp.dot(kms[j], qt_ref[p * 128:(p + 1) * 128, :], preferred_element_type=F32).astype(BF16)

    st_next = scores(0)
    for hh in range(N_HEADS):
        st = st_next
        if hh + 1 < N_HEADS:
            st_next = scores(hh + 1)
        m_prev = m_ref[hh]
        m_new = jnp.maximum(m_prev, jnp.max(st, axis=0, keepdims=True).astype(F32))
        alpha = jnp.exp2(m_prev - m_new)
        pt = jnp.exp2(st - m_new[0:1].astype(BF16))
        acc_ref[hh] = alpha[0:1] * acc_ref[hh] + jnp.dot(vms[hh % 2], pt, preferred_element_type=F32)
        m_ref[hh] = m_new

    @pl.when(kv == pl.num_programs(1) - 1)
    def _():
        for p in range(N_HEADS // 2):
            ls = slice(p * 128, (p + 1) * 128)
            a0, a1 = acc_ref[2 * p], acc_ref[2 * p + 1]
            ot = jnp.concatenate([a0[:HEAD_DIM] / a0[HEAD_DIM:], a1[HEAD_DIM:] / a1[:HEAD_DIM]], axis=0)
            o_ref[:, ls] = (jnp.transpose(ot) * sg_ref[:, ls]).astype(BF16)


def flash_gqa(qt, k, vt, sg, *, q_block0, nq, tk, nk):
    tq = ROW_TILE
    return pl.pallas_call(
        _flash_gqa_body,
        out_shape=jax.ShapeDtypeStruct((nq * tq, BRANCH_W), BF16),
        grid=(nq, nk),
        in_specs=[
            pl.BlockSpec((BRANCH_W, tq), lambda i, j: (0, i + q_block0)),
            pl.BlockSpec((tk, GQA_KV_W), lambda i, j: (j, 0)),
            pl.BlockSpec((GQA_KV_W, tk), lambda i, j: (0, j)),
            pl.BlockSpec((tq, BRANCH_W), lambda i, j: (i + q_block0, 0)),
        ],
        out_specs=pl.BlockSpec((tq, BRANCH_W), lambda i, j: (i, 0)),
        scratch_shapes=[pltpu.VMEM((N_HEADS, 8, tq), F32), pltpu.VMEM((N_HEADS, 128, tq), F32)],
        compiler_params=_cparams(("parallel", "arbitrary")),
        name="flash_gqa",
    )(qt, k, vt, sg)


def _out_body(x_ref, ya_ref, yac_ref, yb_ref, ybc_ref, wa_ref, wb_ref, gate_ref, o_ref):
    is_ctx = pl.program_id(0) == 0
    ya = jnp.where(is_ctx, yac_ref[...], ya_ref[...])
    yb = jnp.where(is_ctx, ybc_ref[...], yb_ref[...])
    y = jnp.dot(ya, wa_ref[...], preferred_element_type=F32)
    y += jnp.dot(yb, wb_ref[...], preferred_element_type=F32)
    o_ref[...] = x_ref[...] + gate_ref[0] * y


def out_proj(xs, ya_lat, ya_ctx, yb_lat, yb_ctx, wa, wb, gate2):
    r = xs.shape[0]
    d = D_MODEL
    rows = lambda w: pl.BlockSpec((ROW_TILE, w), lambda i: (i, 0))
    lat = pl.BlockSpec((ROW_TILE, BRANCH_W), lambda i: (jnp.maximum(i - 1, 0), 0))
    const = lambda shp: pl.BlockSpec(shp, lambda i: (0,) * len(shp))
    ctx = const((CTX_LEN, BRANCH_W))
    return pl.pallas_call(
        _out_body,
        out_shape=jax.ShapeDtypeStruct((r, d), F32),
        grid=(r // ROW_TILE,),
        in_specs=[rows(d), lat, ctx, lat, ctx, const((BRANCH_W, d)), const((BRANCH_W, d)),
                  pl.BlockSpec((1, 1, d), lambda i: (jnp.minimum(i, 1), 0, 0))],
        out_specs=rows(d),
        compiler_params=_cparams(("parallel",)),
        name="out_proj",
    )(xs, ya_lat, ya_ctx, yb_lat, yb_ctx, wa.astype(BF16), wb.astype(BF16), gate2)


KV_TILE_MAX_BLOCKS = 13


def _kv_tile(r):
    nb = r // ROW_TILE
    best = max(k for k in range(1, KV_TILE_MAX_BLOCKS + 1) if nb % k == 0)
    return best * ROW_TILE, nb // best


def attn_layer(xs, mods, norm_g, w_in, rpb, q_gain, k_gain, w_out, cos_t, sin_t):
    r = xs.shape[0]
    n = r - CTX_LEN
    shift2, scale2, gate2 = mods
    qa, ka, va, sga, qbt, kb, vbt, sgb = attn_in_proj(xs, norm_g, scale2, shift2, w_in, q_gain, k_gain, cos_t, sin_t)
    bias_tabs = _na_bias_tables(rpb, n // GRID_W)
    ya_lat = na_attention(qa, ka, va, sga, bias_tabs, n)
    ya_ctx = flash_mha(qa, ka, va, sga, q_block0=0, nq=1, tk=CTX_LEN, nk=1)
    tk, nk = _kv_tile(r)
    yb_lat = flash_gqa(qbt, kb, vbt, sgb, q_block0=1, nq=n // ROW_TILE, tk=tk, nk=nk)
    yb_ctx = flash_gqa(qbt, kb, vbt, sgb, q_block0=0, nq=1, tk=CTX_LEN, nk=1)
    wb = _reorder_heads(w_out[BRANCH_W:], GQA_HEAD_ORDER, 0)
    return out_proj(xs, ya_lat, ya_ctx, yb_lat, yb_ctx, w_out[:BRANCH_W], wb, gate2)


def _split(a):
    hi = a.astype(BF16)
    return hi, (a - hi.astype(F32)).astype(BF16)


def _dot3(a, b, dims=(((1,), (0,)), ((), ()))):
    ah, al = _split(a)
    bh, bl = _split(b)
    dg = functools.partial(lax.dot_general, dimension_numbers=dims, preferred_element_type=F32)
    return dg(ah, bh) + (dg(al, bh) + dg(ah, bl))


def _dot1(a, b, dims=(((1,), (0,)), ((), ()))):
    return lax.dot_general(a.astype(BF16), b.astype(BF16), dims, preferred_element_type=F32)


_NT = (((1,), (1,)), ((), ()))
_TN = (((0,), (0,)), ((), ()))


RWKV_SHIFT_W = 1664
HY_IN_W = 1536
HALO = 8
REC_HALO_W = RWKV_SHIFT_W + HY_IN_W


def _rec_in_body(x_ref, xp_ref, xn_ref, g_ref, scale_ref, shift_ref, w_ref, mu_ref, taps_ref,
                 rw_ref, hv_ref, hx1_ref, hx2_ref, sgr_ref, sgh_ref, u_scr):
    i = pl.program_id(0)
    nt = pl.num_programs(0)
    xe = jnp.concatenate([xp_ref[...], x_ref[...], xn_ref[...]], axis=0)
    y = xe * lax.rsqrt(jnp.mean(xe * xe, axis=-1, keepdims=True) + NORM_EPS)
    xm = ((y * g_ref[...]) * (1.0 + scale_ref[0]) + shift_ref[0]).astype(BF16)
    u = jnp.dot(xm, w_ref[...], preferred_element_type=F32)
    row = lax.broadcasted_iota(jnp.int32, (ROW_TILE + 2 * HALO, 1), 0)
    keep = jnp.logical_and(jnp.logical_or(row >= HALO, i >= 2),
                           jnp.logical_or(row < ROW_TILE + HALO, jnp.logical_and(i >= 1, i < nt - 1)))
    u_scr[...] = jnp.where(keep, u[:, :REC_HALO_W], 0.0)
    up = u_scr[pl.ds(HALO - 1, ROW_TILE), :]
    uc = u_scr[pl.ds(HALO, ROW_TILE), :]
    un = u_scr[pl.ds(HALO + 1, ROW_TILE), :]
    w = RWKV_SHIFT_W
    rw_c = uc[:, :w]
    rw_ref[...] = rw_c + (0.5 * (up[:, :w] + un[:, :w]) - rw_c) * mu_ref[...]
    hy = up[:, w:] * taps_ref[0:1] + uc[:, w:] * taps_ref[1:2] + un[:, w:] * taps_ref[2:3]
    hv_ref[...] = hy[:, 0:512]
    hx1_ref[...] = hy[:, 512:1024]
    hx2_ref[...] = hy[:, 1024:1536]
    uc_all = u[HALO:HALO + ROW_TILE]
    sgr_ref[...] = _silu(uc_all[:, REC_HALO_W:REC_HALO_W + 512])
    sgh_ref[...] = _silu(uc_all[:, REC_HALO_W + 512:REC_HALO_W + 1024])


def rec_in_proj(xs, norm_g, scale2, shift2, w_in, mu, hy_short):
    r = xs.shape[0]
    d = D_MODEL
    w = RWKV_SHIFT_W
    w_ext = jnp.concatenate([w_in[:, :w], w_in[:, w + 512:w + 512 + HY_IN_W], w_in[:, w:w + 512],
                             w_in[:, w + 512 + HY_IN_W:]], axis=1).astype(BF16)
    nh = r // HALO
    per = ROW_TILE // HALO
    const = lambda shp: pl.BlockSpec(shp, lambda i: (0,) * len(shp))
    rows = lambda wd: pl.BlockSpec((ROW_TILE, wd), lambda i: (i, 0))
    f = lambda wd: jax.ShapeDtypeStruct((r, wd), F32)
    mod = _mod_specs()
    return pl.pallas_call(
        _rec_in_body,
        out_shape=[f(w), f(512), f(512), f(512), f(512), f(512)],
        grid=(r // ROW_TILE,),
        in_specs=[mod[0],
                  pl.BlockSpec((HALO, d), lambda i: (jnp.maximum(i * per - 1, 0), 0)),
                  pl.BlockSpec((HALO, d), lambda i: (jnp.minimum((i + 1) * per, nh - 1), 0)),
                  mod[1], mod[2], mod[3], const((d, w_ext.shape[1])), const((1, w)), const((3, HY_IN_W))],
        out_specs=[rows(w), rows(512), rows(512), rows(512), rows(512), rows(512)],
        scratch_shapes=[pltpu.VMEM((ROW_TILE + 2 * HALO, REC_HALO_W), F32)],
        compiler_params=_cparams(("parallel",)),
        name="rec_in_proj",
    )(xs, xs, xs, norm_g[None], scale2, shift2, w_ext, mu[None], hy_short)


CHUNK = 64
CPT = ROW_TILE // CHUNK


def _block_sum_mat(width, value):
    return jnp.asarray(np.kron(np.eye(width // HEAD_DIM), np.full((HEAD_DIM, HEAD_DIM), value)), F32)


def _rwkv_prep_body(r_ref, k_ref, v_ref, lora_ref, w0_ref, wup_ref, a0_ref, aup_ref, kk_ref, ka_ref, rk_ref,
                    tri_ref, bs_ref, g_ref, add_ref, bonus_ref):
    t = ROW_TILE
    r, k, v, lora = r_ref[...], k_ref[...], v_ref[...], lora_ref[...]
    bs = bs_ref[...]
    kk = k * kk_ref[...]
    kk = kk * lax.rsqrt(_dot3(kk * kk, bs) + 1e-12)
    tanh_lora = jnp.tanh(lora)
    row = lax.broadcasted_iota(jnp.int32, (t, t), 0)
    col = lax.broadcasted_iota(jnp.int32, (t, t), 1)
    same = ((row // CHUNK) == (col // CHUNK)).astype(F32)
    eye = (row == col).astype(F32)
    half = _lane_half((t, 128))
    half_c = _lane_half((HEAD_DIM, 128))
    rowc = lax.broadcasted_iota(jnp.int32, (HEAD_DIM, 128), 0)
    lanec = lax.broadcasted_iota(jnp.int32, (HEAD_DIM, 128), 1)
    level_masks = []
    bsz = 2
    while bsz < CHUNK:
        level_masks.append(jnp.logical_and((row // (2 * bsz)) == (col // (2 * bsz)), (row // bsz) != (col // bsz)))
        bsz *= 2
    first_mask = (row // 2) == (col // 2)

    dirs = []
    for d in range(2):
        tri = tri_ref[d]
        wl = w0_ref[d] + _dot3(tanh_lora, wup_ref[d])
        z = -wl
        w_log = -(jnp.maximum(z, 0.0) + jnp.log(1.0 + jnp.exp(-jnp.abs(z)))) - 0.5
        lw = -jnp.exp(w_log)
        a = 1.0 / (1.0 + jnp.exp(-(a0_ref[d] + _dot3(lora, aup_ref[d]))))
        kd = k * (1.0 + (a - 1.0) * ka_ref[...])
        b = kk * a
        incl = tri > 0.5
        cs = _dot3(tri, lw)
        tot = _dot3(same, lw)
        w_inv = jnp.exp(-cs)
        w_rest = jnp.exp(tot - cs)
        dirs.append(dict(kd=kd, incl=incl, strict=jnp.logical_and(incl, row != col), tot=tot,
                         kkt=kk * jnp.exp(cs - lw), kh=kd * w_inv, bh=b * w_inv, rt=r * jnp.exp(cs),
                         kdd=kd * w_rest, bdd=b * w_rest))
    chains = [(dd, j) for dd in dirs for j in range(2)]
    sels = [half == j for _, j in chains]
    bms = [jnp.where(sel, dd["bh"], 0.0) for (dd, _), sel in zip(chains, sels)]
    kms = [jnp.where(sel, dd["kh"], 0.0) for (dd, _), sel in zip(chains, sels)]
    l_bs = [jnp.where(dd["strict"], _dot1(dd["kkt"], bm, _NT), 0.0) for (dd, _), bm in zip(chains, bms)]
    tinvs = [eye - jnp.where(first_mask, l_b, 0.0) for l_b in l_bs]
    for mask in level_masks:
        xs = [_dot1(jnp.where(mask, l_b, 0.0), tinv) for l_b, tinv in zip(l_bs, tinvs)]
        tinvs = [tinv - _dot1(tinv, x) for tinv, x in zip(tinvs, xs)]
    l_ks = [jnp.where(dd["strict"], _dot1(dd["kkt"], km, _NT), 0.0) for (dd, _), km in zip(chains, kms)]
    a_rks = [jnp.where(dd["incl"], _dot1(dd["rt"], km, _NT), 0.0) for (dd, _), km in zip(chains, kms)]
    a_rbs = [jnp.where(dd["incl"], _dot1(dd["rt"], bm, _NT), 0.0) for (dd, _), bm in zip(chains, bms)]
    lvs = [_dot1(l_k, v) for l_k in l_ks]
    pus = [_dot1(tinv, jnp.concatenate([dd["kkt"], lv], axis=1)) for (dd, _), tinv, lv in zip(chains, tinvs, lvs)]
    cors = [_dot1(a_rb, pu) for a_rb, pu in zip(a_rbs, pus)]
    ps = [pu[:, :128] for pu in pus]
    u0s = [pu[:, 128:] for pu in pus]
    qs = [dd["rt"] - cor[:, :128] for (dd, _), cor in zip(chains, cors)]
    y0s = [_dot1(a_rk, v) - cor[:, 128:] for a_rk, cor in zip(a_rks, cors)]

    sel0 = half == 0
    for d, dd in enumerate(dirs):
        p, u0, q, y0 = (jnp.where(sel0, x[2 * d], x[2 * d + 1]) for x in (ps, u0s, qs, y0s))
        for c in range(CPT):
            rs = slice(c * CHUNK, (c + 1) * CHUNK)
            x1 = _dot1(dd["bdd"][rs], p[rs], _TN)
            x2 = _dot1(dd["kdd"][rs], v[rs], _TN) - _dot1(dd["bdd"][rs], u0[rs], _TN)
            m_pair = jnp.where(half_c == 0, x1[:HEAD_DIM], x1[HEAD_DIM:])
            n_pair = jnp.where(half_c == 0, x2[:HEAD_DIM], x2[HEAD_DIM:])
            wc = jnp.exp(dd["tot"][c * CHUNK:c * CHUNK + 1])
            dg = jnp.where((lanec % HEAD_DIM) == rowc, wc, 0.0)
            g_ref[c, d, 0:HEAD_DIM, :] = dg - m_pair
            g_ref[c, d, HEAD_DIM:, :] = q[rs]
            add_ref[c, d, 0:HEAD_DIM, :] = n_pair
            add_ref[c, d, HEAD_DIM:, :] = y0[rs]

    kd_sum = dirs[0]["kd"] + dirs[1]["kd"]
    bonus_ref[...] = 0.5 * _dot3(r * kd_sum * rk_ref[...], bs) * v


def _lora_ext(up, first_row):
    out = jnp.zeros((2, 128, BRANCH_W), F32)
    for d in range(2):
        out = out.at[d, first_row + 32 * d:first_row + 32 * (d + 1)].set(up[d])
    return out


def rwkv_prep(rw, w0, w_up, a0, a_up, k_k, k_a, r_k):
    r = rw.shape[0]
    nt = r // ROW_TILE
    nch = r // CHUNK
    t = ROW_TILE
    ii = np.arange(t)
    same = (ii[:, None] // CHUNK) == (ii[None, :] // CHUNK)
    tri = jnp.asarray(np.stack([same & (ii[None, :] <= ii[:, None]), same & (ii[None, :] >= ii[:, None])]), F32)
    lane = lambda blk: pl.BlockSpec((t, 128), lambda i, p, blk=blk: (i, blk + p))
    pvec = pl.BlockSpec((1, 128), lambda i, p: (0, p))
    dvec = pl.BlockSpec((2, 1, 128), lambda i, p: (0, 0, p))
    dmat = pl.BlockSpec((2, 128, 128), lambda i, p: (0, 0, p))
    gspec = pl.BlockSpec((CPT, 2, HEAD_DIM + CHUNK, 128), lambda i, p: (i, 0, 0, p))
    gshape = jax.ShapeDtypeStruct((nch, 2, HEAD_DIM + CHUNK, BRANCH_W), F32)
    return pl.pallas_call(
        _rwkv_prep_body,
        out_shape=[gshape, gshape, jax.ShapeDtypeStruct((r, BRANCH_W), F32)],
        grid=(nt, N_HEADS // 2),
        in_specs=[lane(0), lane(4), lane(8), pl.BlockSpec((t, 128), lambda i, p: (i, 12)),
                  dvec, dmat, dvec, dmat, pvec, pvec, pvec,
                  pl.BlockSpec((2, t, t), lambda i, p: (0, 0, 0)),
                  pl.BlockSpec((128, 128), lambda i, p: (0, 0))],
        out_specs=[gspec, gspec, pl.BlockSpec((t, 128), lambda i, p: (i, p))],
        compiler_params=_cparams(("parallel", "parallel")),
        name="rwkv_prep",
    )(rw, rw, rw, rw, w0.reshape(2, 1, BRANCH_W), _lora_ext(w_up, 0), a0.reshape(2, 1, BRANCH_W),
      _lora_ext(a_up, 64), k_k[None], k_a[None], r_k.reshape(1, BRANCH_W), tri, _block_sum_mat(128, 1.0))


def _rwkv_scan_body(gf_ref, af_ref, gb_ref, ab_ref, yf_ref, yb_ref, st_ref):
    @pl.when(pl.program_id(0) == 0)
    def _():
        st_ref[...] = jnp.zeros(st_ref.shape, F32)

    rowh = lax.broadcasted_iota(jnp.int32, (128, 128), 0) // HEAD_DIM
    diag = rowh == _lane_half((128, 128))
    for s in range(CPT):
        for d, (g_ref, a_ref, y_ref, c) in enumerate(((gf_ref, af_ref, yf_ref, s), (gb_ref, ab_ref, yb_ref, CPT - 1 - s))):
            for p in range(N_HEADS // 2):
                ls = slice(p * 128, (p + 1) * 128)
                out = _dot3(g_ref[c, 0, :, ls], st_ref[d, p]) + a_ref[c, 0, :, ls]
                hn = out[:HEAD_DIM]
                st_ref[d, p] = jnp.where(diag, jnp.concatenate([hn, hn], axis=0), 0.0)
                y_ref[c * CHUNK:(c + 1) * CHUNK, ls] = out[HEAD_DIM:]


def rwkv_scan(g, add):
    nch = g.shape[0]
    r = nch * CHUNK
    nt = r // ROW_TILE
    assert CTX_LEN == ROW_TILE
    rev = lambda i: jnp.where(i == 0, 0, nt - i)
    blk = (CPT, 1, HEAD_DIM + CHUNK, BRANCH_W)
    fwd = pl.BlockSpec(blk, lambda i: (i, 0, 0, 0))
    bwd = pl.BlockSpec(blk, lambda i: (rev(i), 1, 0, 0))
    yshape = jax.ShapeDtypeStruct((r, BRANCH_W), F32)
    return pl.pallas_call(
        _rwkv_scan_body,
        out_shape=[yshape, yshape],
        grid=(nt,),
        in_specs=[fwd, fwd, bwd, bwd],
        out_specs=[pl.BlockSpec((ROW_TILE, BRANCH_W), lambda i: (i, 0)),
                   pl.BlockSpec((ROW_TILE, BRANCH_W), lambda i: (rev(i), 0))],
        scratch_shapes=[pltpu.VMEM((2, N_HEADS // 2, 128, 128), F32)],
        compiler_params=_cparams(("arbitrary",)),
        name="rwkv_scan",
    )(g, add, g, add)


HY_WIDTH = 512
HY_ORDER = 2
HY_POS_BANDS = 16
HY_HIDDEN = 64
HY_TAPS_W = 2 * HY_ORDER * HY_WIDTH
FFT_N2 = ROW_TILE


def _dot3c(ah, al, b):
    bh, bl = _split(b)
    dg = functools.partial(jnp.dot, preferred_element_type=F32)
    return dg(ah, bh) + (dg(al, bh) + dg(ah, bl))


def _dotc(ah, b):
    return jnp.dot(ah, b.astype(BF16), preferred_element_type=F32)


def _split_const(m):
    m = np.asarray(m, np.float32)
    hi = m.astype(BF16)
    lo = (m - hi.astype(np.float32)).astype(BF16)
    return jnp.asarray(hi), jnp.asarray(lo)


TAPS_FLAT_COLS = 8


def _filter_taps(t_idx, length, c2pb_ref, w1t_ref, w1c_ref, w1s_ref, b1_ref, w2_ref, b2_ref, w3_ref, b3_ref, absd_ref):
    t = t_idx / float(max(length - 1, 1))
    ang = c2pb_ref[...] * t_idx / float(length)
    pre = t * w1t_ref[...] + _dot3(jnp.cos(ang), w1c_ref[...]) - _dot3(jnp.sin(ang), w1s_ref[...]) + b1_ref[...]
    hid = jnp.sin(pre)
    hid = jnp.sin(_dot3(hid, w2_ref[...]) + b2_ref[...])
    return (_dot3(hid, w3_ref[...]) + b3_ref[...]) * jnp.exp(-t * absd_ref[...])


def _hy_taps_body(*refs, length):
    taps_ref, ssq_ref, tap0_ref = refs[-3:]
    i = pl.program_id(0)
    t_idx = (i * ROW_TILE + lax.broadcasted_iota(jnp.int32, (ROW_TILE, 1), 0)).astype(F32)
    taps = _filter_taps(t_idx, length, *refs[:-3])
    taps_ref[...] = taps

    @pl.when(i == 0)
    def _():
        ssq_ref[...] = jnp.zeros(ssq_ref.shape, F32)
        tap0_ref[...] = taps[0:1]

    ssq_ref[...] += jnp.sum(taps * taps, axis=0, keepdims=True)


def _hy_taps_flat_body(c2pb_ref, w1t_ref, w1c_ref, w1s_ref, b1_ref, w2_ref, b2_ref, w3_ref, b3_ref, absd_ref,
                       taps_ref, ssq_ref, tap0_ref, *, length):
    j = pl.program_id(0)
    rows = length // FFT_N2
    nb = TAPS_FLAT_COLS

    def positions(width, per):
        a = lax.broadcasted_iota(jnp.int32, (rows, width), 0)
        b = lax.broadcasted_iota(jnp.int32, (rows, width), 1) // per
        return (a * FFT_N2 + j * nb + b).astype(F32)

    @pl.when(j == 0)
    def _():
        ssq_ref[...] = jnp.zeros(ssq_ref.shape, F32)

    h = HY_HIDDEN
    ang = c2pb_ref[...] * positions(nb * HY_POS_BANDS, HY_POS_BANDS) / float(length)
    t_h = positions(nb * h, h) / float(max(length - 1, 1))
    pre = t_h * w1t_ref[...] + _dot3(jnp.cos(ang), w1c_ref[...]) - _dot3(jnp.sin(ang), w1s_ref[...]) + b1_ref[...]
    hid = jnp.sin(pre)
    hid = jnp.sin(_dot3(hid, w2_ref[...]) + b2_ref[...])
    for bp in range(nb // 2):
        t_w = (positions(2 * HY_TAPS_W, HY_TAPS_W) + float(2 * bp)) / float(max(length - 1, 1))
        taps = (_dot3(hid[:, bp * 2 * h:(bp + 1) * 2 * h], w3_ref[...]) + b3_ref[...]) * jnp.exp(-t_w * absd_ref[...])
        taps_ref[:, bp * 2 * HY_TAPS_W:(bp + 1) * 2 * HY_TAPS_W] = taps.astype(taps_ref.dtype)
        if bp == 0:
            @pl.when(j == 0)
            def _():
                tap0_ref[...] = taps[0:1, :HY_TAPS_W]
        sq = jnp.sum(taps * taps, axis=0, keepdims=True)
        ssq_ref[...] += sq[:, :HY_TAPS_W] + sq[:, HY_TAPS_W:]


def hyena_taps(length, w1, b1, w2, b2, w3, b3, *, flat):
    bands = jnp.linspace(1e-4, HY_POS_BANDS - 1, HY_POS_BANDS, dtype=F32)
    c2pb = jnp.zeros((1, 128), F32).at[0, :HY_POS_BANDS].set(2.0 * math.pi * bands)
    pad = lambda m: jnp.zeros((128, HY_HIDDEN), F32).at[:HY_POS_BANDS].set(m)
    deltas = jnp.linspace(math.log(1e-2) / 0.3, math.log(1e-2) / 1.5, HY_WIDTH, dtype=F32)
    absd = jnp.tile(jnp.abs(deltas), 2 * HY_ORDER)[None]
    const = lambda shp: pl.BlockSpec(shp, lambda i: (0,) * len(shp))
    w1t, w1c, w1s = w1[0:1], w1[1:1 + HY_POS_BANDS], w1[1 + HY_POS_BANDS:]
    if flat:
        rows, nb = length // FFT_N2, TAPS_FLAT_COLS
        bd = lambda m, k: jnp.kron(jnp.eye(k, dtype=F32), m)
        tile = lambda v, k: jnp.tile(v, k)[None]
        operands = (tile(c2pb[0, :HY_POS_BANDS], nb), tile(w1t[0], nb), bd(w1c, nb), bd(w1s, nb), tile(b1, nb),
                    bd(w2, nb), tile(b2, nb), bd(w3, 2), tile(b3, 2), tile(absd[0], 2))
        body, grid = _hy_taps_flat_body, (FFT_N2 // nb,)
        taps_shape, taps_dtype = (rows, FFT_N2 * HY_TAPS_W), BF16
        taps_spec = pl.BlockSpec((rows, nb * HY_TAPS_W), lambda i: (0, i))
    else:
        operands = (c2pb, w1t, pad(w1c), pad(w1s), b1[None], w2, b2[None], w3, b3[None], absd)
        body, grid = _hy_taps_body, (length // ROW_TILE,)
        taps_shape, taps_dtype = (length, HY_TAPS_W), F32
        taps_spec = pl.BlockSpec((ROW_TILE, HY_TAPS_W), lambda i: (i, 0))
    taps, ssq, tap0 = pl.pallas_call(
        functools.partial(body, length=length),
        out_shape=[jax.ShapeDtypeStruct(taps_shape, taps_dtype), jax.ShapeDtypeStruct((1, HY_TAPS_W), F32),
                   jax.ShapeDtypeStruct((1, HY_TAPS_W), F32)],
        grid=grid,
        in_specs=[const(op.shape) for op in operands],
        out_specs=[taps_spec, const((1, HY_TAPS_W)), const((1, HY_TAPS_W))],
        compiler_params=_cparams(("arbitrary",)),
        name="hyena_taps_flat" if flat else "hyena_taps",
    )(*operands)
    hw = HY_TAPS_W // 2
    norm2 = ssq[:, :hw] + ssq[:, hw:] + 2.0 * tap0[:, :hw] * tap0[:, hw:]
    return taps, lax.rsqrt(norm2)


class _FftPlan:
    def __init__(self, length):
        self.length = length
        self.n = 2 * length
        self.n2 = FFT_N2
        self.n1 = self.n // self.n2
        self.n1h = self.n1 // 2
        k1 = self.n1h + 1
        self.k1p = -(-k1 // 8) * 8
        kk = np.arange(self.k1p)[:, None].astype(np.float64)
        live = (kk < k1)
        nn = np.arange(self.n1h)[None, :].astype(np.float64)
        th = 2.0 * np.pi * kk * nn / self.n1
        self.f1 = _split_const(np.concatenate([np.cos(th) * live, -np.sin(th) * live], axis=0))
        ck = np.where((kk == 0) | (kk == self.n1h), 1.0, 2.0) * live / self.n
        self.g1 = _split_const(np.concatenate([np.cos(th) * ck, -np.sin(th) * ck], axis=0).T)
        m = np.arange(self.n2).astype(np.float64)
        ph = 2.0 * np.pi * np.outer(m, m) / self.n2
        c, s = np.cos(ph), np.sin(ph)
        self.fb = _split_const(np.block([[c, s], [-s, c]]))
        self.fbi = _split_const(np.block([[c, -s], [s, c]]))
        tw = 2.0 * np.pi * kk[:, :, None] * m[None, :, None] / self.n
        self.twc = jnp.asarray(np.cos(tw), F32)
        self.tws = jnp.asarray(np.sin(tw), F32)


FFT_TN = 4096


def _fft_a_body(fh_ref, x_ref, o_ref):
    o_ref[...] = _dotc(fh_ref[...], x_ref[...]).astype(o_ref.dtype)


def fft_stage_a(plan, xf, lead):
    rows_in, m = xf.shape
    rows = 2 * plan.k1p
    tn = min(FFT_TN, m)
    fh = jnp.pad(plan.f1[0], ((0, 0), (lead, 0)))
    fspec = pl.BlockSpec((rows, rows_in), lambda j: (0, 0))
    return pl.pallas_call(
        _fft_a_body,
        out_shape=jax.ShapeDtypeStruct((rows, m), BF16),
        grid=(m // tn,),
        in_specs=[fspec, pl.BlockSpec((rows_in, tn), lambda j: (0, j))],
        out_specs=pl.BlockSpec((rows, tn), lambda j: (0, j)),
        compiler_params=_cparams(("parallel",)),
        name="fft_stage_a",
    )(fh, xf)


def _twiddled(a_ref, twc_ref, tws_ref):
    are, aim = a_ref[0, 0].astype(F32), a_ref[1, 0].astype(F32)
    c, s = twc_ref[0], tws_ref[0]
    return jnp.concatenate([are * c + aim * s, aim * c - are * s], axis=0)


def _fft_filter_b_body(a_ref, twc_ref, tws_ref, fbh_ref, scale_ref, o_ref):
    n2 = FFT_N2
    x = _dotc(fbh_ref[...], _twiddled(a_ref, twc_ref, tws_ref))
    hw = HY_TAPS_W // 2
    xre, xim = x[:n2], x[n2:]
    o_ref[0, 0] = ((xre[:, :hw] + xre[:, hw:]) * scale_ref[...]).astype(o_ref.dtype)
    o_ref[1, 0] = ((xim[:, :hw] - xim[:, hw:]) * scale_ref[...]).astype(o_ref.dtype)


def fft_filter_stage_b(plan, a, scale):
    n2, k1p = plan.n2, plan.k1p
    hw = HY_TAPS_W // 2
    const = lambda shp: pl.BlockSpec(shp, lambda k: (0,) * len(shp))
    return pl.pallas_call(
        _fft_filter_b_body,
        out_shape=jax.ShapeDtypeStruct((2, k1p, n2, hw), BF16),
        grid=(k1p,),
        in_specs=[pl.BlockSpec((2, 1, n2, HY_TAPS_W), lambda k: (0, k, 0, 0)),
                  pl.BlockSpec((1, n2, 1), lambda k: (k, 0, 0)), pl.BlockSpec((1, n2, 1), lambda k: (k, 0, 0)),
                  const((2 * n2, 2 * n2)), const((1, hw))],
        out_specs=pl.BlockSpec((2, 1, n2, hw), lambda k: (0, k, 0, 0)),
        compiler_params=_cparams(("parallel",)),
        name="fft_filter_stage_b",
    )(a, plan.twc, plan.tws, plan.fb[0], scale)


def _fft_conv_b_body(a_ref, kf_ref, twc_ref, tws_ref, fbh_ref, fih_ref, o_ref):
    n2 = FFT_N2
    z = _dotc(fbh_ref[...], _twiddled(a_ref, twc_ref, tws_ref))
    zre, zim = z[:n2], z[n2:]
    kre, kim = kf_ref[0, 0].astype(F32), kf_ref[1, 0].astype(F32)
    y = jnp.concatenate([zre * kre - zim * kim, zre * kim + zim * kre], axis=0)
    q = _dotc(fih_ref[...], y)
    qre, qim = q[:n2], q[n2:]
    c, s = twc_ref[0], tws_ref[0]
    o_ref[0, 0] = (qre * c - qim * s).astype(o_ref.dtype)
    o_ref[1, 0] = (qim * c + qre * s).astype(o_ref.dtype)


def fft_conv_stage_b(plan, a, kf, order):
    n2, k1p = plan.n2, plan.k1p
    const = lambda shp: pl.BlockSpec(shp, lambda k: (0,) * len(shp))
    return pl.pallas_call(
        _fft_conv_b_body,
        out_shape=jax.ShapeDtypeStruct((2, k1p, n2, HY_WIDTH), BF16),
        grid=(k1p,),
        in_specs=[pl.BlockSpec((2, 1, n2, HY_WIDTH), lambda k: (0, k, 0, 0)),
                  pl.BlockSpec((2, 1, n2, HY_WIDTH), lambda k: (0, k, 0, order)),
                  pl.BlockSpec((1, n2, 1), lambda k: (k, 0, 0)), pl.BlockSpec((1, n2, 1), lambda k: (k, 0, 0)),
                  const((2 * n2, 2 * n2)), const((2 * n2, 2 * n2))],
        out_specs=pl.BlockSpec((2, 1, n2, HY_WIDTH), lambda k: (0, k, 0, 0)),
        compiler_params=_cparams(("parallel",)),
        name="fft_conv_stage_b",
    )(a, kf, plan.twc, plan.tws, plan.fb[0], plan.fbi[0])


def _fft_inv_a_body(gh_ref, q_ref, z_ref, gate_ref, skip_ref, o_ref):
    y = _dotc(gh_ref[...], q_ref[...])
    o_ref[...] = gate_ref[...] * (y + z_ref[...] * skip_ref[...])


def fft_inv_stage_a(plan, qf, zf, gatef, skip_t, lead):
    rows_out, m = zf.shape
    rows = 2 * plan.k1p
    tn = skip_t.shape[1]
    gh = jnp.pad(plan.g1[0], ((lead, 0), (0, 0)))
    col = lambda r_: pl.BlockSpec((r_, tn), lambda j: (0, j))
    gspec = pl.BlockSpec((rows_out, rows), lambda j: (0, 0))
    return pl.pallas_call(
        _fft_inv_a_body,
        out_shape=jax.ShapeDtypeStruct((rows_out, m), F32),
        grid=(m // tn,),
        in_specs=[gspec, col(rows), col(rows_out), col(rows_out), pl.BlockSpec((1, tn), lambda j: (0, 0))],
        out_specs=col(rows_out),
        compiler_params=_cparams(("parallel",)),
        name="fft_inv_stage_a",
    )(gh, qf, zf, gatef, skip_t)


def hyena_long(hv, hx1, hx2, taps, scale, skip, lead):
    length = taps.shape[0] * FFT_N2
    plan = _FftPlan(length)
    n2, k1p = plan.n2, plan.k1p
    m = n2 * HY_WIDTH
    tn = min(FFT_TN, m)
    flat = lambda a: a.reshape(lead + plan.n1h, m)
    ta = fft_stage_a(plan, taps, 0)
    kf = fft_filter_stage_b(plan, ta.reshape(2, k1p, n2, HY_TAPS_W), scale)
    z = flat(hv)
    for o, gate in enumerate((hx1, hx2)):
        a = fft_stage_a(plan, z, lead)
        q = fft_conv_stage_b(plan, a.reshape(2, k1p, n2, HY_WIDTH), kf, o)
        z = fft_inv_stage_a(plan, q.reshape(2 * k1p, m), z, flat(gate), jnp.tile(skip[o], tn // HY_WIDTH)[None], lead)
    return z.reshape(-1, HY_WIDTH)


def _rec_out_body(x_ref, yf_ref, yb_ref, bonus_ref, sgr_ref, zh_ref, zc_ref, sgh_ref, gnw_ref, gnb_ref, bm_ref, wa_ref, wb_ref,
                  gate_ref, fin_ref, o_ref, *, final):
    y = yf_ref[...] + yb_ref[...]
    bm = bm_ref[...]
    mean = _dot_stat(y, bm)
    yc = y - mean
    var = _dot_stat(yc * yc, bm)
    yn = yc * lax.rsqrt(var + RWKV_GN_EPS) * gnw_ref[...] + gnb_ref[...]
    ya = ((yn + bonus_ref[...]) * sgr_ref[...]).astype(BF16)
    zh = zh_ref[...]
    if not final:
        zh = jnp.where(pl.program_id(0) == 0, zc_ref[...], zh)
    yh = (zh * sgh_ref[...]).astype(BF16)
    out = jnp.dot(ya, wa_ref[...], preferred_element_type=F32) + jnp.dot(yh, wb_ref[...], preferred_element_type=F32)
    xn = x_ref[...] + gate_ref[0] * out
    if final:
        xn = xn * lax.rsqrt(jnp.mean(xn * xn, axis=-1, keepdims=True) + NORM_EPS) * fin_ref[...]
    o_ref[...] = xn


RWKV_GN_EPS = 64e-5


def rec_out_proj(xs, yf, yb, bonus, sgr, zh, zc, sgh, gn_w, gn_b, w_out, gate2, final_g, *, final):
    r = xs.shape[0]
    d = D_MODEL
    off = 1 if final else 0
    nt = r // ROW_TILE - off
    rows = lambda w: pl.BlockSpec((ROW_TILE, w), lambda i: (i + off, 0))
    const = lambda shp: pl.BlockSpec(shp, lambda i: (0,) * len(shp))
    bw = BRANCH_W
    return pl.pallas_call(
        functools.partial(_rec_out_body, final=final),
        out_shape=jax.ShapeDtypeStruct((nt * ROW_TILE, d), F32),
        grid=(nt,),
        in_specs=[rows(d), rows(bw), rows(bw), rows(bw), rows(bw), rows(bw), const((CTX_LEN, bw)), rows(bw),
                  const((1, bw)), const((1, bw)),
                  const((bw, bw)), const((bw, d)), const((bw, d)),
                  pl.BlockSpec((1, 1, d), lambda i: (jnp.minimum(i + off, 1), 0, 0)), const((1, d))],
        out_specs=pl.BlockSpec((ROW_TILE, d), lambda i: (i, 0)),
        compiler_params=_cparams(("parallel",)),
        name="rec_out_proj",
    )(xs, yf, yb, bonus, sgr, zh, zc, sgh, gn_w[None], gn_b[None], _block_sum_mat(bw, 1.0 / HEAD_DIM).astype(BF16),
      w_out[:bw].astype(BF16), w_out[bw:].astype(BF16), gate2, final_g[None])


def rec_layer(xs, mods, norm_g, w_in, mu, w0, w_up, a0, a_up, k_k, k_a, r_k, gn_w, gn_b, hy_short, hy_w1, hy_b1, hy_w2,
              hy_b2, hy_w3, hy_b3, hy_skip, w_out, final_g, final):
    shift2, scale2, gate2 = mods
    rw, hv, hx1, hx2, sgr, sgh = rec_in_proj(xs, norm_g, scale2, shift2, w_in, mu, hy_short)
    g, add, bonus = rwkv_prep(rw, w0, w_up, a0, a_up, k_k, k_a, r_k)
    yf, yb = rwkv_scan(g, add)
    fargs = (hy_w1, hy_b1, hy_w2, hy_b2, hy_w3, hy_b3)
    n = xs.shape[0] - CTX_LEN
    taps, scale = hyena_taps(n, *fargs, flat=True)
    zh = hyena_long(hv, hx1, hx2, taps, scale, hy_skip, CTX_LEN // FFT_N2)
    if final:
        z_ctx = hv[:CTX_LEN]
    else:
        taps_c, scale_c = hyena_taps(CTX_LEN, *fargs, flat=False)
        z_ctx = hyena_short(hv[:CTX_LEN], hx1[:CTX_LEN], hx2[:CTX_LEN], taps_c, scale_c, hy_skip)
    return rec_out_proj(xs, yf, yb, bonus, sgr, zh, z_ctx, sgh, gn_w, gn_b, w_out, gate2, final_g, final=final)


def kernel(x, c, ctx, c_ctx, attn_norm, attn_ada_w, attn_ada_b, attn_w_in, na_rpb, gqa_q_gain, gqa_k_gain, attn_w_out,
           rec_norm, rec_ada_w, rec_ada_b, rec_w_in, rwkv_mu, rwkv_w0, rwkv_w_up, rwkv_a0, rwkv_a_up, rwkv_k_k, rwkv_k_a,
           rwkv_r_k, rwkv_gn_w, rwkv_gn_b, hy_short, hy_w1, hy_b1, hy_w2, hy_b2, hy_w3, hy_b3, hy_skip, rec_w_out,
           final_norm):
    assert x.shape[0] == 1 and ctx.shape[1] == CTX_LEN and x.shape[2] == D_MODEL
    n = x.shape[1]
    assert n % ROW_TILE == 0 and n // ROW_TILE >= 3
    assert attn_w_in.shape[0] == rec_w_in.shape[0]
    d = D_MODEL
    cond8 = jnp.zeros((8, d), F32).at[0].set(c_ctx).at[1].set(c[0])
    m_attn = adaln_all(cond8, attn_ada_w, attn_ada_b)
    m_rec = adaln_all(cond8, rec_ada_w, rec_ada_b)
    mods = lambda m, i: tuple(m[i, :2, j * d:(j + 1) * d].reshape(2, 1, d) for j in range(3))
    cos_t, sin_t = _rope_tables(n)
    xs = jnp.concatenate([ctx[0], x[0]], axis=0)
    depth = attn_w_in.shape[0] + rec_w_in.shape[0]
    for layer in range(depth):
        i = layer // 2
        final = layer == depth - 1
        if layer % 2 == 0:
            xs = attn_layer(xs, mods(m_attn, i), attn_norm[i], attn_w_in[i], na_rpb[i], gqa_q_gain[i], gqa_k_gain[i],
                            attn_w_out[i], cos_t, sin_t)
        else:
            xs = rec_layer(xs, mods(m_rec, i), rec_norm[i], rec_w_in[i], rwkv_mu[i], rwkv_w0[i], rwkv_w_up[i],
                           rwkv_a0[i], rwkv_a_up[i], rwkv_k_k[i], rwkv_k_a[i], rwkv_r_k[i], rwkv_gn_w[i], rwkv_gn_b[i],
                           hy_short[i], hy_w1[i], hy_b1[i], hy_w2[i], hy_b2[i], hy_w3[i], hy_b3[i], hy_skip[i],
                           rec_w_out[i], final_norm, final)
    return xs[None]


def _hy_short_body(fh_ref, fl_ref, gh_ref, gl_ref, v_ref, x1_ref, x2_ref, taps_ref, scale_ref, skip_ref, o_ref):
    fh, fl, gh, gl = fh_ref[...], fl_ref[...], gh_ref[...], gl_ref[...]
    kp = fh.shape[0] // 2
    hw = HY_TAPS_W // 2
    tf = _dot3c(fh, fl, taps_ref[...])
    kre = (tf[:kp, :hw] + tf[:kp, hw:]) * scale_ref[...]
    kim = (tf[kp:, :hw] - tf[kp:, hw:]) * scale_ref[...]
    z = v_ref[...]
    for o, gate_ref in enumerate((x1_ref, x2_ref)):
        ls = slice(o * HY_WIDTH, (o + 1) * HY_WIDTH)
        zf = _dot3c(fh, fl, z)
        zre, zim = zf[:kp], zf[kp:]
        y = jnp.concatenate([zre * kre[:, ls] - zim * kim[:, ls], zre * kim[:, ls] + zim * kre[:, ls]], axis=0)
        z = gate_ref[...] * (_dot3c(gh, gl, y) + z * skip_ref[o:o + 1])
    o_ref[...] = z


def hyena_short(hv, hx1, hx2, taps, scale, skip):
    length = hv.shape[0]
    n = 2 * length
    k1 = length + 1
    kp = -(-k1 // 8) * 8
    kk = np.arange(kp)[:, None].astype(np.float64)
    live = kk < k1
    th = 2.0 * np.pi * kk * np.arange(length)[None, :] / n
    f = _split_const(np.concatenate([np.cos(th) * live, -np.sin(th) * live], axis=0))
    ck = np.where((kk == 0) | (kk == length), 1.0, 2.0) * live / n
    g = _split_const(np.concatenate([np.cos(th) * ck, -np.sin(th) * ck], axis=0).T)
    return pl.pallas_call(
        _hy_short_body,
        out_shape=jax.ShapeDtypeStruct((length, HY_WIDTH), F32),
        compiler_params=pltpu.CompilerParams(vmem_limit_bytes=VMEM_LIMIT),
        name="hyena_short",
    )(f[0], f[1], g[0], g[1], hv, hx1, hx2, taps, scale, skip)
```

```python
import functools
import math

import jax
import jax.numpy as jnp
import numpy as np
from jax import lax
from jax.experimental import pallas as pl
from jax.experimental.pallas import tpu as pltpu

F32 = jnp.float32
BF16 = jnp.bfloat16
HIGHEST = lax.Precision.HIGHEST

D_MODEL = 1024
GRID_W = 64
CTX_LEN = 256
HEAD_DIM = 64
BRANCH_W = 512
N_HEADS = 8
GQA_KV_W = 128
NA_WIN_ROWS = 8
NA_WIN_COLS = 16
ROPE_THETA = 10000.0
ROPE_FREQS = 16
NORM_EPS = 1e-6
ROW_TILE = 256
NA_GROUP_ROWS = 4
NEG_BIG = -1e30
LOG2E = math.log2(math.e)
QK_SCALE = HEAD_DIM ** -0.5 * LOG2E
VMEM_LIMIT = 56 * 1024 * 1024

ATTN_SPLITS = (512, 512, 512, 512, 512, 128, 128, 512)
GQA_HEAD_ORDER = (0, 4, 1, 5, 2, 6, 3, 7)


def _cparams(sem):
    return pltpu.CompilerParams(dimension_semantics=sem, vmem_limit_bytes=VMEM_LIMIT)


def _silu(v):
    return v * (1.0 / (1.0 + jnp.exp(-v)))


def _lane_half(shape):
    return (lax.broadcasted_iota(jnp.int32, shape, len(shape) - 1) // HEAD_DIM) % 2


def _dot_stat(a, block_mat):
    hi = a.astype(BF16)
    lo = (a - hi.astype(F32)).astype(BF16)
    return jnp.dot(hi, block_mat, preferred_element_type=F32) + jnp.dot(lo, block_mat, preferred_element_type=F32)


def _dot_nt(a, b):
    return lax.dot_general(a, b, (((1,), (1,)), ((), ())), preferred_element_type=F32)


def _adaln_body(cond_ref, w_ref, b_ref, o_ref):
    s = _silu(cond_ref[...])
    o_ref[0] = jnp.dot(s, w_ref[0], precision=HIGHEST, preferred_element_type=F32) + b_ref[0]


def adaln_all(cond8, ada_w, ada_b):
    nl = ada_w.shape[0]
    d = D_MODEL
    return pl.pallas_call(
        _adaln_body,
        out_shape=jax.ShapeDtypeStruct((nl, 8, 3 * d), F32),
        grid=(nl, 3),
        in_specs=[
            pl.BlockSpec((8, d), lambda l, j: (0, 0)),
            pl.BlockSpec((1, d, d), lambda l, j: (l, 0, j)),
            pl.BlockSpec((1, 1, d), lambda l, j: (l, 0, j)),
        ],
        out_specs=pl.BlockSpec((1, 8, d), lambda l, j: (l, 0, j)),
        compiler_params=_cparams(("parallel", "parallel")),
        name="adaln",
    )(cond8, ada_w, ada_b.reshape(nl, 1, 3 * d))


def _modulated(x_ref, g_ref, scale_ref, shift_ref):
    xf = x_ref[...]
    y = xf * lax.rsqrt(jnp.mean(xf * xf, axis=-1, keepdims=True) + NORM_EPS)
    return (y * g_ref[...]) * (1.0 + scale_ref[0]) + shift_ref[0]


def _mod_specs():
    d = D_MODEL
    return [
        pl.BlockSpec((ROW_TILE, d), lambda i: (i, 0)),
        pl.BlockSpec((1, d), lambda i: (0, 0)),
        pl.BlockSpec((1, 1, d), lambda i: (jnp.minimum(i, 1), 0, 0)),
        pl.BlockSpec((1, 1, d), lambda i: (jnp.minimum(i, 1), 0, 0)),
    ]


def _attn_in_body(x_ref, g_ref, scale_ref, shift_ref, w_ref, cos_ref, sin_ref, gq_ref, gqs_ref, gk_ref, gks_ref,
                  bdq_ref, bdk_ref,
                  qa_ref, ka_ref, va_ref, sga_ref, qb_ref, kb_ref, vb_ref, sgb_ref):
    xm = _modulated(x_ref, g_ref, scale_ref, shift_ref).astype(BF16)
    u = jnp.dot(xm, w_ref[...], preferred_element_type=F32)
    qa, ka, va, ga = u[:, 0:512], u[:, 512:1024], u[:, 1024:1536], u[:, 1536:2048]
    qb, kb, vb, gb = u[:, 2048:2560], u[:, 2560:2688], u[:, 2688:2816], u[:, 2816:3328]
    qbs, kbs = u[:, 3328:3840], u[:, 3840:3968]
    scale = QK_SCALE
    qa_ref[...] = (qa * scale).astype(BF16)
    ka_ref[...] = ka.astype(BF16)
    va_ref[...] = va.astype(BF16)
    sga_ref[...] = _silu(ga)
    sgb_ref[...] = _silu(gb)
    vb_ref[...] = jnp.transpose(vb).astype(BF16)
    cos_k, sin_k = cos_ref[...], sin_ref[...]
    cos_q = jnp.concatenate([cos_k] * 4, axis=1)
    sin_q = jnp.concatenate([sin_k] * 4, axis=1)
    rs_q = lax.rsqrt(_dot_stat(qb * qb, bdq_ref[...]) + NORM_EPS)
    rs_k = lax.rsqrt(_dot_stat(kb * kb, bdk_ref[...]) + NORM_EPS)
    qr = rs_q * (qb * gq_ref[...] * cos_q + qbs * gqs_ref[...] * sin_q)
    kr = rs_k * (kb * gk_ref[...] * cos_k + kbs * gks_ref[...] * sin_k)
    qb_ref[...] = jnp.transpose(qr * scale).astype(BF16)
    kb_ref[...] = kr.astype(BF16)


def _rope_tables(n):
    t = jnp.arange(n, dtype=jnp.int32)
    pos = jnp.stack([t // GRID_W, t % GRID_W], axis=-1).astype(F32)
    inv_freq = ROPE_THETA ** (-jnp.arange(ROPE_FREQS, dtype=F32) / ROPE_FREQS)
    ang = pos[:, :, None] * inv_freq
    c, s = jnp.cos(ang), jnp.sin(ang)
    cos64 = jnp.concatenate([c[:, 0], c[:, 0], c[:, 1], c[:, 1]], axis=-1)
    sin64 = jnp.concatenate([-s[:, 0], s[:, 0], -s[:, 1], s[:, 1]], axis=-1)
    cos64 = jnp.concatenate([jnp.ones((CTX_LEN, HEAD_DIM), F32), cos64], axis=0)
    sin64 = jnp.concatenate([jnp.zeros((CTX_LEN, HEAD_DIM), F32), sin64], axis=0)
    return jnp.tile(cos64, (1, 2)), jnp.tile(sin64, (1, 2))


def _reorder_heads(w, order, axis):
    take = lambda h: lax.slice_in_dim(w, h * HEAD_DIM, (h + 1) * HEAD_DIM, axis=axis)
    return jnp.concatenate([take(h) for h in order], axis=axis)


def _swap_rope_halves(w):
    shp = w.shape
    return jnp.flip(w.reshape(shp[:-1] + (shp[-1] // (2 * ROPE_FREQS), 2, ROPE_FREQS)), axis=-2).reshape(shp)


def attn_in_proj(xs, norm_g, scale2, shift2, w_in, q_gain, k_gain, cos_t, sin_t):
    r = xs.shape[0]
    d = D_MODEL
    parts, start = [], 0
    for s in ATTN_SPLITS:
        parts.append(w_in[:, start:start + s])
        start += s
    wqa, wka, wva, wga, wqb, wkb, wvb, wgb = parts
    wqb_p = _reorder_heads(wqb, GQA_HEAD_ORDER, 1)
    wgb_p = _reorder_heads(wgb, GQA_HEAD_ORDER, 1)
    wqb_sw = _swap_rope_halves(wqb_p)
    wkb_sw = _swap_rope_halves(wkb)
    w_ext = jnp.concatenate([wqa, wka, wva, wga, wqb_p, wkb, wvb, wgb_p, wqb_sw, wkb_sw], axis=1).astype(BF16)
    gq = jnp.tile(q_gain, N_HEADS)[None]
    gqs = jnp.tile(_swap_rope_halves(q_gain), N_HEADS)[None]
    gk = jnp.tile(k_gain, 2)[None]
    gks = jnp.tile(_swap_rope_halves(k_gain), 2)[None]
    bdq = jnp.asarray(np.kron(np.eye(N_HEADS), np.full((HEAD_DIM, HEAD_DIM), 1.0 / HEAD_DIM)), BF16)
    bdk = jnp.asarray(np.kron(np.eye(2), np.full((HEAD_DIM, HEAD_DIM), 1.0 / HEAD_DIM)), BF16)
    wcols = w_ext.shape[1]
    const = lambda shp: pl.BlockSpec(shp, lambda i: (0,) * len(shp))
    rows = lambda w: pl.BlockSpec((ROW_TILE, w), lambda i: (i, 0))
    out_shapes = [
        jax.ShapeDtypeStruct((r, 512), BF16), jax.ShapeDtypeStruct((r, 512), BF16), jax.ShapeDtypeStruct((r, 512), BF16),
        jax.ShapeDtypeStruct((r, 512), F32),
        jax.ShapeDtypeStruct((512, r), BF16), jax.ShapeDtypeStruct((r, 128), BF16), jax.ShapeDtypeStruct((128, r), BF16),
        jax.ShapeDtypeStruct((r, 512), F32),
    ]
    cols = lambda w: pl.BlockSpec((w, ROW_TILE), lambda i: (0, i))
    return pl.pallas_call(
        _attn_in_body,
        out_shape=out_shapes,
        grid=(r // ROW_TILE,),
        in_specs=_mod_specs() + [const((d, wcols)), rows(128), rows(128), const((1, 512)), const((1, 512)),
                                 const((1, 128)), const((1, 128)), const((512, 512)), const((128, 128))],
        out_specs=[rows(512), rows(512), rows(512), rows(512), cols(512), rows(128), cols(128), rows(512)],
        compiler_params=_cparams(("parallel",)),
        name="attn_in_proj",
    )(xs, norm_g[None], scale2, shift2, w_ext, cos_t, sin_t, gq, gqs, gk, gks, bdq, bdk)


def _na_cols_body(rpb_ref, sel_ref, neg_ref, o_ref):
    o_ref[...] = jnp.dot(rpb_ref[...], sel_ref[...], precision=HIGHEST, preferred_element_type=F32) + neg_ref[...]


def _na_bias_tables(rpb, rows):
    nrel_r, nrel_c = 2 * NA_WIN_ROWS - 1, 2 * NA_WIN_COLS - 1
    qc = np.arange(GRID_W)[:, None]
    kc = np.arange(GRID_W)[None, :]
    col0 = np.clip(qc - NA_WIN_COLS // 2, 0, GRID_W - NA_WIN_COLS)
    col_ok = (kc >= col0) & (kc < col0 + NA_WIN_COLS)
    rc = kc - qc + NA_WIN_COLS - 1
    sel = np.zeros((128, GRID_W * GRID_W), np.float32)
    sel[np.where(col_ok, rc, 127).reshape(-1), np.arange(GRID_W * GRID_W)] = col_ok.reshape(-1)
    neg = np.where(col_ok, 0.0, NEG_BIG).astype(np.float32).reshape(1, -1)
    rpb2 = jnp.zeros((128, 128), F32).at[:N_HEADS * nrel_r, :nrel_c].set(rpb.reshape(N_HEADS * nrel_r, nrel_c))
    cols = pl.pallas_call(
        _na_cols_body,
        out_shape=jax.ShapeDtypeStruct((128, GRID_W * GRID_W), F32),
        name="na_bias_cols",
    )(rpb2, jnp.asarray(sel), jnp.asarray(neg))
    cols = cols[:N_HEADS * nrel_r].reshape(N_HEADS, nrel_r, GRID_W, GRID_W)
    kh = min(NA_WIN_ROWS, rows)
    g = rows // NA_GROUP_ROWS
    cases = [(0, 0), (NA_GROUP_ROWS, 0), (rows - NA_GROUP_ROWS, NA_GROUP_ROWS * (g - 3))]
    masked = jnp.full((N_HEADS, GRID_W, GRID_W), NEG_BIG, F32)
    tabs = []
    for qr_first, start in cases:
        blocks = []
        for j in range(NA_GROUP_ROWS):
            qr = qr_first + j
            row0 = min(max(qr - kh // 2, 0), rows - kh)
            for i in range(3 * NA_GROUP_ROWS):
                kr = start + i
                blocks.append(cols[:, kr - qr + NA_WIN_ROWS - 1] if row0 <= kr < row0 + kh else masked)
        tab = jnp.stack(blocks, axis=1).reshape(N_HEADS, NA_GROUP_ROWS, 3 * NA_GROUP_ROWS, GRID_W, GRID_W)
        tabs.append(tab.transpose(0, 1, 3, 2, 4).reshape(N_HEADS, ROW_TILE, 3 * ROW_TILE))
    return jnp.stack(tabs) * LOG2E


def _na_body(q_ref, kc_ref, k0_ref, k1_ref, k2_ref, vc_ref, v0_ref, v1_ref, v2_ref, bias_ref, sg_ref, o_ref):
    half = _lane_half((ROW_TILE, 128))
    k_refs = (k0_ref, k1_ref, k2_ref, kc_ref)
    v_refs = (v0_ref, v1_ref, v2_ref, vc_ref)

    def scores(h):
        hp, j = divmod(h, 2)
        ls = slice(hp * 128, (hp + 1) * 128)
        qp = q_ref[:, ls]
        qm = jnp.where(half == j, qp, jnp.zeros_like(qp))
        return [_dot_nt(qm, r[:, ls]) for r in k_refs]

    s_next = scores(0)
    outs = []
    for h in range(N_HEADS):
        hp, j = divmod(h, 2)
        ls = slice(hp * 128, (hp + 1) * 128)
        s = s_next
        if h + 1 < N_HEADS:
            s_next = scores(h + 1)
        s_win = jnp.concatenate(s[:3], axis=1) + bias_ref[0, h]
        s_ctx = s[3]
        m = jnp.maximum(jnp.max(s_win, axis=1, keepdims=True), jnp.max(s_ctx, axis=1, keepdims=True))
        p_win = jnp.exp2(s_win - m).astype(BF16)
        p_ctx = jnp.exp2(s_ctx - m).astype(BF16)
        vms = [jnp.where(half == j, r[:, ls], jnp.ones((ROW_TILE, 128), BF16)) for r in v_refs]
        o = jnp.dot(p_ctx, vms[3], preferred_element_type=F32)
        for b in range(3):
            o += jnp.dot(p_win[:, b * ROW_TILE:(b + 1) * ROW_TILE], vms[b], preferred_element_type=F32)
        outs.append(o / pltpu.roll(o, HEAD_DIM, 1))
        if j == 1:
            o_pair = jnp.where(half == 0, outs[h - 1], outs[h])
            o_ref[:, ls] = (o_pair * sg_ref[:, ls]).astype(BF16)


def na_attention(qa, ka, va, sga, bias_tabs, n):
    g = n // ROW_TILE
    w = BRANCH_W

    def kv_spec(off):
        return pl.BlockSpec((ROW_TILE, w), lambda i: (jnp.clip(i - 1, 0, g - 3) + off + 1, 0))

    ctx_spec = pl.BlockSpec((ROW_TILE, w), lambda i: (0, 0))
    q_spec = pl.BlockSpec((ROW_TILE, w), lambda i: (i + 1, 0))
    case = lambda i: jnp.where(i == 0, 0, jnp.where(i == g - 1, 2, 1))
    bias_spec = pl.BlockSpec((1, N_HEADS, ROW_TILE, 3 * ROW_TILE), lambda i: (case(i), 0, 0, 0))
    return pl.pallas_call(
        _na_body,
        out_shape=jax.ShapeDtypeStruct((n, w), BF16),
        grid=(g,),
        in_specs=[q_spec, ctx_spec, kv_spec(0), kv_spec(1), kv_spec(2), ctx_spec, kv_spec(0), kv_spec(1), kv_spec(2),
                  bias_spec, q_spec],
        out_specs=pl.BlockSpec((ROW_TILE, w), lambda i: (i, 0)),
        compiler_params=_cparams(("parallel",)),
        name="na_attention",
    )(qa, ka, ka, ka, ka, va, va, va, va, bias_tabs, sga)


def _flash_mha_body(q_ref, k_ref, v_ref, sg_ref, o_ref, m_ref, acc_ref):
    kv = pl.program_id(1)
    tq = q_ref.shape[0]

    @pl.when(kv == 0)
    def _():
        m_ref[...] = jnp.full(m_ref.shape, NEG_BIG, F32)
        acc_ref[...] = jnp.zeros(acc_ref.shape, F32)

    khalf = _lane_half((k_ref.shape[0], 128))
    for p in range(N_HEADS // 2):
        ls = slice(p * 128, (p + 1) * 128)
        qp = q_ref[:, ls]
        kp = k_ref[:, ls]
        vp = v_ref[:, ls]
        for j in range(2):
            hh = 2 * p + j
            km = jnp.where(khalf == j, kp, jnp.zeros_like(kp))
            vm = jnp.where(khalf == j, vp, jnp.ones_like(vp))
            s = _dot_nt(qp, km)
            m_prev = m_ref[hh]
            m_new = jnp.maximum(m_prev, jnp.max(s, axis=1, keepdims=True))
            alpha = jnp.exp2(m_prev - m_new)
            pr = jnp.exp2(s - m_new[:, :1]).astype(BF16)
            acc_ref[hh] = alpha * acc_ref[hh] + jnp.dot(pr, vm, preferred_element_type=F32)
            m_ref[hh] = m_new

    @pl.when(kv == pl.num_programs(1) - 1)
    def _():
        half = _lane_half((tq, 128))
        for p in range(N_HEADS // 2):
            ls = slice(p * 128, (p + 1) * 128)
            a0, a1 = acc_ref[2 * p], acc_ref[2 * p + 1]
            o0 = a0 / pltpu.roll(a0, HEAD_DIM, 1)
            o1 = a1 / pltpu.roll(a1, HEAD_DIM, 1)
            o_ref[:, ls] = (jnp.where(half == 0, o0, o1) * sg_ref[:, ls]).astype(BF16)


def flash_mha(q, k, v, sg, *, q_block0, nq, tk, nk):
    tq = ROW_TILE
    return pl.pallas_call(
        _flash_mha_body,
        out_shape=jax.ShapeDtypeStruct((nq * tq, BRANCH_W), BF16),
        grid=(nq, nk),
        in_specs=[
            pl.BlockSpec((tq, BRANCH_W), lambda i, j: (i + q_block0, 0)),
            pl.BlockSpec((tk, BRANCH_W), lambda i, j: (j, 0)),
            pl.BlockSpec((tk, BRANCH_W), lambda i, j: (j, 0)),
            pl.BlockSpec((tq, BRANCH_W), lambda i, j: (i + q_block0, 0)),
        ],
        out_specs=pl.BlockSpec((tq, BRANCH_W), lambda i, j: (i, 0)),
        scratch_shapes=[pltpu.VMEM((N_HEADS, tq, 128), F32)] * 2,
        compiler_params=_cparams(("parallel", "arbitrary")),
        name="flash_mha",
    )(q, k, v, sg)


FLASH_SUM_ROWS = 16


def _flash_gqa_body(qt_ref, k_ref, vt_ref, sg_ref, o_ref, m_ref, acc_ref):
    kv = pl.program_id(1)
    tq = qt_ref.shape[1]
    tk = k_ref.shape[0]

    @pl.when(kv == 0)
    def _():
        m_ref[...] = jnp.full(m_ref.shape, NEG_BIG, F32)
        acc_ref[...] = jnp.zeros(acc_ref.shape, F32)

    khalf = _lane_half((tk, 128))
    kb = k_ref[...]
    kms = [jnp.where(khalf == j, kb, jnp.zeros_like(kb)) for j in range(2)]
    ones = jnp.ones((FLASH_SUM_ROWS, tk), BF16)
    vms = [jnp.concatenate([vt_ref[j * HEAD_DIM:(j + 1) * HEAD_DIM, :], ones], axis=0) for j in range(2)]

    def scores(hh):
        p, j = divmod(hh, 2)
        return jnp.dot(kms[j], qt_ref[p * 128:(p + 1) * 128, :], preferred_element_type=F32).astype(BF16)

    st_next = scores(0)
    for hh in range(N_HEADS):
        st = st_next
        if hh + 1 < N_HEADS:
            st_next = scores(hh + 1)
        m_prev = m_ref[hh]
        m_new = jnp.maximum(m_prev, jnp.max(st, axis=0, keepdims=True).astype(F32))
        alpha = jnp.exp2(m_prev - m_new)
        pt = jnp.exp2(st - m_new[0:1].astype(BF16))
        acc_ref[hh] = alpha[0:1] * acc_ref[hh] + jnp.dot(vms[hh % 2], pt, preferred_element_type=F32)
        m_ref[hh] = m_new

    @pl.when(kv == pl.num_programs(1) - 1)
    def _():
        for p in range(N_HEADS // 2):
            ls = slice(p * 128, (p + 1) * 128)
            a0, a1 = acc_ref[2 * p], acc_ref[2 * p + 1]
            ot = jnp.concatenate([a0[:HEAD_DIM] / a0[HEAD_DIM:HEAD_DIM + 1], a1[:HEAD_DIM] / a1[HEAD_DIM:HEAD_DIM + 1]],
                                 axis=0)
            o_ref[:, ls] = (jnp.transpose(ot) * sg_ref[:, ls]).astype(BF16)


def flash_gqa(qt, k, vt, sg, *, q_block0, nq, tk, nk):
    tq = ROW_TILE
    return pl.pallas_call(
        _flash_gqa_body,
        out_shape=jax.ShapeDtypeStruct((nq * tq, BRANCH_W), BF16),
        grid=(nq, nk),
        in_specs=[
            pl.BlockSpec((BRANCH_W, tq), lambda i, j: (0, i + q_block0)),
            pl.BlockSpec((tk, GQA_KV_W), lambda i, j: (j, 0)),
            pl.BlockSpec((GQA_KV_W, tk), lambda i, j: (0, j)),
            pl.BlockSpec((tq, BRANCH_W), lambda i, j: (i + q_block0, 0)),
        ],
        out_specs=pl.BlockSpec((tq, BRANCH_W), lambda i, j: (i, 0)),
        scratch_shapes=[pltpu.VMEM((N_HEADS, 8, tq), F32), pltpu.VMEM((N_HEADS, HEAD_DIM + FLASH_SUM_ROWS, tq), F32)],
        compiler_params=_cparams(("parallel", "arbitrary")),
        name="flash_gqa",
    )(qt, k, vt, sg)


def _out_body(x_ref, ya_ref, yac_ref, yb_ref, ybc_ref, wa_ref, wb_ref, gate_ref, o_ref):
    is_ctx = pl.program_id(0) == 0
    ya = jnp.where(is_ctx, yac_ref[...], ya_ref[...])
    yb = jnp.where(is_ctx, ybc_ref[...], yb_ref[...])
    y = jnp.dot(ya, wa_ref[...], preferred_element_type=F32)
    y += jnp.dot(yb, wb_ref[...], preferred_element_type=F32)
    o_ref[...] = x_ref[...] + gate_ref[0] * y


def out_proj(xs, ya_lat, ya_ctx, yb_lat, yb_ctx, wa, wb, gate2):
    r = xs.shape[0]
    d = D_MODEL
    rows = lambda w: pl.BlockSpec((ROW_TILE, w), lambda i: (i, 0))
    lat = pl.BlockSpec((ROW_TILE, BRANCH_W), lambda i: (jnp.maximum(i - 1, 0), 0))
    const = lambda shp: pl.BlockSpec(shp, lambda i: (0,) * len(shp))
    ctx = const((CTX_LEN, BRANCH_W))
    return pl.pallas_call(
        _out_body,
        out_shape=jax.ShapeDtypeStruct((r, d), F32),
        grid=(r // ROW_TILE,),
        in_specs=[rows(d), lat, ctx, lat, ctx, const((BRANCH_W, d)), const((BRANCH_W, d)),
                  pl.BlockSpec((1, 1, d), lambda i: (jnp.minimum(i, 1), 0, 0))],
        out_specs=rows(d),
        compiler_params=_cparams(("parallel",)),
        name="out_proj",
    )(xs, ya_lat, ya_ctx, yb_lat, yb_ctx, wa.astype(BF16), wb.astype(BF16), gate2)


KV_TILE_MAX_BLOCKS = 13


def _kv_tile(r):
    nb = r // ROW_TILE
    best = max(k for k in range(1, KV_TILE_MAX_BLOCKS + 1) if nb % k == 0)
    return best * ROW_TILE, nb // best


def attn_layer(xs, mods, norm_g, w_in, rpb, q_gain, k_gain, w_out, cos_t, sin_t):
    r = xs.shape[0]
    n = r - CTX_LEN
    shift2, scale2, gate2 = mods
    qa, ka, va, sga, qbt, kb, vbt, sgb = attn_in_proj(xs, norm_g, scale2, shift2, w_in, q_gain, k_gain, cos_t, sin_t)
    bias_tabs = _na_bias_tables(rpb, n // GRID_W)
    ya_lat = na_attention(qa, ka, va, sga, bias_tabs, n)
    ya_ctx = flash_mha(qa, ka, va, sga, q_block0=0, nq=1, tk=CTX_LEN, nk=1)
    tk, nk = _kv_tile(r)
    yb_lat = flash_gqa(qbt, kb, vbt, sgb, q_block0=1, nq=n // ROW_TILE, tk=tk, nk=nk)
    yb_ctx = flash_gqa(qbt, kb, vbt, sgb, q_block0=0, nq=1, tk=CTX_LEN, nk=1)
    wb = _reorder_heads(w_out[BRANCH_W:], GQA_HEAD_ORDER, 0)
    return out_proj(xs, ya_lat, ya_ctx, yb_lat, yb_ctx, w_out[:BRANCH_W], wb, gate2)


def _split(a):
    hi = a.astype(BF16)
    return hi, (a - hi.astype(F32)).astype(BF16)


def _dot3(a, b, dims=(((1,), (0,)), ((), ()))):
    ah, al = _split(a)
    bh, bl = _split(b)
    dg = functools.partial(lax.dot_general, dimension_numbers=dims, preferred_element_type=F32)
    return dg(ah, bh) + (dg(al, bh) + dg(ah, bl))


def _dot1(a, b, dims=(((1,), (0,)), ((), ()))):
    return lax.dot_general(a.astype(BF16), b.astype(BF16), dims, preferred_element_type=F32)


_NT = (((1,), (1,)), ((), ()))
_TN = (((0,), (0,)), ((), ()))


RWKV_SHIFT_W = 1664
HY_IN_W = 1536
HALO = 8
REC_HALO_W = RWKV_SHIFT_W + HY_IN_W


def _rec_in_body(x_ref, xp_ref, xn_ref, g_ref, scale_ref, shift_ref, w_ref, mu_ref, taps_ref,
                 rw_ref, hv_ref, hx1_ref, hx2_ref, sgr_ref, sgh_ref, u_scr):
    i = pl.program_id(0)
    nt = pl.num_programs(0)
    xe = jnp.concatenate([xp_ref[...], x_ref[...], xn_ref[...]], axis=0)
    y = xe * lax.rsqrt(jnp.mean(xe * xe, axis=-1, keepdims=True) + NORM_EPS)
    xm = ((y * g_ref[...]) * (1.0 + scale_ref[0]) + shift_ref[0]).astype(BF16)
    u = jnp.dot(xm, w_ref[...], preferred_element_type=F32)
    row = lax.broadcasted_iota(jnp.int32, (ROW_TILE + 2 * HALO, 1), 0)
    keep = jnp.logical_and(jnp.logical_or(row >= HALO, i >= 2),
                           jnp.logical_or(row < ROW_TILE + HALO, jnp.logical_and(i >= 1, i < nt - 1)))
    u_scr[...] = jnp.where(keep, u[:, :REC_HALO_W], 0.0)
    up = u_scr[pl.ds(HALO - 1, ROW_TILE), :]
    uc = u_scr[pl.ds(HALO, ROW_TILE), :]
    un = u_scr[pl.ds(HALO + 1, ROW_TILE), :]
    w = RWKV_SHIFT_W
    rw_c = uc[:, :w]
    rw_ref[...] = rw_c + (0.5 * (up[:, :w] + un[:, :w]) - rw_c) * mu_ref[...]
    hy = up[:, w:] * taps_ref[0:1] + uc[:, w:] * taps_ref[1:2] + un[:, w:] * taps_ref[2:3]
    hv_ref[...] = hy[:, 0:512]
    hx1_ref[...] = hy[:, 512:1024]
    hx2_ref[...] = hy[:, 1024:1536]
    uc_all = u[HALO:HALO + ROW_TILE]
    sgr_ref[...] = _silu(uc_all[:, REC_HALO_W:REC_HALO_W + 512])
    sgh_ref[...] = _silu(uc_all[:, REC_HALO_W + 512:REC_HALO_W + 1024])


def rec_in_proj(xs, norm_g, scale2, shift2, w_in, mu, hy_short):
    r = xs.shape[0]
    d = D_MODEL
    w = RWKV_SHIFT_W
    w_ext = jnp.concatenate([w_in[:, :w], w_in[:, w + 512:w + 512 + HY_IN_W], w_in[:, w:w + 512],
                             w_in[:, w + 512 + HY_IN_W:]], axis=1).astype(BF16)
    nh = r // HALO
    per = ROW_TILE // HALO
    const = lambda shp: pl.BlockSpec(shp, lambda i: (0,) * len(shp))
    rows = lambda wd: pl.BlockSpec((ROW_TILE, wd), lambda i: (i, 0))
    f = lambda wd: jax.ShapeDtypeStruct((r, wd), F32)
    mod = _mod_specs()
    return pl.pallas_call(
        _rec_in_body,
        out_shape=[f(w), f(512), f(512), f(512), f(512), f(512)],
        grid=(r // ROW_TILE,),
        in_specs=[mod[0],
                  pl.BlockSpec((HALO, d), lambda i: (jnp.maximum(i * per - 1, 0), 0)),
                  pl.BlockSpec((HALO, d), lambda i: (jnp.minimum((i + 1) * per, nh - 1), 0)),
                  mod[1], mod[2], mod[3], const((d, w_ext.shape[1])), const((1, w)), const((3, HY_IN_W))],
        out_specs=[rows(w), rows(512), rows(512), rows(512), rows(512), rows(512)],
        scratch_shapes=[pltpu.VMEM((ROW_TILE + 2 * HALO, REC_HALO_W), F32)],
        compiler_params=_cparams(("parallel",)),
        name="rec_in_proj",
    )(xs, xs, xs, norm_g[None], scale2, shift2, w_ext, mu[None], hy_short)


CHUNK = 64
CPT = ROW_TILE // CHUNK


def _block_sum_mat(width, value):
    return jnp.asarray(np.kron(np.eye(width // HEAD_DIM), np.full((HEAD_DIM, HEAD_DIM), value)), F32)


def _rwkv_prep_body(r_ref, k_ref, v_ref, lora_ref, w0_ref, wup_ref, a0_ref, aup_ref, kk_ref, ka_ref, rk_ref,
                    tri_ref, bs_ref, g_ref, add_ref, bonus_ref):
    t = ROW_TILE
    r, k, v, lora = r_ref[...], k_ref[...], v_ref[...], lora_ref[...]
    bs = bs_ref[...]
    kk = k * kk_ref[...]
    kk = kk * lax.rsqrt(_dot_stat(kk * kk, bs) + 1e-12)
    row = lax.broadcasted_iota(jnp.int32, (t, t), 0)
    col = lax.broadcasted_iota(jnp.int32, (t, t), 1)
    same = (row // CHUNK) == (col // CHUNK)
    eye = (row == col).astype(F32)
    wl_both = _dot3(jnp.tanh(lora), wup_ref[...])
    al_both = _dot3(lora, aup_ref[...])
    half = _lane_half((t, 128))
    half_c = _lane_half((HEAD_DIM, 128))
    rowc = lax.broadcasted_iota(jnp.int32, (HEAD_DIM, 128), 0)
    lanec = lax.broadcasted_iota(jnp.int32, (HEAD_DIM, 128), 1)
    level_masks = []
    bsz = 2
    while bsz < CHUNK:
        level_masks.append(jnp.logical_and((row // (2 * bsz)) == (col // (2 * bsz)), (row // bsz) != (col // bsz)))
        bsz *= 2
    first_mask = (row // 2) == (col // 2)

    dirs = []
    for d in range(2):
        ds = slice(d * 128, (d + 1) * 128)
        wl = w0_ref[d] + wl_both[:, ds]
        z = -wl
        w_log = -(jnp.maximum(z, 0.0) + jnp.log(1.0 + jnp.exp(-jnp.abs(z)))) - 0.5
        lw = -jnp.exp(w_log)
        a = 1.0 / (1.0 + jnp.exp(-(a0_ref[d] + al_both[:, ds])))
        kd = k * (1.0 + (a - 1.0) * ka_ref[...])
        b = kk * a
        incl = jnp.logical_and(same, (col <= row) if d == 0 else (col >= row))
        lw_hi, lw_lo = _split(lw)
        tri = tri_ref[d]
        cs = jnp.dot(tri, lw_hi, preferred_element_type=F32) + jnp.dot(tri, lw_lo, preferred_element_type=F32)
        ends = [c * CHUNK + (CHUNK - 1 if d == 0 else 0) for c in range(CPT)]
        tot = jnp.concatenate([jnp.broadcast_to(cs[e:e + 1], (CHUNK, 128)) for e in ends], axis=0)
        w_inv = jnp.exp(-cs)
        w_rest = jnp.exp(tot - cs)
        dirs.append(dict(kd=kd, incl=incl, strict=jnp.logical_and(incl, row != col), tot=tot,
                         kkt=kk * jnp.exp(cs - lw), kh=kd * w_inv, bh=b * w_inv, rt=r * jnp.exp(cs),
                         kdd=kd * w_rest, bdd=b * w_rest))
    chains = [(dd, j) for dd in dirs for j in range(2)]
    sels = [half == j for _, j in chains]
    bms = [jnp.where(sel, dd["bh"], 0.0) for (dd, _), sel in zip(chains, sels)]
    kms = [jnp.where(sel, dd["kh"], 0.0) for (dd, _), sel in zip(chains, sels)]
    l_bs = [jnp.where(dd["strict"], _dot1(dd["kkt"], bm, _NT), 0.0) for (dd, _), bm in zip(chains, bms)]
    tinvs = [eye - jnp.where(first_mask, l_b, 0.0) for l_b in l_bs]
    for mask in level_masks:
        xs = [_dot1(jnp.where(mask, l_b, 0.0), tinv) for l_b, tinv in zip(l_bs, tinvs)]
        tinvs = [tinv - _dot1(tinv, x) for tinv, x in zip(tinvs, xs)]
    l_ks = [jnp.where(dd["strict"], _dot1(dd["kkt"], km, _NT), 0.0) for (dd, _), km in zip(chains, kms)]
    a_rks = [jnp.where(dd["incl"], _dot1(dd["rt"], km, _NT), 0.0) for (dd, _), km in zip(chains, kms)]
    a_rbs = [jnp.where(dd["incl"], _dot1(dd["rt"], bm, _NT), 0.0) for (dd, _), bm in zip(chains, bms)]
    lvs = [_dot1(l_k, v) for l_k in l_ks]
    pus = [_dot1(tinv, jnp.concatenate([dd["kkt"], lv], axis=1)) for (dd, _), tinv, lv in zip(chains, tinvs, lvs)]
    cors = [_dot1(a_rb, pu) for a_rb, pu in zip(a_rbs, pus)]
    ps = [pu[:, :128] for pu in pus]
    u0s = [pu[:, 128:] for pu in pus]
    qs = [dd["rt"] - cor[:, :128] for (dd, _), cor in zip(chains, cors)]
    y0s = [_dot1(a_rk, v) - cor[:, 128:] for a_rk, cor in zip(a_rks, cors)]

    sel0 = half == 0
    for d, dd in enumerate(dirs):
        p, u0, q, y0 = (jnp.where(sel0, x[2 * d], x[2 * d + 1]) for x in (ps, u0s, qs, y0s))
        for c in range(CPT):
            rs = slice(c * CHUNK, (c + 1) * CHUNK)
            x1 = _dot1(dd["bdd"][rs], p[rs], _TN)
            x2 = _dot1(dd["kdd"][rs], v[rs], _TN) - _dot1(dd["bdd"][rs], u0[rs], _TN)
            m_pair = jnp.where(half_c == 0, x1[:HEAD_DIM], x1[HEAD_DIM:])
            n_pair = jnp.where(half_c == 0, x2[:HEAD_DIM], x2[HEAD_DIM:])
            wc = jnp.exp(dd["tot"][c * CHUNK:c * CHUNK + 1])
            dg = jnp.where((lanec % HEAD_DIM) == rowc, wc, 0.0)
            g_ref[c, d, 0:HEAD_DIM, :] = dg - m_pair
            g_ref[c, d, HEAD_DIM:, :] = q[rs]
            add_ref[c, d, 0:HEAD_DIM, :] = n_pair
            add_ref[c, d, HEAD_DIM:, :] = y0[rs]

    kd_sum = dirs[0]["kd"] + dirs[1]["kd"]
    bonus_ref[...] = 0.5 * _dot_stat(r * kd_sum * rk_ref[...], bs) * v


def _lora_ext(up, first_row):
    out = jnp.zeros((2, 128, BRANCH_W), F32)
    for d in range(2):
        out = out.at[d, first_row + 32 * d:first_row + 32 * (d + 1)].set(up[d])
    return out


def rwkv_prep(rw, w0, w_up, a0, a_up, k_k, k_a, r_k):
    r = rw.shape[0]
    nt = r // ROW_TILE
    nch = r // CHUNK
    t = ROW_TILE
    ii = np.arange(t)
    same = (ii[:, None] // CHUNK) == (ii[None, :] // CHUNK)
    tri = jnp.asarray(np.stack([same & (ii[None, :] <= ii[:, None]), same & (ii[None, :] >= ii[:, None])]), BF16)
    pair_cat = lambda w: w.reshape(2, 128, N_HEADS // 2, 128).transpose(1, 2, 0, 3).reshape(128, 2 * BRANCH_W)
    lane = lambda blk: pl.BlockSpec((t, 128), lambda i, p, blk=blk: (i, blk + p))
    pvec = pl.BlockSpec((1, 128), lambda i, p: (0, p))
    dvec = pl.BlockSpec((2, 1, 128), lambda i, p: (0, 0, p))
    dmat = pl.BlockSpec((128, 256), lambda i, p: (0, p))
    gspec = pl.BlockSpec((CPT, 2, HEAD_DIM + CHUNK, 128), lambda i, p: (i, 0, 0, p))
    gshape = jax.ShapeDtypeStruct((nch, 2, HEAD_DIM + CHUNK, BRANCH_W), F32)
    return pl.pallas_call(
        _rwkv_prep_body,
        out_shape=[gshape, gshape, jax.ShapeDtypeStruct((r, BRANCH_W), F32)],
        grid=(nt, N_HEADS // 2),
        in_specs=[lane(0), lane(4), lane(8), pl.BlockSpec((t, 128), lambda i, p: (i, 12)),
                  dvec, dmat, dvec, dmat, pvec, pvec, pvec,
                  pl.BlockSpec((2, t, t), lambda i, p: (0, 0, 0)),
                  pl.BlockSpec((128, 128), lambda i, p: (0, 0))],
        out_specs=[gspec, gspec, pl.BlockSpec((t, 128), lambda i, p: (i, p))],
        compiler_params=_cparams(("parallel", "parallel")),
        name="rwkv_prep",
    )(rw, rw, rw, rw, w0.reshape(2, 1, BRANCH_W), pair_cat(_lora_ext(w_up, 0)), a0.reshape(2, 1, BRANCH_W),
      pair_cat(_lora_ext(a_up, 64)), k_k[None], k_a[None], r_k.reshape(1, BRANCH_W), tri,
      _block_sum_mat(128, 1.0).astype(BF16))


def _rwkv_scan_body(gf_ref, af_ref, gb_ref, ab_ref, yf_ref, yb_ref, st_ref):
    @pl.when(pl.program_id(0) == 0)
    def _():
        st_ref[...] = jnp.zeros(st_ref.shape, F32)

    rowh = lax.broadcasted_iota(jnp.int32, (128, 128), 0) // HEAD_DIM
    diag = rowh == _lane_half((128, 128))
    for s in range(CPT):
        for d, (g_ref, a_ref, y_ref, c) in enumerate(((gf_ref, af_ref, yf_ref, s), (gb_ref, ab_ref, yb_ref, CPT - 1 - s))):
            for p in range(N_HEADS // 2):
                ls = slice(p * 128, (p + 1) * 128)
                out = _dot3(g_ref[c, 0, :, ls], st_ref[d, p]) + a_ref[c, 0, :, ls]
                hn = out[:HEAD_DIM]
                st_ref[d, p] = jnp.where(diag, jnp.concatenate([hn, hn], axis=0), 0.0)
                y_ref[c * CHUNK:(c + 1) * CHUNK, ls] = out[HEAD_DIM:]


def rwkv_scan(g, add):
    nch = g.shape[0]
    r = nch * CHUNK
    nt = r // ROW_TILE
    assert CTX_LEN == ROW_TILE
    rev = lambda i: jnp.where(i == 0, 0, nt - i)
    blk = (CPT, 1, HEAD_DIM + CHUNK, BRANCH_W)
    fwd = pl.BlockSpec(blk, lambda i: (i, 0, 0, 0))
    bwd = pl.BlockSpec(blk, lambda i: (rev(i), 1, 0, 0))
    yshape = jax.ShapeDtypeStruct((r, BRANCH_W), F32)
    return pl.pallas_call(
        _rwkv_scan_body,
        out_shape=[yshape, yshape],
        grid=(nt,),
        in_specs=[fwd, fwd, bwd, bwd],
        out_specs=[pl.BlockSpec((ROW_TILE, BRANCH_W), lambda i: (i, 0)),
                   pl.BlockSpec((ROW_TILE, BRANCH_W), lambda i: (rev(i), 0))],
        scratch_shapes=[pltpu.VMEM((2, N_HEADS // 2, 128, 128), F32)],
        compiler_params=_cparams(("arbitrary",)),
        name="rwkv_scan",
    )(g, add, g, add)


HY_WIDTH = 512
HY_ORDER = 2
HY_POS_BANDS = 16
HY_HIDDEN = 64
HY_TAPS_W = 2 * HY_ORDER * HY_WIDTH
FFT_N2 = ROW_TILE


def _dot3c(ah, al, b):
    bh, bl = _split(b)
    dg = functools.partial(jnp.dot, preferred_element_type=F32)
    return dg(ah, bh) + (dg(al, bh) + dg(ah, bl))


def _dotc(ah, b):
    return jnp.dot(ah, b.astype(BF16), preferred_element_type=F32)


def _split_const(m):
    m = np.asarray(m, np.float32)
    hi = m.astype(BF16)
    lo = (m - hi.astype(np.float32)).astype(BF16)
    return jnp.asarray(hi), jnp.asarray(lo)


TAPS_FLAT_COLS = 8


def _filter_taps(t_idx, length, c2pb_ref, w1t_ref, w1c_ref, w1s_ref, b1_ref, w2_ref, b2_ref, w3_ref, b3_ref, absd_ref):
    t = t_idx / float(max(length - 1, 1))
    ang = c2pb_ref[...] * t_idx / float(length)
    pre = t * w1t_ref[...] + _dot3(jnp.cos(ang), w1c_ref[...]) - _dot3(jnp.sin(ang), w1s_ref[...]) + b1_ref[...]
    hid = jnp.sin(pre)
    hid = jnp.sin(_dot3(hid, w2_ref[...]) + b2_ref[...])
    return (_dot3(hid, w3_ref[...]) + b3_ref[...]) * jnp.exp(-t * absd_ref[...])


def _hy_taps_body(*refs, length):
    taps_ref, ssq_ref, tap0_ref = refs[-3:]
    i = pl.program_id(0)
    t_idx = (i * ROW_TILE + lax.broadcasted_iota(jnp.int32, (ROW_TILE, 1), 0)).astype(F32)
    taps = _filter_taps(t_idx, length, *refs[:-3])
    taps_ref[...] = taps

    @pl.when(i == 0)
    def _():
        ssq_ref[...] = jnp.zeros(ssq_ref.shape, F32)
        tap0_ref[...] = taps[0:1]

    ssq_ref[...] += jnp.sum(taps * taps, axis=0, keepdims=True)


def _hy_taps_flat_body(c2pb_ref, w1t_ref, w1c_ref, w1s_ref, b1_ref, w2_ref, b2_ref, w3_ref, b3_ref, absd_ref,
                       taps_ref, ssq_ref, tap0_ref, *, length):
    j = pl.program_id(0)
    rows = length // FFT_N2
    nb = TAPS_FLAT_COLS

    def positions(width, per):
        a = lax.broadcasted_iota(jnp.int32, (rows, width), 0)
        b = lax.broadcasted_iota(jnp.int32, (rows, width), 1) // per
        return (a * FFT_N2 + j * nb + b).astype(F32)

    @pl.when(j == 0)
    def _():
        ssq_ref[...] = jnp.zeros(ssq_ref.shape, F32)

    h = HY_HIDDEN
    ang = c2pb_ref[...] * positions(nb * HY_POS_BANDS, HY_POS_BANDS) / float(length)
    t_h = positions(nb * h, h) / float(max(length - 1, 1))
    pre = t_h * w1t_ref[...] + _dot3(jnp.cos(ang), w1c_ref[...]) - _dot3(jnp.sin(ang), w1s_ref[...]) + b1_ref[...]
    hid = jnp.sin(pre)
    hid = jnp.sin(_dot3(hid, w2_ref[...]) + b2_ref[...])
    for bp in range(nb // 2):
        t_w = (positions(2 * HY_TAPS_W, HY_TAPS_W) + float(2 * bp)) / float(max(length - 1, 1))
        taps = (_dot3(hid[:, bp * 2 * h:(bp + 1) * 2 * h], w3_ref[...]) + b3_ref[...]) * jnp.exp(-t_w * absd_ref[...])
        taps_ref[:, bp * 2 * HY_TAPS_W:(bp + 1) * 2 * HY_TAPS_W] = taps.astype(taps_ref.dtype)
        if bp == 0:
            @pl.when(j == 0)
            def _():
                tap0_ref[...] = taps[0:1, :HY_TAPS_W]
        sq = jnp.sum(taps * taps, axis=0, keepdims=True)
        ssq_ref[...] += sq[:, :HY_TAPS_W] + sq[:, HY_TAPS_W:]


def hyena_taps(length, w1, b1, w2, b2, w3, b3, *, flat):
    bands = jnp.linspace(1e-4, HY_POS_BANDS - 1, HY_POS_BANDS, dtype=F32)
    c2pb = jnp.zeros((1, 128), F32).at[0, :HY_POS_BANDS].set(2.0 * math.pi * bands)
    pad = lambda m: jnp.zeros((128, HY_HIDDEN), F32).at[:HY_POS_BANDS].set(m)
    deltas = jnp.linspace(math.log(1e-2) / 0.3, math.log(1e-2) / 1.5, HY_WIDTH, dtype=F32)
    absd = jnp.tile(jnp.abs(deltas), 2 * HY_ORDER)[None]
    const = lambda shp: pl.BlockSpec(shp, lambda i: (0,) * len(shp))
    w1t, w1c, w1s = w1[0:1], w1[1:1 + HY_POS_BANDS], w1[1 + HY_POS_BANDS:]
    if flat:
        rows, nb = length // FFT_N2, TAPS_FLAT_COLS
        bd = lambda m, k: jnp.kron(jnp.eye(k, dtype=F32), m)
        tile = lambda v, k: jnp.tile(v, k)[None]
        operands = (tile(c2pb[0, :HY_POS_BANDS], nb), tile(w1t[0], nb), bd(w1c, nb), bd(w1s, nb), tile(b1, nb),
                    bd(w2, nb), tile(b2, nb), bd(w3, 2), tile(b3, 2), tile(absd[0], 2))
        body, grid = _hy_taps_flat_body, (FFT_N2 // nb,)
        taps_shape, taps_dtype = (rows, FFT_N2 * HY_TAPS_W), BF16
        taps_spec = pl.BlockSpec((rows, nb * HY_TAPS_W), lambda i: (0, i))
    else:
        operands = (c2pb, w1t, pad(w1c), pad(w1s), b1[None], w2, b2[None], w3, b3[None], absd)
        body, grid = _hy_taps_body, (length // ROW_TILE,)
        taps_shape, taps_dtype = (length, HY_TAPS_W), F32
        taps_spec = pl.BlockSpec((ROW_TILE, HY_TAPS_W), lambda i: (i, 0))
    taps, ssq, tap0 = pl.pallas_call(
        functools.partial(body, length=length),
        out_shape=[jax.ShapeDtypeStruct(taps_shape, taps_dtype), jax.ShapeDtypeStruct((1, HY_TAPS_W), F32),
                   jax.ShapeDtypeStruct((1, HY_TAPS_W), F32)],
        grid=grid,
        in_specs=[const(op.shape) for op in operands],
        out_specs=[taps_spec, const((1, HY_TAPS_W)), const((1, HY_TAPS_W))],
        compiler_params=_cparams(("arbitrary",)),
        name="hyena_taps_flat" if flat else "hyena_taps",
    )(*operands)
    hw = HY_TAPS_W // 2
    norm2 = ssq[:, :hw] + ssq[:, hw:] + 2.0 * tap0[:, :hw] * tap0[:, hw:]
    return taps, lax.rsqrt(norm2)


class _FftPlan:
    def __init__(self, length):
        self.length = length
        self.n = 2 * length
        self.n2 = FFT_N2
        self.n1 = self.n // self.n2
        self.n1h = self.n1 // 2
        k1 = self.n1h + 1
        self.k1p = -(-k1 // 8) * 8
        kk = np.arange(self.k1p)[:, None].astype(np.float64)
        live = (kk < k1)
        nn = np.arange(self.n1h)[None, :].astype(np.float64)
        th = 2.0 * np.pi * kk * nn / self.n1
        self.f1 = _split_const(np.concatenate([np.cos(th) * live, -np.sin(th) * live], axis=0))
        ck = np.where((kk == 0) | (kk == self.n1h), 1.0, 2.0) * live / self.n
        self.g1 = _split_const(np.concatenate([np.cos(th) * ck, -np.sin(th) * ck], axis=0).T)
        m = np.arange(self.n2).astype(np.float64)
        ph = 2.0 * np.pi * np.outer(m, m) / self.n2
        c, s = np.cos(ph), np.sin(ph)
        self.fb = _split_const(np.block([[c, s], [-s, c]]))
        self.fbi = _split_const(np.block([[c, -s], [s, c]]))
        tw = 2.0 * np.pi * kk[:, :, None] * m[None, :, None] / self.n
        self.twc = jnp.asarray(np.cos(tw), F32)
        self.tws = jnp.asarray(np.sin(tw), F32)


FFT_TN = 4096


def _fft_a_body(fh_ref, x_ref, o_ref):
    o_ref[...] = _dotc(fh_ref[...], x_ref[...]).astype(o_ref.dtype)


def fft_stage_a(plan, xf, lead):
    rows_in, m = xf.shape
    rows = 2 * plan.k1p
    tn = min(FFT_TN, m)
    fh = jnp.pad(plan.f1[0], ((0, 0), (lead, 0)))
    fspec = pl.BlockSpec((rows, rows_in), lambda j: (0, 0))
    return pl.pallas_call(
        _fft_a_body,
        out_shape=jax.ShapeDtypeStruct((rows, m), BF16),
        grid=(m // tn,),
        in_specs=[fspec, pl.BlockSpec((rows_in, tn), lambda j: (0, j))],
        out_specs=pl.BlockSpec((rows, tn), lambda j: (0, j)),
        compiler_params=_cparams(("parallel",)),
        name="fft_stage_a",
    )(fh, xf)


def _twiddled(a_ref, twc_ref, tws_ref):
    are, aim = a_ref[0, 0].astype(F32), a_ref[1, 0].astype(F32)
    c, s = twc_ref[0], tws_ref[0]
    return jnp.concatenate([are * c + aim * s, aim * c - are * s], axis=0)


def _fft_filter_b_body(a_ref, twc_ref, tws_ref, fbh_ref, scale_ref, o_ref):
    n2 = FFT_N2
    x = _dotc(fbh_ref[...], _twiddled(a_ref, twc_ref, tws_ref))
    hw = HY_TAPS_W // 2
    xre, xim = x[:n2], x[n2:]
    o_ref[0, 0] = ((xre[:, :hw] + xre[:, hw:]) * scale_ref[...]).astype(o_ref.dtype)
    o_ref[1, 0] = ((xim[:, :hw] - xim[:, hw:]) * scale_ref[...]).astype(o_ref.dtype)


def fft_filter_stage_b(plan, a, scale):
    n2, k1p = plan.n2, plan.k1p
    hw = HY_TAPS_W // 2
    const = lambda shp: pl.BlockSpec(shp, lambda k: (0,) * len(shp))
    return pl.pallas_call(
        _fft_filter_b_body,
        out_shape=jax.ShapeDtypeStruct((2, k1p, n2, hw), BF16),
        grid=(k1p,),
        in_specs=[pl.BlockSpec((2, 1, n2, HY_TAPS_W), lambda k: (0, k, 0, 0)),
                  pl.BlockSpec((1, n2, 1), lambda k: (k, 0, 0)), pl.BlockSpec((1, n2, 1), lambda k: (k, 0, 0)),
                  const((2 * n2, 2 * n2)), const((1, hw))],
        out_specs=pl.BlockSpec((2, 1, n2, hw), lambda k: (0, k, 0, 0)),
        compiler_params=_cparams(("parallel",)),
        name="fft_filter_stage_b",
    )(a, plan.twc, plan.tws, plan.fb[0], scale)


def _fft_conv_b_body(a_ref, kf_ref, twc_ref, tws_ref, fbh_ref, fih_ref, o_ref):
    n2 = FFT_N2
    z = _dotc(fbh_ref[...], _twiddled(a_ref, twc_ref, tws_ref))
    zre, zim = z[:n2], z[n2:]
    kre, kim = kf_ref[0, 0].astype(F32), kf_ref[1, 0].astype(F32)
    y = jnp.concatenate([zre * kre - zim * kim, zre * kim + zim * kre], axis=0)
    q = _dotc(fih_ref[...], y)
    qre, qim = q[:n2], q[n2:]
    c, s = twc_ref[0], tws_ref[0]
    o_ref[0, 0] = (qre * c - qim * s).astype(o_ref.dtype)
    o_ref[1, 0] = (qim * c + qre * s).astype(o_ref.dtype)


def fft_conv_stage_b(plan, a, kf, order):
    n2, k1p = plan.n2, plan.k1p
    const = lambda shp: pl.BlockSpec(shp, lambda k: (0,) * len(shp))
    return pl.pallas_call(
        _fft_conv_b_body,
        out_shape=jax.ShapeDtypeStruct((2, k1p, n2, HY_WIDTH), BF16),
        grid=(k1p,),
        in_specs=[pl.BlockSpec((2, 1, n2, HY_WIDTH), lambda k: (0, k, 0, 0)),
                  pl.BlockSpec((2, 1, n2, HY_WIDTH), lambda k: (0, k, 0, order)),
                  pl.BlockSpec((1, n2, 1), lambda k: (k, 0, 0)), pl.BlockSpec((1, n2, 1), lambda k: (k, 0, 0)),
                  const((2 * n2, 2 * n2)), const((2 * n2, 2 * n2))],
        out_specs=pl.BlockSpec((2, 1, n2, HY_WIDTH), lambda k: (0, k, 0, 0)),
        compiler_params=_cparams(("parallel",)),
        name="fft_conv_stage_b",
    )(a, kf, plan.twc, plan.tws, plan.fb[0], plan.fbi[0])


def _fft_inv_a_body(gh_ref, q_ref, z_ref, gate_ref, skip_ref, o_ref):
    y = _dotc(gh_ref[...], q_ref[...])
    o_ref[...] = gate_ref[...] * (y + z_ref[...] * skip_ref[...])


def fft_inv_stage_a(plan, qf, zf, gatef, skip_t, lead):
    rows_out, m = zf.shape
    rows = 2 * plan.k1p
    tn = skip_t.shape[1]
    gh = jnp.pad(plan.g1[0], ((lead, 0), (0, 0)))
    col = lambda r_: pl.BlockSpec((r_, tn), lambda j: (0, j))
    gspec = pl.BlockSpec((rows_out, rows), lambda j: (0, 0))
    return pl.pallas_call(
        _fft_inv_a_body,
        out_shape=jax.ShapeDtypeStruct((rows_out, m), F32),
        grid=(m // tn,),
        in_specs=[gspec, col(rows), col(rows_out), col(rows_out), pl.BlockSpec((1, tn), lambda j: (0, 0))],
        out_specs=col(rows_out),
        compiler_params=_cparams(("parallel",)),
        name="fft_inv_stage_a",
    )(gh, qf, zf, gatef, skip_t)


def hyena_long(hv, hx1, hx2, taps, scale, skip, lead):
    length = taps.shape[0] * FFT_N2
    plan = _FftPlan(length)
    n2, k1p = plan.n2, plan.k1p
    m = n2 * HY_WIDTH
    tn = min(FFT_TN, m)
    flat = lambda a: a.reshape(lead + plan.n1h, m)
    ta = fft_stage_a(plan, taps, 0)
    kf = fft_filter_stage_b(plan, ta.reshape(2, k1p, n2, HY_TAPS_W), scale)
    z = flat(hv)
    for o, gate in enumerate((hx1, hx2)):
        a = fft_stage_a(plan, z, lead)
        q = fft_conv_stage_b(plan, a.reshape(2, k1p, n2, HY_WIDTH), kf, o)
        z = fft_inv_stage_a(plan, q.reshape(2 * k1p, m), z, flat(gate), jnp.tile(skip[o], tn // HY_WIDTH)[None], lead)
    return z.reshape(-1, HY_WIDTH)


def _rec_out_body(x_ref, yf_ref, yb_ref, bonus_ref, sgr_ref, zh_ref, zc_ref, sgh_ref, gnw_ref, gnb_ref, bm_ref, wa_ref, wb_ref,
                  gate_ref, fin_ref, o_ref, *, final):
    y = yf_ref[...] + yb_ref[...]
    bm = bm_ref[...]
    mean = _dot_stat(y, bm)
    yc = y - mean
    var = _dot_stat(yc * yc, bm)
    yn = yc * lax.rsqrt(var + RWKV_GN_EPS) * gnw_ref[...] + gnb_ref[...]
    ya = ((yn + bonus_ref[...]) * sgr_ref[...]).astype(BF16)
    zh = zh_ref[...]
    if not final:
        zh = jnp.where(pl.program_id(0) == 0, zc_ref[...], zh)
    yh = (zh * sgh_ref[...]).astype(BF16)
    out = jnp.dot(ya, wa_ref[...], preferred_element_type=F32) + jnp.dot(yh, wb_ref[...], preferred_element_type=F32)
    xn = x_ref[...] + gate_ref[0] * out
    if final:
        xn = xn * lax.rsqrt(jnp.mean(xn * xn, axis=-1, keepdims=True) + NORM_EPS) * fin_ref[...]
    o_ref[...] = xn


RWKV_GN_EPS = 64e-5


def rec_out_proj(xs, yf, yb, bonus, sgr, zh, zc, sgh, gn_w, gn_b, w_out, gate2, final_g, *, final):
    r = xs.shape[0]
    d = D_MODEL
    off = 1 if final else 0
    nt = r // ROW_TILE - off
    rows = lambda w: pl.BlockSpec((ROW_TILE, w), lambda i: (i + off, 0))
    const = lambda shp: pl.BlockSpec(shp, lambda i: (0,) * len(shp))
    bw = BRANCH_W
    return pl.pallas_call(
        functools.partial(_rec_out_body, final=final),
        out_shape=jax.ShapeDtypeStruct((nt * ROW_TILE, d), F32),
        grid=(nt,),
        in_specs=[rows(d), rows(bw), rows(bw), rows(bw), rows(bw), rows(bw), const((CTX_LEN, bw)), rows(bw),
                  const((1, bw)), const((1, bw)),
                  const((bw, bw)), const((bw, d)), const((bw, d)),
                  pl.BlockSpec((1, 1, d), lambda i: (jnp.minimum(i + off, 1), 0, 0)), const((1, d))],
        out_specs=pl.BlockSpec((ROW_TILE, d), lambda i: (i, 0)),
        compiler_params=_cparams(("parallel",)),
        name="rec_out_proj",
    )(xs, yf, yb, bonus, sgr, zh, zc, sgh, gn_w[None], gn_b[None], _block_sum_mat(bw, 1.0 / HEAD_DIM).astype(BF16),
      w_out[:bw].astype(BF16), w_out[bw:].astype(BF16), gate2, final_g[None])


def rec_layer(xs, mods, norm_g, w_in, mu, w0, w_up, a0, a_up, k_k, k_a, r_k, gn_w, gn_b, hy_short, hy_w1, hy_b1, hy_w2,
              hy_b2, hy_w3, hy_b3, hy_skip, w_out, final_g, final):
    shift2, scale2, gate2 = mods
    rw, hv, hx1, hx2, sgr, sgh = rec_in_proj(xs, norm_g, scale2, shift2, w_in, mu, hy_short)
    g, add, bonus = rwkv_prep(rw, w0, w_up, a0, a_up, k_k, k_a, r_k)
    yf, yb = rwkv_scan(g, add)
    fargs = (hy_w1, hy_b1, hy_w2, hy_b2, hy_w3, hy_b3)
    n = xs.shape[0] - CTX_LEN
    taps, scale = hyena_taps(n, *fargs, flat=True)
    zh = hyena_long(hv, hx1, hx2, taps, scale, hy_skip, CTX_LEN // FFT_N2)
    if final:
        z_ctx = hv[:CTX_LEN]
    else:
        taps_c, scale_c = hyena_taps(CTX_LEN, *fargs, flat=False)
        z_ctx = hyena_short(hv[:CTX_LEN], hx1[:CTX_LEN], hx2[:CTX_LEN], taps_c, scale_c, hy_skip)
    return rec_out_proj(xs, yf, yb, bonus, sgr, zh, z_ctx, sgh, gn_w, gn_b, w_out, gate2, final_g, final=final)


def kernel(x, c, ctx, c_ctx, attn_norm, attn_ada_w, attn_ada_b, attn_w_in, na_rpb, gqa_q_gain, gqa_k_gain, attn_w_out,
           rec_norm, rec_ada_w, rec_ada_b, rec_w_in, rwkv_mu, rwkv_w0, rwkv_w_up, rwkv_a0, rwkv_a_up, rwkv_k_k, rwkv_k_a,
           rwkv_r_k, rwkv_gn_w, rwkv_gn_b, hy_short, hy_w1, hy_b1, hy_w2, hy_b2, hy_w3, hy_b3, hy_skip, rec_w_out,
           final_norm):
    assert x.shape[0] == 1 and ctx.shape[1] == CTX_LEN and x.shape[2] == D_MODEL
    n = x.shape[1]
    assert n % ROW_TILE == 0 and n // ROW_TILE >= 3
    assert attn_w_in.shape[0] == rec_w_in.shape[0]
    d = D_MODEL
    cond8 = jnp.zeros((8, d), F32).at[0].set(c_ctx).at[1].set(c[0])
    m_attn = adaln_all(cond8, attn_ada_w, attn_ada_b)
    m_rec = adaln_all(cond8, rec_ada_w, rec_ada_b)
    mods = lambda m, i: tuple(m[i, :2, j * d:(j + 1) * d].reshape(2, 1, d) for j in range(3))
    cos_t, sin_t = _rope_tables(n)
    xs = jnp.concatenate([ctx[0], x[0]], axis=0)
    depth = attn_w_in.shape[0] + rec_w_in.shape[0]
    for layer in range(depth):
        i = layer // 2
        final = layer == depth - 1
        if layer % 2 == 0:
            xs = attn_layer(xs, mods(m_attn, i), attn_norm[i], attn_w_in[i], na_rpb[i], gqa_q_gain[i], gqa_k_gain[i],
                            attn_w_out[i], cos_t, sin_t)
        else:
            xs = rec_layer(xs, mods(m_rec, i), rec_norm[i], rec_w_in[i], rwkv_mu[i], rwkv_w0[i], rwkv_w_up[i],
                           rwkv_a0[i], rwkv_a_up[i], rwkv_k_k[i], rwkv_k_a[i], rwkv_r_k[i], rwkv_gn_w[i], rwkv_gn_b[i],
                           hy_short[i], hy_w1[i], hy_b1[i], hy_w2[i], hy_b2[i], hy_w3[i], hy_b3[i], hy_skip[i],
                           rec_w_out[i], final_norm, final)
    return xs[None]


def _hy_short_body(fh_ref, fl_ref, gh_ref, gl_ref, v_ref, x1_ref, x2_ref, taps_ref, scale_ref, skip_ref, o_ref):
    fh, fl, gh, gl = fh_ref[...], fl_ref[...], gh_ref[...], gl_ref[...]
    kp = fh.shape[0] // 2
    hw = HY_TAPS_W // 2
    tf = _dot3c(fh, fl, taps_ref[...])
    kre = (tf[:kp, :hw] + tf[:kp, hw:]) * scale_ref[...]
    kim = (tf[kp:, :hw] - tf[kp:, hw:]) * scale_ref[...]
    z = v_ref[...]
    for o, gate_ref in enumerate((x1_ref, x2_ref)):
        ls = slice(o * HY_WIDTH, (o + 1) * HY_WIDTH)
        zf = _dot3c(fh, fl, z)
        zre, zim = zf[:kp], zf[kp:]
        y = jnp.concatenate([zre * kre[:, ls] - zim * kim[:, ls], zre * kim[:, ls] + zim * kre[:, ls]], axis=0)
        z = gate_ref[...] * (_dot3c(gh, gl, y) + z * skip_ref[o:o + 1])
    o_ref[...] = z


def hyena_short(hv, hx1, hx2, taps, scale, skip):
    length = hv.shape[0]
    n = 2 * length
    k1 = length + 1
    kp = -(-k1 // 8) * 8
    kk = np.arange(kp)[:, None].astype(np.float64)
    live = kk < k1
    th = 2.0 * np.pi * kk * np.arange(length)[None, :] / n
    f = _split_const(np.concatenate([np.cos(th) * live, -np.sin(th) * live], axis=0))
    ck = np.where((kk == 0) | (kk == length), 1.0, 2.0) * live / n
    g = _split_const(np.concatenate([np.cos(th) * ck, -np.sin(th) * ck], axis=0).T)
    return pl.pallas_call(
        _hy_short_body,
        out_shape=jax.ShapeDtypeStruct((length, HY_WIDTH), F32),
        compiler_params=pltpu.CompilerParams(vmem_limit_bytes=VMEM_LIMIT),
        name="hyena_short",
    )(f[0], f[1], g[0], g[1], hv, hx1, hx2, taps, scale, skip)
```

```python
import functools
import math

import jax
import jax.numpy as jnp
import numpy as np
from jax import lax
from jax.experimental import pallas as pl
from jax.experimental.pallas import tpu as pltpu

F32 = jnp.float32
BF16 = jnp.bfloat16
HIGHEST = lax.Precision.HIGHEST

D_MODEL = 1024
GRID_W = 64
CTX_LEN = 256
HEAD_DIM = 64
BRANCH_W = 512
N_HEADS = 8
GQA_KV_W = 128
NA_WIN_ROWS = 8
NA_WIN_COLS = 16
ROPE_THETA = 10000.0
ROPE_FREQS = 16
NORM_EPS = 1e-6
ROW_TILE = 256
NA_GROUP_ROWS = 4
NEG_BIG = -1e30
LOG2E = math.log2(math.e)
QK_SCALE = HEAD_DIM ** -0.5 * LOG2E
VMEM_LIMIT = 56 * 1024 * 1024

ATTN_SPLITS = (512, 512, 512, 512, 512, 128, 128, 512)
GQA_HEAD_ORDER = (0, 4, 1, 5, 2, 6, 3, 7)


def _cparams(sem):
    return pltpu.CompilerParams(dimension_semantics=sem, vmem_limit_bytes=VMEM_LIMIT)


def _silu(v):
    return v * (1.0 / (1.0 + jnp.exp(-v)))


def _lane_half(shape):
    return (lax.broadcasted_iota(jnp.int32, shape, len(shape) - 1) // HEAD_DIM) % 2


def _dot_stat(a, block_mat):
    hi = a.astype(BF16)
    lo = (a - hi.astype(F32)).astype(BF16)
    return jnp.dot(hi, block_mat, preferred_element_type=F32) + jnp.dot(lo, block_mat, preferred_element_type=F32)


def _dot_nt(a, b):
    return lax.dot_general(a, b, (((1,), (1,)), ((), ())), preferred_element_type=F32)


def _adaln_body(cond_ref, w_ref, b_ref, o_ref):
    s = _silu(cond_ref[...])
    o_ref[0] = jnp.dot(s, w_ref[0], precision=HIGHEST, preferred_element_type=F32) + b_ref[0]


def adaln_all(cond8, ada_w, ada_b):
    nl = ada_w.shape[0]
    d = D_MODEL
    return pl.pallas_call(
        _adaln_body,
        out_shape=jax.ShapeDtypeStruct((nl, 8, 3 * d), F32),
        grid=(nl, 3),
        in_specs=[
            pl.BlockSpec((8, d), lambda l, j: (0, 0)),
            pl.BlockSpec((1, d, d), lambda l, j: (l, 0, j)),
            pl.BlockSpec((1, 1, d), lambda l, j: (l, 0, j)),
        ],
        out_specs=pl.BlockSpec((1, 8, d), lambda l, j: (l, 0, j)),
        compiler_params=_cparams(("parallel", "parallel")),
        name="adaln",
    )(cond8, ada_w, ada_b.reshape(nl, 1, 3 * d))


def _modulated(x_ref, g_ref, scale_ref, shift_ref):
    xf = x_ref[...]
    y = xf * lax.rsqrt(jnp.mean(xf * xf, axis=-1, keepdims=True) + NORM_EPS)
    return (y * g_ref[...]) * (1.0 + scale_ref[0]) + shift_ref[0]


def _mod_specs():
    d = D_MODEL
    return [
        pl.BlockSpec((ROW_TILE, d), lambda i: (i, 0)),
        pl.BlockSpec((1, d), lambda i: (0, 0)),
        pl.BlockSpec((1, 1, d), lambda i: (jnp.minimum(i, 1), 0, 0)),
        pl.BlockSpec((1, 1, d), lambda i: (jnp.minimum(i, 1), 0, 0)),
    ]


def _attn_in_body(x_ref, g_ref, scale_ref, shift_ref, w_ref, cos_ref, sin_ref, gq_ref, gqs_ref, gk_ref, gks_ref,
                  bdq_ref, bdk_ref,
                  qa_ref, ka_ref, va_ref, sga_ref, qb_ref, kb_ref, vb_ref, sgb_ref):
    xm = _modulated(x_ref, g_ref, scale_ref, shift_ref).astype(BF16)
    u = jnp.dot(xm, w_ref[...], preferred_element_type=F32)
    qa, ka, va, ga = u[:, 0:512], u[:, 512:1024], u[:, 1024:1536], u[:, 1536:2048]
    qb, kb, vb, gb = u[:, 2048:2560], u[:, 2560:2688], u[:, 2688:2816], u[:, 2816:3328]
    qbs, kbs = u[:, 3328:3840], u[:, 3840:3968]
    scale = QK_SCALE
    qa_ref[...] = (qa * scale).astype(BF16)
    ka_ref[...] = ka.astype(BF16)
    va_ref[...] = va.astype(BF16)
    sga_ref[...] = _silu(ga)
    sgb_ref[...] = _silu(gb)
    vb_ref[...] = jnp.transpose(vb).astype(BF16)
    cos_k, sin_k = cos_ref[...], sin_ref[...]
    cos_q = jnp.concatenate([cos_k] * 4, axis=1)
    sin_q = jnp.concatenate([sin_k] * 4, axis=1)
    rs_q = lax.rsqrt(_dot_stat(qb * qb, bdq_ref[...]) + NORM_EPS)
    rs_k = lax.rsqrt(_dot_stat(kb * kb, bdk_ref[...]) + NORM_EPS)
    qr = rs_q * (qb * gq_ref[...] * cos_q + qbs * gqs_ref[...] * sin_q)
    kr = rs_k * (kb * gk_ref[...] * cos_k + kbs * gks_ref[...] * sin_k)
    qb_ref[...] = jnp.transpose(qr * scale).astype(BF16)
    kb_ref[...] = kr.astype(BF16)


def _rope_tables(n):
    t = jnp.arange(n, dtype=jnp.int32)
    pos = jnp.stack([t // GRID_W, t % GRID_W], axis=-1).astype(F32)
    inv_freq = ROPE_THETA ** (-jnp.arange(ROPE_FREQS, dtype=F32) / ROPE_FREQS)
    ang = pos[:, :, None] * inv_freq
    c, s = jnp.cos(ang), jnp.sin(ang)
    cos64 = jnp.concatenate([c[:, 0], c[:, 0], c[:, 1], c[:, 1]], axis=-1)
    sin64 = jnp.concatenate([-s[:, 0], s[:, 0], -s[:, 1], s[:, 1]], axis=-1)
    cos64 = jnp.concatenate([jnp.ones((CTX_LEN, HEAD_DIM), F32), cos64], axis=0)
    sin64 = jnp.concatenate([jnp.zeros((CTX_LEN, HEAD_DIM), F32), sin64], axis=0)
    return jnp.tile(cos64, (1, 2)), jnp.tile(sin64, (1, 2))


def _reorder_heads(w, order, axis):
    take = lambda h: lax.slice_in_dim(w, h * HEAD_DIM, (h + 1) * HEAD_DIM, axis=axis)
    return jnp.concatenate([take(h) for h in order], axis=axis)


def _swap_rope_halves(w):
    shp = w.shape
    return jnp.flip(w.reshape(shp[:-1] + (shp[-1] // (2 * ROPE_FREQS), 2, ROPE_FREQS)), axis=-2).reshape(shp)


def attn_in_proj(xs, norm_g, scale2, shift2, w_in, q_gain, k_gain, cos_t, sin_t):
    r = xs.shape[0]
    d = D_MODEL
    parts, start = [], 0
    for s in ATTN_SPLITS:
        parts.append(w_in[:, start:start + s])
        start += s
    wqa, wka, wva, wga, wqb, wkb, wvb, wgb = parts
    wqb_p = _reorder_heads(wqb, GQA_HEAD_ORDER, 1)
    wgb_p = _reorder_heads(wgb, GQA_HEAD_ORDER, 1)
    wqb_sw = _swap_rope_halves(wqb_p)
    wkb_sw = _swap_rope_halves(wkb)
    w_ext = jnp.concatenate([wqa, wka, wva, wga, wqb_p, wkb, wvb, wgb_p, wqb_sw, wkb_sw], axis=1).astype(BF16)
    gq = jnp.tile(q_gain, N_HEADS)[None]
    gqs = jnp.tile(_swap_rope_halves(q_gain), N_HEADS)[None]
    gk = jnp.tile(k_gain, 2)[None]
    gks = jnp.tile(_swap_rope_halves(k_gain), 2)[None]
    bdq = jnp.asarray(np.kron(np.eye(N_HEADS), np.full((HEAD_DIM, HEAD_DIM), 1.0 / HEAD_DIM)), BF16)
    bdk = jnp.asarray(np.kron(np.eye(2), np.full((HEAD_DIM, HEAD_DIM), 1.0 / HEAD_DIM)), BF16)
    wcols = w_ext.shape[1]
    const = lambda shp: pl.BlockSpec(shp, lambda i: (0,) * len(shp))
    rows = lambda w: pl.BlockSpec((ROW_TILE, w), lambda i: (i, 0))
    out_shapes = [
        jax.ShapeDtypeStruct((r, 512), BF16), jax.ShapeDtypeStruct((r, 512), BF16), jax.ShapeDtypeStruct((r, 512), BF16),
        jax.ShapeDtypeStruct((r, 512), F32),
        jax.ShapeDtypeStruct((512, r), BF16), jax.ShapeDtypeStruct((r, 128), BF16), jax.ShapeDtypeStruct((128, r), BF16),
        jax.ShapeDtypeStruct((r, 512), F32),
    ]
    cols = lambda w: pl.BlockSpec((w, ROW_TILE), lambda i: (0, i))
    return pl.pallas_call(
        _attn_in_body,
        out_shape=out_shapes,
        grid=(r // ROW_TILE,),
        in_specs=_mod_specs() + [const((d, wcols)), rows(128), rows(128), const((1, 512)), const((1, 512)),
                                 const((1, 128)), const((1, 128)), const((512, 512)), const((128, 128))],
        out_specs=[rows(512), rows(512), rows(512), rows(512), cols(512), rows(128), cols(128), rows(512)],
        compiler_params=_cparams(("parallel",)),
        name="attn_in_proj",
    )(xs, norm_g[None], scale2, shift2, w_ext, cos_t, sin_t, gq, gqs, gk, gks, bdq, bdk)


def _na_cols_body(rpb_ref, sel_ref, neg_ref, o_ref):
    o_ref[...] = jnp.dot(rpb_ref[...], sel_ref[...], precision=HIGHEST, preferred_element_type=F32) + neg_ref[...]


def _na_bias_tables(rpb, rows):
    nrel_r, nrel_c = 2 * NA_WIN_ROWS - 1, 2 * NA_WIN_COLS - 1
    qc = np.arange(GRID_W)[:, None]
    kc = np.arange(GRID_W)[None, :]
    col0 = np.clip(qc - NA_WIN_COLS // 2, 0, GRID_W - NA_WIN_COLS)
    col_ok = (kc >= col0) & (kc < col0 + NA_WIN_COLS)
    rc = kc - qc + NA_WIN_COLS - 1
    sel = np.zeros((128, GRID_W * GRID_W), np.float32)
    sel[np.where(col_ok, rc, 127).reshape(-1), np.arange(GRID_W * GRID_W)] = col_ok.reshape(-1)
    neg = np.where(col_ok, 0.0, NEG_BIG).astype(np.float32).reshape(1, -1)
    rpb2 = jnp.zeros((128, 128), F32).at[:N_HEADS * nrel_r, :nrel_c].set(rpb.reshape(N_HEADS * nrel_r, nrel_c))
    cols = pl.pallas_call(
        _na_cols_body,
        out_shape=jax.ShapeDtypeStruct((128, GRID_W * GRID_W), F32),
        name="na_bias_cols",
    )(rpb2, jnp.asarray(sel), jnp.asarray(neg))
    cols = cols[:N_HEADS * nrel_r].reshape(N_HEADS, nrel_r, GRID_W, GRID_W)
    kh = min(NA_WIN_ROWS, rows)
    g = rows // NA_GROUP_ROWS
    cases = [(0, 0), (NA_GROUP_ROWS, 0), (rows - NA_GROUP_ROWS, NA_GROUP_ROWS * (g - 3))]
    masked = jnp.full((N_HEADS, GRID_W, GRID_W), NEG_BIG, F32)
    tabs = []
    for qr_first, start in cases:
        blocks = []
        for j in range(NA_GROUP_ROWS):
            qr = qr_first + j
            row0 = min(max(qr - kh // 2, 0), rows - kh)
            for i in range(3 * NA_GROUP_ROWS):
                kr = start + i
                blocks.append(cols[:, kr - qr + NA_WIN_ROWS - 1] if row0 <= kr < row0 + kh else masked)
        tab = jnp.stack(blocks, axis=1).reshape(N_HEADS, NA_GROUP_ROWS, 3 * NA_GROUP_ROWS, GRID_W, GRID_W)
        tabs.append(tab.transpose(0, 1, 3, 2, 4).reshape(N_HEADS, ROW_TILE, 3 * ROW_TILE))
    return jnp.stack(tabs) * LOG2E


def _na_body(q_ref, kc_ref, k0_ref, k1_ref, k2_ref, vc_ref, v0_ref, v1_ref, v2_ref, bias_ref, sg_ref, o_ref):
    half = _lane_half((ROW_TILE, 128))
    k_refs = (k0_ref, k1_ref, k2_ref, kc_ref)
    v_refs = (v0_ref, v1_ref, v2_ref, vc_ref)

    def scores(h):
        hp, j = divmod(h, 2)
        ls = slice(hp * 128, (hp + 1) * 128)
        qp = q_ref[:, ls]
        qm = jnp.where(half == j, qp, jnp.zeros_like(qp))
        return [_dot_nt(qm, r[:, ls]) for r in k_refs]

    s_next = scores(0)
    outs = []
    for h in range(N_HEADS):
        hp, j = divmod(h, 2)
        ls = slice(hp * 128, (hp + 1) * 128)
        s = s_next
        if h + 1 < N_HEADS:
            s_next = scores(h + 1)
        s_win = jnp.concatenate(s[:3], axis=1) + bias_ref[0, h]
        s_ctx = s[3]
        m = jnp.maximum(jnp.max(s_win, axis=1, keepdims=True), jnp.max(s_ctx, axis=1, keepdims=True))
        p_win = jnp.exp2(s_win - m).astype(BF16)
        p_ctx = jnp.exp2(s_ctx - m).astype(BF16)
        vms = [jnp.where(half == j, r[:, ls], jnp.ones((ROW_TILE, 128), BF16)) for r in v_refs]
        o = jnp.dot(p_ctx, vms[3], preferred_element_type=F32)
        for b in range(3):
            o += jnp.dot(p_win[:, b * ROW_TILE:(b + 1) * ROW_TILE], vms[b], preferred_element_type=F32)
        outs.append(o / pltpu.roll(o, HEAD_DIM, 1))
        if j == 1:
            o_pair = jnp.where(half == 0, outs[h - 1], outs[h])
            o_ref[:, ls] = (o_pair * sg_ref[:, ls]).astype(BF16)


def na_attention(qa, ka, va, sga, bias_tabs, n):
    g = n // ROW_TILE
    w = BRANCH_W

    def kv_spec(off):
        return pl.BlockSpec((ROW_TILE, w), lambda i: (jnp.clip(i - 1, 0, g - 3) + off + 1, 0))

    ctx_spec = pl.BlockSpec((ROW_TILE, w), lambda i: (0, 0))
    q_spec = pl.BlockSpec((ROW_TILE, w), lambda i: (i + 1, 0))
    case = lambda i: jnp.where(i == 0, 0, jnp.where(i == g - 1, 2, 1))
    bias_spec = pl.BlockSpec((1, N_HEADS, ROW_TILE, 3 * ROW_TILE), lambda i: (case(i), 0, 0, 0))
    return pl.pallas_call(
        _na_body,
        out_shape=jax.ShapeDtypeStruct((n, w), BF16),
        grid=(g,),
        in_specs=[q_spec, ctx_spec, kv_spec(0), kv_spec(1), kv_spec(2), ctx_spec, kv_spec(0), kv_spec(1), kv_spec(2),
                  bias_spec, q_spec],
        out_specs=pl.BlockSpec((ROW_TILE, w), lambda i: (i, 0)),
        compiler_params=_cparams(("parallel",)),
        name="na_attention",
    )(qa, ka, ka, ka, ka, va, va, va, va, bias_tabs, sga)


def _flash_mha_body(q_ref, k_ref, v_ref, sg_ref, o_ref, m_ref, acc_ref):
    kv = pl.program_id(1)
    tq = q_ref.shape[0]

    @pl.when(kv == 0)
    def _():
        m_ref[...] = jnp.full(m_ref.shape, NEG_BIG, F32)
        acc_ref[...] = jnp.zeros(acc_ref.shape, F32)

    khalf = _lane_half((k_ref.shape[0], 128))
    for p in range(N_HEADS // 2):
        ls = slice(p * 128, (p + 1) * 128)
        qp = q_ref[:, ls]
        kp = k_ref[:, ls]
        vp = v_ref[:, ls]
        for j in range(2):
            hh = 2 * p + j
            km = jnp.where(khalf == j, kp, jnp.zeros_like(kp))
            vm = jnp.where(khalf == j, vp, jnp.ones_like(vp))
            s = _dot_nt(qp, km)
            m_prev = m_ref[hh]
            m_new = jnp.maximum(m_prev, jnp.max(s, axis=1, keepdims=True))
            alpha = jnp.exp2(m_prev - m_new)
            pr = jnp.exp2(s - m_new[:, :1]).astype(BF16)
            acc_ref[hh] = alpha * acc_ref[hh] + jnp.dot(pr, vm, preferred_element_type=F32)
            m_ref[hh] = m_new

    @pl.when(kv == pl.num_programs(1) - 1)
    def _():
        half = _lane_half((tq, 128))
        for p in range(N_HEADS // 2):
            ls = slice(p * 128, (p + 1) * 128)
            a0, a1 = acc_ref[2 * p], acc_ref[2 * p + 1]
            o0 = a0 / pltpu.roll(a0, HEAD_DIM, 1)
            o1 = a1 / pltpu.roll(a1, HEAD_DIM, 1)
            o_ref[:, ls] = (jnp.where(half == 0, o0, o1) * sg_ref[:, ls]).astype(BF16)


def flash_mha(q, k, v, sg, *, q_block0, nq, tk, nk):
    tq = ROW_TILE
    return pl.pallas_call(
        _flash_mha_body,
        out_shape=jax.ShapeDtypeStruct((nq * tq, BRANCH_W), BF16),
        grid=(nq, nk),
        in_specs=[
            pl.BlockSpec((tq, BRANCH_W), lambda i, j: (i + q_block0, 0)),
            pl.BlockSpec((tk, BRANCH_W), lambda i, j: (j, 0)),
            pl.BlockSpec((tk, BRANCH_W), lambda i, j: (j, 0)),
            pl.BlockSpec((tq, BRANCH_W), lambda i, j: (i + q_block0, 0)),
        ],
        out_specs=pl.BlockSpec((tq, BRANCH_W), lambda i, j: (i, 0)),
        scratch_shapes=[pltpu.VMEM((N_HEADS, tq, 128), F32)] * 2,
        compiler_params=_cparams(("parallel", "arbitrary")),
        name="flash_mha",
    )(q, k, v, sg)


def _flash_gqa_body(qt_ref, k_ref, vt_ref, sg_ref, o_ref, m_ref, acc_ref):
    kv = pl.program_id(1)
    tq = qt_ref.shape[1]
    tk = k_ref.shape[0]

    @pl.when(kv == 0)
    def _():
        m_ref[...] = jnp.full(m_ref.shape, NEG_BIG, F32)
        acc_ref[...] = jnp.zeros(acc_ref.shape, F32)

    khalf = _lane_half((tk, 128))
    vhalf = lax.broadcasted_iota(jnp.int32, (128, tk), 0) // HEAD_DIM
    kb = k_ref[...]
    vt = vt_ref[...]
    kms = [jnp.where(khalf == j, kb, jnp.zeros_like(kb)) for j in range(2)]
    vms = [jnp.where(vhalf == j, vt, jnp.ones_like(vt)) for j in range(2)]

    def scores(hh):
        p, j = divmod(hh, 2)
        return jnp.dot(kms[j], qt_ref[p * 128:(p + 1) * 128, :], preferred_element_type=F32).astype(BF16)

    st_next = scores(0)
    for hh in range(N_HEADS):
        st = st_next
        if hh + 1 < N_HEADS:
            st_next = scores(hh + 1)
        m_prev = m_ref[hh]
        m_new = jnp.maximum(m_prev, jnp.max(st, axis=0, keepdims=True).astype(F32))
        alpha = jnp.exp2(m_prev - m_new)
        pt = jnp.exp2(st - m_new[0:1].astype(BF16))
        acc_ref[hh] = alpha[0:1] * acc_ref[hh] + jnp.dot(vms[hh % 2], pt, preferred_element_type=F32)
        m_ref[hh] = m_new

    @pl.when(kv == pl.num_programs(1) - 1)
    def _():
        for p in range(N_HEADS // 2):
            ls = slice(p * 128, (p + 1) * 128)
            a0, a1 = acc_ref[2 * p], acc_ref[2 * p + 1]
            ot = jnp.concatenate([a0[:HEAD_DIM] / a0[HEAD_DIM:], a1[HEAD_DIM:] / a1[:HEAD_DIM]], axis=0)
            o_ref[:, ls] = (jnp.transpose(ot) * sg_ref[:, ls]).astype(BF16)


def flash_gqa(qt, k, vt, sg, *, q_block0, nq, tk, nk):
    tq = ROW_TILE
    return pl.pallas_call(
        _flash_gqa_body,
        out_shape=jax.ShapeDtypeStruct((nq * tq, BRANCH_W), BF16),
        grid=(nq, nk),
        in_specs=[
            pl.BlockSpec((BRANCH_W, tq), lambda i, j: (0, i + q_block0)),
            pl.BlockSpec((tk, GQA_KV_W), lambda i, j: (j, 0)),
            pl.BlockSpec((GQA_KV_W, tk), lambda i, j: (0, j)),
            pl.BlockSpec((tq, BRANCH_W), lambda i, j: (i + q_block0, 0)),
        ],
        out_specs=pl.BlockSpec((tq, BRANCH_W), lambda i, j: (i, 0)),
        scratch_shapes=[pltpu.VMEM((N_HEADS, 8, tq), F32), pltpu.VMEM((N_HEADS, 128, tq), F32)],
        compiler_params=_cparams(("parallel", "arbitrary")),
        name="flash_gqa",
    )(qt, k, vt, sg)


def _out_body(x_ref, ya_ref, yac_ref, yb_ref, ybc_ref, wa_ref, wb_ref, gate_ref, o_ref):
    is_ctx = pl.program_id(0) == 0
    ya = jnp.where(is_ctx, yac_ref[...], ya_ref[...])
    yb = jnp.where(is_ctx, ybc_ref[...], yb_ref[...])
    y = jnp.dot(ya, wa_ref[...], preferred_element_type=F32)
    y += jnp.dot(yb, wb_ref[...], preferred_element_type=F32)
    o_ref[...] = x_ref[...] + gate_ref[0] * y


def out_proj(xs, ya_lat, ya_ctx, yb_lat, yb_ctx, wa, wb, gate2):
    r = xs.shape[0]
    d = D_MODEL
    rows = lambda w: pl.BlockSpec((ROW_TILE, w), lambda i: (i, 0))
    lat = pl.BlockSpec((ROW_TILE, BRANCH_W), lambda i: (jnp.maximum(i - 1, 0), 0))
    const = lambda shp: pl.BlockSpec(shp, lambda i: (0,) * len(shp))
    ctx = const((CTX_LEN, BRANCH_W))
    return pl.pallas_call(
        _out_body,
        out_shape=jax.ShapeDtypeStruct((r, d), F32),
        grid=(r // ROW_TILE,),
        in_specs=[rows(d), lat, ctx, lat, ctx, const((BRANCH_W, d)), const((BRANCH_W, d)),
                  pl.BlockSpec((1, 1, d), lambda i: (jnp.minimum(i, 1), 0, 0))],
        out_specs=rows(d),
        compiler_params=_cparams(("parallel",)),
        name="out_proj",
    )(xs, ya_lat, ya_ctx, yb_lat, yb_ctx, wa.astype(BF16), wb.astype(BF16), gate2)


KV_TILE_MAX_BLOCKS = 13


def _kv_tile(r):
    nb = r // ROW_TILE
    best = max(k for k in range(1, KV_TILE_MAX_BLOCKS + 1) if nb % k == 0)
    return best * ROW_TILE, nb // best


def attn_layer(xs, mods, norm_g, w_in, rpb, q_gain, k_gain, w_out, cos_t, sin_t):
    r = xs.shape[0]
    n = r - CTX_LEN
    shift2, scale2, gate2 = mods
    qa, ka, va, sga, qbt, kb, vbt, sgb = attn_in_proj(xs, norm_g, scale2, shift2, w_in, q_gain, k_gain, cos_t, sin_t)
    bias_tabs = _na_bias_tables(rpb, n // GRID_W)
    ya_lat = na_attention(qa, ka, va, sga, bias_tabs, n)
    ya_ctx = flash_mha(qa, ka, va, sga, q_block0=0, nq=1, tk=CTX_LEN, nk=1)
    tk, nk = _kv_tile(r)
    yb_lat = flash_gqa(qbt, kb, vbt, sgb, q_block0=1, nq=n // ROW_TILE, tk=tk, nk=nk)
    yb_ctx = flash_gqa(qbt, kb, vbt, sgb, q_block0=0, nq=1, tk=CTX_LEN, nk=1)
    wb = _reorder_heads(w_out[BRANCH_W:], GQA_HEAD_ORDER, 0)
    return out_proj(xs, ya_lat, ya_ctx, yb_lat, yb_ctx, w_out[:BRANCH_W], wb, gate2)


def _split(a):
    hi = a.astype(BF16)
    return hi, (a - hi.astype(F32)).astype(BF16)


def _dot3(a, b, dims=(((1,), (0,)), ((), ()))):
    ah, al = _split(a)
    bh, bl = _split(b)
    dg = functools.partial(lax.dot_general, dimension_numbers=dims, preferred_element_type=F32)
    return dg(ah, bh) + (dg(al, bh) + dg(ah, bl))


def _dot1(a, b, dims=(((1,), (0,)), ((), ()))):
    return lax.dot_general(a.astype(BF16), b.astype(BF16), dims, preferred_element_type=F32)


_NT = (((1,), (1,)), ((), ()))
_TN = (((0,), (0,)), ((), ()))


RWKV_SHIFT_W = 1664
HY_IN_W = 1536
HALO = 8
REC_HALO_W = RWKV_SHIFT_W + HY_IN_W


def _rec_in_body(x_ref, xp_ref, xn_ref, g_ref, scale_ref, shift_ref, w_ref, mu_ref, taps_ref,
                 rw_ref, hv_ref, hx1_ref, hx2_ref, sgr_ref, sgh_ref, u_scr):
    i = pl.program_id(0)
    nt = pl.num_programs(0)
    xe = jnp.concatenate([xp_ref[...], x_ref[...], xn_ref[...]], axis=0)
    y = xe * lax.rsqrt(jnp.mean(xe * xe, axis=-1, keepdims=True) + NORM_EPS)
    xm = ((y * g_ref[...]) * (1.0 + scale_ref[0]) + shift_ref[0]).astype(BF16)
    u = jnp.dot(xm, w_ref[...], preferred_element_type=F32)
    row = lax.broadcasted_iota(jnp.int32, (ROW_TILE + 2 * HALO, 1), 0)
    keep = jnp.logical_and(jnp.logical_or(row >= HALO, i >= 2),
                           jnp.logical_or(row < ROW_TILE + HALO, jnp.logical_and(i >= 1, i < nt - 1)))
    u_scr[...] = jnp.where(keep, u[:, :REC_HALO_W], 0.0)
    up = u_scr[pl.ds(HALO - 1, ROW_TILE), :]
    uc = u_scr[pl.ds(HALO, ROW_TILE), :]
    un = u_scr[pl.ds(HALO + 1, ROW_TILE), :]
    w = RWKV_SHIFT_W
    rw_c = uc[:, :w]
    rw_ref[...] = rw_c + (0.5 * (up[:, :w] + un[:, :w]) - rw_c) * mu_ref[...]
    hy = up[:, w:] * taps_ref[0:1] + uc[:, w:] * taps_ref[1:2] + un[:, w:] * taps_ref[2:3]
    hv_ref[...] = hy[:, 0:512]
    hx1_ref[...] = hy[:, 512:1024]
    hx2_ref[...] = hy[:, 1024:1536]
    uc_all = u[HALO:HALO + ROW_TILE]
    sgr_ref[...] = _silu(uc_all[:, REC_HALO_W:REC_HALO_W + 512])
    sgh_ref[...] = _silu(uc_all[:, REC_HALO_W + 512:REC_HALO_W + 1024])


def rec_in_proj(xs, norm_g, scale2, shift2, w_in, mu, hy_short):
    r = xs.shape[0]
    d = D_MODEL
    w = RWKV_SHIFT_W
    w_ext = jnp.concatenate([w_in[:, :w], w_in[:, w + 512:w + 512 + HY_IN_W], w_in[:, w:w + 512],
                             w_in[:, w + 512 + HY_IN_W:]], axis=1).astype(BF16)
    nh = r // HALO
    per = ROW_TILE // HALO
    const = lambda shp: pl.BlockSpec(shp, lambda i: (0,) * len(shp))
    rows = lambda wd: pl.BlockSpec((ROW_TILE, wd), lambda i: (i, 0))
    f = lambda wd: jax.ShapeDtypeStruct((r, wd), F32)
    mod = _mod_specs()
    return pl.pallas_call(
        _rec_in_body,
        out_shape=[f(w), f(512), f(512), f(512), f(512), f(512)],
        grid=(r // ROW_TILE,),
        in_specs=[mod[0],
                  pl.BlockSpec((HALO, d), lambda i: (jnp.maximum(i * per - 1, 0), 0)),
                  pl.BlockSpec((HALO, d), lambda i: (jnp.minimum((i + 1) * per, nh - 1), 0)),
                  mod[1], mod[2], mod[3], const((d, w_ext.shape[1])), const((1, w)), const((3, HY_IN_W))],
        out_specs=[rows(w), rows(512), rows(512), rows(512), rows(512), rows(512)],
        scratch_shapes=[pltpu.VMEM((ROW_TILE + 2 * HALO, REC_HALO_W), F32)],
        compiler_params=_cparams(("parallel",)),
        name="rec_in_proj",
    )(xs, xs, xs, norm_g[None], scale2, shift2, w_ext, mu[None], hy_short)


CHUNK = 64
CPT = ROW_TILE // CHUNK


def _block_sum_mat(width, value):
    return jnp.asarray(np.kron(np.eye(width // HEAD_DIM), np.full((HEAD_DIM, HEAD_DIM), value)), F32)


def _rwkv_prep_body(r_ref, k_ref, v_ref, lora_ref, w0_ref, wup_ref, a0_ref, aup_ref, kk_ref, ka_ref, rk_ref,
                    tri_ref, bs_ref, g_ref, add_ref, bonus_ref):
    t = ROW_TILE
    r, k, v, lora = r_ref[...], k_ref[...], v_ref[...], lora_ref[...]
    bs = bs_ref[...]
    kk = k * kk_ref[...]
    kk = kk * lax.rsqrt(_dot_stat(kk * kk, bs) + 1e-12)
    row = lax.broadcasted_iota(jnp.int32, (t, t), 0)
    col = lax.broadcasted_iota(jnp.int32, (t, t), 1)
    same = (row // CHUNK) == (col // CHUNK)
    eye = (row == col).astype(F32)
    wl_both = _dot3(jnp.tanh(lora), wup_ref[...])
    al_both = _dot3(lora, aup_ref[...])
    half = _lane_half((t, 128))
    half_c = _lane_half((HEAD_DIM, 128))
    rowc = lax.broadcasted_iota(jnp.int32, (HEAD_DIM, 128), 0)
    lanec = lax.broadcasted_iota(jnp.int32, (HEAD_DIM, 128), 1)
    level_masks = []
    bsz = 2
    while bsz < CHUNK:
        level_masks.append(jnp.logical_and((row // (2 * bsz)) == (col // (2 * bsz)), (row // bsz) != (col // bsz)))
        bsz *= 2
    first_mask = (row // 2) == (col // 2)

    dirs = []
    for d in range(2):
        ds = slice(d * 128, (d + 1) * 128)
        wl = w0_ref[d] + wl_both[:, ds]
        z = -wl
        w_log = -(jnp.maximum(z, 0.0) + jnp.log(1.0 + jnp.exp(-jnp.abs(z)))) - 0.5
        lw = -jnp.exp(w_log)
        a = 1.0 / (1.0 + jnp.exp(-(a0_ref[d] + al_both[:, ds])))
        kd = k * (1.0 + (a - 1.0) * ka_ref[...])
        b = kk * a
        incl = jnp.logical_and(same, (col <= row) if d == 0 else (col >= row))
        lw_hi, lw_lo = _split(lw)
        tri = tri_ref[d]
        cs = jnp.dot(tri, lw_hi, preferred_element_type=F32) + jnp.dot(tri, lw_lo, preferred_element_type=F32)
        ends = [c * CHUNK + (CHUNK - 1 if d == 0 else 0) for c in range(CPT)]
        tot = jnp.concatenate([jnp.broadcast_to(cs[e:e + 1], (CHUNK, 128)) for e in ends], axis=0)
        w_inv = jnp.exp(-cs)
        w_rest = jnp.exp(tot - cs)
        dirs.append(dict(kd=kd, incl=incl, strict=jnp.logical_and(incl, row != col), tot=tot,
                         kkt=kk * jnp.exp(cs - lw), kh=kd * w_inv, bh=b * w_inv, rt=r * jnp.exp(cs),
                         kdd=kd * w_rest, bdd=b * w_rest))
    chains = [(dd, j) for dd in dirs for j in range(2)]
    sels = [half == j for _, j in chains]
    bms = [jnp.where(sel, dd["bh"], 0.0) for (dd, _), sel in zip(chains, sels)]
    kms = [jnp.where(sel, dd["kh"], 0.0) for (dd, _), sel in zip(chains, sels)]
    l_bs = [jnp.where(dd["strict"], _dot1(dd["kkt"], bm, _NT), 0.0) for (dd, _), bm in zip(chains, bms)]
    tinvs = [eye - jnp.where(first_mask, l_b, 0.0) for l_b in l_bs]
    for mask in level_masks:
        xs = [_dot1(jnp.where(mask, l_b, 0.0), tinv) for l_b, tinv in zip(l_bs, tinvs)]
        tinvs = [tinv - _dot1(tinv, x) for tinv, x in zip(tinvs, xs)]
    l_ks = [jnp.where(dd["strict"], _dot1(dd["kkt"], km, _NT), 0.0) for (dd, _), km in zip(chains, kms)]
    a_rks = [jnp.where(dd["incl"], _dot1(dd["rt"], km, _NT), 0.0) for (dd, _), km in zip(chains, kms)]
    a_rbs = [jnp.where(dd["incl"], _dot1(dd["rt"], bm, _NT), 0.0) for (dd, _), bm in zip(chains, bms)]
    lvs = [_dot1(l_k, v) for l_k in l_ks]
    pus = [_dot1(tinv, jnp.concatenate([dd["kkt"], lv], axis=1)) for (dd, _), tinv, lv in zip(chains, tinvs, lvs)]
    cors = [_dot1(a_rb, pu) for a_rb, pu in zip(a_rbs, pus)]
    ps = [pu[:, :128] for pu in pus]
    u0s = [pu[:, 128:] for pu in pus]
    qs = [dd["rt"] - cor[:, :128] for (dd, _), cor in zip(chains, cors)]
    y0s = [_dot1(a_rk, v) - cor[:, 128:] for a_rk, cor in zip(a_rks, cors)]

    sel0 = half == 0
    for d, dd in enumerate(dirs):
        p, u0, q, y0 = (jnp.where(sel0, x[2 * d], x[2 * d + 1]) for x in (ps, u0s, qs, y0s))
        for c in range(CPT):
            rs = slice(c * CHUNK, (c + 1) * CHUNK)
            x1 = _dot1(dd["bdd"][rs], p[rs], _TN)
            x2 = _dot1(dd["kdd"][rs], v[rs], _TN) - _dot1(dd["bdd"][rs], u0[rs], _TN)
            m_pair = jnp.where(half_c == 0, x1[:HEAD_DIM], x1[HEAD_DIM:])
            n_pair = jnp.where(half_c == 0, x2[:HEAD_DIM], x2[HEAD_DIM:])
            wc = jnp.exp(dd["tot"][c * CHUNK:c * CHUNK + 1])
            dg = jnp.where((lanec % HEAD_DIM) == rowc, wc, 0.0)
            g_ref[c, d, 0:HEAD_DIM, :] = dg - m_pair
            g_ref[c, d, HEAD_DIM:, :] = q[rs]
            add_ref[c, d, 0:HEAD_DIM, :] = n_pair
            add_ref[c, d, HEAD_DIM:, :] = y0[rs]

    kd_sum = dirs[0]["kd"] + dirs[1]["kd"]
    bonus_ref[...] = 0.5 * _dot_stat(r * kd_sum * rk_ref[...], bs) * v


def _lora_ext(up, first_row):
    out = jnp.zeros((2, 128, BRANCH_W), F32)
    for d in range(2):
        out = out.at[d, first_row + 32 * d:first_row + 32 * (d + 1)].set(up[d])
    return out


def rwkv_prep(rw, w0, w_up, a0, a_up, k_k, k_a, r_k):
    r = rw.shape[0]
    nt = r // ROW_TILE
    nch = r // CHUNK
    t = ROW_TILE
    ii = np.arange(t)
    same = (ii[:, None] // CHUNK) == (ii[None, :] // CHUNK)
    tri = jnp.asarray(np.stack([same & (ii[None, :] <= ii[:, None]), same & (ii[None, :] >= ii[:, None])]), BF16)
    pair_cat = lambda w: w.reshape(2, 128, N_HEADS // 2, 128).transpose(1, 2, 0, 3).reshape(128, 2 * BRANCH_W)
    lane = lambda blk: pl.BlockSpec((t, 128), lambda i, p, blk=blk: (i, blk + p))
    pvec = pl.BlockSpec((1, 128), lambda i, p: (0, p))
    dvec = pl.BlockSpec((2, 1, 128), lambda i, p: (0, 0, p))
    dmat = pl.BlockSpec((128, 256), lambda i, p: (0, p))
    gspec = pl.BlockSpec((CPT, 2, HEAD_DIM + CHUNK, 128), lambda i, p: (i, 0, 0, p))
    gshape = jax.ShapeDtypeStruct((nch, 2, HEAD_DIM + CHUNK, BRANCH_W), F32)
    return pl.pallas_call(
        _rwkv_prep_body,
        out_shape=[gshape, gshape, jax.ShapeDtypeStruct((r, BRANCH_W), F32)],
        grid=(nt, N_HEADS // 2),
        in_specs=[lane(0), lane(4), lane(8), pl.BlockSpec((t, 128), lambda i, p: (i, 12)),
                  dvec, dmat, dvec, dmat, pvec, pvec, pvec,
                  pl.BlockSpec((2, t, t), lambda i, p: (0, 0, 0)),
                  pl.BlockSpec((128, 128), lambda i, p: (0, 0))],
        out_specs=[gspec, gspec, pl.BlockSpec((t, 128), lambda i, p: (i, p))],
        compiler_params=_cparams(("parallel", "parallel")),
        name="rwkv_prep",
    )(rw, rw, rw, rw, w0.reshape(2, 1, BRANCH_W), pair_cat(_lora_ext(w_up, 0)), a0.reshape(2, 1, BRANCH_W),
      pair_cat(_lora_ext(a_up, 64)), k_k[None], k_a[None], r_k.reshape(1, BRANCH_W), tri,
      _block_sum_mat(128, 1.0).astype(BF16))


def _rwkv_scan_body(gf_ref, af_ref, gb_ref, ab_ref, yf_ref, yb_ref, st_ref):
    @pl.when(pl.program_id(0) == 0)
    def _():
        st_ref[...] = jnp.zeros(st_ref.shape, F32)

    rowh = lax.broadcasted_iota(jnp.int32, (128, 128), 0) // HEAD_DIM
    diag = rowh == _lane_half((128, 128))
    for s in range(CPT):
        for d, (g_ref, a_ref, y_ref, c) in enumerate(((gf_ref, af_ref, yf_ref, s), (gb_ref, ab_ref, yb_ref, CPT - 1 - s))):
            for p in range(N_HEADS // 2):
                ls = slice(p * 128, (p + 1) * 128)
                out = _dot3(g_ref[c, 0, :, ls], st_ref[d, p]) + a_ref[c, 0, :, ls]
                hn = out[:HEAD_DIM]
                st_ref[d, p] = jnp.where(diag, jnp.concatenate([hn, hn], axis=0), 0.0)
                y_ref[c * CHUNK:(c + 1) * CHUNK, ls] = out[HEAD_DIM:]


def rwkv_scan(g, add):
    nch = g.shape[0]
    r = nch * CHUNK
    nt = r // ROW_TILE
    assert CTX_LEN == ROW_TILE
    rev = lambda i: jnp.where(i == 0, 0, nt - i)
    blk = (CPT, 1, HEAD_DIM + CHUNK, BRANCH_W)
    fwd = pl.BlockSpec(blk, lambda i: (i, 0, 0, 0))
    bwd = pl.BlockSpec(blk, lambda i: (rev(i), 1, 0, 0))
    yshape = jax.ShapeDtypeStruct((r, BRANCH_W), F32)
    return pl.pallas_call(
        _rwkv_scan_body,
        out_shape=[yshape, yshape],
        grid=(nt,),
        in_specs=[fwd, fwd, bwd, bwd],
        out_specs=[pl.BlockSpec((ROW_TILE, BRANCH_W), lambda i: (i, 0)),
                   pl.BlockSpec((ROW_TILE, BRANCH_W), lambda i: (rev(i), 0))],
        scratch_shapes=[pltpu.VMEM((2, N_HEADS // 2, 128, 128), F32)],
        compiler_params=_cparams(("arbitrary",)),
        name="rwkv_scan",
    )(g, add, g, add)


HY_WIDTH = 512
HY_ORDER = 2
HY_POS_BANDS = 16
HY_HIDDEN = 64
HY_TAPS_W = 2 * HY_ORDER * HY_WIDTH
FFT_N2 = ROW_TILE


def _dot3c(ah, al, b):
    bh, bl = _split(b)
    dg = functools.partial(jnp.dot, preferred_element_type=F32)
    return dg(ah, bh) + (dg(al, bh) + dg(ah, bl))


def _dotc(ah, b):
    return jnp.dot(ah, b.astype(BF16), preferred_element_type=F32)


def _split_const(m):
    m = np.asarray(m, np.float32)
    hi = m.astype(BF16)
    lo = (m - hi.astype(np.float32)).astype(BF16)
    return jnp.asarray(hi), jnp.asarray(lo)


TAPS_FLAT_COLS = 8


def _filter_taps(t_idx, length, c2pb_ref, w1t_ref, w1c_ref, w1s_ref, b1_ref, w2_ref, b2_ref, w3_ref, b3_ref, absd_ref):
    t = t_idx / float(max(length - 1, 1))
    ang = c2pb_ref[...] * t_idx / float(length)
    pre = t * w1t_ref[...] + _dot3(jnp.cos(ang), w1c_ref[...]) - _dot3(jnp.sin(ang), w1s_ref[...]) + b1_ref[...]
    hid = jnp.sin(pre)
    hid = jnp.sin(_dot3(hid, w2_ref[...]) + b2_ref[...])
    return (_dot3(hid, w3_ref[...]) + b3_ref[...]) * jnp.exp(-t * absd_ref[...])


def _hy_taps_body(*refs, length):
    taps_ref, ssq_ref, tap0_ref = refs[-3:]
    i = pl.program_id(0)
    t_idx = (i * ROW_TILE + lax.broadcasted_iota(jnp.int32, (ROW_TILE, 1), 0)).astype(F32)
    taps = _filter_taps(t_idx, length, *refs[:-3])
    taps_ref[...] = taps

    @pl.when(i == 0)
    def _():
        ssq_ref[...] = jnp.zeros(ssq_ref.shape, F32)
        tap0_ref[...] = taps[0:1]

    ssq_ref[...] += jnp.sum(taps * taps, axis=0, keepdims=True)


def _hy_taps_flat_body(c2pb_ref, w1t_ref, w1c_ref, w1s_ref, b1_ref, w2_ref, b2_ref, w3_ref, b3_ref, absd_ref, f1_ref,
                       planes_ref, ssq_ref, tap0_ref, *, length):
    j = pl.program_id(0)
    rows = length // FFT_N2
    nb = TAPS_FLAT_COLS

    def positions(width, per):
        a = lax.broadcasted_iota(jnp.int32, (rows, width), 0)
        b = lax.broadcasted_iota(jnp.int32, (rows, width), 1) // per
        return (a * FFT_N2 + j * nb + b).astype(F32)

    @pl.when(j == 0)
    def _():
        ssq_ref[...] = jnp.zeros(ssq_ref.shape, F32)

    h = HY_HIDDEN
    ang = c2pb_ref[...] * positions(nb * HY_POS_BANDS, HY_POS_BANDS) / float(length)
    t_h = positions(nb * h, h) / float(max(length - 1, 1))
    pre = t_h * w1t_ref[...] + _dot3(jnp.cos(ang), w1c_ref[...]) - _dot3(jnp.sin(ang), w1s_ref[...]) + b1_ref[...]
    hid = jnp.sin(pre)
    hid = jnp.sin(_dot3(hid, w2_ref[...]) + b2_ref[...])
    for bp in range(nb // 2):
        t_w = (positions(2 * HY_TAPS_W, HY_TAPS_W) + float(2 * bp)) / float(max(length - 1, 1))
        taps = (_dot3(hid[:, bp * 2 * h:(bp + 1) * 2 * h], w3_ref[...]) + b3_ref[...]) * jnp.exp(-t_w * absd_ref[...])
        planes = _dotc(f1_ref[...], taps)
        planes_ref[:, bp * 2 * HY_TAPS_W:(bp + 1) * 2 * HY_TAPS_W] = planes.astype(planes_ref.dtype)
        if bp == 0:
            @pl.when(j == 0)
            def _():
                tap0_ref[...] = taps[0:1, :HY_TAPS_W]
        sq = jnp.sum(taps * taps, axis=0, keepdims=True)
        ssq_ref[...] += sq[:, :HY_TAPS_W] + sq[:, HY_TAPS_W:]


def hyena_taps(length, w1, b1, w2, b2, w3, b3, *, flat):
    bands = jnp.linspace(1e-4, HY_POS_BANDS - 1, HY_POS_BANDS, dtype=F32)
    c2pb = jnp.zeros((1, 128), F32).at[0, :HY_POS_BANDS].set(2.0 * math.pi * bands)
    pad = lambda m: jnp.zeros((128, HY_HIDDEN), F32).at[:HY_POS_BANDS].set(m)
    deltas = jnp.linspace(math.log(1e-2) / 0.3, math.log(1e-2) / 1.5, HY_WIDTH, dtype=F32)
    absd = jnp.tile(jnp.abs(deltas), 2 * HY_ORDER)[None]
    const = lambda shp: pl.BlockSpec(shp, lambda i: (0,) * len(shp))
    w1t, w1c, w1s = w1[0:1], w1[1:1 + HY_POS_BANDS], w1[1 + HY_POS_BANDS:]
    if flat:
        nb = TAPS_FLAT_COLS
        f1 = _FftPlan(length).f1[0]
        rows = f1.shape[0]
        bd = lambda m, k: jnp.kron(jnp.eye(k, dtype=F32), m)
        tile = lambda v, k: jnp.tile(v, k)[None]
        operands = (tile(c2pb[0, :HY_POS_BANDS], nb), tile(w1t[0], nb), bd(w1c, nb), bd(w1s, nb), tile(b1, nb),
                    bd(w2, nb), tile(b2, nb), bd(w3, 2), tile(b3, 2), tile(absd[0], 2), f1)
        body, grid = _hy_taps_flat_body, (FFT_N2 // nb,)
        taps_shape, taps_dtype = (rows, FFT_N2 * HY_TAPS_W), BF16
        taps_spec = pl.BlockSpec((rows, nb * HY_TAPS_W), lambda i: (0, i))
    else:
        operands = (c2pb, w1t, pad(w1c), pad(w1s), b1[None], w2, b2[None], w3, b3[None], absd)
        body, grid = _hy_taps_body, (length // ROW_TILE,)
        taps_shape, taps_dtype = (length, HY_TAPS_W), F32
        taps_spec = pl.BlockSpec((ROW_TILE, HY_TAPS_W), lambda i: (i, 0))
    taps, ssq, tap0 = pl.pallas_call(
        functools.partial(body, length=length),
        out_shape=[jax.ShapeDtypeStruct(taps_shape, taps_dtype), jax.ShapeDtypeStruct((1, HY_TAPS_W), F32),
                   jax.ShapeDtypeStruct((1, HY_TAPS_W), F32)],
        grid=grid,
        in_specs=[const(op.shape) for op in operands],
        out_specs=[taps_spec, const((1, HY_TAPS_W)), const((1, HY_TAPS_W))],
        compiler_params=_cparams(("arbitrary",)),
        name="hyena_taps_flat" if flat else "hyena_taps",
    )(*operands)
    hw = HY_TAPS_W // 2
    norm2 = ssq[:, :hw] + ssq[:, hw:] + 2.0 * tap0[:, :hw] * tap0[:, hw:]
    return taps, lax.rsqrt(norm2)


class _FftPlan:
    def __init__(self, length):
        self.length = length
        self.n = 2 * length
        self.n2 = FFT_N2
        self.n1 = self.n // self.n2
        self.n1h = self.n1 // 2
        k1 = self.n1h + 1
        self.k1p = -(-k1 // 8) * 8
        kk = np.arange(self.k1p)[:, None].astype(np.float64)
        live = (kk < k1)
        nn = np.arange(self.n1h)[None, :].astype(np.float64)
        th = 2.0 * np.pi * kk * nn / self.n1
        self.f1 = _split_const(np.concatenate([np.cos(th) * live, -np.sin(th) * live], axis=0))
        ck = np.where((kk == 0) | (kk == self.n1h), 1.0, 2.0) * live / self.n
        self.g1 = _split_const(np.concatenate([np.cos(th) * ck, -np.sin(th) * ck], axis=0).T)
        m = np.arange(self.n2).astype(np.float64)
        ph = 2.0 * np.pi * np.outer(m, m) / self.n2
        c, s = np.cos(ph), np.sin(ph)
        self.fb = _split_const(np.block([[c, s], [-s, c]]))
        self.fbi = _split_const(np.block([[c, -s], [s, c]]))
        tw = 2.0 * np.pi * kk[:, :, None] * m[None, :, None] / self.n
        self.twc = jnp.asarray(np.cos(tw), F32)
        self.tws = jnp.asarray(np.sin(tw), F32)


FFT_TN = 4096


def _fft_a_body(fh_ref, x_ref, o_ref):
    o_ref[...] = _dotc(fh_ref[...], x_ref[...]).astype(o_ref.dtype)


def fft_stage_a(plan, xf, lead):
    rows_in, m = xf.shape
    rows = 2 * plan.k1p
    tn = min(FFT_TN, m)
    fh = jnp.pad(plan.f1[0], ((0, 0), (lead, 0)))
    fspec = pl.BlockSpec((rows, rows_in), lambda j: (0, 0))
    return pl.pallas_call(
        _fft_a_body,
        out_shape=jax.ShapeDtypeStruct((rows, m), BF16),
        grid=(m // tn,),
        in_specs=[fspec, pl.BlockSpec((rows_in, tn), lambda j: (0, j))],
        out_specs=pl.BlockSpec((rows, tn), lambda j: (0, j)),
        compiler_params=_cparams(("parallel",)),
        name="fft_stage_a",
    )(fh, xf)


def _twiddled(a_ref, twc_ref, tws_ref):
    are, aim = a_ref[0, 0].astype(F32), a_ref[1, 0].astype(F32)
    c, s = twc_ref[0], tws_ref[0]
    return jnp.concatenate([are * c + aim * s, aim * c - are * s], axis=0)


def _fft_filter_b_body(a_ref, twc_ref, tws_ref, fbh_ref, scale_ref, o_ref):
    n2 = FFT_N2
    x = _dotc(fbh_ref[...], _twiddled(a_ref, twc_ref, tws_ref))
    hw = HY_TAPS_W // 2
    xre, xim = x[:n2], x[n2:]
    o_ref[0, 0] = ((xre[:, :hw] + xre[:, hw:]) * scale_ref[...]).astype(o_ref.dtype)
    o_ref[1, 0] = ((xim[:, :hw] - xim[:, hw:]) * scale_ref[...]).astype(o_ref.dtype)


def fft_filter_stage_b(plan, a, scale):
    n2, k1p = plan.n2, plan.k1p
    hw = HY_TAPS_W // 2
    const = lambda shp: pl.BlockSpec(shp, lambda k: (0,) * len(shp))
    return pl.pallas_call(
        _fft_filter_b_body,
        out_shape=jax.ShapeDtypeStruct((2, k1p, n2, hw), BF16),
        grid=(k1p,),
        in_specs=[pl.BlockSpec((2, 1, n2, HY_TAPS_W), lambda k: (0, k, 0, 0)),
                  pl.BlockSpec((1, n2, 1), lambda k: (k, 0, 0)), pl.BlockSpec((1, n2, 1), lambda k: (k, 0, 0)),
                  const((2 * n2, 2 * n2)), const((1, hw))],
        out_specs=pl.BlockSpec((2, 1, n2, hw), lambda k: (0, k, 0, 0)),
        compiler_params=_cparams(("parallel",)),
        name="fft_filter_stage_b",
    )(a, plan.twc, plan.tws, plan.fb[0], scale)


def _fft_conv_b_body(a_ref, kf_ref, twc_ref, tws_ref, fbh_ref, fih_ref, o_ref):
    n2 = FFT_N2
    z = _dotc(fbh_ref[...], _twiddled(a_ref, twc_ref, tws_ref))
    zre, zim = z[:n2], z[n2:]
    kre, kim = kf_ref[0, 0].astype(F32), kf_ref[1, 0].astype(F32)
    y = jnp.concatenate([zre * kre - zim * kim, zre * kim + zim * kre], axis=0)
    q = _dotc(fih_ref[...], y)
    qre, qim = q[:n2], q[n2:]
    c, s = twc_ref[0], tws_ref[0]
    o_ref[0, 0] = (qre * c - qim * s).astype(o_ref.dtype)
    o_ref[1, 0] = (qim * c + qre * s).astype(o_ref.dtype)


def fft_conv_stage_b(plan, a, kf, order):
    n2, k1p = plan.n2, plan.k1p
    const = lambda shp: pl.BlockSpec(shp, lambda k: (0,) * len(shp))
    return pl.pallas_call(
        _fft_conv_b_body,
        out_shape=jax.ShapeDtypeStruct((2, k1p, n2, HY_WIDTH), BF16),
        grid=(k1p,),
        in_specs=[pl.BlockSpec((2, 1, n2, HY_WIDTH), lambda k: (0, k, 0, 0)),
                  pl.BlockSpec((2, 1, n2, HY_WIDTH), lambda k: (0, k, 0, order)),
                  pl.BlockSpec((1, n2, 1), lambda k: (k, 0, 0)), pl.BlockSpec((1, n2, 1), lambda k: (k, 0, 0)),
                  const((2 * n2, 2 * n2)), const((2 * n2, 2 * n2))],
        out_specs=pl.BlockSpec((2, 1, n2, HY_WIDTH), lambda k: (0, k, 0, 0)),
        compiler_params=_cparams(("parallel",)),
        name="fft_conv_stage_b",
    )(a, kf, plan.twc, plan.tws, plan.fb[0], plan.fbi[0])


def _fft_inv_a_body(gh_ref, fh_ref, q_ref, z_ref, gate_ref, skip_ref, o_ref, *next_ref):
    y = _dotc(gh_ref[...], q_ref[...])
    z_next = gate_ref[...] * (y + z_ref[...] * skip_ref[...])
    o_ref[...] = z_next
    if next_ref:
        next_ref[0][...] = _dotc(fh_ref[...], z_next).astype(next_ref[0].dtype)


def fft_inv_stage_a(plan, qf, zf, gatef, skip_t, lead, with_next):
    rows_out, m = zf.shape
    rows = 2 * plan.k1p
    tn = skip_t.shape[1]
    gh = jnp.pad(plan.g1[0], ((lead, 0), (0, 0)))
    fh = jnp.pad(plan.f1[0], ((0, 0), (lead, 0)))
    col = lambda r_: pl.BlockSpec((r_, tn), lambda j: (0, j))
    out_shape = [jax.ShapeDtypeStruct((rows_out, m), F32)]
    out_specs = [col(rows_out)]
    if with_next:
        out_shape.append(jax.ShapeDtypeStruct((rows, m), BF16))
        out_specs.append(col(rows))
    return pl.pallas_call(
        _fft_inv_a_body,
        out_shape=out_shape,
        grid=(m // tn,),
        in_specs=[pl.BlockSpec((rows_out, rows), lambda j: (0, 0)), pl.BlockSpec((rows, rows_out), lambda j: (0, 0)),
                  col(rows), col(rows_out), col(rows_out), pl.BlockSpec((1, tn), lambda j: (0, 0))],
        out_specs=out_specs,
        compiler_params=_cparams(("parallel",)),
        name="fft_inv_stage_a",
    )(gh, fh, qf, zf, gatef, skip_t)


def hyena_long(hv, hx1, hx2, ta, scale, skip, lead):
    length = hv.shape[0] - lead * FFT_N2
    plan = _FftPlan(length)
    n2, k1p = plan.n2, plan.k1p
    m = n2 * HY_WIDTH
    tn = min(FFT_TN, m)
    flat = lambda a: a.reshape(lead + plan.n1h, m)
    kf = fft_filter_stage_b(plan, ta.reshape(2, k1p, n2, HY_TAPS_W), scale)
    z = flat(hv)
    a = fft_stage_a(plan, z, lead)
    gates = (hx1, hx2)
    for o, gate in enumerate(gates):
        q = fft_conv_stage_b(plan, a.reshape(2, k1p, n2, HY_WIDTH), kf, o)
        last = o == len(gates) - 1
        out = fft_inv_stage_a(plan, q.reshape(2 * k1p, m), z, flat(gate), jnp.tile(skip[o], tn // HY_WIDTH)[None], lead,
                              not last)
        z = out[0]
        if not last:
            a = out[1]
    return z.reshape(-1, HY_WIDTH)


def _rec_out_body(x_ref, yf_ref, yb_ref, bonus_ref, sgr_ref, zh_ref, zc_ref, sgh_ref, gnw_ref, gnb_ref, bm_ref, wa_ref, wb_ref,
                  gate_ref, fin_ref, o_ref, *, final):
    y = yf_ref[...] + yb_ref[...]
    bm = bm_ref[...]
    mean = _dot_stat(y, bm)
    yc = y - mean
    var = _dot_stat(yc * yc, bm)
    yn = yc * lax.rsqrt(var + RWKV_GN_EPS) * gnw_ref[...] + gnb_ref[...]
    ya = ((yn + bonus_ref[...]) * sgr_ref[...]).astype(BF16)
    zh = zh_ref[...]
    if not final:
        zh = jnp.where(pl.program_id(0) == 0, zc_ref[...], zh)
    yh = (zh * sgh_ref[...]).astype(BF16)
    out = jnp.dot(ya, wa_ref[...], preferred_element_type=F32) + jnp.dot(yh, wb_ref[...], preferred_element_type=F32)
    xn = x_ref[...] + gate_ref[0] * out
    if final:
        xn = xn * lax.rsqrt(jnp.mean(xn * xn, axis=-1, keepdims=True) + NORM_EPS) * fin_ref[...]
    o_ref[...] = xn


RWKV_GN_EPS = 64e-5


def rec_out_proj(xs, yf, yb, bonus, sgr, zh, zc, sgh, gn_w, gn_b, w_out, gate2, final_g, *, final):
    r = xs.shape[0]
    d = D_MODEL
    off = 1 if final else 0
    nt = r // ROW_TILE - off
    rows = lambda w: pl.BlockSpec((ROW_TILE, w), lambda i: (i + off, 0))
    const = lambda shp: pl.BlockSpec(shp, lambda i: (0,) * len(shp))
    bw = BRANCH_W
    return pl.pallas_call(
        functools.partial(_rec_out_body, final=final),
        out_shape=jax.ShapeDtypeStruct((nt * ROW_TILE, d), F32),
        grid=(nt,),
        in_specs=[rows(d), rows(bw), rows(bw), rows(bw), rows(bw), rows(bw), const((CTX_LEN, bw)), rows(bw),
                  const((1, bw)), const((1, bw)),
                  const((bw, bw)), const((bw, d)), const((bw, d)),
                  pl.BlockSpec((1, 1, d), lambda i: (jnp.minimum(i + off, 1), 0, 0)), const((1, d))],
        out_specs=pl.BlockSpec((ROW_TILE, d), lambda i: (i, 0)),
        compiler_params=_cparams(("parallel",)),
        name="rec_out_proj",
    )(xs, yf, yb, bonus, sgr, zh, zc, sgh, gn_w[None], gn_b[None], _block_sum_mat(bw, 1.0 / HEAD_DIM).astype(BF16),
      w_out[:bw].astype(BF16), w_out[bw:].astype(BF16), gate2, final_g[None])


def rec_layer(xs, mods, norm_g, w_in, mu, w0, w_up, a0, a_up, k_k, k_a, r_k, gn_w, gn_b, hy_short, hy_w1, hy_b1, hy_w2,
              hy_b2, hy_w3, hy_b3, hy_skip, w_out, final_g, final):
    shift2, scale2, gate2 = mods
    rw, hv, hx1, hx2, sgr, sgh = rec_in_proj(xs, norm_g, scale2, shift2, w_in, mu, hy_short)
    g, add, bonus = rwkv_prep(rw, w0, w_up, a0, a_up, k_k, k_a, r_k)
    yf, yb = rwkv_scan(g, add)
    fargs = (hy_w1, hy_b1, hy_w2, hy_b2, hy_w3, hy_b3)
    n = xs.shape[0] - CTX_LEN
    taps, scale = hyena_taps(n, *fargs, flat=True)
    zh = hyena_long(hv, hx1, hx2, taps, scale, hy_skip, CTX_LEN // FFT_N2)
    if final:
        z_ctx = hv[:CTX_LEN]
    else:
        taps_c, scale_c = hyena_taps(CTX_LEN, *fargs, flat=False)
        z_ctx = hyena_short(hv[:CTX_LEN], hx1[:CTX_LEN], hx2[:CTX_LEN], taps_c, scale_c, hy_skip)
    return rec_out_proj(xs, yf, yb, bonus, sgr, zh, z_ctx, sgh, gn_w, gn_b, w_out, gate2, final_g, final=final)


def kernel(x, c, ctx, c_ctx, attn_norm, attn_ada_w, attn_ada_b, attn_w_in, na_rpb, gqa_q_gain, gqa_k_gain, attn_w_out,
           rec_norm, rec_ada_w, rec_ada_b, rec_w_in, rwkv_mu, rwkv_w0, rwkv_w_up, rwkv_a0, rwkv_a_up, rwkv_k_k, rwkv_k_a,
           rwkv_r_k, rwkv_gn_w, rwkv_gn_b, hy_short, hy_w1, hy_b1, hy_w2, hy_b2, hy_w3, hy_b3, hy_skip, rec_w_out,
           final_norm):
    assert x.shape[0] == 1 and ctx.shape[1] == CTX_LEN and x.shape[2] == D_MODEL
    n = x.shape[1]
    assert n % ROW_TILE == 0 and n // ROW_TILE >= 3
    assert attn_w_in.shape[0] == rec_w_in.shape[0]
    d = D_MODEL
    cond8 = jnp.zeros((8, d), F32).at[0].set(c_ctx).at[1].set(c[0])
    m_attn = adaln_all(cond8, attn_ada_w, attn_ada_b)
    m_rec = adaln_all(cond8, rec_ada_w, rec_ada_b)
    mods = lambda m, i: tuple(m[i, :2, j * d:(j + 1) * d].reshape(2, 1, d) for j in range(3))
    cos_t, sin_t = _rope_tables(n)
    xs = jnp.concatenate([ctx[0], x[0]], axis=0)
    depth = attn_w_in.shape[0] + rec_w_in.shape[0]
    for layer in range(depth):
        i = layer // 2
        final = layer == depth - 1
        if layer % 2 == 0:
            xs = attn_layer(xs, mods(m_attn, i), attn_norm[i], attn_w_in[i], na_rpb[i], gqa_q_gain[i], gqa_k_gain[i],
                            attn_w_out[i], cos_t, sin_t)
        else:
            xs = rec_layer(xs, mods(m_rec, i), rec_norm[i], rec_w_in[i], rwkv_mu[i], rwkv_w0[i], rwkv_w_up[i],
                           rwkv_a0[i], rwkv_a_up[i], rwkv_k_k[i], rwkv_k_a[i], rwkv_r_k[i], rwkv_gn_w[i], rwkv_gn_b[i],
                           hy_short[i], hy_w1[i], hy_b1[i], hy_w2[i], hy_b2[i], hy_w3[i], hy_b3[i], hy_skip[i],
                           rec_w_out[i], final_norm, final)
    return xs[None]


def _hy_short_body(fh_ref, fl_ref, gh_ref, gl_ref, v_ref, x1_ref, x2_ref, taps_ref, scale_ref, skip_ref, o_ref):
    fh, fl, gh, gl = fh_ref[...], fl_ref[...], gh_ref[...], gl_ref[...]
    kp = fh.shape[0] // 2
    hw = HY_TAPS_W // 2
    tf = _dot3c(fh, fl, taps_ref[...])
    kre = (tf[:kp, :hw] + tf[:kp, hw:]) * scale_ref[...]
    kim = (tf[kp:, :hw] - tf[kp:, hw:]) * scale_ref[...]
    z = v_ref[...]
    for o, gate_ref in enumerate((x1_ref, x2_ref)):
        ls = slice(o * HY_WIDTH, (o + 1) * HY_WIDTH)
        zf = _dot3c(fh, fl, z)
        zre, zim = zf[:kp], zf[kp:]
        y = jnp.concatenate([zre * kre[:, ls] - zim * kim[:, ls], zre * kim[:, ls] + zim * kre[:, ls]], axis=0)
        z = gate_ref[...] * (_dot3c(gh, gl, y) + z * skip_ref[o:o + 1])
    o_ref[...] = z


def hyena_short(hv, hx1, hx2, taps, scale, skip):
    length = hv.shape[0]
    n = 2 * length
    k1 = length + 1
    kp = -(-k1 // 8) * 8
    kk = np.arange(kp)[:, None].astype(np.float64)
    live = kk < k1
    th = 2.0 * np.pi * kk * np.arange(length)[None, :] / n
    f = _split_const(np.concatenate([np.cos(th) * live, -np.sin(th) * live], axis=0))
    ck = np.where((kk == 0) | (kk == length), 1.0, 2.0) * live / n
    g = _split_const(np.concatenate([np.cos(th) * ck, -np.sin(th) * ck], axis=0).T)
    return pl.pallas_call(
        _hy_short_body,
        out_shape=jax.ShapeDtypeStruct((length, HY_WIDTH), F32),
        compiler_params=pltpu.CompilerParams(vmem_limit_bytes=VMEM_LIMIT),
        name="hyena_short",
    )(f[0], f[1], g[0], g[1], hv, hx1, hx2, taps, scale, skip)
```

```python
import functools
import math

import jax
import jax.numpy as jnp
import numpy as np
from jax import lax
from jax.experimental import pallas as pl
from jax.experimental.pallas import tpu as pltpu

F32 = jnp.float32
BF16 = jnp.bfloat16
HIGHEST = lax.Precision.HIGHEST

D_MODEL = 1024
GRID_W = 64
CTX_LEN = 256
HEAD_DIM = 64
BRANCH_W = 512
N_HEADS = 8
GQA_KV_W = 128
NA_WIN_ROWS = 8
NA_WIN_COLS = 16
ROPE_THETA = 10000.0
ROPE_FREQS = 16
NORM_EPS = 1e-6
ROW_TILE = 256
NA_GROUP_ROWS = 4
NEG_BIG = -1e30
LOG2E = math.log2(math.e)
QK_SCALE = HEAD_DIM ** -0.5 * LOG2E
VMEM_LIMIT = 56 * 1024 * 1024

ATTN_SPLITS = (512, 512, 512, 512, 512, 128, 128, 512)
GQA_HEAD_ORDER = (0, 4, 1, 5, 2, 6, 3, 7)


def _cparams(sem):
    return pltpu.CompilerParams(dimension_semantics=sem, vmem_limit_bytes=VMEM_LIMIT)


def _silu(v):
    return v * (1.0 / (1.0 + jnp.exp(-v)))


def _lane_half(shape):
    return (lax.broadcasted_iota(jnp.int32, shape, len(shape) - 1) // HEAD_DIM) % 2


def _dot_stat(a, block_mat):
    hi = a.astype(BF16)
    lo = (a - hi.astype(F32)).astype(BF16)
    return jnp.dot(hi, block_mat, preferred_element_type=F32) + jnp.dot(lo, block_mat, preferred_element_type=F32)


def _dot_nt(a, b):
    return lax.dot_general(a, b, (((1,), (1,)), ((), ())), preferred_element_type=F32)


def _adaln_body(cond_ref, w_ref, b_ref, o_ref):
    s = _silu(cond_ref[...])
    o_ref[0] = jnp.dot(s, w_ref[0], precision=HIGHEST, preferred_element_type=F32) + b_ref[0]


def adaln_all(cond8, ada_w, ada_b):
    nl = ada_w.shape[0]
    d = D_MODEL
    return pl.pallas_call(
        _adaln_body,
        out_shape=jax.ShapeDtypeStruct((nl, 8, 3 * d), F32),
        grid=(nl, 3),
        in_specs=[
            pl.BlockSpec((8, d), lambda l, j: (0, 0)),
            pl.BlockSpec((1, d, d), lambda l, j: (l, 0, j)),
            pl.BlockSpec((1, 1, d), lambda l, j: (l, 0, j)),
        ],
        out_specs=pl.BlockSpec((1, 8, d), lambda l, j: (l, 0, j)),
        compiler_params=_cparams(("parallel", "parallel")),
        name="adaln",
    )(cond8, ada_w, ada_b.reshape(nl, 1, 3 * d))


def _modulated(x_ref, g_ref, scale_ref, shift_ref):
    xf = x_ref[...]
    y = xf * lax.rsqrt(jnp.mean(xf * xf, axis=-1, keepdims=True) + NORM_EPS)
    return (y * g_ref[...]) * (1.0 + scale_ref[0]) + shift_ref[0]


def _mod_specs():
    d = D_MODEL
    return [
        pl.BlockSpec((ROW_TILE, d), lambda i: (i, 0)),
        pl.BlockSpec((1, d), lambda i: (0, 0)),
        pl.BlockSpec((1, 1, d), lambda i: (jnp.minimum(i, 1), 0, 0)),
        pl.BlockSpec((1, 1, d), lambda i: (jnp.minimum(i, 1), 0, 0)),
    ]


def _attn_in_body(x_ref, g_ref, scale_ref, shift_ref, w_ref, cos_ref, sin_ref, gq_ref, gqs_ref, gk_ref, gks_ref,
                  bdq_ref, bdk_ref,
                  qa_ref, ka_ref, va_ref, sga_ref, qb_ref, kb_ref, vb_ref, sgb_ref):
    xm = _modulated(x_ref, g_ref, scale_ref, shift_ref).astype(BF16)
    u = jnp.dot(xm, w_ref[...], preferred_element_type=F32)
    qa, ka, va, ga = u[:, 0:512], u[:, 512:1024], u[:, 1024:1536], u[:, 1536:2048]
    qb, kb, vb, gb = u[:, 2048:2560], u[:, 2560:2688], u[:, 2688:2816], u[:, 2816:3328]
    qbs, kbs = u[:, 3328:3840], u[:, 3840:3968]
    scale = QK_SCALE
    qa_ref[...] = (qa * scale).astype(BF16)
    ka_ref[...] = ka.astype(BF16)
    va_ref[...] = va.astype(BF16)
    sga_ref[...] = _silu(ga)
    sgb_ref[...] = _silu(gb)
    vb_ref[...] = jnp.transpose(vb).astype(BF16)
    cos_k, sin_k = cos_ref[...], sin_ref[...]
    cos_q = jnp.concatenate([cos_k] * 4, axis=1)
    sin_q = jnp.concatenate([sin_k] * 4, axis=1)
    rs_q = lax.rsqrt(_dot_stat(qb * qb, bdq_ref[...]) + NORM_EPS)
    rs_k = lax.rsqrt(_dot_stat(kb * kb, bdk_ref[...]) + NORM_EPS)
    qr = rs_q * (qb * gq_ref[...] * cos_q + qbs * gqs_ref[...] * sin_q)
    kr = rs_k * (kb * gk_ref[...] * cos_k + kbs * gks_ref[...] * sin_k)
    qb_ref[...] = jnp.transpose(qr * scale).astype(BF16)
    kb_ref[...] = kr.astype(BF16)


def _rope_tables(n):
    t = jnp.arange(n, dtype=jnp.int32)
    pos = jnp.stack([t // GRID_W, t % GRID_W], axis=-1).astype(F32)
    inv_freq = ROPE_THETA ** (-jnp.arange(ROPE_FREQS, dtype=F32) / ROPE_FREQS)
    ang = pos[:, :, None] * inv_freq
    c, s = jnp.cos(ang), jnp.sin(ang)
    cos64 = jnp.concatenate([c[:, 0], c[:, 0], c[:, 1], c[:, 1]], axis=-1)
    sin64 = jnp.concatenate([-s[:, 0], s[:, 0], -s[:, 1], s[:, 1]], axis=-1)
    cos64 = jnp.concatenate([jnp.ones((CTX_LEN, HEAD_DIM), F32), cos64], axis=0)
    sin64 = jnp.concatenate([jnp.zeros((CTX_LEN, HEAD_DIM), F32), sin64], axis=0)
    return jnp.tile(cos64, (1, 2)), jnp.tile(sin64, (1, 2))


def _reorder_heads(w, order, axis):
    take = lambda h: lax.slice_in_dim(w, h * HEAD_DIM, (h + 1) * HEAD_DIM, axis=axis)
    return jnp.concatenate([take(h) for h in order], axis=axis)


def _swap_rope_halves(w):
    shp = w.shape
    return jnp.flip(w.reshape(shp[:-1] + (shp[-1] // (2 * ROPE_FREQS), 2, ROPE_FREQS)), axis=-2).reshape(shp)


def attn_in_proj(xs, norm_g, scale2, shift2, w_in, q_gain, k_gain, cos_t, sin_t):
    r = xs.shape[0]
    d = D_MODEL
    parts, start = [], 0
    for s in ATTN_SPLITS:
        parts.append(w_in[:, start:start + s])
        start += s
    wqa, wka, wva, wga, wqb, wkb, wvb, wgb = parts
    wqb_p = _reorder_heads(wqb, GQA_HEAD_ORDER, 1)
    wgb_p = _reorder_heads(wgb, GQA_HEAD_ORDER, 1)
    wqb_sw = _swap_rope_halves(wqb_p)
    wkb_sw = _swap_rope_halves(wkb)
    w_ext = jnp.concatenate([wqa, wka, wva, wga, wqb_p, wkb, wvb, wgb_p, wqb_sw, wkb_sw], axis=1).astype(BF16)
    gq = jnp.tile(q_gain, N_HEADS)[None]
    gqs = jnp.tile(_swap_rope_halves(q_gain), N_HEADS)[None]
    gk = jnp.tile(k_gain, 2)[None]
    gks = jnp.tile(_swap_rope_halves(k_gain), 2)[None]
    bdq = jnp.asarray(np.kron(np.eye(N_HEADS), np.full((HEAD_DIM, HEAD_DIM), 1.0 / HEAD_DIM)), BF16)
    bdk = jnp.asarray(np.kron(np.eye(2), np.full((HEAD_DIM, HEAD_DIM), 1.0 / HEAD_DIM)), BF16)
    wcols = w_ext.shape[1]
    const = lambda shp: pl.BlockSpec(shp, lambda i: (0,) * len(shp))
    rows = lambda w: pl.BlockSpec((ROW_TILE, w), lambda i: (i, 0))
    out_shapes = [
        jax.ShapeDtypeStruct((r, 512), BF16), jax.ShapeDtypeStruct((r, 512), BF16), jax.ShapeDtypeStruct((r, 512), BF16),
        jax.ShapeDtypeStruct((r, 512), F32),
        jax.ShapeDtypeStruct((512, r), BF16), jax.ShapeDtypeStruct((r, 128), BF16), jax.ShapeDtypeStruct((128, r), BF16),
        jax.ShapeDtypeStruct((r, 512), F32),
    ]
    cols = lambda w: pl.BlockSpec((w, ROW_TILE), lambda i: (0, i))
    return pl.pallas_call(
        _attn_in_body,
        out_shape=out_shapes,
        grid=(r // ROW_TILE,),
        in_specs=_mod_specs() + [const((d, wcols)), rows(128), rows(128), const((1, 512)), const((1, 512)),
                                 const((1, 128)), const((1, 128)), const((512, 512)), const((128, 128))],
        out_specs=[rows(512), rows(512), rows(512), rows(512), cols(512), rows(128), cols(128), rows(512)],
        compiler_params=_cparams(("parallel",)),
        name="attn_in_proj",
    )(xs, norm_g[None], scale2, shift2, w_ext, cos_t, sin_t, gq, gqs, gk, gks, bdq, bdk)


def _na_cols_body(rpb_ref, sel_ref, neg_ref, o_ref):
    o_ref[...] = jnp.dot(rpb_ref[...], sel_ref[...], precision=HIGHEST, preferred_element_type=F32) + neg_ref[...]


def _na_bias_tables(rpb, rows):
    nrel_r, nrel_c = 2 * NA_WIN_ROWS - 1, 2 * NA_WIN_COLS - 1
    qc = np.arange(GRID_W)[:, None]
    kc = np.arange(GRID_W)[None, :]
    col0 = np.clip(qc - NA_WIN_COLS // 2, 0, GRID_W - NA_WIN_COLS)
    col_ok = (kc >= col0) & (kc < col0 + NA_WIN_COLS)
    rc = kc - qc + NA_WIN_COLS - 1
    sel = np.zeros((128, GRID_W * GRID_W), np.float32)
    sel[np.where(col_ok, rc, 127).reshape(-1), np.arange(GRID_W * GRID_W)] = col_ok.reshape(-1)
    neg = np.where(col_ok, 0.0, NEG_BIG).astype(np.float32).reshape(1, -1)
    rpb2 = jnp.zeros((128, 128), F32).at[:N_HEADS * nrel_r, :nrel_c].set(rpb.reshape(N_HEADS * nrel_r, nrel_c))
    cols = pl.pallas_call(
        _na_cols_body,
        out_shape=jax.ShapeDtypeStruct((128, GRID_W * GRID_W), F32),
        name="na_bias_cols",
    )(rpb2, jnp.asarray(sel), jnp.asarray(neg))
    cols = cols[:N_HEADS * nrel_r].reshape(N_HEADS, nrel_r, GRID_W, GRID_W)
    kh = min(NA_WIN_ROWS, rows)
    g = rows // NA_GROUP_ROWS
    cases = [(0, 0), (NA_GROUP_ROWS, 0), (rows - NA_GROUP_ROWS, NA_GROUP_ROWS * (g - 3))]
    masked = jnp.full((N_HEADS, GRID_W, GRID_W), NEG_BIG, F32)
    tabs = []
    for qr_first, start in cases:
        blocks = []
        for j in range(NA_GROUP_ROWS):
            qr = qr_first + j
            row0 = min(max(qr - kh // 2, 0), rows - kh)
            for i in range(3 * NA_GROUP_ROWS):
                kr = start + i
                blocks.append(cols[:, kr - qr + NA_WIN_ROWS - 1] if row0 <= kr < row0 + kh else masked)
        tab = jnp.stack(blocks, axis=1).reshape(N_HEADS, NA_GROUP_ROWS, 3 * NA_GROUP_ROWS, GRID_W, GRID_W)
        tabs.append(tab.transpose(0, 1, 3, 2, 4).reshape(N_HEADS, ROW_TILE, 3 * ROW_TILE))
    return jnp.stack(tabs) * LOG2E


def _na_body(q_ref, kc_ref, k0_ref, k1_ref, k2_ref, vc_ref, v0_ref, v1_ref, v2_ref, bias_ref, sg_ref, o_ref):
    half = _lane_half((ROW_TILE, 128))
    k_refs = (k0_ref, k1_ref, k2_ref, kc_ref)
    v_refs = (v0_ref, v1_ref, v2_ref, vc_ref)

    def scores(h):
        hp, j = divmod(h, 2)
        ls = slice(hp * 128, (hp + 1) * 128)
        qp = q_ref[:, ls]
        qm = jnp.where(half == j, qp, jnp.zeros_like(qp))
        return [_dot_nt(qm, r[:, ls]) for r in k_refs]

    s_next = scores(0)
    outs = []
    for h in range(N_HEADS):
        hp, j = divmod(h, 2)
        ls = slice(hp * 128, (hp + 1) * 128)
        s = s_next
        if h + 1 < N_HEADS:
            s_next = scores(h + 1)
        s_win = jnp.concatenate(s[:3], axis=1) + bias_ref[0, h]
        s_ctx = s[3]
        m = jnp.maximum(jnp.max(s_win, axis=1, keepdims=True), jnp.max(s_ctx, axis=1, keepdims=True))
        p_win = jnp.exp2(s_win - m).astype(BF16)
        p_ctx = jnp.exp2(s_ctx - m).astype(BF16)
        vms = [jnp.where(half == j, r[:, ls], jnp.ones((ROW_TILE, 128), BF16)) for r in v_refs]
        o = jnp.dot(p_ctx, vms[3], preferred_element_type=F32)
        for b in range(3):
            o += jnp.dot(p_win[:, b * ROW_TILE:(b + 1) * ROW_TILE], vms[b], preferred_element_type=F32)
        outs.append(o / pltpu.roll(o, HEAD_DIM, 1))
        if j == 1:
            o_pair = jnp.where(half == 0, outs[h - 1], outs[h])
            o_ref[:, ls] = (o_pair * sg_ref[:, ls]).astype(BF16)


def na_attention(qa, ka, va, sga, bias_tabs, n):
    g = n // ROW_TILE
    w = BRANCH_W

    def kv_spec(off):
        return pl.BlockSpec((ROW_TILE, w), lambda i: (jnp.clip(i - 1, 0, g - 3) + off + 1, 0))

    ctx_spec = pl.BlockSpec((ROW_TILE, w), lambda i: (0, 0))
    q_spec = pl.BlockSpec((ROW_TILE, w), lambda i: (i + 1, 0))
    case = lambda i: jnp.where(i == 0, 0, jnp.where(i == g - 1, 2, 1))
    bias_spec = pl.BlockSpec((1, N_HEADS, ROW_TILE, 3 * ROW_TILE), lambda i: (case(i), 0, 0, 0))
    return pl.pallas_call(
        _na_body,
        out_shape=jax.ShapeDtypeStruct((n, w), BF16),
        grid=(g,),
        in_specs=[q_spec, ctx_spec, kv_spec(0), kv_spec(1), kv_spec(2), ctx_spec, kv_spec(0), kv_spec(1), kv_spec(2),
                  bias_spec, q_spec],
        out_specs=pl.BlockSpec((ROW_TILE, w), lambda i: (i, 0)),
        compiler_params=_cparams(("parallel",)),
        name="na_attention",
    )(qa, ka, ka, ka, ka, va, va, va, va, bias_tabs, sga)


def _flash_mha_body(q_ref, k_ref, v_ref, sg_ref, o_ref, m_ref, acc_ref):
    kv = pl.program_id(1)
    tq = q_ref.shape[0]

    @pl.when(kv == 0)
    def _():
        m_ref[...] = jnp.full(m_ref.shape, NEG_BIG, F32)
        acc_ref[...] = jnp.zeros(acc_ref.shape, F32)

    khalf = _lane_half((k_ref.shape[0], 128))
    for p in range(N_HEADS // 2):
        ls = slice(p * 128, (p + 1) * 128)
        qp = q_ref[:, ls]
        kp = k_ref[:, ls]
        vp = v_ref[:, ls]
        for j in range(2):
            hh = 2 * p + j
            km = jnp.where(khalf == j, kp, jnp.zeros_like(kp))
            vm = jnp.where(khalf == j, vp, jnp.ones_like(vp))
            s = _dot_nt(qp, km)
            m_prev = m_ref[hh]
            m_new = jnp.maximum(m_prev, jnp.max(s, axis=1, keepdims=True))
            alpha = jnp.exp2(m_prev - m_new)
            pr = jnp.exp2(s - m_new[:, :1]).astype(BF16)
            acc_ref[hh] = alpha * acc_ref[hh] + jnp.dot(pr, vm, preferred_element_type=F32)
            m_ref[hh] = m_new

    @pl.when(kv == pl.num_programs(1) - 1)
    def _():
        half = _lane_half((tq, 128))
        for p in range(N_HEADS // 2):
            ls = slice(p * 128, (p + 1) * 128)
            a0, a1 = acc_ref[2 * p], acc_ref[2 * p + 1]
            o0 = a0 / pltpu.roll(a0, HEAD_DIM, 1)
            o1 = a1 / pltpu.roll(a1, HEAD_DIM, 1)
            o_ref[:, ls] = (jnp.where(half == 0, o0, o1) * sg_ref[:, ls]).astype(BF16)


def flash_mha(q, k, v, sg, *, q_block0, nq, tk, nk):
    tq = ROW_TILE
    return pl.pallas_call(
        _flash_mha_body,
        out_shape=jax.ShapeDtypeStruct((nq * tq, BRANCH_W), BF16),
        grid=(nq, nk),
        in_specs=[
            pl.BlockSpec((tq, BRANCH_W), lambda i, j: (i + q_block0, 0)),
            pl.BlockSpec((tk, BRANCH_W), lambda i, j: (j, 0)),
            pl.BlockSpec((tk, BRANCH_W), lambda i, j: (j, 0)),
            pl.BlockSpec((tq, BRANCH_W), lambda i, j: (i + q_block0, 0)),
        ],
        out_specs=pl.BlockSpec((tq, BRANCH_W), lambda i, j: (i, 0)),
        scratch_shapes=[pltpu.VMEM((N_HEADS, tq, 128), F32)] * 2,
        compiler_params=_cparams(("parallel", "arbitrary")),
        name="flash_mha",
    )(q, k, v, sg)


def _flash_gqa_body(qt_ref, k_ref, vt_ref, sg_ref, o_ref, m_ref, acc_ref):
    kv = pl.program_id(1)
    tq = qt_ref.shape[1]
    tk = k_ref.shape[0]

    @pl.when(kv == 0)
    def _():
        m_ref[...] = jnp.full(m_ref.shape, NEG_BIG, F32)
        acc_ref[...] = jnp.zeros(acc_ref.shape, F32)

    khalf = _lane_half((tk, 128))
    vhalf = lax.broadcasted_iota(jnp.int32, (128, tk), 0) // HEAD_DIM
    kb = k_ref[...]
    vt = vt_ref[...]
    kms = [jnp.where(khalf == j, kb, jnp.zeros_like(kb)) for j in range(2)]
    vms = [jnp.where(vhalf == j, vt, jnp.ones_like(vt)) for j in range(2)]

    def scores(hh):
        p, j = divmod(hh, 2)
        return jnp.dot(kms[j], qt_ref[p * 128:(p + 1) * 128, :], preferred_element_type=F32).astype(BF16)

    st_next = scores(0)
    for hh in range(N_HEADS):
        st = st_next
        if hh + 1 < N_HEADS:
            st_next = scores(hh + 1)
        m_prev = m_ref[hh]
        m_new = jnp.maximum(m_prev, jnp.max(st, axis=0, keepdims=True).astype(F32))
        alpha = jnp.exp2(m_prev - m_new)
        pt = jnp.exp2(st - m_new[0:1].astype(BF16))
        acc_ref[hh] = alpha[0:1] * acc_ref[hh] + jnp.dot(vms[hh % 2], pt, preferred_element_type=F32)
        m_ref[hh] = m_new

    @pl.when(kv == pl.num_programs(1) - 1)
    def _():
        for p in range(N_HEADS // 2):
            ls = slice(p * 128, (p + 1) * 128)
            a0, a1 = acc_ref[2 * p], acc_ref[2 * p + 1]
            ot = jnp.concatenate([a0[:HEAD_DIM] / a0[HEAD_DIM:], a1[HEAD_DIM:] / a1[:HEAD_DIM]], axis=0)
            o_ref[:, ls] = (jnp.transpose(ot) * sg_ref[:, ls]).astype(BF16)


def flash_gqa(qt, k, vt, sg, *, q_block0, nq, tk, nk):
    tq = ROW_TILE
    return pl.pallas_call(
        _flash_gqa_body,
        out_shape=jax.ShapeDtypeStruct((nq * tq, BRANCH_W), BF16),
        grid=(nq, nk),
        in_specs=[
            pl.BlockSpec((BRANCH_W, tq), lambda i, j: (0, i + q_block0)),
            pl.BlockSpec((tk, GQA_KV_W), lambda i, j: (j, 0)),
            pl.BlockSpec((GQA_KV_W, tk), lambda i, j: (0, j)),
            pl.BlockSpec((tq, BRANCH_W), lambda i, j: (i + q_block0, 0)),
        ],
        out_specs=pl.BlockSpec((tq, BRANCH_W), lambda i, j: (i, 0)),
        scratch_shapes=[pltpu.VMEM((N_HEADS, 8, tq), F32), pltpu.VMEM((N_HEADS, 128, tq), F32)],
        compiler_params=_cparams(("parallel", "arbitrary")),
        name="flash_gqa",
    )(qt, k, vt, sg)


def _out_body(x_ref, ya_ref, yac_ref, yb_ref, ybc_ref, wa_ref, wb_ref, gate_ref, o_ref):
    is_ctx = pl.program_id(0) == 0
    ya = jnp.where(is_ctx, yac_ref[...], ya_ref[...])
    yb = jnp.where(is_ctx, ybc_ref[...], yb_ref[...])
    y = jnp.dot(ya, wa_ref[...], preferred_element_type=F32)
    y += jnp.dot(yb, wb_ref[...], preferred_element_type=F32)
    o_ref[...] = x_ref[...] + gate_ref[0] * y


def out_proj(xs, ya_lat, ya_ctx, yb_lat, yb_ctx, wa, wb, gate2):
    r = xs.shape[0]
    d = D_MODEL
    rows = lambda w: pl.BlockSpec((ROW_TILE, w), lambda i: (i, 0))
    lat = pl.BlockSpec((ROW_TILE, BRANCH_W), lambda i: (jnp.maximum(i - 1, 0), 0))
    const = lambda shp: pl.BlockSpec(shp, lambda i: (0,) * len(shp))
    ctx = const((CTX_LEN, BRANCH_W))
    return pl.pallas_call(
        _out_body,
        out_shape=jax.ShapeDtypeStruct((r, d), F32),
        grid=(r // ROW_TILE,),
        in_specs=[rows(d), lat, ctx, lat, ctx, const((BRANCH_W, d)), const((BRANCH_W, d)),
                  pl.BlockSpec((1, 1, d), lambda i: (jnp.minimum(i, 1), 0, 0))],
        out_specs=rows(d),
        compiler_params=_cparams(("parallel",)),
        name="out_proj",
    )(xs, ya_lat, ya_ctx, yb_lat, yb_ctx, wa.astype(BF16), wb.astype(BF16), gate2)


KV_TILE_MAX_BLOCKS = 13


def _kv_tile(r):
    nb = r // ROW_TILE
    best = max(k for k in range(1, KV_TILE_MAX_BLOCKS + 1) if nb % k == 0)
    return best * ROW_TILE, nb // best


def attn_layer(xs, mods, norm_g, w_in, rpb, q_gain, k_gain, w_out, cos_t, sin_t):
    r = xs.shape[0]
    n = r - CTX_LEN
    shift2, scale2, gate2 = mods
    qa, ka, va, sga, qbt, kb, vbt, sgb = attn_in_proj(xs, norm_g, scale2, shift2, w_in, q_gain, k_gain, cos_t, sin_t)
    bias_tabs = _na_bias_tables(rpb, n // GRID_W)
    ya_lat = na_attention(qa, ka, va, sga, bias_tabs, n)
    ya_ctx = flash_mha(qa, ka, va, sga, q_block0=0, nq=1, tk=CTX_LEN, nk=1)
    tk, nk = _kv_tile(r)
    yb_lat = flash_gqa(qbt, kb, vbt, sgb, q_block0=1, nq=n // ROW_TILE, tk=tk, nk=nk)
    yb_ctx = flash_gqa(qbt, kb, vbt, sgb, q_block0=0, nq=1, tk=CTX_LEN, nk=1)
    wb = _reorder_heads(w_out[BRANCH_W:], GQA_HEAD_ORDER, 0)
    return out_proj(xs, ya_lat, ya_ctx, yb_lat, yb_ctx, w_out[:BRANCH_W], wb, gate2)


def _split(a):
    hi = a.astype(BF16)
    return hi, (a - hi.astype(F32)).astype(BF16)


def _dot3(a, b, dims=(((1,), (0,)), ((), ()))):
    ah, al = _split(a)
    bh, bl = _split(b)
    dg = functools.partial(lax.dot_general, dimension_numbers=dims, preferred_element_type=F32)
    return dg(ah, bh) + (dg(al, bh) + dg(ah, bl))


def _dot1(a, b, dims=(((1,), (0,)), ((), ()))):
    return lax.dot_general(a.astype(BF16), b.astype(BF16), dims, preferred_element_type=F32)


_NT = (((1,), (1,)), ((), ()))
_TN = (((0,), (0,)), ((), ()))


RWKV_SHIFT_W = 1664
HY_IN_W = 1536
HALO = 8
REC_HALO_W = RWKV_SHIFT_W + HY_IN_W


def _rec_in_body(x_ref, xp_ref, xn_ref, g_ref, scale_ref, shift_ref, w_ref, mu_ref, taps_ref,
                 rw_ref, hv_ref, hx1_ref, hx2_ref, sgr_ref, sgh_ref, u_scr):
    i = pl.program_id(0)
    nt = pl.num_programs(0)
    xe = jnp.concatenate([xp_ref[...], x_ref[...], xn_ref[...]], axis=0)
    y = xe * lax.rsqrt(jnp.mean(xe * xe, axis=-1, keepdims=True) + NORM_EPS)
    xm = ((y * g_ref[...]) * (1.0 + scale_ref[0]) + shift_ref[0]).astype(BF16)
    u = jnp.dot(xm, w_ref[...], preferred_element_type=F32)
    row = lax.broadcasted_iota(jnp.int32, (ROW_TILE + 2 * HALO, 1), 0)
    keep = jnp.logical_and(jnp.logical_or(row >= HALO, i >= 2),
                           jnp.logical_or(row < ROW_TILE + HALO, jnp.logical_and(i >= 1, i < nt - 1)))
    u_scr[...] = jnp.where(keep, u[:, :REC_HALO_W], 0.0)
    up = u_scr[pl.ds(HALO - 1, ROW_TILE), :]
    uc = u_scr[pl.ds(HALO, ROW_TILE), :]
    un = u_scr[pl.ds(HALO + 1, ROW_TILE), :]
    w = RWKV_SHIFT_W
    rw_c = uc[:, :w]
    rw_ref[...] = rw_c + (0.5 * (up[:, :w] + un[:, :w]) - rw_c) * mu_ref[...]
    hy = up[:, w:] * taps_ref[0:1] + uc[:, w:] * taps_ref[1:2] + un[:, w:] * taps_ref[2:3]
    hv_ref[...] = hy[:, 0:512]
    hx1_ref[...] = hy[:, 512:1024]
    hx2_ref[...] = hy[:, 1024:1536]
    uc_all = u[HALO:HALO + ROW_TILE]
    sgr_ref[...] = _silu(uc_all[:, REC_HALO_W:REC_HALO_W + 512])
    sgh_ref[...] = _silu(uc_all[:, REC_HALO_W + 512:REC_HALO_W + 1024])


def rec_in_proj(xs, norm_g, scale2, shift2, w_in, mu, hy_short):
    r = xs.shape[0]
    d = D_MODEL
    w = RWKV_SHIFT_W
    w_ext = jnp.concatenate([w_in[:, :w], w_in[:, w + 512:w + 512 + HY_IN_W], w_in[:, w:w + 512],
                             w_in[:, w + 512 + HY_IN_W:]], axis=1).astype(BF16)
    nh = r // HALO
    per = ROW_TILE // HALO
    const = lambda shp: pl.BlockSpec(shp, lambda i: (0,) * len(shp))
    rows = lambda wd: pl.BlockSpec((ROW_TILE, wd), lambda i: (i, 0))
    f = lambda wd: jax.ShapeDtypeStruct((r, wd), F32)
    mod = _mod_specs()
    return pl.pallas_call(
        _rec_in_body,
        out_shape=[f(w), f(512), f(512), f(512), f(512), f(512)],
        grid=(r // ROW_TILE,),
        in_specs=[mod[0],
                  pl.BlockSpec((HALO, d), lambda i: (jnp.maximum(i * per - 1, 0), 0)),
                  pl.BlockSpec((HALO, d), lambda i: (jnp.minimum((i + 1) * per, nh - 1), 0)),
                  mod[1], mod[2], mod[3], const((d, w_ext.shape[1])), const((1, w)), const((3, HY_IN_W))],
        out_specs=[rows(w), rows(512), rows(512), rows(512), rows(512), rows(512)],
        scratch_shapes=[pltpu.VMEM((ROW_TILE + 2 * HALO, REC_HALO_W), F32)],
        compiler_params=_cparams(("parallel",)),
        name="rec_in_proj",
    )(xs, xs, xs, norm_g[None], scale2, shift2, w_ext, mu[None], hy_short)


CHUNK = 64
CPT = ROW_TILE // CHUNK


def _block_sum_mat(width, value):
    return jnp.asarray(np.kron(np.eye(width // HEAD_DIM), np.full((HEAD_DIM, HEAD_DIM), value)), F32)


PREP_PAIRS = 2


def _rwkv_prep_body(r_ref, k_ref, v_ref, lora_ref, w0_ref, wup_ref, a0_ref, aup_ref, kk_ref, ka_ref, rk_ref,
                    tri_ref, bs_ref, g_ref, add_ref, bonus_ref):
    t = ROW_TILE
    lora = lora_ref[...]
    bs = bs_ref[...]
    row = lax.broadcasted_iota(jnp.int32, (t, t), 0)
    col = lax.broadcasted_iota(jnp.int32, (t, t), 1)
    same = (row // CHUNK) == (col // CHUNK)
    eye = (row == col).astype(F32)
    wl_all = _dot3(jnp.tanh(lora), wup_ref[...])
    al_all = _dot3(lora, aup_ref[...])
    half = _lane_half((t, 128))
    half_c = _lane_half((HEAD_DIM, 128))
    rowc = lax.broadcasted_iota(jnp.int32, (HEAD_DIM, 128), 0)
    lanec = lax.broadcasted_iota(jnp.int32, (HEAD_DIM, 128), 1)
    level_masks = []
    bsz = 2
    while bsz < CHUNK:
        level_masks.append(jnp.logical_and((row // (2 * bsz)) == (col // (2 * bsz)), (row // bsz) != (col // bsz)))
        bsz *= 2
    first_mask = (row // 2) == (col // 2)

    groups = []
    for q in range(PREP_PAIRS):
        ls = slice(q * 128, (q + 1) * 128)
        r, k, v = r_ref[:, ls], k_ref[:, ls], v_ref[:, ls]
        kk = k * kk_ref[:, ls]
        kk = kk * lax.rsqrt(_dot_stat(kk * kk, bs) + 1e-12)
        kd_sum = None
        for d in range(2):
            ds = slice(q * 256 + d * 128, q * 256 + (d + 1) * 128)
            wl = w0_ref[d, :, ls] + wl_all[:, ds]
            z = -wl
            w_log = -(jnp.maximum(z, 0.0) + jnp.log(1.0 + jnp.exp(-jnp.abs(z)))) - 0.5
            lw = -jnp.exp(w_log)
            a = 1.0 / (1.0 + jnp.exp(-(a0_ref[d, :, ls] + al_all[:, ds])))
            kd = k * (1.0 + (a - 1.0) * ka_ref[:, ls])
            b = kk * a
            kd_sum = kd if kd_sum is None else kd_sum + kd
            incl = jnp.logical_and(same, (col <= row) if d == 0 else (col >= row))
            lw_hi, lw_lo = _split(lw)
            tri = tri_ref[d]
            cs = jnp.dot(tri, lw_hi, preferred_element_type=F32) + jnp.dot(tri, lw_lo, preferred_element_type=F32)
            ends = [c * CHUNK + (CHUNK - 1 if d == 0 else 0) for c in range(CPT)]
            tot = jnp.concatenate([jnp.broadcast_to(cs[e:e + 1], (CHUNK, 128)) for e in ends], axis=0)
            w_inv = jnp.exp(-cs)
            w_rest = jnp.exp(tot - cs)
            groups.append(dict(q=q, d=d, v=v, incl=incl, strict=jnp.logical_and(incl, row != col), tot=tot,
                               kkt=kk * jnp.exp(cs - lw), kh=kd * w_inv, bh=b * w_inv, rt=r * jnp.exp(cs),
                               kdd=kd * w_rest, bdd=b * w_rest))
        bonus_ref[:, ls] = 0.5 * _dot_stat(r * kd_sum * rk_ref[:, ls], bs) * v
    chains = [(gg, j) for gg in groups for j in range(2)]
    sels = [half == j for _, j in chains]
    bms = [jnp.where(sel, gg["bh"], 0.0) for (gg, _), sel in zip(chains, sels)]
    kms = [jnp.where(sel, gg["kh"], 0.0) for (gg, _), sel in zip(chains, sels)]
    l_bs = [jnp.where(gg["strict"], _dot1(gg["kkt"], bm, _NT), 0.0) for (gg, _), bm in zip(chains, bms)]
    tinvs = [eye - jnp.where(first_mask, l_b, 0.0) for l_b in l_bs]
    for mask in level_masks:
        xs = [_dot1(jnp.where(mask, l_b, 0.0), tinv) for l_b, tinv in zip(l_bs, tinvs)]
        tinvs = [tinv - _dot1(tinv, x) for tinv, x in zip(tinvs, xs)]
    l_ks = [jnp.where(gg["strict"], _dot1(gg["kkt"], km, _NT), 0.0) for (gg, _), km in zip(chains, kms)]
    a_rks = [jnp.where(gg["incl"], _dot1(gg["rt"], km, _NT), 0.0) for (gg, _), km in zip(chains, kms)]
    a_rbs = [jnp.where(gg["incl"], _dot1(gg["rt"], bm, _NT), 0.0) for (gg, _), bm in zip(chains, bms)]
    lvs = [_dot1(l_k, gg["v"]) for (gg, _), l_k in zip(chains, l_ks)]
    pus = [_dot1(tinv, jnp.concatenate([gg["kkt"], lv], axis=1)) for (gg, _), tinv, lv in zip(chains, tinvs, lvs)]
    cors = [_dot1(a_rb, pu) for a_rb, pu in zip(a_rbs, pus)]
    ps = [pu[:, :128] for pu in pus]
    u0s = [pu[:, 128:] for pu in pus]
    qs = [gg["rt"] - cor[:, :128] for (gg, _), cor in zip(chains, cors)]
    y0s = [_dot1(a_rk, gg["v"]) - cor[:, 128:] for (gg, _), a_rk, cor in zip(chains, a_rks, cors)]

    sel0 = half == 0
    for n, gg in enumerate(groups):
        d, v = gg["d"], gg["v"]
        ls = slice(gg["q"] * 128, (gg["q"] + 1) * 128)
        p, u0, q, y0 = (jnp.where(sel0, x[2 * n], x[2 * n + 1]) for x in (ps, u0s, qs, y0s))
        for c in range(CPT):
            rs = slice(c * CHUNK, (c + 1) * CHUNK)
            x1 = _dot1(gg["bdd"][rs], p[rs], _TN)
            x2 = _dot1(gg["kdd"][rs], v[rs], _TN) - _dot1(gg["bdd"][rs], u0[rs], _TN)
            m_pair = jnp.where(half_c == 0, x1[:HEAD_DIM], x1[HEAD_DIM:])
            n_pair = jnp.where(half_c == 0, x2[:HEAD_DIM], x2[HEAD_DIM:])
            wc = jnp.exp(gg["tot"][c * CHUNK:c * CHUNK + 1])
            dg = jnp.where((lanec % HEAD_DIM) == rowc, wc, 0.0)
            g_ref[c, d, 0:HEAD_DIM, ls] = dg - m_pair
            g_ref[c, d, HEAD_DIM:, ls] = q[rs]
            add_ref[c, d, 0:HEAD_DIM, ls] = n_pair
            add_ref[c, d, HEAD_DIM:, ls] = y0[rs]


def _lora_ext(up, first_row):
    out = jnp.zeros((2, 128, BRANCH_W), F32)
    for d in range(2):
        out = out.at[d, first_row + 32 * d:first_row + 32 * (d + 1)].set(up[d])
    return out


def rwkv_prep(rw, w0, w_up, a0, a_up, k_k, k_a, r_k):
    r = rw.shape[0]
    nt = r // ROW_TILE
    nch = r // CHUNK
    t = ROW_TILE
    ii = np.arange(t)
    same = (ii[:, None] // CHUNK) == (ii[None, :] // CHUNK)
    tri = jnp.asarray(np.stack([same & (ii[None, :] <= ii[:, None]), same & (ii[None, :] >= ii[:, None])]), BF16)
    pair_cat = lambda w: w.reshape(2, 128, N_HEADS // 2, 128).transpose(1, 2, 0, 3).reshape(128, 2 * BRANCH_W)
    w = 128 * PREP_PAIRS
    steps = N_HEADS // 2 // PREP_PAIRS
    lane = lambda blk: pl.BlockSpec((t, w), lambda i, p, blk=blk: (i, blk + p))
    pvec = pl.BlockSpec((1, w), lambda i, p: (0, p))
    dvec = pl.BlockSpec((2, 1, w), lambda i, p: (0, 0, p))
    dmat = pl.BlockSpec((128, 2 * w), lambda i, p: (0, p))
    gspec = pl.BlockSpec((CPT, 2, HEAD_DIM + CHUNK, w), lambda i, p: (i, 0, 0, p))
    gshape = jax.ShapeDtypeStruct((nch, 2, HEAD_DIM + CHUNK, BRANCH_W), F32)
    return pl.pallas_call(
        _rwkv_prep_body,
        out_shape=[gshape, gshape, jax.ShapeDtypeStruct((r, BRANCH_W), F32)],
        grid=(nt, steps),
        in_specs=[lane(0), lane(steps), lane(2 * steps), pl.BlockSpec((t, 128), lambda i, p: (i, 12)),
                  dvec, dmat, dvec, dmat, pvec, pvec, pvec,
                  pl.BlockSpec((2, t, t), lambda i, p: (0, 0, 0)),
                  pl.BlockSpec((128, 128), lambda i, p: (0, 0))],
        out_specs=[gspec, gspec, pl.BlockSpec((t, w), lambda i, p: (i, p))],
        compiler_params=_cparams(("parallel", "parallel")),
        name="rwkv_prep",
    )(rw, rw, rw, rw, w0.reshape(2, 1, BRANCH_W), pair_cat(_lora_ext(w_up, 0)), a0.reshape(2, 1, BRANCH_W),
      pair_cat(_lora_ext(a_up, 64)), k_k[None], k_a[None], r_k.reshape(1, BRANCH_W), tri,
      _block_sum_mat(128, 1.0).astype(BF16))


def _rwkv_scan_body(gf_ref, af_ref, gb_ref, ab_ref, yf_ref, yb_ref, st_ref):
    @pl.when(pl.program_id(0) == 0)
    def _():
        st_ref[...] = jnp.zeros(st_ref.shape, F32)

    rowh = lax.broadcasted_iota(jnp.int32, (128, 128), 0) // HEAD_DIM
    diag = rowh == _lane_half((128, 128))
    for s in range(CPT):
        for d, (g_ref, a_ref, y_ref, c) in enumerate(((gf_ref, af_ref, yf_ref, s), (gb_ref, ab_ref, yb_ref, CPT - 1 - s))):
            for p in range(N_HEADS // 2):
                ls = slice(p * 128, (p + 1) * 128)
                out = _dot3(g_ref[c, 0, :, ls], st_ref[d, p]) + a_ref[c, 0, :, ls]
                hn = out[:HEAD_DIM]
                st_ref[d, p] = jnp.where(diag, jnp.concatenate([hn, hn], axis=0), 0.0)
                y_ref[c * CHUNK:(c + 1) * CHUNK, ls] = out[HEAD_DIM:]


def rwkv_scan(g, add):
    nch = g.shape[0]
    r = nch * CHUNK
    nt = r // ROW_TILE
    assert CTX_LEN == ROW_TILE
    rev = lambda i: jnp.where(i == 0, 0, nt - i)
    blk = (CPT, 1, HEAD_DIM + CHUNK, BRANCH_W)
    fwd = pl.BlockSpec(blk, lambda i: (i, 0, 0, 0))
    bwd = pl.BlockSpec(blk, lambda i: (rev(i), 1, 0, 0))
    yshape = jax.ShapeDtypeStruct((r, BRANCH_W), F32)
    return pl.pallas_call(
        _rwkv_scan_body,
        out_shape=[yshape, yshape],
        grid=(nt,),
        in_specs=[fwd, fwd, bwd, bwd],
        out_specs=[pl.BlockSpec((ROW_TILE, BRANCH_W), lambda i: (i, 0)),
                   pl.BlockSpec((ROW_TILE, BRANCH_W), lambda i: (rev(i), 0))],
        scratch_shapes=[pltpu.VMEM((2, N_HEADS // 2, 128, 128), F32)],
        compiler_params=_cparams(("arbitrary",)),
        name="rwkv_scan",
    )(g, add, g, add)


HY_WIDTH = 512
HY_ORDER = 2
HY_POS_BANDS = 16
HY_HIDDEN = 64
HY_TAPS_W = 2 * HY_ORDER * HY_WIDTH
FFT_N2 = ROW_TILE


def _dot3c(ah, al, b):
    bh, bl = _split(b)
    dg = functools.partial(jnp.dot, preferred_element_type=F32)
    return dg(ah, bh) + (dg(al, bh) + dg(ah, bl))


def _dotc(ah, b):
    return jnp.dot(ah, b.astype(BF16), preferred_element_type=F32)


def _split_const(m):
    m = np.asarray(m, np.float32)
    hi = m.astype(BF16)
    lo = (m - hi.astype(np.float32)).astype(BF16)
    return jnp.asarray(hi), jnp.asarray(lo)


TAPS_FLAT_COLS = 8


def _filter_taps(t_idx, length, c2pb_ref, w1t_ref, w1c_ref, w1s_ref, b1_ref, w2_ref, b2_ref, w3_ref, b3_ref, absd_ref):
    t = t_idx / float(max(length - 1, 1))
    ang = c2pb_ref[...] * t_idx / float(length)
    pre = t * w1t_ref[...] + _dot3(jnp.cos(ang), w1c_ref[...]) - _dot3(jnp.sin(ang), w1s_ref[...]) + b1_ref[...]
    hid = jnp.sin(pre)
    hid = jnp.sin(_dot3(hid, w2_ref[...]) + b2_ref[...])
    return (_dot3(hid, w3_ref[...]) + b3_ref[...]) * jnp.exp(-t * absd_ref[...])


def _hy_taps_body(*refs, length):
    taps_ref, ssq_ref, tap0_ref = refs[-3:]
    i = pl.program_id(0)
    t_idx = (i * ROW_TILE + lax.broadcasted_iota(jnp.int32, (ROW_TILE, 1), 0)).astype(F32)
    taps = _filter_taps(t_idx, length, *refs[:-3])
    taps_ref[...] = taps

    @pl.when(i == 0)
    def _():
        ssq_ref[...] = jnp.zeros(ssq_ref.shape, F32)
        tap0_ref[...] = taps[0:1]

    ssq_ref[...] += jnp.sum(taps * taps, axis=0, keepdims=True)


def _hy_taps_flat_body(c2pb_ref, w1t_ref, w1c_ref, w1s_ref, b1_ref, w2_ref, b2_ref, w3_ref, b3_ref, absd_ref, f1_ref,
                       planes_ref, ssq_ref, tap0_ref, *, length):
    j = pl.program_id(0)
    rows = length // FFT_N2
    nb = TAPS_FLAT_COLS

    def positions(width, per):
        a = lax.broadcasted_iota(jnp.int32, (rows, width), 0)
        b = lax.broadcasted_iota(jnp.int32, (rows, width), 1) // per
        return (a * FFT_N2 + j * nb + b).astype(F32)

    @pl.when(j == 0)
    def _():
        ssq_ref[...] = jnp.zeros(ssq_ref.shape, F32)

    h = HY_HIDDEN
    ang = c2pb_ref[...] * positions(nb * HY_POS_BANDS, HY_POS_BANDS) / float(length)
    t_h = positions(nb * h, h) / float(max(length - 1, 1))
    pre = t_h * w1t_ref[...] + _dot3(jnp.cos(ang), w1c_ref[...]) - _dot3(jnp.sin(ang), w1s_ref[...]) + b1_ref[...]
    hid = jnp.sin(pre)
    hid = jnp.sin(_dot3(hid, w2_ref[...]) + b2_ref[...])
    for bp in range(nb // 2):
        t_w = (positions(2 * HY_TAPS_W, HY_TAPS_W) + float(2 * bp)) / float(max(length - 1, 1))
        taps = (_dot3(hid[:, bp * 2 * h:(bp + 1) * 2 * h], w3_ref[...]) + b3_ref[...]) * jnp.exp(-t_w * absd_ref[...])
        planes = _dotc(f1_ref[...], taps)
        planes_ref[:, bp * 2 * HY_TAPS_W:(bp + 1) * 2 * HY_TAPS_W] = planes.astype(planes_ref.dtype)
        if bp == 0:
            @pl.when(j == 0)
            def _():
                tap0_ref[...] = taps[0:1, :HY_TAPS_W]
        sq = jnp.sum(taps * taps, axis=0, keepdims=True)
        ssq_ref[...] += sq[:, :HY_TAPS_W] + sq[:, HY_TAPS_W:]


def hyena_taps(length, w1, b1, w2, b2, w3, b3, *, flat):
    bands = jnp.linspace(1e-4, HY_POS_BANDS - 1, HY_POS_BANDS, dtype=F32)
    c2pb = jnp.zeros((1, 128), F32).at[0, :HY_POS_BANDS].set(2.0 * math.pi * bands)
    pad = lambda m: jnp.zeros((128, HY_HIDDEN), F32).at[:HY_POS_BANDS].set(m)
    deltas = jnp.linspace(math.log(1e-2) / 0.3, math.log(1e-2) / 1.5, HY_WIDTH, dtype=F32)
    absd = jnp.tile(jnp.abs(deltas), 2 * HY_ORDER)[None]
    const = lambda shp: pl.BlockSpec(shp, lambda i: (0,) * len(shp))
    w1t, w1c, w1s = w1[0:1], w1[1:1 + HY_POS_BANDS], w1[1 + HY_POS_BANDS:]
    if flat:
        nb = TAPS_FLAT_COLS
        f1 = _FftPlan(length).f1[0]
        rows = f1.shape[0]
        bd = lambda m, k: jnp.kron(jnp.eye(k, dtype=F32), m)
        tile = lambda v, k: jnp.tile(v, k)[None]
        operands = (tile(c2pb[0, :HY_POS_BANDS], nb), tile(w1t[0], nb), bd(w1c, nb), bd(w1s, nb), tile(b1, nb),
                    bd(w2, nb), tile(b2, nb), bd(w3, 2), tile(b3, 2), tile(absd[0], 2), f1)
        body, grid = _hy_taps_flat_body, (FFT_N2 // nb,)
        taps_shape, taps_dtype = (rows, FFT_N2 * HY_TAPS_W), BF16
        taps_spec = pl.BlockSpec((rows, nb * HY_TAPS_W), lambda i: (0, i))
    else:
        operands = (c2pb, w1t, pad(w1c), pad(w1s), b1[None], w2, b2[None], w3, b3[None], absd)
        body, grid = _hy_taps_body, (length // ROW_TILE,)
        taps_shape, taps_dtype = (length, HY_TAPS_W), F32
        taps_spec = pl.BlockSpec((ROW_TILE, HY_TAPS_W), lambda i: (i, 0))
    taps, ssq, tap0 = pl.pallas_call(
        functools.partial(body, length=length),
        out_shape=[jax.ShapeDtypeStruct(taps_shape, taps_dtype), jax.ShapeDtypeStruct((1, HY_TAPS_W), F32),
                   jax.ShapeDtypeStruct((1, HY_TAPS_W), F32)],
        grid=grid,
        in_specs=[const(op.shape) for op in operands],
        out_specs=[taps_spec, const((1, HY_TAPS_W)), const((1, HY_TAPS_W))],
        compiler_params=_cparams(("arbitrary",)),
        name="hyena_taps_flat" if flat else "hyena_taps",
    )(*operands)
    hw = HY_TAPS_W // 2
    norm2 = ssq[:, :hw] + ssq[:, hw:] + 2.0 * tap0[:, :hw] * tap0[:, hw:]
    return taps, lax.rsqrt(norm2)


class _FftPlan:
    def __init__(self, length):
        self.length = length
        self.n = 2 * length
        self.n2 = FFT_N2
        self.n1 = self.n // self.n2
        self.n1h = self.n1 // 2
        k1 = self.n1h + 1
        self.k1p = -(-k1 // 8) * 8
        kk = np.arange(self.k1p)[:, None].astype(np.float64)
        live = (kk < k1)
        nn = np.arange(self.n1h)[None, :].astype(np.float64)
        th = 2.0 * np.pi * kk * nn / self.n1
        self.f1 = _split_const(np.concatenate([np.cos(th) * live, -np.sin(th) * live], axis=0))
        ck = np.where((kk == 0) | (kk == self.n1h), 1.0, 2.0) * live / self.n
        self.g1 = _split_const(np.concatenate([np.cos(th) * ck, -np.sin(th) * ck], axis=0).T)
        m = np.arange(self.n2).astype(np.float64)
        ph = 2.0 * np.pi * np.outer(m, m) / self.n2
        c, s = np.cos(ph), np.sin(ph)
        self.fb = _split_const(np.block([[c, s], [-s, c]]))
        self.fbi = _split_const(np.block([[c, -s], [s, c]]))
        tw = 2.0 * np.pi * kk[:, :, None] * m[None, :, None] / self.n
        self.twc = jnp.asarray(np.cos(tw), F32)
        self.tws = jnp.asarray(np.sin(tw), F32)


FFT_TN = 4096


def _fft_a_body(fh_ref, x_ref, o_ref):
    o_ref[...] = _dotc(fh_ref[...], x_ref[...]).astype(o_ref.dtype)


def fft_stage_a(plan, xf, lead):
    rows_in, m = xf.shape
    rows = 2 * plan.k1p
    tn = min(FFT_TN, m)
    fh = jnp.pad(plan.f1[0], ((0, 0), (lead, 0)))
    fspec = pl.BlockSpec((rows, rows_in), lambda j: (0, 0))
    return pl.pallas_call(
        _fft_a_body,
        out_shape=jax.ShapeDtypeStruct((rows, m), BF16),
        grid=(m // tn,),
        in_specs=[fspec, pl.BlockSpec((rows_in, tn), lambda j: (0, j))],
        out_specs=pl.BlockSpec((rows, tn), lambda j: (0, j)),
        compiler_params=_cparams(("parallel",)),
        name="fft_stage_a",
    )(fh, xf)


FFT_PLANES = 2


def _twiddled(a_ref, twc_ref, tws_ref, p):
    are, aim = a_ref[0, p].astype(F32), a_ref[1, p].astype(F32)
    c, s = twc_ref[p], tws_ref[p]
    return jnp.concatenate([are * c + aim * s, aim * c - are * s], axis=0)


def _fft_filter_b_body(a_ref, twc_ref, tws_ref, fbh_ref, scale_ref, o_ref):
    n2 = FFT_N2
    hw = HY_TAPS_W // 2
    xs = [_dotc(fbh_ref[...], _twiddled(a_ref, twc_ref, tws_ref, p)) for p in range(FFT_PLANES)]
    for p, x in enumerate(xs):
        xre, xim = x[:n2], x[n2:]
        o_ref[0, p] = ((xre[:, :hw] + xre[:, hw:]) * scale_ref[...]).astype(o_ref.dtype)
        o_ref[1, p] = ((xim[:, :hw] - xim[:, hw:]) * scale_ref[...]).astype(o_ref.dtype)


def fft_filter_stage_b(plan, a, scale):
    n2, k1p = plan.n2, plan.k1p
    hw = HY_TAPS_W // 2
    const = lambda shp: pl.BlockSpec(shp, lambda k: (0,) * len(shp))
    return pl.pallas_call(
        _fft_filter_b_body,
        out_shape=jax.ShapeDtypeStruct((2, k1p, n2, hw), BF16),
        grid=(k1p // FFT_PLANES,),
        in_specs=[pl.BlockSpec((2, FFT_PLANES, n2, HY_TAPS_W), lambda k: (0, k, 0, 0)),
                  pl.BlockSpec((FFT_PLANES, n2, 1), lambda k: (k, 0, 0)),
                  pl.BlockSpec((FFT_PLANES, n2, 1), lambda k: (k, 0, 0)),
                  const((2 * n2, 2 * n2)), const((1, hw))],
        out_specs=pl.BlockSpec((2, FFT_PLANES, n2, hw), lambda k: (0, k, 0, 0)),
        compiler_params=_cparams(("parallel",)),
        name="fft_filter_stage_b",
    )(a, plan.twc, plan.tws, plan.fb[0], scale)


def _fft_conv_b_body(a_ref, kf_ref, twc_ref, tws_ref, fbh_ref, fih_ref, o_ref):
    n2 = FFT_N2
    planes = range(FFT_PLANES)
    zs = [_dotc(fbh_ref[...], _twiddled(a_ref, twc_ref, tws_ref, p)) for p in planes]
    ys = []
    for p, z in zip(planes, zs):
        zre, zim = z[:n2], z[n2:]
        kre, kim = kf_ref[0, p].astype(F32), kf_ref[1, p].astype(F32)
        ys.append(jnp.concatenate([zre * kre - zim * kim, zre * kim + zim * kre], axis=0))
    qs = [_dotc(fih_ref[...], y) for y in ys]
    for p, q in zip(planes, qs):
        qre, qim = q[:n2], q[n2:]
        c, s = twc_ref[p], tws_ref[p]
        o_ref[0, p] = (qre * c - qim * s).astype(o_ref.dtype)
        o_ref[1, p] = (qim * c + qre * s).astype(o_ref.dtype)


def fft_conv_stage_b(plan, a, kf, order):
    n2, k1p = plan.n2, plan.k1p
    const = lambda shp: pl.BlockSpec(shp, lambda k: (0,) * len(shp))
    return pl.pallas_call(
        _fft_conv_b_body,
        out_shape=jax.ShapeDtypeStruct((2, k1p, n2, HY_WIDTH), BF16),
        grid=(k1p // FFT_PLANES,),
        in_specs=[pl.BlockSpec((2, FFT_PLANES, n2, HY_WIDTH), lambda k: (0, k, 0, 0)),
                  pl.BlockSpec((2, FFT_PLANES, n2, HY_WIDTH), lambda k: (0, k, 0, order)),
                  pl.BlockSpec((FFT_PLANES, n2, 1), lambda k: (k, 0, 0)),
                  pl.BlockSpec((FFT_PLANES, n2, 1), lambda k: (k, 0, 0)),
                  const((2 * n2, 2 * n2)), const((2 * n2, 2 * n2))],
        out_specs=pl.BlockSpec((2, FFT_PLANES, n2, HY_WIDTH), lambda k: (0, k, 0, 0)),
        compiler_params=_cparams(("parallel",)),
        name="fft_conv_stage_b",
    )(a, kf, plan.twc, plan.tws, plan.fb[0], plan.fbi[0])


def _fft_inv_a_body(gh_ref, fh_ref, q_ref, z_ref, gate_ref, skip_ref, o_ref, *next_ref):
    y = _dotc(gh_ref[...], q_ref[...])
    z_next = gate_ref[...] * (y + z_ref[...] * skip_ref[...])
    o_ref[...] = z_next
    if next_ref:
        next_ref[0][...] = _dotc(fh_ref[...], z_next).astype(next_ref[0].dtype)


def fft_inv_stage_a(plan, qf, zf, gatef, skip_t, lead, with_next):
    rows_out, m = zf.shape
    rows = 2 * plan.k1p
    tn = skip_t.shape[1]
    gh = jnp.pad(plan.g1[0], ((lead, 0), (0, 0)))
    fh = jnp.pad(plan.f1[0], ((0, 0), (lead, 0)))
    col = lambda r_: pl.BlockSpec((r_, tn), lambda j: (0, j))
    out_shape = [jax.ShapeDtypeStruct((rows_out, m), F32)]
    out_specs = [col(rows_out)]
    if with_next:
        out_shape.append(jax.ShapeDtypeStruct((rows, m), BF16))
        out_specs.append(col(rows))
    return pl.pallas_call(
        _fft_inv_a_body,
        out_shape=out_shape,
        grid=(m // tn,),
        in_specs=[pl.BlockSpec((rows_out, rows), lambda j: (0, 0)), pl.BlockSpec((rows, rows_out), lambda j: (0, 0)),
                  col(rows), col(rows_out), col(rows_out), pl.BlockSpec((1, tn), lambda j: (0, 0))],
        out_specs=out_specs,
        compiler_params=_cparams(("parallel",)),
        name="fft_inv_stage_a",
    )(gh, fh, qf, zf, gatef, skip_t)


def hyena_long(hv, hx1, hx2, ta, scale, skip, lead):
    length = hv.shape[0] - lead * FFT_N2
    plan = _FftPlan(length)
    n2, k1p = plan.n2, plan.k1p
    m = n2 * HY_WIDTH
    tn = min(FFT_TN, m)
    flat = lambda a: a.reshape(lead + plan.n1h, m)
    kf = fft_filter_stage_b(plan, ta.reshape(2, k1p, n2, HY_TAPS_W), scale)
    z = flat(hv)
    a = fft_stage_a(plan, z, lead)
    gates = (hx1, hx2)
    for o, gate in enumerate(gates):
        q = fft_conv_stage_b(plan, a.reshape(2, k1p, n2, HY_WIDTH), kf, o)
        last = o == len(gates) - 1
        out = fft_inv_stage_a(plan, q.reshape(2 * k1p, m), z, flat(gate), jnp.tile(skip[o], tn // HY_WIDTH)[None], lead,
                              not last)
        z = out[0]
        if not last:
            a = out[1]
    return z.reshape(-1, HY_WIDTH)


def _rec_out_body(x_ref, yf_ref, yb_ref, bonus_ref, sgr_ref, zh_ref, zc_ref, sgh_ref, gnw_ref, gnb_ref, bm_ref, wa_ref, wb_ref,
                  gate_ref, fin_ref, o_ref, *, final):
    y = yf_ref[...] + yb_ref[...]
    bm = bm_ref[...]
    mean = _dot_stat(y, bm)
    yc = y - mean
    var = _dot_stat(yc * yc, bm)
    yn = yc * lax.rsqrt(var + RWKV_GN_EPS) * gnw_ref[...] + gnb_ref[...]
    ya = ((yn + bonus_ref[...]) * sgr_ref[...]).astype(BF16)
    zh = zh_ref[...]
    if not final:
        zh = jnp.where(pl.program_id(0) == 0, zc_ref[...], zh)
    yh = (zh * sgh_ref[...]).astype(BF16)
    out = jnp.dot(ya, wa_ref[...], preferred_element_type=F32) + jnp.dot(yh, wb_ref[...], preferred_element_type=F32)
    xn = x_ref[...] + gate_ref[0] * out
    if final:
        xn = xn * lax.rsqrt(jnp.mean(xn * xn, axis=-1, keepdims=True) + NORM_EPS) * fin_ref[...]
    o_ref[...] = xn


RWKV_GN_EPS = 64e-5


def rec_out_proj(xs, yf, yb, bonus, sgr, zh, zc, sgh, gn_w, gn_b, w_out, gate2, final_g, *, final):
    r = xs.shape[0]
    d = D_MODEL
    off = 1 if final else 0
    nt = r // ROW_TILE - off
    rows = lambda w: pl.BlockSpec((ROW_TILE, w), lambda i: (i + off, 0))
    const = lambda shp: pl.BlockSpec(shp, lambda i: (0,) * len(shp))
    bw = BRANCH_W
    return pl.pallas_call(
        functools.partial(_rec_out_body, final=final),
        out_shape=jax.ShapeDtypeStruct((nt * ROW_TILE, d), F32),
        grid=(nt,),
        in_specs=[rows(d), rows(bw), rows(bw), rows(bw), rows(bw), rows(bw), const((CTX_LEN, bw)), rows(bw),
                  const((1, bw)), const((1, bw)),
                  const((bw, bw)), const((bw, d)), const((bw, d)),
                  pl.BlockSpec((1, 1, d), lambda i: (jnp.minimum(i + off, 1), 0, 0)), const((1, d))],
        out_specs=pl.BlockSpec((ROW_TILE, d), lambda i: (i, 0)),
        compiler_params=_cparams(("parallel",)),
        name="rec_out_proj",
    )(xs, yf, yb, bonus, sgr, zh, zc, sgh, gn_w[None], gn_b[None], _block_sum_mat(bw, 1.0 / HEAD_DIM).astype(BF16),
      w_out[:bw].astype(BF16), w_out[bw:].astype(BF16), gate2, final_g[None])


def rec_layer(xs, mods, norm_g, w_in, mu, w0, w_up, a0, a_up, k_k, k_a, r_k, gn_w, gn_b, hy_short, hy_w1, hy_b1, hy_w2,
              hy_b2, hy_w3, hy_b3, hy_skip, w_out, final_g, final):
    shift2, scale2, gate2 = mods
    rw, hv, hx1, hx2, sgr, sgh = rec_in_proj(xs, norm_g, scale2, shift2, w_in, mu, hy_short)
    g, add, bonus = rwkv_prep(rw, w0, w_up, a0, a_up, k_k, k_a, r_k)
    yf, yb = rwkv_scan(g, add)
    fargs = (hy_w1, hy_b1, hy_w2, hy_b2, hy_w3, hy_b3)
    n = xs.shape[0] - CTX_LEN
    taps, scale = hyena_taps(n, *fargs, flat=True)
    zh = hyena_long(hv, hx1, hx2, taps, scale, hy_skip, CTX_LEN // FFT_N2)
    if final:
        z_ctx = hv[:CTX_LEN]
    else:
        taps_c, scale_c = hyena_taps(CTX_LEN, *fargs, flat=False)
        z_ctx = hyena_short(hv[:CTX_LEN], hx1[:CTX_LEN], hx2[:CTX_LEN], taps_c, scale_c, hy_skip)
    return rec_out_proj(xs, yf, yb, bonus, sgr, zh, z_ctx, sgh, gn_w, gn_b, w_out, gate2, final_g, final=final)


def kernel(x, c, ctx, c_ctx, attn_norm, attn_ada_w, attn_ada_b, attn_w_in, na_rpb, gqa_q_gain, gqa_k_gain, attn_w_out,
           rec_norm, rec_ada_w, rec_ada_b, rec_w_in, rwkv_mu, rwkv_w0, rwkv_w_up, rwkv_a0, rwkv_a_up, rwkv_k_k, rwkv_k_a,
           rwkv_r_k, rwkv_gn_w, rwkv_gn_b, hy_short, hy_w1, hy_b1, hy_w2, hy_b2, hy_w3, hy_b3, hy_skip, rec_w_out,
           final_norm):
    assert x.shape[0] == 1 and ctx.shape[1] == CTX_LEN and x.shape[2] == D_MODEL
    n = x.shape[1]
    assert n % ROW_TILE == 0 and n // ROW_TILE >= 3
    assert attn_w_in.shape[0] == rec_w_in.shape[0]
    d = D_MODEL
    cond8 = jnp.zeros((8, d), F32).at[0].set(c_ctx).at[1].set(c[0])
    m_attn = adaln_all(cond8, attn_ada_w, attn_ada_b)
    m_rec = adaln_all(cond8, rec_ada_w, rec_ada_b)
    mods = lambda m, i: tuple(m[i, :2, j * d:(j + 1) * d].reshape(2, 1, d) for j in range(3))
    cos_t, sin_t = _rope_tables(n)
    xs = jnp.concatenate([ctx[0], x[0]], axis=0)
    depth = attn_w_in.shape[0] + rec_w_in.shape[0]
    for layer in range(depth):
        i = layer // 2
        final = layer == depth - 1
        if layer % 2 == 0:
            xs = attn_layer(xs, mods(m_attn, i), attn_norm[i], attn_w_in[i], na_rpb[i], gqa_q_gain[i], gqa_k_gain[i],
                            attn_w_out[i], cos_t, sin_t)
        else:
            xs = rec_layer(xs, mods(m_rec, i), rec_norm[i], rec_w_in[i], rwkv_mu[i], rwkv_w0[i], rwkv_w_up[i],
                           rwkv_a0[i], rwkv_a_up[i], rwkv_k_k[i], rwkv_k_a[i], rwkv_r_k[i], rwkv_gn_w[i], rwkv_gn_b[i],
                           hy_short[i], hy_w1[i], hy_b1[i], hy_w2[i], hy_b2[i], hy_w3[i], hy_b3[i], hy_skip[i],
                           rec_w_out[i], final_norm, final)
    return xs[None]


def _hy_short_body(fh_ref, fl_ref, gh_ref, gl_ref, v_ref, x1_ref, x2_ref, taps_ref, scale_ref, skip_ref, o_ref):
    fh, fl, gh, gl = fh_ref[...], fl_ref[...], gh_ref[...], gl_ref[...]
    kp = fh.shape[0] // 2
    hw = HY_TAPS_W // 2
    tf = _dot3c(fh, fl, taps_ref[...])
    kre = (tf[:kp, :hw] + tf[:kp, hw:]) * scale_ref[...]
    kim = (tf[kp:, :hw] - tf[kp:, hw:]) * scale_ref[...]
    z = v_ref[...]
    for o, gate_ref in enumerate((x1_ref, x2_ref)):
        ls = slice(o * HY_WIDTH, (o + 1) * HY_WIDTH)
        zf = _dot3c(fh, fl, z)
        zre, zim = zf[:kp], zf[kp:]
        y = jnp.concatenate([zre * kre[:, ls] - zim * kim[:, ls], zre * kim[:, ls] + zim * kre[:, ls]], axis=0)
        z = gate_ref[...] * (_dot3c(gh, gl, y) + z * skip_ref[o:o + 1])
    o_ref[...] = z


def hyena_short(hv, hx1, hx2, taps, scale, skip):
    length = hv.shape[0]
    n = 2 * length
    k1 = length + 1
    kp = -(-k1 // 8) * 8
    kk = np.arange(kp)[:, None].astype(np.float64)
    live = kk < k1
    th = 2.0 * np.pi * kk * np.arange(length)[None, :] / n
    f = _split_const(np.concatenate([np.cos(th) * live, -np.sin(th) * live], axis=0))
    ck = np.where((kk == 0) | (kk == length), 1.0, 2.0) * live / n
    g = _split_const(np.concatenate([np.cos(th) * ck, -np.sin(th) * ck], axis=0).T)
    return pl.pallas_call(
        _hy_short_body,
        out_shape=jax.ShapeDtypeStruct((length, HY_WIDTH), F32),
        compiler_params=pltpu.CompilerParams(vmem_limit_bytes=VMEM_LIMIT),
        name="hyena_short",
    )(f[0], f[1], g[0], g[1], hv, hx1, hx2, taps, scale, skip)
```

```python
import functools
import math

import jax
import jax.numpy as jnp
import numpy as np
from jax import lax
from jax.experimental import pallas as pl
from jax.experimental.pallas import tpu as pltpu

F32 = jnp.float32
BF16 = jnp.bfloat16
HIGHEST = lax.Precision.HIGHEST

D_MODEL = 1024
GRID_W = 64
CTX_LEN = 256
HEAD_DIM = 64
BRANCH_W = 512
N_HEADS = 8
GQA_KV_W = 128
NA_WIN_ROWS = 8
NA_WIN_COLS = 16
ROPE_THETA = 10000.0
ROPE_FREQS = 16
NORM_EPS = 1e-6
ROW_TILE = 256
NA_GROUP_ROWS = 4
NEG_BIG = -1e30
LOG2E = math.log2(math.e)
QK_SCALE = HEAD_DIM ** -0.5 * LOG2E
FLASH_SCORES_AHEAD = 3
NA_SCORES_AHEAD = 1
VMEM_LIMIT = 56 * 1024 * 1024

ATTN_SPLITS = (512, 512, 512, 512, 512, 128, 128, 512)
GQA_HEAD_ORDER = (0, 4, 1, 5, 2, 6, 3, 7)


def _cparams(sem):
    return pltpu.CompilerParams(dimension_semantics=sem, vmem_limit_bytes=VMEM_LIMIT)


def _silu(v):
    return v * (1.0 / (1.0 + jnp.exp(-v)))


def _lane_half(shape):
    return (lax.broadcasted_iota(jnp.int32, shape, len(shape) - 1) // HEAD_DIM) % 2


def _dot_stat(a, block_mat):
    hi = a.astype(BF16)
    lo = (a - hi.astype(F32)).astype(BF16)
    return jnp.dot(hi, block_mat, preferred_element_type=F32) + jnp.dot(lo, block_mat, preferred_element_type=F32)


def _dot_nt(a, b):
    return lax.dot_general(a, b, (((1,), (1,)), ((), ())), preferred_element_type=F32)


def _adaln_body(cond_ref, w_ref, b_ref, o_ref):
    s = _silu(cond_ref[...])
    o_ref[0] = jnp.dot(s, w_ref[0], precision=HIGHEST, preferred_element_type=F32) + b_ref[0]


def adaln_all(cond8, ada_w, ada_b):
    nl = ada_w.shape[0]
    d = D_MODEL
    return pl.pallas_call(
        _adaln_body,
        out_shape=jax.ShapeDtypeStruct((nl, 8, 3 * d), F32),
        grid=(nl, 3),
        in_specs=[
            pl.BlockSpec((8, d), lambda l, j: (0, 0)),
            pl.BlockSpec((1, d, d), lambda l, j: (l, 0, j)),
            pl.BlockSpec((1, 1, d), lambda l, j: (l, 0, j)),
        ],
        out_specs=pl.BlockSpec((1, 8, d), lambda l, j: (l, 0, j)),
        compiler_params=_cparams(("parallel", "parallel")),
        name="adaln",
    )(cond8, ada_w, ada_b.reshape(nl, 1, 3 * d))


def _modulated(x_ref, g_ref, scale_ref, shift_ref):
    xf = x_ref[...]
    y = xf * lax.rsqrt(jnp.mean(xf * xf, axis=-1, keepdims=True) + NORM_EPS)
    return (y * g_ref[...]) * (1.0 + scale_ref[0]) + shift_ref[0]


def _mod_specs():
    d = D_MODEL
    return [
        pl.BlockSpec((ROW_TILE, d), lambda i: (i, 0)),
        pl.BlockSpec((1, d), lambda i: (0, 0)),
        pl.BlockSpec((1, 1, d), lambda i: (jnp.minimum(i, 1), 0, 0)),
        pl.BlockSpec((1, 1, d), lambda i: (jnp.minimum(i, 1), 0, 0)),
    ]


def _attn_in_body(x_ref, g_ref, scale_ref, shift_ref, w_ref, cos_ref, sin_ref, gq_ref, gqs_ref, gk_ref, gks_ref,
                  bdq_ref, bdk_ref,
                  qa_ref, ka_ref, va_ref, sga_ref, qb_ref, kb_ref, vb_ref, sgb_ref):
    xm = _modulated(x_ref, g_ref, scale_ref, shift_ref).astype(BF16)
    u = jnp.dot(xm, w_ref[...], preferred_element_type=F32)
    qa, ka, va, ga = u[:, 0:512], u[:, 512:1024], u[:, 1024:1536], u[:, 1536:2048]
    qb, kb, vb, gb = u[:, 2048:2560], u[:, 2560:2688], u[:, 2688:2816], u[:, 2816:3328]
    qbs, kbs = u[:, 3328:3840], u[:, 3840:3968]
    scale = QK_SCALE
    qa_ref[...] = (qa * scale).astype(BF16)
    ka_ref[...] = ka.astype(BF16)
    va_ref[...] = va.astype(BF16)
    sga_ref[...] = _silu(ga)
    sgb_ref[...] = _silu(gb)
    vb_ref[...] = jnp.transpose(vb).astype(BF16)
    cos_k, sin_k = cos_ref[...], sin_ref[...]
    cos_q = jnp.concatenate([cos_k] * 4, axis=1)
    sin_q = jnp.concatenate([sin_k] * 4, axis=1)
    rs_q = lax.rsqrt(_dot_stat(qb * qb, bdq_ref[...]) + NORM_EPS)
    rs_k = lax.rsqrt(_dot_stat(kb * kb, bdk_ref[...]) + NORM_EPS)
    qr = rs_q * (qb * gq_ref[...] * cos_q + qbs * gqs_ref[...] * sin_q)
    kr = rs_k * (kb * gk_ref[...] * cos_k + kbs * gks_ref[...] * sin_k)
    qb_ref[...] = jnp.transpose(qr * scale).astype(BF16)
    kb_ref[...] = kr.astype(BF16)


def _rope_tables(n):
    t = jnp.arange(n, dtype=jnp.int32)
    pos = jnp.stack([t // GRID_W, t % GRID_W], axis=-1).astype(F32)
    inv_freq = ROPE_THETA ** (-jnp.arange(ROPE_FREQS, dtype=F32) / ROPE_FREQS)
    ang = pos[:, :, None] * inv_freq
    c, s = jnp.cos(ang), jnp.sin(ang)
    cos64 = jnp.concatenate([c[:, 0], c[:, 0], c[:, 1], c[:, 1]], axis=-1)
    sin64 = jnp.concatenate([-s[:, 0], s[:, 0], -s[:, 1], s[:, 1]], axis=-1)
    cos64 = jnp.concatenate([jnp.ones((CTX_LEN, HEAD_DIM), F32), cos64], axis=0)
    sin64 = jnp.concatenate([jnp.zeros((CTX_LEN, HEAD_DIM), F32), sin64], axis=0)
    return jnp.tile(cos64, (1, 2)), jnp.tile(sin64, (1, 2))


def _reorder_heads(w, order, axis):
    take = lambda h: lax.slice_in_dim(w, h * HEAD_DIM, (h + 1) * HEAD_DIM, axis=axis)
    return jnp.concatenate([take(h) for h in order], axis=axis)


def _swap_rope_halves(w):
    shp = w.shape
    return jnp.flip(w.reshape(shp[:-1] + (shp[-1] // (2 * ROPE_FREQS), 2, ROPE_FREQS)), axis=-2).reshape(shp)


def attn_in_proj(xs, norm_g, scale2, shift2, w_in, q_gain, k_gain, cos_t, sin_t):
    r = xs.shape[0]
    d = D_MODEL
    parts, start = [], 0
    for s in ATTN_SPLITS:
        parts.append(w_in[:, start:start + s])
        start += s
    wqa, wka, wva, wga, wqb, wkb, wvb, wgb = parts
    wqb_p = _reorder_heads(wqb, GQA_HEAD_ORDER, 1)
    wgb_p = _reorder_heads(wgb, GQA_HEAD_ORDER, 1)
    wqb_sw = _swap_rope_halves(wqb_p)
    wkb_sw = _swap_rope_halves(wkb)
    w_ext = jnp.concatenate([wqa, wka, wva, wga, wqb_p, wkb, wvb, wgb_p, wqb_sw, wkb_sw], axis=1).astype(BF16)
    gq = jnp.tile(q_gain, N_HEADS)[None]
    gqs = jnp.tile(_swap_rope_halves(q_gain), N_HEADS)[None]
    gk = jnp.tile(k_gain, 2)[None]
    gks = jnp.tile(_swap_rope_halves(k_gain), 2)[None]
    bdq = jnp.asarray(np.kron(np.eye(N_HEADS), np.full((HEAD_DIM, HEAD_DIM), 1.0 / HEAD_DIM)), BF16)
    bdk = jnp.asarray(np.kron(np.eye(2), np.full((HEAD_DIM, HEAD_DIM), 1.0 / HEAD_DIM)), BF16)
    wcols = w_ext.shape[1]
    const = lambda shp: pl.BlockSpec(shp, lambda i: (0,) * len(shp))
    rows = lambda w: pl.BlockSpec((ROW_TILE, w), lambda i: (i, 0))
    out_shapes = [
        jax.ShapeDtypeStruct((r, 512), BF16), jax.ShapeDtypeStruct((r, 512), BF16), jax.ShapeDtypeStruct((r, 512), BF16),
        jax.ShapeDtypeStruct((r, 512), F32),
        jax.ShapeDtypeStruct((512, r), BF16), jax.ShapeDtypeStruct((r, 128), BF16), jax.ShapeDtypeStruct((128, r), BF16),
        jax.ShapeDtypeStruct((r, 512), F32),
    ]
    cols = lambda w: pl.BlockSpec((w, ROW_TILE), lambda i: (0, i))
    return pl.pallas_call(
        _attn_in_body,
        out_shape=out_shapes,
        grid=(r // ROW_TILE,),
        in_specs=_mod_specs() + [const((d, wcols)), rows(128), rows(128), const((1, 512)), const((1, 512)),
                                 const((1, 128)), const((1, 128)), const((512, 512)), const((128, 128))],
        out_specs=[rows(512), rows(512), rows(512), rows(512), cols(512), rows(128), cols(128), rows(512)],
        compiler_params=_cparams(("parallel",)),
        name="attn_in_proj",
    )(xs, norm_g[None], scale2, shift2, w_ext, cos_t, sin_t, gq, gqs, gk, gks, bdq, bdk)


def _na_cols_body(rpb_ref, sel_ref, neg_ref, o_ref):
    o_ref[...] = jnp.dot(rpb_ref[...], sel_ref[...], precision=HIGHEST, preferred_element_type=F32) + neg_ref[...]


def _na_bias_tables(rpb, rows):
    nrel_r, nrel_c = 2 * NA_WIN_ROWS - 1, 2 * NA_WIN_COLS - 1
    qc = np.arange(GRID_W)[:, None]
    kc = np.arange(GRID_W)[None, :]
    col0 = np.clip(qc - NA_WIN_COLS // 2, 0, GRID_W - NA_WIN_COLS)
    col_ok = (kc >= col0) & (kc < col0 + NA_WIN_COLS)
    rc = kc - qc + NA_WIN_COLS - 1
    sel = np.zeros((128, GRID_W * GRID_W), np.float32)
    sel[np.where(col_ok, rc, 127).reshape(-1), np.arange(GRID_W * GRID_W)] = col_ok.reshape(-1)
    neg = np.where(col_ok, 0.0, NEG_BIG).astype(np.float32).reshape(1, -1)
    rpb2 = jnp.zeros((128, 128), F32).at[:N_HEADS * nrel_r, :nrel_c].set(rpb.reshape(N_HEADS * nrel_r, nrel_c))
    cols = pl.pallas_call(
        _na_cols_body,
        out_shape=jax.ShapeDtypeStruct((128, GRID_W * GRID_W), F32),
        name="na_bias_cols",
    )(rpb2, jnp.asarray(sel), jnp.asarray(neg))
    cols = cols[:N_HEADS * nrel_r].reshape(N_HEADS, nrel_r, GRID_W, GRID_W)
    kh = min(NA_WIN_ROWS, rows)
    g = rows // NA_GROUP_ROWS
    cases = [(0, 0), (NA_GROUP_ROWS, 0), (rows - NA_GROUP_ROWS, NA_GROUP_ROWS * (g - 3))]
    masked = jnp.full((N_HEADS, GRID_W, GRID_W), NEG_BIG, F32)
    tabs = []
    for qr_first, start in cases:
        blocks = []
        for j in range(NA_GROUP_ROWS):
            qr = qr_first + j
            row0 = min(max(qr - kh // 2, 0), rows - kh)
            for i in range(3 * NA_GROUP_ROWS):
                kr = start + i
                blocks.append(cols[:, kr - qr + NA_WIN_ROWS - 1] if row0 <= kr < row0 + kh else masked)
        tab = jnp.stack(blocks, axis=1).reshape(N_HEADS, NA_GROUP_ROWS, 3 * NA_GROUP_ROWS, GRID_W, GRID_W)
        tabs.append(tab.transpose(0, 1, 3, 2, 4).reshape(N_HEADS, ROW_TILE, 3 * ROW_TILE))
    return jnp.stack(tabs) * LOG2E


def _na_body(q_ref, kc_ref, k0_ref, k1_ref, k2_ref, vc_ref, v0_ref, v1_ref, v2_ref, bias_ref, sg_ref, o_ref):
    half = _lane_half((ROW_TILE, 128))
    k_refs = (k0_ref, k1_ref, k2_ref, kc_ref)
    v_refs = (v0_ref, v1_ref, v2_ref, vc_ref)

    def scores(h):
        hp, j = divmod(h, 2)
        ls = slice(hp * 128, (hp + 1) * 128)
        qp = q_ref[:, ls]
        qm = jnp.where(half == j, qp, jnp.zeros_like(qp))
        return [_dot_nt(qm, r[:, ls]) for r in k_refs]

    pending = [scores(h) for h in range(NA_SCORES_AHEAD)]
    outs = []
    for h in range(N_HEADS):
        hp, j = divmod(h, 2)
        ls = slice(hp * 128, (hp + 1) * 128)
        s = pending.pop(0)
        if h + NA_SCORES_AHEAD < N_HEADS:
            pending.append(scores(h + NA_SCORES_AHEAD))
        s_win = jnp.concatenate(s[:3], axis=1) + bias_ref[0, h]
        s_ctx = s[3]
        m = jnp.maximum(jnp.max(s_win, axis=1, keepdims=True), jnp.max(s_ctx, axis=1, keepdims=True))
        p_win = jnp.exp2(s_win - m).astype(BF16)
        p_ctx = jnp.exp2(s_ctx - m).astype(BF16)
        vms = [jnp.where(half == j, r[:, ls], jnp.ones((ROW_TILE, 128), BF16)) for r in v_refs]
        o = jnp.dot(p_ctx, vms[3], preferred_element_type=F32)
        for b in range(3):
            o += jnp.dot(p_win[:, b * ROW_TILE:(b + 1) * ROW_TILE], vms[b], preferred_element_type=F32)
        outs.append(o / pltpu.roll(o, HEAD_DIM, 1))
        if j == 1:
            o_pair = jnp.where(half == 0, outs[h - 1], outs[h])
            o_ref[:, ls] = (o_pair * sg_ref[:, ls]).astype(BF16)


def na_attention(qa, ka, va, sga, bias_tabs, n):
    g = n // ROW_TILE
    w = BRANCH_W

    def kv_spec(off):
        return pl.BlockSpec((ROW_TILE, w), lambda i: (jnp.clip(i - 1, 0, g - 3) + off + 1, 0))

    ctx_spec = pl.BlockSpec((ROW_TILE, w), lambda i: (0, 0))
    q_spec = pl.BlockSpec((ROW_TILE, w), lambda i: (i + 1, 0))
    case = lambda i: jnp.where(i == 0, 0, jnp.where(i == g - 1, 2, 1))
    bias_spec = pl.BlockSpec((1, N_HEADS, ROW_TILE, 3 * ROW_TILE), lambda i: (case(i), 0, 0, 0))
    return pl.pallas_call(
        _na_body,
        out_shape=jax.ShapeDtypeStruct((n, w), BF16),
        grid=(g,),
        in_specs=[q_spec, ctx_spec, kv_spec(0), kv_spec(1), kv_spec(2), ctx_spec, kv_spec(0), kv_spec(1), kv_spec(2),
                  bias_spec, q_spec],
        out_specs=pl.BlockSpec((ROW_TILE, w), lambda i: (i, 0)),
        compiler_params=_cparams(("parallel",)),
        name="na_attention",
    )(qa, ka, ka, ka, ka, va, va, va, va, bias_tabs, sga)


def _flash_mha_body(q_ref, k_ref, v_ref, sg_ref, o_ref, m_ref, acc_ref):
    kv = pl.program_id(1)
    tq = q_ref.shape[0]

    @pl.when(kv == 0)
    def _():
        m_ref[...] = jnp.full(m_ref.shape, NEG_BIG, F32)
        acc_ref[...] = jnp.zeros(acc_ref.shape, F32)

    khalf = _lane_half((k_ref.shape[0], 128))
    for p in range(N_HEADS // 2):
        ls = slice(p * 128, (p + 1) * 128)
        qp = q_ref[:, ls]
        kp = k_ref[:, ls]
        vp = v_ref[:, ls]
        for j in range(2):
            hh = 2 * p + j
            km = jnp.where(khalf == j, kp, jnp.zeros_like(kp))
            vm = jnp.where(khalf == j, vp, jnp.ones_like(vp))
            s = _dot_nt(qp, km)
            m_prev = m_ref[hh]
            m_new = jnp.maximum(m_prev, jnp.max(s, axis=1, keepdims=True))
            alpha = jnp.exp2(m_prev - m_new)
            pr = jnp.exp2(s - m_new[:, :1]).astype(BF16)
            acc_ref[hh] = alpha * acc_ref[hh] + jnp.dot(pr, vm, preferred_element_type=F32)
            m_ref[hh] = m_new

    @pl.when(kv == pl.num_programs(1) - 1)
    def _():
        half = _lane_half((tq, 128))
        for p in range(N_HEADS // 2):
            ls = slice(p * 128, (p + 1) * 128)
            a0, a1 = acc_ref[2 * p], acc_ref[2 * p + 1]
            o0 = a0 / pltpu.roll(a0, HEAD_DIM, 1)
            o1 = a1 / pltpu.roll(a1, HEAD_DIM, 1)
            o_ref[:, ls] = (jnp.where(half == 0, o0, o1) * sg_ref[:, ls]).astype(BF16)


def flash_mha(q, k, v, sg, *, q_block0, nq, tk, nk):
    tq = ROW_TILE
    return pl.pallas_call(
        _flash_mha_body,
        out_shape=jax.ShapeDtypeStruct((nq * tq, BRANCH_W), BF16),
        grid=(nq, nk),
        in_specs=[
            pl.BlockSpec((tq, BRANCH_W), lambda i, j: (i + q_block0, 0)),
            pl.BlockSpec((tk, BRANCH_W), lambda i, j: (j, 0)),
            pl.BlockSpec((tk, BRANCH_W), lambda i, j: (j, 0)),
            pl.BlockSpec((tq, BRANCH_W), lambda i, j: (i + q_block0, 0)),
        ],
        out_specs=pl.BlockSpec((tq, BRANCH_W), lambda i, j: (i, 0)),
        scratch_shapes=[pltpu.VMEM((N_HEADS, tq, 128), F32)] * 2,
        compiler_params=_cparams(("parallel", "arbitrary")),
        name="flash_mha",
    )(q, k, v, sg)


def _flash_gqa_body(qt_ref, k_ref, vt_ref, sg_ref, o_ref, m_ref, acc_ref):
    kv = pl.program_id(1)
    tq = qt_ref.shape[1]
    tk = k_ref.shape[0]

    @pl.when(kv == 0)
    def _():
        m_ref[...] = jnp.full(m_ref.shape, NEG_BIG, F32)
        acc_ref[...] = jnp.zeros(acc_ref.shape, F32)

    khalf = _lane_half((tk, 128))
    vhalf = lax.broadcasted_iota(jnp.int32, (128, tk), 0) // HEAD_DIM
    kb = k_ref[...]
    vt = vt_ref[...]
    kms = [jnp.where(khalf == j, kb, jnp.zeros_like(kb)) for j in range(2)]
    vms = [jnp.where(vhalf == j, vt, jnp.ones_like(vt)) for j in range(2)]

    def scores(hh):
        p, j = divmod(hh, 2)
        return jnp.dot(kms[j], qt_ref[p * 128:(p + 1) * 128, :], preferred_element_type=F32).astype(BF16)

    pending = [scores(hh) for hh in range(FLASH_SCORES_AHEAD)]
    for hh in range(N_HEADS):
        st = pending.pop(0)
        if hh + FLASH_SCORES_AHEAD < N_HEADS:
            pending.append(scores(hh + FLASH_SCORES_AHEAD))
        m_prev = m_ref[hh]
        m_new = jnp.maximum(m_prev, jnp.max(st, axis=0, keepdims=True).astype(F32))
        alpha = jnp.exp2(m_prev - m_new)
        pt = jnp.exp2(st - m_new[0:1].astype(BF16))
        acc_ref[hh] = alpha[0:1] * acc_ref[hh] + jnp.dot(vms[hh % 2], pt, preferred_element_type=F32)
        m_ref[hh] = m_new

    @pl.when(kv == pl.num_programs(1) - 1)
    def _():
        for p in range(N_HEADS // 2):
            ls = slice(p * 128, (p + 1) * 128)
            a0, a1 = acc_ref[2 * p], acc_ref[2 * p + 1]
            ot = jnp.concatenate([a0[:HEAD_DIM] / a0[HEAD_DIM:], a1[HEAD_DIM:] / a1[:HEAD_DIM]], axis=0)
            o_ref[:, ls] = (jnp.transpose(ot) * sg_ref[:, ls]).astype(BF16)


def flash_gqa(qt, k, vt, sg, *, q_block0, nq, tk, nk):
    tq = ROW_TILE
    return pl.pallas_call(
        _flash_gqa_body,
        out_shape=jax.ShapeDtypeStruct((nq * tq, BRANCH_W), BF16),
        grid=(nq, nk),
        in_specs=[
            pl.BlockSpec((BRANCH_W, tq), lambda i, j: (0, i + q_block0)),
            pl.BlockSpec((tk, GQA_KV_W), lambda i, j: (j, 0)),
            pl.BlockSpec((GQA_KV_W, tk), lambda i, j: (0, j)),
            pl.BlockSpec((tq, BRANCH_W), lambda i, j: (i + q_block0, 0)),
        ],
        out_specs=pl.BlockSpec((tq, BRANCH_W), lambda i, j: (i, 0)),
        scratch_shapes=[pltpu.VMEM((N_HEADS, 8, tq), F32), pltpu.VMEM((N_HEADS, 128, tq), F32)],
        compiler_params=_cparams(("parallel", "arbitrary")),
        name="flash_gqa",
    )(qt, k, vt, sg)


def _out_body(x_ref, ya_ref, yac_ref, yb_ref, ybc_ref, wa_ref, wb_ref, gate_ref, o_ref):
    is_ctx = pl.program_id(0) == 0
    ya = jnp.where(is_ctx, yac_ref[...], ya_ref[...])
    yb = jnp.where(is_ctx, ybc_ref[...], yb_ref[...])
    y = jnp.dot(ya, wa_ref[...], preferred_element_type=F32)
    y += jnp.dot(yb, wb_ref[...], preferred_element_type=F32)
    o_ref[...] = x_ref[...] + gate_ref[0] * y


def out_proj(xs, ya_lat, ya_ctx, yb_lat, yb_ctx, wa, wb, gate2):
    r = xs.shape[0]
    d = D_MODEL
    rows = lambda w: pl.BlockSpec((ROW_TILE, w), lambda i: (i, 0))
    lat = pl.BlockSpec((ROW_TILE, BRANCH_W), lambda i: (jnp.maximum(i - 1, 0), 0))
    const = lambda shp: pl.BlockSpec(shp, lambda i: (0,) * len(shp))
    ctx = const((CTX_LEN, BRANCH_W))
    return pl.pallas_call(
        _out_body,
        out_shape=jax.ShapeDtypeStruct((r, d), F32),
        grid=(r // ROW_TILE,),
        in_specs=[rows(d), lat, ctx, lat, ctx, const((BRANCH_W, d)), const((BRANCH_W, d)),
                  pl.BlockSpec((1, 1, d), lambda i: (jnp.minimum(i, 1), 0, 0))],
        out_specs=rows(d),
        compiler_params=_cparams(("parallel",)),
        name="out_proj",
    )(xs, ya_lat, ya_ctx, yb_lat, yb_ctx, wa.astype(BF16), wb.astype(BF16), gate2)


KV_TILE_MAX_BLOCKS = 13


def _kv_tile(r):
    nb = r // ROW_TILE
    best = max(k for k in range(1, KV_TILE_MAX_BLOCKS + 1) if nb % k == 0)
    return best * ROW_TILE, nb // best


def attn_layer(xs, mods, norm_g, w_in, rpb, q_gain, k_gain, w_out, cos_t, sin_t):
    r = xs.shape[0]
    n = r - CTX_LEN
    shift2, scale2, gate2 = mods
    qa, ka, va, sga, qbt, kb, vbt, sgb = attn_in_proj(xs, norm_g, scale2, shift2, w_in, q_gain, k_gain, cos_t, sin_t)
    bias_tabs = _na_bias_tables(rpb, n // GRID_W)
    ya_lat = na_attention(qa, ka, va, sga, bias_tabs, n)
    ya_ctx = flash_mha(qa, ka, va, sga, q_block0=0, nq=1, tk=CTX_LEN, nk=1)
    tk, nk = _kv_tile(r)
    yb_lat = flash_gqa(qbt, kb, vbt, sgb, q_block0=1, nq=n // ROW_TILE, tk=tk, nk=nk)
    yb_ctx = flash_gqa(qbt, kb, vbt, sgb, q_block0=0, nq=1, tk=CTX_LEN, nk=1)
    wb = _reorder_heads(w_out[BRANCH_W:], GQA_HEAD_ORDER, 0)
    return out_proj(xs, ya_lat, ya_ctx, yb_lat, yb_ctx, w_out[:BRANCH_W], wb, gate2)


def _split(a):
    hi = a.astype(BF16)
    return hi, (a - hi.astype(F32)).astype(BF16)


def _dot3(a, b, dims=(((1,), (0,)), ((), ()))):
    ah, al = _split(a)
    bh, bl = _split(b)
    dg = functools.partial(lax.dot_general, dimension_numbers=dims, preferred_element_type=F32)
    return dg(ah, bh) + (dg(al, bh) + dg(ah, bl))


def _dot1(a, b, dims=(((1,), (0,)), ((), ()))):
    return lax.dot_general(a.astype(BF16), b.astype(BF16), dims, preferred_element_type=F32)


_NT = (((1,), (1,)), ((), ()))
_TN = (((0,), (0,)), ((), ()))


RWKV_SHIFT_W = 1664
HY_IN_W = 1536
HALO = 8
REC_HALO_W = RWKV_SHIFT_W + HY_IN_W


def _rec_in_body(x_ref, xp_ref, xn_ref, g_ref, scale_ref, shift_ref, w_ref, mu_ref, taps_ref,
                 rw_ref, hv_ref, hx1_ref, hx2_ref, sgr_ref, sgh_ref, u_scr):
    i = pl.program_id(0)
    nt = pl.num_programs(0)
    xe = jnp.concatenate([xp_ref[...], x_ref[...], xn_ref[...]], axis=0)
    y = xe * lax.rsqrt(jnp.mean(xe * xe, axis=-1, keepdims=True) + NORM_EPS)
    xm = ((y * g_ref[...]) * (1.0 + scale_ref[0]) + shift_ref[0]).astype(BF16)
    u = jnp.dot(xm, w_ref[...], preferred_element_type=F32)
    row = lax.broadcasted_iota(jnp.int32, (ROW_TILE + 2 * HALO, 1), 0)
    keep = jnp.logical_and(jnp.logical_or(row >= HALO, i >= 2),
                           jnp.logical_or(row < ROW_TILE + HALO, jnp.logical_and(i >= 1, i < nt - 1)))
    u_scr[...] = jnp.where(keep, u[:, :REC_HALO_W], 0.0)
    up = u_scr[pl.ds(HALO - 1, ROW_TILE), :]
    uc = u_scr[pl.ds(HALO, ROW_TILE), :]
    un = u_scr[pl.ds(HALO + 1, ROW_TILE), :]
    w = RWKV_SHIFT_W
    rw_c = uc[:, :w]
    rw_ref[...] = rw_c + (0.5 * (up[:, :w] + un[:, :w]) - rw_c) * mu_ref[...]
    hy = up[:, w:] * taps_ref[0:1] + uc[:, w:] * taps_ref[1:2] + un[:, w:] * taps_ref[2:3]
    hv_ref[...] = hy[:, 0:512]
    hx1_ref[...] = hy[:, 512:1024]
    hx2_ref[...] = hy[:, 1024:1536]
    uc_all = u[HALO:HALO + ROW_TILE]
    sgr_ref[...] = _silu(uc_all[:, REC_HALO_W:REC_HALO_W + 512])
    sgh_ref[...] = _silu(uc_all[:, REC_HALO_W + 512:REC_HALO_W + 1024])


def rec_in_proj(xs, norm_g, scale2, shift2, w_in, mu, hy_short):
    r = xs.shape[0]
    d = D_MODEL
    w = RWKV_SHIFT_W
    w_ext = jnp.concatenate([w_in[:, :w], w_in[:, w + 512:w + 512 + HY_IN_W], w_in[:, w:w + 512],
                             w_in[:, w + 512 + HY_IN_W:]], axis=1).astype(BF16)
    nh = r // HALO
    per = ROW_TILE // HALO
    const = lambda shp: pl.BlockSpec(shp, lambda i: (0,) * len(shp))
    rows = lambda wd: pl.BlockSpec((ROW_TILE, wd), lambda i: (i, 0))
    f = lambda wd: jax.ShapeDtypeStruct((r, wd), F32)
    mod = _mod_specs()
    return pl.pallas_call(
        _rec_in_body,
        out_shape=[f(w), f(512), f(512), f(512), f(512), f(512)],
        grid=(r // ROW_TILE,),
        in_specs=[mod[0],
                  pl.BlockSpec((HALO, d), lambda i: (jnp.maximum(i * per - 1, 0), 0)),
                  pl.BlockSpec((HALO, d), lambda i: (jnp.minimum((i + 1) * per, nh - 1), 0)),
                  mod[1], mod[2], mod[3], const((d, w_ext.shape[1])), const((1, w)), const((3, HY_IN_W))],
        out_specs=[rows(w), rows(512), rows(512), rows(512), rows(512), rows(512)],
        scratch_shapes=[pltpu.VMEM((ROW_TILE + 2 * HALO, REC_HALO_W), F32)],
        compiler_params=_cparams(("parallel",)),
        name="rec_in_proj",
    )(xs, xs, xs, norm_g[None], scale2, shift2, w_ext, mu[None], hy_short)


CHUNK = 64
CPT = ROW_TILE // CHUNK


def _block_sum_mat(width, value):
    return jnp.asarray(np.kron(np.eye(width // HEAD_DIM), np.full((HEAD_DIM, HEAD_DIM), value)), F32)


PREP_PAIRS = 2


def _rwkv_prep_body(r_ref, k_ref, v_ref, lora_ref, w0_ref, wup_ref, a0_ref, aup_ref, kk_ref, ka_ref, rk_ref,
                    tri_ref, bs_ref, g_ref, add_ref, bonus_ref):
    t = ROW_TILE
    lora = lora_ref[...]
    bs = bs_ref[...]
    row = lax.broadcasted_iota(jnp.int32, (t, t), 0)
    col = lax.broadcasted_iota(jnp.int32, (t, t), 1)
    same = (row // CHUNK) == (col // CHUNK)
    eye = (row == col).astype(F32)
    wl_all = _dot3(jnp.tanh(lora), wup_ref[...])
    al_all = _dot3(lora, aup_ref[...])
    half = _lane_half((t, 128))
    half_c = _lane_half((HEAD_DIM, 128))
    rowc = lax.broadcasted_iota(jnp.int32, (HEAD_DIM, 128), 0)
    lanec = lax.broadcasted_iota(jnp.int32, (HEAD_DIM, 128), 1)
    level_masks = []
    bsz = 2
    while bsz < CHUNK:
        level_masks.append(jnp.logical_and((row // (2 * bsz)) == (col // (2 * bsz)), (row // bsz) != (col // bsz)))
        bsz *= 2
    first_mask = (row // 2) == (col // 2)

    groups = []
    for q in range(PREP_PAIRS):
        ls = slice(q * 128, (q + 1) * 128)
        r, k, v = r_ref[:, ls], k_ref[:, ls], v_ref[:, ls]
        kk = k * kk_ref[:, ls]
        kk = kk * lax.rsqrt(_dot_stat(kk * kk, bs) + 1e-12)
        kd_sum = None
        for d in range(2):
            ds = slice(q * 256 + d * 128, q * 256 + (d + 1) * 128)
            wl = w0_ref[d, :, ls] + wl_all[:, ds]
            z = -wl
            w_log = -(jnp.maximum(z, 0.0) + jnp.log(1.0 + jnp.exp(-jnp.abs(z)))) - 0.5
            lw = -jnp.exp(w_log)
            a = 1.0 / (1.0 + jnp.exp(-(a0_ref[d, :, ls] + al_all[:, ds])))
            kd = k * (1.0 + (a - 1.0) * ka_ref[:, ls])
            b = kk * a
            kd_sum = kd if kd_sum is None else kd_sum + kd
            incl = jnp.logical_and(same, (col <= row) if d == 0 else (col >= row))
            lw_hi, lw_lo = _split(lw)
            tri = tri_ref[d]
            cs = jnp.dot(tri, lw_hi, preferred_element_type=F32) + jnp.dot(tri, lw_lo, preferred_element_type=F32)
            ends = [c * CHUNK + (CHUNK - 1 if d == 0 else 0) for c in range(CPT)]
            tot = jnp.concatenate([jnp.broadcast_to(cs[e:e + 1], (CHUNK, 128)) for e in ends], axis=0)
            w_inv = jnp.exp(-cs)
            w_rest = jnp.exp(tot - cs)
            groups.append(dict(q=q, d=d, v=v, incl=incl, strict=jnp.logical_and(incl, row != col), tot=tot,
                               kkt=kk * jnp.exp(cs - lw), kh=kd * w_inv, bh=b * w_inv, rt=r * jnp.exp(cs),
                               kdd=kd * w_rest, bdd=b * w_rest))
        bonus_ref[:, ls] = 0.5 * _dot_stat(r * kd_sum * rk_ref[:, ls], bs) * v
    chains = [(gg, j) for gg in groups for j in range(2)]
    sels = [half == j for _, j in chains]
    bms = [jnp.where(sel, gg["bh"], 0.0) for (gg, _), sel in zip(chains, sels)]
    kms = [jnp.where(sel, gg["kh"], 0.0) for (gg, _), sel in zip(chains, sels)]
    l_bs = [jnp.where(gg["strict"], _dot1(gg["kkt"], bm, _NT), 0.0) for (gg, _), bm in zip(chains, bms)]
    tinvs = [eye - jnp.where(first_mask, l_b, 0.0) for l_b in l_bs]
    for mask in level_masks:
        xs = [_dot1(jnp.where(mask, l_b, 0.0), tinv) for l_b, tinv in zip(l_bs, tinvs)]
        tinvs = [tinv - _dot1(tinv, x) for tinv, x in zip(tinvs, xs)]
    l_ks = [jnp.where(gg["strict"], _dot1(gg["kkt"], km, _NT), 0.0) for (gg, _), km in zip(chains, kms)]
    a_rks = [jnp.where(gg["incl"], _dot1(gg["rt"], km, _NT), 0.0) for (gg, _), km in zip(chains, kms)]
    a_rbs = [jnp.where(gg["incl"], _dot1(gg["rt"], bm, _NT), 0.0) for (gg, _), bm in zip(chains, bms)]
    lvs = [_dot1(l_k, gg["v"]) for (gg, _), l_k in zip(chains, l_ks)]
    pus = [_dot1(tinv, jnp.concatenate([gg["kkt"], lv], axis=1)) for (gg, _), tinv, lv in zip(chains, tinvs, lvs)]
    cors = [_dot1(a_rb, pu) for a_rb, pu in zip(a_rbs, pus)]
    ps = [pu[:, :128] for pu in pus]
    u0s = [pu[:, 128:] for pu in pus]
    qs = [gg["rt"] - cor[:, :128] for (gg, _), cor in zip(chains, cors)]
    y0s = [_dot1(a_rk, gg["v"]) - cor[:, 128:] for (gg, _), a_rk, cor in zip(chains, a_rks, cors)]

    sel0 = half == 0
    for n, gg in enumerate(groups):
        d, v = gg["d"], gg["v"]
        ls = slice(gg["q"] * 128, (gg["q"] + 1) * 128)
        p, u0, q, y0 = (jnp.where(sel0, x[2 * n], x[2 * n + 1]) for x in (ps, u0s, qs, y0s))
        for c in range(CPT):
            rs = slice(c * CHUNK, (c + 1) * CHUNK)
            x1 = _dot1(gg["bdd"][rs], p[rs], _TN)
            x2 = _dot1(gg["kdd"][rs], v[rs], _TN) - _dot1(gg["bdd"][rs], u0[rs], _TN)
            m_pair = jnp.where(half_c == 0, x1[:HEAD_DIM], x1[HEAD_DIM:])
            n_pair = jnp.where(half_c == 0, x2[:HEAD_DIM], x2[HEAD_DIM:])
            wc = jnp.exp(gg["tot"][c * CHUNK:c * CHUNK + 1])
            dg = jnp.where((lanec % HEAD_DIM) == rowc, wc, 0.0)
            g_ref[c, d, 0:HEAD_DIM, ls] = dg - m_pair
            g_ref[c, d, HEAD_DIM:, ls] = q[rs]
            add_ref[c, d, 0:HEAD_DIM, ls] = n_pair
            add_ref[c, d, HEAD_DIM:, ls] = y0[rs]


def _lora_ext(up, first_row):
    out = jnp.zeros((2, 128, BRANCH_W), F32)
    for d in range(2):
        out = out.at[d, first_row + 32 * d:first_row + 32 * (d + 1)].set(up[d])
    return out


def rwkv_prep(rw, w0, w_up, a0, a_up, k_k, k_a, r_k):
    r = rw.shape[0]
    nt = r // ROW_TILE
    nch = r // CHUNK
    t = ROW_TILE
    ii = np.arange(t)
    same = (ii[:, None] // CHUNK) == (ii[None, :] // CHUNK)
    tri = jnp.asarray(np.stack([same & (ii[None, :] <= ii[:, None]), same & (ii[None, :] >= ii[:, None])]), BF16)
    pair_cat = lambda w: w.reshape(2, 128, N_HEADS // 2, 128).transpose(1, 2, 0, 3).reshape(128, 2 * BRANCH_W)
    w = 128 * PREP_PAIRS
    steps = N_HEADS // 2 // PREP_PAIRS
    lane = lambda blk: pl.BlockSpec((t, w), lambda i, p, blk=blk: (i, blk + p))
    pvec = pl.BlockSpec((1, w), lambda i, p: (0, p))
    dvec = pl.BlockSpec((2, 1, w), lambda i, p: (0, 0, p))
    dmat = pl.BlockSpec((128, 2 * w), lambda i, p: (0, p))
    gspec = pl.BlockSpec((CPT, 2, HEAD_DIM + CHUNK, w), lambda i, p: (i, 0, 0, p))
    gshape = jax.ShapeDtypeStruct((nch, 2, HEAD_DIM + CHUNK, BRANCH_W), F32)
    return pl.pallas_call(
        _rwkv_prep_body,
        out_shape=[gshape, gshape, jax.ShapeDtypeStruct((r, BRANCH_W), F32)],
        grid=(nt, steps),
        in_specs=[lane(0), lane(steps), lane(2 * steps), pl.BlockSpec((t, 128), lambda i, p: (i, 12)),
                  dvec, dmat, dvec, dmat, pvec, pvec, pvec,
                  pl.BlockSpec((2, t, t), lambda i, p: (0, 0, 0)),
                  pl.BlockSpec((128, 128), lambda i, p: (0, 0))],
        out_specs=[gspec, gspec, pl.BlockSpec((t, w), lambda i, p: (i, p))],
        compiler_params=_cparams(("parallel", "parallel")),
        name="rwkv_prep",
    )(rw, rw, rw, rw, w0.reshape(2, 1, BRANCH_W), pair_cat(_lora_ext(w_up, 0)), a0.reshape(2, 1, BRANCH_W),
      pair_cat(_lora_ext(a_up, 64)), k_k[None], k_a[None], r_k.reshape(1, BRANCH_W), tri,
      _block_sum_mat(128, 1.0).astype(BF16))


def _rwkv_scan_body(gf_ref, af_ref, gb_ref, ab_ref, yf_ref, yb_ref, st_ref):
    @pl.when(pl.program_id(0) == 0)
    def _():
        st_ref[...] = jnp.zeros(st_ref.shape, F32)

    rowh = lax.broadcasted_iota(jnp.int32, (128, 128), 0) // HEAD_DIM
    diag = rowh == _lane_half((128, 128))
    for s in range(CPT):
        for d, (g_ref, a_ref, y_ref, c) in enumerate(((gf_ref, af_ref, yf_ref, s), (gb_ref, ab_ref, yb_ref, CPT - 1 - s))):
            for p in range(N_HEADS // 2):
                ls = slice(p * 128, (p + 1) * 128)
                out = _dot3(g_ref[c, 0, :, ls], st_ref[d, p]) + a_ref[c, 0, :, ls]
                hn = out[:HEAD_DIM]
                st_ref[d, p] = jnp.where(diag, jnp.concatenate([hn, hn], axis=0), 0.0)
                y_ref[c * CHUNK:(c + 1) * CHUNK, ls] = out[HEAD_DIM:]


def rwkv_scan(g, add):
    nch = g.shape[0]
    r = nch * CHUNK
    nt = r // ROW_TILE
    assert CTX_LEN == ROW_TILE
    rev = lambda i: jnp.where(i == 0, 0, nt - i)
    blk = (CPT, 1, HEAD_DIM + CHUNK, BRANCH_W)
    fwd = pl.BlockSpec(blk, lambda i: (i, 0, 0, 0))
    bwd = pl.BlockSpec(blk, lambda i: (rev(i), 1, 0, 0))
    yshape = jax.ShapeDtypeStruct((r, BRANCH_W), F32)
    return pl.pallas_call(
        _rwkv_scan_body,
        out_shape=[yshape, yshape],
        grid=(nt,),
        in_specs=[fwd, fwd, bwd, bwd],
        out_specs=[pl.BlockSpec((ROW_TILE, BRANCH_W), lambda i: (i, 0)),
                   pl.BlockSpec((ROW_TILE, BRANCH_W), lambda i: (rev(i), 0))],
        scratch_shapes=[pltpu.VMEM((2, N_HEADS // 2, 128, 128), F32)],
        compiler_params=_cparams(("arbitrary",)),
        name="rwkv_scan",
    )(g, add, g, add)


HY_WIDTH = 512
HY_ORDER = 2
HY_POS_BANDS = 16
HY_HIDDEN = 64
HY_TAPS_W = 2 * HY_ORDER * HY_WIDTH
FFT_N2 = ROW_TILE


def _dot3c(ah, al, b):
    bh, bl = _split(b)
    dg = functools.partial(jnp.dot, preferred_element_type=F32)
    return dg(ah, bh) + (dg(al, bh) + dg(ah, bl))


def _dotc(ah, b):
    return jnp.dot(ah, b.astype(BF16), preferred_element_type=F32)


def _split_const(m):
    m = np.asarray(m, np.float32)
    hi = m.astype(BF16)
    lo = (m - hi.astype(np.float32)).astype(BF16)
    return jnp.asarray(hi), jnp.asarray(lo)


TAPS_FLAT_COLS = 8


def _filter_taps(t_idx, length, c2pb_ref, w1t_ref, w1c_ref, w1s_ref, b1_ref, w2_ref, b2_ref, w3_ref, b3_ref, absd_ref):
    t = t_idx / float(max(length - 1, 1))
    ang = c2pb_ref[...] * t_idx / float(length)
    pre = t * w1t_ref[...] + _dot3(jnp.cos(ang), w1c_ref[...]) - _dot3(jnp.sin(ang), w1s_ref[...]) + b1_ref[...]
    hid = jnp.sin(pre)
    hid = jnp.sin(_dot3(hid, w2_ref[...]) + b2_ref[...])
    return (_dot3(hid, w3_ref[...]) + b3_ref[...]) * jnp.exp(-t * absd_ref[...])


def _hy_taps_body(*refs, length):
    taps_ref, ssq_ref, tap0_ref = refs[-3:]
    i = pl.program_id(0)
    t_idx = (i * ROW_TILE + lax.broadcasted_iota(jnp.int32, (ROW_TILE, 1), 0)).astype(F32)
    taps = _filter_taps(t_idx, length, *refs[:-3])
    taps_ref[...] = taps

    @pl.when(i == 0)
    def _():
        ssq_ref[...] = jnp.zeros(ssq_ref.shape, F32)
        tap0_ref[...] = taps[0:1]

    ssq_ref[...] += jnp.sum(taps * taps, axis=0, keepdims=True)


def _hy_taps_flat_body(c2pb_ref, w1t_ref, w1c_ref, w1s_ref, b1_ref, w2_ref, b2_ref, w3_ref, b3_ref, absd_ref, f1_ref,
                       planes_ref, ssq_ref, tap0_ref, *, length):
    j = pl.program_id(0)
    rows = length // FFT_N2
    nb = TAPS_FLAT_COLS

    def positions(width, per):
        a = lax.broadcasted_iota(jnp.int32, (rows, width), 0)
        b = lax.broadcasted_iota(jnp.int32, (rows, width), 1) // per
        return (a * FFT_N2 + j * nb + b).astype(F32)

    @pl.when(j == 0)
    def _():
        ssq_ref[...] = jnp.zeros(ssq_ref.shape, F32)

    h = HY_HIDDEN
    ang = c2pb_ref[...] * positions(nb * HY_POS_BANDS, HY_POS_BANDS) / float(length)
    t_h = positions(nb * h, h) / float(max(length - 1, 1))
    pre = t_h * w1t_ref[...] + _dot3(jnp.cos(ang), w1c_ref[...]) - _dot3(jnp.sin(ang), w1s_ref[...]) + b1_ref[...]
    hid = jnp.sin(pre)
    hid = jnp.sin(_dot3(hid, w2_ref[...]) + b2_ref[...])
    for bp in range(nb // 2):
        t_w = (positions(2 * HY_TAPS_W, HY_TAPS_W) + float(2 * bp)) / float(max(length - 1, 1))
        taps = (_dot3(hid[:, bp * 2 * h:(bp + 1) * 2 * h], w3_ref[...]) + b3_ref[...]) * jnp.exp(-t_w * absd_ref[...])
        planes = _dotc(f1_ref[...], taps)
        planes_ref[:, bp * 2 * HY_TAPS_W:(bp + 1) * 2 * HY_TAPS_W] = planes.astype(planes_ref.dtype)
        if bp == 0:
            @pl.when(j == 0)
            def _():
                tap0_ref[...] = taps[0:1, :HY_TAPS_W]
        sq = jnp.sum(taps * taps, axis=0, keepdims=True)
        ssq_ref[...] += sq[:, :HY_TAPS_W] + sq[:, HY_TAPS_W:]


def hyena_taps(length, w1, b1, w2, b2, w3, b3, *, flat):
    bands = jnp.linspace(1e-4, HY_POS_BANDS - 1, HY_POS_BANDS, dtype=F32)
    c2pb = jnp.zeros((1, 128), F32).at[0, :HY_POS_BANDS].set(2.0 * math.pi * bands)
    pad = lambda m: jnp.zeros((128, HY_HIDDEN), F32).at[:HY_POS_BANDS].set(m)
    deltas = jnp.linspace(math.log(1e-2) / 0.3, math.log(1e-2) / 1.5, HY_WIDTH, dtype=F32)
    absd = jnp.tile(jnp.abs(deltas), 2 * HY_ORDER)[None]
    const = lambda shp: pl.BlockSpec(shp, lambda i: (0,) * len(shp))
    w1t, w1c, w1s = w1[0:1], w1[1:1 + HY_POS_BANDS], w1[1 + HY_POS_BANDS:]
    if flat:
        nb = TAPS_FLAT_COLS
        f1 = _FftPlan(length).f1[0]
        rows = f1.shape[0]
        bd = lambda m, k: jnp.kron(jnp.eye(k, dtype=F32), m)
        tile = lambda v, k: jnp.tile(v, k)[None]
        operands = (tile(c2pb[0, :HY_POS_BANDS], nb), tile(w1t[0], nb), bd(w1c, nb), bd(w1s, nb), tile(b1, nb),
                    bd(w2, nb), tile(b2, nb), bd(w3, 2), tile(b3, 2), tile(absd[0], 2), f1)
        body, grid = _hy_taps_flat_body, (FFT_N2 // nb,)
        taps_shape, taps_dtype = (rows, FFT_N2 * HY_TAPS_W), BF16
        taps_spec = pl.BlockSpec((rows, nb * HY_TAPS_W), lambda i: (0, i))
    else:
        operands = (c2pb, w1t, pad(w1c), pad(w1s), b1[None], w2, b2[None], w3, b3[None], absd)
        body, grid = _hy_taps_body, (length // ROW_TILE,)
        taps_shape, taps_dtype = (length, HY_TAPS_W), F32
        taps_spec = pl.BlockSpec((ROW_TILE, HY_TAPS_W), lambda i: (i, 0))
    taps, ssq, tap0 = pl.pallas_call(
        functools.partial(body, length=length),
        out_shape=[jax.ShapeDtypeStruct(taps_shape, taps_dtype), jax.ShapeDtypeStruct((1, HY_TAPS_W), F32),
                   jax.ShapeDtypeStruct((1, HY_TAPS_W), F32)],
        grid=grid,
        in_specs=[const(op.shape) for op in operands],
        out_specs=[taps_spec, const((1, HY_TAPS_W)), const((1, HY_TAPS_W))],
        compiler_params=_cparams(("arbitrary",)),
        name="hyena_taps_flat" if flat else "hyena_taps",
    )(*operands)
    hw = HY_TAPS_W // 2
    norm2 = ssq[:, :hw] + ssq[:, hw:] + 2.0 * tap0[:, :hw] * tap0[:, hw:]
    return taps, lax.rsqrt(norm2)


class _FftPlan:
    def __init__(self, length):
        self.length = length
        self.n = 2 * length
        self.n2 = FFT_N2
        self.n1 = self.n // self.n2
        self.n1h = self.n1 // 2
        k1 = self.n1h + 1
        self.k1p = -(-k1 // 8) * 8
        kk = np.arange(self.k1p)[:, None].astype(np.float64)
        live = (kk < k1)
        nn = np.arange(self.n1h)[None, :].astype(np.float64)
        th = 2.0 * np.pi * kk * nn / self.n1
        self.f1 = _split_const(np.concatenate([np.cos(th) * live, -np.sin(th) * live], axis=0))
        ck = np.where((kk == 0) | (kk == self.n1h), 1.0, 2.0) * live / self.n
        self.g1 = _split_const(np.concatenate([np.cos(th) * ck, -np.sin(th) * ck], axis=0).T)
        m = np.arange(self.n2).astype(np.float64)
        ph = 2.0 * np.pi * np.outer(m, m) / self.n2
        c, s = np.cos(ph), np.sin(ph)
        self.fb = _split_const(np.block([[c, s], [-s, c]]))
        self.fbi = _split_const(np.block([[c, -s], [s, c]]))
        tw = 2.0 * np.pi * kk[:, :, None] * m[None, :, None] / self.n
        self.twc = jnp.asarray(np.cos(tw), F32)
        self.tws = jnp.asarray(np.sin(tw), F32)


FFT_TN = 4096


def _fft_a_body(fh_ref, x_ref, o_ref):
    o_ref[...] = _dotc(fh_ref[...], x_ref[...]).astype(o_ref.dtype)


def fft_stage_a(plan, xf, lead):
    rows_in, m = xf.shape
    rows = 2 * plan.k1p
    tn = min(FFT_TN, m)
    fh = jnp.pad(plan.f1[0], ((0, 0), (lead, 0)))
    fspec = pl.BlockSpec((rows, rows_in), lambda j: (0, 0))
    return pl.pallas_call(
        _fft_a_body,
        out_shape=jax.ShapeDtypeStruct((rows, m), BF16),
        grid=(m // tn,),
        in_specs=[fspec, pl.BlockSpec((rows_in, tn), lambda j: (0, j))],
        out_specs=pl.BlockSpec((rows, tn), lambda j: (0, j)),
        compiler_params=_cparams(("parallel",)),
        name="fft_stage_a",
    )(fh, xf)


FFT_PLANES = 2


def _twiddled(a_ref, twc_ref, tws_ref, p):
    are, aim = a_ref[0, p].astype(F32), a_ref[1, p].astype(F32)
    c, s = twc_ref[p], tws_ref[p]
    return jnp.concatenate([are * c + aim * s, aim * c - are * s], axis=0)


def _fft_filter_b_body(a_ref, twc_ref, tws_ref, fbh_ref, scale_ref, o_ref):
    n2 = FFT_N2
    hw = HY_TAPS_W // 2
    xs = [_dotc(fbh_ref[...], _twiddled(a_ref, twc_ref, tws_ref, p)) for p in range(FFT_PLANES)]
    for p, x in enumerate(xs):
        xre, xim = x[:n2], x[n2:]
        o_ref[0, p] = ((xre[:, :hw] + xre[:, hw:]) * scale_ref[...]).astype(o_ref.dtype)
        o_ref[1, p] = ((xim[:, :hw] - xim[:, hw:]) * scale_ref[...]).astype(o_ref.dtype)


def fft_filter_stage_b(plan, a, scale):
    n2, k1p = plan.n2, plan.k1p
    hw = HY_TAPS_W // 2
    const = lambda shp: pl.BlockSpec(shp, lambda k: (0,) * len(shp))
    return pl.pallas_call(
        _fft_filter_b_body,
        out_shape=jax.ShapeDtypeStruct((2, k1p, n2, hw), BF16),
        grid=(k1p // FFT_PLANES,),
        in_specs=[pl.BlockSpec((2, FFT_PLANES, n2, HY_TAPS_W), lambda k: (0, k, 0, 0)),
                  pl.BlockSpec((FFT_PLANES, n2, 1), lambda k: (k, 0, 0)),
                  pl.BlockSpec((FFT_PLANES, n2, 1), lambda k: (k, 0, 0)),
                  const((2 * n2, 2 * n2)), const((1, hw))],
        out_specs=pl.BlockSpec((2, FFT_PLANES, n2, hw), lambda k: (0, k, 0, 0)),
        compiler_params=_cparams(("parallel",)),
        name="fft_filter_stage_b",
    )(a, plan.twc, plan.tws, plan.fb[0], scale)


def _fft_conv_b_body(a_ref, kf_ref, twc_ref, tws_ref, fbh_ref, fih_ref, o_ref):
    n2 = FFT_N2
    planes = range(FFT_PLANES)
    zs = [_dotc(fbh_ref[...], _twiddled(a_ref, twc_ref, tws_ref, p)) for p in planes]
    ys = []
    for p, z in zip(planes, zs):
        zre, zim = z[:n2], z[n2:]
        kre, kim = kf_ref[0, p].astype(F32), kf_ref[1, p].astype(F32)
        ys.append(jnp.concatenate([zre * kre - zim * kim, zre * kim + zim * kre], axis=0))
    qs = [_dotc(fih_ref[...], y) for y in ys]
    for p, q in zip(planes, qs):
        qre, qim = q[:n2], q[n2:]
        c, s = twc_ref[p], tws_ref[p]
        o_ref[0, p] = (qre * c - qim * s).astype(o_ref.dtype)
        o_ref[1, p] = (qim * c + qre * s).astype(o_ref.dtype)


def fft_conv_stage_b(plan, a, kf, order):
    n2, k1p = plan.n2, plan.k1p
    const = lambda shp: pl.BlockSpec(shp, lambda k: (0,) * len(shp))
    return pl.pallas_call(
        _fft_conv_b_body,
        out_shape=jax.ShapeDtypeStruct((2, k1p, n2, HY_WIDTH), BF16),
        grid=(k1p // FFT_PLANES,),
        in_specs=[pl.BlockSpec((2, FFT_PLANES, n2, HY_WIDTH), lambda k: (0, k, 0, 0)),
                  pl.BlockSpec((2, FFT_PLANES, n2, HY_WIDTH), lambda k: (0, k, 0, order)),
                  pl.BlockSpec((FFT_PLANES, n2, 1), lambda k: (k, 0, 0)),
                  pl.BlockSpec((FFT_PLANES, n2, 1), lambda k: (k, 0, 0)),
                  const((2 * n2, 2 * n2)), const((2 * n2, 2 * n2))],
        out_specs=pl.BlockSpec((2, FFT_PLANES, n2, HY_WIDTH), lambda k: (0, k, 0, 0)),
        compiler_params=_cparams(("parallel",)),
        name="fft_conv_stage_b",
    )(a, kf, plan.twc, plan.tws, plan.fb[0], plan.fbi[0])


def _fft_inv_a_body(gh_ref, fh_ref, q_ref, z_ref, gate_ref, skip_ref, o_ref, *next_ref):
    y = _dotc(gh_ref[...], q_ref[...])
    z_next = gate_ref[...] * (y + z_ref[...] * skip_ref[...])
    o_ref[...] = z_next
    if next_ref:
        next_ref[0][...] = _dotc(fh_ref[...], z_next).astype(next_ref[0].dtype)


def fft_inv_stage_a(plan, qf, zf, gatef, skip_t, lead, with_next):
    rows_out, m = zf.shape
    rows = 2 * plan.k1p
    tn = skip_t.shape[1]
    gh = jnp.pad(plan.g1[0], ((lead, 0), (0, 0)))
    fh = jnp.pad(plan.f1[0], ((0, 0), (lead, 0)))
    col = lambda r_: pl.BlockSpec((r_, tn), lambda j: (0, j))
    out_shape = [jax.ShapeDtypeStruct((rows_out, m), F32)]
    out_specs = [col(rows_out)]
    if with_next:
        out_shape.append(jax.ShapeDtypeStruct((rows, m), BF16))
        out_specs.append(col(rows))
    return pl.pallas_call(
        _fft_inv_a_body,
        out_shape=out_shape,
        grid=(m // tn,),
        in_specs=[pl.BlockSpec((rows_out, rows), lambda j: (0, 0)), pl.BlockSpec((rows, rows_out), lambda j: (0, 0)),
                  col(rows), col(rows_out), col(rows_out), pl.BlockSpec((1, tn), lambda j: (0, 0))],
        out_specs=out_specs,
        compiler_params=_cparams(("parallel",)),
        name="fft_inv_stage_a",
    )(gh, fh, qf, zf, gatef, skip_t)


def hyena_long(hv, hx1, hx2, ta, scale, skip, lead):
    length = hv.shape[0] - lead * FFT_N2
    plan = _FftPlan(length)
    n2, k1p = plan.n2, plan.k1p
    m = n2 * HY_WIDTH
    tn = min(FFT_TN, m)
    flat = lambda a: a.reshape(lead + plan.n1h, m)
    kf = fft_filter_stage_b(plan, ta.reshape(2, k1p, n2, HY_TAPS_W), scale)
    z = flat(hv)
    a = fft_stage_a(plan, z, lead)
    gates = (hx1, hx2)
    for o, gate in enumerate(gates):
        q = fft_conv_stage_b(plan, a.reshape(2, k1p, n2, HY_WIDTH), kf, o)
        last = o == len(gates) - 1
        out = fft_inv_stage_a(plan, q.reshape(2 * k1p, m), z, flat(gate), jnp.tile(skip[o], tn // HY_WIDTH)[None], lead,
                              not last)
        z = out[0]
        if not last:
            a = out[1]
    return z.reshape(-1, HY_WIDTH)


def _rec_out_body(x_ref, yf_ref, yb_ref, bonus_ref, sgr_ref, zh_ref, zc_ref, sgh_ref, gnw_ref, gnb_ref, bm_ref, wa_ref, wb_ref,
                  gate_ref, fin_ref, o_ref, *, final):
    y = yf_ref[...] + yb_ref[...]
    bm = bm_ref[...]
    mean = _dot_stat(y, bm)
    yc = y - mean
    var = _dot_stat(yc * yc, bm)
    yn = yc * lax.rsqrt(var + RWKV_GN_EPS) * gnw_ref[...] + gnb_ref[...]
    ya = ((yn + bonus_ref[...]) * sgr_ref[...]).astype(BF16)
    zh = zh_ref[...]
    if not final:
        zh = jnp.where(pl.program_id(0) == 0, zc_ref[...], zh)
    yh = (zh * sgh_ref[...]).astype(BF16)
    out = jnp.dot(ya, wa_ref[...], preferred_element_type=F32) + jnp.dot(yh, wb_ref[...], preferred_element_type=F32)
    xn = x_ref[...] + gate_ref[0] * out
    if final:
        xn = xn * lax.rsqrt(jnp.mean(xn * xn, axis=-1, keepdims=True) + NORM_EPS) * fin_ref[...]
    o_ref[...] = xn


RWKV_GN_EPS = 64e-5


def rec_out_proj(xs, yf, yb, bonus, sgr, zh, zc, sgh, gn_w, gn_b, w_out, gate2, final_g, *, final):
    r = xs.shape[0]
    d = D_MODEL
    off = 1 if final else 0
    nt = r // ROW_TILE - off
    rows = lambda w: pl.BlockSpec((ROW_TILE, w), lambda i: (i + off, 0))
    const = lambda shp: pl.BlockSpec(shp, lambda i: (0,) * len(shp))
    bw = BRANCH_W
    return pl.pallas_call(
        functools.partial(_rec_out_body, final=final),
        out_shape=jax.ShapeDtypeStruct((nt * ROW_TILE, d), F32),
        grid=(nt,),
        in_specs=[rows(d), rows(bw), rows(bw), rows(bw), rows(bw), rows(bw), const((CTX_LEN, bw)), rows(bw),
                  const((1, bw)), const((1, bw)),
                  const((bw, bw)), const((bw, d)), const((bw, d)),
                  pl.BlockSpec((1, 1, d), lambda i: (jnp.minimum(i + off, 1), 0, 0)), const((1, d))],
        out_specs=pl.BlockSpec((ROW_TILE, d), lambda i: (i, 0)),
        compiler_params=_cparams(("parallel",)),
        name="rec_out_proj",
    )(xs, yf, yb, bonus, sgr, zh, zc, sgh, gn_w[None], gn_b[None], _block_sum_mat(bw, 1.0 / HEAD_DIM).astype(BF16),
      w_out[:bw].astype(BF16), w_out[bw:].astype(BF16), gate2, final_g[None])


def rec_layer(xs, mods, norm_g, w_in, mu, w0, w_up, a0, a_up, k_k, k_a, r_k, gn_w, gn_b, hy_short, hy_w1, hy_b1, hy_w2,
              hy_b2, hy_w3, hy_b3, hy_skip, w_out, final_g, final):
    shift2, scale2, gate2 = mods
    rw, hv, hx1, hx2, sgr, sgh = rec_in_proj(xs, norm_g, scale2, shift2, w_in, mu, hy_short)
    g, add, bonus = rwkv_prep(rw, w0, w_up, a0, a_up, k_k, k_a, r_k)
    yf, yb = rwkv_scan(g, add)
    fargs = (hy_w1, hy_b1, hy_w2, hy_b2, hy_w3, hy_b3)
    n = xs.shape[0] - CTX_LEN
    taps, scale = hyena_taps(n, *fargs, flat=True)
    zh = hyena_long(hv, hx1, hx2, taps, scale, hy_skip, CTX_LEN // FFT_N2)
    if final:
        z_ctx = hv[:CTX_LEN]
    else:
        taps_c, scale_c = hyena_taps(CTX_LEN, *fargs, flat=False)
        z_ctx = hyena_short(hv[:CTX_LEN], hx1[:CTX_LEN], hx2[:CTX_LEN], taps_c, scale_c, hy_skip)
    return rec_out_proj(xs, yf, yb, bonus, sgr, zh, z_ctx, sgh, gn_w, gn_b, w_out, gate2, final_g, final=final)


def kernel(x, c, ctx, c_ctx, attn_norm, attn_ada_w, attn_ada_b, attn_w_in, na_rpb, gqa_q_gain, gqa_k_gain, attn_w_out,
           rec_norm, rec_ada_w, rec_ada_b, rec_w_in, rwkv_mu, rwkv_w0, rwkv_w_up, rwkv_a0, rwkv_a_up, rwkv_k_k, rwkv_k_a,
           rwkv_r_k, rwkv_gn_w, rwkv_gn_b, hy_short, hy_w1, hy_b1, hy_w2, hy_b2, hy_w3, hy_b3, hy_skip, rec_w_out,
           final_norm):
    assert x.shape[0] == 1 and ctx.shape[1] == CTX_LEN and x.shape[2] == D_MODEL
    n = x.shape[1]
    assert n % ROW_TILE == 0 and n // ROW_TILE >= 3
    assert attn_w_in.shape[0] == rec_w_in.shape[0]
    d = D_MODEL
    cond8 = jnp.zeros((8, d), F32).at[0].set(c_ctx).at[1].set(c[0])
    m_attn = adaln_all(cond8, attn_ada_w, attn_ada_b)
    m_rec = adaln_all(cond8, rec_ada_w, rec_ada_b)
    mods = lambda m, i: tuple(m[i, :2, j * d:(j + 1) * d].reshape(2, 1, d) for j in range(3))
    cos_t, sin_t = _rope_tables(n)
    xs = jnp.concatenate([ctx[0], x[0]], axis=0)
    depth = attn_w_in.shape[0] + rec_w_in.shape[0]
    for layer in range(depth):
        i = layer // 2
        final = layer == depth - 1
        if layer % 2 == 0:
            xs = attn_layer(xs, mods(m_attn, i), attn_norm[i], attn_w_in[i], na_rpb[i], gqa_q_gain[i], gqa_k_gain[i],
                            attn_w_out[i], cos_t, sin_t)
        else:
            xs = rec_layer(xs, mods(m_rec, i), rec_norm[i], rec_w_in[i], rwkv_mu[i], rwkv_w0[i], rwkv_w_up[i],
                           rwkv_a0[i], rwkv_a_up[i], rwkv_k_k[i], rwkv_k_a[i], rwkv_r_k[i], rwkv_gn_w[i], rwkv_gn_b[i],
                           hy_short[i], hy_w1[i], hy_b1[i], hy_w2[i], hy_b2[i], hy_w3[i], hy_b3[i], hy_skip[i],
                           rec_w_out[i], final_norm, final)
    return xs[None]


def _hy_short_body(fh_ref, fl_ref, gh_ref, gl_ref, v_ref, x1_ref, x2_ref, taps_ref, scale_ref, skip_ref, o_ref):
    fh, fl, gh, gl = fh_ref[...], fl_ref[...], gh_ref[...], gl_ref[...]
    kp = fh.shape[0] // 2
    hw = HY_TAPS_W // 2
    tf = _dot3c(fh, fl, taps_ref[...])
    kre = (tf[:kp, :hw] + tf[:kp, hw:]) * scale_ref[...]
    kim = (tf[kp:, :hw] - tf[kp:, hw:]) * scale_ref[...]
    z = v_ref[...]
    for o, gate_ref in enumerate((x1_ref, x2_ref)):
        ls = slice(o * HY_WIDTH, (o + 1) * HY_WIDTH)
        zf = _dot3c(fh, fl, z)
        zre, zim = zf[:kp], zf[kp:]
        y = jnp.concatenate([zre * kre[:, ls] - zim * kim[:, ls], zre * kim[:, ls] + zim * kre[:, ls]], axis=0)
        z = gate_ref[...] * (_dot3c(gh, gl, y) + z * skip_ref[o:o + 1])
    o_ref[...] = z


def hyena_short(hv, hx1, hx2, taps, scale, skip):
    length = hv.shape[0]
    n = 2 * length
    k1 = length + 1
    kp = -(-k1 // 8) * 8
    kk = np.arange(kp)[:, None].astype(np.float64)
    live = kk < k1
    th = 2.0 * np.pi * kk * np.arange(length)[None, :] / n
    f = _split_const(np.concatenate([np.cos(th) * live, -np.sin(th) * live], axis=0))
    ck = np.where((kk == 0) | (kk == length), 1.0, 2.0) * live / n
    g = _split_const(np.concatenate([np.cos(th) * ck, -np.sin(th) * ck], axis=0).T)
    return pl.pallas_call(
        _hy_short_body,
        out_shape=jax.ShapeDtypeStruct((length, HY_WIDTH), F32),
        compiler_params=pltpu.CompilerParams(vmem_limit_bytes=VMEM_LIMIT),
        name="hyena_short",
    )(f[0], f[1], g[0], g[1], hv, hx1, hx2, taps, scale, skip)
```

```python
import functools
import math

import jax
import jax.numpy as jnp
import numpy as np
from jax import lax
from jax.experimental import pallas as pl
from jax.experimental.pallas import tpu as pltpu

F32 = jnp.float32
BF16 = jnp.bfloat16
HIGHEST = lax.Precision.HIGHEST

D_MODEL = 1024
GRID_W = 64
CTX_LEN = 256
HEAD_DIM = 64
BRANCH_W = 512
N_HEADS = 8
GQA_KV_W = 128
NA_WIN_ROWS = 8
NA_WIN_COLS = 16
ROPE_THETA = 10000.0
ROPE_FREQS = 16
NORM_EPS = 1e-6
ROW_TILE = 256
NA_GROUP_ROWS = 4
NEG_BIG = -1e30
LOG2E = math.log2(math.e)
QK_SCALE = HEAD_DIM ** -0.5 * LOG2E
FLASH_SCORES_AHEAD = 3
NA_SCORES_AHEAD = 1
VMEM_LIMIT = 56 * 1024 * 1024

ATTN_SPLITS = (512, 512, 512, 512, 512, 128, 128, 512)
GQA_HEAD_ORDER = (0, 4, 1, 5, 2, 6, 3, 7)


def _cparams(sem):
    return pltpu.CompilerParams(dimension_semantics=sem, vmem_limit_bytes=VMEM_LIMIT)


def _silu(v):
    return v * (1.0 / (1.0 + jnp.exp(-v)))


def _lane_half(shape):
    return (lax.broadcasted_iota(jnp.int32, shape, len(shape) - 1) // HEAD_DIM) % 2


def _dot_stat(a, block_mat):
    hi = a.astype(BF16)
    lo = (a - hi.astype(F32)).astype(BF16)
    return jnp.dot(hi, block_mat, preferred_element_type=F32) + jnp.dot(lo, block_mat, preferred_element_type=F32)


def _dot_nt(a, b):
    return lax.dot_general(a, b, (((1,), (1,)), ((), ())), preferred_element_type=F32)


def _adaln_body(cond_ref, w_ref, b_ref, o_ref):
    s = _silu(cond_ref[...])
    o_ref[0] = jnp.dot(s, w_ref[0], precision=HIGHEST, preferred_element_type=F32) + b_ref[0]


def adaln_all(cond8, ada_w, ada_b):
    nl = ada_w.shape[0]
    d = D_MODEL
    return pl.pallas_call(
        _adaln_body,
        out_shape=jax.ShapeDtypeStruct((nl, 8, 3 * d), F32),
        grid=(nl, 3),
        in_specs=[
            pl.BlockSpec((8, d), lambda l, j: (0, 0)),
            pl.BlockSpec((1, d, d), lambda l, j: (l, 0, j)),
            pl.BlockSpec((1, 1, d), lambda l, j: (l, 0, j)),
        ],
        out_specs=pl.BlockSpec((1, 8, d), lambda l, j: (l, 0, j)),
        compiler_params=_cparams(("parallel", "parallel")),
        name="adaln",
    )(cond8, ada_w, ada_b.reshape(nl, 1, 3 * d))


def _modulated(x_ref, g_ref, scale_ref, shift_ref):
    xf = x_ref[...]
    y = xf * lax.rsqrt(jnp.mean(xf * xf, axis=-1, keepdims=True) + NORM_EPS)
    return (y * g_ref[...]) * (1.0 + scale_ref[0]) + shift_ref[0]


def _mod_specs():
    d = D_MODEL
    return [
        pl.BlockSpec((ROW_TILE, d), lambda i: (i, 0)),
        pl.BlockSpec((1, d), lambda i: (0, 0)),
        pl.BlockSpec((1, 1, d), lambda i: (jnp.minimum(i, 1), 0, 0)),
        pl.BlockSpec((1, 1, d), lambda i: (jnp.minimum(i, 1), 0, 0)),
    ]


def _attn_in_body(x_ref, g_ref, scale_ref, shift_ref, w_ref, cos_ref, sin_ref, gq_ref, gqs_ref, gk_ref, gks_ref,
                  bdq_ref, bdk_ref,
                  qa_ref, ka_ref, va_ref, sga_ref, qb_ref, kb_ref, vb_ref, sgb_ref):
    xm = _modulated(x_ref, g_ref, scale_ref, shift_ref).astype(BF16)
    u = jnp.dot(xm, w_ref[...], preferred_element_type=F32)
    qa, ka, va, ga = u[:, 0:512], u[:, 512:1024], u[:, 1024:1536], u[:, 1536:2048]
    qb, kb, vb, gb = u[:, 2048:2560], u[:, 2560:2688], u[:, 2688:2816], u[:, 2816:3328]
    qbs, kbs = u[:, 3328:3840], u[:, 3840:3968]
    scale = QK_SCALE
    qa_ref[...] = (qa * scale).astype(BF16)
    ka_ref[...] = ka.astype(BF16)
    va_ref[...] = va.astype(BF16)
    sga_ref[...] = _silu(ga).astype(sga_ref.dtype)
    sgb_ref[...] = _silu(gb).astype(sgb_ref.dtype)
    vb_ref[...] = jnp.transpose(vb).astype(BF16)
    cos_k, sin_k = cos_ref[...], sin_ref[...]
    cos_q = jnp.concatenate([cos_k] * 4, axis=1)
    sin_q = jnp.concatenate([sin_k] * 4, axis=1)
    rs_q = lax.rsqrt(_dot_stat(qb * qb, bdq_ref[...]) + NORM_EPS)
    rs_k = lax.rsqrt(_dot_stat(kb * kb, bdk_ref[...]) + NORM_EPS)
    qr = rs_q * (qb * gq_ref[...] * cos_q + qbs * gqs_ref[...] * sin_q)
    kr = rs_k * (kb * gk_ref[...] * cos_k + kbs * gks_ref[...] * sin_k)
    qb_ref[...] = jnp.transpose(qr * scale).astype(BF16)
    kb_ref[...] = kr.astype(BF16)


def _rope_tables(n):
    t = jnp.arange(n, dtype=jnp.int32)
    pos = jnp.stack([t // GRID_W, t % GRID_W], axis=-1).astype(F32)
    inv_freq = ROPE_THETA ** (-jnp.arange(ROPE_FREQS, dtype=F32) / ROPE_FREQS)
    ang = pos[:, :, None] * inv_freq
    c, s = jnp.cos(ang), jnp.sin(ang)
    cos64 = jnp.concatenate([c[:, 0], c[:, 0], c[:, 1], c[:, 1]], axis=-1)
    sin64 = jnp.concatenate([-s[:, 0], s[:, 0], -s[:, 1], s[:, 1]], axis=-1)
    cos64 = jnp.concatenate([jnp.ones((CTX_LEN, HEAD_DIM), F32), cos64], axis=0)
    sin64 = jnp.concatenate([jnp.zeros((CTX_LEN, HEAD_DIM), F32), sin64], axis=0)
    return jnp.tile(cos64, (1, 2)), jnp.tile(sin64, (1, 2))


def _reorder_heads(w, order, axis):
    take = lambda h: lax.slice_in_dim(w, h * HEAD_DIM, (h + 1) * HEAD_DIM, axis=axis)
    return jnp.concatenate([take(h) for h in order], axis=axis)


def _swap_rope_halves(w):
    shp = w.shape
    return jnp.flip(w.reshape(shp[:-1] + (shp[-1] // (2 * ROPE_FREQS), 2, ROPE_FREQS)), axis=-2).reshape(shp)


def attn_in_proj(xs, norm_g, scale2, shift2, w_in, q_gain, k_gain, cos_t, sin_t):
    r = xs.shape[0]
    d = D_MODEL
    parts, start = [], 0
    for s in ATTN_SPLITS:
        parts.append(w_in[:, start:start + s])
        start += s
    wqa, wka, wva, wga, wqb, wkb, wvb, wgb = parts
    wqb_p = _reorder_heads(wqb, GQA_HEAD_ORDER, 1)
    wgb_p = _reorder_heads(wgb, GQA_HEAD_ORDER, 1)
    wqb_sw = _swap_rope_halves(wqb_p)
    wkb_sw = _swap_rope_halves(wkb)
    w_ext = jnp.concatenate([wqa, wka, wva, wga, wqb_p, wkb, wvb, wgb_p, wqb_sw, wkb_sw], axis=1).astype(BF16)
    gq = jnp.tile(q_gain, N_HEADS)[None]
    gqs = jnp.tile(_swap_rope_halves(q_gain), N_HEADS)[None]
    gk = jnp.tile(k_gain, 2)[None]
    gks = jnp.tile(_swap_rope_halves(k_gain), 2)[None]
    bdq = jnp.asarray(np.kron(np.eye(N_HEADS), np.full((HEAD_DIM, HEAD_DIM), 1.0 / HEAD_DIM)), BF16)
    bdk = jnp.asarray(np.kron(np.eye(2), np.full((HEAD_DIM, HEAD_DIM), 1.0 / HEAD_DIM)), BF16)
    wcols = w_ext.shape[1]
    const = lambda shp: pl.BlockSpec(shp, lambda i: (0,) * len(shp))
    rows = lambda w: pl.BlockSpec((ROW_TILE, w), lambda i: (i, 0))
    out_shapes = [
        jax.ShapeDtypeStruct((r, 512), BF16), jax.ShapeDtypeStruct((r, 512), BF16), jax.ShapeDtypeStruct((r, 512), BF16),
        jax.ShapeDtypeStruct((r, 512), BF16),
        jax.ShapeDtypeStruct((512, r), BF16), jax.ShapeDtypeStruct((r, 128), BF16), jax.ShapeDtypeStruct((128, r), BF16),
        jax.ShapeDtypeStruct((r, 512), BF16),
    ]
    cols = lambda w: pl.BlockSpec((w, ROW_TILE), lambda i: (0, i))
    return pl.pallas_call(
        _attn_in_body,
        out_shape=out_shapes,
        grid=(r // ROW_TILE,),
        in_specs=_mod_specs() + [const((d, wcols)), rows(128), rows(128), const((1, 512)), const((1, 512)),
                                 const((1, 128)), const((1, 128)), const((512, 512)), const((128, 128))],
        out_specs=[rows(512), rows(512), rows(512), rows(512), cols(512), rows(128), cols(128), rows(512)],
        compiler_params=_cparams(("parallel",)),
        name="attn_in_proj",
    )(xs, norm_g[None], scale2, shift2, w_ext, cos_t, sin_t, gq, gqs, gk, gks, bdq, bdk)


def _na_cols_body(rpb_ref, sel_ref, neg_ref, o_ref):
    o_ref[...] = jnp.dot(rpb_ref[...], sel_ref[...], precision=HIGHEST, preferred_element_type=F32) + neg_ref[...]


def _na_bias_tables(rpb, rows):
    nrel_r, nrel_c = 2 * NA_WIN_ROWS - 1, 2 * NA_WIN_COLS - 1
    qc = np.arange(GRID_W)[:, None]
    kc = np.arange(GRID_W)[None, :]
    col0 = np.clip(qc - NA_WIN_COLS // 2, 0, GRID_W - NA_WIN_COLS)
    col_ok = (kc >= col0) & (kc < col0 + NA_WIN_COLS)
    rc = kc - qc + NA_WIN_COLS - 1
    sel = np.zeros((128, GRID_W * GRID_W), np.float32)
    sel[np.where(col_ok, rc, 127).reshape(-1), np.arange(GRID_W * GRID_W)] = col_ok.reshape(-1)
    neg = np.where(col_ok, 0.0, NEG_BIG).astype(np.float32).reshape(1, -1)
    rpb2 = jnp.zeros((128, 128), F32).at[:N_HEADS * nrel_r, :nrel_c].set(rpb.reshape(N_HEADS * nrel_r, nrel_c))
    cols = pl.pallas_call(
        _na_cols_body,
        out_shape=jax.ShapeDtypeStruct((128, GRID_W * GRID_W), F32),
        name="na_bias_cols",
    )(rpb2, jnp.asarray(sel), jnp.asarray(neg))
    cols = cols[:N_HEADS * nrel_r].reshape(N_HEADS, nrel_r, GRID_W, GRID_W)
    kh = min(NA_WIN_ROWS, rows)
    g = rows // NA_GROUP_ROWS
    cases = [(0, 0), (NA_GROUP_ROWS, 0), (rows - NA_GROUP_ROWS, NA_GROUP_ROWS * (g - 3))]
    masked = jnp.full((N_HEADS, GRID_W, GRID_W), NEG_BIG, F32)
    tabs = []
    for qr_first, start in cases:
        blocks = []
        for j in range(NA_GROUP_ROWS):
            qr = qr_first + j
            row0 = min(max(qr - kh // 2, 0), rows - kh)
            for i in range(3 * NA_GROUP_ROWS):
                kr = start + i
                blocks.append(cols[:, kr - qr + NA_WIN_ROWS - 1] if row0 <= kr < row0 + kh else masked)
        tab = jnp.stack(blocks, axis=1).reshape(N_HEADS, NA_GROUP_ROWS, 3 * NA_GROUP_ROWS, GRID_W, GRID_W)
        tabs.append(tab.transpose(0, 1, 3, 2, 4).reshape(N_HEADS, ROW_TILE, 3 * ROW_TILE))
    return jnp.stack(tabs) * LOG2E


def _na_body(q_ref, kc_ref, k0_ref, k1_ref, k2_ref, vc_ref, v0_ref, v1_ref, v2_ref, bias_ref, sg_ref, o_ref):
    half = _lane_half((ROW_TILE, 128))
    k_refs = (k0_ref, k1_ref, k2_ref, kc_ref)
    v_refs = (v0_ref, v1_ref, v2_ref, vc_ref)

    def scores(h):
        hp, j = divmod(h, 2)
        ls = slice(hp * 128, (hp + 1) * 128)
        qp = q_ref[:, ls]
        qm = jnp.where(half == j, qp, jnp.zeros_like(qp))
        return [_dot_nt(qm, r[:, ls]) for r in k_refs]

    pending = [scores(h) for h in range(NA_SCORES_AHEAD)]
    outs = []
    for h in range(N_HEADS):
        hp, j = divmod(h, 2)
        ls = slice(hp * 128, (hp + 1) * 128)
        s = pending.pop(0)
        if h + NA_SCORES_AHEAD < N_HEADS:
            pending.append(scores(h + NA_SCORES_AHEAD))
        s_win = jnp.concatenate(s[:3], axis=1) + bias_ref[0, h]
        s_ctx = s[3]
        m = jnp.maximum(jnp.max(s_win, axis=1, keepdims=True), jnp.max(s_ctx, axis=1, keepdims=True))
        p_win = jnp.exp2(s_win - m).astype(BF16)
        p_ctx = jnp.exp2(s_ctx - m).astype(BF16)
        vms = [jnp.where(half == j, r[:, ls], jnp.ones((ROW_TILE, 128), BF16)) for r in v_refs]
        o = jnp.dot(p_ctx, vms[3], preferred_element_type=F32)
        for b in range(3):
            o += jnp.dot(p_win[:, b * ROW_TILE:(b + 1) * ROW_TILE], vms[b], preferred_element_type=F32)
        outs.append(o / pltpu.roll(o, HEAD_DIM, 1))
        if j == 1:
            o_pair = jnp.where(half == 0, outs[h - 1], outs[h])
            o_ref[:, ls] = (o_pair * sg_ref[:, ls]).astype(BF16)


def na_attention(qa, ka, va, sga, bias_tabs, n):
    g = n // ROW_TILE
    w = BRANCH_W

    def kv_spec(off):
        return pl.BlockSpec((ROW_TILE, w), lambda i: (jnp.clip(i - 1, 0, g - 3) + off + 1, 0))

    ctx_spec = pl.BlockSpec((ROW_TILE, w), lambda i: (0, 0))
    q_spec = pl.BlockSpec((ROW_TILE, w), lambda i: (i + 1, 0))
    case = lambda i: jnp.where(i == 0, 0, jnp.where(i == g - 1, 2, 1))
    bias_spec = pl.BlockSpec((1, N_HEADS, ROW_TILE, 3 * ROW_TILE), lambda i: (case(i), 0, 0, 0))
    return pl.pallas_call(
        _na_body,
        out_shape=jax.ShapeDtypeStruct((n, w), BF16),
        grid=(g,),
        in_specs=[q_spec, ctx_spec, kv_spec(0), kv_spec(1), kv_spec(2), ctx_spec, kv_spec(0), kv_spec(1), kv_spec(2),
                  bias_spec, q_spec],
        out_specs=pl.BlockSpec((ROW_TILE, w), lambda i: (i, 0)),
        compiler_params=_cparams(("parallel",)),
        name="na_attention",
    )(qa, ka, ka, ka, ka, va, va, va, va, bias_tabs, sga)


def _flash_mha_body(q_ref, k_ref, v_ref, sg_ref, o_ref, m_ref, acc_ref):
    kv = pl.program_id(1)
    tq = q_ref.shape[0]

    @pl.when(kv == 0)
    def _():
        m_ref[...] = jnp.full(m_ref.shape, NEG_BIG, F32)
        acc_ref[...] = jnp.zeros(acc_ref.shape, F32)

    khalf = _lane_half((k_ref.shape[0], 128))
    for p in range(N_HEADS // 2):
        ls = slice(p * 128, (p + 1) * 128)
        qp = q_ref[:, ls]
        kp = k_ref[:, ls]
        vp = v_ref[:, ls]
        for j in range(2):
            hh = 2 * p + j
            km = jnp.where(khalf == j, kp, jnp.zeros_like(kp))
            vm = jnp.where(khalf == j, vp, jnp.ones_like(vp))
            s = _dot_nt(qp, km)
            m_prev = m_ref[hh]
            m_new = jnp.maximum(m_prev, jnp.max(s, axis=1, keepdims=True))
            alpha = jnp.exp2(m_prev - m_new)
            pr = jnp.exp2(s - m_new[:, :1]).astype(BF16)
            acc_ref[hh] = alpha * acc_ref[hh] + jnp.dot(pr, vm, preferred_element_type=F32)
            m_ref[hh] = m_new

    @pl.when(kv == pl.num_programs(1) - 1)
    def _():
        half = _lane_half((tq, 128))
        for p in range(N_HEADS // 2):
            ls = slice(p * 128, (p + 1) * 128)
            a0, a1 = acc_ref[2 * p], acc_ref[2 * p + 1]
            o0 = a0 / pltpu.roll(a0, HEAD_DIM, 1)
            o1 = a1 / pltpu.roll(a1, HEAD_DIM, 1)
            o_ref[:, ls] = (jnp.where(half == 0, o0, o1) * sg_ref[:, ls]).astype(BF16)


def flash_mha(q, k, v, sg, *, q_block0, nq, tk, nk):
    tq = ROW_TILE
    return pl.pallas_call(
        _flash_mha_body,
        out_shape=jax.ShapeDtypeStruct((nq * tq, BRANCH_W), BF16),
        grid=(nq, nk),
        in_specs=[
            pl.BlockSpec((tq, BRANCH_W), lambda i, j: (i + q_block0, 0)),
            pl.BlockSpec((tk, BRANCH_W), lambda i, j: (j, 0)),
            pl.BlockSpec((tk, BRANCH_W), lambda i, j: (j, 0)),
            pl.BlockSpec((tq, BRANCH_W), lambda i, j: (i + q_block0, 0)),
        ],
        out_specs=pl.BlockSpec((tq, BRANCH_W), lambda i, j: (i, 0)),
        scratch_shapes=[pltpu.VMEM((N_HEADS, tq, 128), F32)] * 2,
        compiler_params=_cparams(("parallel", "arbitrary")),
        name="flash_mha",
    )(q, k, v, sg)


def _flash_gqa_body(qt_ref, k_ref, vt_ref, sg_ref, o_ref, m_ref, acc_ref):
    kv = pl.program_id(1)
    tq = qt_ref.shape[1]
    tk = k_ref.shape[0]

    @pl.when(kv == 0)
    def _():
        m_ref[...] = jnp.full(m_ref.shape, NEG_BIG, F32)
        acc_ref[...] = jnp.zeros(acc_ref.shape, F32)

    khalf = _lane_half((tk, 128))
    vhalf = lax.broadcasted_iota(jnp.int32, (128, tk), 0) // HEAD_DIM
    kb = k_ref[...]
    vt = vt_ref[...]
    kms = [jnp.where(khalf == j, kb, jnp.zeros_like(kb)) for j in range(2)]
    vms = [jnp.where(vhalf == j, vt, jnp.ones_like(vt)) for j in range(2)]

    def scores(hh):
        p, j = divmod(hh, 2)
        return jnp.dot(kms[j], qt_ref[p * 128:(p + 1) * 128, :], preferred_element_type=F32).astype(BF16)

    pending = [scores(hh) for hh in range(FLASH_SCORES_AHEAD)]
    for hh in range(N_HEADS):
        st = pending.pop(0)
        if hh + FLASH_SCORES_AHEAD < N_HEADS:
            pending.append(scores(hh + FLASH_SCORES_AHEAD))
        m_prev = m_ref[hh]
        m_new = jnp.maximum(m_prev, jnp.max(st, axis=0, keepdims=True).astype(F32))
        alpha = jnp.exp2(m_prev - m_new)
        pt = jnp.exp2(st - m_new[0:1].astype(BF16))
        acc_ref[hh] = alpha[0:1] * acc_ref[hh] + jnp.dot(vms[hh % 2], pt, preferred_element_type=F32)
        m_ref[hh] = m_new

    @pl.when(kv == pl.num_programs(1) - 1)
    def _():
        for p in range(N_HEADS // 2):
            ls = slice(p * 128, (p + 1) * 128)
            a0, a1 = acc_ref[2 * p], acc_ref[2 * p + 1]
            ot = jnp.concatenate([a0[:HEAD_DIM] / a0[HEAD_DIM:], a1[HEAD_DIM:] / a1[:HEAD_DIM]], axis=0)
            o_ref[:, ls] = (jnp.transpose(ot) * sg_ref[:, ls]).astype(BF16)


def flash_gqa(qt, k, vt, sg, *, q_block0, nq, tk, nk):
    tq = ROW_TILE
    return pl.pallas_call(
        _flash_gqa_body,
        out_shape=jax.ShapeDtypeStruct((nq * tq, BRANCH_W), BF16),
        grid=(nq, nk),
        in_specs=[
            pl.BlockSpec((BRANCH_W, tq), lambda i, j: (0, i + q_block0)),
            pl.BlockSpec((tk, GQA_KV_W), lambda i, j: (j, 0)),
            pl.BlockSpec((GQA_KV_W, tk), lambda i, j: (0, j)),
            pl.BlockSpec((tq, BRANCH_W), lambda i, j: (i + q_block0, 0)),
        ],
        out_specs=pl.BlockSpec((tq, BRANCH_W), lambda i, j: (i, 0)),
        scratch_shapes=[pltpu.VMEM((N_HEADS, 8, tq), F32), pltpu.VMEM((N_HEADS, 128, tq), F32)],
        compiler_params=_cparams(("parallel", "arbitrary")),
        name="flash_gqa",
    )(qt, k, vt, sg)


def _out_body(x_ref, ya_ref, yac_ref, yb_ref, ybc_ref, wa_ref, wb_ref, gate_ref, o_ref):
    is_ctx = pl.program_id(0) == 0
    ya = jnp.where(is_ctx, yac_ref[...], ya_ref[...])
    yb = jnp.where(is_ctx, ybc_ref[...], yb_ref[...])
    y = jnp.dot(ya, wa_ref[...], preferred_element_type=F32)
    y += jnp.dot(yb, wb_ref[...], preferred_element_type=F32)
    o_ref[...] = x_ref[...] + gate_ref[0] * y


def out_proj(xs, ya_lat, ya_ctx, yb_lat, yb_ctx, wa, wb, gate2):
    r = xs.shape[0]
    d = D_MODEL
    rows = lambda w: pl.BlockSpec((ROW_TILE, w), lambda i: (i, 0))
    lat = pl.BlockSpec((ROW_TILE, BRANCH_W), lambda i: (jnp.maximum(i - 1, 0), 0))
    const = lambda shp: pl.BlockSpec(shp, lambda i: (0,) * len(shp))
    ctx = const((CTX_LEN, BRANCH_W))
    return pl.pallas_call(
        _out_body,
        out_shape=jax.ShapeDtypeStruct((r, d), F32),
        grid=(r // ROW_TILE,),
        in_specs=[rows(d), lat, ctx, lat, ctx, const((BRANCH_W, d)), const((BRANCH_W, d)),
                  pl.BlockSpec((1, 1, d), lambda i: (jnp.minimum(i, 1), 0, 0))],
        out_specs=rows(d),
        compiler_params=_cparams(("parallel",)),
        name="out_proj",
    )(xs, ya_lat, ya_ctx, yb_lat, yb_ctx, wa.astype(BF16), wb.astype(BF16), gate2)


KV_TILE_MAX_BLOCKS = 13


def _kv_tile(r):
    nb = r // ROW_TILE
    best = max(k for k in range(1, KV_TILE_MAX_BLOCKS + 1) if nb % k == 0)
    return best * ROW_TILE, nb // best


def attn_layer(xs, mods, norm_g, w_in, rpb, q_gain, k_gain, w_out, cos_t, sin_t):
    r = xs.shape[0]
    n = r - CTX_LEN
    shift2, scale2, gate2 = mods
    qa, ka, va, sga, qbt, kb, vbt, sgb = attn_in_proj(xs, norm_g, scale2, shift2, w_in, q_gain, k_gain, cos_t, sin_t)
    bias_tabs = _na_bias_tables(rpb, n // GRID_W)
    ya_lat = na_attention(qa, ka, va, sga, bias_tabs, n)
    ya_ctx = flash_mha(qa, ka, va, sga, q_block0=0, nq=1, tk=CTX_LEN, nk=1)
    tk, nk = _kv_tile(r)
    yb_lat = flash_gqa(qbt, kb, vbt, sgb, q_block0=1, nq=n // ROW_TILE, tk=tk, nk=nk)
    yb_ctx = flash_gqa(qbt, kb, vbt, sgb, q_block0=0, nq=1, tk=CTX_LEN, nk=1)
    wb = _reorder_heads(w_out[BRANCH_W:], GQA_HEAD_ORDER, 0)
    return out_proj(xs, ya_lat, ya_ctx, yb_lat, yb_ctx, w_out[:BRANCH_W], wb, gate2)


def _split(a):
    hi = a.astype(BF16)
    return hi, (a - hi.astype(F32)).astype(BF16)


def _dot3(a, b, dims=(((1,), (0,)), ((), ()))):
    ah, al = _split(a)
    bh, bl = _split(b)
    dg = functools.partial(lax.dot_general, dimension_numbers=dims, preferred_element_type=F32)
    return dg(ah, bh) + (dg(al, bh) + dg(ah, bl))


def _dot1(a, b, dims=(((1,), (0,)), ((), ()))):
    return lax.dot_general(a.astype(BF16), b.astype(BF16), dims, preferred_element_type=F32)


_NT = (((1,), (1,)), ((), ()))
_TN = (((0,), (0,)), ((), ()))


RWKV_SHIFT_W = 1664
HY_IN_W = 1536
HALO = 8
REC_HALO_W = RWKV_SHIFT_W + HY_IN_W


def _rec_in_body(x_ref, xp_ref, xn_ref, g_ref, scale_ref, shift_ref, w_ref, mu_ref, taps_ref,
                 rw_ref, hv_ref, hx1_ref, hx2_ref, sgr_ref, sgh_ref, u_scr):
    i = pl.program_id(0)
    nt = pl.num_programs(0)
    xe = jnp.concatenate([xp_ref[...], x_ref[...], xn_ref[...]], axis=0)
    y = xe * lax.rsqrt(jnp.mean(xe * xe, axis=-1, keepdims=True) + NORM_EPS)
    xm = ((y * g_ref[...]) * (1.0 + scale_ref[0]) + shift_ref[0]).astype(BF16)
    u = jnp.dot(xm, w_ref[...], preferred_element_type=F32)
    row = lax.broadcasted_iota(jnp.int32, (ROW_TILE + 2 * HALO, 1), 0)
    keep = jnp.logical_and(jnp.logical_or(row >= HALO, i >= 2),
                           jnp.logical_or(row < ROW_TILE + HALO, jnp.logical_and(i >= 1, i < nt - 1)))
    u_scr[...] = jnp.where(keep, u[:, :REC_HALO_W], 0.0)
    up = u_scr[pl.ds(HALO - 1, ROW_TILE), :]
    uc = u_scr[pl.ds(HALO, ROW_TILE), :]
    un = u_scr[pl.ds(HALO + 1, ROW_TILE), :]
    w = RWKV_SHIFT_W
    rw_c = uc[:, :w]
    rw_ref[...] = rw_c + (0.5 * (up[:, :w] + un[:, :w]) - rw_c) * mu_ref[...]
    hy = up[:, w:] * taps_ref[0:1] + uc[:, w:] * taps_ref[1:2] + un[:, w:] * taps_ref[2:3]
    hv_ref[...] = hy[:, 0:512].astype(hv_ref.dtype)
    hx1_ref[...] = hy[:, 512:1024].astype(hx1_ref.dtype)
    hx2_ref[...] = hy[:, 1024:1536].astype(hx2_ref.dtype)
    uc_all = u[HALO:HALO + ROW_TILE]
    sgr_ref[...] = _silu(uc_all[:, REC_HALO_W:REC_HALO_W + 512]).astype(sgr_ref.dtype)
    sgh_ref[...] = _silu(uc_all[:, REC_HALO_W + 512:REC_HALO_W + 1024]).astype(sgh_ref.dtype)


def rec_in_proj(xs, norm_g, scale2, shift2, w_in, mu, hy_short):
    r = xs.shape[0]
    d = D_MODEL
    w = RWKV_SHIFT_W
    w_ext = jnp.concatenate([w_in[:, :w], w_in[:, w + 512:w + 512 + HY_IN_W], w_in[:, w:w + 512],
                             w_in[:, w + 512 + HY_IN_W:]], axis=1).astype(BF16)
    nh = r // HALO
    per = ROW_TILE // HALO
    const = lambda shp: pl.BlockSpec(shp, lambda i: (0,) * len(shp))
    rows = lambda wd: pl.BlockSpec((ROW_TILE, wd), lambda i: (i, 0))
    f = lambda wd: jax.ShapeDtypeStruct((r, wd), F32)
    h = jax.ShapeDtypeStruct((r, 512), BF16)
    mod = _mod_specs()
    return pl.pallas_call(
        _rec_in_body,
        out_shape=[f(w), h, h, h, h, h],
        grid=(r // ROW_TILE,),
        in_specs=[mod[0],
                  pl.BlockSpec((HALO, d), lambda i: (jnp.maximum(i * per - 1, 0), 0)),
                  pl.BlockSpec((HALO, d), lambda i: (jnp.minimum((i + 1) * per, nh - 1), 0)),
                  mod[1], mod[2], mod[3], const((d, w_ext.shape[1])), const((1, w)), const((3, HY_IN_W))],
        out_specs=[rows(w), rows(512), rows(512), rows(512), rows(512), rows(512)],
        scratch_shapes=[pltpu.VMEM((ROW_TILE + 2 * HALO, REC_HALO_W), F32)],
        compiler_params=_cparams(("parallel",)),
        name="rec_in_proj",
    )(xs, xs, xs, norm_g[None], scale2, shift2, w_ext, mu[None], hy_short)


CHUNK = 64
CPT = ROW_TILE // CHUNK


def _block_sum_mat(width, value):
    return jnp.asarray(np.kron(np.eye(width // HEAD_DIM), np.full((HEAD_DIM, HEAD_DIM), value)), F32)


PREP_PAIRS = 2


def _rwkv_prep_body(r_ref, k_ref, v_ref, lora_ref, w0_ref, wup_ref, a0_ref, aup_ref, kk_ref, ka_ref, rk_ref,
                    tri_ref, bs_ref, g_ref, add_ref, bonus_ref):
    t = ROW_TILE
    lora = lora_ref[...]
    bs = bs_ref[...]
    row = lax.broadcasted_iota(jnp.int32, (t, t), 0)
    col = lax.broadcasted_iota(jnp.int32, (t, t), 1)
    same = (row // CHUNK) == (col // CHUNK)
    eye = (row == col).astype(F32)
    wl_all = _dot3(jnp.tanh(lora), wup_ref[...])
    al_all = _dot3(lora, aup_ref[...])
    half = _lane_half((t, 128))
    half_c = _lane_half((HEAD_DIM, 128))
    rowc = lax.broadcasted_iota(jnp.int32, (HEAD_DIM, 128), 0)
    lanec = lax.broadcasted_iota(jnp.int32, (HEAD_DIM, 128), 1)
    level_masks = []
    bsz = 2
    while bsz < CHUNK:
        level_masks.append(jnp.logical_and((row // (2 * bsz)) == (col // (2 * bsz)), (row // bsz) != (col // bsz)))
        bsz *= 2
    first_mask = (row // 2) == (col // 2)

    groups = []
    for q in range(PREP_PAIRS):
        ls = slice(q * 128, (q + 1) * 128)
        r, k, v = r_ref[:, ls], k_ref[:, ls], v_ref[:, ls]
        kk = k * kk_ref[:, ls]
        kk = kk * lax.rsqrt(_dot_stat(kk * kk, bs) + 1e-12)
        kd_sum = None
        for d in range(2):
            ds = slice(q * 256 + d * 128, q * 256 + (d + 1) * 128)
            wl = w0_ref[d, :, ls] + wl_all[:, ds]
            z = -wl
            w_log = -(jnp.maximum(z, 0.0) + jnp.log(1.0 + jnp.exp(-jnp.abs(z)))) - 0.5
            lw = -jnp.exp(w_log)
            a = 1.0 / (1.0 + jnp.exp(-(a0_ref[d, :, ls] + al_all[:, ds])))
            kd = k * (1.0 + (a - 1.0) * ka_ref[:, ls])
            b = kk * a
            kd_sum = kd if kd_sum is None else kd_sum + kd
            incl = jnp.logical_and(same, (col <= row) if d == 0 else (col >= row))
            lw_hi, lw_lo = _split(lw)
            tri = tri_ref[d]
            cs = jnp.dot(tri, lw_hi, preferred_element_type=F32) + jnp.dot(tri, lw_lo, preferred_element_type=F32)
            ends = [c * CHUNK + (CHUNK - 1 if d == 0 else 0) for c in range(CPT)]
            tot = jnp.concatenate([jnp.broadcast_to(cs[e:e + 1], (CHUNK, 128)) for e in ends], axis=0)
            w_inv = jnp.exp(-cs)
            w_rest = jnp.exp(tot - cs)
            groups.append(dict(q=q, d=d, v=v, incl=incl, strict=jnp.logical_and(incl, row != col), tot=tot,
                               kkt=kk * jnp.exp(cs - lw), kh=kd * w_inv, bh=b * w_inv, rt=r * jnp.exp(cs),
                               kdd=kd * w_rest, bdd=b * w_rest))
        bonus_ref[:, ls] = 0.5 * _dot_stat(r * kd_sum * rk_ref[:, ls], bs) * v
    chains = [(gg, j) for gg in groups for j in range(2)]
    sels = [half == j for _, j in chains]
    bms = [jnp.where(sel, gg["bh"], 0.0) for (gg, _), sel in zip(chains, sels)]
    kms = [jnp.where(sel, gg["kh"], 0.0) for (gg, _), sel in zip(chains, sels)]
    l_bs = [jnp.where(gg["strict"], _dot1(gg["kkt"], bm, _NT), 0.0) for (gg, _), bm in zip(chains, bms)]
    tinvs = [eye - jnp.where(first_mask, l_b, 0.0) for l_b in l_bs]
    for mask in level_masks:
        xs = [_dot1(jnp.where(mask, l_b, 0.0), tinv) for l_b, tinv in zip(l_bs, tinvs)]
        tinvs = [tinv - _dot1(tinv, x) for tinv, x in zip(tinvs, xs)]
    l_ks = [jnp.where(gg["strict"], _dot1(gg["kkt"], km, _NT), 0.0) for (gg, _), km in zip(chains, kms)]
    a_rks = [jnp.where(gg["incl"], _dot1(gg["rt"], km, _NT), 0.0) for (gg, _), km in zip(chains, kms)]
    a_rbs = [jnp.where(gg["incl"], _dot1(gg["rt"], bm, _NT), 0.0) for (gg, _), bm in zip(chains, bms)]
    lvs = [_dot1(l_k, gg["v"]) for (gg, _), l_k in zip(chains, l_ks)]
    pus = [_dot1(tinv, jnp.concatenate([gg["kkt"], lv], axis=1)) for (gg, _), tinv, lv in zip(chains, tinvs, lvs)]
    cors = [_dot1(a_rb, pu) for a_rb, pu in zip(a_rbs, pus)]
    ps = [pu[:, :128] for pu in pus]
    u0s = [pu[:, 128:] for pu in pus]
    qs = [gg["rt"] - cor[:, :128] for (gg, _), cor in zip(chains, cors)]
    y0s = [_dot1(a_rk, gg["v"]) - cor[:, 128:] for (gg, _), a_rk, cor in zip(chains, a_rks, cors)]

    sel0 = half == 0
    for n, gg in enumerate(groups):
        d, v = gg["d"], gg["v"]
        ls = slice(gg["q"] * 128, (gg["q"] + 1) * 128)
        p, u0, q, y0 = (jnp.where(sel0, x[2 * n], x[2 * n + 1]) for x in (ps, u0s, qs, y0s))
        for c in range(CPT):
            rs = slice(c * CHUNK, (c + 1) * CHUNK)
            x1 = _dot1(gg["bdd"][rs], p[rs], _TN)
            x2 = _dot1(gg["kdd"][rs], v[rs], _TN) - _dot1(gg["bdd"][rs], u0[rs], _TN)
            m_pair = jnp.where(half_c == 0, x1[:HEAD_DIM], x1[HEAD_DIM:])
            n_pair = jnp.where(half_c == 0, x2[:HEAD_DIM], x2[HEAD_DIM:])
            wc = jnp.exp(gg["tot"][c * CHUNK:c * CHUNK + 1])
            dg = jnp.where((lanec % HEAD_DIM) == rowc, wc, 0.0)
            g_ref[c, d, 0:HEAD_DIM, ls] = dg - m_pair
            g_ref[c, d, HEAD_DIM:, ls] = q[rs]
            add_ref[c, d, 0:HEAD_DIM, ls] = n_pair
            add_ref[c, d, HEAD_DIM:, ls] = y0[rs]


def _lora_ext(up, first_row):
    out = jnp.zeros((2, 128, BRANCH_W), F32)
    for d in range(2):
        out = out.at[d, first_row + 32 * d:first_row + 32 * (d + 1)].set(up[d])
    return out


def rwkv_prep(rw, w0, w_up, a0, a_up, k_k, k_a, r_k):
    r = rw.shape[0]
    nt = r // ROW_TILE
    nch = r // CHUNK
    t = ROW_TILE
    ii = np.arange(t)
    same = (ii[:, None] // CHUNK) == (ii[None, :] // CHUNK)
    tri = jnp.asarray(np.stack([same & (ii[None, :] <= ii[:, None]), same & (ii[None, :] >= ii[:, None])]), BF16)
    pair_cat = lambda w: w.reshape(2, 128, N_HEADS // 2, 128).transpose(1, 2, 0, 3).reshape(128, 2 * BRANCH_W)
    w = 128 * PREP_PAIRS
    steps = N_HEADS // 2 // PREP_PAIRS
    lane = lambda blk: pl.BlockSpec((t, w), lambda i, p, blk=blk: (i, blk + p))
    pvec = pl.BlockSpec((1, w), lambda i, p: (0, p))
    dvec = pl.BlockSpec((2, 1, w), lambda i, p: (0, 0, p))
    dmat = pl.BlockSpec((128, 2 * w), lambda i, p: (0, p))
    gspec = pl.BlockSpec((CPT, 2, HEAD_DIM + CHUNK, w), lambda i, p: (i, 0, 0, p))
    gshape = jax.ShapeDtypeStruct((nch, 2, HEAD_DIM + CHUNK, BRANCH_W), F32)
    return pl.pallas_call(
        _rwkv_prep_body,
        out_shape=[gshape, gshape, jax.ShapeDtypeStruct((r, BRANCH_W), F32)],
        grid=(nt, steps),
        in_specs=[lane(0), lane(steps), lane(2 * steps), pl.BlockSpec((t, 128), lambda i, p: (i, 12)),
                  dvec, dmat, dvec, dmat, pvec, pvec, pvec,
                  pl.BlockSpec((2, t, t), lambda i, p: (0, 0, 0)),
                  pl.BlockSpec((128, 128), lambda i, p: (0, 0))],
        out_specs=[gspec, gspec, pl.BlockSpec((t, w), lambda i, p: (i, p))],
        compiler_params=_cparams(("parallel", "parallel")),
        name="rwkv_prep",
    )(rw, rw, rw, rw, w0.reshape(2, 1, BRANCH_W), pair_cat(_lora_ext(w_up, 0)), a0.reshape(2, 1, BRANCH_W),
      pair_cat(_lora_ext(a_up, 64)), k_k[None], k_a[None], r_k.reshape(1, BRANCH_W), tri,
      _block_sum_mat(128, 1.0).astype(BF16))


def _rwkv_scan_body(gf_ref, af_ref, gb_ref, ab_ref, yf_ref, yb_ref, st_ref):
    @pl.when(pl.program_id(0) == 0)
    def _():
        st_ref[...] = jnp.zeros(st_ref.shape, F32)

    rowh = lax.broadcasted_iota(jnp.int32, (128, 128), 0) // HEAD_DIM
    diag = rowh == _lane_half((128, 128))
    for s in range(CPT):
        for d, (g_ref, a_ref, y_ref, c) in enumerate(((gf_ref, af_ref, yf_ref, s), (gb_ref, ab_ref, yb_ref, CPT - 1 - s))):
            for p in range(N_HEADS // 2):
                ls = slice(p * 128, (p + 1) * 128)
                out = _dot3(g_ref[c, 0, :, ls], st_ref[d, p]) + a_ref[c, 0, :, ls]
                hn = out[:HEAD_DIM]
                st_ref[d, p] = jnp.where(diag, jnp.concatenate([hn, hn], axis=0), 0.0)
                y_ref[c * CHUNK:(c + 1) * CHUNK, ls] = out[HEAD_DIM:]


def rwkv_scan(g, add):
    nch = g.shape[0]
    r = nch * CHUNK
    nt = r // ROW_TILE
    assert CTX_LEN == ROW_TILE
    rev = lambda i: jnp.where(i == 0, 0, nt - i)
    blk = (CPT, 1, HEAD_DIM + CHUNK, BRANCH_W)
    fwd = pl.BlockSpec(blk, lambda i: (i, 0, 0, 0))
    bwd = pl.BlockSpec(blk, lambda i: (rev(i), 1, 0, 0))
    yshape = jax.ShapeDtypeStruct((r, BRANCH_W), F32)
    return pl.pallas_call(
        _rwkv_scan_body,
        out_shape=[yshape, yshape],
        grid=(nt,),
        in_specs=[fwd, fwd, bwd, bwd],
        out_specs=[pl.BlockSpec((ROW_TILE, BRANCH_W), lambda i: (i, 0)),
                   pl.BlockSpec((ROW_TILE, BRANCH_W), lambda i: (rev(i), 0))],
        scratch_shapes=[pltpu.VMEM((2, N_HEADS // 2, 128, 128), F32)],
        compiler_params=_cparams(("arbitrary",)),
        name="rwkv_scan",
    )(g, add, g, add)


HY_WIDTH = 512
HY_ORDER = 2
HY_POS_BANDS = 16
HY_HIDDEN = 64
HY_TAPS_W = 2 * HY_ORDER * HY_WIDTH
FFT_N2 = ROW_TILE


def _dot3c(ah, al, b):
    bh, bl = _split(b)
    dg = functools.partial(jnp.dot, preferred_element_type=F32)
    return dg(ah, bh) + (dg(al, bh) + dg(ah, bl))


def _dotc(ah, b):
    return jnp.dot(ah, b.astype(BF16), preferred_element_type=F32)


def _split_const(m):
    m = np.asarray(m, np.float32)
    hi = m.astype(BF16)
    lo = (m - hi.astype(np.float32)).astype(BF16)
    return jnp.asarray(hi), jnp.asarray(lo)


TAPS_FLAT_COLS = 8


def _filter_taps(t_idx, length, c2pb_ref, w1t_ref, w1c_ref, w1s_ref, b1_ref, w2_ref, b2_ref, w3_ref, b3_ref, absd_ref):
    t = t_idx / float(max(length - 1, 1))
    ang = c2pb_ref[...] * t_idx / float(length)
    pre = t * w1t_ref[...] + _dot3(jnp.cos(ang), w1c_ref[...]) - _dot3(jnp.sin(ang), w1s_ref[...]) + b1_ref[...]
    hid = jnp.sin(pre)
    hid = jnp.sin(_dot3(hid, w2_ref[...]) + b2_ref[...])
    return (_dot3(hid, w3_ref[...]) + b3_ref[...]) * jnp.exp(-t * absd_ref[...])


def _hy_taps_body(*refs, length):
    taps_ref, ssq_ref, tap0_ref = refs[-3:]
    i = pl.program_id(0)
    t_idx = (i * ROW_TILE + lax.broadcasted_iota(jnp.int32, (ROW_TILE, 1), 0)).astype(F32)
    taps = _filter_taps(t_idx, length, *refs[:-3])
    taps_ref[...] = taps

    @pl.when(i == 0)
    def _():
        ssq_ref[...] = jnp.zeros(ssq_ref.shape, F32)
        tap0_ref[...] = taps[0:1]

    ssq_ref[...] += jnp.sum(taps * taps, axis=0, keepdims=True)


def _hy_taps_flat_body(c2pb_ref, w1t_ref, w1c_ref, w1s_ref, b1_ref, w2_ref, b2_ref, w3_ref, b3_ref, absd_ref, f1_ref,
                       planes_ref, ssq_ref, tap0_ref, *, length):
    j = pl.program_id(0)
    rows = length // FFT_N2
    nb = TAPS_FLAT_COLS

    def positions(width, per):
        a = lax.broadcasted_iota(jnp.int32, (rows, width), 0)
        b = lax.broadcasted_iota(jnp.int32, (rows, width), 1) // per
        return (a * FFT_N2 + j * nb + b).astype(F32)

    @pl.when(j == 0)
    def _():
        ssq_ref[...] = jnp.zeros(ssq_ref.shape, F32)

    h = HY_HIDDEN
    ang = c2pb_ref[...] * positions(nb * HY_POS_BANDS, HY_POS_BANDS) / float(length)
    t_h = positions(nb * h, h) / float(max(length - 1, 1))
    pre = t_h * w1t_ref[...] + _dot3(jnp.cos(ang), w1c_ref[...]) - _dot3(jnp.sin(ang), w1s_ref[...]) + b1_ref[...]
    hid = jnp.sin(pre)
    hid = jnp.sin(_dot3(hid, w2_ref[...]) + b2_ref[...])
    for bp in range(nb // 2):
        t_w = (positions(2 * HY_TAPS_W, HY_TAPS_W) + float(2 * bp)) / float(max(length - 1, 1))
        taps = (_dot3(hid[:, bp * 2 * h:(bp + 1) * 2 * h], w3_ref[...]) + b3_ref[...]) * jnp.exp(-t_w * absd_ref[...])
        planes = _dotc(f1_ref[...], taps)
        planes_ref[:, bp * 2 * HY_TAPS_W:(bp + 1) * 2 * HY_TAPS_W] = planes.astype(planes_ref.dtype)
        if bp == 0:
            @pl.when(j == 0)
            def _():
                tap0_ref[...] = taps[0:1, :HY_TAPS_W]
        sq = jnp.sum(taps * taps, axis=0, keepdims=True)
        ssq_ref[...] += sq[:, :HY_TAPS_W] + sq[:, HY_TAPS_W:]


def hyena_taps(length, w1, b1, w2, b2, w3, b3, *, flat):
    bands = jnp.linspace(1e-4, HY_POS_BANDS - 1, HY_POS_BANDS, dtype=F32)
    c2pb = jnp.zeros((1, 128), F32).at[0, :HY_POS_BANDS].set(2.0 * math.pi * bands)
    pad = lambda m: jnp.zeros((128, HY_HIDDEN), F32).at[:HY_POS_BANDS].set(m)
    deltas = jnp.linspace(math.log(1e-2) / 0.3, math.log(1e-2) / 1.5, HY_WIDTH, dtype=F32)
    absd = jnp.tile(jnp.abs(deltas), 2 * HY_ORDER)[None]
    const = lambda shp: pl.BlockSpec(shp, lambda i: (0,) * len(shp))
    w1t, w1c, w1s = w1[0:1], w1[1:1 + HY_POS_BANDS], w1[1 + HY_POS_BANDS:]
    if flat:
        nb = TAPS_FLAT_COLS
        f1 = _FftPlan(length).f1[0]
        rows = f1.shape[0]
        bd = lambda m, k: jnp.kron(jnp.eye(k, dtype=F32), m)
        tile = lambda v, k: jnp.tile(v, k)[None]
        operands = (tile(c2pb[0, :HY_POS_BANDS], nb), tile(w1t[0], nb), bd(w1c, nb), bd(w1s, nb), tile(b1, nb),
                    bd(w2, nb), tile(b2, nb), bd(w3, 2), tile(b3, 2), tile(absd[0], 2), f1)
        body, grid = _hy_taps_flat_body, (FFT_N2 // nb,)
        taps_shape, taps_dtype = (rows, FFT_N2 * HY_TAPS_W), BF16
        taps_spec = pl.BlockSpec((rows, nb * HY_TAPS_W), lambda i: (0, i))
    else:
        operands = (c2pb, w1t, pad(w1c), pad(w1s), b1[None], w2, b2[None], w3, b3[None], absd)
        body, grid = _hy_taps_body, (length // ROW_TILE,)
        taps_shape, taps_dtype = (length, HY_TAPS_W), F32
        taps_spec = pl.BlockSpec((ROW_TILE, HY_TAPS_W), lambda i: (i, 0))
    taps, ssq, tap0 = pl.pallas_call(
        functools.partial(body, length=length),
        out_shape=[jax.ShapeDtypeStruct(taps_shape, taps_dtype), jax.ShapeDtypeStruct((1, HY_TAPS_W), F32),
                   jax.ShapeDtypeStruct((1, HY_TAPS_W), F32)],
        grid=grid,
        in_specs=[const(op.shape) for op in operands],
        out_specs=[taps_spec, const((1, HY_TAPS_W)), const((1, HY_TAPS_W))],
        compiler_params=_cparams(("arbitrary",)),
        name="hyena_taps_flat" if flat else "hyena_taps",
    )(*operands)
    hw = HY_TAPS_W // 2
    norm2 = ssq[:, :hw] + ssq[:, hw:] + 2.0 * tap0[:, :hw] * tap0[:, hw:]
    return taps, lax.rsqrt(norm2)


class _FftPlan:
    def __init__(self, length):
        self.length = length
        self.n = 2 * length
        self.n2 = FFT_N2
        self.n1 = self.n // self.n2
        self.n1h = self.n1 // 2
        k1 = self.n1h + 1
        self.k1p = -(-k1 // 8) * 8
        kk = np.arange(self.k1p)[:, None].astype(np.float64)
        live = (kk < k1)
        nn = np.arange(self.n1h)[None, :].astype(np.float64)
        th = 2.0 * np.pi * kk * nn / self.n1
        self.f1 = _split_const(np.concatenate([np.cos(th) * live, -np.sin(th) * live], axis=0))
        ck = np.where((kk == 0) | (kk == self.n1h), 1.0, 2.0) * live / self.n
        self.g1 = _split_const(np.concatenate([np.cos(th) * ck, -np.sin(th) * ck], axis=0).T)
        m = np.arange(self.n2).astype(np.float64)
        ph = 2.0 * np.pi * np.outer(m, m) / self.n2
        c, s = np.cos(ph), np.sin(ph)
        self.fb = _split_const(np.block([[c, s], [-s, c]]))
        self.fbi = _split_const(np.block([[c, -s], [s, c]]))
        tw = 2.0 * np.pi * kk[:, :, None] * m[None, :, None] / self.n
        self.twc = jnp.asarray(np.cos(tw), F32)
        self.tws = jnp.asarray(np.sin(tw), F32)


FFT_TN = 4096


def _fft_a_body(fh_ref, x_ref, o_ref):
    o_ref[...] = _dotc(fh_ref[...], x_ref[...]).astype(o_ref.dtype)


def fft_stage_a(plan, xf, lead):
    rows_in, m = xf.shape
    rows = 2 * plan.k1p
    tn = min(FFT_TN, m)
    fh = jnp.pad(plan.f1[0], ((0, 0), (lead, 0)))
    fspec = pl.BlockSpec((rows, rows_in), lambda j: (0, 0))
    return pl.pallas_call(
        _fft_a_body,
        out_shape=jax.ShapeDtypeStruct((rows, m), BF16),
        grid=(m // tn,),
        in_specs=[fspec, pl.BlockSpec((rows_in, tn), lambda j: (0, j))],
        out_specs=pl.BlockSpec((rows, tn), lambda j: (0, j)),
        compiler_params=_cparams(("parallel",)),
        name="fft_stage_a",
    )(fh, xf)


FFT_PLANES = 2


def _twiddled(a_ref, twc_ref, tws_ref, p):
    are, aim = a_ref[0, p].astype(F32), a_ref[1, p].astype(F32)
    c, s = twc_ref[p], tws_ref[p]
    return jnp.concatenate([are * c + aim * s, aim * c - are * s], axis=0)


def _fft_filter_b_body(a_ref, twc_ref, tws_ref, fbh_ref, scale_ref, o_ref):
    n2 = FFT_N2
    hw = HY_TAPS_W // 2
    xs = [_dotc(fbh_ref[...], _twiddled(a_ref, twc_ref, tws_ref, p)) for p in range(FFT_PLANES)]
    for p, x in enumerate(xs):
        xre, xim = x[:n2], x[n2:]
        o_ref[0, p] = ((xre[:, :hw] + xre[:, hw:]) * scale_ref[...]).astype(o_ref.dtype)
        o_ref[1, p] = ((xim[:, :hw] - xim[:, hw:]) * scale_ref[...]).astype(o_ref.dtype)


def fft_filter_stage_b(plan, a, scale):
    n2, k1p = plan.n2, plan.k1p
    hw = HY_TAPS_W // 2
    const = lambda shp: pl.BlockSpec(shp, lambda k: (0,) * len(shp))
    return pl.pallas_call(
        _fft_filter_b_body,
        out_shape=jax.ShapeDtypeStruct((2, k1p, n2, hw), BF16),
        grid=(k1p // FFT_PLANES,),
        in_specs=[pl.BlockSpec((2, FFT_PLANES, n2, HY_TAPS_W), lambda k: (0, k, 0, 0)),
                  pl.BlockSpec((FFT_PLANES, n2, 1), lambda k: (k, 0, 0)),
                  pl.BlockSpec((FFT_PLANES, n2, 1), lambda k: (k, 0, 0)),
                  const((2 * n2, 2 * n2)), const((1, hw))],
        out_specs=pl.BlockSpec((2, FFT_PLANES, n2, hw), lambda k: (0, k, 0, 0)),
        compiler_params=_cparams(("parallel",)),
        name="fft_filter_stage_b",
    )(a, plan.twc, plan.tws, plan.fb[0], scale)


def _fft_conv_b_body(a_ref, kf_ref, twc_ref, tws_ref, fbh_ref, fih_ref, o_ref):
    n2 = FFT_N2
    planes = range(FFT_PLANES)
    zs = [_dotc(fbh_ref[...], _twiddled(a_ref, twc_ref, tws_ref, p)) for p in planes]
    ys = []
    for p, z in zip(planes, zs):
        zre, zim = z[:n2], z[n2:]
        kre, kim = kf_ref[0, p].astype(F32), kf_ref[1, p].astype(F32)
        ys.append(jnp.concatenate([zre * kre - zim * kim, zre * kim + zim * kre], axis=0))
    qs = [_dotc(fih_ref[...], y) for y in ys]
    for p, q in zip(planes, qs):
        qre, qim = q[:n2], q[n2:]
        c, s = twc_ref[p], tws_ref[p]
        o_ref[0, p] = (qre * c - qim * s).astype(o_ref.dtype)
        o_ref[1, p] = (qim * c + qre * s).astype(o_ref.dtype)


def fft_conv_stage_b(plan, a, kf, order):
    n2, k1p = plan.n2, plan.k1p
    const = lambda shp: pl.BlockSpec(shp, lambda k: (0,) * len(shp))
    return pl.pallas_call(
        _fft_conv_b_body,
        out_shape=jax.ShapeDtypeStruct((2, k1p, n2, HY_WIDTH), BF16),
        grid=(k1p // FFT_PLANES,),
        in_specs=[pl.BlockSpec((2, FFT_PLANES, n2, HY_WIDTH), lambda k: (0, k, 0, 0)),
                  pl.BlockSpec((2, FFT_PLANES, n2, HY_WIDTH), lambda k: (0, k, 0, order)),
                  pl.BlockSpec((FFT_PLANES, n2, 1), lambda k: (k, 0, 0)),
                  pl.BlockSpec((FFT_PLANES, n2, 1), lambda k: (k, 0, 0)),
                  const((2 * n2, 2 * n2)), const((2 * n2, 2 * n2))],
        out_specs=pl.BlockSpec((2, FFT_PLANES, n2, HY_WIDTH), lambda k: (0, k, 0, 0)),
        compiler_params=_cparams(("parallel",)),
        name="fft_conv_stage_b",
    )(a, kf, plan.twc, plan.tws, plan.fb[0], plan.fbi[0])


def _fft_inv_a_body(gh_ref, fh_ref, q_ref, z_ref, gate_ref, skip_ref, o_ref, *next_ref):
    y = _dotc(gh_ref[...], q_ref[...])
    z_next = gate_ref[...] * (y + z_ref[...] * skip_ref[...])
    o_ref[...] = z_next.astype(o_ref.dtype)
    if next_ref:
        next_ref[0][...] = _dotc(fh_ref[...], z_next).astype(next_ref[0].dtype)


def fft_inv_stage_a(plan, qf, zf, gatef, skip_t, lead, with_next):
    rows_out, m = zf.shape
    rows = 2 * plan.k1p
    tn = skip_t.shape[1]
    gh = jnp.pad(plan.g1[0], ((lead, 0), (0, 0)))
    fh = jnp.pad(plan.f1[0], ((0, 0), (lead, 0)))
    col = lambda r_: pl.BlockSpec((r_, tn), lambda j: (0, j))
    out_shape = [jax.ShapeDtypeStruct((rows_out, m), zf.dtype)]
    out_specs = [col(rows_out)]
    if with_next:
        out_shape.append(jax.ShapeDtypeStruct((rows, m), BF16))
        out_specs.append(col(rows))
    return pl.pallas_call(
        _fft_inv_a_body,
        out_shape=out_shape,
        grid=(m // tn,),
        in_specs=[pl.BlockSpec((rows_out, rows), lambda j: (0, 0)), pl.BlockSpec((rows, rows_out), lambda j: (0, 0)),
                  col(rows), col(rows_out), col(rows_out), pl.BlockSpec((1, tn), lambda j: (0, 0))],
        out_specs=out_specs,
        compiler_params=_cparams(("parallel",)),
        name="fft_inv_stage_a",
    )(gh, fh, qf, zf, gatef, skip_t)


def hyena_long(hv, hx1, hx2, ta, scale, skip, lead):
    length = hv.shape[0] - lead * FFT_N2
    plan = _FftPlan(length)
    n2, k1p = plan.n2, plan.k1p
    m = n2 * HY_WIDTH
    tn = min(FFT_TN, m)
    flat = lambda a: a.reshape(lead + plan.n1h, m)
    kf = fft_filter_stage_b(plan, ta.reshape(2, k1p, n2, HY_TAPS_W), scale)
    z = flat(hv)
    a = fft_stage_a(plan, z, lead)
    gates = (hx1, hx2)
    for o, gate in enumerate(gates):
        q = fft_conv_stage_b(plan, a.reshape(2, k1p, n2, HY_WIDTH), kf, o)
        last = o == len(gates) - 1
        out = fft_inv_stage_a(plan, q.reshape(2 * k1p, m), z, flat(gate), jnp.tile(skip[o], tn // HY_WIDTH)[None], lead,
                              not last)
        z = out[0]
        if not last:
            a = out[1]
    return z.reshape(-1, HY_WIDTH)


def _rec_out_body(x_ref, yf_ref, yb_ref, bonus_ref, sgr_ref, zh_ref, zc_ref, sgh_ref, gnw_ref, gnb_ref, bm_ref, wa_ref, wb_ref,
                  gate_ref, fin_ref, o_ref, *, final):
    y = yf_ref[...] + yb_ref[...]
    bm = bm_ref[...]
    mean = _dot_stat(y, bm)
    yc = y - mean
    var = _dot_stat(yc * yc, bm)
    yn = yc * lax.rsqrt(var + RWKV_GN_EPS) * gnw_ref[...] + gnb_ref[...]
    ya = ((yn + bonus_ref[...]) * sgr_ref[...]).astype(BF16)
    zh = zh_ref[...]
    if not final:
        zh = jnp.where(pl.program_id(0) == 0, zc_ref[...], zh)
    yh = (zh * sgh_ref[...]).astype(BF16)
    out = jnp.dot(ya, wa_ref[...], preferred_element_type=F32) + jnp.dot(yh, wb_ref[...], preferred_element_type=F32)
    xn = x_ref[...] + gate_ref[0] * out
    if final:
        xn = xn * lax.rsqrt(jnp.mean(xn * xn, axis=-1, keepdims=True) + NORM_EPS) * fin_ref[...]
    o_ref[...] = xn


RWKV_GN_EPS = 64e-5


def rec_out_proj(xs, yf, yb, bonus, sgr, zh, zc, sgh, gn_w, gn_b, w_out, gate2, final_g, *, final):
    r = xs.shape[0]
    d = D_MODEL
    off = 1 if final else 0
    nt = r // ROW_TILE - off
    rows = lambda w: pl.BlockSpec((ROW_TILE, w), lambda i: (i + off, 0))
    const = lambda shp: pl.BlockSpec(shp, lambda i: (0,) * len(shp))
    bw = BRANCH_W
    return pl.pallas_call(
        functools.partial(_rec_out_body, final=final),
        out_shape=jax.ShapeDtypeStruct((nt * ROW_TILE, d), F32),
        grid=(nt,),
        in_specs=[rows(d), rows(bw), rows(bw), rows(bw), rows(bw), rows(bw), const((CTX_LEN, bw)), rows(bw),
                  const((1, bw)), const((1, bw)),
                  const((bw, bw)), const((bw, d)), const((bw, d)),
                  pl.BlockSpec((1, 1, d), lambda i: (jnp.minimum(i + off, 1), 0, 0)), const((1, d))],
        out_specs=pl.BlockSpec((ROW_TILE, d), lambda i: (i, 0)),
        compiler_params=_cparams(("parallel",)),
        name="rec_out_proj",
    )(xs, yf, yb, bonus, sgr, zh, zc, sgh, gn_w[None], gn_b[None], _block_sum_mat(bw, 1.0 / HEAD_DIM).astype(BF16),
      w_out[:bw].astype(BF16), w_out[bw:].astype(BF16), gate2, final_g[None])


def rec_layer(xs, mods, norm_g, w_in, mu, w0, w_up, a0, a_up, k_k, k_a, r_k, gn_w, gn_b, hy_short, hy_w1, hy_b1, hy_w2,
              hy_b2, hy_w3, hy_b3, hy_skip, w_out, final_g, final):
    shift2, scale2, gate2 = mods
    rw, hv, hx1, hx2, sgr, sgh = rec_in_proj(xs, norm_g, scale2, shift2, w_in, mu, hy_short)
    g, add, bonus = rwkv_prep(rw, w0, w_up, a0, a_up, k_k, k_a, r_k)
    yf, yb = rwkv_scan(g, add)
    fargs = (hy_w1, hy_b1, hy_w2, hy_b2, hy_w3, hy_b3)
    n = xs.shape[0] - CTX_LEN
    taps, scale = hyena_taps(n, *fargs, flat=True)
    zh = hyena_long(hv, hx1, hx2, taps, scale, hy_skip, CTX_LEN // FFT_N2)
    if final:
        z_ctx = hv[:CTX_LEN]
    else:
        taps_c, scale_c = hyena_taps(CTX_LEN, *fargs, flat=False)
        z_ctx = hyena_short(hv[:CTX_LEN], hx1[:CTX_LEN], hx2[:CTX_LEN], taps_c, scale_c, hy_skip)
    return rec_out_proj(xs, yf, yb, bonus, sgr, zh, z_ctx, sgh, gn_w, gn_b, w_out, gate2, final_g, final=final)


def kernel(x, c, ctx, c_ctx, attn_norm, attn_ada_w, attn_ada_b, attn_w_in, na_rpb, gqa_q_gain, gqa_k_gain, attn_w_out,
           rec_norm, rec_ada_w, rec_ada_b, rec_w_in, rwkv_mu, rwkv_w0, rwkv_w_up, rwkv_a0, rwkv_a_up, rwkv_k_k, rwkv_k_a,
           rwkv_r_k, rwkv_gn_w, rwkv_gn_b, hy_short, hy_w1, hy_b1, hy_w2, hy_b2, hy_w3, hy_b3, hy_skip, rec_w_out,
           final_norm):
    assert x.shape[0] == 1 and ctx.shape[1] == CTX_LEN and x.shape[2] == D_MODEL
    n = x.shape[1]
    assert n % ROW_TILE == 0 and n // ROW_TILE >= 3
    assert attn_w_in.shape[0] == rec_w_in.shape[0]
    d = D_MODEL
    cond8 = jnp.zeros((8, d), F32).at[0].set(c_ctx).at[1].set(c[0])
    m_attn = adaln_all(cond8, attn_ada_w, attn_ada_b)
    m_rec = adaln_all(cond8, rec_ada_w, rec_ada_b)
    mods = lambda m, i: tuple(m[i, :2, j * d:(j + 1) * d].reshape(2, 1, d) for j in range(3))
    cos_t, sin_t = _rope_tables(n)
    xs = jnp.concatenate([ctx[0], x[0]], axis=0)
    depth = attn_w_in.shape[0] + rec_w_in.shape[0]
    for layer in range(depth):
        i = layer // 2
        final = layer == depth - 1
        if layer % 2 == 0:
            xs = attn_layer(xs, mods(m_attn, i), attn_norm[i], attn_w_in[i], na_rpb[i], gqa_q_gain[i], gqa_k_gain[i],
                            attn_w_out[i], cos_t, sin_t)
        else:
            xs = rec_layer(xs, mods(m_rec, i), rec_norm[i], rec_w_in[i], rwkv_mu[i], rwkv_w0[i], rwkv_w_up[i],
                           rwkv_a0[i], rwkv_a_up[i], rwkv_k_k[i], rwkv_k_a[i], rwkv_r_k[i], rwkv_gn_w[i], rwkv_gn_b[i],
                           hy_short[i], hy_w1[i], hy_b1[i], hy_w2[i], hy_b2[i], hy_w3[i], hy_b3[i], hy_skip[i],
                           rec_w_out[i], final_norm, final)
    return xs[None]


def _hy_short_body(fh_ref, fl_ref, gh_ref, gl_ref, v_ref, x1_ref, x2_ref, taps_ref, scale_ref, skip_ref, o_ref):
    fh, fl, gh, gl = fh_ref[...], fl_ref[...], gh_ref[...], gl_ref[...]
    kp = fh.shape[0] // 2
    hw = HY_TAPS_W // 2
    tf = _dot3c(fh, fl, taps_ref[...])
    kre = (tf[:kp, :hw] + tf[:kp, hw:]) * scale_ref[...]
    kim = (tf[kp:, :hw] - tf[kp:, hw:]) * scale_ref[...]
    z = v_ref[...].astype(F32)
    for o, gate_ref in enumerate((x1_ref, x2_ref)):
        ls = slice(o * HY_WIDTH, (o + 1) * HY_WIDTH)
        zf = _dot3c(fh, fl, z)
        zre, zim = zf[:kp], zf[kp:]
        y = jnp.concatenate([zre * kre[:, ls] - zim * kim[:, ls], zre * kim[:, ls] + zim * kre[:, ls]], axis=0)
        z = gate_ref[...] * (_dot3c(gh, gl, y) + z * skip_ref[o:o + 1])
    o_ref[...] = z


def hyena_short(hv, hx1, hx2, taps, scale, skip):
    length = hv.shape[0]
    n = 2 * length
    k1 = length + 1
    kp = -(-k1 // 8) * 8
    kk = np.arange(kp)[:, None].astype(np.float64)
    live = kk < k1
    th = 2.0 * np.pi * kk * np.arange(length)[None, :] / n
    f = _split_const(np.concatenate([np.cos(th) * live, -np.sin(th) * live], axis=0))
    ck = np.where((kk == 0) | (kk == length), 1.0, 2.0) * live / n
    g = _split_const(np.concatenate([np.cos(th) * ck, -np.sin(th) * ck], axis=0).T)
    return pl.pallas_call(
        _hy_short_body,
        out_shape=jax.ShapeDtypeStruct((length, HY_WIDTH), F32),
        compiler_params=pltpu.CompilerParams(vmem_limit_bytes=VMEM_LIMIT),
        name="hyena_short",
    )(f[0], f[1], g[0], g[1], hv, hx1, hx2, taps, scale, skip)
```

```python
import functools
import math

import jax
import jax.numpy as jnp
import numpy as np
from jax import lax
from jax.experimental import pallas as pl
from jax.experimental.pallas import tpu as pltpu

F32 = jnp.float32
BF16 = jnp.bfloat16
HIGHEST = lax.Precision.HIGHEST

D_MODEL = 1024
GRID_W = 64
CTX_LEN = 256
HEAD_DIM = 64
BRANCH_W = 512
N_HEADS = 8
GQA_KV_W = 128
NA_WIN_ROWS = 8
NA_WIN_COLS = 16
ROPE_THETA = 10000.0
ROPE_FREQS = 16
NORM_EPS = 1e-6
ROW_TILE = 256
NA_GROUP_ROWS = 4
NEG_BIG = -1e30
LOG2E = math.log2(math.e)
QK_SCALE = HEAD_DIM ** -0.5 * LOG2E
FLASH_Q_TILES = 4
FLASH_SCORES_AHEAD = 3
NA_SCORES_AHEAD = 1
VMEM_LIMIT = 56 * 1024 * 1024

ATTN_SPLITS = (512, 512, 512, 512, 512, 128, 128, 512)
GQA_HEAD_ORDER = (0, 4, 1, 5, 2, 6, 3, 7)


def _cparams(sem):
    return pltpu.CompilerParams(dimension_semantics=sem, vmem_limit_bytes=VMEM_LIMIT)


def _silu(v):
    return v * (1.0 / (1.0 + jnp.exp(-v)))


def _lane_half(shape):
    return (lax.broadcasted_iota(jnp.int32, shape, len(shape) - 1) // HEAD_DIM) % 2


def _dot_stat(a, block_mat):
    hi = a.astype(BF16)
    lo = (a - hi.astype(F32)).astype(BF16)
    return jnp.dot(hi, block_mat, preferred_element_type=F32) + jnp.dot(lo, block_mat, preferred_element_type=F32)


def _dot_nt(a, b):
    return lax.dot_general(a, b, (((1,), (1,)), ((), ())), preferred_element_type=F32)


def _adaln_body(cond_ref, w_ref, b_ref, o_ref):
    s = _silu(cond_ref[...])
    o_ref[0] = jnp.dot(s, w_ref[0], precision=HIGHEST, preferred_element_type=F32) + b_ref[0]


def adaln_all(cond8, ada_w, ada_b):
    nl = ada_w.shape[0]
    d = D_MODEL
    return pl.pallas_call(
        _adaln_body,
        out_shape=jax.ShapeDtypeStruct((nl, 8, 3 * d), F32),
        grid=(nl, 3),
        in_specs=[
            pl.BlockSpec((8, d), lambda l, j: (0, 0)),
            pl.BlockSpec((1, d, d), lambda l, j: (l, 0, j)),
            pl.BlockSpec((1, 1, d), lambda l, j: (l, 0, j)),
        ],
        out_specs=pl.BlockSpec((1, 8, d), lambda l, j: (l, 0, j)),
        compiler_params=_cparams(("parallel", "parallel")),
        name="adaln",
    )(cond8, ada_w, ada_b.reshape(nl, 1, 3 * d))


def _modulated(x_ref, g_ref, scale_ref, shift_ref):
    xf = x_ref[...]
    y = xf * lax.rsqrt(jnp.mean(xf * xf, axis=-1, keepdims=True) + NORM_EPS)
    return (y * g_ref[...]) * (1.0 + scale_ref[0]) + shift_ref[0]


def _mod_specs():
    d = D_MODEL
    return [
        pl.BlockSpec((ROW_TILE, d), lambda i: (i, 0)),
        pl.BlockSpec((1, d), lambda i: (0, 0)),
        pl.BlockSpec((1, 1, d), lambda i: (jnp.minimum(i, 1), 0, 0)),
        pl.BlockSpec((1, 1, d), lambda i: (jnp.minimum(i, 1), 0, 0)),
    ]


def _attn_in_body(x_ref, g_ref, scale_ref, shift_ref, w_ref, cos_ref, sin_ref, gq_ref, gqs_ref, gk_ref, gks_ref,
                  bdq_ref, bdk_ref,
                  qa_ref, ka_ref, va_ref, sga_ref, qb_ref, kb_ref, vb_ref, sgb_ref):
    xm = _modulated(x_ref, g_ref, scale_ref, shift_ref).astype(BF16)
    u = jnp.dot(xm, w_ref[...], preferred_element_type=F32)
    qa, ka, va, ga = u[:, 0:512], u[:, 512:1024], u[:, 1024:1536], u[:, 1536:2048]
    qb, kb, vb, gb = u[:, 2048:2560], u[:, 2560:2688], u[:, 2688:2816], u[:, 2816:3328]
    qbs, kbs = u[:, 3328:3840], u[:, 3840:3968]
    scale = QK_SCALE
    qa_ref[...] = (qa * scale).astype(BF16)
    ka_ref[...] = ka.astype(BF16)
    va_ref[...] = va.astype(BF16)
    sga_ref[...] = _silu(ga).astype(sga_ref.dtype)
    sgb_ref[...] = _silu(gb).astype(sgb_ref.dtype)
    vb_ref[...] = jnp.transpose(vb).astype(BF16)
    cos_k, sin_k = cos_ref[...], sin_ref[...]
    cos_q = jnp.concatenate([cos_k] * 4, axis=1)
    sin_q = jnp.concatenate([sin_k] * 4, axis=1)
    rs_q = lax.rsqrt(_dot_stat(qb * qb, bdq_ref[...]) + NORM_EPS)
    rs_k = lax.rsqrt(_dot_stat(kb * kb, bdk_ref[...]) + NORM_EPS)
    qr = rs_q * (qb * gq_ref[...] * cos_q + qbs * gqs_ref[...] * sin_q)
    kr = rs_k * (kb * gk_ref[...] * cos_k + kbs * gks_ref[...] * sin_k)
    qb_ref[...] = jnp.transpose(qr * scale).astype(BF16)
    kb_ref[...] = kr.astype(BF16)


def _rope_tables(n):
    t = jnp.arange(n, dtype=jnp.int32)
    pos = jnp.stack([t // GRID_W, t % GRID_W], axis=-1).astype(F32)
    inv_freq = ROPE_THETA ** (-jnp.arange(ROPE_FREQS, dtype=F32) / ROPE_FREQS)
    ang = pos[:, :, None] * inv_freq
    c, s = jnp.cos(ang), jnp.sin(ang)
    cos64 = jnp.concatenate([c[:, 0], c[:, 0], c[:, 1], c[:, 1]], axis=-1)
    sin64 = jnp.concatenate([-s[:, 0], s[:, 0], -s[:, 1], s[:, 1]], axis=-1)
    cos64 = jnp.concatenate([jnp.ones((CTX_LEN, HEAD_DIM), F32), cos64], axis=0)
    sin64 = jnp.concatenate([jnp.zeros((CTX_LEN, HEAD_DIM), F32), sin64], axis=0)
    return jnp.tile(cos64, (1, 2)), jnp.tile(sin64, (1, 2))


def _reorder_heads(w, order, axis):
    take = lambda h: lax.slice_in_dim(w, h * HEAD_DIM, (h + 1) * HEAD_DIM, axis=axis)
    return jnp.concatenate([take(h) for h in order], axis=axis)


def _swap_rope_halves(w):
    shp = w.shape
    return jnp.flip(w.reshape(shp[:-1] + (shp[-1] // (2 * ROPE_FREQS), 2, ROPE_FREQS)), axis=-2).reshape(shp)


def attn_in_proj(xs, norm_g, scale2, shift2, w_in, q_gain, k_gain, cos_t, sin_t):
    r = xs.shape[0]
    d = D_MODEL
    parts, start = [], 0
    for s in ATTN_SPLITS:
        parts.append(w_in[:, start:start + s])
        start += s
    wqa, wka, wva, wga, wqb, wkb, wvb, wgb = parts
    wqb_p = _reorder_heads(wqb, GQA_HEAD_ORDER, 1)
    wgb_p = _reorder_heads(wgb, GQA_HEAD_ORDER, 1)
    wqb_sw = _swap_rope_halves(wqb_p)
    wkb_sw = _swap_rope_halves(wkb)
    w_ext = jnp.concatenate([wqa, wka, wva, wga, wqb_p, wkb, wvb, wgb_p, wqb_sw, wkb_sw], axis=1).astype(BF16)
    gq = jnp.tile(q_gain, N_HEADS)[None]
    gqs = jnp.tile(_swap_rope_halves(q_gain), N_HEADS)[None]
    gk = jnp.tile(k_gain, 2)[None]
    gks = jnp.tile(_swap_rope_halves(k_gain), 2)[None]
    bdq = jnp.asarray(np.kron(np.eye(N_HEADS), np.full((HEAD_DIM, HEAD_DIM), 1.0 / HEAD_DIM)), BF16)
    bdk = jnp.asarray(np.kron(np.eye(2), np.full((HEAD_DIM, HEAD_DIM), 1.0 / HEAD_DIM)), BF16)
    wcols = w_ext.shape[1]
    const = lambda shp: pl.BlockSpec(shp, lambda i: (0,) * len(shp))
    rows = lambda w: pl.BlockSpec((ROW_TILE, w), lambda i: (i, 0))
    out_shapes = [
        jax.ShapeDtypeStruct((r, 512), BF16), jax.ShapeDtypeStruct((r, 512), BF16), jax.ShapeDtypeStruct((r, 512), BF16),
        jax.ShapeDtypeStruct((r, 512), BF16),
        jax.ShapeDtypeStruct((512, r), BF16), jax.ShapeDtypeStruct((r, 128), BF16), jax.ShapeDtypeStruct((128, r), BF16),
        jax.ShapeDtypeStruct((r, 512), BF16),
    ]
    cols = lambda w: pl.BlockSpec((w, ROW_TILE), lambda i: (0, i))
    return pl.pallas_call(
        _attn_in_body,
        out_shape=out_shapes,
        grid=(r // ROW_TILE,),
        in_specs=_mod_specs() + [const((d, wcols)), rows(128), rows(128), const((1, 512)), const((1, 512)),
                                 const((1, 128)), const((1, 128)), const((512, 512)), const((128, 128))],
        out_specs=[rows(512), rows(512), rows(512), rows(512), cols(512), rows(128), cols(128), rows(512)],
        compiler_params=_cparams(("parallel",)),
        name="attn_in_proj",
    )(xs, norm_g[None], scale2, shift2, w_ext, cos_t, sin_t, gq, gqs, gk, gks, bdq, bdk)


def _na_cols_body(rpb_ref, sel_ref, neg_ref, o_ref):
    o_ref[...] = jnp.dot(rpb_ref[...], sel_ref[...], precision=HIGHEST, preferred_element_type=F32) + neg_ref[...]


def _na_bias_tables(rpb, rows):
    nrel_r, nrel_c = 2 * NA_WIN_ROWS - 1, 2 * NA_WIN_COLS - 1
    qc = np.arange(GRID_W)[:, None]
    kc = np.arange(GRID_W)[None, :]
    col0 = np.clip(qc - NA_WIN_COLS // 2, 0, GRID_W - NA_WIN_COLS)
    col_ok = (kc >= col0) & (kc < col0 + NA_WIN_COLS)
    rc = kc - qc + NA_WIN_COLS - 1
    sel = np.zeros((128, GRID_W * GRID_W), np.float32)
    sel[np.where(col_ok, rc, 127).reshape(-1), np.arange(GRID_W * GRID_W)] = col_ok.reshape(-1)
    neg = np.where(col_ok, 0.0, NEG_BIG).astype(np.float32).reshape(1, -1)
    rpb2 = jnp.zeros((128, 128), F32).at[:N_HEADS * nrel_r, :nrel_c].set(rpb.reshape(N_HEADS * nrel_r, nrel_c))
    cols = pl.pallas_call(
        _na_cols_body,
        out_shape=jax.ShapeDtypeStruct((128, GRID_W * GRID_W), F32),
        name="na_bias_cols",
    )(rpb2, jnp.asarray(sel), jnp.asarray(neg))
    cols = cols[:N_HEADS * nrel_r].reshape(N_HEADS, nrel_r, GRID_W, GRID_W)
    kh = min(NA_WIN_ROWS, rows)
    g = rows // NA_GROUP_ROWS
    cases = [(0, 0), (NA_GROUP_ROWS, 0), (rows - NA_GROUP_ROWS, NA_GROUP_ROWS * (g - 3))]
    masked = jnp.full((N_HEADS, GRID_W, GRID_W), NEG_BIG, F32)
    tabs = []
    for qr_first, start in cases:
        blocks = []
        for j in range(NA_GROUP_ROWS):
            qr = qr_first + j
            row0 = min(max(qr - kh // 2, 0), rows - kh)
            for i in range(3 * NA_GROUP_ROWS):
                kr = start + i
                blocks.append(cols[:, kr - qr + NA_WIN_ROWS - 1] if row0 <= kr < row0 + kh else masked)
        tab = jnp.stack(blocks, axis=1).reshape(N_HEADS, NA_GROUP_ROWS, 3 * NA_GROUP_ROWS, GRID_W, GRID_W)
        tabs.append(tab.transpose(0, 1, 3, 2, 4).reshape(N_HEADS, ROW_TILE, 3 * ROW_TILE))
    return jnp.stack(tabs) * LOG2E


def _na_body(q_ref, kc_ref, k0_ref, k1_ref, k2_ref, vc_ref, v0_ref, v1_ref, v2_ref, bias_ref, sg_ref, o_ref):
    half = _lane_half((ROW_TILE, 128))
    k_refs = (k0_ref, k1_ref, k2_ref, kc_ref)
    v_refs = (v0_ref, v1_ref, v2_ref, vc_ref)

    def scores(h):
        hp, j = divmod(h, 2)
        ls = slice(hp * 128, (hp + 1) * 128)
        qp = q_ref[:, ls]
        qm = jnp.where(half == j, qp, jnp.zeros_like(qp))
        return [_dot_nt(qm, r[:, ls]) for r in k_refs]

    pending = [scores(h) for h in range(NA_SCORES_AHEAD)]
    outs = []
    for h in range(N_HEADS):
        hp, j = divmod(h, 2)
        ls = slice(hp * 128, (hp + 1) * 128)
        s = pending.pop(0)
        if h + NA_SCORES_AHEAD < N_HEADS:
            pending.append(scores(h + NA_SCORES_AHEAD))
        s_win = jnp.concatenate(s[:3], axis=1) + bias_ref[0, h]
        s_ctx = s[3]
        m = jnp.maximum(jnp.max(s_win, axis=1, keepdims=True), jnp.max(s_ctx, axis=1, keepdims=True))
        p_win = jnp.exp2(s_win - m).astype(BF16)
        p_ctx = jnp.exp2(s_ctx - m).astype(BF16)
        vms = [jnp.where(half == j, r[:, ls], jnp.ones((ROW_TILE, 128), BF16)) for r in v_refs]
        o = jnp.dot(p_ctx, vms[3], preferred_element_type=F32)
        for b in range(3):
            o += jnp.dot(p_win[:, b * ROW_TILE:(b + 1) * ROW_TILE], vms[b], preferred_element_type=F32)
        outs.append(o / pltpu.roll(o, HEAD_DIM, 1))
        if j == 1:
            o_pair = jnp.where(half == 0, outs[h - 1], outs[h])
            o_ref[:, ls] = (o_pair * sg_ref[:, ls]).astype(BF16)


def na_attention(qa, ka, va, sga, bias_tabs, n):
    g = n // ROW_TILE
    w = BRANCH_W

    def kv_spec(off):
        return pl.BlockSpec((ROW_TILE, w), lambda i: (jnp.clip(i - 1, 0, g - 3) + off + 1, 0))

    ctx_spec = pl.BlockSpec((ROW_TILE, w), lambda i: (0, 0))
    q_spec = pl.BlockSpec((ROW_TILE, w), lambda i: (i + 1, 0))
    case = lambda i: jnp.where(i == 0, 0, jnp.where(i == g - 1, 2, 1))
    bias_spec = pl.BlockSpec((1, N_HEADS, ROW_TILE, 3 * ROW_TILE), lambda i: (case(i), 0, 0, 0))
    return pl.pallas_call(
        _na_body,
        out_shape=jax.ShapeDtypeStruct((n, w), BF16),
        grid=(g,),
        in_specs=[q_spec, ctx_spec, kv_spec(0), kv_spec(1), kv_spec(2), ctx_spec, kv_spec(0), kv_spec(1), kv_spec(2),
                  bias_spec, q_spec],
        out_specs=pl.BlockSpec((ROW_TILE, w), lambda i: (i, 0)),
        compiler_params=_cparams(("parallel",)),
        name="na_attention",
    )(qa, ka, ka, ka, ka, va, va, va, va, bias_tabs, sga)


def _flash_mha_body(q_ref, k_ref, v_ref, sg_ref, o_ref, m_ref, acc_ref):
    kv = pl.program_id(1)
    tq = q_ref.shape[0]

    @pl.when(kv == 0)
    def _():
        m_ref[...] = jnp.full(m_ref.shape, NEG_BIG, F32)
        acc_ref[...] = jnp.zeros(acc_ref.shape, F32)

    khalf = _lane_half((k_ref.shape[0], 128))
    for p in range(N_HEADS // 2):
        ls = slice(p * 128, (p + 1) * 128)
        qp = q_ref[:, ls]
        kp = k_ref[:, ls]
        vp = v_ref[:, ls]
        for j in range(2):
            hh = 2 * p + j
            km = jnp.where(khalf == j, kp, jnp.zeros_like(kp))
            vm = jnp.where(khalf == j, vp, jnp.ones_like(vp))
            s = _dot_nt(qp, km)
            m_prev = m_ref[hh]
            m_new = jnp.maximum(m_prev, jnp.max(s, axis=1, keepdims=True))
            alpha = jnp.exp2(m_prev - m_new)
            pr = jnp.exp2(s - m_new[:, :1]).astype(BF16)
            acc_ref[hh] = alpha * acc_ref[hh] + jnp.dot(pr, vm, preferred_element_type=F32)
            m_ref[hh] = m_new

    @pl.when(kv == pl.num_programs(1) - 1)
    def _():
        half = _lane_half((tq, 128))
        for p in range(N_HEADS // 2):
            ls = slice(p * 128, (p + 1) * 128)
            a0, a1 = acc_ref[2 * p], acc_ref[2 * p + 1]
            o0 = a0 / pltpu.roll(a0, HEAD_DIM, 1)
            o1 = a1 / pltpu.roll(a1, HEAD_DIM, 1)
            o_ref[:, ls] = (jnp.where(half == 0, o0, o1) * sg_ref[:, ls]).astype(BF16)


def flash_mha(q, k, v, sg, *, q_block0, nq, tk, nk):
    tq = ROW_TILE
    return pl.pallas_call(
        _flash_mha_body,
        out_shape=jax.ShapeDtypeStruct((nq * tq, BRANCH_W), BF16),
        grid=(nq, nk),
        in_specs=[
            pl.BlockSpec((tq, BRANCH_W), lambda i, j: (i + q_block0, 0)),
            pl.BlockSpec((tk, BRANCH_W), lambda i, j: (j, 0)),
            pl.BlockSpec((tk, BRANCH_W), lambda i, j: (j, 0)),
            pl.BlockSpec((tq, BRANCH_W), lambda i, j: (i + q_block0, 0)),
        ],
        out_specs=pl.BlockSpec((tq, BRANCH_W), lambda i, j: (i, 0)),
        scratch_shapes=[pltpu.VMEM((N_HEADS, tq, 128), F32)] * 2,
        compiler_params=_cparams(("parallel", "arbitrary")),
        name="flash_mha",
    )(q, k, v, sg)


def _flash_gqa_body(*refs, nsub):
    qt_refs, (k_ref, vt_ref), sg_refs = refs[:nsub], refs[nsub:nsub + 2], refs[nsub + 2:2 * nsub + 2]
    o_ref, m_ref, acc_ref = refs[2 * nsub + 2:]
    kv = pl.program_id(1)
    tq = qt_refs[0].shape[1]
    tk = k_ref.shape[0]

    @pl.when(kv == 0)
    def _():
        m_ref[...] = jnp.full(m_ref.shape, NEG_BIG, F32)
        acc_ref[...] = jnp.zeros(acc_ref.shape, F32)

    khalf = _lane_half((tk, 128))
    vhalf = lax.broadcasted_iota(jnp.int32, (128, tk), 0) // HEAD_DIM
    kb = k_ref[...]
    vt = vt_ref[...]
    kms = [jnp.where(khalf == j, kb, jnp.zeros_like(kb)) for j in range(2)]
    vms = [jnp.where(vhalf == j, vt, jnp.ones_like(vt)) for j in range(2)]
    units = nsub * N_HEADS

    def scores(u):
        sub, hh = divmod(u, N_HEADS)
        p, j = divmod(hh, 2)
        return jnp.dot(kms[j], qt_refs[sub][p * 128:(p + 1) * 128, :], preferred_element_type=F32).astype(BF16)

    pending = [scores(u) for u in range(FLASH_SCORES_AHEAD)]
    for u in range(units):
        st = pending.pop(0)
        if u + FLASH_SCORES_AHEAD < units:
            pending.append(scores(u + FLASH_SCORES_AHEAD))
        m_prev = m_ref[u]
        m_new = jnp.maximum(m_prev, jnp.max(st, axis=0, keepdims=True).astype(F32))
        alpha = jnp.exp2(m_prev - m_new)
        pt = jnp.exp2(st - m_new[0:1].astype(BF16))
        acc_ref[u] = alpha[0:1] * acc_ref[u] + jnp.dot(vms[u % 2], pt, preferred_element_type=F32)
        m_ref[u] = m_new

    @pl.when(kv == pl.num_programs(1) - 1)
    def _():
        for sub in range(nsub):
            for p in range(N_HEADS // 2):
                ls = slice(p * 128, (p + 1) * 128)
                a0, a1 = acc_ref[sub * N_HEADS + 2 * p], acc_ref[sub * N_HEADS + 2 * p + 1]
                ot = jnp.concatenate([a0[:HEAD_DIM] / a0[HEAD_DIM:], a1[HEAD_DIM:] / a1[:HEAD_DIM]], axis=0)
                o_ref[sub * tq:(sub + 1) * tq, ls] = (jnp.transpose(ot) * sg_refs[sub][:, ls]).astype(BF16)


def flash_gqa(qt, k, vt, sg, *, q_block0, nq, tk, nk, nsub):
    tq = ROW_TILE
    assert nq % nsub == 0
    qt_specs = [pl.BlockSpec((BRANCH_W, tq), lambda i, j, s=s: (0, nsub * i + s + q_block0)) for s in range(nsub)]
    sg_specs = [pl.BlockSpec((tq, BRANCH_W), lambda i, j, s=s: (nsub * i + s + q_block0, 0)) for s in range(nsub)]
    return pl.pallas_call(
        functools.partial(_flash_gqa_body, nsub=nsub),
        out_shape=jax.ShapeDtypeStruct((nq * tq, BRANCH_W), BF16),
        grid=(nq // nsub, nk),
        in_specs=qt_specs + [pl.BlockSpec((tk, GQA_KV_W), lambda i, j: (j, 0)),
                             pl.BlockSpec((GQA_KV_W, tk), lambda i, j: (0, j))] + sg_specs,
        out_specs=pl.BlockSpec((nsub * tq, BRANCH_W), lambda i, j: (i, 0)),
        scratch_shapes=[pltpu.VMEM((nsub * N_HEADS, 8, tq), F32), pltpu.VMEM((nsub * N_HEADS, 128, tq), F32)],
        compiler_params=_cparams(("parallel", "arbitrary")),
        name="flash_gqa",
    )(*([qt] * nsub), k, vt, *([sg] * nsub))


def _out_body(x_ref, ya_ref, yac_ref, yb_ref, ybc_ref, wa_ref, wb_ref, gate_ref, o_ref):
    is_ctx = pl.program_id(0) == 0
    ya = jnp.where(is_ctx, yac_ref[...], ya_ref[...])
    yb = jnp.where(is_ctx, ybc_ref[...], yb_ref[...])
    y = jnp.dot(ya, wa_ref[...], preferred_element_type=F32)
    y += jnp.dot(yb, wb_ref[...], preferred_element_type=F32)
    o_ref[...] = x_ref[...] + gate_ref[0] * y


def out_proj(xs, ya_lat, ya_ctx, yb_lat, yb_ctx, wa, wb, gate2):
    r = xs.shape[0]
    d = D_MODEL
    rows = lambda w: pl.BlockSpec((ROW_TILE, w), lambda i: (i, 0))
    lat = pl.BlockSpec((ROW_TILE, BRANCH_W), lambda i: (jnp.maximum(i - 1, 0), 0))
    const = lambda shp: pl.BlockSpec(shp, lambda i: (0,) * len(shp))
    ctx = const((CTX_LEN, BRANCH_W))
    return pl.pallas_call(
        _out_body,
        out_shape=jax.ShapeDtypeStruct((r, d), F32),
        grid=(r // ROW_TILE,),
        in_specs=[rows(d), lat, ctx, lat, ctx, const((BRANCH_W, d)), const((BRANCH_W, d)),
                  pl.BlockSpec((1, 1, d), lambda i: (jnp.minimum(i, 1), 0, 0))],
        out_specs=rows(d),
        compiler_params=_cparams(("parallel",)),
        name="out_proj",
    )(xs, ya_lat, ya_ctx, yb_lat, yb_ctx, wa.astype(BF16), wb.astype(BF16), gate2)


KV_TILE_MAX_BLOCKS = 13


def _kv_tile(r):
    nb = r // ROW_TILE
    best = max(k for k in range(1, KV_TILE_MAX_BLOCKS + 1) if nb % k == 0)
    return best * ROW_TILE, nb // best


def attn_layer(xs, mods, norm_g, w_in, rpb, q_gain, k_gain, w_out, cos_t, sin_t):
    r = xs.shape[0]
    n = r - CTX_LEN
    shift2, scale2, gate2 = mods
    qa, ka, va, sga, qbt, kb, vbt, sgb = attn_in_proj(xs, norm_g, scale2, shift2, w_in, q_gain, k_gain, cos_t, sin_t)
    bias_tabs = _na_bias_tables(rpb, n // GRID_W)
    ya_lat = na_attention(qa, ka, va, sga, bias_tabs, n)
    ya_ctx = flash_mha(qa, ka, va, sga, q_block0=0, nq=1, tk=CTX_LEN, nk=1)
    tk, nk = _kv_tile(r)
    nq = n // ROW_TILE
    yb_lat = flash_gqa(qbt, kb, vbt, sgb, q_block0=1, nq=nq, tk=tk, nk=nk, nsub=FLASH_Q_TILES if nq % FLASH_Q_TILES == 0 else 1)
    yb_ctx = flash_gqa(qbt, kb, vbt, sgb, q_block0=0, nq=1, tk=CTX_LEN, nk=1, nsub=1)
    wb = _reorder_heads(w_out[BRANCH_W:], GQA_HEAD_ORDER, 0)
    return out_proj(xs, ya_lat, ya_ctx, yb_lat, yb_ctx, w_out[:BRANCH_W], wb, gate2)


def _split(a):
    hi = a.astype(BF16)
    return hi, (a - hi.astype(F32)).astype(BF16)


def _dot3(a, b, dims=(((1,), (0,)), ((), ()))):
    ah, al = _split(a)
    bh, bl = _split(b)
    dg = functools.partial(lax.dot_general, dimension_numbers=dims, preferred_element_type=F32)
    return dg(ah, bh) + (dg(al, bh) + dg(ah, bl))


def _dot1(a, b, dims=(((1,), (0,)), ((), ()))):
    return lax.dot_general(a.astype(BF16), b.astype(BF16), dims, preferred_element_type=F32)


_NT = (((1,), (1,)), ((), ()))
_TN = (((0,), (0,)), ((), ()))


RWKV_SHIFT_W = 1664
HY_IN_W = 1536
HALO = 8
REC_HALO_W = RWKV_SHIFT_W + HY_IN_W


def _rec_in_body(x_ref, xp_ref, xn_ref, g_ref, scale_ref, shift_ref, w_ref, mu_ref, taps_ref,
                 rw_ref, hv_ref, hx1_ref, hx2_ref, sgr_ref, sgh_ref, u_scr):
    i = pl.program_id(0)
    nt = pl.num_programs(0)
    xe = jnp.concatenate([xp_ref[...], x_ref[...], xn_ref[...]], axis=0)
    y = xe * lax.rsqrt(jnp.mean(xe * xe, axis=-1, keepdims=True) + NORM_EPS)
    xm = ((y * g_ref[...]) * (1.0 + scale_ref[0]) + shift_ref[0]).astype(BF16)
    u = jnp.dot(xm, w_ref[...], preferred_element_type=F32)
    row = lax.broadcasted_iota(jnp.int32, (ROW_TILE + 2 * HALO, 1), 0)
    keep = jnp.logical_and(jnp.logical_or(row >= HALO, i >= 2),
                           jnp.logical_or(row < ROW_TILE + HALO, jnp.logical_and(i >= 1, i < nt - 1)))
    u_scr[...] = jnp.where(keep, u[:, :REC_HALO_W], 0.0)
    up = u_scr[pl.ds(HALO - 1, ROW_TILE), :]
    uc = u_scr[pl.ds(HALO, ROW_TILE), :]
    un = u_scr[pl.ds(HALO + 1, ROW_TILE), :]
    w = RWKV_SHIFT_W
    rw_c = uc[:, :w]
    rw_ref[...] = rw_c + (0.5 * (up[:, :w] + un[:, :w]) - rw_c) * mu_ref[...]
    hy = up[:, w:] * taps_ref[0:1] + uc[:, w:] * taps_ref[1:2] + un[:, w:] * taps_ref[2:3]
    hv_ref[...] = hy[:, 0:512].astype(hv_ref.dtype)
    hx1_ref[...] = hy[:, 512:1024].astype(hx1_ref.dtype)
    hx2_ref[...] = hy[:, 1024:1536].astype(hx2_ref.dtype)
    uc_all = u[HALO:HALO + ROW_TILE]
    sgr_ref[...] = _silu(uc_all[:, REC_HALO_W:REC_HALO_W + 512]).astype(sgr_ref.dtype)
    sgh_ref[...] = _silu(uc_all[:, REC_HALO_W + 512:REC_HALO_W + 1024]).astype(sgh_ref.dtype)


def rec_in_proj(xs, norm_g, scale2, shift2, w_in, mu, hy_short):
    r = xs.shape[0]
    d = D_MODEL
    w = RWKV_SHIFT_W
    w_ext = jnp.concatenate([w_in[:, :w], w_in[:, w + 512:w + 512 + HY_IN_W], w_in[:, w:w + 512],
                             w_in[:, w + 512 + HY_IN_W:]], axis=1).astype(BF16)
    nh = r // HALO
    per = ROW_TILE // HALO
    const = lambda shp: pl.BlockSpec(shp, lambda i: (0,) * len(shp))
    rows = lambda wd: pl.BlockSpec((ROW_TILE, wd), lambda i: (i, 0))
    f = lambda wd: jax.ShapeDtypeStruct((r, wd), F32)
    h = jax.ShapeDtypeStruct((r, 512), BF16)
    mod = _mod_specs()
    return pl.pallas_call(
        _rec_in_body,
        out_shape=[f(w), h, h, h, h, h],
        grid=(r // ROW_TILE,),
        in_specs=[mod[0],
                  pl.BlockSpec((HALO, d), lambda i: (jnp.maximum(i * per - 1, 0), 0)),
                  pl.BlockSpec((HALO, d), lambda i: (jnp.minimum((i + 1) * per, nh - 1), 0)),
                  mod[1], mod[2], mod[3], const((d, w_ext.shape[1])), const((1, w)), const((3, HY_IN_W))],
        out_specs=[rows(w), rows(512), rows(512), rows(512), rows(512), rows(512)],
        scratch_shapes=[pltpu.VMEM((ROW_TILE + 2 * HALO, REC_HALO_W), F32)],
        compiler_params=_cparams(("parallel",)),
        name="rec_in_proj",
    )(xs, xs, xs, norm_g[None], scale2, shift2, w_ext, mu[None], hy_short)


CHUNK = 64
CPT = ROW_TILE // CHUNK


def _block_sum_mat(width, value):
    return jnp.asarray(np.kron(np.eye(width // HEAD_DIM), np.full((HEAD_DIM, HEAD_DIM), value)), F32)


PREP_PAIRS = 2


def _rwkv_prep_body(r_ref, k_ref, v_ref, lora_ref, w0_ref, wup_ref, a0_ref, aup_ref, kk_ref, ka_ref, rk_ref,
                    tri_ref, bs_ref, g_ref, add_ref, bonus_ref):
    t = ROW_TILE
    lora = lora_ref[...]
    bs = bs_ref[...]
    row = lax.broadcasted_iota(jnp.int32, (t, t), 0)
    col = lax.broadcasted_iota(jnp.int32, (t, t), 1)
    same = (row // CHUNK) == (col // CHUNK)
    eye = (row == col).astype(F32)
    wl_all = _dot3(jnp.tanh(lora), wup_ref[...])
    al_all = _dot3(lora, aup_ref[...])
    half = _lane_half((t, 128))
    half_c = _lane_half((HEAD_DIM, 128))
    rowc = lax.broadcasted_iota(jnp.int32, (HEAD_DIM, 128), 0)
    lanec = lax.broadcasted_iota(jnp.int32, (HEAD_DIM, 128), 1)
    level_masks = []
    bsz = 2
    while bsz < CHUNK:
        level_masks.append(jnp.logical_and((row // (2 * bsz)) == (col // (2 * bsz)), (row // bsz) != (col // bsz)))
        bsz *= 2
    first_mask = (row // 2) == (col // 2)

    groups = []
    for q in range(PREP_PAIRS):
        ls = slice(q * 128, (q + 1) * 128)
        r, k, v = r_ref[:, ls], k_ref[:, ls], v_ref[:, ls]
        kk = k * kk_ref[:, ls]
        kk = kk * lax.rsqrt(_dot_stat(kk * kk, bs) + 1e-12)
        kd_sum = None
        for d in range(2):
            ds = slice(q * 256 + d * 128, q * 256 + (d + 1) * 128)
            wl = w0_ref[d, :, ls] + wl_all[:, ds]
            z = -wl
            w_log = -(jnp.maximum(z, 0.0) + jnp.log(1.0 + jnp.exp(-jnp.abs(z)))) - 0.5
            lw = -jnp.exp(w_log)
            a = 1.0 / (1.0 + jnp.exp(-(a0_ref[d, :, ls] + al_all[:, ds])))
            kd = k * (1.0 + (a - 1.0) * ka_ref[:, ls])
            b = kk * a
            kd_sum = kd if kd_sum is None else kd_sum + kd
            incl = jnp.logical_and(same, (col <= row) if d == 0 else (col >= row))
            lw_hi, lw_lo = _split(lw)
            tri = tri_ref[d]
            cs = jnp.dot(tri, lw_hi, preferred_element_type=F32) + jnp.dot(tri, lw_lo, preferred_element_type=F32)
            ends = [c * CHUNK + (CHUNK - 1 if d == 0 else 0) for c in range(CPT)]
            tot = jnp.concatenate([jnp.broadcast_to(cs[e:e + 1], (CHUNK, 128)) for e in ends], axis=0)
            w_inv = jnp.exp(-cs)
            w_rest = jnp.exp(tot - cs)
            groups.append(dict(q=q, d=d, v=v, incl=incl, strict=jnp.logical_and(incl, row != col), tot=tot,
                               kkt=kk * jnp.exp(cs - lw), kh=kd * w_inv, bh=b * w_inv, rt=r * jnp.exp(cs),
                               kdd=kd * w_rest, bdd=b * w_rest))
        bonus_ref[:, ls] = 0.5 * _dot_stat(r * kd_sum * rk_ref[:, ls], bs) * v
    chains = [(gg, j) for gg in groups for j in range(2)]
    sels = [half == j for _, j in chains]
    bms = [jnp.where(sel, gg["bh"], 0.0) for (gg, _), sel in zip(chains, sels)]
    kms = [jnp.where(sel, gg["kh"], 0.0) for (gg, _), sel in zip(chains, sels)]
    l_bs = [jnp.where(gg["strict"], _dot1(gg["kkt"], bm, _NT), 0.0) for (gg, _), bm in zip(chains, bms)]
    tinvs = [eye - jnp.where(first_mask, l_b, 0.0) for l_b in l_bs]
    for mask in level_masks:
        xs = [_dot1(jnp.where(mask, l_b, 0.0), tinv) for l_b, tinv in zip(l_bs, tinvs)]
        tinvs = [tinv - _dot1(tinv, x) for tinv, x in zip(tinvs, xs)]
    l_ks = [jnp.where(gg["strict"], _dot1(gg["kkt"], km, _NT), 0.0) for (gg, _), km in zip(chains, kms)]
    a_rks = [jnp.where(gg["incl"], _dot1(gg["rt"], km, _NT), 0.0) for (gg, _), km in zip(chains, kms)]
    a_rbs = [jnp.where(gg["incl"], _dot1(gg["rt"], bm, _NT), 0.0) for (gg, _), bm in zip(chains, bms)]
    lvs = [_dot1(l_k, gg["v"]) for (gg, _), l_k in zip(chains, l_ks)]
    pus = [_dot1(tinv, jnp.concatenate([gg["kkt"], lv], axis=1)) for (gg, _), tinv, lv in zip(chains, tinvs, lvs)]
    cors = [_dot1(a_rb, pu) for a_rb, pu in zip(a_rbs, pus)]
    ps = [pu[:, :128] for pu in pus]
    u0s = [pu[:, 128:] for pu in pus]
    qs = [gg["rt"] - cor[:, :128] for (gg, _), cor in zip(chains, cors)]
    y0s = [_dot1(a_rk, gg["v"]) - cor[:, 128:] for (gg, _), a_rk, cor in zip(chains, a_rks, cors)]

    sel0 = half == 0
    for n, gg in enumerate(groups):
        d, v = gg["d"], gg["v"]
        ls = slice(gg["q"] * 128, (gg["q"] + 1) * 128)
        p, u0, q, y0 = (jnp.where(sel0, x[2 * n], x[2 * n + 1]) for x in (ps, u0s, qs, y0s))
        for c in range(CPT):
            rs = slice(c * CHUNK, (c + 1) * CHUNK)
            x1 = _dot1(gg["bdd"][rs], p[rs], _TN)
            x2 = _dot1(gg["kdd"][rs], v[rs], _TN) - _dot1(gg["bdd"][rs], u0[rs], _TN)
            m_pair = jnp.where(half_c == 0, x1[:HEAD_DIM], x1[HEAD_DIM:])
            n_pair = jnp.where(half_c == 0, x2[:HEAD_DIM], x2[HEAD_DIM:])
            wc = jnp.exp(gg["tot"][c * CHUNK:c * CHUNK + 1])
            dg = jnp.where((lanec % HEAD_DIM) == rowc, wc, 0.0)
            g_ref[c, d, 0:HEAD_DIM, ls] = dg - m_pair
            g_ref[c, d, HEAD_DIM:, ls] = q[rs]
            add_ref[c, d, 0:HEAD_DIM, ls] = n_pair
            add_ref[c, d, HEAD_DIM:, ls] = y0[rs]


def _lora_ext(up, first_row):
    out = jnp.zeros((2, 128, BRANCH_W), F32)
    for d in range(2):
        out = out.at[d, first_row + 32 * d:first_row + 32 * (d + 1)].set(up[d])
    return out


def rwkv_prep(rw, w0, w_up, a0, a_up, k_k, k_a, r_k):
    r = rw.shape[0]
    nt = r // ROW_TILE
    nch = r // CHUNK
    t = ROW_TILE
    ii = np.arange(t)
    same = (ii[:, None] // CHUNK) == (ii[None, :] // CHUNK)
    tri = jnp.asarray(np.stack([same & (ii[None, :] <= ii[:, None]), same & (ii[None, :] >= ii[:, None])]), BF16)
    pair_cat = lambda w: w.reshape(2, 128, N_HEADS // 2, 128).transpose(1, 2, 0, 3).reshape(128, 2 * BRANCH_W)
    w = 128 * PREP_PAIRS
    steps = N_HEADS // 2 // PREP_PAIRS
    lane = lambda blk: pl.BlockSpec((t, w), lambda i, p, blk=blk: (i, blk + p))
    pvec = pl.BlockSpec((1, w), lambda i, p: (0, p))
    dvec = pl.BlockSpec((2, 1, w), lambda i, p: (0, 0, p))
    dmat = pl.BlockSpec((128, 2 * w), lambda i, p: (0, p))
    gspec = pl.BlockSpec((CPT, 2, HEAD_DIM + CHUNK, w), lambda i, p: (i, 0, 0, p))
    gshape = jax.ShapeDtypeStruct((nch, 2, HEAD_DIM + CHUNK, BRANCH_W), F32)
    return pl.pallas_call(
        _rwkv_prep_body,
        out_shape=[gshape, gshape, jax.ShapeDtypeStruct((r, BRANCH_W), F32)],
        grid=(nt, steps),
        in_specs=[lane(0), lane(steps), lane(2 * steps), pl.BlockSpec((t, 128), lambda i, p: (i, 12)),
                  dvec, dmat, dvec, dmat, pvec, pvec, pvec,
                  pl.BlockSpec((2, t, t), lambda i, p: (0, 0, 0)),
                  pl.BlockSpec((128, 128), lambda i, p: (0, 0))],
        out_specs=[gspec, gspec, pl.BlockSpec((t, w), lambda i, p: (i, p))],
        compiler_params=_cparams(("parallel", "parallel")),
        name="rwkv_prep",
    )(rw, rw, rw, rw, w0.reshape(2, 1, BRANCH_W), pair_cat(_lora_ext(w_up, 0)), a0.reshape(2, 1, BRANCH_W),
      pair_cat(_lora_ext(a_up, 64)), k_k[None], k_a[None], r_k.reshape(1, BRANCH_W), tri,
      _block_sum_mat(128, 1.0).astype(BF16))


def _rwkv_scan_body(gf_ref, af_ref, gb_ref, ab_ref, yf_ref, yb_ref, st_ref):
    @pl.when(pl.program_id(0) == 0)
    def _():
        st_ref[...] = jnp.zeros(st_ref.shape, F32)

    rowh = lax.broadcasted_iota(jnp.int32, (128, 128), 0) // HEAD_DIM
    diag = rowh == _lane_half((128, 128))
    for s in range(CPT):
        for d, (g_ref, a_ref, y_ref, c) in enumerate(((gf_ref, af_ref, yf_ref, s), (gb_ref, ab_ref, yb_ref, CPT - 1 - s))):
            for p in range(N_HEADS // 2):
                ls = slice(p * 128, (p + 1) * 128)
                out = _dot3(g_ref[c, 0, :, ls], st_ref[d, p]) + a_ref[c, 0, :, ls]
                hn = out[:HEAD_DIM]
                st_ref[d, p] = jnp.where(diag, jnp.concatenate([hn, hn], axis=0), 0.0)
                y_ref[c * CHUNK:(c + 1) * CHUNK, ls] = out[HEAD_DIM:]


def rwkv_scan(g, add):
    nch = g.shape[0]
    r = nch * CHUNK
    nt = r // ROW_TILE
    assert CTX_LEN == ROW_TILE
    rev = lambda i: jnp.where(i == 0, 0, nt - i)
    blk = (CPT, 1, HEAD_DIM + CHUNK, BRANCH_W)
    fwd = pl.BlockSpec(blk, lambda i: (i, 0, 0, 0))
    bwd = pl.BlockSpec(blk, lambda i: (rev(i), 1, 0, 0))
    yshape = jax.ShapeDtypeStruct((r, BRANCH_W), F32)
    return pl.pallas_call(
        _rwkv_scan_body,
        out_shape=[yshape, yshape],
        grid=(nt,),
        in_specs=[fwd, fwd, bwd, bwd],
        out_specs=[pl.BlockSpec((ROW_TILE, BRANCH_W), lambda i: (i, 0)),
                   pl.BlockSpec((ROW_TILE, BRANCH_W), lambda i: (rev(i), 0))],
        scratch_shapes=[pltpu.VMEM((2, N_HEADS // 2, 128, 128), F32)],
        compiler_params=_cparams(("arbitrary",)),
        name="rwkv_scan",
    )(g, add, g, add)


HY_WIDTH = 512
HY_ORDER = 2
HY_POS_BANDS = 16
HY_HIDDEN = 64
HY_TAPS_W = 2 * HY_ORDER * HY_WIDTH
FFT_N2 = ROW_TILE


def _dot3c(ah, al, b):
    bh, bl = _split(b)
    dg = functools.partial(jnp.dot, preferred_element_type=F32)
    return dg(ah, bh) + (dg(al, bh) + dg(ah, bl))


def _dotc(ah, b):
    return jnp.dot(ah, b.astype(BF16), preferred_element_type=F32)


def _split_const(m):
    m = np.asarray(m, np.float32)
    hi = m.astype(BF16)
    lo = (m - hi.astype(np.float32)).astype(BF16)
    return jnp.asarray(hi), jnp.asarray(lo)


TAPS_FLAT_COLS = 8


def _filter_taps(t_idx, length, c2pb_ref, w1t_ref, w1c_ref, w1s_ref, b1_ref, w2_ref, b2_ref, w3_ref, b3_ref, absd_ref):
    t = t_idx / float(max(length - 1, 1))
    ang = c2pb_ref[...] * t_idx / float(length)
    pre = t * w1t_ref[...] + _dot3(jnp.cos(ang), w1c_ref[...]) - _dot3(jnp.sin(ang), w1s_ref[...]) + b1_ref[...]
    hid = jnp.sin(pre)
    hid = jnp.sin(_dot3(hid, w2_ref[...]) + b2_ref[...])
    return (_dot3(hid, w3_ref[...]) + b3_ref[...]) * jnp.exp(-t * absd_ref[...])


def _hy_taps_body(*refs, length):
    taps_ref, ssq_ref, tap0_ref = refs[-3:]
    i = pl.program_id(0)
    t_idx = (i * ROW_TILE + lax.broadcasted_iota(jnp.int32, (ROW_TILE, 1), 0)).astype(F32)
    taps = _filter_taps(t_idx, length, *refs[:-3])
    taps_ref[...] = taps

    @pl.when(i == 0)
    def _():
        ssq_ref[...] = jnp.zeros(ssq_ref.shape, F32)
        tap0_ref[...] = taps[0:1]

    ssq_ref[...] += jnp.sum(taps * taps, axis=0, keepdims=True)


def _hy_taps_flat_body(c2pb_ref, w1t_ref, w1c_ref, w1s_ref, b1_ref, w2_ref, b2_ref, w3_ref, b3_ref, absd_ref, f1_ref,
                       planes_ref, ssq_ref, tap0_ref, *, length):
    j = pl.program_id(0)
    rows = length // FFT_N2
    nb = TAPS_FLAT_COLS

    def positions(width, per):
        a = lax.broadcasted_iota(jnp.int32, (rows, width), 0)
        b = lax.broadcasted_iota(jnp.int32, (rows, width), 1) // per
        return (a * FFT_N2 + j * nb + b).astype(F32)

    @pl.when(j == 0)
    def _():
        ssq_ref[...] = jnp.zeros(ssq_ref.shape, F32)

    h = HY_HIDDEN
    ang = c2pb_ref[...] * positions(nb * HY_POS_BANDS, HY_POS_BANDS) / float(length)
    t_h = positions(nb * h, h) / float(max(length - 1, 1))
    pre = t_h * w1t_ref[...] + _dot3(jnp.cos(ang), w1c_ref[...]) - _dot3(jnp.sin(ang), w1s_ref[...]) + b1_ref[...]
    hid = jnp.sin(pre)
    hid = jnp.sin(_dot3(hid, w2_ref[...]) + b2_ref[...])
    for bp in range(nb // 2):
        t_w = (positions(2 * HY_TAPS_W, HY_TAPS_W) + float(2 * bp)) / float(max(length - 1, 1))
        taps = (_dot3(hid[:, bp * 2 * h:(bp + 1) * 2 * h], w3_ref[...]) + b3_ref[...]) * jnp.exp(-t_w * absd_ref[...])
        planes = _dotc(f1_ref[...], taps)
        planes_ref[:, bp * 2 * HY_TAPS_W:(bp + 1) * 2 * HY_TAPS_W] = planes.astype(planes_ref.dtype)
        if bp == 0:
            @pl.when(j == 0)
            def _():
                tap0_ref[...] = taps[0:1, :HY_TAPS_W]
        sq = jnp.sum(taps * taps, axis=0, keepdims=True)
        ssq_ref[...] += sq[:, :HY_TAPS_W] + sq[:, HY_TAPS_W:]


def hyena_taps(length, w1, b1, w2, b2, w3, b3, *, flat):
    bands = jnp.linspace(1e-4, HY_POS_BANDS - 1, HY_POS_BANDS, dtype=F32)
    c2pb = jnp.zeros((1, 128), F32).at[0, :HY_POS_BANDS].set(2.0 * math.pi * bands)
    pad = lambda m: jnp.zeros((128, HY_HIDDEN), F32).at[:HY_POS_BANDS].set(m)
    deltas = jnp.linspace(math.log(1e-2) / 0.3, math.log(1e-2) / 1.5, HY_WIDTH, dtype=F32)
    absd = jnp.tile(jnp.abs(deltas), 2 * HY_ORDER)[None]
    const = lambda shp: pl.BlockSpec(shp, lambda i: (0,) * len(shp))
    w1t, w1c, w1s = w1[0:1], w1[1:1 + HY_POS_BANDS], w1[1 + HY_POS_BANDS:]
    if flat:
        nb = TAPS_FLAT_COLS
        f1 = _FftPlan(length).f1[0]
        rows = f1.shape[0]
        bd = lambda m, k: jnp.kron(jnp.eye(k, dtype=F32), m)
        tile = lambda v, k: jnp.tile(v, k)[None]
        operands = (tile(c2pb[0, :HY_POS_BANDS], nb), tile(w1t[0], nb), bd(w1c, nb), bd(w1s, nb), tile(b1, nb),
                    bd(w2, nb), tile(b2, nb), bd(w3, 2), tile(b3, 2), tile(absd[0], 2), f1)
        body, grid = _hy_taps_flat_body, (FFT_N2 // nb,)
        taps_shape, taps_dtype = (rows, FFT_N2 * HY_TAPS_W), BF16
        taps_spec = pl.BlockSpec((rows, nb * HY_TAPS_W), lambda i: (0, i))
    else:
        operands = (c2pb, w1t, pad(w1c), pad(w1s), b1[None], w2, b2[None], w3, b3[None], absd)
        body, grid = _hy_taps_body, (length // ROW_TILE,)
        taps_shape, taps_dtype = (length, HY_TAPS_W), F32
        taps_spec = pl.BlockSpec((ROW_TILE, HY_TAPS_W), lambda i: (i, 0))
    taps, ssq, tap0 = pl.pallas_call(
        functools.partial(body, length=length),
        out_shape=[jax.ShapeDtypeStruct(taps_shape, taps_dtype), jax.ShapeDtypeStruct((1, HY_TAPS_W), F32),
                   jax.ShapeDtypeStruct((1, HY_TAPS_W), F32)],
        grid=grid,
        in_specs=[const(op.shape) for op in operands],
        out_specs=[taps_spec, const((1, HY_TAPS_W)), const((1, HY_TAPS_W))],
        compiler_params=_cparams(("arbitrary",)),
        name="hyena_taps_flat" if flat else "hyena_taps",
    )(*operands)
    hw = HY_TAPS_W // 2
    norm2 = ssq[:, :hw] + ssq[:, hw:] + 2.0 * tap0[:, :hw] * tap0[:, hw:]
    return taps, lax.rsqrt(norm2)


class _FftPlan:
    def __init__(self, length):
        self.length = length
        self.n = 2 * length
        self.n2 = FFT_N2
        self.n1 = self.n // self.n2
        self.n1h = self.n1 // 2
        k1 = self.n1h + 1
        self.k1p = -(-k1 // 8) * 8
        kk = np.arange(self.k1p)[:, None].astype(np.float64)
        live = (kk < k1)
        nn = np.arange(self.n1h)[None, :].astype(np.float64)
        th = 2.0 * np.pi * kk * nn / self.n1
        self.f1 = _split_const(np.concatenate([np.cos(th) * live, -np.sin(th) * live], axis=0))
        ck = np.where((kk == 0) | (kk == self.n1h), 1.0, 2.0) * live / self.n
        self.g1 = _split_const(np.concatenate([np.cos(th) * ck, -np.sin(th) * ck], axis=0).T)
        m = np.arange(self.n2).astype(np.float64)
        ph = 2.0 * np.pi * np.outer(m, m) / self.n2
        c, s = np.cos(ph), np.sin(ph)
        self.fb = _split_const(np.block([[c, s], [-s, c]]))
        self.fbi = _split_const(np.block([[c, -s], [s, c]]))
        tw = 2.0 * np.pi * kk[:, :, None] * m[None, :, None] / self.n
        self.twc = jnp.asarray(np.cos(tw), F32)
        self.tws = jnp.asarray(np.sin(tw), F32)


FFT_TN = 4096


def _fft_a_body(fh_ref, x_ref, o_ref):
    o_ref[...] = _dotc(fh_ref[...], x_ref[...]).astype(o_ref.dtype)


def fft_stage_a(plan, xf, lead):
    rows_in, m = xf.shape
    rows = 2 * plan.k1p
    tn = min(FFT_TN, m)
    fh = jnp.pad(plan.f1[0], ((0, 0), (lead, 0)))
    fspec = pl.BlockSpec((rows, rows_in), lambda j: (0, 0))
    return pl.pallas_call(
        _fft_a_body,
        out_shape=jax.ShapeDtypeStruct((rows, m), BF16),
        grid=(m // tn,),
        in_specs=[fspec, pl.BlockSpec((rows_in, tn), lambda j: (0, j))],
        out_specs=pl.BlockSpec((rows, tn), lambda j: (0, j)),
        compiler_params=_cparams(("parallel",)),
        name="fft_stage_a",
    )(fh, xf)


FFT_PLANES = 2


def _twiddled(a_ref, twc_ref, tws_ref, p):
    are, aim = a_ref[0, p].astype(F32), a_ref[1, p].astype(F32)
    c, s = twc_ref[p], tws_ref[p]
    return jnp.concatenate([are * c + aim * s, aim * c - are * s], axis=0)


def _fft_filter_b_body(a_ref, twc_ref, tws_ref, fbh_ref, scale_ref, o_ref):
    n2 = FFT_N2
    hw = HY_TAPS_W // 2
    xs = [_dotc(fbh_ref[...], _twiddled(a_ref, twc_ref, tws_ref, p)) for p in range(FFT_PLANES)]
    for p, x in enumerate(xs):
        xre, xim = x[:n2], x[n2:]
        o_ref[0, p] = ((xre[:, :hw] + xre[:, hw:]) * scale_ref[...]).astype(o_ref.dtype)
        o_ref[1, p] = ((xim[:, :hw] - xim[:, hw:]) * scale_ref[...]).astype(o_ref.dtype)


def fft_filter_stage_b(plan, a, scale):
    n2, k1p = plan.n2, plan.k1p
    hw = HY_TAPS_W // 2
    const = lambda shp: pl.BlockSpec(shp, lambda k: (0,) * len(shp))
    return pl.pallas_call(
        _fft_filter_b_body,
        out_shape=jax.ShapeDtypeStruct((2, k1p, n2, hw), BF16),
        grid=(k1p // FFT_PLANES,),
        in_specs=[pl.BlockSpec((2, FFT_PLANES, n2, HY_TAPS_W), lambda k: (0, k, 0, 0)),
                  pl.BlockSpec((FFT_PLANES, n2, 1), lambda k: (k, 0, 0)),
                  pl.BlockSpec((FFT_PLANES, n2, 1), lambda k: (k, 0, 0)),
                  const((2 * n2, 2 * n2)), const((1, hw))],
        out_specs=pl.BlockSpec((2, FFT_PLANES, n2, hw), lambda k: (0, k, 0, 0)),
        compiler_params=_cparams(("parallel",)),
        name="fft_filter_stage_b",
    )(a, plan.twc, plan.tws, plan.fb[0], scale)


def _fft_conv_b_body(a_ref, kf_ref, twc_ref, tws_ref, fbh_ref, fih_ref, o_ref):
    n2 = FFT_N2
    planes = range(FFT_PLANES)
    zs = [_dotc(fbh_ref[...], _twiddled(a_ref, twc_ref, tws_ref, p)) for p in planes]
    ys = []
    for p, z in zip(planes, zs):
        zre, zim = z[:n2], z[n2:]
        kre, kim = kf_ref[0, p].astype(F32), kf_ref[1, p].astype(F32)
        ys.append(jnp.concatenate([zre * kre - zim * kim, zre * kim + zim * kre], axis=0))
    qs = [_dotc(fih_ref[...], y) for y in ys]
    for p, q in zip(planes, qs):
        qre, qim = q[:n2], q[n2:]
        c, s = twc_ref[p], tws_ref[p]
        o_ref[0, p] = (qre * c - qim * s).astype(o_ref.dtype)
        o_ref[1, p] = (qim * c + qre * s).astype(o_ref.dtype)


def fft_conv_stage_b(plan, a, kf, order):
    n2, k1p = plan.n2, plan.k1p
    const = lambda shp: pl.BlockSpec(shp, lambda k: (0,) * len(shp))
    return pl.pallas_call(
        _fft_conv_b_body,
        out_shape=jax.ShapeDtypeStruct((2, k1p, n2, HY_WIDTH), BF16),
        grid=(k1p // FFT_PLANES,),
        in_specs=[pl.BlockSpec((2, FFT_PLANES, n2, HY_WIDTH), lambda k: (0, k, 0, 0)),
                  pl.BlockSpec((2, FFT_PLANES, n2, HY_WIDTH), lambda k: (0, k, 0, order)),
                  pl.BlockSpec((FFT_PLANES, n2, 1), lambda k: (k, 0, 0)),
                  pl.BlockSpec((FFT_PLANES, n2, 1), lambda k: (k, 0, 0)),
                  const((2 * n2, 2 * n2)), const((2 * n2, 2 * n2))],
        out_specs=pl.BlockSpec((2, FFT_PLANES, n2, HY_WIDTH), lambda k: (0, k, 0, 0)),
        compiler_params=_cparams(("parallel",)),
        name="fft_conv_stage_b",
    )(a, kf, plan.twc, plan.tws, plan.fb[0], plan.fbi[0])


def _fft_inv_a_body(gh_ref, fh_ref, q_ref, z_ref, gate_ref, skip_ref, o_ref, *next_ref):
    y = _dotc(gh_ref[...], q_ref[...])
    z_next = gate_ref[...] * (y + z_ref[...] * skip_ref[...])
    o_ref[...] = z_next.astype(o_ref.dtype)
    if next_ref:
        next_ref[0][...] = _dotc(fh_ref[...], z_next).astype(next_ref[0].dtype)


def fft_inv_stage_a(plan, qf, zf, gatef, skip_t, lead, with_next):
    rows_out, m = zf.shape
    rows = 2 * plan.k1p
    tn = skip_t.shape[1]
    gh = jnp.pad(plan.g1[0], ((lead, 0), (0, 0)))
    fh = jnp.pad(plan.f1[0], ((0, 0), (lead, 0)))
    col = lambda r_: pl.BlockSpec((r_, tn), lambda j: (0, j))
    out_shape = [jax.ShapeDtypeStruct((rows_out, m), zf.dtype)]
    out_specs = [col(rows_out)]
    if with_next:
        out_shape.append(jax.ShapeDtypeStruct((rows, m), BF16))
        out_specs.append(col(rows))
    return pl.pallas_call(
        _fft_inv_a_body,
        out_shape=out_shape,
        grid=(m // tn,),
        in_specs=[pl.BlockSpec((rows_out, rows), lambda j: (0, 0)), pl.BlockSpec((rows, rows_out), lambda j: (0, 0)),
                  col(rows), col(rows_out), col(rows_out), pl.BlockSpec((1, tn), lambda j: (0, 0))],
        out_specs=out_specs,
        compiler_params=_cparams(("parallel",)),
        name="fft_inv_stage_a",
    )(gh, fh, qf, zf, gatef, skip_t)


def hyena_long(hv, hx1, hx2, ta, scale, skip, lead):
    length = hv.shape[0] - lead * FFT_N2
    plan = _FftPlan(length)
    n2, k1p = plan.n2, plan.k1p
    m = n2 * HY_WIDTH
    tn = min(FFT_TN, m)
    flat = lambda a: a.reshape(lead + plan.n1h, m)
    kf = fft_filter_stage_b(plan, ta.reshape(2, k1p, n2, HY_TAPS_W), scale)
    z = flat(hv)
    a = fft_stage_a(plan, z, lead)
    gates = (hx1, hx2)
    for o, gate in enumerate(gates):
        q = fft_conv_stage_b(plan, a.reshape(2, k1p, n2, HY_WIDTH), kf, o)
        last = o == len(gates) - 1
        out = fft_inv_stage_a(plan, q.reshape(2 * k1p, m), z, flat(gate), jnp.tile(skip[o], tn // HY_WIDTH)[None], lead,
                              not last)
        z = out[0]
        if not last:
            a = out[1]
    return z.reshape(-1, HY_WIDTH)


def _rec_out_body(x_ref, yf_ref, yb_ref, bonus_ref, sgr_ref, zh_ref, zc_ref, sgh_ref, gnw_ref, gnb_ref, bm_ref, wa_ref, wb_ref,
                  gate_ref, fin_ref, o_ref, *, final):
    y = yf_ref[...] + yb_ref[...]
    bm = bm_ref[...]
    mean = _dot_stat(y, bm)
    yc = y - mean
    var = _dot_stat(yc * yc, bm)
    yn = yc * lax.rsqrt(var + RWKV_GN_EPS) * gnw_ref[...] + gnb_ref[...]
    ya = ((yn + bonus_ref[...]) * sgr_ref[...]).astype(BF16)
    zh = zh_ref[...]
    if not final:
        zh = jnp.where(pl.program_id(0) == 0, zc_ref[...], zh)
    yh = (zh * sgh_ref[...]).astype(BF16)
    out = jnp.dot(ya, wa_ref[...], preferred_element_type=F32) + jnp.dot(yh, wb_ref[...], preferred_element_type=F32)
    xn = x_ref[...] + gate_ref[0] * out
    if final:
        xn = xn * lax.rsqrt(jnp.mean(xn * xn, axis=-1, keepdims=True) + NORM_EPS) * fin_ref[...]
    o_ref[...] = xn


RWKV_GN_EPS = 64e-5


def rec_out_proj(xs, yf, yb, bonus, sgr, zh, zc, sgh, gn_w, gn_b, w_out, gate2, final_g, *, final):
    r = xs.shape[0]
    d = D_MODEL
    off = 1 if final else 0
    nt = r // ROW_TILE - off
    rows = lambda w: pl.BlockSpec((ROW_TILE, w), lambda i: (i + off, 0))
    const = lambda shp: pl.BlockSpec(shp, lambda i: (0,) * len(shp))
    bw = BRANCH_W
    return pl.pallas_call(
        functools.partial(_rec_out_body, final=final),
        out_shape=jax.ShapeDtypeStruct((nt * ROW_TILE, d), F32),
        grid=(nt,),
        in_specs=[rows(d), rows(bw), rows(bw), rows(bw), rows(bw), rows(bw), const((CTX_LEN, bw)), rows(bw),
                  const((1, bw)), const((1, bw)),
                  const((bw, bw)), const((bw, d)), const((bw, d)),
                  pl.BlockSpec((1, 1, d), lambda i: (jnp.minimum(i + off, 1), 0, 0)), const((1, d))],
        out_specs=pl.BlockSpec((ROW_TILE, d), lambda i: (i, 0)),
        compiler_params=_cparams(("parallel",)),
        name="rec_out_proj",
    )(xs, yf, yb, bonus, sgr, zh, zc, sgh, gn_w[None], gn_b[None], _block_sum_mat(bw, 1.0 / HEAD_DIM).astype(BF16),
      w_out[:bw].astype(BF16), w_out[bw:].astype(BF16), gate2, final_g[None])


def rec_layer(xs, mods, norm_g, w_in, mu, w0, w_up, a0, a_up, k_k, k_a, r_k, gn_w, gn_b, hy_short, hy_w1, hy_b1, hy_w2,
              hy_b2, hy_w3, hy_b3, hy_skip, w_out, final_g, final):
    shift2, scale2, gate2 = mods
    rw, hv, hx1, hx2, sgr, sgh = rec_in_proj(xs, norm_g, scale2, shift2, w_in, mu, hy_short)
    g, add, bonus = rwkv_prep(rw, w0, w_up, a0, a_up, k_k, k_a, r_k)
    yf, yb = rwkv_scan(g, add)
    fargs = (hy_w1, hy_b1, hy_w2, hy_b2, hy_w3, hy_b3)
    n = xs.shape[0] - CTX_LEN
    taps, scale = hyena_taps(n, *fargs, flat=True)
    zh = hyena_long(hv, hx1, hx2, taps, scale, hy_skip, CTX_LEN // FFT_N2)
    if final:
        z_ctx = hv[:CTX_LEN]
    else:
        taps_c, scale_c = hyena_taps(CTX_LEN, *fargs, flat=False)
        z_ctx = hyena_short(hv[:CTX_LEN], hx1[:CTX_LEN], hx2[:CTX_LEN], taps_c, scale_c, hy_skip)
    return rec_out_proj(xs, yf, yb, bonus, sgr, zh, z_ctx, sgh, gn_w, gn_b, w_out, gate2, final_g, final=final)


def kernel(x, c, ctx, c_ctx, attn_norm, attn_ada_w, attn_ada_b, attn_w_in, na_rpb, gqa_q_gain, gqa_k_gain, attn_w_out,
           rec_norm, rec_ada_w, rec_ada_b, rec_w_in, rwkv_mu, rwkv_w0, rwkv_w_up, rwkv_a0, rwkv_a_up, rwkv_k_k, rwkv_k_a,
           rwkv_r_k, rwkv_gn_w, rwkv_gn_b, hy_short, hy_w1, hy_b1, hy_w2, hy_b2, hy_w3, hy_b3, hy_skip, rec_w_out,
           final_norm):
    assert x.shape[0] == 1 and ctx.shape[1] == CTX_LEN and x.shape[2] == D_MODEL
    n = x.shape[1]
    assert n % ROW_TILE == 0 and n // ROW_TILE >= 3
    assert attn_w_in.shape[0] == rec_w_in.shape[0]
    d = D_MODEL
    cond8 = jnp.zeros((8, d), F32).at[0].set(c_ctx).at[1].set(c[0])
    m_attn = adaln_all(cond8, attn_ada_w, attn_ada_b)
    m_rec = adaln_all(cond8, rec_ada_w, rec_ada_b)
    mods = lambda m, i: tuple(m[i, :2, j * d:(j + 1) * d].reshape(2, 1, d) for j in range(3))
    cos_t, sin_t = _rope_tables(n)
    xs = jnp.concatenate([ctx[0], x[0]], axis=0)
    depth = attn_w_in.shape[0] + rec_w_in.shape[0]
    for layer in range(depth):
        i = layer // 2
        final = layer == depth - 1
        if layer % 2 == 0:
            xs = attn_layer(xs, mods(m_attn, i), attn_norm[i], attn_w_in[i], na_rpb[i], gqa_q_gain[i], gqa_k_gain[i],
                            attn_w_out[i], cos_t, sin_t)
        else:
            xs = rec_layer(xs, mods(m_rec, i), rec_norm[i], rec_w_in[i], rwkv_mu[i], rwkv_w0[i], rwkv_w_up[i],
                           rwkv_a0[i], rwkv_a_up[i], rwkv_k_k[i], rwkv_k_a[i], rwkv_r_k[i], rwkv_gn_w[i], rwkv_gn_b[i],
                           hy_short[i], hy_w1[i], hy_b1[i], hy_w2[i], hy_b2[i], hy_w3[i], hy_b3[i], hy_skip[i],
                           rec_w_out[i], final_norm, final)
    return xs[None]


def _hy_short_body(fh_ref, fl_ref, gh_ref, gl_ref, v_ref, x1_ref, x2_ref, taps_ref, scale_ref, skip_ref, o_ref):
    fh, fl, gh, gl = fh_ref[...], fl_ref[...], gh_ref[...], gl_ref[...]
    kp = fh.shape[0] // 2
    hw = HY_TAPS_W // 2
    tf = _dot3c(fh, fl, taps_ref[...])
    kre = (tf[:kp, :hw] + tf[:kp, hw:]) * scale_ref[...]
    kim = (tf[kp:, :hw] - tf[kp:, hw:]) * scale_ref[...]
    z = v_ref[...].astype(F32)
    for o, gate_ref in enumerate((x1_ref, x2_ref)):
        ls = slice(o * HY_WIDTH, (o + 1) * HY_WIDTH)
        zf = _dot3c(fh, fl, z)
        zre, zim = zf[:kp], zf[kp:]
        y = jnp.concatenate([zre * kre[:, ls] - zim * kim[:, ls], zre * kim[:, ls] + zim * kre[:, ls]], axis=0)
        z = gate_ref[...] * (_dot3c(gh, gl, y) + z * skip_ref[o:o + 1])
    o_ref[...] = z


def hyena_short(hv, hx1, hx2, taps, scale, skip):
    length = hv.shape[0]
    n = 2 * length
    k1 = length + 1
    kp = -(-k1 // 8) * 8
    kk = np.arange(kp)[:, None].astype(np.float64)
    live = kk < k1
    th = 2.0 * np.pi * kk * np.arange(length)[None, :] / n
    f = _split_const(np.concatenate([np.cos(th) * live, -np.sin(th) * live], axis=0))
    ck = np.where((kk == 0) | (kk == length), 1.0, 2.0) * live / n
    g = _split_const(np.concatenate([np.cos(th) * ck, -np.sin(th) * ck], axis=0).T)
    return pl.pallas_call(
        _hy_short_body,
        out_shape=jax.ShapeDtypeStruct((length, HY_WIDTH), F32),
        compiler_params=pltpu.CompilerParams(vmem_limit_bytes=VMEM_LIMIT),
        name="hyena_short",
    )(f[0], f[1], g[0], g[1], hv, hx1, hx2, taps, scale, skip)
```

```python
import functools
import math

import jax
import jax.numpy as jnp
import numpy as np
from jax import lax
from jax.experimental import pallas as pl
from jax.experimental.pallas import tpu as pltpu

F32 = jnp.float32
BF16 = jnp.bfloat16
HIGHEST = lax.Precision.HIGHEST

D_MODEL = 1024
GRID_W = 64
CTX_LEN = 256
HEAD_DIM = 64
BRANCH_W = 512
N_HEADS = 8
GQA_KV_W = 128
NA_WIN_ROWS = 8
NA_WIN_COLS = 16
ROPE_THETA = 10000.0
ROPE_FREQS = 16
NORM_EPS = 1e-6
ROW_TILE = 256
NA_GROUP_ROWS = 4
NEG_BIG = -1e30
LOG2E = math.log2(math.e)
QK_SCALE = HEAD_DIM ** -0.5 * LOG2E
FLASH_SUM_ROWS = 16
FLASH_Q_TILES = 4
FLASH_SCORES_AHEAD = 3
NA_SCORES_AHEAD = 1
VMEM_LIMIT = 56 * 1024 * 1024

ATTN_SPLITS = (512, 512, 512, 512, 512, 128, 128, 512)
GQA_HEAD_ORDER = (0, 4, 1, 5, 2, 6, 3, 7)


def _cparams(sem):
    return pltpu.CompilerParams(dimension_semantics=sem, vmem_limit_bytes=VMEM_LIMIT)


def _silu(v):
    return v * (1.0 / (1.0 + jnp.exp(-v)))


def _lane_half(shape):
    return (lax.broadcasted_iota(jnp.int32, shape, len(shape) - 1) // HEAD_DIM) % 2


def _dot_stat(a, block_mat):
    hi = a.astype(BF16)
    lo = (a - hi.astype(F32)).astype(BF16)
    return jnp.dot(hi, block_mat, preferred_element_type=F32) + jnp.dot(lo, block_mat, preferred_element_type=F32)


def _dot_nt(a, b):
    return lax.dot_general(a, b, (((1,), (1,)), ((), ())), preferred_element_type=F32)


def _adaln_body(cond_ref, w_ref, b_ref, o_ref):
    s = _silu(cond_ref[...])
    o_ref[0] = jnp.dot(s, w_ref[0], precision=HIGHEST, preferred_element_type=F32) + b_ref[0]


def adaln_all(cond8, ada_w, ada_b):
    nl = ada_w.shape[0]
    d = D_MODEL
    return pl.pallas_call(
        _adaln_body,
        out_shape=jax.ShapeDtypeStruct((nl, 8, 3 * d), F32),
        grid=(nl, 3),
        in_specs=[
            pl.BlockSpec((8, d), lambda l, j: (0, 0)),
            pl.BlockSpec((1, d, d), lambda l, j: (l, 0, j)),
            pl.BlockSpec((1, 1, d), lambda l, j: (l, 0, j)),
        ],
        out_specs=pl.BlockSpec((1, 8, d), lambda l, j: (l, 0, j)),
        compiler_params=_cparams(("parallel", "parallel")),
        name="adaln",
    )(cond8, ada_w, ada_b.reshape(nl, 1, 3 * d))


def _modulated(x_ref, g_ref, scale_ref, shift_ref):
    xf = x_ref[...]
    y = xf * lax.rsqrt(jnp.mean(xf * xf, axis=-1, keepdims=True) + NORM_EPS)
    return (y * g_ref[...]) * (1.0 + scale_ref[0]) + shift_ref[0]


def _mod_specs():
    d = D_MODEL
    return [
        pl.BlockSpec((ROW_TILE, d), lambda i: (i, 0)),
        pl.BlockSpec((1, d), lambda i: (0, 0)),
        pl.BlockSpec((1, 1, d), lambda i: (jnp.minimum(i, 1), 0, 0)),
        pl.BlockSpec((1, 1, d), lambda i: (jnp.minimum(i, 1), 0, 0)),
    ]


def _attn_in_body(x_ref, g_ref, scale_ref, shift_ref, w_ref, cos_ref, sin_ref, gq_ref, gqs_ref, gk_ref, gks_ref,
                  bdq_ref, bdk_ref,
                  qa_ref, ka_ref, va_ref, sga_ref, qb_ref, kb_ref, vb_ref, sgb_ref):
    xm = _modulated(x_ref, g_ref, scale_ref, shift_ref).astype(BF16)
    u = jnp.dot(xm, w_ref[...], preferred_element_type=F32)
    qa, ka, va, ga = u[:, 0:512], u[:, 512:1024], u[:, 1024:1536], u[:, 1536:2048]
    qb, kb, vb, gb = u[:, 2048:2560], u[:, 2560:2688], u[:, 2688:2816], u[:, 2816:3328]
    qbs, kbs = u[:, 3328:3840], u[:, 3840:3968]
    scale = QK_SCALE
    qa_ref[...] = (qa * scale).astype(BF16)
    ka_ref[...] = ka.astype(BF16)
    va_ref[...] = va.astype(BF16)
    sga_ref[...] = _silu(ga).astype(sga_ref.dtype)
    sgb_ref[...] = _silu(gb).astype(sgb_ref.dtype)
    vb_ref[...] = jnp.transpose(vb).astype(BF16)
    cos_k, sin_k = cos_ref[...], sin_ref[...]
    cos_q = jnp.concatenate([cos_k] * 4, axis=1)
    sin_q = jnp.concatenate([sin_k] * 4, axis=1)
    rs_q = lax.rsqrt(_dot_stat(qb * qb, bdq_ref[...]) + NORM_EPS)
    rs_k = lax.rsqrt(_dot_stat(kb * kb, bdk_ref[...]) + NORM_EPS)
    qr = rs_q * (qb * gq_ref[...] * cos_q + qbs * gqs_ref[...] * sin_q)
    kr = rs_k * (kb * gk_ref[...] * cos_k + kbs * gks_ref[...] * sin_k)
    qb_ref[...] = jnp.transpose(qr * scale).astype(BF16)
    kb_ref[...] = kr.astype(BF16)


def _rope_tables(n):
    t = jnp.arange(n, dtype=jnp.int32)
    pos = jnp.stack([t // GRID_W, t % GRID_W], axis=-1).astype(F32)
    inv_freq = ROPE_THETA ** (-jnp.arange(ROPE_FREQS, dtype=F32) / ROPE_FREQS)
    ang = pos[:, :, None] * inv_freq
    c, s = jnp.cos(ang), jnp.sin(ang)
    cos64 = jnp.concatenate([c[:, 0], c[:, 0], c[:, 1], c[:, 1]], axis=-1)
    sin64 = jnp.concatenate([-s[:, 0], s[:, 0], -s[:, 1], s[:, 1]], axis=-1)
    cos64 = jnp.concatenate([jnp.ones((CTX_LEN, HEAD_DIM), F32), cos64], axis=0)
    sin64 = jnp.concatenate([jnp.zeros((CTX_LEN, HEAD_DIM), F32), sin64], axis=0)
    return jnp.tile(cos64, (1, 2)), jnp.tile(sin64, (1, 2))


def _reorder_heads(w, order, axis):
    take = lambda h: lax.slice_in_dim(w, h * HEAD_DIM, (h + 1) * HEAD_DIM, axis=axis)
    return jnp.concatenate([take(h) for h in order], axis=axis)


def _swap_rope_halves(w):
    shp = w.shape
    return jnp.flip(w.reshape(shp[:-1] + (shp[-1] // (2 * ROPE_FREQS), 2, ROPE_FREQS)), axis=-2).reshape(shp)


def attn_in_proj(xs, norm_g, scale2, shift2, w_in, q_gain, k_gain, cos_t, sin_t):
    r = xs.shape[0]
    d = D_MODEL
    parts, start = [], 0
    for s in ATTN_SPLITS:
        parts.append(w_in[:, start:start + s])
        start += s
    wqa, wka, wva, wga, wqb, wkb, wvb, wgb = parts
    wqb_p = _reorder_heads(wqb, GQA_HEAD_ORDER, 1)
    wgb_p = _reorder_heads(wgb, GQA_HEAD_ORDER, 1)
    wqb_sw = _swap_rope_halves(wqb_p)
    wkb_sw = _swap_rope_halves(wkb)
    w_ext = jnp.concatenate([wqa, wka, wva, wga, wqb_p, wkb, wvb, wgb_p, wqb_sw, wkb_sw], axis=1).astype(BF16)
    gq = jnp.tile(q_gain, N_HEADS)[None]
    gqs = jnp.tile(_swap_rope_halves(q_gain), N_HEADS)[None]
    gk = jnp.tile(k_gain, 2)[None]
    gks = jnp.tile(_swap_rope_halves(k_gain), 2)[None]
    bdq = jnp.asarray(np.kron(np.eye(N_HEADS), np.full((HEAD_DIM, HEAD_DIM), 1.0 / HEAD_DIM)), BF16)
    bdk = jnp.asarray(np.kron(np.eye(2), np.full((HEAD_DIM, HEAD_DIM), 1.0 / HEAD_DIM)), BF16)
    wcols = w_ext.shape[1]
    const = lambda shp: pl.BlockSpec(shp, lambda i: (0,) * len(shp))
    rows = lambda w: pl.BlockSpec((ROW_TILE, w), lambda i: (i, 0))
    out_shapes = [
        jax.ShapeDtypeStruct((r, 512), BF16), jax.ShapeDtypeStruct((r, 512), BF16), jax.ShapeDtypeStruct((r, 512), BF16),
        jax.ShapeDtypeStruct((r, 512), BF16),
        jax.ShapeDtypeStruct((512, r), BF16), jax.ShapeDtypeStruct((r, 128), BF16), jax.ShapeDtypeStruct((128, r), BF16),
        jax.ShapeDtypeStruct((r, 512), BF16),
    ]
    cols = lambda w: pl.BlockSpec((w, ROW_TILE), lambda i: (0, i))
    return pl.pallas_call(
        _attn_in_body,
        out_shape=out_shapes,
        grid=(r // ROW_TILE,),
        in_specs=_mod_specs() + [const((d, wcols)), rows(128), rows(128), const((1, 512)), const((1, 512)),
                                 const((1, 128)), const((1, 128)), const((512, 512)), const((128, 128))],
        out_specs=[rows(512), rows(512), rows(512), rows(512), cols(512), rows(128), cols(128), rows(512)],
        compiler_params=_cparams(("parallel",)),
        name="attn_in_proj",
    )(xs, norm_g[None], scale2, shift2, w_ext, cos_t, sin_t, gq, gqs, gk, gks, bdq, bdk)


def _na_cols_body(rpb_ref, sel_ref, neg_ref, o_ref):
    o_ref[...] = jnp.dot(rpb_ref[...], sel_ref[...], precision=HIGHEST, preferred_element_type=F32) + neg_ref[...]


def _na_bias_tables(rpb, rows):
    nrel_r, nrel_c = 2 * NA_WIN_ROWS - 1, 2 * NA_WIN_COLS - 1
    qc = np.arange(GRID_W)[:, None]
    kc = np.arange(GRID_W)[None, :]
    col0 = np.clip(qc - NA_WIN_COLS // 2, 0, GRID_W - NA_WIN_COLS)
    col_ok = (kc >= col0) & (kc < col0 + NA_WIN_COLS)
    rc = kc - qc + NA_WIN_COLS - 1
    sel = np.zeros((128, GRID_W * GRID_W), np.float32)
    sel[np.where(col_ok, rc, 127).reshape(-1), np.arange(GRID_W * GRID_W)] = col_ok.reshape(-1)
    neg = np.where(col_ok, 0.0, NEG_BIG).astype(np.float32).reshape(1, -1)
    rpb2 = jnp.zeros((128, 128), F32).at[:N_HEADS * nrel_r, :nrel_c].set(rpb.reshape(N_HEADS * nrel_r, nrel_c))
    cols = pl.pallas_call(
        _na_cols_body,
        out_shape=jax.ShapeDtypeStruct((128, GRID_W * GRID_W), F32),
        name="na_bias_cols",
    )(rpb2, jnp.asarray(sel), jnp.asarray(neg))
    cols = cols[:N_HEADS * nrel_r].reshape(N_HEADS, nrel_r, GRID_W, GRID_W)
    kh = min(NA_WIN_ROWS, rows)
    g = rows // NA_GROUP_ROWS
    cases = [(0, 0), (NA_GROUP_ROWS, 0), (rows - NA_GROUP_ROWS, NA_GROUP_ROWS * (g - 3))]
    masked = jnp.full((N_HEADS, GRID_W, GRID_W), NEG_BIG, F32)
    tabs = []
    for qr_first, start in cases:
        blocks = []
        for j in range(NA_GROUP_ROWS):
            qr = qr_first + j
            row0 = min(max(qr - kh // 2, 0), rows - kh)
            for i in range(3 * NA_GROUP_ROWS):
                kr = start + i
                blocks.append(cols[:, kr - qr + NA_WIN_ROWS - 1] if row0 <= kr < row0 + kh else masked)
        tab = jnp.stack(blocks, axis=1).reshape(N_HEADS, NA_GROUP_ROWS, 3 * NA_GROUP_ROWS, GRID_W, GRID_W)
        tabs.append(tab.transpose(0, 1, 3, 2, 4).reshape(N_HEADS, ROW_TILE, 3 * ROW_TILE))
    return jnp.stack(tabs) * LOG2E


def _na_body(q_ref, kc_ref, k0_ref, k1_ref, k2_ref, vc_ref, v0_ref, v1_ref, v2_ref, bias_ref, sg_ref, o_ref):
    half = _lane_half((ROW_TILE, 128))
    k_refs = (k0_ref, k1_ref, k2_ref, kc_ref)
    v_refs = (v0_ref, v1_ref, v2_ref, vc_ref)

    def scores(h):
        hp, j = divmod(h, 2)
        ls = slice(hp * 128, (hp + 1) * 128)
        qp = q_ref[:, ls]
        qm = jnp.where(half == j, qp, jnp.zeros_like(qp))
        return [_dot_nt(qm, r[:, ls]) for r in k_refs]

    pending = [scores(h) for h in range(NA_SCORES_AHEAD)]
    outs = []
    for h in range(N_HEADS):
        hp, j = divmod(h, 2)
        ls = slice(hp * 128, (hp + 1) * 128)
        s = pending.pop(0)
        if h + NA_SCORES_AHEAD < N_HEADS:
            pending.append(scores(h + NA_SCORES_AHEAD))
        s_win = jnp.concatenate(s[:3], axis=1) + bias_ref[0, h]
        s_ctx = s[3]
        m = jnp.maximum(jnp.max(s_win, axis=1, keepdims=True), jnp.max(s_ctx, axis=1, keepdims=True))
        p_win = jnp.exp2(s_win - m).astype(BF16)
        p_ctx = jnp.exp2(s_ctx - m).astype(BF16)
        vms = [jnp.where(half == j, r[:, ls], jnp.ones((ROW_TILE, 128), BF16)) for r in v_refs]
        o = jnp.dot(p_ctx, vms[3], preferred_element_type=F32)
        for b in range(3):
            o += jnp.dot(p_win[:, b * ROW_TILE:(b + 1) * ROW_TILE], vms[b], preferred_element_type=F32)
        outs.append(o / pltpu.roll(o, HEAD_DIM, 1))
        if j == 1:
            o_pair = jnp.where(half == 0, outs[h - 1], outs[h])
            o_ref[:, ls] = (o_pair * sg_ref[:, ls]).astype(BF16)


def na_attention(qa, ka, va, sga, bias_tabs, n):
    g = n // ROW_TILE
    w = BRANCH_W

    def kv_spec(off):
        return pl.BlockSpec((ROW_TILE, w), lambda i: (jnp.clip(i - 1, 0, g - 3) + off + 1, 0))

    ctx_spec = pl.BlockSpec((ROW_TILE, w), lambda i: (0, 0))
    q_spec = pl.BlockSpec((ROW_TILE, w), lambda i: (i + 1, 0))
    case = lambda i: jnp.where(i == 0, 0, jnp.where(i == g - 1, 2, 1))
    bias_spec = pl.BlockSpec((1, N_HEADS, ROW_TILE, 3 * ROW_TILE), lambda i: (case(i), 0, 0, 0))
    return pl.pallas_call(
        _na_body,
        out_shape=jax.ShapeDtypeStruct((n, w), BF16),
        grid=(g,),
        in_specs=[q_spec, ctx_spec, kv_spec(0), kv_spec(1), kv_spec(2), ctx_spec, kv_spec(0), kv_spec(1), kv_spec(2),
                  bias_spec, q_spec],
        out_specs=pl.BlockSpec((ROW_TILE, w), lambda i: (i, 0)),
        compiler_params=_cparams(("parallel",)),
        name="na_attention",
    )(qa, ka, ka, ka, ka, va, va, va, va, bias_tabs, sga)


def _flash_mha_body(q_ref, k_ref, v_ref, sg_ref, o_ref, m_ref, acc_ref):
    kv = pl.program_id(1)
    tq = q_ref.shape[0]

    @pl.when(kv == 0)
    def _():
        m_ref[...] = jnp.full(m_ref.shape, NEG_BIG, F32)
        acc_ref[...] = jnp.zeros(acc_ref.shape, F32)

    khalf = _lane_half((k_ref.shape[0], 128))
    for p in range(N_HEADS // 2):
        ls = slice(p * 128, (p + 1) * 128)
        qp = q_ref[:, ls]
        kp = k_ref[:, ls]
        vp = v_ref[:, ls]
        for j in range(2):
            hh = 2 * p + j
            km = jnp.where(khalf == j, kp, jnp.zeros_like(kp))
            vm = jnp.where(khalf == j, vp, jnp.ones_like(vp))
            s = _dot_nt(qp, km)
            m_prev = m_ref[hh]
            m_new = jnp.maximum(m_prev, jnp.max(s, axis=1, keepdims=True))
            alpha = jnp.exp2(m_prev - m_new)
            pr = jnp.exp2(s - m_new[:, :1]).astype(BF16)
            acc_ref[hh] = alpha * acc_ref[hh] + jnp.dot(pr, vm, preferred_element_type=F32)
            m_ref[hh] = m_new

    @pl.when(kv == pl.num_programs(1) - 1)
    def _():
        half = _lane_half((tq, 128))
        for p in range(N_HEADS // 2):
            ls = slice(p * 128, (p + 1) * 128)
            a0, a1 = acc_ref[2 * p], acc_ref[2 * p + 1]
            o0 = a0 / pltpu.roll(a0, HEAD_DIM, 1)
            o1 = a1 / pltpu.roll(a1, HEAD_DIM, 1)
            o_ref[:, ls] = (jnp.where(half == 0, o0, o1) * sg_ref[:, ls]).astype(BF16)


def flash_mha(q, k, v, sg, *, q_block0, nq, tk, nk):
    tq = ROW_TILE
    return pl.pallas_call(
        _flash_mha_body,
        out_shape=jax.ShapeDtypeStruct((nq * tq, BRANCH_W), BF16),
        grid=(nq, nk),
        in_specs=[
            pl.BlockSpec((tq, BRANCH_W), lambda i, j: (i + q_block0, 0)),
            pl.BlockSpec((tk, BRANCH_W), lambda i, j: (j, 0)),
            pl.BlockSpec((tk, BRANCH_W), lambda i, j: (j, 0)),
            pl.BlockSpec((tq, BRANCH_W), lambda i, j: (i + q_block0, 0)),
        ],
        out_specs=pl.BlockSpec((tq, BRANCH_W), lambda i, j: (i, 0)),
        scratch_shapes=[pltpu.VMEM((N_HEADS, tq, 128), F32)] * 2,
        compiler_params=_cparams(("parallel", "arbitrary")),
        name="flash_mha",
    )(q, k, v, sg)


def _flash_gqa_body(*refs, nsub):
    qt_refs, (k_ref, vt_ref), sg_refs = refs[:nsub], refs[nsub:nsub + 2], refs[nsub + 2:2 * nsub + 2]
    o_ref, m_ref, acc_ref = refs[2 * nsub + 2:]
    kv = pl.program_id(1)
    tq = qt_refs[0].shape[1]
    tk = k_ref.shape[0]

    @pl.when(kv == 0)
    def _():
        m_ref[...] = jnp.full(m_ref.shape, NEG_BIG, F32)
        acc_ref[...] = jnp.zeros(acc_ref.shape, F32)

    khalf = _lane_half((tk, 128))
    kb = k_ref[...]
    kms = [jnp.where(khalf == j, kb, jnp.zeros_like(kb)) for j in range(2)]
    ones = jnp.ones((FLASH_SUM_ROWS, tk), BF16)
    vms = [jnp.concatenate([vt_ref[j * HEAD_DIM:(j + 1) * HEAD_DIM, :], ones], axis=0) for j in range(2)]
    units = nsub * N_HEADS

    def scores(u):
        sub, hh = divmod(u, N_HEADS)
        p, j = divmod(hh, 2)
        return jnp.dot(kms[j], qt_refs[sub][p * 128:(p + 1) * 128, :], preferred_element_type=F32).astype(BF16)

    pending = [scores(u) for u in range(FLASH_SCORES_AHEAD)]
    for u in range(units):
        st = pending.pop(0)
        if u + FLASH_SCORES_AHEAD < units:
            pending.append(scores(u + FLASH_SCORES_AHEAD))
        m_prev = m_ref[u]
        m_new = jnp.maximum(m_prev, jnp.max(st, axis=0, keepdims=True).astype(F32))
        alpha = jnp.exp2(m_prev - m_new)
        pt = jnp.exp2(st - m_new[0:1].astype(BF16))
        acc_ref[u] = alpha[0:1] * acc_ref[u] + jnp.dot(vms[u % 2], pt, preferred_element_type=F32)
        m_ref[u] = m_new

    @pl.when(kv == pl.num_programs(1) - 1)
    def _():
        for sub in range(nsub):
            for p in range(N_HEADS // 2):
                ls = slice(p * 128, (p + 1) * 128)
                a0, a1 = acc_ref[sub * N_HEADS + 2 * p], acc_ref[sub * N_HEADS + 2 * p + 1]
                ot = jnp.concatenate([a0[:HEAD_DIM] / a0[HEAD_DIM:HEAD_DIM + 1],
                                      a1[:HEAD_DIM] / a1[HEAD_DIM:HEAD_DIM + 1]], axis=0)
                o_ref[sub * tq:(sub + 1) * tq, ls] = (jnp.transpose(ot) * sg_refs[sub][:, ls]).astype(BF16)


def flash_gqa(qt, k, vt, sg, *, q_block0, nq, tk, nk, nsub):
    tq = ROW_TILE
    assert nq % nsub == 0
    qt_specs = [pl.BlockSpec((BRANCH_W, tq), lambda i, j, s=s: (0, nsub * i + s + q_block0)) for s in range(nsub)]
    sg_specs = [pl.BlockSpec((tq, BRANCH_W), lambda i, j, s=s: (nsub * i + s + q_block0, 0)) for s in range(nsub)]
    return pl.pallas_call(
        functools.partial(_flash_gqa_body, nsub=nsub),
        out_shape=jax.ShapeDtypeStruct((nq * tq, BRANCH_W), BF16),
        grid=(nq // nsub, nk),
        in_specs=qt_specs + [pl.BlockSpec((tk, GQA_KV_W), lambda i, j: (j, 0)),
                             pl.BlockSpec((GQA_KV_W, tk), lambda i, j: (0, j))] + sg_specs,
        out_specs=pl.BlockSpec((nsub * tq, BRANCH_W), lambda i, j: (i, 0)),
        scratch_shapes=[pltpu.VMEM((nsub * N_HEADS, 8, tq), F32), pltpu.VMEM((nsub * N_HEADS, HEAD_DIM + FLASH_SUM_ROWS, tq), F32)],
        compiler_params=_cparams(("parallel", "arbitrary")),
        name="flash_gqa",
    )(*([qt] * nsub), k, vt, *([sg] * nsub))


def _out_body(x_ref, ya_ref, yac_ref, yb_ref, ybc_ref, wa_ref, wb_ref, gate_ref, o_ref):
    is_ctx = pl.program_id(0) == 0
    ya = jnp.where(is_ctx, yac_ref[...], ya_ref[...])
    yb = jnp.where(is_ctx, ybc_ref[...], yb_ref[...])
    y = jnp.dot(ya, wa_ref[...], preferred_element_type=F32)
    y += jnp.dot(yb, wb_ref[...], preferred_element_type=F32)
    o_ref[...] = x_ref[...] + gate_ref[0] * y


def out_proj(xs, ya_lat, ya_ctx, yb_lat, yb_ctx, wa, wb, gate2):
    r = xs.shape[0]
    d = D_MODEL
    rows = lambda w: pl.BlockSpec((ROW_TILE, w), lambda i: (i, 0))
    lat = pl.BlockSpec((ROW_TILE, BRANCH_W), lambda i: (jnp.maximum(i - 1, 0), 0))
    const = lambda shp: pl.BlockSpec(shp, lambda i: (0,) * len(shp))
    ctx = const((CTX_LEN, BRANCH_W))
    return pl.pallas_call(
        _out_body,
        out_shape=jax.ShapeDtypeStruct((r, d), F32),
        grid=(r // ROW_TILE,),
        in_specs=[rows(d), lat, ctx, lat, ctx, const((BRANCH_W, d)), const((BRANCH_W, d)),
                  pl.BlockSpec((1, 1, d), lambda i: (jnp.minimum(i, 1), 0, 0))],
        out_specs=rows(d),
        compiler_params=_cparams(("parallel",)),
        name="out_proj",
    )(xs, ya_lat, ya_ctx, yb_lat, yb_ctx, wa.astype(BF16), wb.astype(BF16), gate2)


KV_TILE_MAX_BLOCKS = 13


def _kv_tile(r):
    nb = r // ROW_TILE
    best = max(k for k in range(1, KV_TILE_MAX_BLOCKS + 1) if nb % k == 0)
    return best * ROW_TILE, nb // best


def attn_layer(xs, mods, norm_g, w_in, rpb, q_gain, k_gain, w_out, cos_t, sin_t):
    r = xs.shape[0]
    n = r - CTX_LEN
    shift2, scale2, gate2 = mods
    qa, ka, va, sga, qbt, kb, vbt, sgb = attn_in_proj(xs, norm_g, scale2, shift2, w_in, q_gain, k_gain, cos_t, sin_t)
    bias_tabs = _na_bias_tables(rpb, n // GRID_W)
    ya_lat = na_attention(qa, ka, va, sga, bias_tabs, n)
    ya_ctx = flash_mha(qa, ka, va, sga, q_block0=0, nq=1, tk=CTX_LEN, nk=1)
    tk, nk = _kv_tile(r)
    nq = n // ROW_TILE
    yb_lat = flash_gqa(qbt, kb, vbt, sgb, q_block0=1, nq=nq, tk=tk, nk=nk, nsub=FLASH_Q_TILES if nq % FLASH_Q_TILES == 0 else 1)
    yb_ctx = flash_gqa(qbt, kb, vbt, sgb, q_block0=0, nq=1, tk=CTX_LEN, nk=1, nsub=1)
    wb = _reorder_heads(w_out[BRANCH_W:], GQA_HEAD_ORDER, 0)
    return out_proj(xs, ya_lat, ya_ctx, yb_lat, yb_ctx, w_out[:BRANCH_W], wb, gate2)


def _split(a):
    hi = a.astype(BF16)
    return hi, (a - hi.astype(F32)).astype(BF16)


def _dot3(a, b, dims=(((1,), (0,)), ((), ()))):
    ah, al = _split(a)
    bh, bl = _split(b)
    dg = functools.partial(lax.dot_general, dimension_numbers=dims, preferred_element_type=F32)
    return dg(ah, bh) + (dg(al, bh) + dg(ah, bl))


def _dot1(a, b, dims=(((1,), (0,)), ((), ()))):
    return lax.dot_general(a.astype(BF16), b.astype(BF16), dims, preferred_element_type=F32)


_NT = (((1,), (1,)), ((), ()))
_TN = (((0,), (0,)), ((), ()))


RWKV_SHIFT_W = 1664
HY_IN_W = 1536
HALO = 8
REC_HALO_W = RWKV_SHIFT_W + HY_IN_W


def _rec_in_body(x_ref, xp_ref, xn_ref, g_ref, scale_ref, shift_ref, w_ref, mu_ref, taps_ref,
                 rw_ref, hv_ref, hx1_ref, hx2_ref, sgr_ref, sgh_ref, u_scr):
    i = pl.program_id(0)
    nt = pl.num_programs(0)
    xe = jnp.concatenate([xp_ref[...], x_ref[...], xn_ref[...]], axis=0)
    y = xe * lax.rsqrt(jnp.mean(xe * xe, axis=-1, keepdims=True) + NORM_EPS)
    xm = ((y * g_ref[...]) * (1.0 + scale_ref[0]) + shift_ref[0]).astype(BF16)
    u = jnp.dot(xm, w_ref[...], preferred_element_type=F32)
    row = lax.broadcasted_iota(jnp.int32, (ROW_TILE + 2 * HALO, 1), 0)
    keep = jnp.logical_and(jnp.logical_or(row >= HALO, i >= 2),
                           jnp.logical_or(row < ROW_TILE + HALO, jnp.logical_and(i >= 1, i < nt - 1)))
    u_scr[...] = jnp.where(keep, u[:, :REC_HALO_W], 0.0)
    up = u_scr[pl.ds(HALO - 1, ROW_TILE), :]
    uc = u_scr[pl.ds(HALO, ROW_TILE), :]
    un = u_scr[pl.ds(HALO + 1, ROW_TILE), :]
    w = RWKV_SHIFT_W
    rw_c = uc[:, :w]
    rw_ref[...] = rw_c + (0.5 * (up[:, :w] + un[:, :w]) - rw_c) * mu_ref[...]
    hy = up[:, w:] * taps_ref[0:1] + uc[:, w:] * taps_ref[1:2] + un[:, w:] * taps_ref[2:3]
    hv_ref[...] = hy[:, 0:512].astype(hv_ref.dtype)
    hx1_ref[...] = hy[:, 512:1024].astype(hx1_ref.dtype)
    hx2_ref[...] = hy[:, 1024:1536].astype(hx2_ref.dtype)
    uc_all = u[HALO:HALO + ROW_TILE]
    sgr_ref[...] = _silu(uc_all[:, REC_HALO_W:REC_HALO_W + 512]).astype(sgr_ref.dtype)
    sgh_ref[...] = _silu(uc_all[:, REC_HALO_W + 512:REC_HALO_W + 1024]).astype(sgh_ref.dtype)


def rec_in_proj(xs, norm_g, scale2, shift2, w_in, mu, hy_short):
    r = xs.shape[0]
    d = D_MODEL
    w = RWKV_SHIFT_W
    w_ext = jnp.concatenate([w_in[:, :w], w_in[:, w + 512:w + 512 + HY_IN_W], w_in[:, w:w + 512],
                             w_in[:, w + 512 + HY_IN_W:]], axis=1).astype(BF16)
    nh = r // HALO
    per = ROW_TILE // HALO
    const = lambda shp: pl.BlockSpec(shp, lambda i: (0,) * len(shp))
    rows = lambda wd: pl.BlockSpec((ROW_TILE, wd), lambda i: (i, 0))
    f = lambda wd: jax.ShapeDtypeStruct((r, wd), F32)
    h = jax.ShapeDtypeStruct((r, 512), BF16)
    mod = _mod_specs()
    return pl.pallas_call(
        _rec_in_body,
        out_shape=[f(w), h, h, h, h, h],
        grid=(r // ROW_TILE,),
        in_specs=[mod[0],
                  pl.BlockSpec((HALO, d), lambda i: (jnp.maximum(i * per - 1, 0), 0)),
                  pl.BlockSpec((HALO, d), lambda i: (jnp.minimum((i + 1) * per, nh - 1), 0)),
                  mod[1], mod[2], mod[3], const((d, w_ext.shape[1])), const((1, w)), const((3, HY_IN_W))],
        out_specs=[rows(w), rows(512), rows(512), rows(512), rows(512), rows(512)],
        scratch_shapes=[pltpu.VMEM((ROW_TILE + 2 * HALO, REC_HALO_W), F32)],
        compiler_params=_cparams(("parallel",)),
        name="rec_in_proj",
    )(xs, xs, xs, norm_g[None], scale2, shift2, w_ext, mu[None], hy_short)


CHUNK = 64
CPT = ROW_TILE // CHUNK


def _block_sum_mat(width, value):
    return jnp.asarray(np.kron(np.eye(width // HEAD_DIM), np.full((HEAD_DIM, HEAD_DIM), value)), F32)


PREP_PAIRS = 4


def _rwkv_prep_body(r_ref, k_ref, v_ref, lora_ref, w0_ref, wup_ref, a0_ref, aup_ref, kk_ref, ka_ref, rk_ref,
                    tri_ref, bs_ref, g_ref, add_ref, bonus_ref):
    t = ROW_TILE
    lora = lora_ref[...]
    bs = bs_ref[...]
    row = lax.broadcasted_iota(jnp.int32, (t, t), 0)
    col = lax.broadcasted_iota(jnp.int32, (t, t), 1)
    same = (row // CHUNK) == (col // CHUNK)
    eye = (row == col).astype(F32)
    wl_all = _dot3(jnp.tanh(lora), wup_ref[...])
    al_all = _dot3(lora, aup_ref[...])
    half = _lane_half((t, 128))
    half_c = _lane_half((HEAD_DIM, 128))
    rowc = lax.broadcasted_iota(jnp.int32, (HEAD_DIM, 128), 0)
    lanec = lax.broadcasted_iota(jnp.int32, (HEAD_DIM, 128), 1)
    level_masks = []
    bsz = 2
    while bsz < CHUNK:
        level_masks.append(jnp.logical_and((row // (2 * bsz)) == (col // (2 * bsz)), (row // bsz) != (col // bsz)))
        bsz *= 2
    first_mask = (row // 2) == (col // 2)

    groups = []
    for q in range(PREP_PAIRS):
        ls = slice(q * 128, (q + 1) * 128)
        r, k, v = r_ref[:, ls], k_ref[:, ls], v_ref[:, ls]
        kk = k * kk_ref[:, ls]
        kk = kk * lax.rsqrt(_dot_stat(kk * kk, bs) + 1e-12)
        kd_sum = None
        for d in range(2):
            ds = slice(q * 256 + d * 128, q * 256 + (d + 1) * 128)
            wl = w0_ref[d, :, ls] + wl_all[:, ds]
            z = -wl
            w_log = -(jnp.maximum(z, 0.0) + jnp.log(1.0 + jnp.exp(-jnp.abs(z)))) - 0.5
            lw = -jnp.exp(w_log)
            a = 1.0 / (1.0 + jnp.exp(-(a0_ref[d, :, ls] + al_all[:, ds])))
            kd = k * (1.0 + (a - 1.0) * ka_ref[:, ls])
            b = kk * a
            kd_sum = kd if kd_sum is None else kd_sum + kd
            incl = jnp.logical_and(same, (col <= row) if d == 0 else (col >= row))
            lw_hi, lw_lo = _split(lw)
            tri = tri_ref[d]
            cs = jnp.dot(tri, lw_hi, preferred_element_type=F32) + jnp.dot(tri, lw_lo, preferred_element_type=F32)
            ends = [c * CHUNK + (CHUNK - 1 if d == 0 else 0) for c in range(CPT)]
            tot = jnp.concatenate([jnp.broadcast_to(cs[e:e + 1], (CHUNK, 128)) for e in ends], axis=0)
            w_inv = jnp.exp(-cs)
            w_rest = jnp.exp(tot - cs)
            groups.append(dict(q=q, d=d, v=v, incl=incl, strict=jnp.logical_and(incl, row != col), tot=tot,
                               kkt=kk * jnp.exp(cs - lw), kh=kd * w_inv, bh=b * w_inv, rt=r * jnp.exp(cs),
                               kdd=kd * w_rest, bdd=b * w_rest))
        bonus_ref[:, ls] = 0.5 * _dot_stat(r * kd_sum * rk_ref[:, ls], bs) * v
    chains = [(gg, j) for gg in groups for j in range(2)]
    sels = [half == j for _, j in chains]
    bms = [jnp.where(sel, gg["bh"], 0.0) for (gg, _), sel in zip(chains, sels)]
    kms = [jnp.where(sel, gg["kh"], 0.0) for (gg, _), sel in zip(chains, sels)]
    l_bs = [jnp.where(gg["strict"], _dot1(gg["kkt"], bm, _NT), 0.0) for (gg, _), bm in zip(chains, bms)]
    tinvs = [eye - jnp.where(first_mask, l_b, 0.0) for l_b in l_bs]
    for mask in level_masks:
        xs = [_dot1(jnp.where(mask, l_b, 0.0), tinv) for l_b, tinv in zip(l_bs, tinvs)]
        tinvs = [tinv - _dot1(tinv, x) for tinv, x in zip(tinvs, xs)]
    l_ks = [jnp.where(gg["strict"], _dot1(gg["kkt"], km, _NT), 0.0) for (gg, _), km in zip(chains, kms)]
    a_rks = [jnp.where(gg["incl"], _dot1(gg["rt"], km, _NT), 0.0) for (gg, _), km in zip(chains, kms)]
    a_rbs = [jnp.where(gg["incl"], _dot1(gg["rt"], bm, _NT), 0.0) for (gg, _), bm in zip(chains, bms)]
    lvs = [_dot1(l_k, gg["v"]) for (gg, _), l_k in zip(chains, l_ks)]
    pus = [_dot1(tinv, jnp.concatenate([gg["kkt"], lv], axis=1)) for (gg, _), tinv, lv in zip(chains, tinvs, lvs)]
    cors = [_dot1(a_rb, pu) for a_rb, pu in zip(a_rbs, pus)]
    ps = [pu[:, :128] for pu in pus]
    u0s = [pu[:, 128:] for pu in pus]
    qs = [gg["rt"] - cor[:, :128] for (gg, _), cor in zip(chains, cors)]
    y0s = [_dot1(a_rk, gg["v"]) - cor[:, 128:] for (gg, _), a_rk, cor in zip(chains, a_rks, cors)]

    sel0 = half == 0
    for n, gg in enumerate(groups):
        d, v = gg["d"], gg["v"]
        ls = slice(gg["q"] * 128, (gg["q"] + 1) * 128)
        p, u0, q, y0 = (jnp.where(sel0, x[2 * n], x[2 * n + 1]) for x in (ps, u0s, qs, y0s))
        for c in range(CPT):
            rs = slice(c * CHUNK, (c + 1) * CHUNK)
            x1 = _dot1(gg["bdd"][rs], p[rs], _TN)
            x2 = _dot1(gg["kdd"][rs], v[rs], _TN) - _dot1(gg["bdd"][rs], u0[rs], _TN)
            m_pair = jnp.where(half_c == 0, x1[:HEAD_DIM], x1[HEAD_DIM:])
            n_pair = jnp.where(half_c == 0, x2[:HEAD_DIM], x2[HEAD_DIM:])
            wc = jnp.exp(gg["tot"][c * CHUNK:c * CHUNK + 1])
            dg = jnp.where((lanec % HEAD_DIM) == rowc, wc, 0.0)
            g_ref[c, d, 0:HEAD_DIM, ls] = dg - m_pair
            g_ref[c, d, HEAD_DIM:, ls] = q[rs]
            add_ref[c, d, 0:HEAD_DIM, ls] = n_pair
            add_ref[c, d, HEAD_DIM:, ls] = y0[rs]


def _lora_ext(up, first_row):
    out = jnp.zeros((2, 128, BRANCH_W), F32)
    for d in range(2):
        out = out.at[d, first_row + 32 * d:first_row + 32 * (d + 1)].set(up[d])
    return out


def rwkv_prep(rw, w0, w_up, a0, a_up, k_k, k_a, r_k):
    r = rw.shape[0]
    nt = r // ROW_TILE
    nch = r // CHUNK
    t = ROW_TILE
    ii = np.arange(t)
    same = (ii[:, None] // CHUNK) == (ii[None, :] // CHUNK)
    tri = jnp.asarray(np.stack([same & (ii[None, :] <= ii[:, None]), same & (ii[None, :] >= ii[:, None])]), BF16)
    pair_cat = lambda w: w.reshape(2, 128, N_HEADS // 2, 128).transpose(1, 2, 0, 3).reshape(128, 2 * BRANCH_W)
    w = 128 * PREP_PAIRS
    steps = N_HEADS // 2 // PREP_PAIRS
    lane = lambda blk: pl.BlockSpec((t, w), lambda i, p, blk=blk: (i, blk + p))
    pvec = pl.BlockSpec((1, w), lambda i, p: (0, p))
    dvec = pl.BlockSpec((2, 1, w), lambda i, p: (0, 0, p))
    dmat = pl.BlockSpec((128, 2 * w), lambda i, p: (0, p))
    gspec = pl.BlockSpec((CPT, 2, HEAD_DIM + CHUNK, w), lambda i, p: (i, 0, 0, p))
    gshape = jax.ShapeDtypeStruct((nch, 2, HEAD_DIM + CHUNK, BRANCH_W), F32)
    return pl.pallas_call(
        _rwkv_prep_body,
        out_shape=[gshape, gshape, jax.ShapeDtypeStruct((r, BRANCH_W), F32)],
        grid=(nt, steps),
        in_specs=[lane(0), lane(steps), lane(2 * steps), pl.BlockSpec((t, 128), lambda i, p: (i, 12)),
                  dvec, dmat, dvec, dmat, pvec, pvec, pvec,
                  pl.BlockSpec((2, t, t), lambda i, p: (0, 0, 0)),
                  pl.BlockSpec((128, 128), lambda i, p: (0, 0))],
        out_specs=[gspec, gspec, pl.BlockSpec((t, w), lambda i, p: (i, p))],
        compiler_params=_cparams(("parallel", "parallel")),
        name="rwkv_prep",
    )(rw, rw, rw, rw, w0.reshape(2, 1, BRANCH_W), pair_cat(_lora_ext(w_up, 0)), a0.reshape(2, 1, BRANCH_W),
      pair_cat(_lora_ext(a_up, 64)), k_k[None], k_a[None], r_k.reshape(1, BRANCH_W), tri,
      _block_sum_mat(128, 1.0).astype(BF16))


def _rwkv_scan_body(gf_ref, af_ref, gb_ref, ab_ref, yf_ref, yb_ref, st_ref):
    @pl.when(pl.program_id(0) == 0)
    def _():
        st_ref[...] = jnp.zeros(st_ref.shape, F32)

    rowh = lax.broadcasted_iota(jnp.int32, (128, 128), 0) // HEAD_DIM
    diag = rowh == _lane_half((128, 128))
    for s in range(CPT):
        for d, (g_ref, a_ref, y_ref, c) in enumerate(((gf_ref, af_ref, yf_ref, s), (gb_ref, ab_ref, yb_ref, CPT - 1 - s))):
            for p in range(N_HEADS // 2):
                ls = slice(p * 128, (p + 1) * 128)
                out = _dot3(g_ref[c, 0, :, ls], st_ref[d, p]) + a_ref[c, 0, :, ls]
                hn = out[:HEAD_DIM]
                st_ref[d, p] = jnp.where(diag, jnp.concatenate([hn, hn], axis=0), 0.0)
                y_ref[c * CHUNK:(c + 1) * CHUNK, ls] = out[HEAD_DIM:]


def rwkv_scan(g, add):
    nch = g.shape[0]
    r = nch * CHUNK
    nt = r // ROW_TILE
    assert CTX_LEN == ROW_TILE
    rev = lambda i: jnp.where(i == 0, 0, nt - i)
    blk = (CPT, 1, HEAD_DIM + CHUNK, BRANCH_W)
    fwd = pl.BlockSpec(blk, lambda i: (i, 0, 0, 0))
    bwd = pl.BlockSpec(blk, lambda i: (rev(i), 1, 0, 0))
    yshape = jax.ShapeDtypeStruct((r, BRANCH_W), F32)
    return pl.pallas_call(
        _rwkv_scan_body,
        out_shape=[yshape, yshape],
        grid=(nt,),
        in_specs=[fwd, fwd, bwd, bwd],
        out_specs=[pl.BlockSpec((ROW_TILE, BRANCH_W), lambda i: (i, 0)),
                   pl.BlockSpec((ROW_TILE, BRANCH_W), lambda i: (rev(i), 0))],
        scratch_shapes=[pltpu.VMEM((2, N_HEADS // 2, 128, 128), F32)],
        compiler_params=_cparams(("arbitrary",)),
        name="rwkv_scan",
    )(g, add, g, add)


HY_WIDTH = 512
HY_ORDER = 2
HY_POS_BANDS = 16
HY_HIDDEN = 64
HY_TAPS_W = 2 * HY_ORDER * HY_WIDTH
FFT_N2 = ROW_TILE


def _dot3c(ah, al, b):
    bh, bl = _split(b)
    dg = functools.partial(jnp.dot, preferred_element_type=F32)
    return dg(ah, bh) + (dg(al, bh) + dg(ah, bl))


def _dotc(ah, b):
    return jnp.dot(ah, b.astype(BF16), preferred_element_type=F32)


def _split_const(m):
    m = np.asarray(m, np.float32)
    hi = m.astype(BF16)
    lo = (m - hi.astype(np.float32)).astype(BF16)
    return jnp.asarray(hi), jnp.asarray(lo)


TAPS_FLAT_COLS = 8


def _filter_taps(t_idx, length, c2pb_ref, w1t_ref, w1c_ref, w1s_ref, b1_ref, w2_ref, b2_ref, w3_ref, b3_ref, absd_ref):
    t = t_idx / float(max(length - 1, 1))
    ang = c2pb_ref[...] * t_idx / float(length)
    pre = t * w1t_ref[...] + _dot3(jnp.cos(ang), w1c_ref[...]) - _dot3(jnp.sin(ang), w1s_ref[...]) + b1_ref[...]
    hid = jnp.sin(pre)
    hid = jnp.sin(_dot3(hid, w2_ref[...]) + b2_ref[...])
    return (_dot3(hid, w3_ref[...]) + b3_ref[...]) * jnp.exp(-t * absd_ref[...])


def _hy_taps_body(*refs, length):
    taps_ref, ssq_ref, tap0_ref = refs[-3:]
    i = pl.program_id(0)
    t_idx = (i * ROW_TILE + lax.broadcasted_iota(jnp.int32, (ROW_TILE, 1), 0)).astype(F32)
    taps = _filter_taps(t_idx, length, *refs[:-3])
    taps_ref[...] = taps

    @pl.when(i == 0)
    def _():
        ssq_ref[...] = jnp.zeros(ssq_ref.shape, F32)
        tap0_ref[...] = taps[0:1]

    ssq_ref[...] += jnp.sum(taps * taps, axis=0, keepdims=True)


def _hy_taps_flat_body(c2pb_ref, w1t_ref, w1c_ref, w1s_ref, b1_ref, w2_ref, b2_ref, w3_ref, b3_ref, absd_ref, f1_ref,
                       planes_ref, ssq_ref, tap0_ref, *, length):
    j = pl.program_id(0)
    rows = length // FFT_N2
    nb = TAPS_FLAT_COLS

    def positions(width, per):
        a = lax.broadcasted_iota(jnp.int32, (rows, width), 0)
        b = lax.broadcasted_iota(jnp.int32, (rows, width), 1) // per
        return (a * FFT_N2 + j * nb + b).astype(F32)

    @pl.when(j == 0)
    def _():
        ssq_ref[...] = jnp.zeros(ssq_ref.shape, F32)

    h = HY_HIDDEN
    ang = c2pb_ref[...] * positions(nb * HY_POS_BANDS, HY_POS_BANDS) / float(length)
    t_h = positions(nb * h, h) / float(max(length - 1, 1))
    pre = t_h * w1t_ref[...] + _dot3(jnp.cos(ang), w1c_ref[...]) - _dot3(jnp.sin(ang), w1s_ref[...]) + b1_ref[...]
    hid = jnp.sin(pre)
    hid = jnp.sin(_dot3(hid, w2_ref[...]) + b2_ref[...])
    for bp in range(nb // 2):
        t_w = (positions(2 * HY_TAPS_W, HY_TAPS_W) + float(2 * bp)) / float(max(length - 1, 1))
        taps = (_dot3(hid[:, bp * 2 * h:(bp + 1) * 2 * h], w3_ref[...]) + b3_ref[...]) * jnp.exp(-t_w * absd_ref[...])
        planes = _dotc(f1_ref[...], taps)
        planes_ref[:, bp * 2 * HY_TAPS_W:(bp + 1) * 2 * HY_TAPS_W] = planes.astype(planes_ref.dtype)
        if bp == 0:
            @pl.when(j == 0)
            def _():
                tap0_ref[...] = taps[0:1, :HY_TAPS_W]
        sq = jnp.sum(taps * taps, axis=0, keepdims=True)
        ssq_ref[...] += sq[:, :HY_TAPS_W] + sq[:, HY_TAPS_W:]


def hyena_taps(length, w1, b1, w2, b2, w3, b3, *, flat):
    bands = jnp.linspace(1e-4, HY_POS_BANDS - 1, HY_POS_BANDS, dtype=F32)
    c2pb = jnp.zeros((1, 128), F32).at[0, :HY_POS_BANDS].set(2.0 * math.pi * bands)
    pad = lambda m: jnp.zeros((128, HY_HIDDEN), F32).at[:HY_POS_BANDS].set(m)
    deltas = jnp.linspace(math.log(1e-2) / 0.3, math.log(1e-2) / 1.5, HY_WIDTH, dtype=F32)
    absd = jnp.tile(jnp.abs(deltas), 2 * HY_ORDER)[None]
    const = lambda shp: pl.BlockSpec(shp, lambda i: (0,) * len(shp))
    w1t, w1c, w1s = w1[0:1], w1[1:1 + HY_POS_BANDS], w1[1 + HY_POS_BANDS:]
    if flat:
        nb = TAPS_FLAT_COLS
        f1 = _FftPlan(length).f1[0]
        rows = f1.shape[0]
        bd = lambda m, k: jnp.kron(jnp.eye(k, dtype=F32), m)
        tile = lambda v, k: jnp.tile(v, k)[None]
        operands = (tile(c2pb[0, :HY_POS_BANDS], nb), tile(w1t[0], nb), bd(w1c, nb), bd(w1s, nb), tile(b1, nb),
                    bd(w2, nb), tile(b2, nb), bd(w3, 2), tile(b3, 2), tile(absd[0], 2), f1)
        body, grid = _hy_taps_flat_body, (FFT_N2 // nb,)
        taps_shape, taps_dtype = (rows, FFT_N2 * HY_TAPS_W), BF16
        taps_spec = pl.BlockSpec((rows, nb * HY_TAPS_W), lambda i: (0, i))
    else:
        operands = (c2pb, w1t, pad(w1c), pad(w1s), b1[None], w2, b2[None], w3, b3[None], absd)
        body, grid = _hy_taps_body, (length // ROW_TILE,)
        taps_shape, taps_dtype = (length, HY_TAPS_W), F32
        taps_spec = pl.BlockSpec((ROW_TILE, HY_TAPS_W), lambda i: (i, 0))
    taps, ssq, tap0 = pl.pallas_call(
        functools.partial(body, length=length),
        out_shape=[jax.ShapeDtypeStruct(taps_shape, taps_dtype), jax.ShapeDtypeStruct((1, HY_TAPS_W), F32),
                   jax.ShapeDtypeStruct((1, HY_TAPS_W), F32)],
        grid=grid,
        in_specs=[const(op.shape) for op in operands],
        out_specs=[taps_spec, const((1, HY_TAPS_W)), const((1, HY_TAPS_W))],
        compiler_params=_cparams(("arbitrary",)),
        name="hyena_taps_flat" if flat else "hyena_taps",
    )(*operands)
    hw = HY_TAPS_W // 2
    norm2 = ssq[:, :hw] + ssq[:, hw:] + 2.0 * tap0[:, :hw] * tap0[:, hw:]
    return taps, lax.rsqrt(norm2)


class _FftPlan:
    def __init__(self, length):
        self.length = length
        self.n = 2 * length
        self.n2 = FFT_N2
        self.n1 = self.n // self.n2
        self.n1h = self.n1 // 2
        k1 = self.n1h + 1
        self.k1p = -(-k1 // 8) * 8
        kk = np.arange(self.k1p)[:, None].astype(np.float64)
        live = (kk < k1)
        nn = np.arange(self.n1h)[None, :].astype(np.float64)
        th = 2.0 * np.pi * kk * nn / self.n1
        self.f1 = _split_const(np.concatenate([np.cos(th) * live, -np.sin(th) * live], axis=0))
        ck = np.where((kk == 0) | (kk == self.n1h), 1.0, 2.0) * live / self.n
        self.g1 = _split_const(np.concatenate([np.cos(th) * ck, -np.sin(th) * ck], axis=0).T)
        m = np.arange(self.n2).astype(np.float64)
        ph = 2.0 * np.pi * np.outer(m, m) / self.n2
        c, s = np.cos(ph), np.sin(ph)
        self.fb = _split_const(np.block([[c, s], [-s, c]]))
        self.fbi = _split_const(np.block([[c, -s], [s, c]]))
        tw = 2.0 * np.pi * kk[:, :, None] * m[None, :, None] / self.n
        self.twc = jnp.asarray(np.cos(tw), F32)
        self.tws = jnp.asarray(np.sin(tw), F32)


FFT_TN = 4096


def _fft_a_body(fh_ref, x_ref, o_ref):
    o_ref[...] = _dotc(fh_ref[...], x_ref[...]).astype(o_ref.dtype)


def fft_stage_a(plan, xf, lead):
    rows_in, m = xf.shape
    rows = 2 * plan.k1p
    tn = min(FFT_TN, m)
    fh = jnp.pad(plan.f1[0], ((0, 0), (lead, 0)))
    fspec = pl.BlockSpec((rows, rows_in), lambda j: (0, 0))
    return pl.pallas_call(
        _fft_a_body,
        out_shape=jax.ShapeDtypeStruct((rows, m), BF16),
        grid=(m // tn,),
        in_specs=[fspec, pl.BlockSpec((rows_in, tn), lambda j: (0, j))],
        out_specs=pl.BlockSpec((rows, tn), lambda j: (0, j)),
        compiler_params=_cparams(("parallel",)),
        name="fft_stage_a",
    )(fh, xf)


FFT_PLANES = 2


def _twiddled(a_ref, twc_ref, tws_ref, p):
    are, aim = a_ref[0, p].astype(F32), a_ref[1, p].astype(F32)
    c, s = twc_ref[p], tws_ref[p]
    return jnp.concatenate([are * c + aim * s, aim * c - are * s], axis=0)


def _fft_filter_b_body(a_ref, twc_ref, tws_ref, fbh_ref, scale_ref, o_ref):
    n2 = FFT_N2
    hw = HY_TAPS_W // 2
    xs = [_dotc(fbh_ref[...], _twiddled(a_ref, twc_ref, tws_ref, p)) for p in range(FFT_PLANES)]
    for p, x in enumerate(xs):
        xre, xim = x[:n2], x[n2:]
        o_ref[0, p] = ((xre[:, :hw] + xre[:, hw:]) * scale_ref[...]).astype(o_ref.dtype)
        o_ref[1, p] = ((xim[:, :hw] - xim[:, hw:]) * scale_ref[...]).astype(o_ref.dtype)


def fft_filter_stage_b(plan, a, scale):
    n2, k1p = plan.n2, plan.k1p
    hw = HY_TAPS_W // 2
    const = lambda shp: pl.BlockSpec(shp, lambda k: (0,) * len(shp))
    return pl.pallas_call(
        _fft_filter_b_body,
        out_shape=jax.ShapeDtypeStruct((2, k1p, n2, hw), BF16),
        grid=(k1p // FFT_PLANES,),
        in_specs=[pl.BlockSpec((2, FFT_PLANES, n2, HY_TAPS_W), lambda k: (0, k, 0, 0)),
                  pl.BlockSpec((FFT_PLANES, n2, 1), lambda k: (k, 0, 0)),
                  pl.BlockSpec((FFT_PLANES, n2, 1), lambda k: (k, 0, 0)),
                  const((2 * n2, 2 * n2)), const((1, hw))],
        out_specs=pl.BlockSpec((2, FFT_PLANES, n2, hw), lambda k: (0, k, 0, 0)),
        compiler_params=_cparams(("parallel",)),
        name="fft_filter_stage_b",
    )(a, plan.twc, plan.tws, plan.fb[0], scale)


def _fft_conv_b_body(a_ref, kf_ref, twc_ref, tws_ref, fbh_ref, fih_ref, o_ref):
    n2 = FFT_N2
    planes = range(FFT_PLANES)
    zs = [_dotc(fbh_ref[...], _twiddled(a_ref, twc_ref, tws_ref, p)) for p in planes]
    ys = []
    for p, z in zip(planes, zs):
        zre, zim = z[:n2], z[n2:]
        kre, kim = kf_ref[0, p].astype(F32), kf_ref[1, p].astype(F32)
        ys.append(jnp.concatenate([zre * kre - zim * kim, zre * kim + zim * kre], axis=0))
    qs = [_dotc(fih_ref[...], y) for y in ys]
    for p, q in zip(planes, qs):
        qre, qim = q[:n2], q[n2:]
        c, s = twc_ref[p], tws_ref[p]
        o_ref[0, p] = (qre * c - qim * s).astype(o_ref.dtype)
        o_ref[1, p] = (qim * c + qre * s).astype(o_ref.dtype)


def fft_conv_stage_b(plan, a, kf, order):
    n2, k1p = plan.n2, plan.k1p
    const = lambda shp: pl.BlockSpec(shp, lambda k: (0,) * len(shp))
    return pl.pallas_call(
        _fft_conv_b_body,
        out_shape=jax.ShapeDtypeStruct((2, k1p, n2, HY_WIDTH), BF16),
        grid=(k1p // FFT_PLANES,),
        in_specs=[pl.BlockSpec((2, FFT_PLANES, n2, HY_WIDTH), lambda k: (0, k, 0, 0)),
                  pl.BlockSpec((2, FFT_PLANES, n2, HY_WIDTH), lambda k: (0, k, 0, order)),
                  pl.BlockSpec((FFT_PLANES, n2, 1), lambda k: (k, 0, 0)),
                  pl.BlockSpec((FFT_PLANES, n2, 1), lambda k: (k, 0, 0)),
                  const((2 * n2, 2 * n2)), const((2 * n2, 2 * n2))],
        out_specs=pl.BlockSpec((2, FFT_PLANES, n2, HY_WIDTH), lambda k: (0, k, 0, 0)),
        compiler_params=_cparams(("parallel",)),
        name="fft_conv_stage_b",
    )(a, kf, plan.twc, plan.tws, plan.fb[0], plan.fbi[0])


def _fft_inv_a_body(gh_ref, fh_ref, q_ref, z_ref, gate_ref, skip_ref, o_ref, *next_ref):
    y = _dotc(gh_ref[...], q_ref[...])
    z_next = gate_ref[...] * (y + z_ref[...] * skip_ref[...])
    o_ref[...] = z_next.astype(o_ref.dtype)
    if next_ref:
        next_ref[0][...] = _dotc(fh_ref[...], z_next).astype(next_ref[0].dtype)


def fft_inv_stage_a(plan, qf, zf, gatef, skip_t, lead, with_next):
    rows_out, m = zf.shape
    rows = 2 * plan.k1p
    tn = skip_t.shape[1]
    gh = jnp.pad(plan.g1[0], ((lead, 0), (0, 0)))
    fh = jnp.pad(plan.f1[0], ((0, 0), (lead, 0)))
    col = lambda r_: pl.BlockSpec((r_, tn), lambda j: (0, j))
    out_shape = [jax.ShapeDtypeStruct((rows_out, m), zf.dtype)]
    out_specs = [col(rows_out)]
    if with_next:
        out_shape.append(jax.ShapeDtypeStruct((rows, m), BF16))
        out_specs.append(col(rows))
    return pl.pallas_call(
        _fft_inv_a_body,
        out_shape=out_shape,
        grid=(m // tn,),
        in_specs=[pl.BlockSpec((rows_out, rows), lambda j: (0, 0)), pl.BlockSpec((rows, rows_out), lambda j: (0, 0)),
                  col(rows), col(rows_out), col(rows_out), pl.BlockSpec((1, tn), lambda j: (0, 0))],
        out_specs=out_specs,
        compiler_params=_cparams(("parallel",)),
        name="fft_inv_stage_a",
    )(gh, fh, qf, zf, gatef, skip_t)


def hyena_long(hv, hx1, hx2, ta, scale, skip, lead):
    length = hv.shape[0] - lead * FFT_N2
    plan = _FftPlan(length)
    n2, k1p = plan.n2, plan.k1p
    m = n2 * HY_WIDTH
    tn = min(FFT_TN, m)
    flat = lambda a: a.reshape(lead + plan.n1h, m)
    kf = fft_filter_stage_b(plan, ta.reshape(2, k1p, n2, HY_TAPS_W), scale)
    z = flat(hv)
    a = fft_stage_a(plan, z, lead)
    gates = (hx1, hx2)
    for o, gate in enumerate(gates):
        q = fft_conv_stage_b(plan, a.reshape(2, k1p, n2, HY_WIDTH), kf, o)
        last = o == len(gates) - 1
        out = fft_inv_stage_a(plan, q.reshape(2 * k1p, m), z, flat(gate), jnp.tile(skip[o], tn // HY_WIDTH)[None], lead,
                              not last)
        z = out[0]
        if not last:
            a = out[1]
    return z.reshape(-1, HY_WIDTH)


def _rec_out_body(x_ref, yf_ref, yb_ref, bonus_ref, sgr_ref, zh_ref, zc_ref, sgh_ref, gnw_ref, gnb_ref, bm_ref, wa_ref, wb_ref,
                  gate_ref, fin_ref, o_ref, *, final):
    y = yf_ref[...] + yb_ref[...]
    bm = bm_ref[...]
    mean = _dot_stat(y, bm)
    yc = y - mean
    var = _dot_stat(yc * yc, bm)
    yn = yc * lax.rsqrt(var + RWKV_GN_EPS) * gnw_ref[...] + gnb_ref[...]
    ya = ((yn + bonus_ref[...]) * sgr_ref[...]).astype(BF16)
    zh = zh_ref[...]
    if not final:
        zh = jnp.where(pl.program_id(0) == 0, zc_ref[...], zh)
    yh = (zh * sgh_ref[...]).astype(BF16)
    out = jnp.dot(ya, wa_ref[...], preferred_element_type=F32) + jnp.dot(yh, wb_ref[...], preferred_element_type=F32)
    xn = x_ref[...] + gate_ref[0] * out
    if final:
        xn = xn * lax.rsqrt(jnp.mean(xn * xn, axis=-1, keepdims=True) + NORM_EPS) * fin_ref[...]
    o_ref[...] = xn


RWKV_GN_EPS = 64e-5


def rec_out_proj(xs, yf, yb, bonus, sgr, zh, zc, sgh, gn_w, gn_b, w_out, gate2, final_g, *, final):
    r = xs.shape[0]
    d = D_MODEL
    off = 1 if final else 0
    nt = r // ROW_TILE - off
    rows = lambda w: pl.BlockSpec((ROW_TILE, w), lambda i: (i + off, 0))
    const = lambda shp: pl.BlockSpec(shp, lambda i: (0,) * len(shp))
    bw = BRANCH_W
    return pl.pallas_call(
        functools.partial(_rec_out_body, final=final),
        out_shape=jax.ShapeDtypeStruct((nt * ROW_TILE, d), F32),
        grid=(nt,),
        in_specs=[rows(d), rows(bw), rows(bw), rows(bw), rows(bw), rows(bw), const((CTX_LEN, bw)), rows(bw),
                  const((1, bw)), const((1, bw)),
                  const((bw, bw)), const((bw, d)), const((bw, d)),
                  pl.BlockSpec((1, 1, d), lambda i: (jnp.minimum(i + off, 1), 0, 0)), const((1, d))],
        out_specs=pl.BlockSpec((ROW_TILE, d), lambda i: (i, 0)),
        compiler_params=_cparams(("parallel",)),
        name="rec_out_proj",
    )(xs, yf, yb, bonus, sgr, zh, zc, sgh, gn_w[None], gn_b[None], _block_sum_mat(bw, 1.0 / HEAD_DIM).astype(BF16),
      w_out[:bw].astype(BF16), w_out[bw:].astype(BF16), gate2, final_g[None])


def rec_layer(xs, mods, norm_g, w_in, mu, w0, w_up, a0, a_up, k_k, k_a, r_k, gn_w, gn_b, hy_short, hy_w1, hy_b1, hy_w2,
              hy_b2, hy_w3, hy_b3, hy_skip, w_out, final_g, final):
    shift2, scale2, gate2 = mods
    rw, hv, hx1, hx2, sgr, sgh = rec_in_proj(xs, norm_g, scale2, shift2, w_in, mu, hy_short)
    g, add, bonus = rwkv_prep(rw, w0, w_up, a0, a_up, k_k, k_a, r_k)
    yf, yb = rwkv_scan(g, add)
    fargs = (hy_w1, hy_b1, hy_w2, hy_b2, hy_w3, hy_b3)
    n = xs.shape[0] - CTX_LEN
    taps, scale = hyena_taps(n, *fargs, flat=True)
    zh = hyena_long(hv, hx1, hx2, taps, scale, hy_skip, CTX_LEN // FFT_N2)
    if final:
        z_ctx = hv[:CTX_LEN]
    else:
        taps_c, scale_c = hyena_taps(CTX_LEN, *fargs, flat=False)
        z_ctx = hyena_short(hv[:CTX_LEN], hx1[:CTX_LEN], hx2[:CTX_LEN], taps_c, scale_c, hy_skip)
    return rec_out_proj(xs, yf, yb, bonus, sgr, zh, z_ctx, sgh, gn_w, gn_b, w_out, gate2, final_g, final=final)


def kernel(x, c, ctx, c_ctx, attn_norm, attn_ada_w, attn_ada_b, attn_w_in, na_rpb, gqa_q_gain, gqa_k_gain, attn_w_out,
           rec_norm, rec_ada_w, rec_ada_b, rec_w_in, rwkv_mu, rwkv_w0, rwkv_w_up, rwkv_a0, rwkv_a_up, rwkv_k_k, rwkv_k_a,
           rwkv_r_k, rwkv_gn_w, rwkv_gn_b, hy_short, hy_w1, hy_b1, hy_w2, hy_b2, hy_w3, hy_b3, hy_skip, rec_w_out,
           final_norm):
    assert x.shape[0] == 1 and ctx.shape[1] == CTX_LEN and x.shape[2] == D_MODEL
    n = x.shape[1]
    assert n % ROW_TILE == 0 and n // ROW_TILE >= 3
    assert attn_w_in.shape[0] == rec_w_in.shape[0]
    d = D_MODEL
    cond8 = jnp.zeros((8, d), F32).at[0].set(c_ctx).at[1].set(c[0])
    m_attn = adaln_all(cond8, attn_ada_w, attn_ada_b)
    m_rec = adaln_all(cond8, rec_ada_w, rec_ada_b)
    mods = lambda m, i: tuple(m[i, :2, j * d:(j + 1) * d].reshape(2, 1, d) for j in range(3))
    cos_t, sin_t = _rope_tables(n)
    xs = jnp.concatenate([ctx[0], x[0]], axis=0)
    depth = attn_w_in.shape[0] + rec_w_in.shape[0]
    for layer in range(depth):
        i = layer // 2
        final = layer == depth - 1
        if layer % 2 == 0:
            xs = attn_layer(xs, mods(m_attn, i), attn_norm[i], attn_w_in[i], na_rpb[i], gqa_q_gain[i], gqa_k_gain[i],
                            attn_w_out[i], cos_t, sin_t)
        else:
            xs = rec_layer(xs, mods(m_rec, i), rec_norm[i], rec_w_in[i], rwkv_mu[i], rwkv_w0[i], rwkv_w_up[i],
                           rwkv_a0[i], rwkv_a_up[i], rwkv_k_k[i], rwkv_k_a[i], rwkv_r_k[i], rwkv_gn_w[i], rwkv_gn_b[i],
                           hy_short[i], hy_w1[i], hy_b1[i], hy_w2[i], hy_b2[i], hy_w3[i], hy_b3[i], hy_skip[i],
                           rec_w_out[i], final_norm, final)
    return xs[None]


def _hy_short_body(fh_ref, fl_ref, gh_ref, gl_ref, v_ref, x1_ref, x2_ref, taps_ref, scale_ref, skip_ref, o_ref):
    fh, fl, gh, gl = fh_ref[...], fl_ref[...], gh_ref[...], gl_ref[...]
    kp = fh.shape[0] // 2
    hw = HY_TAPS_W // 2
    tf = _dot3c(fh, fl, taps_ref[...])
    kre = (tf[:kp, :hw] + tf[:kp, hw:]) * scale_ref[...]
    kim = (tf[kp:, :hw] - tf[kp:, hw:]) * scale_ref[...]
    z = v_ref[...].astype(F32)
    for o, gate_ref in enumerate((x1_ref, x2_ref)):
        ls = slice(o * HY_WIDTH, (o + 1) * HY_WIDTH)
        zf = _dot3c(fh, fl, z)
        zre, zim = zf[:kp], zf[kp:]
        y = jnp.concatenate([zre * kre[:, ls] - zim * kim[:, ls], zre * kim[:, ls] + zim * kre[:, ls]], axis=0)
        z = gate_ref[...] * (_dot3c(gh, gl, y) + z * skip_ref[o:o + 1])
    o_ref[...] = z


def hyena_short(hv, hx1, hx2, taps, scale, skip):
    length = hv.shape[0]
    n = 2 * length
    k1 = length + 1
    kp = -(-k1 // 8) * 8
    kk = np.arange(kp)[:, None].astype(np.float64)
    live = kk < k1
    th = 2.0 * np.pi * kk * np.arange(length)[None, :] / n
    f = _split_const(np.concatenate([np.cos(th) * live, -np.sin(th) * live], axis=0))
    ck = np.where((kk == 0) | (kk == length), 1.0, 2.0) * live / n
    g = _split_const(np.concatenate([np.cos(th) * ck, -np.sin(th) * ck], axis=0).T)
    return pl.pallas_call(
        _hy_short_body,
        out_shape=jax.ShapeDtypeStruct((length, HY_WIDTH), F32),
        compiler_params=pltpu.CompilerParams(vmem_limit_bytes=VMEM_LIMIT),
        name="hyena_short",
    )(f[0], f[1], g[0], g[1], hv, hx1, hx2, taps, scale, skip)
```

```python
import functools
import math

import jax
import jax.numpy as jnp
import numpy as np
from jax import lax
from jax.experimental import pallas as pl
from jax.experimental.pallas import tpu as pltpu

F32 = jnp.float32
BF16 = jnp.bfloat16
HIGHEST = lax.Precision.HIGHEST

D_MODEL = 1024
GRID_W = 64
CTX_LEN = 256
HEAD_DIM = 64
BRANCH_W = 512
N_HEADS = 8
GQA_KV_W = 128
NA_WIN_ROWS = 8
NA_WIN_COLS = 16
ROPE_THETA = 10000.0
ROPE_FREQS = 16
NORM_EPS = 1e-6
ROW_TILE = 256
NA_GROUP_ROWS = 4
NEG_BIG = -1e30
LOG2E = math.log2(math.e)
QK_SCALE = HEAD_DIM ** -0.5 * LOG2E
FLASH_Q_TILES = 4
FLASH_SCORES_AHEAD = 3
NA_SCORES_AHEAD = 1
VMEM_LIMIT = 56 * 1024 * 1024

ATTN_SPLITS = (512, 512, 512, 512, 512, 128, 128, 512)
GQA_HEAD_ORDER = (0, 4, 1, 5, 2, 6, 3, 7)


def _cparams(sem):
    return pltpu.CompilerParams(dimension_semantics=sem, vmem_limit_bytes=VMEM_LIMIT)


def _silu(v):
    return v * (1.0 / (1.0 + jnp.exp(-v)))


def _lane_half(shape):
    return (lax.broadcasted_iota(jnp.int32, shape, len(shape) - 1) // HEAD_DIM) % 2


def _dot_stat(a, block_mat):
    hi = a.astype(BF16)
    lo = (a - hi.astype(F32)).astype(BF16)
    return jnp.dot(hi, block_mat, preferred_element_type=F32) + jnp.dot(lo, block_mat, preferred_element_type=F32)


def _dot_nt(a, b):
    return lax.dot_general(a, b, (((1,), (1,)), ((), ())), preferred_element_type=F32)


def _adaln_body(cond_ref, w_ref, b_ref, o_ref):
    s = _silu(cond_ref[...])
    o_ref[0] = jnp.dot(s, w_ref[0], precision=HIGHEST, preferred_element_type=F32) + b_ref[0]


def adaln_all(cond8, ada_w, ada_b):
    nl = ada_w.shape[0]
    d = D_MODEL
    return pl.pallas_call(
        _adaln_body,
        out_shape=jax.ShapeDtypeStruct((nl, 8, 3 * d), F32),
        grid=(nl, 3),
        in_specs=[
            pl.BlockSpec((8, d), lambda l, j: (0, 0)),
            pl.BlockSpec((1, d, d), lambda l, j: (l, 0, j)),
            pl.BlockSpec((1, 1, d), lambda l, j: (l, 0, j)),
        ],
        out_specs=pl.BlockSpec((1, 8, d), lambda l, j: (l, 0, j)),
        compiler_params=_cparams(("parallel", "parallel")),
        name="adaln",
    )(cond8, ada_w, ada_b.reshape(nl, 1, 3 * d))


def _modulated(x_ref, g_ref, scale_ref, shift_ref):
    xf = x_ref[...]
    y = xf * lax.rsqrt(jnp.mean(xf * xf, axis=-1, keepdims=True) + NORM_EPS)
    return (y * g_ref[...]) * (1.0 + scale_ref[0]) + shift_ref[0]


def _mod_specs():
    d = D_MODEL
    return [
        pl.BlockSpec((ROW_TILE, d), lambda i: (i, 0)),
        pl.BlockSpec((1, d), lambda i: (0, 0)),
        pl.BlockSpec((1, 1, d), lambda i: (jnp.minimum(i, 1), 0, 0)),
        pl.BlockSpec((1, 1, d), lambda i: (jnp.minimum(i, 1), 0, 0)),
    ]


def _attn_in_body(x_ref, g_ref, scale_ref, shift_ref, w_ref, cos_ref, sin_ref, gq_ref, gqs_ref, gk_ref, gks_ref,
                  bdq_ref, bdk_ref,
                  qa_ref, ka_ref, va_ref, sga_ref, qb_ref, kb_ref, vb_ref, sgb_ref):
    xm = _modulated(x_ref, g_ref, scale_ref, shift_ref).astype(BF16)
    u = jnp.dot(xm, w_ref[...], preferred_element_type=F32)
    qa, ka, va, ga = u[:, 0:512], u[:, 512:1024], u[:, 1024:1536], u[:, 1536:2048]
    qb, kb, vb, gb = u[:, 2048:2560], u[:, 2560:2688], u[:, 2688:2816], u[:, 2816:3328]
    qbs, kbs = u[:, 3328:3840], u[:, 3840:3968]
    scale = QK_SCALE
    qa_ref[...] = (qa * scale).astype(BF16)
    ka_ref[...] = ka.astype(BF16)
    va_ref[...] = va.astype(BF16)
    sga_ref[...] = _silu(ga).astype(sga_ref.dtype)
    sgb_ref[...] = _silu(gb).astype(sgb_ref.dtype)
    vb_ref[...] = jnp.transpose(vb).astype(BF16)
    cos_k, sin_k = cos_ref[...], sin_ref[...]
    cos_q = jnp.concatenate([cos_k] * 4, axis=1)
    sin_q = jnp.concatenate([sin_k] * 4, axis=1)
    rs_q = lax.rsqrt(_dot_stat(qb * qb, bdq_ref[...]) + NORM_EPS)
    rs_k = lax.rsqrt(_dot_stat(kb * kb, bdk_ref[...]) + NORM_EPS)
    qr = rs_q * (qb * gq_ref[...] * cos_q + qbs * gqs_ref[...] * sin_q)
    kr = rs_k * (kb * gk_ref[...] * cos_k + kbs * gks_ref[...] * sin_k)
    qb_ref[...] = jnp.transpose(qr * scale).astype(BF16)
    kb_ref[...] = kr.astype(BF16)


def _rope_tables(n):
    t = jnp.arange(n, dtype=jnp.int32)
    pos = jnp.stack([t // GRID_W, t % GRID_W], axis=-1).astype(F32)
    inv_freq = ROPE_THETA ** (-jnp.arange(ROPE_FREQS, dtype=F32) / ROPE_FREQS)
    ang = pos[:, :, None] * inv_freq
    c, s = jnp.cos(ang), jnp.sin(ang)
    cos64 = jnp.concatenate([c[:, 0], c[:, 0], c[:, 1], c[:, 1]], axis=-1)
    sin64 = jnp.concatenate([-s[:, 0], s[:, 0], -s[:, 1], s[:, 1]], axis=-1)
    cos64 = jnp.concatenate([jnp.ones((CTX_LEN, HEAD_DIM), F32), cos64], axis=0)
    sin64 = jnp.concatenate([jnp.zeros((CTX_LEN, HEAD_DIM), F32), sin64], axis=0)
    return jnp.tile(cos64, (1, 2)), jnp.tile(sin64, (1, 2))


def _reorder_heads(w, order, axis):
    take = lambda h: lax.slice_in_dim(w, h * HEAD_DIM, (h + 1) * HEAD_DIM, axis=axis)
    return jnp.concatenate([take(h) for h in order], axis=axis)


def _swap_rope_halves(w):
    shp = w.shape
    return jnp.flip(w.reshape(shp[:-1] + (shp[-1] // (2 * ROPE_FREQS), 2, ROPE_FREQS)), axis=-2).reshape(shp)


def attn_in_proj(xs, norm_g, scale2, shift2, w_in, q_gain, k_gain, cos_t, sin_t):
    r = xs.shape[0]
    d = D_MODEL
    parts, start = [], 0
    for s in ATTN_SPLITS:
        parts.append(w_in[:, start:start + s])
        start += s
    wqa, wka, wva, wga, wqb, wkb, wvb, wgb = parts
    wqb_p = _reorder_heads(wqb, GQA_HEAD_ORDER, 1)
    wgb_p = _reorder_heads(wgb, GQA_HEAD_ORDER, 1)
    wqb_sw = _swap_rope_halves(wqb_p)
    wkb_sw = _swap_rope_halves(wkb)
    w_ext = jnp.concatenate([wqa, wka, wva, wga, wqb_p, wkb, wvb, wgb_p, wqb_sw, wkb_sw], axis=1).astype(BF16)
    gq = jnp.tile(q_gain, N_HEADS)[None]
    gqs = jnp.tile(_swap_rope_halves(q_gain), N_HEADS)[None]
    gk = jnp.tile(k_gain, 2)[None]
    gks = jnp.tile(_swap_rope_halves(k_gain), 2)[None]
    bdq = jnp.asarray(np.kron(np.eye(N_HEADS), np.full((HEAD_DIM, HEAD_DIM), 1.0 / HEAD_DIM)), BF16)
    bdk = jnp.asarray(np.kron(np.eye(2), np.full((HEAD_DIM, HEAD_DIM), 1.0 / HEAD_DIM)), BF16)
    wcols = w_ext.shape[1]
    const = lambda shp: pl.BlockSpec(shp, lambda i: (0,) * len(shp))
    rows = lambda w: pl.BlockSpec((ROW_TILE, w), lambda i: (i, 0))
    out_shapes = [
        jax.ShapeDtypeStruct((r, 512), BF16), jax.ShapeDtypeStruct((r, 512), BF16), jax.ShapeDtypeStruct((r, 512), BF16),
        jax.ShapeDtypeStruct((r, 512), BF16),
        jax.ShapeDtypeStruct((512, r), BF16), jax.ShapeDtypeStruct((r, 128), BF16), jax.ShapeDtypeStruct((128, r), BF16),
        jax.ShapeDtypeStruct((r, 512), BF16),
    ]
    cols = lambda w: pl.BlockSpec((w, ROW_TILE), lambda i: (0, i))
    return pl.pallas_call(
        _attn_in_body,
        out_shape=out_shapes,
        grid=(r // ROW_TILE,),
        in_specs=_mod_specs() + [const((d, wcols)), rows(128), rows(128), const((1, 512)), const((1, 512)),
                                 const((1, 128)), const((1, 128)), const((512, 512)), const((128, 128))],
        out_specs=[rows(512), rows(512), rows(512), rows(512), cols(512), rows(128), cols(128), rows(512)],
        compiler_params=_cparams(("parallel",)),
        name="attn_in_proj",
    )(xs, norm_g[None], scale2, shift2, w_ext, cos_t, sin_t, gq, gqs, gk, gks, bdq, bdk)


def _na_cols_body(rpb_ref, sel_ref, neg_ref, o_ref):
    o_ref[...] = jnp.dot(rpb_ref[...], sel_ref[...], precision=HIGHEST, preferred_element_type=F32) + neg_ref[...]


def _na_bias_tables(rpb, rows):
    nrel_r, nrel_c = 2 * NA_WIN_ROWS - 1, 2 * NA_WIN_COLS - 1
    qc = np.arange(GRID_W)[:, None]
    kc = np.arange(GRID_W)[None, :]
    col0 = np.clip(qc - NA_WIN_COLS // 2, 0, GRID_W - NA_WIN_COLS)
    col_ok = (kc >= col0) & (kc < col0 + NA_WIN_COLS)
    rc = kc - qc + NA_WIN_COLS - 1
    sel = np.zeros((128, GRID_W * GRID_W), np.float32)
    sel[np.where(col_ok, rc, 127).reshape(-1), np.arange(GRID_W * GRID_W)] = col_ok.reshape(-1)
    neg = np.where(col_ok, 0.0, NEG_BIG).astype(np.float32).reshape(1, -1)
    rpb2 = jnp.zeros((128, 128), F32).at[:N_HEADS * nrel_r, :nrel_c].set(rpb.reshape(N_HEADS * nrel_r, nrel_c))
    cols = pl.pallas_call(
        _na_cols_body,
        out_shape=jax.ShapeDtypeStruct((128, GRID_W * GRID_W), F32),
        name="na_bias_cols",
    )(rpb2, jnp.asarray(sel), jnp.asarray(neg))
    cols = cols[:N_HEADS * nrel_r].reshape(N_HEADS, nrel_r, GRID_W, GRID_W)
    kh = min(NA_WIN_ROWS, rows)
    g = rows // NA_GROUP_ROWS
    cases = [(0, 0), (NA_GROUP_ROWS, 0), (rows - NA_GROUP_ROWS, NA_GROUP_ROWS * (g - 3))]
    masked = jnp.full((N_HEADS, GRID_W, GRID_W), NEG_BIG, F32)
    tabs = []
    for qr_first, start in cases:
        blocks = []
        for j in range(NA_GROUP_ROWS):
            qr = qr_first + j
            row0 = min(max(qr - kh // 2, 0), rows - kh)
            for i in range(3 * NA_GROUP_ROWS):
                kr = start + i
                blocks.append(cols[:, kr - qr + NA_WIN_ROWS - 1] if row0 <= kr < row0 + kh else masked)
        tab = jnp.stack(blocks, axis=1).reshape(N_HEADS, NA_GROUP_ROWS, 3 * NA_GROUP_ROWS, GRID_W, GRID_W)
        tabs.append(tab.transpose(0, 1, 3, 2, 4).reshape(N_HEADS, ROW_TILE, 3 * ROW_TILE))
    return jnp.stack(tabs) * LOG2E


def _na_body(q_ref, kc_ref, k0_ref, k1_ref, k2_ref, vc_ref, v0_ref, v1_ref, v2_ref, bias_ref, sg_ref, o_ref):
    half = _lane_half((ROW_TILE, 128))
    k_refs = (k0_ref, k1_ref, k2_ref, kc_ref)
    v_refs = (v0_ref, v1_ref, v2_ref, vc_ref)

    def scores(h):
        hp, j = divmod(h, 2)
        ls = slice(hp * 128, (hp + 1) * 128)
        qp = q_ref[:, ls]
        qm = jnp.where(half == j, qp, jnp.zeros_like(qp))
        return [_dot_nt(qm, r[:, ls]) for r in k_refs]

    pending = [scores(h) for h in range(NA_SCORES_AHEAD)]
    outs = []
    for h in range(N_HEADS):
        hp, j = divmod(h, 2)
        ls = slice(hp * 128, (hp + 1) * 128)
        s = pending.pop(0)
        if h + NA_SCORES_AHEAD < N_HEADS:
            pending.append(scores(h + NA_SCORES_AHEAD))
        s_win = jnp.concatenate(s[:3], axis=1) + bias_ref[0, h]
        s_ctx = s[3]
        m = jnp.maximum(jnp.max(s_win, axis=1, keepdims=True), jnp.max(s_ctx, axis=1, keepdims=True))
        p_win = jnp.exp2(s_win - m).astype(BF16)
        p_ctx = jnp.exp2(s_ctx - m).astype(BF16)
        vms = [jnp.where(half == j, r[:, ls], jnp.ones((ROW_TILE, 128), BF16)) for r in v_refs]
        o = jnp.dot(p_ctx, vms[3], preferred_element_type=F32)
        for b in range(3):
            o += jnp.dot(p_win[:, b * ROW_TILE:(b + 1) * ROW_TILE], vms[b], preferred_element_type=F32)
        outs.append(o / pltpu.roll(o, HEAD_DIM, 1))
        if j == 1:
            o_pair = jnp.where(half == 0, outs[h - 1], outs[h])
            o_ref[:, ls] = (o_pair * sg_ref[:, ls]).astype(BF16)


def na_attention(qa, ka, va, sga, bias_tabs, n):
    g = n // ROW_TILE
    w = BRANCH_W

    def kv_spec(off):
        return pl.BlockSpec((ROW_TILE, w), lambda i: (jnp.clip(i - 1, 0, g - 3) + off + 1, 0))

    ctx_spec = pl.BlockSpec((ROW_TILE, w), lambda i: (0, 0))
    q_spec = pl.BlockSpec((ROW_TILE, w), lambda i: (i + 1, 0))
    case = lambda i: jnp.where(i == 0, 0, jnp.where(i == g - 1, 2, 1))
    bias_spec = pl.BlockSpec((1, N_HEADS, ROW_TILE, 3 * ROW_TILE), lambda i: (case(i), 0, 0, 0))
    return pl.pallas_call(
        _na_body,
        out_shape=jax.ShapeDtypeStruct((n, w), BF16),
        grid=(g,),
        in_specs=[q_spec, ctx_spec, kv_spec(0), kv_spec(1), kv_spec(2), ctx_spec, kv_spec(0), kv_spec(1), kv_spec(2),
                  bias_spec, q_spec],
        out_specs=pl.BlockSpec((ROW_TILE, w), lambda i: (i, 0)),
        compiler_params=_cparams(("parallel",)),
        name="na_attention",
    )(qa, ka, ka, ka, ka, va, va, va, va, bias_tabs, sga)


def _flash_mha_body(q_ref, k_ref, v_ref, sg_ref, o_ref, m_ref, acc_ref):
    kv = pl.program_id(1)
    tq = q_ref.shape[0]

    @pl.when(kv == 0)
    def _():
        m_ref[...] = jnp.full(m_ref.shape, NEG_BIG, F32)
        acc_ref[...] = jnp.zeros(acc_ref.shape, F32)

    khalf = _lane_half((k_ref.shape[0], 128))
    for p in range(N_HEADS // 2):
        ls = slice(p * 128, (p + 1) * 128)
        qp = q_ref[:, ls]
        kp = k_ref[:, ls]
        vp = v_ref[:, ls]
        for j in range(2):
            hh = 2 * p + j
            km = jnp.where(khalf == j, kp, jnp.zeros_like(kp))
            vm = jnp.where(khalf == j, vp, jnp.ones_like(vp))
            s = _dot_nt(qp, km)
            m_prev = m_ref[hh]
            m_new = jnp.maximum(m_prev, jnp.max(s, axis=1, keepdims=True))
            alpha = jnp.exp2(m_prev - m_new)
            pr = jnp.exp2(s - m_new[:, :1]).astype(BF16)
            acc_ref[hh] = alpha * acc_ref[hh] + jnp.dot(pr, vm, preferred_element_type=F32)
            m_ref[hh] = m_new

    @pl.when(kv == pl.num_programs(1) - 1)
    def _():
        half = _lane_half((tq, 128))
        for p in range(N_HEADS // 2):
            ls = slice(p * 128, (p + 1) * 128)
            a0, a1 = acc_ref[2 * p], acc_ref[2 * p + 1]
            o0 = a0 / pltpu.roll(a0, HEAD_DIM, 1)
            o1 = a1 / pltpu.roll(a1, HEAD_DIM, 1)
            o_ref[:, ls] = (jnp.where(half == 0, o0, o1) * sg_ref[:, ls]).astype(BF16)


def flash_mha(q, k, v, sg, *, q_block0, nq, tk, nk):
    tq = ROW_TILE
    return pl.pallas_call(
        _flash_mha_body,
        out_shape=jax.ShapeDtypeStruct((nq * tq, BRANCH_W), BF16),
        grid=(nq, nk),
        in_specs=[
            pl.BlockSpec((tq, BRANCH_W), lambda i, j: (i + q_block0, 0)),
            pl.BlockSpec((tk, BRANCH_W), lambda i, j: (j, 0)),
            pl.BlockSpec((tk, BRANCH_W), lambda i, j: (j, 0)),
            pl.BlockSpec((tq, BRANCH_W), lambda i, j: (i + q_block0, 0)),
        ],
        out_specs=pl.BlockSpec((tq, BRANCH_W), lambda i, j: (i, 0)),
        scratch_shapes=[pltpu.VMEM((N_HEADS, tq, 128), F32)] * 2,
        compiler_params=_cparams(("parallel", "arbitrary")),
        name="flash_mha",
    )(q, k, v, sg)


def _flash_gqa_body(*refs, nsub):
    qt_refs, (k_ref, vt_ref), sg_refs = refs[:nsub], refs[nsub:nsub + 2], refs[nsub + 2:2 * nsub + 2]
    o_ref, m_ref, acc_ref = refs[2 * nsub + 2:]
    kv = pl.program_id(1)
    tq = qt_refs[0].shape[1]
    tk = k_ref.shape[0]

    @pl.when(kv == 0)
    def _():
        m_ref[...] = jnp.full(m_ref.shape, NEG_BIG, F32)
        acc_ref[...] = jnp.zeros(acc_ref.shape, F32)

    khalf = _lane_half((tk, 128))
    vhalf = lax.broadcasted_iota(jnp.int32, (128, tk), 0) // HEAD_DIM
    kb = k_ref[...]
    vt = vt_ref[...]
    kms = [jnp.where(khalf == j, kb, jnp.zeros_like(kb)) for j in range(2)]
    vms = [jnp.where(vhalf == j, vt, jnp.ones_like(vt)) for j in range(2)]
    units = nsub * N_HEADS

    def scores(u):
        sub, hh = divmod(u, N_HEADS)
        p, j = divmod(hh, 2)
        return jnp.dot(kms[j], qt_refs[sub][p * 128:(p + 1) * 128, :], preferred_element_type=F32).astype(BF16)

    pending = [scores(u) for u in range(FLASH_SCORES_AHEAD)]
    for u in range(units):
        st = pending.pop(0)
        if u + FLASH_SCORES_AHEAD < units:
            pending.append(scores(u + FLASH_SCORES_AHEAD))
        m_prev = m_ref[u]
        m_new = jnp.maximum(m_prev, jnp.max(st, axis=0, keepdims=True).astype(F32))
        alpha = jnp.exp2(m_prev - m_new)
        pt = jnp.exp2(st - m_new[0:1].astype(BF16))
        acc_ref[u] = alpha[0:1] * acc_ref[u] + jnp.dot(vms[u % 2], pt, preferred_element_type=F32)
        m_ref[u] = m_new

    @pl.when(kv == pl.num_programs(1) - 1)
    def _():
        for sub in range(nsub):
            for p in range(N_HEADS // 2):
                ls = slice(p * 128, (p + 1) * 128)
                a0, a1 = acc_ref[sub * N_HEADS + 2 * p], acc_ref[sub * N_HEADS + 2 * p + 1]
                ot = jnp.concatenate([a0[:HEAD_DIM] / a0[HEAD_DIM:], a1[HEAD_DIM:] / a1[:HEAD_DIM]], axis=0)
                o_ref[sub * tq:(sub + 1) * tq, ls] = (jnp.transpose(ot) * sg_refs[sub][:, ls]).astype(BF16)


def flash_gqa(qt, k, vt, sg, *, q_block0, nq, tk, nk, nsub):
    tq = ROW_TILE
    assert nq % nsub == 0
    qt_specs = [pl.BlockSpec((BRANCH_W, tq), lambda i, j, s=s: (0, nsub * i + s + q_block0)) for s in range(nsub)]
    sg_specs = [pl.BlockSpec((tq, BRANCH_W), lambda i, j, s=s: (nsub * i + s + q_block0, 0)) for s in range(nsub)]
    return pl.pallas_call(
        functools.partial(_flash_gqa_body, nsub=nsub),
        out_shape=jax.ShapeDtypeStruct((nq * tq, BRANCH_W), BF16),
        grid=(nq // nsub, nk),
        in_specs=qt_specs + [pl.BlockSpec((tk, GQA_KV_W), lambda i, j: (j, 0)),
                             pl.BlockSpec((GQA_KV_W, tk), lambda i, j: (0, j))] + sg_specs,
        out_specs=pl.BlockSpec((nsub * tq, BRANCH_W), lambda i, j: (i, 0)),
        scratch_shapes=[pltpu.VMEM((nsub * N_HEADS, 8, tq), F32), pltpu.VMEM((nsub * N_HEADS, 128, tq), F32)],
        compiler_params=_cparams(("parallel", "arbitrary")),
        name="flash_gqa",
    )(*([qt] * nsub), k, vt, *([sg] * nsub))


def _out_body(x_ref, ya_ref, yac_ref, yb_ref, ybc_ref, wa_ref, wb_ref, gate_ref, o_ref):
    is_ctx = pl.program_id(0) == 0
    ya = jnp.where(is_ctx, yac_ref[...], ya_ref[...])
    yb = jnp.where(is_ctx, ybc_ref[...], yb_ref[...])
    y = jnp.dot(ya, wa_ref[...], preferred_element_type=F32)
    y += jnp.dot(yb, wb_ref[...], preferred_element_type=F32)
    o_ref[...] = x_ref[...] + gate_ref[0] * y


def out_proj(xs, ya_lat, ya_ctx, yb_lat, yb_ctx, wa, wb, gate2):
    r = xs.shape[0]
    d = D_MODEL
    rows = lambda w: pl.BlockSpec((ROW_TILE, w), lambda i: (i, 0))
    lat = pl.BlockSpec((ROW_TILE, BRANCH_W), lambda i: (jnp.maximum(i - 1, 0), 0))
    const = lambda shp: pl.BlockSpec(shp, lambda i: (0,) * len(shp))
    ctx = const((CTX_LEN, BRANCH_W))
    return pl.pallas_call(
        _out_body,
        out_shape=jax.ShapeDtypeStruct((r, d), F32),
        grid=(r // ROW_TILE,),
        in_specs=[rows(d), lat, ctx, lat, ctx, const((BRANCH_W, d)), const((BRANCH_W, d)),
                  pl.BlockSpec((1, 1, d), lambda i: (jnp.minimum(i, 1), 0, 0))],
        out_specs=rows(d),
        compiler_params=_cparams(("parallel",)),
        name="out_proj",
    )(xs, ya_lat, ya_ctx, yb_lat, yb_ctx, wa.astype(BF16), wb.astype(BF16), gate2)


KV_TILE_MAX_BLOCKS = 13


def _kv_tile(r):
    nb = r // ROW_TILE
    best = max(k for k in range(1, KV_TILE_MAX_BLOCKS + 1) if nb % k == 0)
    return best * ROW_TILE, nb // best


def attn_layer(xs, mods, norm_g, w_in, rpb, q_gain, k_gain, w_out, cos_t, sin_t):
    r = xs.shape[0]
    n = r - CTX_LEN
    shift2, scale2, gate2 = mods
    qa, ka, va, sga, qbt, kb, vbt, sgb = attn_in_proj(xs, norm_g, scale2, shift2, w_in, q_gain, k_gain, cos_t, sin_t)
    bias_tabs = _na_bias_tables(rpb, n // GRID_W)
    ya_lat = na_attention(qa, ka, va, sga, bias_tabs, n)
    ya_ctx = flash_mha(qa, ka, va, sga, q_block0=0, nq=1, tk=CTX_LEN, nk=1)
    tk, nk = _kv_tile(r)
    nq = n // ROW_TILE
    yb_lat = flash_gqa(qbt, kb, vbt, sgb, q_block0=1, nq=nq, tk=tk, nk=nk, nsub=FLASH_Q_TILES if nq % FLASH_Q_TILES == 0 else 1)
    yb_ctx = flash_gqa(qbt, kb, vbt, sgb, q_block0=0, nq=1, tk=CTX_LEN, nk=1, nsub=1)
    wb = _reorder_heads(w_out[BRANCH_W:], GQA_HEAD_ORDER, 0)
    return out_proj(xs, ya_lat, ya_ctx, yb_lat, yb_ctx, w_out[:BRANCH_W], wb, gate2)


def _split(a):
    hi = a.astype(BF16)
    return hi, (a - hi.astype(F32)).astype(BF16)


def _dot3(a, b, dims=(((1,), (0,)), ((), ()))):
    ah, al = _split(a)
    bh, bl = _split(b)
    dg = functools.partial(lax.dot_general, dimension_numbers=dims, preferred_element_type=F32)
    return dg(ah, bh) + (dg(al, bh) + dg(ah, bl))


def _dot1(a, b, dims=(((1,), (0,)), ((), ()))):
    return lax.dot_general(a.astype(BF16), b.astype(BF16), dims, preferred_element_type=F32)


_NT = (((1,), (1,)), ((), ()))
_TN = (((0,), (0,)), ((), ()))


RWKV_SHIFT_W = 1664
HY_IN_W = 1536
HALO = 8
REC_HALO_W = RWKV_SHIFT_W + HY_IN_W


def _rec_in_body(x_ref, xp_ref, xn_ref, g_ref, scale_ref, shift_ref, w_ref, mu_ref, taps_ref,
                 rw_ref, hv_ref, hx1_ref, hx2_ref, sgr_ref, sgh_ref, u_scr):
    i = pl.program_id(0)
    nt = pl.num_programs(0)
    xe = jnp.concatenate([xp_ref[...], x_ref[...], xn_ref[...]], axis=0)
    y = xe * lax.rsqrt(jnp.mean(xe * xe, axis=-1, keepdims=True) + NORM_EPS)
    xm = ((y * g_ref[...]) * (1.0 + scale_ref[0]) + shift_ref[0]).astype(BF16)
    u = jnp.dot(xm, w_ref[...], preferred_element_type=F32)
    row = lax.broadcasted_iota(jnp.int32, (ROW_TILE + 2 * HALO, 1), 0)
    keep = jnp.logical_and(jnp.logical_or(row >= HALO, i >= 2),
                           jnp.logical_or(row < ROW_TILE + HALO, jnp.logical_and(i >= 1, i < nt - 1)))
    u_scr[...] = jnp.where(keep, u[:, :REC_HALO_W], 0.0)
    up = u_scr[pl.ds(HALO - 1, ROW_TILE), :]
    uc = u_scr[pl.ds(HALO, ROW_TILE), :]
    un = u_scr[pl.ds(HALO + 1, ROW_TILE), :]
    w = RWKV_SHIFT_W
    rw_c = uc[:, :w]
    rw_ref[...] = rw_c + (0.5 * (up[:, :w] + un[:, :w]) - rw_c) * mu_ref[...]
    hy = up[:, w:] * taps_ref[0:1] + uc[:, w:] * taps_ref[1:2] + un[:, w:] * taps_ref[2:3]
    hv_ref[...] = hy[:, 0:512].astype(hv_ref.dtype)
    hx1_ref[...] = hy[:, 512:1024].astype(hx1_ref.dtype)
    hx2_ref[...] = hy[:, 1024:1536].astype(hx2_ref.dtype)
    uc_all = u[HALO:HALO + ROW_TILE]
    sgr_ref[...] = _silu(uc_all[:, REC_HALO_W:REC_HALO_W + 512]).astype(sgr_ref.dtype)
    sgh_ref[...] = _silu(uc_all[:, REC_HALO_W + 512:REC_HALO_W + 1024]).astype(sgh_ref.dtype)


def rec_in_proj(xs, norm_g, scale2, shift2, w_in, mu, hy_short):
    r = xs.shape[0]
    d = D_MODEL
    w = RWKV_SHIFT_W
    w_ext = jnp.concatenate([w_in[:, :w], w_in[:, w + 512:w + 512 + HY_IN_W], w_in[:, w:w + 512],
                             w_in[:, w + 512 + HY_IN_W:]], axis=1).astype(BF16)
    nh = r // HALO
    per = ROW_TILE // HALO
    const = lambda shp: pl.BlockSpec(shp, lambda i: (0,) * len(shp))
    rows = lambda wd: pl.BlockSpec((ROW_TILE, wd), lambda i: (i, 0))
    f = lambda wd: jax.ShapeDtypeStruct((r, wd), F32)
    h = jax.ShapeDtypeStruct((r, 512), BF16)
    mod = _mod_specs()
    return pl.pallas_call(
        _rec_in_body,
        out_shape=[f(w), h, h, h, h, h],
        grid=(r // ROW_TILE,),
        in_specs=[mod[0],
                  pl.BlockSpec((HALO, d), lambda i: (jnp.maximum(i * per - 1, 0), 0)),
                  pl.BlockSpec((HALO, d), lambda i: (jnp.minimum((i + 1) * per, nh - 1), 0)),
                  mod[1], mod[2], mod[3], const((d, w_ext.shape[1])), const((1, w)), const((3, HY_IN_W))],
        out_specs=[rows(w), rows(512), rows(512), rows(512), rows(512), rows(512)],
        scratch_shapes=[pltpu.VMEM((ROW_TILE + 2 * HALO, REC_HALO_W), F32)],
        compiler_params=_cparams(("parallel",)),
        name="rec_in_proj",
    )(xs, xs, xs, norm_g[None], scale2, shift2, w_ext, mu[None], hy_short)


CHUNK = 64
CPT = ROW_TILE // CHUNK


def _block_sum_mat(width, value):
    return jnp.asarray(np.kron(np.eye(width // HEAD_DIM), np.full((HEAD_DIM, HEAD_DIM), value)), F32)


PREP_PAIRS = 4


def _rwkv_prep_body(r_ref, k_ref, v_ref, lora_ref, w0_ref, wup_ref, a0_ref, aup_ref, kk_ref, ka_ref, rk_ref,
                    tri_ref, bs_ref, g_ref, add_ref, bonus_ref):
    t = ROW_TILE
    lora = lora_ref[...]
    bs = bs_ref[...]
    row = lax.broadcasted_iota(jnp.int32, (t, t), 0)
    col = lax.broadcasted_iota(jnp.int32, (t, t), 1)
    same = (row // CHUNK) == (col // CHUNK)
    eye = (row == col).astype(F32)
    wl_all = _dot3(jnp.tanh(lora), wup_ref[...])
    al_all = _dot3(lora, aup_ref[...])
    half = _lane_half((t, 128))
    half_c = _lane_half((HEAD_DIM, 128))
    rowc = lax.broadcasted_iota(jnp.int32, (HEAD_DIM, 128), 0)
    lanec = lax.broadcasted_iota(jnp.int32, (HEAD_DIM, 128), 1)
    level_masks = []
    bsz = 2
    while bsz < CHUNK:
        level_masks.append(jnp.logical_and((row // (2 * bsz)) == (col // (2 * bsz)), (row // bsz) != (col // bsz)))
        bsz *= 2
    first_mask = (row // 2) == (col // 2)

    groups = []
    for q in range(PREP_PAIRS):
        ls = slice(q * 128, (q + 1) * 128)
        r, k, v = r_ref[:, ls], k_ref[:, ls], v_ref[:, ls]
        kk = k * kk_ref[:, ls]
        kk = kk * lax.rsqrt(_dot_stat(kk * kk, bs) + 1e-12)
        kd_sum = None
        for d in range(2):
            ds = slice(q * 256 + d * 128, q * 256 + (d + 1) * 128)
            wl = w0_ref[d, :, ls] + wl_all[:, ds]
            z = -wl
            w_log = -(jnp.maximum(z, 0.0) + jnp.log(1.0 + jnp.exp(-jnp.abs(z)))) - 0.5
            lw = -jnp.exp(w_log)
            a = 1.0 / (1.0 + jnp.exp(-(a0_ref[d, :, ls] + al_all[:, ds])))
            kd = k * (1.0 + (a - 1.0) * ka_ref[:, ls])
            b = kk * a
            kd_sum = kd if kd_sum is None else kd_sum + kd
            incl = jnp.logical_and(same, (col <= row) if d == 0 else (col >= row))
            lw_hi, lw_lo = _split(lw)
            tri = tri_ref[d]
            cs = jnp.dot(tri, lw_hi, preferred_element_type=F32) + jnp.dot(tri, lw_lo, preferred_element_type=F32)
            ends = [c * CHUNK + (CHUNK - 1 if d == 0 else 0) for c in range(CPT)]
            tot = jnp.concatenate([jnp.broadcast_to(cs[e:e + 1], (CHUNK, 128)) for e in ends], axis=0)
            w_inv = jnp.exp(-cs)
            w_rest = jnp.exp(tot - cs)
            groups.append(dict(q=q, d=d, v=v, incl=incl, strict=jnp.logical_and(incl, row != col), tot=tot,
                               kkt=kk * jnp.exp(cs - lw), kh=kd * w_inv, bh=b * w_inv, rt=r * jnp.exp(cs),
                               kdd=kd * w_rest, bdd=b * w_rest))
        bonus_ref[:, ls] = 0.5 * _dot_stat(r * kd_sum * rk_ref[:, ls], bs) * v
    chains = [(gg, j) for gg in groups for j in range(2)]
    sels = [half == j for _, j in chains]
    bms = [jnp.where(sel, gg["bh"], 0.0) for (gg, _), sel in zip(chains, sels)]
    kms = [jnp.where(sel, gg["kh"], 0.0) for (gg, _), sel in zip(chains, sels)]
    l_bs = [jnp.where(gg["strict"], _dot1(gg["kkt"], bm, _NT), 0.0) for (gg, _), bm in zip(chains, bms)]
    tinvs = [eye - jnp.where(first_mask, l_b, 0.0) for l_b in l_bs]
    for mask in level_masks:
        xs = [_dot1(jnp.where(mask, l_b, 0.0), tinv) for l_b, tinv in zip(l_bs, tinvs)]
        tinvs = [tinv - _dot1(tinv, x) for tinv, x in zip(tinvs, xs)]
    l_ks = [jnp.where(gg["strict"], _dot1(gg["kkt"], km, _NT), 0.0) for (gg, _), km in zip(chains, kms)]
    a_rks = [jnp.where(gg["incl"], _dot1(gg["rt"], km, _NT), 0.0) for (gg, _), km in zip(chains, kms)]
    a_rbs = [jnp.where(gg["incl"], _dot1(gg["rt"], bm, _NT), 0.0) for (gg, _), bm in zip(chains, bms)]
    lvs = [_dot1(l_k, gg["v"]) for (gg, _), l_k in zip(chains, l_ks)]
    pus = [_dot1(tinv, jnp.concatenate([gg["kkt"], lv], axis=1)) for (gg, _), tinv, lv in zip(chains, tinvs, lvs)]
    cors = [_dot1(a_rb, pu) for a_rb, pu in zip(a_rbs, pus)]
    ps = [pu[:, :128] for pu in pus]
    u0s = [pu[:, 128:] for pu in pus]
    qs = [gg["rt"] - cor[:, :128] for (gg, _), cor in zip(chains, cors)]
    y0s = [_dot1(a_rk, gg["v"]) - cor[:, 128:] for (gg, _), a_rk, cor in zip(chains, a_rks, cors)]

    sel0 = half == 0
    for n, gg in enumerate(groups):
        d, v = gg["d"], gg["v"]
        ls = slice(gg["q"] * 128, (gg["q"] + 1) * 128)
        p, u0, q, y0 = (jnp.where(sel0, x[2 * n], x[2 * n + 1]) for x in (ps, u0s, qs, y0s))
        for c in range(CPT):
            rs = slice(c * CHUNK, (c + 1) * CHUNK)
            x1 = _dot1(gg["bdd"][rs], p[rs], _TN)
            x2 = _dot1(gg["kdd"][rs], v[rs], _TN) - _dot1(gg["bdd"][rs], u0[rs], _TN)
            m_pair = jnp.where(half_c == 0, x1[:HEAD_DIM], x1[HEAD_DIM:])
            n_pair = jnp.where(half_c == 0, x2[:HEAD_DIM], x2[HEAD_DIM:])
            wc = jnp.exp(gg["tot"][c * CHUNK:c * CHUNK + 1])
            dg = jnp.where((lanec % HEAD_DIM) == rowc, wc, 0.0)
            g_ref[c, d, 0:HEAD_DIM, ls] = dg - m_pair
            g_ref[c, d, HEAD_DIM:, ls] = q[rs]
            add_ref[c, d, 0:HEAD_DIM, ls] = n_pair
            add_ref[c, d, HEAD_DIM:, ls] = y0[rs]


def _lora_ext(up, first_row):
    out = jnp.zeros((2, 128, BRANCH_W), F32)
    for d in range(2):
        out = out.at[d, first_row + 32 * d:first_row + 32 * (d + 1)].set(up[d])
    return out


def rwkv_prep(rw, w0, w_up, a0, a_up, k_k, k_a, r_k):
    r = rw.shape[0]
    nt = r // ROW_TILE
    nch = r // CHUNK
    t = ROW_TILE
    ii = np.arange(t)
    same = (ii[:, None] // CHUNK) == (ii[None, :] // CHUNK)
    tri = jnp.asarray(np.stack([same & (ii[None, :] <= ii[:, None]), same & (ii[None, :] >= ii[:, None])]), BF16)
    pair_cat = lambda w: w.reshape(2, 128, N_HEADS // 2, 128).transpose(1, 2, 0, 3).reshape(128, 2 * BRANCH_W)
    w = 128 * PREP_PAIRS
    steps = N_HEADS // 2 // PREP_PAIRS
    lane = lambda blk: pl.BlockSpec((t, w), lambda i, p, blk=blk: (i, blk + p))
    pvec = pl.BlockSpec((1, w), lambda i, p: (0, p))
    dvec = pl.BlockSpec((2, 1, w), lambda i, p: (0, 0, p))
    dmat = pl.BlockSpec((128, 2 * w), lambda i, p: (0, p))
    gspec = pl.BlockSpec((CPT, 2, HEAD_DIM + CHUNK, w), lambda i, p: (i, 0, 0, p))
    gshape = jax.ShapeDtypeStruct((nch, 2, HEAD_DIM + CHUNK, BRANCH_W), F32)
    return pl.pallas_call(
        _rwkv_prep_body,
        out_shape=[gshape, gshape, jax.ShapeDtypeStruct((r, BRANCH_W), F32)],
        grid=(nt, steps),
        in_specs=[lane(0), lane(steps), lane(2 * steps), pl.BlockSpec((t, 128), lambda i, p: (i, 12)),
                  dvec, dmat, dvec, dmat, pvec, pvec, pvec,
                  pl.BlockSpec((2, t, t), lambda i, p: (0, 0, 0)),
                  pl.BlockSpec((128, 128), lambda i, p: (0, 0))],
        out_specs=[gspec, gspec, pl.BlockSpec((t, w), lambda i, p: (i, p))],
        compiler_params=_cparams(("parallel", "parallel")),
        name="rwkv_prep",
    )(rw, rw, rw, rw, w0.reshape(2, 1, BRANCH_W), pair_cat(_lora_ext(w_up, 0)), a0.reshape(2, 1, BRANCH_W),
      pair_cat(_lora_ext(a_up, 64)), k_k[None], k_a[None], r_k.reshape(1, BRANCH_W), tri,
      _block_sum_mat(128, 1.0).astype(BF16))


def _rwkv_scan_body(gf_ref, af_ref, gb_ref, ab_ref, yf_ref, yb_ref, st_ref):
    @pl.when(pl.program_id(0) == 0)
    def _():
        st_ref[...] = jnp.zeros(st_ref.shape, F32)

    rowh = lax.broadcasted_iota(jnp.int32, (128, 128), 0) // HEAD_DIM
    diag = rowh == _lane_half((128, 128))
    for s in range(CPT):
        for d, (g_ref, a_ref, y_ref, c) in enumerate(((gf_ref, af_ref, yf_ref, s), (gb_ref, ab_ref, yb_ref, CPT - 1 - s))):
            for p in range(N_HEADS // 2):
                ls = slice(p * 128, (p + 1) * 128)
                out = _dot3(g_ref[c, 0, :, ls], st_ref[d, p]) + a_ref[c, 0, :, ls]
                hn = out[:HEAD_DIM]
                st_ref[d, p] = jnp.where(diag, jnp.concatenate([hn, hn], axis=0), 0.0)
                y_ref[c * CHUNK:(c + 1) * CHUNK, ls] = out[HEAD_DIM:]


def rwkv_scan(g, add):
    nch = g.shape[0]
    r = nch * CHUNK
    nt = r // ROW_TILE
    assert CTX_LEN == ROW_TILE
    rev = lambda i: jnp.where(i == 0, 0, nt - i)
    blk = (CPT, 1, HEAD_DIM + CHUNK, BRANCH_W)
    fwd = pl.BlockSpec(blk, lambda i: (i, 0, 0, 0))
    bwd = pl.BlockSpec(blk, lambda i: (rev(i), 1, 0, 0))
    yshape = jax.ShapeDtypeStruct((r, BRANCH_W), F32)
    return pl.pallas_call(
        _rwkv_scan_body,
        out_shape=[yshape, yshape],
        grid=(nt,),
        in_specs=[fwd, fwd, bwd, bwd],
        out_specs=[pl.BlockSpec((ROW_TILE, BRANCH_W), lambda i: (i, 0)),
                   pl.BlockSpec((ROW_TILE, BRANCH_W), lambda i: (rev(i), 0))],
        scratch_shapes=[pltpu.VMEM((2, N_HEADS // 2, 128, 128), F32)],
        compiler_params=_cparams(("arbitrary",)),
        name="rwkv_scan",
    )(g, add, g, add)


HY_WIDTH = 512
HY_ORDER = 2
HY_POS_BANDS = 16
HY_HIDDEN = 64
HY_TAPS_W = 2 * HY_ORDER * HY_WIDTH
FFT_N2 = ROW_TILE


def _dot3c(ah, al, b):
    bh, bl = _split(b)
    dg = functools.partial(jnp.dot, preferred_element_type=F32)
    return dg(ah, bh) + (dg(al, bh) + dg(ah, bl))


def _dotc(ah, b):
    return jnp.dot(ah, b.astype(BF16), preferred_element_type=F32)


def _split_const(m):
    m = np.asarray(m, np.float32)
    hi = m.astype(BF16)
    lo = (m - hi.astype(np.float32)).astype(BF16)
    return jnp.asarray(hi), jnp.asarray(lo)


TAPS_FLAT_COLS = 8


def _filter_taps(t_idx, length, c2pb_ref, w1t_ref, w1c_ref, w1s_ref, b1_ref, w2_ref, b2_ref, w3_ref, b3_ref, absd_ref):
    t = t_idx / float(max(length - 1, 1))
    ang = c2pb_ref[...] * t_idx / float(length)
    pre = t * w1t_ref[...] + _dot3(jnp.cos(ang), w1c_ref[...]) - _dot3(jnp.sin(ang), w1s_ref[...]) + b1_ref[...]
    hid = jnp.sin(pre)
    hid = jnp.sin(_dot3(hid, w2_ref[...]) + b2_ref[...])
    return (_dot3(hid, w3_ref[...]) + b3_ref[...]) * jnp.exp(-t * absd_ref[...])


def _hy_taps_body(*refs, length):
    taps_ref, ssq_ref, tap0_ref = refs[-3:]
    i = pl.program_id(0)
    t_idx = (i * ROW_TILE + lax.broadcasted_iota(jnp.int32, (ROW_TILE, 1), 0)).astype(F32)
    taps = _filter_taps(t_idx, length, *refs[:-3])
    taps_ref[...] = taps

    @pl.when(i == 0)
    def _():
        ssq_ref[...] = jnp.zeros(ssq_ref.shape, F32)
        tap0_ref[...] = taps[0:1]

    ssq_ref[...] += jnp.sum(taps * taps, axis=0, keepdims=True)


def _hy_taps_flat_body(c2pb_ref, w1t_ref, w1c_ref, w1s_ref, b1_ref, w2_ref, b2_ref, w3_ref, b3_ref, absd_ref, f1_ref,
                       planes_ref, ssq_ref, tap0_ref, *, length):
    j = pl.program_id(0)
    rows = length // FFT_N2
    nb = TAPS_FLAT_COLS

    def positions(width, per):
        a = lax.broadcasted_iota(jnp.int32, (rows, width), 0)
        b = lax.broadcasted_iota(jnp.int32, (rows, width), 1) // per
        return (a * FFT_N2 + j * nb + b).astype(F32)

    @pl.when(j == 0)
    def _():
        ssq_ref[...] = jnp.zeros(ssq_ref.shape, F32)

    h = HY_HIDDEN
    ang = c2pb_ref[...] * positions(nb * HY_POS_BANDS, HY_POS_BANDS) / float(length)
    t_h = positions(nb * h, h) / float(max(length - 1, 1))
    pre = t_h * w1t_ref[...] + _dot3(jnp.cos(ang), w1c_ref[...]) - _dot3(jnp.sin(ang), w1s_ref[...]) + b1_ref[...]
    hid = jnp.sin(pre)
    hid = jnp.sin(_dot3(hid, w2_ref[...]) + b2_ref[...])
    for bp in range(nb // 2):
        t_w = (positions(2 * HY_TAPS_W, HY_TAPS_W) + float(2 * bp)) / float(max(length - 1, 1))
        taps = (_dot3(hid[:, bp * 2 * h:(bp + 1) * 2 * h], w3_ref[...]) + b3_ref[...]) * jnp.exp(-t_w * absd_ref[...])
        planes = _dotc(f1_ref[...], taps)
        planes_ref[:, bp * 2 * HY_TAPS_W:(bp + 1) * 2 * HY_TAPS_W] = planes.astype(planes_ref.dtype)
        if bp == 0:
            @pl.when(j == 0)
            def _():
                tap0_ref[...] = taps[0:1, :HY_TAPS_W]
        sq = jnp.sum(taps * taps, axis=0, keepdims=True)
        ssq_ref[...] += sq[:, :HY_TAPS_W] + sq[:, HY_TAPS_W:]


def hyena_taps(length, w1, b1, w2, b2, w3, b3, *, flat):
    bands = jnp.linspace(1e-4, HY_POS_BANDS - 1, HY_POS_BANDS, dtype=F32)
    c2pb = jnp.zeros((1, 128), F32).at[0, :HY_POS_BANDS].set(2.0 * math.pi * bands)
    pad = lambda m: jnp.zeros((128, HY_HIDDEN), F32).at[:HY_POS_BANDS].set(m)
    deltas = jnp.linspace(math.log(1e-2) / 0.3, math.log(1e-2) / 1.5, HY_WIDTH, dtype=F32)
    absd = jnp.tile(jnp.abs(deltas), 2 * HY_ORDER)[None]
    const = lambda shp: pl.BlockSpec(shp, lambda i: (0,) * len(shp))
    w1t, w1c, w1s = w1[0:1], w1[1:1 + HY_POS_BANDS], w1[1 + HY_POS_BANDS:]
    if flat:
        nb = TAPS_FLAT_COLS
        f1 = _FftPlan(length).f1[0]
        rows = f1.shape[0]
        bd = lambda m, k: jnp.kron(jnp.eye(k, dtype=F32), m)
        tile = lambda v, k: jnp.tile(v, k)[None]
        operands = (tile(c2pb[0, :HY_POS_BANDS], nb), tile(w1t[0], nb), bd(w1c, nb), bd(w1s, nb), tile(b1, nb),
                    bd(w2, nb), tile(b2, nb), bd(w3, 2), tile(b3, 2), tile(absd[0], 2), f1)
        body, grid = _hy_taps_flat_body, (FFT_N2 // nb,)
        taps_shape, taps_dtype = (rows, FFT_N2 * HY_TAPS_W), BF16
        taps_spec = pl.BlockSpec((rows, nb * HY_TAPS_W), lambda i: (0, i))
    else:
        operands = (c2pb, w1t, pad(w1c), pad(w1s), b1[None], w2, b2[None], w3, b3[None], absd)
        body, grid = _hy_taps_body, (length // ROW_TILE,)
        taps_shape, taps_dtype = (length, HY_TAPS_W), F32
        taps_spec = pl.BlockSpec((ROW_TILE, HY_TAPS_W), lambda i: (i, 0))
    taps, ssq, tap0 = pl.pallas_call(
        functools.partial(body, length=length),
        out_shape=[jax.ShapeDtypeStruct(taps_shape, taps_dtype), jax.ShapeDtypeStruct((1, HY_TAPS_W), F32),
                   jax.ShapeDtypeStruct((1, HY_TAPS_W), F32)],
        grid=grid,
        in_specs=[const(op.shape) for op in operands],
        out_specs=[taps_spec, const((1, HY_TAPS_W)), const((1, HY_TAPS_W))],
        compiler_params=_cparams(("arbitrary",)),
        name="hyena_taps_flat" if flat else "hyena_taps",
    )(*operands)
    hw = HY_TAPS_W // 2
    norm2 = ssq[:, :hw] + ssq[:, hw:] + 2.0 * tap0[:, :hw] * tap0[:, hw:]
    return taps, lax.rsqrt(norm2)


class _FftPlan:
    def __init__(self, length):
        self.length = length
        self.n = 2 * length
        self.n2 = FFT_N2
        self.n1 = self.n // self.n2
        self.n1h = self.n1 // 2
        k1 = self.n1h + 1
        self.k1p = -(-k1 // 8) * 8
        kk = np.arange(self.k1p)[:, None].astype(np.float64)
        live = (kk < k1)
        nn = np.arange(self.n1h)[None, :].astype(np.float64)
        th = 2.0 * np.pi * kk * nn / self.n1
        self.f1 = _split_const(np.concatenate([np.cos(th) * live, -np.sin(th) * live], axis=0))
        ck = np.where((kk == 0) | (kk == self.n1h), 1.0, 2.0) * live / self.n
        self.g1 = _split_const(np.concatenate([np.cos(th) * ck, -np.sin(th) * ck], axis=0).T)
        m = np.arange(self.n2).astype(np.float64)
        ph = 2.0 * np.pi * np.outer(m, m) / self.n2
        c, s = np.cos(ph), np.sin(ph)
        self.fb = _split_const(np.block([[c, s], [-s, c]]))
        self.fbi = _split_const(np.block([[c, -s], [s, c]]))
        tw = 2.0 * np.pi * kk[:, :, None] * m[None, :, None] / self.n
        self.twc = jnp.asarray(np.cos(tw), F32)
        self.tws = jnp.asarray(np.sin(tw), F32)


FFT_TN = 4096


def _fft_a_body(fh_ref, x_ref, o_ref):
    o_ref[...] = _dotc(fh_ref[...], x_ref[...]).astype(o_ref.dtype)


def fft_stage_a(plan, xf, lead):
    rows_in, m = xf.shape
    rows = 2 * plan.k1p
    tn = min(FFT_TN, m)
    fh = jnp.pad(plan.f1[0], ((0, 0), (lead, 0)))
    fspec = pl.BlockSpec((rows, rows_in), lambda j: (0, 0))
    return pl.pallas_call(
        _fft_a_body,
        out_shape=jax.ShapeDtypeStruct((rows, m), BF16),
        grid=(m // tn,),
        in_specs=[fspec, pl.BlockSpec((rows_in, tn), lambda j: (0, j))],
        out_specs=pl.BlockSpec((rows, tn), lambda j: (0, j)),
        compiler_params=_cparams(("parallel",)),
        name="fft_stage_a",
    )(fh, xf)


FFT_PLANES = 2


def _twiddled(a_ref, twc_ref, tws_ref, p):
    are, aim = a_ref[0, p].astype(F32), a_ref[1, p].astype(F32)
    c, s = twc_ref[p], tws_ref[p]
    return jnp.concatenate([are * c + aim * s, aim * c - are * s], axis=0)


def _fft_filter_b_body(a_ref, twc_ref, tws_ref, fbh_ref, scale_ref, o_ref):
    n2 = FFT_N2
    hw = HY_TAPS_W // 2
    xs = [_dotc(fbh_ref[...], _twiddled(a_ref, twc_ref, tws_ref, p)) for p in range(FFT_PLANES)]
    for p, x in enumerate(xs):
        xre, xim = x[:n2], x[n2:]
        o_ref[0, p] = ((xre[:, :hw] + xre[:, hw:]) * scale_ref[...]).astype(o_ref.dtype)
        o_ref[1, p] = ((xim[:, :hw] - xim[:, hw:]) * scale_ref[...]).astype(o_ref.dtype)


def fft_filter_stage_b(plan, a, scale):
    n2, k1p = plan.n2, plan.k1p
    hw = HY_TAPS_W // 2
    const = lambda shp: pl.BlockSpec(shp, lambda k: (0,) * len(shp))
    return pl.pallas_call(
        _fft_filter_b_body,
        out_shape=jax.ShapeDtypeStruct((2, k1p, n2, hw), BF16),
        grid=(k1p // FFT_PLANES,),
        in_specs=[pl.BlockSpec((2, FFT_PLANES, n2, HY_TAPS_W), lambda k: (0, k, 0, 0)),
                  pl.BlockSpec((FFT_PLANES, n2, 1), lambda k: (k, 0, 0)),
                  pl.BlockSpec((FFT_PLANES, n2, 1), lambda k: (k, 0, 0)),
                  const((2 * n2, 2 * n2)), const((1, hw))],
        out_specs=pl.BlockSpec((2, FFT_PLANES, n2, hw), lambda k: (0, k, 0, 0)),
        compiler_params=_cparams(("parallel",)),
        name="fft_filter_stage_b",
    )(a, plan.twc, plan.tws, plan.fb[0], scale)


def _fft_conv_b_body(a_ref, kf_ref, twc_ref, tws_ref, fbh_ref, fih_ref, o_ref):
    n2 = FFT_N2
    planes = range(FFT_PLANES)
    zs = [_dotc(fbh_ref[...], _twiddled(a_ref, twc_ref, tws_ref, p)) for p in planes]
    ys = []
    for p, z in zip(planes, zs):
        zre, zim = z[:n2], z[n2:]
        kre, kim = kf_ref[0, p].astype(F32), kf_ref[1, p].astype(F32)
        ys.append(jnp.concatenate([zre * kre - zim * kim, zre * kim + zim * kre], axis=0))
    qs = [_dotc(fih_ref[...], y) for y in ys]
    for p, q in zip(planes, qs):
        qre, qim = q[:n2], q[n2:]
        c, s = twc_ref[p], tws_ref[p]
        o_ref[0, p] = (qre * c - qim * s).astype(o_ref.dtype)
        o_ref[1, p] = (qim * c + qre * s).astype(o_ref.dtype)


def fft_conv_stage_b(plan, a, kf, order):
    n2, k1p = plan.n2, plan.k1p
    const = lambda shp: pl.BlockSpec(shp, lambda k: (0,) * len(shp))
    return pl.pallas_call(
        _fft_conv_b_body,
        out_shape=jax.ShapeDtypeStruct((2, k1p, n2, HY_WIDTH), BF16),
        grid=(k1p // FFT_PLANES,),
        in_specs=[pl.BlockSpec((2, FFT_PLANES, n2, HY_WIDTH), lambda k: (0, k, 0, 0)),
                  pl.BlockSpec((2, FFT_PLANES, n2, HY_WIDTH), lambda k: (0, k, 0, order)),
                  pl.BlockSpec((FFT_PLANES, n2, 1), lambda k: (k, 0, 0)),
                  pl.BlockSpec((FFT_PLANES, n2, 1), lambda k: (k, 0, 0)),
                  const((2 * n2, 2 * n2)), const((2 * n2, 2 * n2))],
        out_specs=pl.BlockSpec((2, FFT_PLANES, n2, HY_WIDTH), lambda k: (0, k, 0, 0)),
        compiler_params=_cparams(("parallel",)),
        name="fft_conv_stage_b",
    )(a, kf, plan.twc, plan.tws, plan.fb[0], plan.fbi[0])


def _fft_inv_a_body(gh_ref, fh_ref, q_ref, z_ref, gate_ref, skip_ref, o_ref, *next_ref):
    y = _dotc(gh_ref[...], q_ref[...])
    z_next = gate_ref[...] * (y + z_ref[...] * skip_ref[...])
    o_ref[...] = z_next.astype(o_ref.dtype)
    if next_ref:
        next_ref[0][...] = _dotc(fh_ref[...], z_next).astype(next_ref[0].dtype)


def fft_inv_stage_a(plan, qf, zf, gatef, skip_t, lead, with_next):
    rows_out, m = zf.shape
    rows = 2 * plan.k1p
    tn = skip_t.shape[1]
    gh = jnp.pad(plan.g1[0], ((lead, 0), (0, 0)))
    fh = jnp.pad(plan.f1[0], ((0, 0), (lead, 0)))
    col = lambda r_: pl.BlockSpec((r_, tn), lambda j: (0, j))
    out_shape = [jax.ShapeDtypeStruct((rows_out, m), zf.dtype)]
    out_specs = [col(rows_out)]
    if with_next:
        out_shape.append(jax.ShapeDtypeStruct((rows, m), BF16))
        out_specs.append(col(rows))
    return pl.pallas_call(
        _fft_inv_a_body,
        out_shape=out_shape,
        grid=(m // tn,),
        in_specs=[pl.BlockSpec((rows_out, rows), lambda j: (0, 0)), pl.BlockSpec((rows, rows_out), lambda j: (0, 0)),
                  col(rows), col(rows_out), col(rows_out), pl.BlockSpec((1, tn), lambda j: (0, 0))],
        out_specs=out_specs,
        compiler_params=_cparams(("parallel",)),
        name="fft_inv_stage_a",
    )(gh, fh, qf, zf, gatef, skip_t)


def hyena_long(hv, hx1, hx2, ta, scale, skip, lead):
    length = hv.shape[0] - lead * FFT_N2
    plan = _FftPlan(length)
    n2, k1p = plan.n2, plan.k1p
    m = n2 * HY_WIDTH
    tn = min(FFT_TN, m)
    flat = lambda a: a.reshape(lead + plan.n1h, m)
    kf = fft_filter_stage_b(plan, ta.reshape(2, k1p, n2, HY_TAPS_W), scale)
    z = flat(hv)
    a = fft_stage_a(plan, z, lead)
    gates = (hx1, hx2)
    for o, gate in enumerate(gates):
        q = fft_conv_stage_b(plan, a.reshape(2, k1p, n2, HY_WIDTH), kf, o)
        last = o == len(gates) - 1
        out = fft_inv_stage_a(plan, q.reshape(2 * k1p, m), z, flat(gate), jnp.tile(skip[o], tn // HY_WIDTH)[None], lead,
                              not last)
        z = out[0]
        if not last:
            a = out[1]
    return z.reshape(-1, HY_WIDTH)


def _rec_out_body(x_ref, yf_ref, yb_ref, bonus_ref, sgr_ref, zh_ref, zc_ref, sgh_ref, gnw_ref, gnb_ref, bm_ref, wa_ref, wb_ref,
                  gate_ref, fin_ref, o_ref, *, final):
    y = yf_ref[...] + yb_ref[...]
    bm = bm_ref[...]
    mean = _dot_stat(y, bm)
    yc = y - mean
    var = _dot_stat(yc * yc, bm)
    yn = yc * lax.rsqrt(var + RWKV_GN_EPS) * gnw_ref[...] + gnb_ref[...]
    ya = ((yn + bonus_ref[...]) * sgr_ref[...]).astype(BF16)
    zh = zh_ref[...]
    if not final:
        zh = jnp.where(pl.program_id(0) == 0, zc_ref[...], zh)
    yh = (zh * sgh_ref[...]).astype(BF16)
    out = jnp.dot(ya, wa_ref[...], preferred_element_type=F32) + jnp.dot(yh, wb_ref[...], preferred_element_type=F32)
    xn = x_ref[...] + gate_ref[0] * out
    if final:
        xn = xn * lax.rsqrt(jnp.mean(xn * xn, axis=-1, keepdims=True) + NORM_EPS) * fin_ref[...]
    o_ref[...] = xn


RWKV_GN_EPS = 64e-5


def rec_out_proj(xs, yf, yb, bonus, sgr, zh, zc, sgh, gn_w, gn_b, w_out, gate2, final_g, *, final):
    r = xs.shape[0]
    d = D_MODEL
    off = 1 if final else 0
    nt = r // ROW_TILE - off
    rows = lambda w: pl.BlockSpec((ROW_TILE, w), lambda i: (i + off, 0))
    const = lambda shp: pl.BlockSpec(shp, lambda i: (0,) * len(shp))
    bw = BRANCH_W
    return pl.pallas_call(
        functools.partial(_rec_out_body, final=final),
        out_shape=jax.ShapeDtypeStruct((nt * ROW_TILE, d), F32),
        grid=(nt,),
        in_specs=[rows(d), rows(bw), rows(bw), rows(bw), rows(bw), rows(bw), const((CTX_LEN, bw)), rows(bw),
                  const((1, bw)), const((1, bw)),
                  const((bw, bw)), const((bw, d)), const((bw, d)),
                  pl.BlockSpec((1, 1, d), lambda i: (jnp.minimum(i + off, 1), 0, 0)), const((1, d))],
        out_specs=pl.BlockSpec((ROW_TILE, d), lambda i: (i, 0)),
        compiler_params=_cparams(("parallel",)),
        name="rec_out_proj",
    )(xs, yf, yb, bonus, sgr, zh, zc, sgh, gn_w[None], gn_b[None], _block_sum_mat(bw, 1.0 / HEAD_DIM).astype(BF16),
      w_out[:bw].astype(BF16), w_out[bw:].astype(BF16), gate2, final_g[None])


def rec_layer(xs, mods, norm_g, w_in, mu, w0, w_up, a0, a_up, k_k, k_a, r_k, gn_w, gn_b, hy_short, hy_w1, hy_b1, hy_w2,
              hy_b2, hy_w3, hy_b3, hy_skip, w_out, final_g, final):
    shift2, scale2, gate2 = mods
    rw, hv, hx1, hx2, sgr, sgh = rec_in_proj(xs, norm_g, scale2, shift2, w_in, mu, hy_short)
    g, add, bonus = rwkv_prep(rw, w0, w_up, a0, a_up, k_k, k_a, r_k)
    yf, yb = rwkv_scan(g, add)
    fargs = (hy_w1, hy_b1, hy_w2, hy_b2, hy_w3, hy_b3)
    n = xs.shape[0] - CTX_LEN
    taps, scale = hyena_taps(n, *fargs, flat=True)
    zh = hyena_long(hv, hx1, hx2, taps, scale, hy_skip, CTX_LEN // FFT_N2)
    if final:
        z_ctx = hv[:CTX_LEN]
    else:
        taps_c, scale_c = hyena_taps(CTX_LEN, *fargs, flat=False)
        z_ctx = hyena_short(hv[:CTX_LEN], hx1[:CTX_LEN], hx2[:CTX_LEN], taps_c, scale_c, hy_skip)
    return rec_out_proj(xs, yf, yb, bonus, sgr, zh, z_ctx, sgh, gn_w, gn_b, w_out, gate2, final_g, final=final)


def kernel(x, c, ctx, c_ctx, attn_norm, attn_ada_w, attn_ada_b, attn_w_in, na_rpb, gqa_q_gain, gqa_k_gain, attn_w_out,
           rec_norm, rec_ada_w, rec_ada_b, rec_w_in, rwkv_mu, rwkv_w0, rwkv_w_up, rwkv_a0, rwkv_a_up, rwkv_k_k, rwkv_k_a,
           rwkv_r_k, rwkv_gn_w, rwkv_gn_b, hy_short, hy_w1, hy_b1, hy_w2, hy_b2, hy_w3, hy_b3, hy_skip, rec_w_out,
           final_norm):
    assert x.shape[0] == 1 and ctx.shape[1] == CTX_LEN and x.shape[2] == D_MODEL
    n = x.shape[1]
    assert n % ROW_TILE == 0 and n // ROW_TILE >= 3
    assert attn_w_in.shape[0] == rec_w_in.shape[0]
    d = D_MODEL
    cond8 = jnp.zeros((8, d), F32).at[0].set(c_ctx).at[1].set(c[0])
    m_attn = adaln_all(cond8, attn_ada_w, attn_ada_b)
    m_rec = adaln_all(cond8, rec_ada_w, rec_ada_b)
    mods = lambda m, i: tuple(m[i, :2, j * d:(j + 1) * d].reshape(2, 1, d) for j in range(3))
    cos_t, sin_t = _rope_tables(n)
    xs = jnp.concatenate([ctx[0], x[0]], axis=0)
    depth = attn_w_in.shape[0] + rec_w_in.shape[0]
    for layer in range(depth):
        i = layer // 2
        final = layer == depth - 1
        if layer % 2 == 0:
            xs = attn_layer(xs, mods(m_attn, i), attn_norm[i], attn_w_in[i], na_rpb[i], gqa_q_gain[i], gqa_k_gain[i],
                            attn_w_out[i], cos_t, sin_t)
        else:
            xs = rec_layer(xs, mods(m_rec, i), rec_norm[i], rec_w_in[i], rwkv_mu[i], rwkv_w0[i], rwkv_w_up[i],
                           rwkv_a0[i], rwkv_a_up[i], rwkv_k_k[i], rwkv_k_a[i], rwkv_r_k[i], rwkv_gn_w[i], rwkv_gn_b[i],
                           hy_short[i], hy_w1[i], hy_b1[i], hy_w2[i], hy_b2[i], hy_w3[i], hy_b3[i], hy_skip[i],
                           rec_w_out[i], final_norm, final)
    return xs[None]


def _hy_short_body(fh_ref, fl_ref, gh_ref, gl_ref, v_ref, x1_ref, x2_ref, taps_ref, scale_ref, skip_ref, o_ref):
    fh, fl, gh, gl = fh_ref[...], fl_ref[...], gh_ref[...], gl_ref[...]
    kp = fh.shape[0] // 2
    hw = HY_TAPS_W // 2
    tf = _dot3c(fh, fl, taps_ref[...])
    kre = (tf[:kp, :hw] + tf[:kp, hw:]) * scale_ref[...]
    kim = (tf[kp:, :hw] - tf[kp:, hw:]) * scale_ref[...]
    z = v_ref[...].astype(F32)
    for o, gate_ref in enumerate((x1_ref, x2_ref)):
        ls = slice(o * HY_WIDTH, (o + 1) * HY_WIDTH)
        zf = _dot3c(fh, fl, z)
        zre, zim = zf[:kp], zf[kp:]
        y = jnp.concatenate([zre * kre[:, ls] - zim * kim[:, ls], zre * kim[:, ls] + zim * kre[:, ls]], axis=0)
        z = gate_ref[...] * (_dot3c(gh, gl, y) + z * skip_ref[o:o + 1])
    o_ref[...] = z


def hyena_short(hv, hx1, hx2, taps, scale, skip):
    length = hv.shape[0]
    n = 2 * length
    k1 = length + 1
    kp = -(-k1 // 8) * 8
    kk = np.arange(kp)[:, None].astype(np.float64)
    live = kk < k1
    th = 2.0 * np.pi * kk * np.arange(length)[None, :] / n
    f = _split_const(np.concatenate([np.cos(th) * live, -np.sin(th) * live], axis=0))
    ck = np.where((kk == 0) | (kk == length), 1.0, 2.0) * live / n
    g = _split_const(np.concatenate([np.cos(th) * ck, -np.sin(th) * ck], axis=0).T)
    return pl.pallas_call(
        _hy_short_body,
        out_shape=jax.ShapeDtypeStruct((length, HY_WIDTH), F32),
        compiler_params=pltpu.CompilerParams(vmem_limit_bytes=VMEM_LIMIT),
        name="hyena_short",
    )(f[0], f[1], g[0], g[1], hv, hx1, hx2, taps, scale, skip)
```

```python
import functools
import math

import jax
import jax.numpy as jnp
import numpy as np
from jax import lax
from jax.experimental import pallas as pl
from jax.experimental.pallas import tpu as pltpu

F32 = jnp.float32
BF16 = jnp.bfloat16
HIGHEST = lax.Precision.HIGHEST

D_MODEL = 1024
GRID_W = 64
CTX_LEN = 256
HEAD_DIM = 64
BRANCH_W = 512
N_HEADS = 8
GQA_KV_W = 128
NA_WIN_ROWS = 8
NA_WIN_COLS = 16
ROPE_THETA = 10000.0
ROPE_FREQS = 16
NORM_EPS = 1e-6
ROW_TILE = 256
NA_GROUP_ROWS = 4
NEG_BIG = -1e30
LOG2E = math.log2(math.e)
QK_SCALE = HEAD_DIM ** -0.5 * LOG2E
FLASH_Q_TILES = 4
FLASH_SCORES_AHEAD = 3
NA_SCORES_AHEAD = 1
VMEM_LIMIT = 56 * 1024 * 1024

ATTN_SPLITS = (512, 512, 512, 512, 512, 128, 128, 512)
GQA_HEAD_ORDER = (0, 4, 1, 5, 2, 6, 3, 7)


def _cparams(sem):
    return pltpu.CompilerParams(dimension_semantics=sem, vmem_limit_bytes=VMEM_LIMIT)


def _silu(v):
    return v * (1.0 / (1.0 + jnp.exp(-v)))


def _lane_half(shape):
    return (lax.broadcasted_iota(jnp.int32, shape, len(shape) - 1) // HEAD_DIM) % 2


def _dot_stat(a, block_mat):
    hi = a.astype(BF16)
    lo = (a - hi.astype(F32)).astype(BF16)
    return jnp.dot(hi, block_mat, preferred_element_type=F32) + jnp.dot(lo, block_mat, preferred_element_type=F32)


def _dot_nt(a, b):
    return lax.dot_general(a, b, (((1,), (1,)), ((), ())), preferred_element_type=F32)


def _adaln_body(cond_ref, w_ref, b_ref, o_ref):
    s = _silu(cond_ref[...])
    o_ref[0] = jnp.dot(s, w_ref[0], precision=HIGHEST, preferred_element_type=F32) + b_ref[0]


def adaln_all(cond8, ada_w, ada_b):
    nl = ada_w.shape[0]
    d = D_MODEL
    return pl.pallas_call(
        _adaln_body,
        out_shape=jax.ShapeDtypeStruct((nl, 8, 3 * d), F32),
        grid=(nl, 3),
        in_specs=[
            pl.BlockSpec((8, d), lambda l, j: (0, 0)),
            pl.BlockSpec((1, d, d), lambda l, j: (l, 0, j)),
            pl.BlockSpec((1, 1, d), lambda l, j: (l, 0, j)),
        ],
        out_specs=pl.BlockSpec((1, 8, d), lambda l, j: (l, 0, j)),
        compiler_params=_cparams(("parallel", "parallel")),
        name="adaln",
    )(cond8, ada_w, ada_b.reshape(nl, 1, 3 * d))


def _modulated(x_ref, g_ref, scale_ref, shift_ref):
    xf = x_ref[...]
    y = xf * lax.rsqrt(jnp.mean(xf * xf, axis=-1, keepdims=True) + NORM_EPS)
    return (y * g_ref[...]) * (1.0 + scale_ref[0]) + shift_ref[0]


def _mod_specs():
    d = D_MODEL
    return [
        pl.BlockSpec((ROW_TILE, d), lambda i: (i, 0)),
        pl.BlockSpec((1, d), lambda i: (0, 0)),
        pl.BlockSpec((1, 1, d), lambda i: (jnp.minimum(i, 1), 0, 0)),
        pl.BlockSpec((1, 1, d), lambda i: (jnp.minimum(i, 1), 0, 0)),
    ]


def _attn_in_body(x_ref, g_ref, scale_ref, shift_ref, w_ref, cos_ref, sin_ref, gq_ref, gqs_ref, gk_ref, gks_ref,
                  bdq_ref, bdk_ref,
                  qa_ref, ka_ref, va_ref, sga_ref, qb_ref, kb_ref, vb_ref, sgb_ref):
    xm = _modulated(x_ref, g_ref, scale_ref, shift_ref).astype(BF16)
    u = jnp.dot(xm, w_ref[...], preferred_element_type=F32)
    qa, ka, va, ga = u[:, 0:512], u[:, 512:1024], u[:, 1024:1536], u[:, 1536:2048]
    qb, kb, vb, gb = u[:, 2048:2560], u[:, 2560:2688], u[:, 2688:2816], u[:, 2816:3328]
    qbs, kbs = u[:, 3328:3840], u[:, 3840:3968]
    scale = QK_SCALE
    qa_ref[...] = (qa * scale).astype(BF16)
    ka_ref[...] = ka.astype(BF16)
    va_ref[...] = va.astype(BF16)
    sga_ref[...] = _silu(ga).astype(sga_ref.dtype)
    sgb_ref[...] = _silu(gb).astype(sgb_ref.dtype)
    vb_ref[...] = jnp.transpose(vb).astype(BF16)
    cos_k, sin_k = cos_ref[...], sin_ref[...]
    cos_q = jnp.concatenate([cos_k] * 4, axis=1)
    sin_q = jnp.concatenate([sin_k] * 4, axis=1)
    rs_q = lax.rsqrt(_dot_stat(qb * qb, bdq_ref[...]) + NORM_EPS)
    rs_k = lax.rsqrt(_dot_stat(kb * kb, bdk_ref[...]) + NORM_EPS)
    qr = rs_q * (qb * gq_ref[...] * cos_q + qbs * gqs_ref[...] * sin_q)
    kr = rs_k * (kb * gk_ref[...] * cos_k + kbs * gks_ref[...] * sin_k)
    qb_ref[...] = jnp.transpose(qr * scale).astype(BF16)
    kb_ref[...] = kr.astype(BF16)


def _rope_tables(n):
    t = jnp.arange(n, dtype=jnp.int32)
    pos = jnp.stack([t // GRID_W, t % GRID_W], axis=-1).astype(F32)
    inv_freq = ROPE_THETA ** (-jnp.arange(ROPE_FREQS, dtype=F32) / ROPE_FREQS)
    ang = pos[:, :, None] * inv_freq
    c, s = jnp.cos(ang), jnp.sin(ang)
    cos64 = jnp.concatenate([c[:, 0], c[:, 0], c[:, 1], c[:, 1]], axis=-1)
    sin64 = jnp.concatenate([-s[:, 0], s[:, 0], -s[:, 1], s[:, 1]], axis=-1)
    cos64 = jnp.concatenate([jnp.ones((CTX_LEN, HEAD_DIM), F32), cos64], axis=0)
    sin64 = jnp.concatenate([jnp.zeros((CTX_LEN, HEAD_DIM), F32), sin64], axis=0)
    return jnp.tile(cos64, (1, 2)), jnp.tile(sin64, (1, 2))


def _reorder_heads(w, order, axis):
    take = lambda h: lax.slice_in_dim(w, h * HEAD_DIM, (h + 1) * HEAD_DIM, axis=axis)
    return jnp.concatenate([take(h) for h in order], axis=axis)


def _swap_rope_halves(w):
    shp = w.shape
    return jnp.flip(w.reshape(shp[:-1] + (shp[-1] // (2 * ROPE_FREQS), 2, ROPE_FREQS)), axis=-2).reshape(shp)


def attn_in_proj(xs, norm_g, scale2, shift2, w_in, q_gain, k_gain, cos_t, sin_t):
    r = xs.shape[0]
    d = D_MODEL
    parts, start = [], 0
    for s in ATTN_SPLITS:
        parts.append(w_in[:, start:start + s])
        start += s
    wqa, wka, wva, wga, wqb, wkb, wvb, wgb = parts
    wqb_p = _reorder_heads(wqb, GQA_HEAD_ORDER, 1)
    wgb_p = _reorder_heads(wgb, GQA_HEAD_ORDER, 1)
    wqb_sw = _swap_rope_halves(wqb_p)
    wkb_sw = _swap_rope_halves(wkb)
    w_ext = jnp.concatenate([wqa, wka, wva, wga, wqb_p, wkb, wvb, wgb_p, wqb_sw, wkb_sw], axis=1).astype(BF16)
    gq = jnp.tile(q_gain, N_HEADS)[None]
    gqs = jnp.tile(_swap_rope_halves(q_gain), N_HEADS)[None]
    gk = jnp.tile(k_gain, 2)[None]
    gks = jnp.tile(_swap_rope_halves(k_gain), 2)[None]
    bdq = jnp.asarray(np.kron(np.eye(N_HEADS), np.full((HEAD_DIM, HEAD_DIM), 1.0 / HEAD_DIM)), BF16)
    bdk = jnp.asarray(np.kron(np.eye(2), np.full((HEAD_DIM, HEAD_DIM), 1.0 / HEAD_DIM)), BF16)
    wcols = w_ext.shape[1]
    const = lambda shp: pl.BlockSpec(shp, lambda i: (0,) * len(shp))
    rows = lambda w: pl.BlockSpec((ROW_TILE, w), lambda i: (i, 0))
    out_shapes = [
        jax.ShapeDtypeStruct((r, 512), BF16), jax.ShapeDtypeStruct((r, 512), BF16), jax.ShapeDtypeStruct((r, 512), BF16),
        jax.ShapeDtypeStruct((r, 512), BF16),
        jax.ShapeDtypeStruct((512, r), BF16), jax.ShapeDtypeStruct((r, 128), BF16), jax.ShapeDtypeStruct((128, r), BF16),
        jax.ShapeDtypeStruct((r, 512), BF16),
    ]
    cols = lambda w: pl.BlockSpec((w, ROW_TILE), lambda i: (0, i))
    return pl.pallas_call(
        _attn_in_body,
        out_shape=out_shapes,
        grid=(r // ROW_TILE,),
        in_specs=_mod_specs() + [const((d, wcols)), rows(128), rows(128), const((1, 512)), const((1, 512)),
                                 const((1, 128)), const((1, 128)), const((512, 512)), const((128, 128))],
        out_specs=[rows(512), rows(512), rows(512), rows(512), cols(512), rows(128), cols(128), rows(512)],
        compiler_params=_cparams(("parallel",)),
        name="attn_in_proj",
    )(xs, norm_g[None], scale2, shift2, w_ext, cos_t, sin_t, gq, gqs, gk, gks, bdq, bdk)


def _na_cols_body(rpb_ref, sel_ref, neg_ref, o_ref):
    o_ref[...] = jnp.dot(rpb_ref[...], sel_ref[...], precision=HIGHEST, preferred_element_type=F32) + neg_ref[...]


def _na_bias_tables(rpb, rows):
    nrel_r, nrel_c = 2 * NA_WIN_ROWS - 1, 2 * NA_WIN_COLS - 1
    qc = np.arange(GRID_W)[:, None]
    kc = np.arange(GRID_W)[None, :]
    col0 = np.clip(qc - NA_WIN_COLS // 2, 0, GRID_W - NA_WIN_COLS)
    col_ok = (kc >= col0) & (kc < col0 + NA_WIN_COLS)
    rc = kc - qc + NA_WIN_COLS - 1
    sel = np.zeros((128, GRID_W * GRID_W), np.float32)
    sel[np.where(col_ok, rc, 127).reshape(-1), np.arange(GRID_W * GRID_W)] = col_ok.reshape(-1)
    neg = np.where(col_ok, 0.0, NEG_BIG).astype(np.float32).reshape(1, -1)
    rpb2 = jnp.zeros((128, 128), F32).at[:N_HEADS * nrel_r, :nrel_c].set(rpb.reshape(N_HEADS * nrel_r, nrel_c))
    cols = pl.pallas_call(
        _na_cols_body,
        out_shape=jax.ShapeDtypeStruct((128, GRID_W * GRID_W), F32),
        name="na_bias_cols",
    )(rpb2, jnp.asarray(sel), jnp.asarray(neg))
    cols = cols[:N_HEADS * nrel_r].reshape(N_HEADS, nrel_r, GRID_W, GRID_W)
    kh = min(NA_WIN_ROWS, rows)
    g = rows // NA_GROUP_ROWS
    cases = [(0, 0), (NA_GROUP_ROWS, 0), (rows - NA_GROUP_ROWS, NA_GROUP_ROWS * (g - 3))]
    masked = jnp.full((N_HEADS, GRID_W, GRID_W), NEG_BIG, F32)
    tabs = []
    for qr_first, start in cases:
        blocks = []
        for j in range(NA_GROUP_ROWS):
            qr = qr_first + j
            row0 = min(max(qr - kh // 2, 0), rows - kh)
            for i in range(3 * NA_GROUP_ROWS):
                kr = start + i
                blocks.append(cols[:, kr - qr + NA_WIN_ROWS - 1] if row0 <= kr < row0 + kh else masked)
        tab = jnp.stack(blocks, axis=1).reshape(N_HEADS, NA_GROUP_ROWS, 3 * NA_GROUP_ROWS, GRID_W, GRID_W)
        tabs.append(tab.transpose(0, 1, 3, 2, 4).reshape(N_HEADS, ROW_TILE, 3 * ROW_TILE))
    return jnp.stack(tabs) * LOG2E


def _na_body(q_ref, kc_ref, k0_ref, k1_ref, k2_ref, vc_ref, v0_ref, v1_ref, v2_ref, bias_ref, sg_ref, o_ref):
    half = _lane_half((ROW_TILE, 128))
    k_refs = (k0_ref, k1_ref, k2_ref, kc_ref)
    v_refs = (v0_ref, v1_ref, v2_ref, vc_ref)

    def scores(h):
        hp, j = divmod(h, 2)
        ls = slice(hp * 128, (hp + 1) * 128)
        qp = q_ref[:, ls]
        qm = jnp.where(half == j, qp, jnp.zeros_like(qp))
        return [_dot_nt(qm, r[:, ls]) for r in k_refs]

    pending = [scores(h) for h in range(NA_SCORES_AHEAD)]
    outs = []
    for h in range(N_HEADS):
        hp, j = divmod(h, 2)
        ls = slice(hp * 128, (hp + 1) * 128)
        s = pending.pop(0)
        if h + NA_SCORES_AHEAD < N_HEADS:
            pending.append(scores(h + NA_SCORES_AHEAD))
        s_win = jnp.concatenate(s[:3], axis=1) + bias_ref[0, h]
        s_ctx = s[3]
        m = jnp.maximum(jnp.max(s_win, axis=1, keepdims=True), jnp.max(s_ctx, axis=1, keepdims=True))
        p_win = jnp.exp2(s_win - m).astype(BF16)
        p_ctx = jnp.exp2(s_ctx - m).astype(BF16)
        vms = [jnp.where(half == j, r[:, ls], jnp.ones((ROW_TILE, 128), BF16)) for r in v_refs]
        o = jnp.dot(p_ctx, vms[3], preferred_element_type=F32)
        for b in range(3):
            o += jnp.dot(p_win[:, b * ROW_TILE:(b + 1) * ROW_TILE], vms[b], preferred_element_type=F32)
        outs.append(o / pltpu.roll(o, HEAD_DIM, 1))
        if j == 1:
            o_pair = jnp.where(half == 0, outs[h - 1], outs[h])
            o_ref[:, ls] = (o_pair * sg_ref[:, ls]).astype(BF16)


def na_attention(qa, ka, va, sga, bias_tabs, n):
    g = n // ROW_TILE
    w = BRANCH_W

    def kv_spec(off):
        return pl.BlockSpec((ROW_TILE, w), lambda i: (jnp.clip(i - 1, 0, g - 3) + off + 1, 0))

    ctx_spec = pl.BlockSpec((ROW_TILE, w), lambda i: (0, 0))
    q_spec = pl.BlockSpec((ROW_TILE, w), lambda i: (i + 1, 0))
    case = lambda i: jnp.where(i == 0, 0, jnp.where(i == g - 1, 2, 1))
    bias_spec = pl.BlockSpec((1, N_HEADS, ROW_TILE, 3 * ROW_TILE), lambda i: (case(i), 0, 0, 0))
    return pl.pallas_call(
        _na_body,
        out_shape=jax.ShapeDtypeStruct((n, w), BF16),
        grid=(g,),
        in_specs=[q_spec, ctx_spec, kv_spec(0), kv_spec(1), kv_spec(2), ctx_spec, kv_spec(0), kv_spec(1), kv_spec(2),
                  bias_spec, q_spec],
        out_specs=pl.BlockSpec((ROW_TILE, w), lambda i: (i, 0)),
        compiler_params=_cparams(("parallel",)),
        name="na_attention",
    )(qa, ka, ka, ka, ka, va, va, va, va, bias_tabs, sga)


def _flash_mha_body(q_ref, k_ref, v_ref, sg_ref, o_ref, m_ref, acc_ref):
    kv = pl.program_id(1)
    tq = q_ref.shape[0]

    @pl.when(kv == 0)
    def _():
        m_ref[...] = jnp.full(m_ref.shape, NEG_BIG, F32)
        acc_ref[...] = jnp.zeros(acc_ref.shape, F32)

    khalf = _lane_half((k_ref.shape[0], 128))
    for p in range(N_HEADS // 2):
        ls = slice(p * 128, (p + 1) * 128)
        qp = q_ref[:, ls]
        kp = k_ref[:, ls]
        vp = v_ref[:, ls]
        for j in range(2):
            hh = 2 * p + j
            km = jnp.where(khalf == j, kp, jnp.zeros_like(kp))
            vm = jnp.where(khalf == j, vp, jnp.ones_like(vp))
            s = _dot_nt(qp, km)
            m_prev = m_ref[hh]
            m_new = jnp.maximum(m_prev, jnp.max(s, axis=1, keepdims=True))
            alpha = jnp.exp2(m_prev - m_new)
            pr = jnp.exp2(s - m_new[:, :1]).astype(BF16)
            acc_ref[hh] = alpha * acc_ref[hh] + jnp.dot(pr, vm, preferred_element_type=F32)
            m_ref[hh] = m_new

    @pl.when(kv == pl.num_programs(1) - 1)
    def _():
        half = _lane_half((tq, 128))
        for p in range(N_HEADS // 2):
            ls = slice(p * 128, (p + 1) * 128)
            a0, a1 = acc_ref[2 * p], acc_ref[2 * p + 1]
            o0 = a0 / pltpu.roll(a0, HEAD_DIM, 1)
            o1 = a1 / pltpu.roll(a1, HEAD_DIM, 1)
            o_ref[:, ls] = (jnp.where(half == 0, o0, o1) * sg_ref[:, ls]).astype(BF16)


def flash_mha(q, k, v, sg, *, q_block0, nq, tk, nk):
    tq = ROW_TILE
    return pl.pallas_call(
        _flash_mha_body,
        out_shape=jax.ShapeDtypeStruct((nq * tq, BRANCH_W), BF16),
        grid=(nq, nk),
        in_specs=[
            pl.BlockSpec((tq, BRANCH_W), lambda i, j: (i + q_block0, 0)),
            pl.BlockSpec((tk, BRANCH_W), lambda i, j: (j, 0)),
            pl.BlockSpec((tk, BRANCH_W), lambda i, j: (j, 0)),
            pl.BlockSpec((tq, BRANCH_W), lambda i, j: (i + q_block0, 0)),
        ],
        out_specs=pl.BlockSpec((tq, BRANCH_W), lambda i, j: (i, 0)),
        scratch_shapes=[pltpu.VMEM((N_HEADS, tq, 128), F32)] * 2,
        compiler_params=_cparams(("parallel", "arbitrary")),
        name="flash_mha",
    )(q, k, v, sg)


def _flash_gqa_body(*refs, nsub):
    qt_refs, (k_ref, vt_ref), sg_refs = refs[:nsub], refs[nsub:nsub + 2], refs[nsub + 2:2 * nsub + 2]
    o_ref, m_ref, acc_ref = refs[2 * nsub + 2:]
    kv = pl.program_id(1)
    tq = qt_refs[0].shape[1]
    tk = k_ref.shape[0]

    @pl.when(kv == 0)
    def _():
        m_ref[...] = jnp.full(m_ref.shape, NEG_BIG, F32)
        acc_ref[...] = jnp.zeros(acc_ref.shape, F32)

    khalf = _lane_half((tk, 128))
    vhalf = lax.broadcasted_iota(jnp.int32, (128, tk), 0) // HEAD_DIM
    kb = k_ref[...]
    vt = vt_ref[...]
    kms = [jnp.where(khalf == j, kb, jnp.zeros_like(kb)) for j in range(2)]
    vms = [jnp.where(vhalf == j, vt, jnp.ones_like(vt)) for j in range(2)]
    units = nsub * N_HEADS

    def scores(u):
        sub, hh = divmod(u, N_HEADS)
        p, j = divmod(hh, 2)
        return jnp.dot(kms[j], qt_refs[sub][p * 128:(p + 1) * 128, :], preferred_element_type=F32).astype(BF16)

    pending = [scores(u) for u in range(FLASH_SCORES_AHEAD)]
    for u in range(units):
        st = pending.pop(0)
        if u + FLASH_SCORES_AHEAD < units:
            pending.append(scores(u + FLASH_SCORES_AHEAD))
        m_prev = m_ref[u]
        m_new = jnp.maximum(m_prev, jnp.max(st, axis=0, keepdims=True).astype(F32))
        alpha = jnp.exp2(m_prev - m_new)
        pt = jnp.exp2(st - m_new[0:1].astype(BF16))
        acc_ref[u] = alpha[0:1] * acc_ref[u] + jnp.dot(vms[u % 2], pt, preferred_element_type=F32)
        m_ref[u] = m_new

    @pl.when(kv == pl.num_programs(1) - 1)
    def _():
        for sub in range(nsub):
            for p in range(N_HEADS // 2):
                ls = slice(p * 128, (p + 1) * 128)
                a0, a1 = acc_ref[sub * N_HEADS + 2 * p], acc_ref[sub * N_HEADS + 2 * p + 1]
                ot = jnp.concatenate([a0[:HEAD_DIM] / a0[HEAD_DIM:], a1[HEAD_DIM:] / a1[:HEAD_DIM]], axis=0)
                o_ref[sub * tq:(sub + 1) * tq, ls] = (jnp.transpose(ot) * sg_refs[sub][:, ls]).astype(BF16)


def flash_gqa(qt, k, vt, sg, *, q_block0, nq, tk, nk, nsub):
    tq = ROW_TILE
    assert nq % nsub == 0
    qt_specs = [pl.BlockSpec((BRANCH_W, tq), lambda i, j, s=s: (0, nsub * i + s + q_block0)) for s in range(nsub)]
    sg_specs = [pl.BlockSpec((tq, BRANCH_W), lambda i, j, s=s: (nsub * i + s + q_block0, 0)) for s in range(nsub)]
    return pl.pallas_call(
        functools.partial(_flash_gqa_body, nsub=nsub),
        out_shape=jax.ShapeDtypeStruct((nq * tq, BRANCH_W), BF16),
        grid=(nq // nsub, nk),
        in_specs=qt_specs + [pl.BlockSpec((tk, GQA_KV_W), lambda i, j: (j, 0)),
                             pl.BlockSpec((GQA_KV_W, tk), lambda i, j: (0, j))] + sg_specs,
        out_specs=pl.BlockSpec((nsub * tq, BRANCH_W), lambda i, j: (i, 0)),
        scratch_shapes=[pltpu.VMEM((nsub * N_HEADS, 8, tq), F32), pltpu.VMEM((nsub * N_HEADS, 128, tq), F32)],
        compiler_params=_cparams(("parallel", "arbitrary")),
        name="flash_gqa",
    )(*([qt] * nsub), k, vt, *([sg] * nsub))


def _out_body(x_ref, ya_ref, yac_ref, yb_ref, ybc_ref, wa_ref, wb_ref, gate_ref, o_ref):
    is_ctx = pl.program_id(0) == 0
    ya = jnp.where(is_ctx, yac_ref[...], ya_ref[...])
    yb = jnp.where(is_ctx, ybc_ref[...], yb_ref[...])
    y = jnp.dot(ya, wa_ref[...], preferred_element_type=F32)
    y += jnp.dot(yb, wb_ref[...], preferred_element_type=F32)
    o_ref[...] = x_ref[...] + gate_ref[0] * y


def out_proj(xs, ya_lat, ya_ctx, yb_lat, yb_ctx, wa, wb, gate2):
    r = xs.shape[0]
    d = D_MODEL
    rows = lambda w: pl.BlockSpec((ROW_TILE, w), lambda i: (i, 0))
    lat = pl.BlockSpec((ROW_TILE, BRANCH_W), lambda i: (jnp.maximum(i - 1, 0), 0))
    const = lambda shp: pl.BlockSpec(shp, lambda i: (0,) * len(shp))
    ctx = const((CTX_LEN, BRANCH_W))
    return pl.pallas_call(
        _out_body,
        out_shape=jax.ShapeDtypeStruct((r, d), F32),
        grid=(r // ROW_TILE,),
        in_specs=[rows(d), lat, ctx, lat, ctx, const((BRANCH_W, d)), const((BRANCH_W, d)),
                  pl.BlockSpec((1, 1, d), lambda i: (jnp.minimum(i, 1), 0, 0))],
        out_specs=rows(d),
        compiler_params=_cparams(("parallel",)),
        name="out_proj",
    )(xs, ya_lat, ya_ctx, yb_lat, yb_ctx, wa.astype(BF16), wb.astype(BF16), gate2)


KV_TILE_MAX_BLOCKS = 13


def _kv_tile(r):
    nb = r // ROW_TILE
    best = max(k for k in range(1, KV_TILE_MAX_BLOCKS + 1) if nb % k == 0)
    return best * ROW_TILE, nb // best


def attn_layer(xs, mods, norm_g, w_in, rpb, q_gain, k_gain, w_out, cos_t, sin_t):
    r = xs.shape[0]
    n = r - CTX_LEN
    shift2, scale2, gate2 = mods
    qa, ka, va, sga, qbt, kb, vbt, sgb = attn_in_proj(xs, norm_g, scale2, shift2, w_in, q_gain, k_gain, cos_t, sin_t)
    bias_tabs = _na_bias_tables(rpb, n // GRID_W)
    ya_lat = na_attention(qa, ka, va, sga, bias_tabs, n)
    ya_ctx = flash_mha(qa, ka, va, sga, q_block0=0, nq=1, tk=CTX_LEN, nk=1)
    tk, nk = _kv_tile(r)
    nq = n // ROW_TILE
    yb_lat = flash_gqa(qbt, kb, vbt, sgb, q_block0=1, nq=nq, tk=tk, nk=nk, nsub=FLASH_Q_TILES if nq % FLASH_Q_TILES == 0 else 1)
    yb_ctx = flash_gqa(qbt, kb, vbt, sgb, q_block0=0, nq=1, tk=CTX_LEN, nk=1, nsub=1)
    wb = _reorder_heads(w_out[BRANCH_W:], GQA_HEAD_ORDER, 0)
    return out_proj(xs, ya_lat, ya_ctx, yb_lat, yb_ctx, w_out[:BRANCH_W], wb, gate2)


def _split(a):
    hi = a.astype(BF16)
    return hi, (a - hi.astype(F32)).astype(BF16)


def _dot3(a, b, dims=(((1,), (0,)), ((), ()))):
    ah, al = _split(a)
    bh, bl = _split(b)
    dg = functools.partial(lax.dot_general, dimension_numbers=dims, preferred_element_type=F32)
    return dg(ah, bh) + (dg(al, bh) + dg(ah, bl))


def _dot1(a, b, dims=(((1,), (0,)), ((), ()))):
    return lax.dot_general(a.astype(BF16), b.astype(BF16), dims, preferred_element_type=F32)


_NT = (((1,), (1,)), ((), ()))
_TN = (((0,), (0,)), ((), ()))


RWKV_SHIFT_W = 1664
HY_IN_W = 1536
HALO = 8
REC_HALO_W = RWKV_SHIFT_W + HY_IN_W


def _rec_in_body(x_ref, xp_ref, xn_ref, g_ref, scale_ref, shift_ref, w_ref, mu_ref, taps_ref,
                 rw_ref, hv_ref, hx1_ref, hx2_ref, sgr_ref, sgh_ref, u_scr):
    i = pl.program_id(0)
    nt = pl.num_programs(0)
    xe = jnp.concatenate([xp_ref[...], x_ref[...], xn_ref[...]], axis=0)
    y = xe * lax.rsqrt(jnp.mean(xe * xe, axis=-1, keepdims=True) + NORM_EPS)
    xm = ((y * g_ref[...]) * (1.0 + scale_ref[0]) + shift_ref[0]).astype(BF16)
    u = jnp.dot(xm, w_ref[...], preferred_element_type=F32)
    row = lax.broadcasted_iota(jnp.int32, (ROW_TILE + 2 * HALO, 1), 0)
    keep = jnp.logical_and(jnp.logical_or(row >= HALO, i >= 2),
                           jnp.logical_or(row < ROW_TILE + HALO, jnp.logical_and(i >= 1, i < nt - 1)))
    u_scr[...] = jnp.where(keep, u[:, :REC_HALO_W], 0.0)
    up = u_scr[pl.ds(HALO - 1, ROW_TILE), :]
    uc = u_scr[pl.ds(HALO, ROW_TILE), :]
    un = u_scr[pl.ds(HALO + 1, ROW_TILE), :]
    w = RWKV_SHIFT_W
    rw_c = uc[:, :w]
    rw_ref[...] = rw_c + (0.5 * (up[:, :w] + un[:, :w]) - rw_c) * mu_ref[...]
    hy = up[:, w:] * taps_ref[0:1] + uc[:, w:] * taps_ref[1:2] + un[:, w:] * taps_ref[2:3]
    hv_ref[...] = hy[:, 0:512].astype(hv_ref.dtype)
    hx1_ref[...] = hy[:, 512:1024].astype(hx1_ref.dtype)
    hx2_ref[...] = hy[:, 1024:1536].astype(hx2_ref.dtype)
    uc_all = u[HALO:HALO + ROW_TILE]
    sgr_ref[...] = _silu(uc_all[:, REC_HALO_W:REC_HALO_W + 512]).astype(sgr_ref.dtype)
    sgh_ref[...] = _silu(uc_all[:, REC_HALO_W + 512:REC_HALO_W + 1024]).astype(sgh_ref.dtype)


def rec_in_proj(xs, norm_g, scale2, shift2, w_in, mu, hy_short):
    r = xs.shape[0]
    d = D_MODEL
    w = RWKV_SHIFT_W
    w_ext = jnp.concatenate([w_in[:, :w], w_in[:, w + 512:w + 512 + HY_IN_W], w_in[:, w:w + 512],
                             w_in[:, w + 512 + HY_IN_W:]], axis=1).astype(BF16)
    nh = r // HALO
    per = ROW_TILE // HALO
    const = lambda shp: pl.BlockSpec(shp, lambda i: (0,) * len(shp))
    rows = lambda wd: pl.BlockSpec((ROW_TILE, wd), lambda i: (i, 0))
    f = lambda wd: jax.ShapeDtypeStruct((r, wd), F32)
    h = jax.ShapeDtypeStruct((r, 512), BF16)
    mod = _mod_specs()
    return pl.pallas_call(
        _rec_in_body,
        out_shape=[f(w), h, h, h, h, h],
        grid=(r // ROW_TILE,),
        in_specs=[mod[0],
                  pl.BlockSpec((HALO, d), lambda i: (jnp.maximum(i * per - 1, 0), 0)),
                  pl.BlockSpec((HALO, d), lambda i: (jnp.minimum((i + 1) * per, nh - 1), 0)),
                  mod[1], mod[2], mod[3], const((d, w_ext.shape[1])), const((1, w)), const((3, HY_IN_W))],
        out_specs=[rows(w), rows(512), rows(512), rows(512), rows(512), rows(512)],
        scratch_shapes=[pltpu.VMEM((ROW_TILE + 2 * HALO, REC_HALO_W), F32)],
        compiler_params=_cparams(("parallel",)),
        name="rec_in_proj",
    )(xs, xs, xs, norm_g[None], scale2, shift2, w_ext, mu[None], hy_short)


CHUNK = 64
CPT = ROW_TILE // CHUNK


def _block_sum_mat(width, value):
    return jnp.asarray(np.kron(np.eye(width // HEAD_DIM), np.full((HEAD_DIM, HEAD_DIM), value)), F32)


PREP_PAIRS = 4


def _rwkv_prep_body(r_ref, k_ref, v_ref, lora_ref, w0_ref, wup_ref, a0_ref, aup_ref, kk_ref, ka_ref, rk_ref,
                    tri_ref, bs_ref, g_ref, add_ref, bonus_ref):
    t = ROW_TILE
    lora = lora_ref[...]
    bs = bs_ref[...]
    row = lax.broadcasted_iota(jnp.int32, (t, t), 0)
    col = lax.broadcasted_iota(jnp.int32, (t, t), 1)
    same = (row // CHUNK) == (col // CHUNK)
    eye = (row == col).astype(F32)
    wl_all = _dot3(jnp.tanh(lora), wup_ref[...])
    al_all = _dot3(lora, aup_ref[...])
    half = _lane_half((t, 128))
    half_c = _lane_half((HEAD_DIM, 128))
    rowc = lax.broadcasted_iota(jnp.int32, (HEAD_DIM, 128), 0)
    lanec = lax.broadcasted_iota(jnp.int32, (HEAD_DIM, 128), 1)
    level_masks = []
    bsz = 2
    while bsz < CHUNK:
        level_masks.append(jnp.logical_and((row // (2 * bsz)) == (col // (2 * bsz)), (row // bsz) != (col // bsz)))
        bsz *= 2
    first_mask = (row // 2) == (col // 2)

    groups = []
    for q in range(PREP_PAIRS):
        ls = slice(q * 128, (q + 1) * 128)
        r, k, v = r_ref[:, ls], k_ref[:, ls], v_ref[:, ls]
        kk = k * kk_ref[:, ls]
        kk = kk * lax.rsqrt(_dot_stat(kk * kk, bs) + 1e-12)
        kd_sum = None
        for d in range(2):
            ds = slice(q * 256 + d * 128, q * 256 + (d + 1) * 128)
            wl = w0_ref[d, :, ls] + wl_all[:, ds]
            z = -wl
            w_log = -(jnp.maximum(z, 0.0) + jnp.log(1.0 + jnp.exp(-jnp.abs(z)))) - 0.5
            lw = -jnp.exp(w_log)
            a = 1.0 / (1.0 + jnp.exp(-(a0_ref[d, :, ls] + al_all[:, ds])))
            kd = k * (1.0 + (a - 1.0) * ka_ref[:, ls])
            b = kk * a
            kd_sum = kd if kd_sum is None else kd_sum + kd
            incl = jnp.logical_and(same, (col <= row) if d == 0 else (col >= row))
            lw_hi, lw_lo = _split(lw)
            tri = tri_ref[d]
            cs = jnp.dot(tri, lw_hi, preferred_element_type=F32) + jnp.dot(tri, lw_lo, preferred_element_type=F32)
            ends = [c * CHUNK + (CHUNK - 1 if d == 0 else 0) for c in range(CPT)]
            tot = jnp.concatenate([jnp.broadcast_to(cs[e:e + 1], (CHUNK, 128)) for e in ends], axis=0)
            w_inv = jnp.exp(-cs)
            w_rest = jnp.exp(tot - cs)
            groups.append(dict(q=q, d=d, v=v, incl=incl, strict=jnp.logical_and(incl, row != col), tot=tot,
                               kkt=kk * jnp.exp(cs - lw), kh=kd * w_inv, bh=b * w_inv, rt=r * jnp.exp(cs),
                               kdd=kd * w_rest, bdd=b * w_rest))
        bonus_ref[:, ls] = 0.5 * _dot_stat(r * kd_sum * rk_ref[:, ls], bs) * v
    chains = [(gg, j) for gg in groups for j in range(2)]
    sels = [half == j for _, j in chains]
    bms = [jnp.where(sel, gg["bh"], 0.0) for (gg, _), sel in zip(chains, sels)]
    kms = [jnp.where(sel, gg["kh"], 0.0) for (gg, _), sel in zip(chains, sels)]
    l_bs = [jnp.where(gg["strict"], _dot1(gg["kkt"], bm, _NT), 0.0) for (gg, _), bm in zip(chains, bms)]
    tinvs = [eye - jnp.where(first_mask, l_b, 0.0) for l_b in l_bs]
    for mask in level_masks:
        xs = [_dot1(jnp.where(mask, l_b, 0.0), tinv) for l_b, tinv in zip(l_bs, tinvs)]
        tinvs = [tinv - _dot1(tinv, x) for tinv, x in zip(tinvs, xs)]
    l_ks = [jnp.where(gg["strict"], _dot1(gg["kkt"], km, _NT), 0.0) for (gg, _), km in zip(chains, kms)]
    a_rks = [jnp.where(gg["incl"], _dot1(gg["rt"], km, _NT), 0.0) for (gg, _), km in zip(chains, kms)]
    a_rbs = [jnp.where(gg["incl"], _dot1(gg["rt"], bm, _NT), 0.0) for (gg, _), bm in zip(chains, bms)]
    lvs = [_dot1(l_k, gg["v"]) for (gg, _), l_k in zip(chains, l_ks)]
    pus = [_dot1(tinv, jnp.concatenate([gg["kkt"], lv], axis=1)) for (gg, _), tinv, lv in zip(chains, tinvs, lvs)]
    cors = [_dot1(a_rb, pu) for a_rb, pu in zip(a_rbs, pus)]
    ps = [pu[:, :128] for pu in pus]
    u0s = [pu[:, 128:] for pu in pus]
    qs = [gg["rt"] - cor[:, :128] for (gg, _), cor in zip(chains, cors)]
    y0s = [_dot1(a_rk, gg["v"]) - cor[:, 128:] for (gg, _), a_rk, cor in zip(chains, a_rks, cors)]

    sel0 = half == 0
    for n, gg in enumerate(groups):
        d, v = gg["d"], gg["v"]
        ls = slice(gg["q"] * 128, (gg["q"] + 1) * 128)
        p, u0, q, y0 = (jnp.where(sel0, x[2 * n], x[2 * n + 1]) for x in (ps, u0s, qs, y0s))
        for c in range(CPT):
            rs = slice(c * CHUNK, (c + 1) * CHUNK)
            x1 = _dot1(gg["bdd"][rs], p[rs], _TN)
            x2 = _dot1(gg["kdd"][rs], v[rs], _TN) - _dot1(gg["bdd"][rs], u0[rs], _TN)
            m_pair = jnp.where(half_c == 0, x1[:HEAD_DIM], x1[HEAD_DIM:])
            n_pair = jnp.where(half_c == 0, x2[:HEAD_DIM], x2[HEAD_DIM:])
            wc = jnp.exp(gg["tot"][c * CHUNK:c * CHUNK + 1])
            dg = jnp.where((lanec % HEAD_DIM) == rowc, wc, 0.0)
            g_ref[c, d, 0:HEAD_DIM, ls] = dg - m_pair
            g_ref[c, d, HEAD_DIM:, ls] = q[rs]
            add_ref[c, d, 0:HEAD_DIM, ls] = n_pair
            add_ref[c, d, HEAD_DIM:, ls] = y0[rs]


def _lora_ext(up, first_row):
    out = jnp.zeros((2, 128, BRANCH_W), F32)
    for d in range(2):
        out = out.at[d, first_row + 32 * d:first_row + 32 * (d + 1)].set(up[d])
    return out


def rwkv_prep(rw, w0, w_up, a0, a_up, k_k, k_a, r_k):
    r = rw.shape[0]
    nt = r // ROW_TILE
    nch = r // CHUNK
    t = ROW_TILE
    ii = np.arange(t)
    same = (ii[:, None] // CHUNK) == (ii[None, :] // CHUNK)
    tri = jnp.asarray(np.stack([same & (ii[None, :] <= ii[:, None]), same & (ii[None, :] >= ii[:, None])]), BF16)
    pair_cat = lambda w: w.reshape(2, 128, N_HEADS // 2, 128).transpose(1, 2, 0, 3).reshape(128, 2 * BRANCH_W)
    w = 128 * PREP_PAIRS
    steps = N_HEADS // 2 // PREP_PAIRS
    lane = lambda blk: pl.BlockSpec((t, w), lambda i, p, blk=blk: (i, blk + p))
    pvec = pl.BlockSpec((1, w), lambda i, p: (0, p))
    dvec = pl.BlockSpec((2, 1, w), lambda i, p: (0, 0, p))
    dmat = pl.BlockSpec((128, 2 * w), lambda i, p: (0, p))
    gspec = pl.BlockSpec((CPT, 2, HEAD_DIM + CHUNK, w), lambda i, p: (i, 0, 0, p))
    gshape = jax.ShapeDtypeStruct((nch, 2, HEAD_DIM + CHUNK, BRANCH_W), F32)
    return pl.pallas_call(
        _rwkv_prep_body,
        out_shape=[gshape, gshape, jax.ShapeDtypeStruct((r, BRANCH_W), F32)],
        grid=(nt, steps),
        in_specs=[lane(0), lane(steps), lane(2 * steps), pl.BlockSpec((t, 128), lambda i, p: (i, 12)),
                  dvec, dmat, dvec, dmat, pvec, pvec, pvec,
                  pl.BlockSpec((2, t, t), lambda i, p: (0, 0, 0)),
                  pl.BlockSpec((128, 128), lambda i, p: (0, 0))],
        out_specs=[gspec, gspec, pl.BlockSpec((t, w), lambda i, p: (i, p))],
        compiler_params=_cparams(("parallel", "parallel")),
        name="rwkv_prep",
    )(rw, rw, rw, rw, w0.reshape(2, 1, BRANCH_W), pair_cat(_lora_ext(w_up, 0)), a0.reshape(2, 1, BRANCH_W),
      pair_cat(_lora_ext(a_up, 64)), k_k[None], k_a[None], r_k.reshape(1, BRANCH_W), tri,
      _block_sum_mat(128, 1.0).astype(BF16))


def _rwkv_scan_body(gf_ref, af_ref, gb_ref, ab_ref, yf_ref, yb_ref, st_ref):
    @pl.when(pl.program_id(0) == 0)
    def _():
        st_ref[...] = jnp.zeros(st_ref.shape, F32)

    rowh = lax.broadcasted_iota(jnp.int32, (128, 128), 0) // HEAD_DIM
    diag = rowh == _lane_half((128, 128))
    for s in range(CPT):
        for d, (g_ref, a_ref, y_ref, c) in enumerate(((gf_ref, af_ref, yf_ref, s), (gb_ref, ab_ref, yb_ref, CPT - 1 - s))):
            for p in range(N_HEADS // 2):
                ls = slice(p * 128, (p + 1) * 128)
                out = _dot3(g_ref[c, 0, :, ls], st_ref[d, p]) + a_ref[c, 0, :, ls]
                hn = out[:HEAD_DIM]
                st_ref[d, p] = jnp.where(diag, jnp.concatenate([hn, hn], axis=0), 0.0)
                y_ref[c * CHUNK:(c + 1) * CHUNK, ls] = out[HEAD_DIM:]


def rwkv_scan(g, add):
    nch = g.shape[0]
    r = nch * CHUNK
    nt = r // ROW_TILE
    assert CTX_LEN == ROW_TILE
    rev = lambda i: jnp.where(i == 0, 0, nt - i)
    blk = (CPT, 1, HEAD_DIM + CHUNK, BRANCH_W)
    fwd = pl.BlockSpec(blk, lambda i: (i, 0, 0, 0))
    bwd = pl.BlockSpec(blk, lambda i: (rev(i), 1, 0, 0))
    yshape = jax.ShapeDtypeStruct((r, BRANCH_W), F32)
    return pl.pallas_call(
        _rwkv_scan_body,
        out_shape=[yshape, yshape],
        grid=(nt,),
        in_specs=[fwd, fwd, bwd, bwd],
        out_specs=[pl.BlockSpec((ROW_TILE, BRANCH_W), lambda i: (i, 0)),
                   pl.BlockSpec((ROW_TILE, BRANCH_W), lambda i: (rev(i), 0))],
        scratch_shapes=[pltpu.VMEM((2, N_HEADS // 2, 128, 128), F32)],
        compiler_params=_cparams(("arbitrary",)),
        name="rwkv_scan",
    )(g, add, g, add)


HY_WIDTH = 512
HY_ORDER = 2
HY_POS_BANDS = 16
HY_HIDDEN = 64
HY_TAPS_W = 2 * HY_ORDER * HY_WIDTH
FFT_N2 = ROW_TILE


def _dot3c(ah, al, b):
    bh, bl = _split(b)
    dg = functools.partial(jnp.dot, preferred_element_type=F32)
    return dg(ah, bh) + (dg(al, bh) + dg(ah, bl))


def _dotc(ah, b):
    return jnp.dot(ah, b.astype(BF16), preferred_element_type=F32)


def _split_const(m):
    m = np.asarray(m, np.float32)
    hi = m.astype(BF16)
    lo = (m - hi.astype(np.float32)).astype(BF16)
    return jnp.asarray(hi), jnp.asarray(lo)


TAPS_FLAT_COLS = 8


def _filter_taps(t_idx, length, c2pb_ref, w1t_ref, w1c_ref, w1s_ref, b1_ref, w2_ref, b2_ref, w3_ref, b3_ref, absd_ref):
    t = t_idx / float(max(length - 1, 1))
    ang = c2pb_ref[...] * t_idx / float(length)
    pre = t * w1t_ref[...] + _dot3(jnp.cos(ang), w1c_ref[...]) - _dot3(jnp.sin(ang), w1s_ref[...]) + b1_ref[...]
    hid = jnp.sin(pre)
    hid = jnp.sin(_dot3(hid, w2_ref[...]) + b2_ref[...])
    return (_dot3(hid, w3_ref[...]) + b3_ref[...]) * jnp.exp(-t * absd_ref[...])


def _hy_taps_body(*refs, length):
    taps_ref, ssq_ref, tap0_ref = refs[-3:]
    i = pl.program_id(0)
    t_idx = (i * ROW_TILE + lax.broadcasted_iota(jnp.int32, (ROW_TILE, 1), 0)).astype(F32)
    taps = _filter_taps(t_idx, length, *refs[:-3])
    taps_ref[...] = taps

    @pl.when(i == 0)
    def _():
        ssq_ref[...] = jnp.zeros(ssq_ref.shape, F32)
        tap0_ref[...] = taps[0:1]

    ssq_ref[...] += jnp.sum(taps * taps, axis=0, keepdims=True)


def _hy_taps_flat_body(c2pb_ref, w1t_ref, w1c_ref, w1s_ref, b1_ref, w2_ref, b2_ref, w3_ref, b3_ref, absd_ref, f1_ref,
                       planes_ref, ssq_ref, tap0_ref, *, length):
    j = pl.program_id(0)
    rows = length // FFT_N2
    nb = TAPS_FLAT_COLS

    def positions(width, per):
        a = lax.broadcasted_iota(jnp.int32, (rows, width), 0)
        b = lax.broadcasted_iota(jnp.int32, (rows, width), 1) // per
        return (a * FFT_N2 + j * nb + b).astype(F32)

    @pl.when(j == 0)
    def _():
        ssq_ref[...] = jnp.zeros(ssq_ref.shape, F32)

    h = HY_HIDDEN
    ang = c2pb_ref[...] * positions(nb * HY_POS_BANDS, HY_POS_BANDS) / float(length)
    t_h = positions(nb * h, h) / float(max(length - 1, 1))
    pre = t_h * w1t_ref[...] + _dot3(jnp.cos(ang), w1c_ref[...]) - _dot3(jnp.sin(ang), w1s_ref[...]) + b1_ref[...]
    hid = jnp.sin(pre)
    hid = jnp.sin(_dot3(hid, w2_ref[...]) + b2_ref[...])
    w3h, w3l = _split(w3_ref[...])
    dg = functools.partial(jnp.dot, preferred_element_type=F32)
    for bp in range(nb // 2):
        t_w = (positions(2 * HY_TAPS_W, HY_TAPS_W) + float(2 * bp)) / float(max(length - 1, 1))
        hh, hl = _split(hid[:, bp * 2 * h:(bp + 1) * 2 * h])
        lin = dg(hh, w3h) + (dg(hl, w3h) + dg(hh, w3l))
        taps = (lin + b3_ref[...]) * jnp.exp(-t_w * absd_ref[...])
        planes = _dotc(f1_ref[...], taps)
        planes_ref[:, bp * 2 * HY_TAPS_W:(bp + 1) * 2 * HY_TAPS_W] = planes.astype(planes_ref.dtype)
        if bp == 0:
            @pl.when(j == 0)
            def _():
                tap0_ref[...] = taps[0:1, :HY_TAPS_W]
        sq = jnp.sum(taps * taps, axis=0, keepdims=True)
        ssq_ref[...] += sq[:, :HY_TAPS_W] + sq[:, HY_TAPS_W:]


def hyena_taps(length, w1, b1, w2, b2, w3, b3, *, flat):
    bands = jnp.linspace(1e-4, HY_POS_BANDS - 1, HY_POS_BANDS, dtype=F32)
    c2pb = jnp.zeros((1, 128), F32).at[0, :HY_POS_BANDS].set(2.0 * math.pi * bands)
    pad = lambda m: jnp.zeros((128, HY_HIDDEN), F32).at[:HY_POS_BANDS].set(m)
    deltas = jnp.linspace(math.log(1e-2) / 0.3, math.log(1e-2) / 1.5, HY_WIDTH, dtype=F32)
    absd = jnp.tile(jnp.abs(deltas), 2 * HY_ORDER)[None]
    const = lambda shp: pl.BlockSpec(shp, lambda i: (0,) * len(shp))
    w1t, w1c, w1s = w1[0:1], w1[1:1 + HY_POS_BANDS], w1[1 + HY_POS_BANDS:]
    if flat:
        nb = TAPS_FLAT_COLS
        f1 = _FftPlan(length).f1[0]
        rows = f1.shape[0]
        bd = lambda m, k: jnp.kron(jnp.eye(k, dtype=F32), m)
        tile = lambda v, k: jnp.tile(v, k)[None]
        operands = (tile(c2pb[0, :HY_POS_BANDS], nb), tile(w1t[0], nb), bd(w1c, nb), bd(w1s, nb), tile(b1, nb),
                    bd(w2, nb), tile(b2, nb), bd(w3, 2), tile(b3, 2), tile(absd[0], 2), f1)
        body, grid = _hy_taps_flat_body, (FFT_N2 // nb,)
        taps_shape, taps_dtype = (rows, FFT_N2 * HY_TAPS_W), BF16
        taps_spec = pl.BlockSpec((rows, nb * HY_TAPS_W), lambda i: (0, i))
    else:
        operands = (c2pb, w1t, pad(w1c), pad(w1s), b1[None], w2, b2[None], w3, b3[None], absd)
        body, grid = _hy_taps_body, (length // ROW_TILE,)
        taps_shape, taps_dtype = (length, HY_TAPS_W), F32
        taps_spec = pl.BlockSpec((ROW_TILE, HY_TAPS_W), lambda i: (i, 0))
    taps, ssq, tap0 = pl.pallas_call(
        functools.partial(body, length=length),
        out_shape=[jax.ShapeDtypeStruct(taps_shape, taps_dtype), jax.ShapeDtypeStruct((1, HY_TAPS_W), F32),
                   jax.ShapeDtypeStruct((1, HY_TAPS_W), F32)],
        grid=grid,
        in_specs=[const(op.shape) for op in operands],
        out_specs=[taps_spec, const((1, HY_TAPS_W)), const((1, HY_TAPS_W))],
        compiler_params=_cparams(("arbitrary",)),
        name="hyena_taps_flat" if flat else "hyena_taps",
    )(*operands)
    hw = HY_TAPS_W // 2
    norm2 = ssq[:, :hw] + ssq[:, hw:] + 2.0 * tap0[:, :hw] * tap0[:, hw:]
    return taps, lax.rsqrt(norm2)


class _FftPlan:
    def __init__(self, length):
        self.length = length
        self.n = 2 * length
        self.n2 = FFT_N2
        self.n1 = self.n // self.n2
        self.n1h = self.n1 // 2
        k1 = self.n1h + 1
        self.k1p = -(-k1 // 8) * 8
        kk = np.arange(self.k1p)[:, None].astype(np.float64)
        live = (kk < k1)
        nn = np.arange(self.n1h)[None, :].astype(np.float64)
        th = 2.0 * np.pi * kk * nn / self.n1
        self.f1 = _split_const(np.concatenate([np.cos(th) * live, -np.sin(th) * live], axis=0))
        ck = np.where((kk == 0) | (kk == self.n1h), 1.0, 2.0) * live / self.n
        self.g1 = _split_const(np.concatenate([np.cos(th) * ck, -np.sin(th) * ck], axis=0).T)
        m = np.arange(self.n2).astype(np.float64)
        ph = 2.0 * np.pi * np.outer(m, m) / self.n2
        c, s = np.cos(ph), np.sin(ph)
        self.fb = _split_const(np.block([[c, s], [-s, c]]))
        self.fbi = _split_const(np.block([[c, -s], [s, c]]))
        tw = 2.0 * np.pi * kk[:, :, None] * m[None, :, None] / self.n
        self.twc = jnp.asarray(np.cos(tw), F32)
        self.tws = jnp.asarray(np.sin(tw), F32)


FFT_TN = 4096


def _fft_a_body(fh_ref, x_ref, o_ref):
    o_ref[...] = _dotc(fh_ref[...], x_ref[...]).astype(o_ref.dtype)


def fft_stage_a(plan, xf, lead):
    rows_in, m = xf.shape
    rows = 2 * plan.k1p
    tn = min(FFT_TN, m)
    fh = jnp.pad(plan.f1[0], ((0, 0), (lead, 0)))
    fspec = pl.BlockSpec((rows, rows_in), lambda j: (0, 0))
    return pl.pallas_call(
        _fft_a_body,
        out_shape=jax.ShapeDtypeStruct((rows, m), BF16),
        grid=(m // tn,),
        in_specs=[fspec, pl.BlockSpec((rows_in, tn), lambda j: (0, j))],
        out_specs=pl.BlockSpec((rows, tn), lambda j: (0, j)),
        compiler_params=_cparams(("parallel",)),
        name="fft_stage_a",
    )(fh, xf)


FFT_PLANES = 2


def _twiddled(a_ref, twc_ref, tws_ref, p):
    are, aim = a_ref[0, p].astype(F32), a_ref[1, p].astype(F32)
    c, s = twc_ref[p], tws_ref[p]
    return jnp.concatenate([are * c + aim * s, aim * c - are * s], axis=0)


def _fft_filter_b_body(a_ref, twc_ref, tws_ref, fbh_ref, scale_ref, o_ref):
    n2 = FFT_N2
    hw = HY_TAPS_W // 2
    xs = [_dotc(fbh_ref[...], _twiddled(a_ref, twc_ref, tws_ref, p)) for p in range(FFT_PLANES)]
    for p, x in enumerate(xs):
        xre, xim = x[:n2], x[n2:]
        o_ref[0, p] = ((xre[:, :hw] + xre[:, hw:]) * scale_ref[...]).astype(o_ref.dtype)
        o_ref[1, p] = ((xim[:, :hw] - xim[:, hw:]) * scale_ref[...]).astype(o_ref.dtype)


def fft_filter_stage_b(plan, a, scale):
    n2, k1p = plan.n2, plan.k1p
    hw = HY_TAPS_W // 2
    const = lambda shp: pl.BlockSpec(shp, lambda k: (0,) * len(shp))
    return pl.pallas_call(
        _fft_filter_b_body,
        out_shape=jax.ShapeDtypeStruct((2, k1p, n2, hw), BF16),
        grid=(k1p // FFT_PLANES,),
        in_specs=[pl.BlockSpec((2, FFT_PLANES, n2, HY_TAPS_W), lambda k: (0, k, 0, 0)),
                  pl.BlockSpec((FFT_PLANES, n2, 1), lambda k: (k, 0, 0)),
                  pl.BlockSpec((FFT_PLANES, n2, 1), lambda k: (k, 0, 0)),
                  const((2 * n2, 2 * n2)), const((1, hw))],
        out_specs=pl.BlockSpec((2, FFT_PLANES, n2, hw), lambda k: (0, k, 0, 0)),
        compiler_params=_cparams(("parallel",)),
        name="fft_filter_stage_b",
    )(a, plan.twc, plan.tws, plan.fb[0], scale)


def _fft_conv_b_body(a_ref, kf_ref, twc_ref, tws_ref, fbh_ref, fih_ref, o_ref):
    n2 = FFT_N2
    planes = range(FFT_PLANES)
    zs = [_dotc(fbh_ref[...], _twiddled(a_ref, twc_ref, tws_ref, p)) for p in planes]
    ys = []
    for p, z in zip(planes, zs):
        zre, zim = z[:n2], z[n2:]
        kre, kim = kf_ref[0, p].astype(F32), kf_ref[1, p].astype(F32)
        ys.append(jnp.concatenate([zre * kre - zim * kim, zre * kim + zim * kre], axis=0))
    qs = [_dotc(fih_ref[...], y) for y in ys]
    for p, q in zip(planes, qs):
        qre, qim = q[:n2], q[n2:]
        c, s = twc_ref[p], tws_ref[p]
        o_ref[0, p] = (qre * c - qim * s).astype(o_ref.dtype)
        o_ref[1, p] = (qim * c + qre * s).astype(o_ref.dtype)


def fft_conv_stage_b(plan, a, kf, order):
    n2, k1p = plan.n2, plan.k1p
    const = lambda shp: pl.BlockSpec(shp, lambda k: (0,) * len(shp))
    return pl.pallas_call(
        _fft_conv_b_body,
        out_shape=jax.ShapeDtypeStruct((2, k1p, n2, HY_WIDTH), BF16),
        grid=(k1p // FFT_PLANES,),
        in_specs=[pl.BlockSpec((2, FFT_PLANES, n2, HY_WIDTH), lambda k: (0, k, 0, 0)),
                  pl.BlockSpec((2, FFT_PLANES, n2, HY_WIDTH), lambda k: (0, k, 0, order)),
                  pl.BlockSpec((FFT_PLANES, n2, 1), lambda k: (k, 0, 0)),
                  pl.BlockSpec((FFT_PLANES, n2, 1), lambda k: (k, 0, 0)),
                  const((2 * n2, 2 * n2)), const((2 * n2, 2 * n2))],
        out_specs=pl.BlockSpec((2, FFT_PLANES, n2, HY_WIDTH), lambda k: (0, k, 0, 0)),
        compiler_params=_cparams(("parallel",)),
        name="fft_conv_stage_b",
    )(a, kf, plan.twc, plan.tws, plan.fb[0], plan.fbi[0])


def _fft_inv_a_body(gh_ref, fh_ref, q_ref, z_ref, gate_ref, skip_ref, o_ref, *next_ref):
    y = _dotc(gh_ref[...], q_ref[...])
    z_next = gate_ref[...] * (y + z_ref[...] * skip_ref[...])
    o_ref[...] = z_next.astype(o_ref.dtype)
    if next_ref:
        next_ref[0][...] = _dotc(fh_ref[...], z_next).astype(next_ref[0].dtype)


def fft_inv_stage_a(plan, qf, zf, gatef, skip_t, lead, with_next):
    rows_out, m = zf.shape
    rows = 2 * plan.k1p
    tn = skip_t.shape[1]
    gh = jnp.pad(plan.g1[0], ((lead, 0), (0, 0)))
    fh = jnp.pad(plan.f1[0], ((0, 0), (lead, 0)))
    col = lambda r_: pl.BlockSpec((r_, tn), lambda j: (0, j))
    out_shape = [jax.ShapeDtypeStruct((rows_out, m), zf.dtype)]
    out_specs = [col(rows_out)]
    if with_next:
        out_shape.append(jax.ShapeDtypeStruct((rows, m), BF16))
        out_specs.append(col(rows))
    return pl.pallas_call(
        _fft_inv_a_body,
        out_shape=out_shape,
        grid=(m // tn,),
        in_specs=[pl.BlockSpec((rows_out, rows), lambda j: (0, 0)), pl.BlockSpec((rows, rows_out), lambda j: (0, 0)),
                  col(rows), col(rows_out), col(rows_out), pl.BlockSpec((1, tn), lambda j: (0, 0))],
        out_specs=out_specs,
        compiler_params=_cparams(("parallel",)),
        name="fft_inv_stage_a",
    )(gh, fh, qf, zf, gatef, skip_t)


def hyena_long(hv, hx1, hx2, ta, scale, skip, lead):
    length = hv.shape[0] - lead * FFT_N2
    plan = _FftPlan(length)
    n2, k1p = plan.n2, plan.k1p
    m = n2 * HY_WIDTH
    tn = min(FFT_TN, m)
    flat = lambda a: a.reshape(lead + plan.n1h, m)
    kf = fft_filter_stage_b(plan, ta.reshape(2, k1p, n2, HY_TAPS_W), scale)
    z = flat(hv)
    a = fft_stage_a(plan, z, lead)
    gates = (hx1, hx2)
    for o, gate in enumerate(gates):
        q = fft_conv_stage_b(plan, a.reshape(2, k1p, n2, HY_WIDTH), kf, o)
        last = o == len(gates) - 1
        out = fft_inv_stage_a(plan, q.reshape(2 * k1p, m), z, flat(gate), jnp.tile(skip[o], tn // HY_WIDTH)[None], lead,
                              not last)
        z = out[0]
        if not last:
            a = out[1]
    return z.reshape(-1, HY_WIDTH)


def _rec_out_body(x_ref, yf_ref, yb_ref, bonus_ref, sgr_ref, zh_ref, zc_ref, sgh_ref, gnw_ref, gnb_ref, bm_ref, wa_ref, wb_ref,
                  gate_ref, fin_ref, o_ref, *, final):
    y = yf_ref[...] + yb_ref[...]
    bm = bm_ref[...]
    mean = _dot_stat(y, bm)
    yc = y - mean
    var = _dot_stat(yc * yc, bm)
    yn = yc * lax.rsqrt(var + RWKV_GN_EPS) * gnw_ref[...] + gnb_ref[...]
    ya = ((yn + bonus_ref[...]) * sgr_ref[...]).astype(BF16)
    zh = zh_ref[...]
    if not final:
        zh = jnp.where(pl.program_id(0) == 0, zc_ref[...], zh)
    yh = (zh * sgh_ref[...]).astype(BF16)
    out = jnp.dot(ya, wa_ref[...], preferred_element_type=F32) + jnp.dot(yh, wb_ref[...], preferred_element_type=F32)
    xn = x_ref[...] + gate_ref[0] * out
    if final:
        xn = xn * lax.rsqrt(jnp.mean(xn * xn, axis=-1, keepdims=True) + NORM_EPS) * fin_ref[...]
    o_ref[...] = xn


RWKV_GN_EPS = 64e-5


def rec_out_proj(xs, yf, yb, bonus, sgr, zh, zc, sgh, gn_w, gn_b, w_out, gate2, final_g, *, final):
    r = xs.shape[0]
    d = D_MODEL
    off = 1 if final else 0
    nt = r // ROW_TILE - off
    rows = lambda w: pl.BlockSpec((ROW_TILE, w), lambda i: (i + off, 0))
    const = lambda shp: pl.BlockSpec(shp, lambda i: (0,) * len(shp))
    bw = BRANCH_W
    return pl.pallas_call(
        functools.partial(_rec_out_body, final=final),
        out_shape=jax.ShapeDtypeStruct((nt * ROW_TILE, d), F32),
        grid=(nt,),
        in_specs=[rows(d), rows(bw), rows(bw), rows(bw), rows(bw), rows(bw), const((CTX_LEN, bw)), rows(bw),
                  const((1, bw)), const((1, bw)),
                  const((bw, bw)), const((bw, d)), const((bw, d)),
                  pl.BlockSpec((1, 1, d), lambda i: (jnp.minimum(i + off, 1), 0, 0)), const((1, d))],
        out_specs=pl.BlockSpec((ROW_TILE, d), lambda i: (i, 0)),
        compiler_params=_cparams(("parallel",)),
        name="rec_out_proj",
    )(xs, yf, yb, bonus, sgr, zh, zc, sgh, gn_w[None], gn_b[None], _block_sum_mat(bw, 1.0 / HEAD_DIM).astype(BF16),
      w_out[:bw].astype(BF16), w_out[bw:].astype(BF16), gate2, final_g[None])


def rec_layer(xs, mods, norm_g, w_in, mu, w0, w_up, a0, a_up, k_k, k_a, r_k, gn_w, gn_b, hy_short, hy_w1, hy_b1, hy_w2,
              hy_b2, hy_w3, hy_b3, hy_skip, w_out, final_g, final):
    shift2, scale2, gate2 = mods
    rw, hv, hx1, hx2, sgr, sgh = rec_in_proj(xs, norm_g, scale2, shift2, w_in, mu, hy_short)
    g, add, bonus = rwkv_prep(rw, w0, w_up, a0, a_up, k_k, k_a, r_k)
    yf, yb = rwkv_scan(g, add)
    fargs = (hy_w1, hy_b1, hy_w2, hy_b2, hy_w3, hy_b3)
    n = xs.shape[0] - CTX_LEN
    taps, scale = hyena_taps(n, *fargs, flat=True)
    zh = hyena_long(hv, hx1, hx2, taps, scale, hy_skip, CTX_LEN // FFT_N2)
    if final:
        z_ctx = hv[:CTX_LEN]
    else:
        taps_c, scale_c = hyena_taps(CTX_LEN, *fargs, flat=False)
        z_ctx = hyena_short(hv[:CTX_LEN], hx1[:CTX_LEN], hx2[:CTX_LEN], taps_c, scale_c, hy_skip)
    return rec_out_proj(xs, yf, yb, bonus, sgr, zh, z_ctx, sgh, gn_w, gn_b, w_out, gate2, final_g, final=final)


def kernel(x, c, ctx, c_ctx, attn_norm, attn_ada_w, attn_ada_b, attn_w_in, na_rpb, gqa_q_gain, gqa_k_gain, attn_w_out,
           rec_norm, rec_ada_w, rec_ada_b, rec_w_in, rwkv_mu, rwkv_w0, rwkv_w_up, rwkv_a0, rwkv_a_up, rwkv_k_k, rwkv_k_a,
           rwkv_r_k, rwkv_gn_w, rwkv_gn_b, hy_short, hy_w1, hy_b1, hy_w2, hy_b2, hy_w3, hy_b3, hy_skip, rec_w_out,
           final_norm):
    assert x.shape[0] == 1 and ctx.shape[1] == CTX_LEN and x.shape[2] == D_MODEL
    n = x.shape[1]
    assert n % ROW_TILE == 0 and n // ROW_TILE >= 3
    assert attn_w_in.shape[0] == rec_w_in.shape[0]
    d = D_MODEL
    cond8 = jnp.zeros((8, d), F32).at[0].set(c_ctx).at[1].set(c[0])
    m_attn = adaln_all(cond8, attn_ada_w, attn_ada_b)
    m_rec = adaln_all(cond8, rec_ada_w, rec_ada_b)
    mods = lambda m, i: tuple(m[i, :2, j * d:(j + 1) * d].reshape(2, 1, d) for j in range(3))
    cos_t, sin_t = _rope_tables(n)
    xs = jnp.concatenate([ctx[0], x[0]], axis=0)
    depth = attn_w_in.shape[0] + rec_w_in.shape[0]
    for layer in range(depth):
        i = layer // 2
        final = layer == depth - 1
        if layer % 2 == 0:
            xs = attn_layer(xs, mods(m_attn, i), attn_norm[i], attn_w_in[i], na_rpb[i], gqa_q_gain[i], gqa_k_gain[i],
                            attn_w_out[i], cos_t, sin_t)
        else:
            xs = rec_layer(xs, mods(m_rec, i), rec_norm[i], rec_w_in[i], rwkv_mu[i], rwkv_w0[i], rwkv_w_up[i],
                           rwkv_a0[i], rwkv_a_up[i], rwkv_k_k[i], rwkv_k_a[i], rwkv_r_k[i], rwkv_gn_w[i], rwkv_gn_b[i],
                           hy_short[i], hy_w1[i], hy_b1[i], hy_w2[i], hy_b2[i], hy_w3[i], hy_b3[i], hy_skip[i],
                           rec_w_out[i], final_norm, final)
    return xs[None]


def _hy_short_body(fh_ref, fl_ref, gh_ref, gl_ref, v_ref, x1_ref, x2_ref, taps_ref, scale_ref, skip_ref, o_ref):
    fh, fl, gh, gl = fh_ref[...], fl_ref[...], gh_ref[...], gl_ref[...]
    kp = fh.shape[0] // 2
    hw = HY_TAPS_W // 2
    tf = _dot3c(fh, fl, taps_ref[...])
    kre = (tf[:kp, :hw] + tf[:kp, hw:]) * scale_ref[...]
    kim = (tf[kp:, :hw] - tf[kp:, hw:]) * scale_ref[...]
    z = v_ref[...].astype(F32)
    for o, gate_ref in enumerate((x1_ref, x2_ref)):
        ls = slice(o * HY_WIDTH, (o + 1) * HY_WIDTH)
        zf = _dot3c(fh, fl, z)
        zre, zim = zf[:kp], zf[kp:]
        y = jnp.concatenate([zre * kre[:, ls] - zim * kim[:, ls], zre * kim[:, ls] + zim * kre[:, ls]], axis=0)
        z = gate_ref[...] * (_dot3c(gh, gl, y) + z * skip_ref[o:o + 1])
    o_ref[...] = z


def hyena_short(hv, hx1, hx2, taps, scale, skip):
    length = hv.shape[0]
    n = 2 * length
    k1 = length + 1
    kp = -(-k1 // 8) * 8
    kk = np.arange(kp)[:, None].astype(np.float64)
    live = kk < k1
    th = 2.0 * np.pi * kk * np.arange(length)[None, :] / n
    f = _split_const(np.concatenate([np.cos(th) * live, -np.sin(th) * live], axis=0))
    ck = np.where((kk == 0) | (kk == length), 1.0, 2.0) * live / n
    g = _split_const(np.concatenate([np.cos(th) * ck, -np.sin(th) * ck], axis=0).T)
    return pl.pallas_call(
        _hy_short_body,
        out_shape=jax.ShapeDtypeStruct((length, HY_WIDTH), F32),
        compiler_params=pltpu.CompilerParams(vmem_limit_bytes=VMEM_LIMIT),
        name="hyena_short",
    )(f[0], f[1], g[0], g[1], hv, hx1, hx2, taps, scale, skip)
```
